```python
import jax, jax.numpy as jnp
from jax import lax
import numpy as np

D_MODEL = 1024
BATCH = 32
SEQ = 256
DEPTH = 4
DEC_BATCH = 8
DEC_SEQ = 4096
PAST_LEN = 256

GRID_W = 64
CONV_WIDTH = 256
NA_HEADS = 4
NA_HEAD_DIM = 64
NA_WIDTH = NA_HEADS * NA_HEAD_DIM
NA_KH = 8
NA_KW = 16
MLA_HEADS = 4
MLA_Q_LORA = 256
MLA_KV_LORA = 128
MLA_NOPE_DIM = 64
MLA_ROPE_DIM = 32
MLA_V_DIM = 64
MLA_WIDTH = MLA_HEADS * MLA_V_DIM
FN_GROUPS = 4
FN_GROUP_DIM = 64
FN_WIDTH = FN_GROUPS * FN_GROUP_DIM
MIX_WIDTH = CONV_WIDTH + NA_WIDTH + MLA_WIDTH + FN_WIDTH
PROJ_SIZES = (CONV_WIDTH, CONV_WIDTH, CONV_WIDTH, NA_WIDTH, NA_WIDTH, NA_WIDTH, MLA_Q_LORA, MLA_KV_LORA, MLA_ROPE_DIM, FN_WIDTH)
PROJ_WIDTH = sum(PROJ_SIZES)
N_EXPERTS = 16
N_EXPERT_GROUPS = 4
EXPERTS_PER_GROUP = N_EXPERTS // N_EXPERT_GROUPS
TOP_K = 2
EXPERT_FF = 256
ROPE_THETA = 10000.0
Q_BLOCK = 128
EPS = 1e-6
NEG_INF = -1e30

kernel_name = 'hybrid_flow_conv_na_mla_fnet_moe_step'


def rmsnorm(x, g):
    xf = x.astype(jnp.float32)
    y = xf * lax.rsqrt(jnp.mean(xf * xf, axis=-1, keepdims=True) + EPS)
    return (y * g.astype(jnp.float32)).astype(x.dtype)


def ada_modulation(cond, w_ada, b_ada):
    m = jax.nn.silu(cond) @ w_ada + b_ada
    return [t[:, None, :] for t in jnp.split(m, 6, axis=-1)]


def modulate(x, g, shift, scale):
    return rmsnorm(x, g) * (1 + scale) + shift


def split_projection(p):
    idx, acc = [], 0
    for s in PROJ_SIZES[:-1]:
        acc += s
        idx.append(acc)
    return jnp.split(p, idx, axis=-1)


def axial_rope_tables(n):
    t = jnp.arange(n)
    pos = jnp.stack([t // GRID_W, t % GRID_W], axis=-1).astype(jnp.float32)
    nf = MLA_ROPE_DIM // 4
    freqs = ROPE_THETA ** (-jnp.arange(nf, dtype=jnp.float32) / nf)
    ang = pos[:, :, None] * freqs
    return jnp.cos(ang), jnp.sin(ang)


def apply_axial_rope(x, cos, sin):
    shp = x.shape
    xr = x.astype(jnp.float32).reshape(shp[:-1] + (2, 2, MLA_ROPE_DIM // 4))
    x1, x2 = xr[..., 0, :], xr[..., 1, :]
    out = jnp.stack([x1 * cos - x2 * sin, x2 * cos + x1 * sin], axis=-2)
    return out.reshape(shp).astype(x.dtype)


def short_conv_mixer(b_gate, c_gate, u, conv_w):
    z = c_gate * u
    n = z.shape[1]
    zp = jnp.pad(z, ((0, 0), (1, 1), (0, 0)))
    conv = zp[:, :n] * conv_w[0] + zp[:, 1:n + 1] * conv_w[1] + zp[:, 2:] * conv_w[2]
    return b_gate * conv


def sdpa(q, k, v):
    B, n, H, d = q.shape
    blk = min(Q_BLOCK, n)
    nb = n // blk
    scale = d ** -0.5

    def one_block(qb):
        s = jnp.einsum('bqhd,bkhd->bhqk', qb, k).astype(jnp.float32) * scale
        p = jax.nn.softmax(s, axis=-1).astype(v.dtype)
        return jnp.einsum('bhqk,bkhd->bqhd', p, v)

    qb = jnp.moveaxis(q.reshape(B, nb, blk, H, d), 1, 0)
    o = lax.map(one_block, qb)
    return jnp.moveaxis(o, 0, 1).reshape(B, n, H * v.shape[-1])


def neighborhood_attention(q, k, v, k_ctx, v_ctx, rpb):
    B, N, H, dh = q.shape
    rows = N // GRID_W
    kh = min(NA_KH, rows)
    scale = dh ** -0.5
    qg, kg, vg = (t.reshape(B, rows, GRID_W, H, dh) for t in (q, k, v))
    r = jnp.arange(rows)
    row_idx = jnp.clip(r - kh // 2, 0, rows - kh)[:, None] + jnp.arange(kh)
    kb, vb = kg[:, row_idx], vg[:, row_idx]
    col = jnp.arange(GRID_W)
    col_start = jnp.clip(col - NA_KW // 2, 0, GRID_W - NA_KW)
    in_window = (col[None, :] >= col_start[:, None]) & (col[None, :] < col_start[:, None] + NA_KW)
    d_row = (row_idx - r[:, None] + NA_KH - 1)[:, None, :, None]
    d_col = jnp.clip(col[None, :] - col[:, None] + NA_KW - 1, 0, 2 * NA_KW - 2)[None, :, None, :]
    bias = rpb[:, d_row, d_col].astype(jnp.float32)
    s_win = jnp.einsum('brqhd,brkchd->bhrqkc', qg, kb).astype(jnp.float32) * scale + bias
    s_win = jnp.where(in_window[:, None, :], s_win, NEG_INF)
    s_ctx = jnp.einsum('brqhd,bhpd->bhrqp', qg, k_ctx).astype(jnp.float32) * scale
    n_win = kh * GRID_W
    s = jnp.concatenate([s_win.reshape(B, H, rows, GRID_W, n_win), s_ctx], axis=-1)
    p = jax.nn.softmax(s, axis=-1).astype(v.dtype)
    p_win = p[..., :n_win].reshape(B, H, rows, GRID_W, kh, GRID_W)
    o = jnp.einsum('bhrqkc,brkchd->brqhd', p_win, vb) + jnp.einsum('bhrqp,bhpd->brqhd', p[..., n_win:], v_ctx)
    return o.reshape(B, N, H * dh)


def fourier_mixer(u, w_fn):
    B, n, _ = u.shape
    ug = u.astype(jnp.float32).reshape(B, n, FN_GROUPS, FN_GROUP_DIM)
    f = jnp.fft.fft2(ug, axes=(1, 3), norm='ortho').real
    return f.reshape(B, n, FN_WIDTH).astype(u.dtype) @ w_fn


def mla_queries(cq, g_q, w_q_up):
    B, n, _ = cq.shape
    q = (rmsnorm(cq, g_q) @ w_q_up).reshape(B, n, MLA_HEADS, MLA_NOPE_DIM + MLA_ROPE_DIM)
    return q[..., :MLA_NOPE_DIM], q[..., MLA_NOPE_DIM:]


def mla_keys_values(ckv_n, k_rope, w_kv_up):
    B, n, _ = ckv_n.shape
    kv = (ckv_n @ w_kv_up).reshape(B, n, MLA_HEADS, MLA_NOPE_DIM + MLA_V_DIM)
    k_r = jnp.broadcast_to(k_rope[:, :, None, :], (B, n, MLA_HEADS, MLA_ROPE_DIM))
    return jnp.concatenate([kv[..., :MLA_NOPE_DIM], k_r], axis=-1), kv[..., MLA_NOPE_DIM:]


def moe_ffn(h, w_router, b_router, w1, w3, w2):
    s = jax.nn.sigmoid((h @ w_router).astype(jnp.float32))
    sb = s + b_router.astype(jnp.float32)
    grp = sb.reshape(sb.shape[:-1] + (N_EXPERT_GROUPS, EXPERTS_PER_GROUP))
    gsel = jnp.argmax(lax.top_k(grp, TOP_K)[0].sum(-1), axis=-1)
    in_group = (jnp.arange(N_EXPERTS) // EXPERTS_PER_GROUP) == gsel[..., None]
    _, idx = lax.top_k(jnp.where(in_group, sb, NEG_INF), TOP_K)
    wsel = jnp.take_along_axis(s, idx, axis=-1)
    wsel = wsel / jnp.sum(wsel, axis=-1, keepdims=True)
    gates = jnp.sum(jax.nn.one_hot(idx, N_EXPERTS, dtype=jnp.float32) * wsel[..., None], axis=-2).astype(h.dtype)
    hid = jax.nn.silu(jnp.einsum('bnd,edf->bnef', h, w1)) * jnp.einsum('bnd,edf->bnef', h, w3)
    return jnp.einsum('bnef,efd->bnd', hid * gates[..., None], w2).astype(h.dtype)


def mixing_context(h, lw):
    B, n, _ = h.shape
    a_b, a_c, a_u, q, k, v, cq, ckv, k_rope, f_u = split_projection(h @ lw['w_in'])
    y_conv = short_conv_mixer(a_b, a_c, a_u, lw['conv_w'])
    q, k, v = (t.reshape(B, n, NA_HEADS, NA_HEAD_DIM) for t in (q, k, v))
    y_na = sdpa(q, k, v)
    q_nope, q_rope = mla_queries(cq, lw['mla_gq'], lw['mla_wq_up'])
    ckv_n = rmsnorm(ckv, lw['mla_gkv'])
    k_m, v_m = mla_keys_values(ckv_n, k_rope, lw['mla_wkv_up'])
    y_mla = sdpa(jnp.concatenate([q_nope, q_rope], axis=-1), k_m, v_m)
    y_fn = fourier_mixer(f_u, lw['w_fn'])
    out = jnp.concatenate([y_conv, y_na, y_mla, y_fn], axis=-1) @ lw['w_out']
    return out, (jnp.swapaxes(k, 1, 2), jnp.swapaxes(v, 1, 2), ckv_n, k_rope)


def mixing_latent(h, lw, na_k_ctx, na_v_ctx, ckv_ctx, krope_ctx, cos, sin):
    B, n, _ = h.shape
    a_b, a_c, a_u, q, k, v, cq, ckv, k_rope, f_u = split_projection(h @ lw['w_in'])
    y_conv = short_conv_mixer(a_b, a_c, a_u, lw['conv_w'])
    q, k, v = (t.reshape(B, n, NA_HEADS, NA_HEAD_DIM) for t in (q, k, v))
    y_na = neighborhood_attention(q, k, v, na_k_ctx, na_v_ctx, lw['na_rpb'])
    q_nope, q_rope = mla_queries(cq, lw['mla_gq'], lw['mla_wq_up'])
    q_rope = apply_axial_rope(q_rope, cos[:, None], sin[:, None])
    k_rope = apply_axial_rope(k_rope, cos, sin)
    k_lat, v_lat = mla_keys_values(rmsnorm(ckv, lw['mla_gkv']), k_rope, lw['mla_wkv_up'])
    k_ctx, v_ctx = mla_keys_values(ckv_ctx, krope_ctx, lw['mla_wkv_up'])
    y_mla = sdpa(jnp.concatenate([q_nope, q_rope], axis=-1),
                 jnp.concatenate([k_lat, k_ctx], axis=1), jnp.concatenate([v_lat, v_ctx], axis=1))
    y_fn = fourier_mixer(f_u, lw['w_fn'])
    return jnp.concatenate([y_conv, y_na, y_mla, y_fn], axis=-1) @ lw['w_out']


def setup_inputs(seed: int = 0) -> dict:
    key = jax.random.key(seed)
    ks = jax.random.split(key, 27)
    nrm = lambda k, shape, s: jax.random.normal(k, shape, jnp.float32) * s
    gain = lambda k, shape: 1.0 + 0.02 * jax.random.normal(k, shape, jnp.float32)
    D = D_MODEL
    return {
        'x_prompt': nrm(ks[0], (BATCH, SEQ, D), 1.0),
        'x_sample': nrm(ks[1], (DEC_BATCH, DEC_SEQ, D), 1.0),
        'cache_na_k': nrm(ks[2], (DEC_BATCH, DEPTH, NA_HEADS, PAST_LEN, NA_HEAD_DIM), 1.0),
        'cache_na_v': nrm(ks[3], (DEC_BATCH, DEPTH, NA_HEADS, PAST_LEN, NA_HEAD_DIM), 1.0),
        'cache_mla_ckv': nrm(ks[4], (DEC_BATCH, DEPTH, PAST_LEN, MLA_KV_LORA), 1.0),
        'cache_mla_krope': nrm(ks[5], (DEC_BATCH, DEPTH, PAST_LEN, MLA_ROPE_DIM), 1.0),
        'c': nrm(ks[6], (DEC_BATCH, D), 1.0),
        'c_ctx': nrm(ks[7], (D,), 1.0),
        'w_ada': nrm(ks[8], (DEPTH, D, 6 * D), 0.5 * D ** -0.5),
        'b_ada': nrm(ks[9], (DEPTH, 6 * D), 0.02),
        'norm1': gain(ks[10], (DEPTH, D)),
        'norm2': gain(ks[11], (DEPTH, D)),
        'w_in': nrm(ks[12], (DEPTH, D, PROJ_WIDTH), D ** -0.5),
        'conv_w': nrm(ks[13], (DEPTH, 3, CONV_WIDTH), 3 ** -0.5),
        'na_rpb': nrm(ks[14], (DEPTH, NA_HEADS, 2 * NA_KH - 1, 2 * NA_KW - 1), 0.5),
        'mla_gq': gain(ks[15], (DEPTH, MLA_Q_LORA)),
        'mla_wq_up': nrm(ks[16], (DEPTH, MLA_Q_LORA, MLA_HEADS * (MLA_NOPE_DIM + MLA_ROPE_DIM)), MLA_Q_LORA ** -0.5),
        'mla_gkv': gain(ks[17], (DEPTH, MLA_KV_LORA)),
        'mla_wkv_up': nrm(ks[18], (DEPTH, MLA_KV_LORA, MLA_HEADS * (MLA_NOPE_DIM + MLA_V_DIM)), MLA_KV_LORA ** -0.5),
        'w_fn': nrm(ks[19], (DEPTH, FN_WIDTH, FN_WIDTH), FN_WIDTH ** -0.5),
        'w_out': nrm(ks[20], (DEPTH, MIX_WIDTH, D), MIX_WIDTH ** -0.5),
        'w_router': nrm(ks[21], (D, N_EXPERTS), D ** -0.5),
        'b_router': nrm(ks[22], (N_EXPERTS,), 0.01),
        'w1': nrm(ks[23], (DEPTH, N_EXPERTS, D, EXPERT_FF), D ** -0.5),
        'w3': nrm(ks[24], (DEPTH, N_EXPERTS, D, EXPERT_FF), D ** -0.5),
        'w2': nrm(ks[25], (DEPTH, N_EXPERTS, EXPERT_FF, D), EXPERT_FF ** -0.5),
        'norm_f': gain(ks[26], (D,)),
    }


def reference(x_prompt, x_sample, cache_na_k, cache_na_v, cache_mla_ckv, cache_mla_krope, c, c_ctx,
              w_ada, b_ada, norm1, norm2, w_in, conv_w, na_rpb, mla_gq, mla_wq_up, mla_gkv, mla_wkv_up,
              w_fn, w_out, w_router, b_router, w1, w3, w2, norm_f):
    def layer_weights(l):
        return {'w_in': w_in[l], 'conv_w': conv_w[l], 'na_rpb': na_rpb[l], 'mla_gq': mla_gq[l],
                'mla_wq_up': mla_wq_up[l], 'mla_gkv': mla_gkv[l], 'mla_wkv_up': mla_wkv_up[l],
                'w_fn': w_fn[l], 'w_out': w_out[l]}

    xp = x_prompt
    st_k, st_v, st_ckv, st_kr = [], [], [], []
    for l in range(DEPTH):
        lw = layer_weights(l)
        sh1, sc1, g1, sh2, sc2, g2 = ada_modulation(c_ctx[None, :], w_ada[l], b_ada[l])
        mix, (nk, nv, nckv, nkr) = mixing_context(modulate(xp, norm1[l], sh1, sc1), lw)
        xp = xp + g1 * mix
        xp = xp + g2 * moe_ffn(modulate(xp, norm2[l], sh2, sc2), w_router, b_router, w1[l], w3[l], w2[l])
        st_k.append(nk)
        st_v.append(nv)
        st_ckv.append(nckv)
        st_kr.append(nkr)
    y_prompt = rmsnorm(xp, norm_f)
    new_na_k = jnp.stack(st_k, axis=1)
    new_na_v = jnp.stack(st_v, axis=1)
    new_mla_ckv = jnp.stack(st_ckv, axis=1)
    new_mla_krope = jnp.stack(st_kr, axis=1)

    cos, sin = axial_rope_tables(x_sample.shape[1])
    xs = x_sample
    for l in range(DEPTH):
        lw = layer_weights(l)
        sh1, sc1, g1, sh2, sc2, g2 = ada_modulation(c, w_ada[l], b_ada[l])
        mix = mixing_latent(modulate(xs, norm1[l], sh1, sc1), lw, cache_na_k[:, l], cache_na_v[:, l],
                            cache_mla_ckv[:, l], cache_mla_krope[:, l], cos, sin)
        xs = xs + g1 * mix
        xs = xs + g2 * moe_ffn(modulate(xs, norm2[l], sh2, sc2), w_router, b_router, w1[l], w3[l], w2[l])
    y_sample = rmsnorm(xs, norm_f)
    return (y_prompt, y_sample, new_na_k, new_na_v, new_mla_ckv, new_mla_krope)
```

```python
import functools
import math

import numpy as np
import jax
import jax.numpy as jnp
from jax import lax
from jax.experimental import pallas as pl
from jax.experimental.pallas import tpu as pltpu

F32 = jnp.float32
BF16 = jnp.bfloat16

GRID_W = 64
CONV_WIDTH = 256
NA_HEADS = 4
NA_HEAD_DIM = 64
NA_WIDTH = NA_HEADS * NA_HEAD_DIM
NA_KH = 8
NA_KW = 16
MLA_HEADS = 4
MLA_Q_LORA = 256
MLA_KV_LORA = 128
MLA_NOPE_DIM = 64
MLA_ROPE_DIM = 32
MLA_V_DIM = 64
MLA_QK_PAD = 128
FN_GROUPS = 4
FN_GROUP_DIM = 64
FN_WIDTH = FN_GROUPS * FN_GROUP_DIM
N_EXPERTS = 16
N_EXPERT_GROUPS = 4
EXPERTS_PER_GROUP = N_EXPERTS // N_EXPERT_GROUPS
EXPERT_FF = 256
ROPE_THETA = 10000.0
EPS = 1e-6
NEG_INF = -1e30
LANES = 128

NA_SCALE = NA_HEAD_DIM ** -0.5
MLA_SCALE = (MLA_NOPE_DIM + MLA_ROPE_DIM) ** -0.5

NA_Q_ROWS = 4
NA_WIN_ROWS = 12

VMEM_LIMIT = 56 * 1024 * 1024

_C_AB, _C_AC, _C_AU, _C_Q, _C_K, _C_V, _C_CQ = 0, 256, 512, 768, 1024, 1280, 1536
_C_CKV, _C_KR, _C_KRS, _C_FU, _C_END = 1792, 1920, 2048, 2176, 2432


def _nt_dot(a, b):
    return lax.dot_general(a, b, (((1,), (1,)), ((), ())), preferred_element_type=F32)


def _dot(a, b):
    return jnp.dot(a, b, preferred_element_type=F32)


def _rms(x, g):
    return x * lax.rsqrt(jnp.mean(x * x, axis=-1, keepdims=True) + EPS) * g


def _params(*sem):
    return pltpu.CompilerParams(dimension_semantics=sem, vmem_limit_bytes=VMEM_LIMIT)


def _ada_kernel(c_ref, w_ref, b_ref, o_ref):
    cnd = c_ref[...]
    act = cnd * jax.nn.sigmoid(cnd)
    o_ref[...] = _dot(act.astype(BF16), w_ref[...].astype(BF16)) + b_ref[...]


def _ada_modulation(cond, w_ada, b_ada):
    depth, d, six_d = w_ada.shape
    r = cond.shape[0]
    tn = 1024
    out = pl.pallas_call(
        _ada_kernel,
        grid=(depth, six_d // tn),
        in_specs=[
            pl.BlockSpec((r, d), lambda l, j: (0, 0)),
            pl.BlockSpec((None, d, tn), lambda l, j: (l, 0, j)),
            pl.BlockSpec((None, 1, tn), lambda l, j: (l, 0, j)),
        ],
        out_specs=pl.BlockSpec((None, r, tn), lambda l, j: (l, 0, j)),
        out_shape=jax.ShapeDtypeStruct((depth, r, six_d), F32),
        compiler_params=_params("parallel", "parallel"),
        name="ada_modulation",
    )(cond, w_ada, b_ada.reshape(depth, 1, six_d))
    return out.reshape(depth, r, 6, d)


def _premix_kernel(use_rope, emit_cache, *refs):
    (x_ref, mod_ref, g1_ref, w_ref, gq_ref, wqa_ref, wqb_ref, gkv_ref, wka_ref, wv_ref, cs_ref,
     cos_ref, sin_ref) = refs[:13]
    outs = refs[13:]
    (ab_ref, z_ref, qn_ref, kn_ref, vn_ref, qm_ref, km_ref, vm_ref, fab_ref) = outs[:9]

    x = x_ref[...]
    mod = mod_ref[...]
    h = _rms(x, g1_ref[...]) * (1.0 + mod[1:2]) + mod[0:1]
    p = _dot(h.astype(BF16), w_ref[...])

    ab_ref[...] = p[:, _C_AB:_C_AC].astype(BF16)
    z_ref[...] = (p[:, _C_AC:_C_AU] * p[:, _C_AU:_C_Q]).astype(BF16)
    k_na = p[:, _C_K:_C_V]
    v_na = p[:, _C_V:_C_CQ]
    qn_ref[...] = (p[:, _C_Q:_C_K] * NA_SCALE).astype(BF16)
    kn_ref[...] = k_na.astype(BF16)
    vn_ref[...] = v_na.astype(BF16)

    cqn = _rms(p[:, _C_CQ:_C_CKV], gq_ref[...]).astype(BF16)
    ckvn = _rms(p[:, _C_CKV:_C_KR], gkv_ref[...])
    ckvn_b = ckvn.astype(BF16)
    qa = _dot(cqn, wqa_ref[...])
    kva = _dot(ckvn_b, wka_ref[...])
    vm_ref[...] = _dot(ckvn_b, wv_ref[...]).astype(BF16)
    kr = p[:, _C_KR:_C_KRS]
    if use_rope:
        cos = cos_ref[...]
        sin = sin_ref[...]
        qb = _dot(cqn, wqb_ref[...])
        krot = kr * cos + p[:, _C_KRS:_C_FU] * sin
    else:
        krot = kr
    for hd in range(MLA_HEADS):
        sl = slice(hd * MLA_QK_PAD, (hd + 1) * MLA_QK_PAD)
        qh = qa[:, sl]
        if use_rope:
            qh = qh * cos + qb[:, sl] * sin
        qm_ref[:, sl] = (qh * MLA_SCALE).astype(BF16)
        km_ref[:, sl] = (kva[:, sl] + krot).astype(BF16)

    fab_ref[...] = _dot(p[:, _C_FU:_C_END].astype(BF16), cs_ref[...]).astype(BF16)

    if emit_cache:
        ck_ref, cv_ref, cckv_ref, ckr_ref = outs[9:]
        for hd in range(NA_HEADS):
            sl = slice(hd * NA_HEAD_DIM, (hd + 1) * NA_HEAD_DIM)
            ck_ref[hd] = k_na[:, sl]
            cv_ref[hd] = v_na[:, sl]
        cckv_ref[...] = ckvn
        ckr_ref[...] = kr[:, MLA_NOPE_DIM:MLA_NOPE_DIM + MLA_ROPE_DIM]


def _premix(x, mod, layer, cond_row_of_tile, n, wts, rope, emit_cache, tm):
    t, d = x.shape
    use_rope = rope is not None
    if use_rope:
        cos_t, sin_t = rope
    else:
        cos_t = sin_t = jnp.zeros((8, LANES), F32)
    tiles_per_seq = n // tm
    const = lambda *_: (0, 0)
    lsel = lambda *_: (layer, 0, 0)
    rope_spec = (pl.BlockSpec((tm, LANES), lambda i: (i % tiles_per_seq, 0)) if use_rope
                 else pl.BlockSpec((8, LANES), const))
    in_specs = [
        pl.BlockSpec((tm, d), lambda i: (i, 0)),
        pl.BlockSpec((None, None, 6, d), lambda i: (layer, cond_row_of_tile(i), 0, 0)),
        pl.BlockSpec((None, 1, d), lsel),
        pl.BlockSpec((None, d, _C_END), lsel),
        pl.BlockSpec((None, 1, MLA_Q_LORA), lsel),
        pl.BlockSpec((None, MLA_Q_LORA, MLA_HEADS * MLA_QK_PAD), lsel),
        pl.BlockSpec((None, MLA_Q_LORA, MLA_HEADS * MLA_QK_PAD), lsel),
        pl.BlockSpec((None, 1, MLA_KV_LORA), lsel),
        pl.BlockSpec((None, MLA_KV_LORA, MLA_HEADS * MLA_QK_PAD), lsel),
        pl.BlockSpec((None, MLA_KV_LORA, MLA_HEADS * MLA_V_DIM), lsel),
        pl.BlockSpec((FN_WIDTH, 2 * FN_WIDTH), const),
        rope_spec,
        rope_spec,
    ]
    row = lambda w: pl.BlockSpec((tm, w), lambda i: (i, 0))
    widths = [CONV_WIDTH, CONV_WIDTH, NA_WIDTH, NA_WIDTH, NA_WIDTH, MLA_HEADS * MLA_QK_PAD,
              MLA_HEADS * MLA_QK_PAD, MLA_HEADS * MLA_V_DIM, 2 * FN_WIDTH]
    out_specs = [row(w) for w in widths]
    out_shape = [jax.ShapeDtypeStruct((t, w), BF16) for w in widths]
    if emit_cache:
        assert tm == n
        b = t // n
        out_specs += [
            pl.BlockSpec((None, NA_HEADS, n, NA_HEAD_DIM), lambda i: (i, 0, 0, 0)),
            pl.BlockSpec((None, NA_HEADS, n, NA_HEAD_DIM), lambda i: (i, 0, 0, 0)),
            pl.BlockSpec((None, n, MLA_KV_LORA), lambda i: (i, 0, 0)),
            pl.BlockSpec((None, n, MLA_ROPE_DIM), lambda i: (i, 0, 0)),
        ]
        out_shape += [
            jax.ShapeDtypeStruct((b, NA_HEADS, n, NA_HEAD_DIM), F32),
            jax.ShapeDtypeStruct((b, NA_HEADS, n, NA_HEAD_DIM), F32),
            jax.ShapeDtypeStruct((b, n, MLA_KV_LORA), F32),
            jax.ShapeDtypeStruct((b, n, MLA_ROPE_DIM), F32),
        ]
    return pl.pallas_call(
        functools.partial(_premix_kernel, use_rope, emit_cache),
        grid=(t // tm,),
        in_specs=in_specs,
        out_specs=out_specs,
        out_shape=out_shape,
        compiler_params=_params("parallel"),
        name="premix_lat" if use_rope else "premix_ctx",
    )(x, mod, wts["norm1"], wts["w_in"], wts["mla_gq"], wts["wq_a"], wts["wq_b"], wts["mla_gkv"],
      wts["wk_a"], wts["wv"], wts["cs_bd"], cos_t, sin_t)


def _softmax_attend(q, k, v):
    s = _nt_dot(q, k)
    m = jnp.max(s, axis=-1, keepdims=True)
    p = jnp.exp(s - m)
    l = jnp.sum(p, axis=-1, keepdims=True)
    return _dot(p.astype(BF16), v) / l


def _ctx_attn_kernel(qn_ref, kn_ref, vn_ref, qm_ref, km_ref, vm_ref, yna_ref, ymla_ref):
    for hd in range(NA_HEADS):
        sl = slice(hd * NA_HEAD_DIM, (hd + 1) * NA_HEAD_DIM)
        yna_ref[:, sl] = _softmax_attend(qn_ref[:, sl], kn_ref[:, sl], vn_ref[:, sl]).astype(BF16)
    for hd in range(MLA_HEADS):
        sq = slice(hd * MLA_QK_PAD, (hd + 1) * MLA_QK_PAD)
        sv = slice(hd * MLA_V_DIM, (hd + 1) * MLA_V_DIM)
        ymla_ref[:, sv] = _softmax_attend(qm_ref[:, sq], km_ref[:, sq], vm_ref[:, sv]).astype(BF16)


def _ctx_attention(qn, kn, vn, qm, km, vm, n):
    t = qn.shape[0]
    spec = lambda w: pl.BlockSpec((n, w), lambda b: (b, 0))
    ins = [qn, kn, vn, qm, km, vm]
    return pl.pallas_call(
        _ctx_attn_kernel,
        grid=(t // n,),
        in_specs=[spec(a.shape[1]) for a in ins],
        out_specs=[spec(NA_WIDTH), spec(MLA_HEADS * MLA_V_DIM)],
        out_shape=[jax.ShapeDtypeStruct((t, NA_WIDTH), BF16),
                   jax.ShapeDtypeStruct((t, MLA_HEADS * MLA_V_DIM), BF16)],
        compiler_params=_params("parallel"),
        name="ctx_attention",
    )(*ins)


def _mla_lat_kernel(n, kc, q_ref, k_ref, v_ref, kx_ref, vx_ref, o_ref):
    tq = q_ref.shape[0]
    past = kx_ref.shape[0]

    def update(carry, q, k, v):
        m_i, l_i, acc = carry
        s = _nt_dot(q, k)
        m_new = jnp.maximum(m_i, jnp.max(s, axis=-1, keepdims=True))
        alpha = jnp.exp(m_i - m_new)
        p = jnp.exp(s - m_new)
        l_new = alpha * l_i + jnp.sum(p, axis=-1, keepdims=True)
        acc_new = alpha * acc + _dot(p.astype(BF16), v)
        return m_new, l_new, acc_new

    for hd in range(MLA_HEADS):
        sq = slice(hd * MLA_QK_PAD, (hd + 1) * MLA_QK_PAD)
        sv = slice(hd * MLA_V_DIM, (hd + 1) * MLA_V_DIM)
        q = q_ref[:, sq]

        def body(c, carry):
            start = pl.multiple_of(c * kc, kc)
            return update(carry, q, k_ref[pl.ds(start, kc), sq], v_ref[pl.ds(start, kc), sv])

        init = (jnp.full((tq, 1), NEG_INF, F32), jnp.zeros((tq, 1), F32), jnp.zeros((tq, MLA_V_DIM), F32))
        carry = lax.fori_loop(0, n // kc, body, init)
        for c in range(past // kc if past >= kc else 1):
            w = min(kc, past)
            carry = update(carry, q, kx_ref[c * w:(c + 1) * w, sq], vx_ref[c * w:(c + 1) * w, sv])
        _, l_i, acc = carry
        o_ref[:, sv] = (acc / l_i).astype(BF16)


def _mla_lat_attention(qm, km, vm, kx, vx, layer, n, tq=256, kc=512):
    t = qm.shape[0]
    past = kx.shape[2]
    kc = min(kc, n)
    qpb = n // tq
    return pl.pallas_call(
        functools.partial(_mla_lat_kernel, n, kc),
        grid=(t // n, qpb),
        in_specs=[
            pl.BlockSpec((tq, qm.shape[1]), lambda b, i: (b * qpb + i, 0)),
            pl.BlockSpec((n, km.shape[1]), lambda b, i: (b, 0)),
            pl.BlockSpec((n, vm.shape[1]), lambda b, i: (b, 0)),
            pl.BlockSpec((None, None, past, kx.shape[3]), lambda b, i: (layer, b, 0, 0)),
            pl.BlockSpec((None, None, past, vx.shape[3]), lambda b, i: (layer, b, 0, 0)),
        ],
        out_specs=pl.BlockSpec((tq, MLA_HEADS * MLA_V_DIM), lambda b, i: (b * qpb + i, 0)),
        out_shape=jax.ShapeDtypeStruct((t, MLA_HEADS * MLA_V_DIM), BF16),
        compiler_params=_params("parallel", "parallel"),
        name="mla_lat_attention",
    )(qm, km, vm, kx, vx)


def _ctx_kv_kernel(ckv_ref, kr_ref, wka_ref, wv_ref, place_ref, k_ref, v_ref):
    ckv = ckv_ref[...].astype(BF16)
    k_ref[...] = (_dot(ckv, wka_ref[...]) + _dot(kr_ref[...].astype(BF16), place_ref[...])).astype(BF16)
    v_ref[...] = _dot(ckv, wv_ref[...]).astype(BF16)


def _ctx_kv(cache_ckv, cache_krope, wk_a, wv):
    bd, depth, past, _ = cache_ckv.shape
    place = np.zeros((MLA_ROPE_DIM, MLA_HEADS * MLA_QK_PAD), np.float32)
    for hd in range(MLA_HEADS):
        for i in range(MLA_ROPE_DIM):
            place[i, hd * MLA_QK_PAD + MLA_NOPE_DIM + i] = 1.0
    kw, vw = MLA_HEADS * MLA_QK_PAD, MLA_HEADS * MLA_V_DIM
    return pl.pallas_call(
        _ctx_kv_kernel,
        grid=(depth, bd),
        in_specs=[
            pl.BlockSpec((None, None, past, MLA_KV_LORA), lambda l, b: (b, l, 0, 0)),
            pl.BlockSpec((None, None, past, MLA_ROPE_DIM), lambda l, b: (b, l, 0, 0)),
            pl.BlockSpec((None, MLA_KV_LORA, kw), lambda l, b: (l, 0, 0)),
            pl.BlockSpec((None, MLA_KV_LORA, vw), lambda l, b: (l, 0, 0)),
            pl.BlockSpec((MLA_ROPE_DIM, kw), lambda l, b: (0, 0)),
        ],
        out_specs=[pl.BlockSpec((None, None, past, kw), lambda l, b: (l, b, 0, 0)),
                   pl.BlockSpec((None, None, past, vw), lambda l, b: (l, b, 0, 0))],
        out_shape=[jax.ShapeDtypeStruct((depth, bd, past, kw), BF16),
                   jax.ShapeDtypeStruct((depth, bd, past, vw), BF16)],
        compiler_params=_params("parallel", "parallel"),
        name="ctx_kv",
    )(cache_ckv, cache_krope, wk_a, wv, jnp.asarray(place, BF16))


def _na_tile_geometry(rows):
    last = rows // NA_Q_ROWS - 1
    geo = []
    for j in (0, 1, last):
        r0 = j * NA_Q_ROWS
        geo.append((r0, min(max(r0 - NA_KH // 2, 0), rows - NA_WIN_ROWS)))
    return geo


def _na_bias_kernel(geo, rows, rpb_ref, o_ref):
    l = pl.program_id(0)
    hd = pl.program_id(1)
    base = (l * NA_HEADS + hd) * (2 * NA_KH - 1) * (2 * NA_KW - 1)
    qc = lax.broadcasted_iota(jnp.int32, (GRID_W, GRID_W), 0)
    kcol = lax.broadcasted_iota(jnp.int32, (GRID_W, GRID_W), 1)
    d_col = jnp.clip(kcol - qc + (NA_KW - 1), 0, 2 * NA_KW - 2)
    col_start = jnp.clip(qc - NA_KW // 2, 0, GRID_W - NA_KW)
    in_cols = (kcol >= col_start) & (kcol < col_start + NA_KW)
    neg = jnp.full((GRID_W, GRID_W), NEG_INF, F32)
    tabs = []
    for dr in range(2 * NA_KH - 1):
        acc = jnp.zeros((GRID_W, GRID_W), F32)
        for dc in range(2 * NA_KW - 1):
            acc = jnp.where(d_col == dc, rpb_ref[base + dr * (2 * NA_KW - 1) + dc], acc)
        tabs.append(jnp.where(in_cols, acc, neg))
    for kind, (r0, ws) in enumerate(geo):
        for i in range(NA_Q_ROWS):
            r = r0 + i
            lo = min(max(r - NA_KH // 2, 0), rows - NA_KH)
            for j in range(NA_WIN_ROWS):
                kr = ws + j
                blk = tabs[kr - r + NA_KH - 1] if lo <= kr < lo + NA_KH else neg
                o_ref[kind, i * GRID_W:(i + 1) * GRID_W, j * GRID_W:(j + 1) * GRID_W] = blk


def _na_bias(na_rpb, rows):
    depth = na_rpb.shape[0]
    geo = _na_tile_geometry(rows)
    qn, kn = NA_Q_ROWS * GRID_W, NA_WIN_ROWS * GRID_W
    return pl.pallas_call(
        functools.partial(_na_bias_kernel, geo, rows),
        grid=(depth, NA_HEADS),
        in_specs=[pl.BlockSpec(memory_space=pltpu.SMEM)],
        out_specs=pl.BlockSpec((None, None, 3, qn, kn), lambda l, h: (l, h, 0, 0, 0)),
        out_shape=jax.ShapeDtypeStruct((depth, NA_HEADS, 3, qn, kn), F32),
        compiler_params=_params("parallel", "parallel"),
        name="na_bias",
    )(na_rpb.reshape(-1))


def _na_lat_kernel(rows, q_ref, k_ref, v_ref, kx_ref, vx_ref, bias_ref, o_ref):
    j = pl.program_id(1)
    ws = jnp.clip(j * NA_Q_ROWS - NA_KH // 2, 0, rows - NA_WIN_ROWS)
    start = pl.multiple_of(ws * GRID_W, GRID_W)
    nk = NA_WIN_ROWS * GRID_W
    for hd in range(NA_HEADS):
        sl = slice(hd * NA_HEAD_DIM, (hd + 1) * NA_HEAD_DIM)
        q = q_ref[:, sl]
        kx = kx_ref[hd].astype(BF16)
        vx = vx_ref[hd].astype(BF16)
        s_win = _nt_dot(q, k_ref[pl.ds(start, nk), sl]) + bias_ref[hd]
        s_ctx = _nt_dot(q, kx)
        m = jnp.maximum(jnp.max(s_win, axis=-1, keepdims=True), jnp.max(s_ctx, axis=-1, keepdims=True))
        p_win = jnp.exp(s_win - m)
        p_ctx = jnp.exp(s_ctx - m)
        l = jnp.sum(p_win, axis=-1, keepdims=True) + jnp.sum(p_ctx, axis=-1, keepdims=True)
        o = _dot(p_win.astype(BF16), v_ref[pl.ds(start, nk), sl]) + _dot(p_ctx.astype(BF16), vx)
        o_ref[:, sl] = (o / l).astype(BF16)


def _na_lat_attention(qn, kn, vn, cache_k, cache_v, bias, layer, n):
    t = qn.shape[0]
    rows = n // GRID_W
    assert rows % NA_Q_ROWS == 0 and rows >= NA_WIN_ROWS + NA_Q_ROWS
    tiles = rows // NA_Q_ROWS
    tq = NA_Q_ROWS * GRID_W
    past = cache_k.shape[3]

    def kind(b, j):
        return (layer, 0, jnp.where(j == 0, 0, jnp.where(j == tiles - 1, 2, 1)), 0, 0)

    return pl.pallas_call(
        functools.partial(_na_lat_kernel, rows),
        grid=(t // n, tiles),
        in_specs=[
            pl.BlockSpec((tq, NA_WIDTH), lambda b, j: (b * tiles + j, 0)),
            pl.BlockSpec((n, NA_WIDTH), lambda b, j: (b, 0)),
            pl.BlockSpec((n, NA_WIDTH), lambda b, j: (b, 0)),
            pl.BlockSpec((None, None, NA_HEADS, past, NA_HEAD_DIM), lambda b, j: (b, layer, 0, 0, 0)),
            pl.BlockSpec((None, None, NA_HEADS, past, NA_HEAD_DIM), lambda b, j: (b, layer, 0, 0, 0)),
            pl.BlockSpec((None, NA_HEADS, None, tq, NA_WIN_ROWS * GRID_W), kind),
        ],
        out_specs=pl.BlockSpec((tq, NA_WIDTH), lambda b, j: (b * tiles + j, 0)),
        out_shape=jax.ShapeDtypeStruct((t, NA_WIDTH), BF16),
        compiler_params=_params("parallel", "parallel"),
        name="na_lat_attention",
    )(qn, kn, vn, cache_k, cache_v, bias)


def _dft_tables(n):
    def thin(j, k, period):
        ang = (2.0 * math.pi / period) * ((j[:, None] * k[None, :]) % period).astype(F32)
        return jnp.cos(ang), jnp.sin(ang)

    k = jnp.arange(n, dtype=jnp.int32)
    scale = float(n) ** -0.5
    if n % 64 == 0 and n > 64:
        n1 = n // 64
        c1, s1 = thin(jnp.arange(n1, dtype=jnp.int32), k, n1)
        c2, s2 = thin(jnp.arange(64, dtype=jnp.int32), k, n)
        c1, s1, c2, s2 = c1[:, None, :], s1[:, None, :], c2[None, :, :], s2[None, :, :]
        cm = (c1 * c2 - s1 * s2).reshape(n, n)
        sm = (s1 * c2 + c1 * s2).reshape(n, n)
    else:
        cm, sm = thin(k, k, n)
    return (cm * scale).astype(BF16), (sm * -scale).astype(BF16)


def _fourier_kernel(c_ref, s_ref, ab_ref, o_ref):
    o_ref[...] = (_dot(c_ref[...], ab_ref[:, :FN_WIDTH]) + _dot(s_ref[...], ab_ref[:, FN_WIDTH:])).astype(BF16)


def _fourier(fab, tables, n, tmf=512):
    t = fab.shape[0]
    tmf = min(tmf, n)
    tiles = n // tmf
    cm, sm = tables
    return pl.pallas_call(
        _fourier_kernel,
        grid=(tiles, t // n),
        in_specs=[
            pl.BlockSpec((tmf, n), lambda i, b: (i, 0)),
            pl.BlockSpec((tmf, n), lambda i, b: (i, 0)),
            pl.BlockSpec((n, 2 * FN_WIDTH), lambda i, b: (b, 0)),
        ],
        out_specs=pl.BlockSpec((tmf, FN_WIDTH), lambda i, b: (b * tiles + i, 0)),
        out_shape=jax.ShapeDtypeStruct((t, FN_WIDTH), BF16),
        compiler_params=_params("parallel", "parallel"),
        name="fourier",
    )(cm, sm, fab)


def _route(s_t, sb_t):
    def top2_sum(v):
        hi1, lo1 = jnp.maximum(v[0], v[1]), jnp.minimum(v[0], v[1])
        hi2, lo2 = jnp.maximum(v[2], v[3]), jnp.minimum(v[2], v[3])
        return jnp.maximum(hi1, hi2) + jnp.maximum(jnp.minimum(hi1, hi2), jnp.maximum(lo1, lo2))

    best = top2_sum(sb_t[0:EXPERTS_PER_GROUP])
    gsel = jnp.zeros_like(best, dtype=jnp.int32)
    for g in range(1, N_EXPERT_GROUPS):
        cand = top2_sum(sb_t[g * EXPERTS_PER_GROUP:(g + 1) * EXPERTS_PER_GROUP])
        better = cand > best
        gsel = jnp.where(better, g, gsel)
        best = jnp.where(better, cand, best)
    chosen = []
    for e in range(N_EXPERTS):
        g = e // EXPERTS_PER_GROUP
        beaten = jnp.zeros_like(gsel)
        for o in range(g * EXPERTS_PER_GROUP, (g + 1) * EXPERTS_PER_GROUP):
            if o == e:
                continue
            ahead = (sb_t[o] > sb_t[e]) | ((sb_t[o] == sb_t[e]) & (o < e))
            beaten = beaten + ahead.astype(jnp.int32)
        chosen.append((gsel == g) & (beaten < 2))
    picked = [jnp.where(chosen[e], s_t[e], 0.0) for e in range(N_EXPERTS)]
    denom = picked[0]
    for e in range(1, N_EXPERTS):
        denom = denom + picked[e]
    return [pk / denom for pk in picked]


def _postmix_kernel(n, final, x_ref, mod_ref, ab_ref, z_ref, zp_ref, zn_ref, yna_ref, ymla_ref, g_ref,
                    cw_ref, wfn_ref, wout_ref, g2_ref, wrh_ref, wrl_ref, br_ref, w13_ref, w2_ref, nf_ref, o_ref):
    tm = x_ref.shape[0]
    i = pl.program_id(0)
    mod = mod_ref[...]
    gate1, shift2, scale2, gate2 = mod[2:3], mod[3:4], mod[4:5], mod[5:6]

    z = z_ref[...].astype(F32)
    ridx = lax.broadcasted_iota(jnp.int32, z.shape, 0)
    at_start = (i * tm) % n == 0
    at_end = ((i + 1) * tm) % n == 0
    prev_row = jnp.where(at_start, 0.0, zp_ref[7:8, :].astype(F32))
    next_row = jnp.where(at_end, 0.0, zn_ref[0:1, :].astype(F32))
    z_m1 = jnp.where(ridx == 0, prev_row, pltpu.roll(z, 1, axis=0))
    z_p1 = jnp.where(ridx == tm - 1, next_row, pltpu.roll(z, tm - 1, axis=0))
    cw = cw_ref[...]
    y_conv = ab_ref[...].astype(F32) * (z_m1 * cw[0:1] + z * cw[1:2] + z_p1 * cw[2:3])

    y_fn = _dot(g_ref[...], wfn_ref[...])
    cat = jnp.concatenate([y_conv.astype(BF16), yna_ref[...], ymla_ref[...], y_fn.astype(BF16)], axis=-1)
    x1 = x_ref[...] + gate1 * _dot(cat, wout_ref[...])

    h2 = _rms(x1, g2_ref[...]) * (1.0 + scale2) + shift2
    h2_hi = h2.astype(BF16)
    h2_lo = (h2 - h2_hi.astype(F32)).astype(BF16)
    logits = _dot(h2_hi, wrh_ref[...]) + (_dot(h2_hi, wrl_ref[...]) + _dot(h2_lo, wrh_ref[...]))
    s = jax.nn.sigmoid(logits)
    s_t = jnp.transpose(s)
    sb_t = jnp.transpose(s + br_ref[...])
    gates_t = _route([s_t[e:e + 1] for e in range(N_EXPERTS)], [sb_t[e:e + 1] for e in range(N_EXPERTS)])
    gates_t = jnp.concatenate(gates_t + [jnp.zeros((LANES - N_EXPERTS, tm), F32)], axis=0)
    gates = jnp.transpose(gates_t)

    acc = jnp.zeros(x1.shape, F32)
    for e in range(N_EXPERTS):
        up = _dot(h2_hi, w13_ref[e])
        a, b = up[:, :EXPERT_FF], up[:, EXPERT_FF:]
        hid = (a * jax.nn.sigmoid(a)) * b * gates[:, e:e + 1]
        acc = acc + _dot(hid.astype(BF16), w2_ref[e])
    out = x1 + gate2 * acc
    if final:
        out = _rms(out, nf_ref[...])
    o_ref[...] = out


def _postmix(x, mod, layer, cond_row_of_tile, n, parts, wts, final, tm):
    t, d = x.shape
    ab, z, yna, ymla, g = parts
    nblk8 = t // 8
    per8 = tm // 8
    const2 = lambda i: (0, 0)
    lsel = lambda i: (layer, 0, 0)
    lsel4 = lambda i: (layer, 0, 0, 0)
    row = lambda w: pl.BlockSpec((tm, w), lambda i: (i, 0))
    resident = dict(pipeline_mode=pl.Buffered(1))
    in_specs = [
        row(d),
        pl.BlockSpec((None, None, 6, d), lambda i: (layer, cond_row_of_tile(i), 0, 0)),
        row(CONV_WIDTH),
        row(CONV_WIDTH),
        pl.BlockSpec((8, CONV_WIDTH), lambda i: (jnp.maximum(i * per8 - 1, 0), 0)),
        pl.BlockSpec((8, CONV_WIDTH), lambda i: (jnp.minimum((i + 1) * per8, nblk8 - 1), 0)),
        row(NA_WIDTH),
        row(MLA_HEADS * MLA_V_DIM),
        row(FN_WIDTH),
        pl.BlockSpec((None, 3, CONV_WIDTH), lsel),
        pl.BlockSpec((None, FN_WIDTH, FN_WIDTH), lsel),
        pl.BlockSpec((None, d, d), lsel),
        pl.BlockSpec((None, 1, d), lsel),
        pl.BlockSpec((d, LANES), const2),
        pl.BlockSpec((d, LANES), const2),
        pl.BlockSpec((1, LANES), const2),
        pl.BlockSpec((None, N_EXPERTS, d, 2 * EXPERT_FF), lsel4, **resident),
        pl.BlockSpec((None, N_EXPERTS, EXPERT_FF, d), lsel4, **resident),
        pl.BlockSpec((1, d), const2),
    ]
    return pl.pallas_call(
        functools.partial(_postmix_kernel, n, final),
        grid=(t // tm,),
        in_specs=in_specs,
        out_specs=row(d),
        out_shape=jax.ShapeDtypeStruct((t, d), F32),
        compiler_params=_params("parallel"),
        name="postmix",
    )(x, mod, ab, z, z, z, yna, ymla, g, wts["conv_w"], wts["w_fn"], wts["w_out"], wts["norm2"],
      wts["wr_hi"], wts["wr_lo"], wts["b_router"], wts["w13"], wts["w2"], wts["norm_f"])


def _swap_halves(w):
    nf = MLA_ROPE_DIM // 4
    idx = np.arange(MLA_ROPE_DIM).reshape(2, 2, nf)[:, ::-1, :].reshape(-1)
    return w[..., idx]


def _pack_weights(w_in, mla_wq_up, mla_wkv_up, w1, w3, w2, w_router, b_router):
    depth, d, _ = w_in.shape
    zeros = lambda w: jnp.zeros((depth, d, w), w_in.dtype)
    w_kr = w_in[..., 1920:1952]
    pad_rope = lambda w: jnp.concatenate([zeros(MLA_NOPE_DIM), w, zeros(MLA_QK_PAD - MLA_NOPE_DIM - MLA_ROPE_DIM)], -1)
    w_main = jnp.concatenate([w_in[..., :1920], pad_rope(w_kr), pad_rope(_swap_halves(w_kr)), w_in[..., 1952:]], -1)

    wq = mla_wq_up.reshape(depth, MLA_Q_LORA, MLA_HEADS, MLA_NOPE_DIM + MLA_ROPE_DIM)
    q_nope, q_rope = wq[..., :MLA_NOPE_DIM], wq[..., MLA_NOPE_DIM:]
    tail = jnp.zeros(q_rope.shape[:-1] + (MLA_QK_PAD - MLA_NOPE_DIM - MLA_ROPE_DIM,), wq.dtype)
    wq_a = jnp.concatenate([q_nope, q_rope, tail], -1).reshape(depth, MLA_Q_LORA, -1)
    wq_b = jnp.concatenate([jnp.zeros_like(q_nope), _swap_halves(q_rope), tail], -1).reshape(depth, MLA_Q_LORA, -1)

    wkv = mla_wkv_up.reshape(depth, MLA_KV_LORA, MLA_HEADS, MLA_NOPE_DIM + MLA_V_DIM)
    k_nope, v_up = wkv[..., :MLA_NOPE_DIM], wkv[..., MLA_NOPE_DIM:]
    k_tail = jnp.zeros(k_nope.shape[:-1] + (MLA_QK_PAD - MLA_NOPE_DIM,), wkv.dtype)
    wk_a = jnp.concatenate([k_nope, k_tail], -1).reshape(depth, MLA_KV_LORA, -1)
    wv = v_up.reshape(depth, MLA_KV_LORA, -1)

    wr = jnp.pad(w_router, ((0, 0), (0, LANES - N_EXPERTS)))
    wr_hi = wr.astype(BF16)
    wr_lo = (wr - wr_hi.astype(F32)).astype(BF16)
    return {
        "w_in": w_main.astype(BF16), "wq_a": wq_a.astype(BF16), "wq_b": wq_b.astype(BF16),
        "wk_a": wk_a.astype(BF16), "wv": wv.astype(BF16),
        "w13": jnp.concatenate([w1, w3], -1).astype(BF16), "w2": w2.astype(BF16),
        "wr_hi": wr_hi, "wr_lo": wr_lo,
        "b_router": jnp.pad(b_router, (0, LANES - N_EXPERTS)).reshape(1, LANES).astype(F32),
    }


def _channel_dft():
    c = np.arange(FN_GROUP_DIM)
    ang = 2.0 * np.pi * ((c[:, None] * c[None, :]) % FN_GROUP_DIM) / FN_GROUP_DIM
    out = np.zeros((FN_WIDTH, 2 * FN_WIDTH), np.float32)
    for g in range(FN_GROUPS):
        sl = slice(g * FN_GROUP_DIM, (g + 1) * FN_GROUP_DIM)
        out[sl, sl] = np.cos(ang) * FN_GROUP_DIM ** -0.5
        out[sl, FN_WIDTH + g * FN_GROUP_DIM:FN_WIDTH + (g + 1) * FN_GROUP_DIM] = np.sin(ang) * FN_GROUP_DIM ** -0.5
    return jnp.asarray(out, BF16)


def _rope_tables(n):
    tok = jnp.arange(n)
    pos = jnp.stack([tok // GRID_W, tok % GRID_W], axis=-1).astype(F32)
    nf = MLA_ROPE_DIM // 4
    freqs = ROPE_THETA ** (-jnp.arange(nf, dtype=F32) / nf)
    ang = pos[:, :, None] * freqs
    cos = jnp.broadcast_to(jnp.cos(ang)[:, :, None, :], (n, 2, 2, nf)).reshape(n, MLA_ROPE_DIM)
    sin = jnp.sin(ang)
    sin = jnp.stack([-sin, sin], axis=2).reshape(n, MLA_ROPE_DIM)
    pad = jnp.zeros((n, MLA_QK_PAD - MLA_NOPE_DIM - MLA_ROPE_DIM), F32)
    cos_t = jnp.concatenate([jnp.ones((n, MLA_NOPE_DIM), F32), cos, pad], -1)
    sin_t = jnp.concatenate([jnp.zeros((n, MLA_NOPE_DIM), F32), sin, pad], -1)
    return cos_t, sin_t


def kernel(x_prompt, x_sample, cache_na_k, cache_na_v, cache_mla_ckv, cache_mla_krope, c, c_ctx, w_ada, b_ada,
           norm1, norm2, w_in, conv_w, na_rpb, mla_gq, mla_wq_up, mla_gkv, mla_wkv_up, w_fn, w_out, w_router,
           b_router, w1, w3, w2, norm_f):
    bp, seq, d = x_prompt.shape
    bd, dec_seq, _ = x_sample.shape
    depth = w_in.shape[0]

    wts = _pack_weights(w_in, mla_wq_up, mla_wkv_up, w1, w3, w2, w_router, b_router)
    wts.update({
        "norm1": norm1.reshape(depth, 1, d), "norm2": norm2.reshape(depth, 1, d),
        "mla_gq": mla_gq.reshape(depth, 1, -1), "mla_gkv": mla_gkv.reshape(depth, 1, -1),
        "conv_w": conv_w, "w_fn": w_fn.astype(BF16), "w_out": w_out.astype(BF16),
        "norm_f": norm_f.reshape(1, d), "cs_bd": _channel_dft(),
    })

    n_cond = bd + 1
    cond = jnp.concatenate([c, c_ctx[None, :], jnp.zeros((-n_cond % 8, d), c.dtype)], axis=0)
    mod = _ada_modulation(cond, w_ada, b_ada)

    def run_pass(x, n, cond_row_of_tile, tm, rope, tables, emit_cache, lat):
        caches = []
        for layer in range(depth):
            outs = _premix(x, mod, layer, cond_row_of_tile, n, wts, rope, emit_cache, tm)
            ab, z, qn, kn, vn, qm, km, vm, fab = outs[:9]
            if lat:
                yna = _na_lat_attention(qn, kn, vn, cache_na_k, cache_na_v, na_bias, layer, n)
                ymla = _mla_lat_attention(qm, km, vm, kx, vx, layer, n)
            else:
                yna, ymla = _ctx_attention(qn, kn, vn, qm, km, vm, n)
                caches.append(outs[9:])
            g = _fourier(fab, tables, n)
            x = _postmix(x, mod, layer, cond_row_of_tile, n, (ab, z, yna, ymla, g), wts, layer == depth - 1, tm)
        return x, caches

    xp, caches = run_pass(x_prompt.reshape(bp * seq, d), seq, lambda i: bd, seq, None, _dft_tables(seq), True, False)
    new_na_k, new_na_v, new_ckv, new_krope = (jnp.stack([cl[j] for cl in caches], axis=1) for j in range(4))

    tm_lat = 256
    kx, vx = _ctx_kv(cache_mla_ckv, cache_mla_krope, wts["wk_a"], wts["wv"])
    na_bias = _na_bias(na_rpb, dec_seq // GRID_W)
    xs, _ = run_pass(x_sample.reshape(bd * dec_seq, d), dec_seq, lambda i: (i * tm_lat) // dec_seq, tm_lat,
                     _rope_tables(dec_seq), _dft_tables(dec_seq), False, True)

    return (xp.reshape(bp, seq, d), xs.reshape(bd, dec_seq, d), new_na_k, new_na_v, new_ckv, new_krope)
```

```python
import functools
import math

import numpy as np
import jax
import jax.numpy as jnp
from jax import lax
from jax.experimental import pallas as pl
from jax.experimental.pallas import tpu as pltpu

F32 = jnp.float32
BF16 = jnp.bfloat16

GRID_W = 64
CONV_WIDTH = 256
NA_HEADS = 4
NA_HEAD_DIM = 64
NA_WIDTH = NA_HEADS * NA_HEAD_DIM
NA_KH = 8
NA_KW = 16
MLA_HEADS = 4
MLA_Q_LORA = 256
MLA_KV_LORA = 128
MLA_NOPE_DIM = 64
MLA_ROPE_DIM = 32
MLA_V_DIM = 64
MLA_QK_PAD = 128
MLA_V_PAD = 96
MLA_KEY_SUB = 128
LOG2E = 1.4426950408889634
FN_GROUPS = 4
FN_GROUP_DIM = 64
FN_WIDTH = FN_GROUPS * FN_GROUP_DIM
N_EXPERTS = 16
N_EXPERT_GROUPS = 4
EXPERTS_PER_GROUP = N_EXPERTS // N_EXPERT_GROUPS
EXPERT_FF = 256
ROPE_THETA = 10000.0
EPS = 1e-6
NEG_INF = -1e30
LANES = 128

NA_SCALE = NA_HEAD_DIM ** -0.5
MLA_SCALE = (MLA_NOPE_DIM + MLA_ROPE_DIM) ** -0.5

NA_Q_ROWS = 4
NA_WIN_ROWS = 12

TM_LAT_PREMIX = 512
TM_LAT_POSTMIX = 256

VMEM_LIMIT = 56 * 1024 * 1024

_C_AB, _C_AC, _C_AU, _C_Q, _C_K, _C_V, _C_CQ = 0, 256, 512, 768, 1024, 1280, 1536
_C_CKV, _C_KR, _C_KRS, _C_FU, _C_END = 1792, 1920, 2048, 2176, 2432


def _nt_dot(a, b):
    return lax.dot_general(a, b, (((1,), (1,)), ((), ())), preferred_element_type=F32)


def _dot(a, b):
    return jnp.dot(a, b, preferred_element_type=F32)


def _rms(x, g):
    return x * lax.rsqrt(jnp.mean(x * x, axis=-1, keepdims=True) + EPS) * g


def _params(*sem, flags=None):
    return pltpu.CompilerParams(dimension_semantics=sem, vmem_limit_bytes=VMEM_LIMIT, flags=flags)


def _ada_kernel(c_ref, w_ref, b_ref, o_ref):
    cnd = c_ref[...]
    act = cnd * jax.nn.sigmoid(cnd)
    o_ref[...] = _dot(act.astype(BF16), w_ref[...].astype(BF16)) + b_ref[...]


def _ada_modulation(cond, w_ada, b_ada):
    depth, d, six_d = w_ada.shape
    r = cond.shape[0]
    tn = 1024
    out = pl.pallas_call(
        _ada_kernel,
        grid=(depth, six_d // tn),
        in_specs=[
            pl.BlockSpec((r, d), lambda l, j: (0, 0)),
            pl.BlockSpec((None, d, tn), lambda l, j: (l, 0, j)),
            pl.BlockSpec((None, 1, tn), lambda l, j: (l, 0, j)),
        ],
        out_specs=pl.BlockSpec((None, r, tn), lambda l, j: (l, 0, j)),
        out_shape=jax.ShapeDtypeStruct((depth, r, six_d), F32),
        compiler_params=_params("parallel", "parallel"),
        name="ada_modulation",
    )(cond, w_ada, b_ada.reshape(depth, 1, six_d))
    return out.reshape(depth, r, 6, d)


def _premix_kernel(lat, *refs):
    (x_ref, mod_ref, g1_ref, w_ref, gq_ref, wqa_ref, wqb_ref, gkv_ref, wka_ref, wv_ref, vone_ref, cs_ref,
     cos_ref, sin_ref) = refs[:14]
    outs = refs[14:]
    (ab_ref, z_ref, qn_ref, kn_ref, vn_ref, km_ref, fab_ref) = outs[:7]

    x = x_ref[...]
    mod = mod_ref[...]
    h = _rms(x, g1_ref[...]) * (1.0 + mod[1:2]) + mod[0:1]
    p = _dot(h.astype(BF16), w_ref[...])

    ab_ref[...] = p[:, _C_AB:_C_AC].astype(BF16)
    z_ref[...] = (p[:, _C_AC:_C_AU] * p[:, _C_AU:_C_Q]).astype(BF16)
    k_na = p[:, _C_K:_C_V]
    v_na = p[:, _C_V:_C_CQ]
    qn_ref[...] = (p[:, _C_Q:_C_K] * NA_SCALE).astype(BF16)
    kn_ref[...] = k_na.astype(BF16)
    vn_ref[...] = v_na.astype(BF16)

    cqn = _rms(p[:, _C_CQ:_C_CKV], gq_ref[...]).astype(BF16)
    ckvn = _rms(p[:, _C_CKV:_C_KR], gkv_ref[...])
    ckvn_b = ckvn.astype(BF16)
    qa = _dot(cqn, wqa_ref[...])
    kva = _dot(ckvn_b, wka_ref[...])
    v_mla = _dot(ckvn_b, wv_ref[...]) + vone_ref[...]
    kr = p[:, _C_KR:_C_KRS]
    if lat:
        cos = cos_ref[...]
        sin = sin_ref[...]
        qb = _dot(cqn, wqb_ref[...])
        krot = kr * cos + p[:, _C_KRS:_C_FU] * sin
        qt_ref, vt_ref = outs[7:]
    else:
        krot = kr
        qm_ref, vm_ref, ck_ref, cv_ref, cckv_ref, ckr_ref = outs[7:]
    for hd in range(MLA_HEADS):
        sl = slice(hd * MLA_QK_PAD, (hd + 1) * MLA_QK_PAD)
        km_ref[:, sl] = (kva[:, sl] + krot).astype(BF16)
        if lat:
            qh = (qa[:, sl] * cos + qb[:, sl] * sin) * (MLA_SCALE * LOG2E)
            qt_ref[sl, :] = jnp.transpose(qh).astype(BF16)
        else:
            qm_ref[:, sl] = (qa[:, sl] * MLA_SCALE).astype(BF16)

    fab_ref[...] = _dot(p[:, _C_FU:_C_END].astype(BF16), cs_ref[...]).astype(BF16)

    if lat:
        for j in range(v_mla.shape[1] // LANES):
            sl = slice(j * LANES, (j + 1) * LANES)
            vt_ref[sl, :] = jnp.transpose(v_mla[:, sl]).astype(BF16)
    else:
        vm_ref[...] = v_mla.astype(BF16)
        for hd in range(NA_HEADS):
            sl = slice(hd * NA_HEAD_DIM, (hd + 1) * NA_HEAD_DIM)
            ck_ref[hd] = k_na[:, sl]
            cv_ref[hd] = v_na[:, sl]
        cckv_ref[...] = ckvn
        ckr_ref[...] = kr[:, MLA_NOPE_DIM:MLA_NOPE_DIM + MLA_ROPE_DIM]


def _cond_row(lat, tm, n, ctx_row):
    return (lambda i: (i * tm) // n) if lat else (lambda i: ctx_row)


def _premix(x, mod, layer, n, wts, rope, lat, tm):
    t, d = x.shape
    if lat:
        cos_t, sin_t = rope
        wv, vone = wts["wv_ext"], wts["vone_ext"]
    else:
        cos_t = sin_t = jnp.zeros((8, LANES), F32)
        wv, vone = wts["wv"], jnp.zeros((1, MLA_HEADS * MLA_V_DIM), F32)
    vw = wv.shape[-1]
    qw = MLA_HEADS * MLA_QK_PAD
    tiles_per_seq = n // tm
    cond_row = _cond_row(lat, tm, n, mod.shape[1] - 1)
    const = lambda *_: (0, 0)
    lsel = lambda *_: (layer, 0, 0)
    rope_spec = (pl.BlockSpec((tm, LANES), lambda i: (i % tiles_per_seq, 0)) if lat
                 else pl.BlockSpec((8, LANES), const))
    in_specs = [
        pl.BlockSpec((tm, d), lambda i: (i, 0)),
        pl.BlockSpec((None, None, 6, d), lambda i: (layer, cond_row(i), 0, 0)),
        pl.BlockSpec((None, 1, d), lsel),
        pl.BlockSpec((None, d, _C_END), lsel),
        pl.BlockSpec((None, 1, MLA_Q_LORA), lsel),
        pl.BlockSpec((None, MLA_Q_LORA, qw), lsel),
        pl.BlockSpec((None, MLA_Q_LORA, qw), lsel),
        pl.BlockSpec((None, 1, MLA_KV_LORA), lsel),
        pl.BlockSpec((None, MLA_KV_LORA, qw), lsel),
        pl.BlockSpec((None, MLA_KV_LORA, vw), lsel),
        pl.BlockSpec((1, vw), const),
        pl.BlockSpec((FN_WIDTH, 2 * FN_WIDTH), const),
        rope_spec,
        rope_spec,
    ]
    row = lambda w: pl.BlockSpec((tm, w), lambda i: (i, 0))
    widths = [CONV_WIDTH, CONV_WIDTH, NA_WIDTH, NA_WIDTH, NA_WIDTH, qw, 2 * FN_WIDTH]
    out_specs = [row(w) for w in widths]
    out_shape = [jax.ShapeDtypeStruct((t, w), BF16) for w in widths]
    if lat:
        out_specs += [pl.BlockSpec((None, qw, tm), lambda i: (i, 0, 0)),
                      pl.BlockSpec((None, vw, tm), lambda i: (i, 0, 0))]
        out_shape += [jax.ShapeDtypeStruct((t // tm, qw, tm), BF16),
                      jax.ShapeDtypeStruct((t // tm, vw, tm), BF16)]
    else:
        assert tm == n
        b = t // n
        out_specs += [
            row(qw), row(vw),
            pl.BlockSpec((None, NA_HEADS, n, NA_HEAD_DIM), lambda i: (i, 0, 0, 0)),
            pl.BlockSpec((None, NA_HEADS, n, NA_HEAD_DIM), lambda i: (i, 0, 0, 0)),
            pl.BlockSpec((None, n, MLA_KV_LORA), lambda i: (i, 0, 0)),
            pl.BlockSpec((None, n, MLA_ROPE_DIM), lambda i: (i, 0, 0)),
        ]
        out_shape += [
            jax.ShapeDtypeStruct((t, qw), BF16), jax.ShapeDtypeStruct((t, vw), BF16),
            jax.ShapeDtypeStruct((b, NA_HEADS, n, NA_HEAD_DIM), F32),
            jax.ShapeDtypeStruct((b, NA_HEADS, n, NA_HEAD_DIM), F32),
            jax.ShapeDtypeStruct((b, n, MLA_KV_LORA), F32),
            jax.ShapeDtypeStruct((b, n, MLA_ROPE_DIM), F32),
        ]
    return pl.pallas_call(
        functools.partial(_premix_kernel, lat),
        grid=(t // tm,),
        in_specs=in_specs,
        out_specs=out_specs,
        out_shape=out_shape,
        compiler_params=_params("parallel"),
        name="premix_lat" if lat else "premix_ctx",
    )(x, mod, wts["norm1"], wts["w_in"], wts["mla_gq"], wts["wq_a"], wts["wq_b"], wts["mla_gkv"],
      wts["wk_a"], wv, vone, wts["cs_bd"], cos_t, sin_t)


def _softmax_attend(q, k, v):
    s = _nt_dot(q, k)
    m = jnp.max(s, axis=-1, keepdims=True)
    p = jnp.exp(s - m)
    l = jnp.sum(p, axis=-1, keepdims=True)
    return _dot(p.astype(BF16), v) / l


def _ctx_attn_kernel(qn_ref, kn_ref, vn_ref, qm_ref, km_ref, vm_ref, yna_ref, ymla_ref):
    for hd in range(NA_HEADS):
        sl = slice(hd * NA_HEAD_DIM, (hd + 1) * NA_HEAD_DIM)
        yna_ref[:, sl] = _softmax_attend(qn_ref[:, sl], kn_ref[:, sl], vn_ref[:, sl]).astype(BF16)
    for hd in range(MLA_HEADS):
        sq = slice(hd * MLA_QK_PAD, (hd + 1) * MLA_QK_PAD)
        sv = slice(hd * MLA_V_DIM, (hd + 1) * MLA_V_DIM)
        ymla_ref[:, sv] = _softmax_attend(qm_ref[:, sq], km_ref[:, sq], vm_ref[:, sv]).astype(BF16)


def _ctx_attention(qn, kn, vn, qm, km, vm, n):
    t = qn.shape[0]
    spec = lambda w: pl.BlockSpec((n, w), lambda b: (b, 0))
    ins = [qn, kn, vn, qm, km, vm]
    return pl.pallas_call(
        _ctx_attn_kernel,
        grid=(t // n,),
        in_specs=[spec(a.shape[1]) for a in ins],
        out_specs=[spec(NA_WIDTH), spec(MLA_HEADS * MLA_V_DIM)],
        out_shape=[jax.ShapeDtypeStruct((t, NA_WIDTH), BF16),
                   jax.ShapeDtypeStruct((t, MLA_HEADS * MLA_V_DIM), BF16)],
        compiler_params=_params("parallel"),
        name="ctx_attention",
    )(*ins)


def _mla_lat_kernel(qt_ref, k_ref, vt_ref, kx_ref, vxt_ref, o_ref, s_scr, p_scr):
    nchunk, _, kc = vt_ref.shape
    tq = qt_ref.shape[1]
    sub = MLA_KEY_SUB

    ksl = lambda hd: slice(hd * MLA_QK_PAD, (hd + 1) * MLA_QK_PAD)
    vsl = lambda hd: slice(hd * MLA_V_PAD, (hd + 1) * MLA_V_PAD)

    def scores(slot, k_of, nk):
        cmax = []
        for hd in range(MLA_HEADS):
            qt = qt_ref[ksl(hd), :]
            part = None
            for j in range(0, nk, sub):
                st = _dot(k_of(hd, j), qt)
                s_scr[slot, hd, j:j + sub, :] = st
                blk = jnp.max(st.reshape(sub // 8, 8, tq), axis=0)
                part = blk if part is None else jnp.maximum(part, blk)
            cmax.append(jnp.max(part, axis=0, keepdims=True))
        return tuple(cmax)

    def attend(slot, cmax, state, vt_of, nk):
        new = []
        for hd in range(MLA_HEADS):
            m_i, acc = state[hd]
            m_new = jnp.maximum(m_i, cmax[hd])
            for j in range(0, nk, sub):
                p_scr[hd, j:j + sub, :] = jnp.exp2(s_scr[slot, hd, j:j + sub, :] - m_new).astype(BF16)
            acc = jnp.exp2(m_i - m_new) * acc + _dot(vt_of(hd), p_scr[hd, 0:nk, :])
            new.append((m_new, acc))
        return tuple(new)

    lat_keys = lambda c: (lambda hd, j: k_ref[pl.ds(pl.multiple_of(c * kc, kc) + j, sub), ksl(hd)])
    past = kx_ref.shape[0]
    state = tuple((jnp.full((1, tq), NEG_INF, F32), jnp.zeros((MLA_V_PAD, tq), F32)) for _ in range(MLA_HEADS))
    cmax = scores(0, lambda hd, j: kx_ref[j:j + sub, ksl(hd)], past)
    state = attend(0, cmax, state, lambda hd: vxt_ref[vsl(hd), :], past)
    cmax = scores(0, lat_keys(0), kc)

    lat_vals = lambda c: (lambda hd: vt_ref[c, vsl(hd), :])

    def body(i, carry):
        cmax0, state = carry
        c = 2 * i
        cmax1 = scores(1, lat_keys(c + 1), kc)
        state = attend(0, cmax0, state, lat_vals(c), kc)
        cmax0 = scores(0, lat_keys(c + 2), kc)
        state = attend(1, cmax1, state, lat_vals(c + 1), kc)
        return cmax0, state

    cmax, state = lax.fori_loop(0, nchunk // 2 - 1, body, (cmax, state))
    cmax1 = scores(1, lat_keys(nchunk - 1), kc)
    state = attend(0, cmax, state, lat_vals(nchunk - 2), kc)
    state = attend(1, cmax1, state, lat_vals(nchunk - 1), kc)
    o_t = jnp.concatenate([acc[:MLA_V_DIM] / acc[MLA_V_DIM:MLA_V_DIM + 1] for _, acc in state], axis=0)
    o_ref[...] = jnp.transpose(o_t).astype(BF16)


def _mla_lat_attention(qt, km, vt, kx, vxt, layer, n):
    ntile, qw, tq = qt.shape
    t = ntile * tq
    past = kx.shape[2]
    qpb = n // tq
    return pl.pallas_call(
        _mla_lat_kernel,
        grid=(t // n, qpb),
        in_specs=[
            pl.BlockSpec((None, qw, tq), lambda b, i: (b * qpb + i, 0, 0)),
            pl.BlockSpec((n, km.shape[1]), lambda b, i: (b, 0)),
            pl.BlockSpec((qpb, vt.shape[1], tq), lambda b, i: (b, 0, 0)),
            pl.BlockSpec((None, None, past, kx.shape[3]), lambda b, i: (layer, b, 0, 0)),
            pl.BlockSpec((None, None, vxt.shape[2], past), lambda b, i: (layer, b, 0, 0)),
        ],
        out_specs=pl.BlockSpec((tq, MLA_HEADS * MLA_V_DIM), lambda b, i: (b * qpb + i, 0)),
        out_shape=jax.ShapeDtypeStruct((t, MLA_HEADS * MLA_V_DIM), BF16),
        scratch_shapes=[pltpu.VMEM((2, MLA_HEADS, max(tq, past), tq), F32),
                        pltpu.VMEM((MLA_HEADS, max(tq, past), tq), BF16)],
        compiler_params=_params("parallel", "parallel"),
        name="mla_lat_attention",
    )(qt, km, vt, kx, vxt)


def _ctx_kv_kernel(ckv_ref, kr_ref, wka_ref, wv_ref, vone_ref, place_ref, k_ref, vt_ref):
    ckv = ckv_ref[...].astype(BF16)
    k_ref[...] = (_dot(ckv, wka_ref[...]) + _dot(kr_ref[...].astype(BF16), place_ref[...])).astype(BF16)
    v = _dot(ckv, wv_ref[...]) + vone_ref[...]
    for j in range(v.shape[1] // LANES):
        sl = slice(j * LANES, (j + 1) * LANES)
        vt_ref[sl, :] = jnp.transpose(v[:, sl]).astype(BF16)


def _ctx_kv(cache_ckv, cache_krope, wk_a, wv_ext, vone_ext):
    bd, depth, past, _ = cache_ckv.shape
    place = np.zeros((MLA_ROPE_DIM, MLA_HEADS * MLA_QK_PAD), np.float32)
    for hd in range(MLA_HEADS):
        for i in range(MLA_ROPE_DIM):
            place[i, hd * MLA_QK_PAD + MLA_NOPE_DIM + i] = 1.0
    kw, vw = MLA_HEADS * MLA_QK_PAD, MLA_HEADS * MLA_V_PAD
    return pl.pallas_call(
        _ctx_kv_kernel,
        grid=(depth, bd),
        in_specs=[
            pl.BlockSpec((None, None, past, MLA_KV_LORA), lambda l, b: (b, l, 0, 0)),
            pl.BlockSpec((None, None, past, MLA_ROPE_DIM), lambda l, b: (b, l, 0, 0)),
            pl.BlockSpec((None, MLA_KV_LORA, kw), lambda l, b: (l, 0, 0)),
            pl.BlockSpec((None, MLA_KV_LORA, vw), lambda l, b: (l, 0, 0)),
            pl.BlockSpec((1, vw), lambda l, b: (0, 0)),
            pl.BlockSpec((MLA_ROPE_DIM, kw), lambda l, b: (0, 0)),
        ],
        out_specs=[pl.BlockSpec((None, None, past, kw), lambda l, b: (l, b, 0, 0)),
                   pl.BlockSpec((None, None, vw, past), lambda l, b: (l, b, 0, 0))],
        out_shape=[jax.ShapeDtypeStruct((depth, bd, past, kw), BF16),
                   jax.ShapeDtypeStruct((depth, bd, vw, past), BF16)],
        compiler_params=_params("parallel", "parallel"),
        name="ctx_kv",
    )(cache_ckv, cache_krope, wk_a, wv_ext, vone_ext, jnp.asarray(place, BF16))


def _na_tile_geometry(rows):
    last = rows // NA_Q_ROWS - 1
    geo = []
    for j in (0, 1, last):
        r0 = j * NA_Q_ROWS
        geo.append((r0, min(max(r0 - NA_KH // 2, 0), rows - NA_WIN_ROWS)))
    return geo


def _na_bias_kernel(geo, rows, rpb_ref, o_ref):
    l = pl.program_id(0)
    hd = pl.program_id(1)
    base = (l * NA_HEADS + hd) * (2 * NA_KH - 1) * (2 * NA_KW - 1)
    qc = lax.broadcasted_iota(jnp.int32, (GRID_W, GRID_W), 0)
    kcol = lax.broadcasted_iota(jnp.int32, (GRID_W, GRID_W), 1)
    d_col = jnp.clip(kcol - qc + (NA_KW - 1), 0, 2 * NA_KW - 2)
    col_start = jnp.clip(qc - NA_KW // 2, 0, GRID_W - NA_KW)
    in_cols = (kcol >= col_start) & (kcol < col_start + NA_KW)
    neg = jnp.full((GRID_W, GRID_W), NEG_INF, F32)
    tabs = []
    for dr in range(2 * NA_KH - 1):
        acc = jnp.zeros((GRID_W, GRID_W), F32)
        for dc in range(2 * NA_KW - 1):
            acc = jnp.where(d_col == dc, rpb_ref[base + dr * (2 * NA_KW - 1) + dc], acc)
        tabs.append(jnp.where(in_cols, acc, neg))
    for kind, (r0, ws) in enumerate(geo):
        for i in range(NA_Q_ROWS):
            r = r0 + i
            lo = min(max(r - NA_KH // 2, 0), rows - NA_KH)
            for j in range(NA_WIN_ROWS):
                kr = ws + j
                blk = tabs[kr - r + NA_KH - 1] if lo <= kr < lo + NA_KH else neg
                o_ref[kind, i * GRID_W:(i + 1) * GRID_W, j * GRID_W:(j + 1) * GRID_W] = blk


def _na_bias(na_rpb, rows):
    depth = na_rpb.shape[0]
    geo = _na_tile_geometry(rows)
    qn, kn = NA_Q_ROWS * GRID_W, NA_WIN_ROWS * GRID_W
    return pl.pallas_call(
        functools.partial(_na_bias_kernel, geo, rows),
        grid=(depth, NA_HEADS),
        in_specs=[pl.BlockSpec(memory_space=pltpu.SMEM)],
        out_specs=pl.BlockSpec((None, None, 3, qn, kn), lambda l, h: (l, h, 0, 0, 0)),
        out_shape=jax.ShapeDtypeStruct((depth, NA_HEADS, 3, qn, kn), F32),
        compiler_params=_params("parallel", "parallel"),
        name="na_bias",
    )(na_rpb.reshape(-1))


def _na_lat_kernel(rows, q_ref, k_ref, v_ref, kx_ref, vx_ref, bias_ref, o_ref):
    j = pl.program_id(1)
    ws = jnp.clip(j * NA_Q_ROWS - NA_KH // 2, 0, rows - NA_WIN_ROWS)
    start = pl.multiple_of(ws * GRID_W, GRID_W)
    nk = NA_WIN_ROWS * GRID_W
    for hd in range(NA_HEADS):
        sl = slice(hd * NA_HEAD_DIM, (hd + 1) * NA_HEAD_DIM)
        q = q_ref[:, sl]
        kx = kx_ref[hd].astype(BF16)
        vx = vx_ref[hd].astype(BF16)
        s_win = _nt_dot(q, k_ref[pl.ds(start, nk), sl]) + bias_ref[hd]
        s_ctx = _nt_dot(q, kx)
        m = jnp.maximum(jnp.max(s_win, axis=-1, keepdims=True), jnp.max(s_ctx, axis=-1, keepdims=True))
        p_win = jnp.exp(s_win - m)
        p_ctx = jnp.exp(s_ctx - m)
        l = jnp.sum(p_win, axis=-1, keepdims=True) + jnp.sum(p_ctx, axis=-1, keepdims=True)
        o = _dot(p_win.astype(BF16), v_ref[pl.ds(start, nk), sl]) + _dot(p_ctx.astype(BF16), vx)
        o_ref[:, sl] = (o / l).astype(BF16)


def _na_lat_attention(qn, kn, vn, cache_k, cache_v, bias, layer, n):
    t = qn.shape[0]
    rows = n // GRID_W
    assert rows % NA_Q_ROWS == 0 and rows >= NA_WIN_ROWS + NA_Q_ROWS
    tiles = rows // NA_Q_ROWS
    tq = NA_Q_ROWS * GRID_W
    past = cache_k.shape[3]

    def kind(b, j):
        return (layer, 0, jnp.where(j == 0, 0, jnp.where(j == tiles - 1, 2, 1)), 0, 0)

    return pl.pallas_call(
        functools.partial(_na_lat_kernel, rows),
        grid=(t // n, tiles),
        in_specs=[
            pl.BlockSpec((tq, NA_WIDTH), lambda b, j: (b * tiles + j, 0)),
            pl.BlockSpec((n, NA_WIDTH), lambda b, j: (b, 0)),
            pl.BlockSpec((n, NA_WIDTH), lambda b, j: (b, 0)),
            pl.BlockSpec((None, None, NA_HEADS, past, NA_HEAD_DIM), lambda b, j: (b, layer, 0, 0, 0)),
            pl.BlockSpec((None, None, NA_HEADS, past, NA_HEAD_DIM), lambda b, j: (b, layer, 0, 0, 0)),
            pl.BlockSpec((None, NA_HEADS, None, tq, NA_WIN_ROWS * GRID_W), kind),
        ],
        out_specs=pl.BlockSpec((tq, NA_WIDTH), lambda b, j: (b * tiles + j, 0)),
        out_shape=jax.ShapeDtypeStruct((t, NA_WIDTH), BF16),
        compiler_params=_params("parallel", "parallel"),
        name="na_lat_attention",
    )(qn, kn, vn, cache_k, cache_v, bias)


def _dft_tables(n):
    def thin(j, k, period):
        ang = (2.0 * math.pi / period) * ((j[:, None] * k[None, :]) % period).astype(F32)
        return jnp.cos(ang), jnp.sin(ang)

    k = jnp.arange(n, dtype=jnp.int32)
    scale = float(n) ** -0.5
    if n % 64 == 0 and n > 64:
        n1 = n // 64
        c1, s1 = thin(jnp.arange(n1, dtype=jnp.int32), k, n1)
        c2, s2 = thin(jnp.arange(64, dtype=jnp.int32), k, n)
        c1, s1, c2, s2 = c1[:, None, :], s1[:, None, :], c2[None, :, :], s2[None, :, :]
        cm = (c1 * c2 - s1 * s2).reshape(n, n)
        sm = (s1 * c2 + c1 * s2).reshape(n, n)
    else:
        cm, sm = thin(k, k, n)
    return (cm * scale).astype(BF16), (sm * -scale).astype(BF16)


def _fourier_kernel(c_ref, s_ref, ab_ref, o_ref):
    o_ref[...] = (_dot(c_ref[...], ab_ref[:, :FN_WIDTH]) + _dot(s_ref[...], ab_ref[:, FN_WIDTH:])).astype(BF16)


def _fourier(fab, tables, n, tmf=512):
    t = fab.shape[0]
    tmf = min(tmf, n)
    tiles = n // tmf
    cm, sm = tables
    return pl.pallas_call(
        _fourier_kernel,
        grid=(tiles, t // n),
        in_specs=[
            pl.BlockSpec((tmf, n), lambda i, b: (i, 0)),
            pl.BlockSpec((tmf, n), lambda i, b: (i, 0)),
            pl.BlockSpec((n, 2 * FN_WIDTH), lambda i, b: (b, 0)),
        ],
        out_specs=pl.BlockSpec((tmf, FN_WIDTH), lambda i, b: (b * tiles + i, 0)),
        out_shape=jax.ShapeDtypeStruct((t, FN_WIDTH), BF16),
        compiler_params=_params("parallel", "parallel"),
        name="fourier",
    )(cm, sm, fab)


def _route(s_t, sb_t):
    def top2_sum(v):
        hi1, lo1 = jnp.maximum(v[0], v[1]), jnp.minimum(v[0], v[1])
        hi2, lo2 = jnp.maximum(v[2], v[3]), jnp.minimum(v[2], v[3])
        return jnp.maximum(hi1, hi2) + jnp.maximum(jnp.minimum(hi1, hi2), jnp.maximum(lo1, lo2))

    best = top2_sum(sb_t[0:EXPERTS_PER_GROUP])
    gsel = jnp.zeros_like(best, dtype=jnp.int32)
    for g in range(1, N_EXPERT_GROUPS):
        cand = top2_sum(sb_t[g * EXPERTS_PER_GROUP:(g + 1) * EXPERTS_PER_GROUP])
        better = cand > best
        gsel = jnp.where(better, g, gsel)
        best = jnp.where(better, cand, best)
    chosen = []
    for e in range(N_EXPERTS):
        g = e // EXPERTS_PER_GROUP
        beaten = jnp.zeros_like(gsel)
        for o in range(g * EXPERTS_PER_GROUP, (g + 1) * EXPERTS_PER_GROUP):
            if o == e:
                continue
            ahead = (sb_t[o] > sb_t[e]) | ((sb_t[o] == sb_t[e]) & (o < e))
            beaten = beaten + ahead.astype(jnp.int32)
        chosen.append((gsel == g) & (beaten < 2))
    picked = [jnp.where(chosen[e], s_t[e], 0.0) for e in range(N_EXPERTS)]
    denom = picked[0]
    for e in range(1, N_EXPERTS):
        denom = denom + picked[e]
    return [pk / denom for pk in picked]


def _postmix_kernel(n, final, x_ref, mod_ref, ab_ref, z_ref, zp_ref, zn_ref, yna_ref, ymla_ref, g_ref,
                    cw_ref, wfn_ref, wout_ref, g2_ref, wrh_ref, wrl_ref, br_ref, w13_ref, w2_ref, nf_ref, o_ref):
    tm = x_ref.shape[0]
    i = pl.program_id(0)
    mod = mod_ref[...]
    gate1, shift2, scale2, gate2 = mod[2:3], mod[3:4], mod[4:5], mod[5:6]

    z = z_ref[...].astype(F32)
    ridx = lax.broadcasted_iota(jnp.int32, z.shape, 0)
    at_start = (i * tm) % n == 0
    at_end = ((i + 1) * tm) % n == 0
    prev_row = jnp.where(at_start, 0.0, zp_ref[7:8, :].astype(F32))
    next_row = jnp.where(at_end, 0.0, zn_ref[0:1, :].astype(F32))
    z_m1 = jnp.where(ridx == 0, prev_row, pltpu.roll(z, 1, axis=0))
    z_p1 = jnp.where(ridx == tm - 1, next_row, pltpu.roll(z, tm - 1, axis=0))
    cw = cw_ref[...]
    y_conv = ab_ref[...].astype(F32) * (z_m1 * cw[0:1] + z * cw[1:2] + z_p1 * cw[2:3])

    y_fn = _dot(g_ref[...], wfn_ref[...])
    cat = jnp.concatenate([y_conv.astype(BF16), yna_ref[...], ymla_ref[...], y_fn.astype(BF16)], axis=-1)
    x1 = x_ref[...] + gate1 * _dot(cat, wout_ref[...])

    h2 = _rms(x1, g2_ref[...]) * (1.0 + scale2) + shift2
    h2_hi = h2.astype(BF16)
    h2_lo = (h2 - h2_hi.astype(F32)).astype(BF16)
    logits = _dot(h2_hi, wrh_ref[...]) + (_dot(h2_hi, wrl_ref[...]) + _dot(h2_lo, wrh_ref[...]))
    s = jax.nn.sigmoid(logits)
    s_t = jnp.transpose(s)
    sb_t = jnp.transpose(s + br_ref[...])
    gates_t = _route([s_t[e:e + 1] for e in range(N_EXPERTS)], [sb_t[e:e + 1] for e in range(N_EXPERTS)])
    gates_t = jnp.concatenate(gates_t + [jnp.zeros((LANES - N_EXPERTS, tm), F32)], axis=0)
    gates = jnp.transpose(gates_t)

    acc = jnp.zeros(x1.shape, F32)
    for e in range(N_EXPERTS):
        up = _dot(h2_hi, w13_ref[e])
        a, b = up[:, :EXPERT_FF], up[:, EXPERT_FF:]
        hid = (a * jax.nn.sigmoid(a)) * b * gates[:, e:e + 1]
        acc = acc + _dot(hid.astype(BF16), w2_ref[e])
    out = x1 + gate2 * acc
    if final:
        out = _rms(out, nf_ref[...])
    o_ref[...] = out


def _postmix(x, mod, layer, n, parts, wts, final, lat, tm):
    t, d = x.shape
    ab, z, yna, ymla, g = parts
    nblk8 = t // 8
    per8 = tm // 8
    cond_row_of_tile = _cond_row(lat, tm, n, mod.shape[1] - 1)
    const2 = lambda i: (0, 0)
    lsel = lambda i: (layer, 0, 0)
    lsel4 = lambda i: (layer, 0, 0, 0)
    row = lambda w: pl.BlockSpec((tm, w), lambda i: (i, 0))
    resident = dict(pipeline_mode=pl.Buffered(1))
    in_specs = [
        row(d),
        pl.BlockSpec((None, None, 6, d), lambda i: (layer, cond_row_of_tile(i), 0, 0)),
        row(CONV_WIDTH),
        row(CONV_WIDTH),
        pl.BlockSpec((8, CONV_WIDTH), lambda i: (jnp.maximum(i * per8 - 1, 0), 0)),
        pl.BlockSpec((8, CONV_WIDTH), lambda i: (jnp.minimum((i + 1) * per8, nblk8 - 1), 0)),
        row(NA_WIDTH),
        row(MLA_HEADS * MLA_V_DIM),
        row(FN_WIDTH),
        pl.BlockSpec((None, 3, CONV_WIDTH), lsel),
        pl.BlockSpec((None, FN_WIDTH, FN_WIDTH), lsel),
        pl.BlockSpec((None, d, d), lsel),
        pl.BlockSpec((None, 1, d), lsel),
        pl.BlockSpec((d, LANES), const2),
        pl.BlockSpec((d, LANES), const2),
        pl.BlockSpec((1, LANES), const2),
        pl.BlockSpec((None, N_EXPERTS, d, 2 * EXPERT_FF), lsel4, **resident),
        pl.BlockSpec((None, N_EXPERTS, EXPERT_FF, d), lsel4, **resident),
        pl.BlockSpec((1, d), const2),
    ]
    return pl.pallas_call(
        functools.partial(_postmix_kernel, n, final),
        grid=(t // tm,),
        in_specs=in_specs,
        out_specs=row(d),
        out_shape=jax.ShapeDtypeStruct((t, d), F32),
        compiler_params=_params("parallel"),
        name="postmix",
    )(x, mod, ab, z, z, z, yna, ymla, g, wts["conv_w"], wts["w_fn"], wts["w_out"], wts["norm2"],
      wts["wr_hi"], wts["wr_lo"], wts["b_router"], wts["w13"], wts["w2"], wts["norm_f"])


def _swap_halves(w):
    nf = MLA_ROPE_DIM // 4
    idx = np.arange(MLA_ROPE_DIM).reshape(2, 2, nf)[:, ::-1, :].reshape(-1)
    return w[..., idx]


def _pack_weights(w_in, mla_wq_up, mla_wkv_up, w1, w3, w2, w_router, b_router):
    depth, d, _ = w_in.shape
    zeros = lambda w: jnp.zeros((depth, d, w), w_in.dtype)
    w_kr = w_in[..., 1920:1952]
    pad_rope = lambda w: jnp.concatenate([zeros(MLA_NOPE_DIM), w, zeros(MLA_QK_PAD - MLA_NOPE_DIM - MLA_ROPE_DIM)], -1)
    w_main = jnp.concatenate([w_in[..., :1920], pad_rope(w_kr), pad_rope(_swap_halves(w_kr)), w_in[..., 1952:]], -1)

    wq = mla_wq_up.reshape(depth, MLA_Q_LORA, MLA_HEADS, MLA_NOPE_DIM + MLA_ROPE_DIM)
    q_nope, q_rope = wq[..., :MLA_NOPE_DIM], wq[..., MLA_NOPE_DIM:]
    tail = jnp.zeros(q_rope.shape[:-1] + (MLA_QK_PAD - MLA_NOPE_DIM - MLA_ROPE_DIM,), wq.dtype)
    wq_a = jnp.concatenate([q_nope, q_rope, tail], -1).reshape(depth, MLA_Q_LORA, -1)
    wq_b = jnp.concatenate([jnp.zeros_like(q_nope), _swap_halves(q_rope), tail], -1).reshape(depth, MLA_Q_LORA, -1)

    wkv = mla_wkv_up.reshape(depth, MLA_KV_LORA, MLA_HEADS, MLA_NOPE_DIM + MLA_V_DIM)
    k_nope, v_up = wkv[..., :MLA_NOPE_DIM], wkv[..., MLA_NOPE_DIM:]
    k_tail = jnp.zeros(k_nope.shape[:-1] + (MLA_QK_PAD - MLA_NOPE_DIM,), wkv.dtype)
    wk_a = jnp.concatenate([k_nope, k_tail], -1).reshape(depth, MLA_KV_LORA, -1)
    wv = v_up.reshape(depth, MLA_KV_LORA, -1)
    v_tail = jnp.zeros(v_up.shape[:-1] + (MLA_V_PAD - MLA_V_DIM,), wkv.dtype)
    wv_ext = jnp.concatenate([v_up, v_tail], -1).reshape(depth, MLA_KV_LORA, -1)
    vone = np.zeros((1, MLA_HEADS * MLA_V_PAD), np.float32)
    vone[0, MLA_V_DIM::MLA_V_PAD] = 1.0

    wr = jnp.pad(w_router, ((0, 0), (0, LANES - N_EXPERTS)))
    wr_hi = wr.astype(BF16)
    wr_lo = (wr - wr_hi.astype(F32)).astype(BF16)
    return {
        "w_in": w_main.astype(BF16), "wq_a": wq_a.astype(BF16), "wq_b": wq_b.astype(BF16),
        "wk_a": wk_a.astype(BF16), "wv": wv.astype(BF16), "wv_ext": wv_ext.astype(BF16),
        "vone_ext": jnp.asarray(vone),
        "w13": jnp.concatenate([w1, w3], -1).astype(BF16), "w2": w2.astype(BF16),
        "wr_hi": wr_hi, "wr_lo": wr_lo,
        "b_router": jnp.pad(b_router, (0, LANES - N_EXPERTS)).reshape(1, LANES).astype(F32),
    }


def _channel_dft():
    c = np.arange(FN_GROUP_DIM)
    ang = 2.0 * np.pi * ((c[:, None] * c[None, :]) % FN_GROUP_DIM) / FN_GROUP_DIM
    out = np.zeros((FN_WIDTH, 2 * FN_WIDTH), np.float32)
    for g in range(FN_GROUPS):
        sl = slice(g * FN_GROUP_DIM, (g + 1) * FN_GROUP_DIM)
        out[sl, sl] = np.cos(ang) * FN_GROUP_DIM ** -0.5
        out[sl, FN_WIDTH + g * FN_GROUP_DIM:FN_WIDTH + (g + 1) * FN_GROUP_DIM] = np.sin(ang) * FN_GROUP_DIM ** -0.5
    return jnp.asarray(out, BF16)


def _rope_tables(n):
    tok = jnp.arange(n)
    pos = jnp.stack([tok // GRID_W, tok % GRID_W], axis=-1).astype(F32)
    nf = MLA_ROPE_DIM // 4
    freqs = ROPE_THETA ** (-jnp.arange(nf, dtype=F32) / nf)
    ang = pos[:, :, None] * freqs
    cos = jnp.broadcast_to(jnp.cos(ang)[:, :, None, :], (n, 2, 2, nf)).reshape(n, MLA_ROPE_DIM)
    sin = jnp.sin(ang)
    sin = jnp.stack([-sin, sin], axis=2).reshape(n, MLA_ROPE_DIM)
    pad = jnp.zeros((n, MLA_QK_PAD - MLA_NOPE_DIM - MLA_ROPE_DIM), F32)
    cos_t = jnp.concatenate([jnp.ones((n, MLA_NOPE_DIM), F32), cos, pad], -1)
    sin_t = jnp.concatenate([jnp.zeros((n, MLA_NOPE_DIM), F32), sin, pad], -1)
    return cos_t, sin_t


def kernel(x_prompt, x_sample, cache_na_k, cache_na_v, cache_mla_ckv, cache_mla_krope, c, c_ctx, w_ada, b_ada,
           norm1, norm2, w_in, conv_w, na_rpb, mla_gq, mla_wq_up, mla_gkv, mla_wkv_up, w_fn, w_out, w_router,
           b_router, w1, w3, w2, norm_f):
    bp, seq, d = x_prompt.shape
    bd, dec_seq, _ = x_sample.shape
    depth = w_in.shape[0]

    wts = _pack_weights(w_in, mla_wq_up, mla_wkv_up, w1, w3, w2, w_router, b_router)
    wts.update({
        "norm1": norm1.reshape(depth, 1, d), "norm2": norm2.reshape(depth, 1, d),
        "mla_gq": mla_gq.reshape(depth, 1, -1), "mla_gkv": mla_gkv.reshape(depth, 1, -1),
        "conv_w": conv_w, "w_fn": w_fn.astype(BF16), "w_out": w_out.astype(BF16),
        "norm_f": norm_f.reshape(1, d), "cs_bd": _channel_dft(),
    })

    cond = jnp.concatenate([c, jnp.zeros((-(bd + 1) % 8, d), c.dtype), c_ctx[None, :]], axis=0)
    mod = _ada_modulation(cond, w_ada, b_ada)

    xp = x_prompt.reshape(bp * seq, d)
    tables = _dft_tables(seq)
    caches = []
    for layer in range(depth):
        outs = _premix(xp, mod, layer, seq, wts, None, False, seq)
        ab, z, qn, kn, vn, km, fab, qm, vm = outs[:9]
        caches.append(outs[9:])
        yna, ymla = _ctx_attention(qn, kn, vn, qm, km, vm, seq)
        g = _fourier(fab, tables, seq)
        xp = _postmix(xp, mod, layer, seq, (ab, z, yna, ymla, g), wts, layer == depth - 1, False, seq)
    new_na_k, new_na_v, new_ckv, new_krope = (jnp.stack([cl[j] for cl in caches], axis=1) for j in range(4))

    xs = x_sample.reshape(bd * dec_seq, d)
    kx, vxt = _ctx_kv(cache_mla_ckv, cache_mla_krope, wts["wk_a"], wts["wv_ext"], wts["vone_ext"])
    na_bias = _na_bias(na_rpb, dec_seq // GRID_W)
    rope = _rope_tables(dec_seq)
    tables = _dft_tables(dec_seq)
    for layer in range(depth):
        ab, z, qn, kn, vn, km, fab, qt, vt = _premix(xs, mod, layer, dec_seq, wts, rope, True, TM_LAT_PREMIX)
        yna = _na_lat_attention(qn, kn, vn, cache_na_k, cache_na_v, na_bias, layer, dec_seq)
        ymla = _mla_lat_attention(qt, km, vt, kx, vxt, layer, dec_seq)
        g = _fourier(fab, tables, dec_seq)
        xs = _postmix(xs, mod, layer, dec_seq, (ab, z, yna, ymla, g), wts, layer == depth - 1, True, TM_LAT_POSTMIX)

    return (xp.reshape(bp, seq, d), xs.reshape(bd, dec_seq, d), new_na_k, new_na_v, new_ckv, new_krope)
```

```python
import functools
import math

import numpy as np
import jax
import jax.numpy as jnp
from jax import lax
from jax.experimental import pallas as pl
from jax.experimental.pallas import tpu as pltpu
from jax.experimental.pallas import tpu_sc as plsc

F32 = jnp.float32
BF16 = jnp.bfloat16

GRID_W = 64
CONV_WIDTH = 256
NA_HEADS = 4
NA_HEAD_DIM = 64
NA_WIDTH = NA_HEADS * NA_HEAD_DIM
NA_KH = 8
NA_KW = 16
MLA_HEADS = 4
MLA_Q_LORA = 256
MLA_KV_LORA = 128
MLA_NOPE_DIM = 64
MLA_ROPE_DIM = 32
MLA_V_DIM = 64
MLA_QK_PAD = 128
MLA_V_PAD = 96
MLA_KEY_SUB = 128
LOG2E = 1.4426950408889634
FN_GROUPS = 4
FN_GROUP_DIM = 64
FN_WIDTH = FN_GROUPS * FN_GROUP_DIM
N_EXPERTS = 16
N_EXPERT_GROUPS = 4
EXPERTS_PER_GROUP = N_EXPERTS // N_EXPERT_GROUPS
EXPERT_FF = 256
ROPE_THETA = 10000.0
EPS = 1e-6
NEG_INF = -1e30
LANES = 128

NA_SCALE = NA_HEAD_DIM ** -0.5
MLA_SCALE = (MLA_NOPE_DIM + MLA_ROPE_DIM) ** -0.5

NA_Q_ROWS = 4
NA_WIN_ROWS = 12

TM_LAT_PREMIX = 512
TM_LAT_MIXOUT = 256
MOE_ROW_BLOCK = 256
SC_GATHER_WINDOW = 128
SC_ROW_SPLIT = 2

VMEM_LIMIT = 56 * 1024 * 1024

_C_AB, _C_AC, _C_AU, _C_Q, _C_K, _C_V, _C_CQ = 0, 256, 512, 768, 1024, 1280, 1536
_C_CKV, _C_KR, _C_KRS, _C_FU, _C_END = 1792, 1920, 2048, 2176, 2432


def _nt_dot(a, b):
    return lax.dot_general(a, b, (((1,), (1,)), ((), ())), preferred_element_type=F32)


def _dot(a, b):
    return jnp.dot(a, b, preferred_element_type=F32)


def _rms(x, g):
    return x * lax.rsqrt(jnp.mean(x * x, axis=-1, keepdims=True) + EPS) * g


def _params(*sem, flags=None):
    return pltpu.CompilerParams(dimension_semantics=sem, vmem_limit_bytes=VMEM_LIMIT, flags=flags)


def _ada_kernel(c_ref, w_ref, b_ref, o_ref):
    cnd = c_ref[...]
    act = cnd * jax.nn.sigmoid(cnd)
    o_ref[...] = _dot(act.astype(BF16), w_ref[...].astype(BF16)) + b_ref[...]


def _ada_modulation(cond, w_ada, b_ada):
    depth, d, six_d = w_ada.shape
    r = cond.shape[0]
    tn = 1024
    out = pl.pallas_call(
        _ada_kernel,
        grid=(depth, six_d // tn),
        in_specs=[
            pl.BlockSpec((r, d), lambda l, j: (0, 0)),
            pl.BlockSpec((None, d, tn), lambda l, j: (l, 0, j)),
            pl.BlockSpec((None, 1, tn), lambda l, j: (l, 0, j)),
        ],
        out_specs=pl.BlockSpec((None, r, tn), lambda l, j: (l, 0, j)),
        out_shape=jax.ShapeDtypeStruct((depth, r, six_d), F32),
        compiler_params=_params("parallel", "parallel"),
        name="ada_modulation",
    )(cond, w_ada, b_ada.reshape(depth, 1, six_d))
    return out.reshape(depth, r, 6, d)


def _premix_kernel(lat, *refs):
    (x_ref, mod_ref, g1_ref, w_ref, gq_ref, wqa_ref, wqb_ref, gkv_ref, wka_ref, wv_ref, vone_ref, cs_ref,
     cos_ref, sin_ref) = refs[:14]
    outs = refs[14:]
    (ab_ref, z_ref, qn_ref, kn_ref, vn_ref, km_ref, fab_ref) = outs[:7]

    x = x_ref[...]
    mod = mod_ref[...]
    h = _rms(x, g1_ref[...]) * (1.0 + mod[1:2]) + mod[0:1]
    p = _dot(h.astype(BF16), w_ref[...])

    ab_ref[...] = p[:, _C_AB:_C_AC].astype(BF16)
    z_ref[...] = (p[:, _C_AC:_C_AU] * p[:, _C_AU:_C_Q]).astype(BF16)
    k_na = p[:, _C_K:_C_V]
    v_na = p[:, _C_V:_C_CQ]
    qn_ref[...] = (p[:, _C_Q:_C_K] * NA_SCALE).astype(BF16)
    kn_ref[...] = k_na.astype(BF16)
    vn_ref[...] = v_na.astype(BF16)

    cqn = _rms(p[:, _C_CQ:_C_CKV], gq_ref[...]).astype(BF16)
    ckvn = _rms(p[:, _C_CKV:_C_KR], gkv_ref[...])
    ckvn_b = ckvn.astype(BF16)
    qa = _dot(cqn, wqa_ref[...])
    kva = _dot(ckvn_b, wka_ref[...])
    v_mla = _dot(ckvn_b, wv_ref[...]) + vone_ref[...]
    kr = p[:, _C_KR:_C_KRS]
    if lat:
        cos = cos_ref[...]
        sin = sin_ref[...]
        qb = _dot(cqn, wqb_ref[...])
        krot = kr * cos + p[:, _C_KRS:_C_FU] * sin
        qt_ref, vt_ref = outs[7:]
    else:
        krot = kr
        qm_ref, vm_ref, ck_ref, cv_ref, cckv_ref, ckr_ref = outs[7:]
    for hd in range(MLA_HEADS):
        sl = slice(hd * MLA_QK_PAD, (hd + 1) * MLA_QK_PAD)
        km_ref[:, sl] = (kva[:, sl] + krot).astype(BF16)
        if lat:
            qh = (qa[:, sl] * cos + qb[:, sl] * sin) * (MLA_SCALE * LOG2E)
            qt_ref[sl, :] = jnp.transpose(qh).astype(BF16)
        else:
            qm_ref[:, sl] = (qa[:, sl] * MLA_SCALE).astype(BF16)

    fab_ref[...] = _dot(p[:, _C_FU:_C_END].astype(BF16), cs_ref[...]).astype(BF16)

    if lat:
        for j in range(v_mla.shape[1] // LANES):
            sl = slice(j * LANES, (j + 1) * LANES)
            vt_ref[sl, :] = jnp.transpose(v_mla[:, sl]).astype(BF16)
    else:
        vm_ref[...] = v_mla.astype(BF16)
        for hd in range(NA_HEADS):
            sl = slice(hd * NA_HEAD_DIM, (hd + 1) * NA_HEAD_DIM)
            ck_ref[hd] = k_na[:, sl]
            cv_ref[hd] = v_na[:, sl]
        cckv_ref[...] = ckvn
        ckr_ref[...] = kr[:, MLA_NOPE_DIM:MLA_NOPE_DIM + MLA_ROPE_DIM]


def _cond_row(lat, tm, n, ctx_row):
    return (lambda i: (i * tm) // n) if lat else (lambda i: ctx_row)


def _premix(x, mod, layer, n, wts, rope, lat, tm):
    t, d = x.shape
    if lat:
        cos_t, sin_t = rope
        wv, vone = wts["wv_ext"], wts["vone_ext"]
    else:
        cos_t = sin_t = jnp.zeros((8, LANES), F32)
        wv, vone = wts["wv"], jnp.zeros((1, MLA_HEADS * MLA_V_DIM), F32)
    vw = wv.shape[-1]
    qw = MLA_HEADS * MLA_QK_PAD
    tiles_per_seq = n // tm
    cond_row = _cond_row(lat, tm, n, mod.shape[1] - 1)
    const = lambda *_: (0, 0)
    lsel = lambda *_: (layer, 0, 0)
    rope_spec = (pl.BlockSpec((tm, LANES), lambda i: (i % tiles_per_seq, 0)) if lat
                 else pl.BlockSpec((8, LANES), const))
    in_specs = [
        pl.BlockSpec((tm, d), lambda i: (i, 0)),
        pl.BlockSpec((None, None, 6, d), lambda i: (layer, cond_row(i), 0, 0)),
        pl.BlockSpec((None, 1, d), lsel),
        pl.BlockSpec((None, d, _C_END), lsel),
        pl.BlockSpec((None, 1, MLA_Q_LORA), lsel),
        pl.BlockSpec((None, MLA_Q_LORA, qw), lsel),
        pl.BlockSpec((None, MLA_Q_LORA, qw), lsel),
        pl.BlockSpec((None, 1, MLA_KV_LORA), lsel),
        pl.BlockSpec((None, MLA_KV_LORA, qw), lsel),
        pl.BlockSpec((None, MLA_KV_LORA, vw), lsel),
        pl.BlockSpec((1, vw), const),
        pl.BlockSpec((FN_WIDTH, 2 * FN_WIDTH), const),
        rope_spec,
        rope_spec,
    ]
    row = lambda w: pl.BlockSpec((tm, w), lambda i: (i, 0))
    widths = [CONV_WIDTH, CONV_WIDTH, NA_WIDTH, NA_WIDTH, NA_WIDTH, qw, 2 * FN_WIDTH]
    out_specs = [row(w) for w in widths]
    out_shape = [jax.ShapeDtypeStruct((t, w), BF16) for w in widths]
    if lat:
        out_specs += [pl.BlockSpec((None, qw, tm), lambda i: (i, 0, 0)),
                      pl.BlockSpec((None, vw, tm), lambda i: (i, 0, 0))]
        out_shape += [jax.ShapeDtypeStruct((t // tm, qw, tm), BF16),
                      jax.ShapeDtypeStruct((t // tm, vw, tm), BF16)]
    else:
        assert tm == n
        b = t // n
        out_specs += [
            row(qw), row(vw),
            pl.BlockSpec((None, NA_HEADS, n, NA_HEAD_DIM), lambda i: (i, 0, 0, 0)),
            pl.BlockSpec((None, NA_HEADS, n, NA_HEAD_DIM), lambda i: (i, 0, 0, 0)),
            pl.BlockSpec((None, n, MLA_KV_LORA), lambda i: (i, 0, 0)),
            pl.BlockSpec((None, n, MLA_ROPE_DIM), lambda i: (i, 0, 0)),
        ]
        out_shape += [
            jax.ShapeDtypeStruct((t, qw), BF16), jax.ShapeDtypeStruct((t, vw), BF16),
            jax.ShapeDtypeStruct((b, NA_HEADS, n, NA_HEAD_DIM), F32),
            jax.ShapeDtypeStruct((b, NA_HEADS, n, NA_HEAD_DIM), F32),
            jax.ShapeDtypeStruct((b, n, MLA_KV_LORA), F32),
            jax.ShapeDtypeStruct((b, n, MLA_ROPE_DIM), F32),
        ]
    return pl.pallas_call(
        functools.partial(_premix_kernel, lat),
        grid=(t // tm,),
        in_specs=in_specs,
        out_specs=out_specs,
        out_shape=out_shape,
        compiler_params=_params("parallel"),
        name="premix_lat" if lat else "premix_ctx",
    )(x, mod, wts["norm1"], wts["w_in"], wts["mla_gq"], wts["wq_a"], wts["wq_b"], wts["mla_gkv"],
      wts["wk_a"], wv, vone, wts["cs_bd"], cos_t, sin_t)


def _softmax_attend(q, k, v):
    s = _nt_dot(q, k)
    m = jnp.max(s, axis=-1, keepdims=True)
    p = jnp.exp(s - m)
    l = jnp.sum(p, axis=-1, keepdims=True)
    return _dot(p.astype(BF16), v) / l


def _ctx_attn_kernel(qn_ref, kn_ref, vn_ref, qm_ref, km_ref, vm_ref, yna_ref, ymla_ref):
    for hd in range(NA_HEADS):
        sl = slice(hd * NA_HEAD_DIM, (hd + 1) * NA_HEAD_DIM)
        yna_ref[:, sl] = _softmax_attend(qn_ref[:, sl], kn_ref[:, sl], vn_ref[:, sl]).astype(BF16)
    for hd in range(MLA_HEADS):
        sq = slice(hd * MLA_QK_PAD, (hd + 1) * MLA_QK_PAD)
        sv = slice(hd * MLA_V_DIM, (hd + 1) * MLA_V_DIM)
        ymla_ref[:, sv] = _softmax_attend(qm_ref[:, sq], km_ref[:, sq], vm_ref[:, sv]).astype(BF16)


def _ctx_attention(qn, kn, vn, qm, km, vm, n):
    t = qn.shape[0]
    spec = lambda w: pl.BlockSpec((n, w), lambda b: (b, 0))
    ins = [qn, kn, vn, qm, km, vm]
    return pl.pallas_call(
        _ctx_attn_kernel,
        grid=(t // n,),
        in_specs=[spec(a.shape[1]) for a in ins],
        out_specs=[spec(NA_WIDTH), spec(MLA_HEADS * MLA_V_DIM)],
        out_shape=[jax.ShapeDtypeStruct((t, NA_WIDTH), BF16),
                   jax.ShapeDtypeStruct((t, MLA_HEADS * MLA_V_DIM), BF16)],
        compiler_params=_params("parallel"),
        name="ctx_attention",
    )(*ins)


def _mla_lat_kernel(qt_ref, k_ref, vt_ref, kx_ref, vxt_ref, o_ref, s_scr, p_scr):
    nchunk, _, kc = vt_ref.shape
    tq = qt_ref.shape[1]
    sub = MLA_KEY_SUB

    ksl = lambda hd: slice(hd * MLA_QK_PAD, (hd + 1) * MLA_QK_PAD)
    vsl = lambda hd: slice(hd * MLA_V_PAD, (hd + 1) * MLA_V_PAD)

    def scores(slot, k_of, nk):
        cmax = []
        for hd in range(MLA_HEADS):
            qt = qt_ref[ksl(hd), :]
            part = None
            for j in range(0, nk, sub):
                st = _dot(k_of(hd, j), qt)
                s_scr[slot, hd, j:j + sub, :] = st
                blk = jnp.max(st.reshape(sub // 8, 8, tq), axis=0)
                part = blk if part is None else jnp.maximum(part, blk)
            cmax.append(jnp.max(part, axis=0, keepdims=True))
        return tuple(cmax)

    def attend(slot, cmax, state, vt_of, nk):
        new = []
        for hd in range(MLA_HEADS):
            m_i, acc = state[hd]
            m_new = jnp.maximum(m_i, cmax[hd])
            for j in range(0, nk, sub):
                p_scr[hd, j:j + sub, :] = jnp.exp2(s_scr[slot, hd, j:j + sub, :] - m_new).astype(BF16)
            acc = jnp.exp2(m_i - m_new) * acc + _dot(vt_of(hd), p_scr[hd, 0:nk, :])
            new.append((m_new, acc))
        return tuple(new)

    lat_keys = lambda c: (lambda hd, j: k_ref[pl.ds(pl.multiple_of(c * kc, kc) + j, sub), ksl(hd)])
    past = kx_ref.shape[0]
    state = tuple((jnp.full((1, tq), NEG_INF, F32), jnp.zeros((MLA_V_PAD, tq), F32)) for _ in range(MLA_HEADS))
    cmax = scores(0, lambda hd, j: kx_ref[j:j + sub, ksl(hd)], past)
    state = attend(0, cmax, state, lambda hd: vxt_ref[vsl(hd), :], past)
    cmax = scores(0, lat_keys(0), kc)

    lat_vals = lambda c: (lambda hd: vt_ref[c, vsl(hd), :])

    def body(i, carry):
        cmax0, state = carry
        c = 2 * i
        cmax1 = scores(1, lat_keys(c + 1), kc)
        state = attend(0, cmax0, state, lat_vals(c), kc)
        cmax0 = scores(0, lat_keys(c + 2), kc)
        state = attend(1, cmax1, state, lat_vals(c + 1), kc)
        return cmax0, state

    cmax, state = lax.fori_loop(0, nchunk // 2 - 1, body, (cmax, state))
    cmax1 = scores(1, lat_keys(nchunk - 1), kc)
    state = attend(0, cmax, state, lat_vals(nchunk - 2), kc)
    state = attend(1, cmax1, state, lat_vals(nchunk - 1), kc)
    o_t = jnp.concatenate([acc[:MLA_V_DIM] / acc[MLA_V_DIM:MLA_V_DIM + 1] for _, acc in state], axis=0)
    o_ref[...] = jnp.transpose(o_t).astype(BF16)


def _mla_lat_attention(qt, km, vt, kx, vxt, layer, n):
    ntile, qw, tq = qt.shape
    t = ntile * tq
    past = kx.shape[2]
    qpb = n // tq
    return pl.pallas_call(
        _mla_lat_kernel,
        grid=(t // n, qpb),
        in_specs=[
            pl.BlockSpec((None, qw, tq), lambda b, i: (b * qpb + i, 0, 0)),
            pl.BlockSpec((n, km.shape[1]), lambda b, i: (b, 0)),
            pl.BlockSpec((qpb, vt.shape[1], tq), lambda b, i: (b, 0, 0)),
            pl.BlockSpec((None, None, past, kx.shape[3]), lambda b, i: (layer, b, 0, 0)),
            pl.BlockSpec((None, None, vxt.shape[2], past), lambda b, i: (layer, b, 0, 0)),
        ],
        out_specs=pl.BlockSpec((tq, MLA_HEADS * MLA_V_DIM), lambda b, i: (b * qpb + i, 0)),
        out_shape=jax.ShapeDtypeStruct((t, MLA_HEADS * MLA_V_DIM), BF16),
        scratch_shapes=[pltpu.VMEM((2, MLA_HEADS, max(tq, past), tq), F32),
                        pltpu.VMEM((MLA_HEADS, max(tq, past), tq), BF16)],
        compiler_params=_params("parallel", "parallel"),
        name="mla_lat_attention",
    )(qt, km, vt, kx, vxt)


def _ctx_kv_kernel(ckv_ref, kr_ref, wka_ref, wv_ref, vone_ref, place_ref, k_ref, vt_ref):
    ckv = ckv_ref[...].astype(BF16)
    k_ref[...] = (_dot(ckv, wka_ref[...]) + _dot(kr_ref[...].astype(BF16), place_ref[...])).astype(BF16)
    v = _dot(ckv, wv_ref[...]) + vone_ref[...]
    for j in range(v.shape[1] // LANES):
        sl = slice(j * LANES, (j + 1) * LANES)
        vt_ref[sl, :] = jnp.transpose(v[:, sl]).astype(BF16)


def _ctx_kv(cache_ckv, cache_krope, wk_a, wv_ext, vone_ext):
    bd, depth, past, _ = cache_ckv.shape
    place = np.zeros((MLA_ROPE_DIM, MLA_HEADS * MLA_QK_PAD), np.float32)
    for hd in range(MLA_HEADS):
        for i in range(MLA_ROPE_DIM):
            place[i, hd * MLA_QK_PAD + MLA_NOPE_DIM + i] = 1.0
    kw, vw = MLA_HEADS * MLA_QK_PAD, MLA_HEADS * MLA_V_PAD
    return pl.pallas_call(
        _ctx_kv_kernel,
        grid=(depth, bd),
        in_specs=[
            pl.BlockSpec((None, None, past, MLA_KV_LORA), lambda l, b: (b, l, 0, 0)),
            pl.BlockSpec((None, None, past, MLA_ROPE_DIM), lambda l, b: (b, l, 0, 0)),
            pl.BlockSpec((None, MLA_KV_LORA, kw), lambda l, b: (l, 0, 0)),
            pl.BlockSpec((None, MLA_KV_LORA, vw), lambda l, b: (l, 0, 0)),
            pl.BlockSpec((1, vw), lambda l, b: (0, 0)),
            pl.BlockSpec((MLA_ROPE_DIM, kw), lambda l, b: (0, 0)),
        ],
        out_specs=[pl.BlockSpec((None, None, past, kw), lambda l, b: (l, b, 0, 0)),
                   pl.BlockSpec((None, None, vw, past), lambda l, b: (l, b, 0, 0))],
        out_shape=[jax.ShapeDtypeStruct((depth, bd, past, kw), BF16),
                   jax.ShapeDtypeStruct((depth, bd, vw, past), BF16)],
        compiler_params=_params("parallel", "parallel"),
        name="ctx_kv",
    )(cache_ckv, cache_krope, wk_a, wv_ext, vone_ext, jnp.asarray(place, BF16))


def _na_tile_geometry(rows):
    last = rows // NA_Q_ROWS - 1
    geo = []
    for j in (0, 1, last):
        r0 = j * NA_Q_ROWS
        geo.append((r0, min(max(r0 - NA_KH // 2, 0), rows - NA_WIN_ROWS)))
    return geo


def _na_bias_kernel(geo, rows, rpb_ref, o_ref):
    l = pl.program_id(0)
    hd = pl.program_id(1)
    base = (l * NA_HEADS + hd) * (2 * NA_KH - 1) * (2 * NA_KW - 1)
    qc = lax.broadcasted_iota(jnp.int32, (GRID_W, GRID_W), 0)
    kcol = lax.broadcasted_iota(jnp.int32, (GRID_W, GRID_W), 1)
    d_col = jnp.clip(kcol - qc + (NA_KW - 1), 0, 2 * NA_KW - 2)
    col_start = jnp.clip(qc - NA_KW // 2, 0, GRID_W - NA_KW)
    in_cols = (kcol >= col_start) & (kcol < col_start + NA_KW)
    neg = jnp.full((GRID_W, GRID_W), NEG_INF, F32)
    tabs = []
    for dr in range(2 * NA_KH - 1):
        acc = jnp.zeros((GRID_W, GRID_W), F32)
        for dc in range(2 * NA_KW - 1):
            acc = jnp.where(d_col == dc, rpb_ref[base + dr * (2 * NA_KW - 1) + dc], acc)
        tabs.append(jnp.where(in_cols, acc, neg))
    for kind, (r0, ws) in enumerate(geo):
        for i in range(NA_Q_ROWS):
            r = r0 + i
            lo = min(max(r - NA_KH // 2, 0), rows - NA_KH)
            for j in range(NA_WIN_ROWS):
                kr = ws + j
                blk = tabs[kr - r + NA_KH - 1] if lo <= kr < lo + NA_KH else neg
                o_ref[kind, i * GRID_W:(i + 1) * GRID_W, j * GRID_W:(j + 1) * GRID_W] = blk


def _na_bias(na_rpb, rows):
    depth = na_rpb.shape[0]
    geo = _na_tile_geometry(rows)
    qn, kn = NA_Q_ROWS * GRID_W, NA_WIN_ROWS * GRID_W
    return pl.pallas_call(
        functools.partial(_na_bias_kernel, geo, rows),
        grid=(depth, NA_HEADS),
        in_specs=[pl.BlockSpec(memory_space=pltpu.SMEM)],
        out_specs=pl.BlockSpec((None, None, 3, qn, kn), lambda l, h: (l, h, 0, 0, 0)),
        out_shape=jax.ShapeDtypeStruct((depth, NA_HEADS, 3, qn, kn), F32),
        compiler_params=_params("parallel", "parallel"),
        name="na_bias",
    )(na_rpb.reshape(-1))


def _na_lat_kernel(rows, q_ref, k_ref, v_ref, kx_ref, vx_ref, bias_ref, o_ref):
    j = pl.program_id(1)
    ws = jnp.clip(j * NA_Q_ROWS - NA_KH // 2, 0, rows - NA_WIN_ROWS)
    start = pl.multiple_of(ws * GRID_W, GRID_W)
    nk = NA_WIN_ROWS * GRID_W
    for hd in range(NA_HEADS):
        sl = slice(hd * NA_HEAD_DIM, (hd + 1) * NA_HEAD_DIM)
        q = q_ref[:, sl]
        kx = kx_ref[hd].astype(BF16)
        vx = vx_ref[hd].astype(BF16)
        s_win = _nt_dot(q, k_ref[pl.ds(start, nk), sl]) + bias_ref[hd]
        s_ctx = _nt_dot(q, kx)
        m = jnp.maximum(jnp.max(s_win, axis=-1, keepdims=True), jnp.max(s_ctx, axis=-1, keepdims=True))
        p_win = jnp.exp(s_win - m)
        p_ctx = jnp.exp(s_ctx - m)
        l = jnp.sum(p_win, axis=-1, keepdims=True) + jnp.sum(p_ctx, axis=-1, keepdims=True)
        o = _dot(p_win.astype(BF16), v_ref[pl.ds(start, nk), sl]) + _dot(p_ctx.astype(BF16), vx)
        o_ref[:, sl] = (o / l).astype(BF16)


def _na_lat_attention(qn, kn, vn, cache_k, cache_v, bias, layer, n):
    t = qn.shape[0]
    rows = n // GRID_W
    assert rows % NA_Q_ROWS == 0 and rows >= NA_WIN_ROWS + NA_Q_ROWS
    tiles = rows // NA_Q_ROWS
    tq = NA_Q_ROWS * GRID_W
    past = cache_k.shape[3]

    def kind(b, j):
        return (layer, 0, jnp.where(j == 0, 0, jnp.where(j == tiles - 1, 2, 1)), 0, 0)

    return pl.pallas_call(
        functools.partial(_na_lat_kernel, rows),
        grid=(t // n, tiles),
        in_specs=[
            pl.BlockSpec((tq, NA_WIDTH), lambda b, j: (b * tiles + j, 0)),
            pl.BlockSpec((n, NA_WIDTH), lambda b, j: (b, 0)),
            pl.BlockSpec((n, NA_WIDTH), lambda b, j: (b, 0)),
            pl.BlockSpec((None, None, NA_HEADS, past, NA_HEAD_DIM), lambda b, j: (b, layer, 0, 0, 0)),
            pl.BlockSpec((None, None, NA_HEADS, past, NA_HEAD_DIM), lambda b, j: (b, layer, 0, 0, 0)),
            pl.BlockSpec((None, NA_HEADS, None, tq, NA_WIN_ROWS * GRID_W), kind),
        ],
        out_specs=pl.BlockSpec((tq, NA_WIDTH), lambda b, j: (b * tiles + j, 0)),
        out_shape=jax.ShapeDtypeStruct((t, NA_WIDTH), BF16),
        compiler_params=_params("parallel", "parallel"),
        name="na_lat_attention",
    )(qn, kn, vn, cache_k, cache_v, bias)


def _dft_tables(n):
    def thin(j, k, period):
        ang = (2.0 * math.pi / period) * ((j[:, None] * k[None, :]) % period).astype(F32)
        return jnp.cos(ang), jnp.sin(ang)

    k = jnp.arange(n, dtype=jnp.int32)
    scale = float(n) ** -0.5
    if n % 64 == 0 and n > 64:
        n1 = n // 64
        c1, s1 = thin(jnp.arange(n1, dtype=jnp.int32), k, n1)
        c2, s2 = thin(jnp.arange(64, dtype=jnp.int32), k, n)
        c1, s1, c2, s2 = c1[:, None, :], s1[:, None, :], c2[None, :, :], s2[None, :, :]
        cm = (c1 * c2 - s1 * s2).reshape(n, n)
        sm = (s1 * c2 + c1 * s2).reshape(n, n)
    else:
        cm, sm = thin(k, k, n)
    return (cm * scale).astype(BF16), (sm * -scale).astype(BF16)


def _fourier_kernel(c_ref, s_ref, ab_ref, o_ref):
    o_ref[...] = (_dot(c_ref[...], ab_ref[:, :FN_WIDTH]) + _dot(s_ref[...], ab_ref[:, FN_WIDTH:])).astype(BF16)


def _fourier(fab, tables, n, tmf=512):
    t = fab.shape[0]
    tmf = min(tmf, n)
    tiles = n // tmf
    cm, sm = tables
    return pl.pallas_call(
        _fourier_kernel,
        grid=(tiles, t // n),
        in_specs=[
            pl.BlockSpec((tmf, n), lambda i, b: (i, 0)),
            pl.BlockSpec((tmf, n), lambda i, b: (i, 0)),
            pl.BlockSpec((n, 2 * FN_WIDTH), lambda i, b: (b, 0)),
        ],
        out_specs=pl.BlockSpec((tmf, FN_WIDTH), lambda i, b: (b * tiles + i, 0)),
        out_shape=jax.ShapeDtypeStruct((t, FN_WIDTH), BF16),
        compiler_params=_params("parallel", "parallel"),
        name="fourier",
    )(cm, sm, fab)


def _route(s_t, sb_t):
    def top2_sum(v):
        hi1, lo1 = jnp.maximum(v[0], v[1]), jnp.minimum(v[0], v[1])
        hi2, lo2 = jnp.maximum(v[2], v[3]), jnp.minimum(v[2], v[3])
        return jnp.maximum(hi1, hi2) + jnp.maximum(jnp.minimum(hi1, hi2), jnp.maximum(lo1, lo2))

    best = top2_sum(sb_t[0:EXPERTS_PER_GROUP])
    gsel = jnp.zeros_like(best, dtype=jnp.int32)
    for g in range(1, N_EXPERT_GROUPS):
        cand = top2_sum(sb_t[g * EXPERTS_PER_GROUP:(g + 1) * EXPERTS_PER_GROUP])
        better = cand > best
        gsel = jnp.where(better, g, gsel)
        best = jnp.where(better, cand, best)
    chosen = []
    for e in range(N_EXPERTS):
        g = e // EXPERTS_PER_GROUP
        beaten = jnp.zeros_like(gsel)
        for o in range(g * EXPERTS_PER_GROUP, (g + 1) * EXPERTS_PER_GROUP):
            if o == e:
                continue
            ahead = (sb_t[o] > sb_t[e]) | ((sb_t[o] == sb_t[e]) & (o < e))
            beaten = beaten + ahead.astype(jnp.int32)
        chosen.append((gsel == g) & (beaten < 2))
    picked = [jnp.where(chosen[e], s_t[e], 0.0) for e in range(N_EXPERTS)]
    denom = picked[0]
    for e in range(1, N_EXPERTS):
        denom = denom + picked[e]
    return chosen, [pk / denom for pk in picked]


def _pack_pairs(x):
    w = x.shape[1] // 2
    hi = pltpu.bitcast(x[:, :w].astype(BF16).astype(F32), jnp.uint32)
    lo = pltpu.bitcast(x[:, w:].astype(BF16).astype(F32), jnp.uint32)
    return hi | (lo >> 16)


def _unpack_pairs(p):
    hi = pltpu.bitcast(p & jnp.uint32(0xFFFF0000), F32)
    lo = pltpu.bitcast(p << 16, F32)
    return jnp.concatenate([hi, lo], axis=-1)


def _mixout_kernel(n, x_ref, mod_ref, ab_ref, z_ref, zp_ref, zn_ref, yna_ref, ymla_ref, g_ref, cw_ref, wfn_ref,
                   wout_ref, g2_ref, wrh_ref, wrl_ref, br_ref, x1_ref, h2_ref, route_ref, gtok_ref, cnt_ref):
    tm = x_ref.shape[0]
    i = pl.program_id(0)
    mod = mod_ref[...]
    gate1, shift2, scale2 = mod[2:3], mod[3:4], mod[4:5]

    z = z_ref[...].astype(F32)
    ridx = lax.broadcasted_iota(jnp.int32, z.shape, 0)
    at_start = (i * tm) % n == 0
    at_end = ((i + 1) * tm) % n == 0
    prev_row = jnp.where(at_start, 0.0, zp_ref[7:8, :].astype(F32))
    next_row = jnp.where(at_end, 0.0, zn_ref[0:1, :].astype(F32))
    z_m1 = jnp.where(ridx == 0, prev_row, pltpu.roll(z, 1, axis=0))
    z_p1 = jnp.where(ridx == tm - 1, next_row, pltpu.roll(z, tm - 1, axis=0))
    cw = cw_ref[...]
    y_conv = ab_ref[...].astype(F32) * (z_m1 * cw[0:1] + z * cw[1:2] + z_p1 * cw[2:3])

    y_fn = _dot(g_ref[...], wfn_ref[...])
    cat = jnp.concatenate([y_conv.astype(BF16), yna_ref[...], ymla_ref[...], y_fn.astype(BF16)], axis=-1)
    x1 = x_ref[...] + gate1 * _dot(cat, wout_ref[...])
    x1_ref[...] = x1

    h2 = _rms(x1, g2_ref[...]) * (1.0 + scale2) + shift2
    h2_ref[...] = _pack_pairs(h2)
    h2_hi = h2.astype(BF16)
    h2_lo = (h2 - h2_hi.astype(F32)).astype(BF16)
    logits = _dot(h2_hi, wrh_ref[...]) + (_dot(h2_hi, wrl_ref[...]) + _dot(h2_lo, wrh_ref[...]))
    s = jax.nn.sigmoid(logits)
    s_t = jnp.transpose(s)
    sb_t = jnp.transpose(s + br_ref[...])
    chosen, gates = _route([s_t[e:e + 1] for e in range(N_EXPERTS)], [sb_t[e:e + 1] for e in range(N_EXPERTS)])

    @pl.when(i == 0)
    def _():
        cnt_ref[...] = jnp.zeros(cnt_ref.shape, F32)

    chosen_f = jnp.concatenate([ch.astype(F32) for ch in chosen], axis=0)
    before = lax.broadcasted_iota(jnp.int32, (tm, tm), 0) < lax.broadcasted_iota(jnp.int32, (tm, tm), 1)
    prefix = _dot(chosen_f.astype(BF16), jnp.where(before, 1.0, 0.0).astype(BF16))
    base = cnt_ref[...]
    rank = jnp.concatenate([base] * (tm // LANES), axis=1) + prefix
    cnt_ref[...] = base + jnp.sum(chosen_f, axis=1, keepdims=True)

    zero = jnp.zeros((1, tm), F32)
    seen = zero
    slots = [[zero, zero, zero], [zero, zero, zero]]
    for e in range(N_EXPERTS):
        for k in range(2):
            hit = chosen[e] & (seen == float(k))
            for j, val in enumerate((float(e), gates[e], rank[e:e + 1])):
                slots[k][j] = jnp.where(hit, val, slots[k][j])
        seen = seen + chosen_f[e:e + 1]
    (e_lo, g_lo, r_lo), (e_hi, g_hi, r_hi) = slots
    route_ref[...] = jnp.concatenate([g_lo, g_hi, e_lo, e_hi, r_lo, r_hi, zero, zero], axis=0)
    gates_t = jnp.concatenate([g_lo, g_hi, jnp.zeros((LANES - 2, tm), F32)], axis=0)
    gtok_ref[...] = jnp.transpose(gates_t)


def _mixout(x, mod, layer, n, parts, wts, lat, tm):
    t, d = x.shape
    ab, z, yna, ymla, g = parts
    nblk8 = t // 8
    per8 = tm // 8
    cond_row_of_tile = _cond_row(lat, tm, n, mod.shape[1] - 1)
    const2 = lambda i: (0, 0)
    lsel = lambda i: (layer, 0, 0)
    row = lambda w: pl.BlockSpec((tm, w), lambda i: (i, 0))
    in_specs = [
        row(d),
        pl.BlockSpec((None, None, 6, d), lambda i: (layer, cond_row_of_tile(i), 0, 0)),
        row(CONV_WIDTH),
        row(CONV_WIDTH),
        pl.BlockSpec((8, CONV_WIDTH), lambda i: (jnp.maximum(i * per8 - 1, 0), 0)),
        pl.BlockSpec((8, CONV_WIDTH), lambda i: (jnp.minimum((i + 1) * per8, nblk8 - 1), 0)),
        row(NA_WIDTH),
        row(MLA_HEADS * MLA_V_DIM),
        row(FN_WIDTH),
        pl.BlockSpec((None, 3, CONV_WIDTH), lsel),
        pl.BlockSpec((None, FN_WIDTH, FN_WIDTH), lsel),
        pl.BlockSpec((None, d, d), lsel),
        pl.BlockSpec((None, 1, d), lsel),
        pl.BlockSpec((d, LANES), const2),
        pl.BlockSpec((d, LANES), const2),
        pl.BlockSpec((1, LANES), const2),
    ]
    out_specs = [
        row(d),
        row(d // 2),
        pl.BlockSpec((8, tm), lambda i: (0, i)),
        row(LANES),
        pl.BlockSpec((N_EXPERTS, LANES), const2),
    ]
    out_shape = [
        jax.ShapeDtypeStruct((t, d), F32),
        jax.ShapeDtypeStruct((t, d // 2), jnp.uint32),
        jax.ShapeDtypeStruct((8, t), F32),
        jax.ShapeDtypeStruct((t, LANES), F32),
        jax.ShapeDtypeStruct((N_EXPERTS, LANES), F32),
    ]
    return pl.pallas_call(
        functools.partial(_mixout_kernel, n),
        grid=(t // tm,),
        in_specs=in_specs,
        out_specs=out_specs,
        out_shape=out_shape,
        compiler_params=_params("arbitrary"),
        name="mixout",
    )(x, mod, ab, z, z, z, yna, ymla, g, wts["conv_w"], wts["w_fn"], wts["w_out"], wts["norm2"],
      wts["wr_hi"], wts["wr_lo"], wts["b_router"])


def _slot_positions(route, counts, rb):
    cnt = counts[:, 0].astype(jnp.int32)
    padded = (cnt + rb - 1) // rb * rb
    ends = jnp.cumsum(padded)
    offs = ends - padded
    experts = route[2:4].astype(jnp.int32)
    ranks = route[4:6].astype(jnp.int32)
    pos = ranks
    for e in range(N_EXPERTS):
        pos = pos + jnp.where(experts == e, offs[e], 0)
    nblk = (2 * route.shape[1]) // rb + N_EXPERTS
    starts = jnp.arange(nblk, dtype=jnp.int32) * rb
    blk_expert = jnp.sum((starts[:, None] >= ends[None, :]).astype(jnp.int32), axis=1)
    blk_expert = jnp.where(blk_expert >= N_EXPERTS, -1, blk_expert)
    t = route.shape[1]
    nrows = nblk * rb
    tok = jnp.tile(jnp.arange(t, dtype=jnp.int32), 2)
    src = (jnp.arange(nrows, dtype=jnp.int32) % t).at[pos.reshape(-1)].set(tok, unique_indices=True)
    return pos, blk_expert, src


def _row_gather(table, idx):
    nrow, width = table.shape
    table = table.reshape(nrow * SC_ROW_SPLIT, width // SC_ROW_SPLIT)
    idx = (idx[:, None] * SC_ROW_SPLIT + jnp.arange(SC_ROW_SPLIT, dtype=idx.dtype)[None, :]).reshape(-1)
    out = _piece_gather(table, idx)
    return out.reshape(-1, width)


def _piece_gather(table, idx):
    b = idx.shape[0]
    w = table.shape[1]
    mesh = plsc.VectorSubcoreMesh(core_axis_name="c", subcore_axis_name="s")

    @functools.partial(pl.kernel, out_type=jax.ShapeDtypeStruct((b, w), table.dtype), mesh=mesh, scratch_types=[])
    def gather(table_hbm, idx_hbm, out_hbm):
        def body(idx_vmem, out_vmem):
            pltpu.sync_copy(table_hbm.at[idx_vmem.at[0]], out_vmem)

        pltpu.emit_pipeline(
            body,
            grid=(b // SC_GATHER_WINDOW,),
            in_specs=[pl.BlockSpec((1, SC_GATHER_WINDOW), lambda i: (0, i))],
            out_specs=[pl.BlockSpec((SC_GATHER_WINDOW, w), lambda i: (i, 0))],
            core_axis_name=("c", "s"),
            dimension_semantics=(pltpu.PARALLEL,),
        )(idx_hbm, out_hbm)

    return gather(table, idx.reshape(1, b))


def _ffn_kernel(blk_ref, xs_ref, w13_ref, w2_ref, y_ref):
    e = blk_ref[pl.program_id(0)]

    @pl.when(e >= 0)
    def _():
        xb = _unpack_pairs(xs_ref[...]).astype(BF16)
        up = _dot(xb, w13_ref[e])
        a, b = up[:, :EXPERT_FF], up[:, EXPERT_FF:]
        hid = (a * jax.nn.sigmoid(a)) * b
        y_ref[...] = _pack_pairs(_dot(hid.astype(BF16), w2_ref[e]))

    @pl.when(e < 0)
    def _():
        y_ref[...] = jnp.zeros(y_ref.shape, y_ref.dtype)


def _expert_ffn(xs, blk_expert, w13, w2, layer, rb):
    nrows, w = xs.shape
    d = 2 * w
    resident = dict(pipeline_mode=pl.Buffered(1))
    used = lambda i, blk: (jnp.where(blk[i] >= 0, i, 0), 0)
    return pl.pallas_call(
        _ffn_kernel,
        grid_spec=pltpu.PrefetchScalarGridSpec(
            num_scalar_prefetch=1,
            grid=(nrows // rb,),
            in_specs=[
                pl.BlockSpec((rb, w), used),
                pl.BlockSpec((None, N_EXPERTS, d, 2 * EXPERT_FF), lambda i, blk: (layer, 0, 0, 0), **resident),
                pl.BlockSpec((None, N_EXPERTS, EXPERT_FF, d), lambda i, blk: (layer, 0, 0, 0), **resident),
            ],
            out_specs=pl.BlockSpec((rb, w), lambda i, blk: (i, 0)),
        ),
        out_shape=jax.ShapeDtypeStruct((nrows, w), xs.dtype),
        compiler_params=_params("parallel"),
        name="expert_ffn",
    )(blk_expert, xs, w13, w2)


def _combine_kernel(final, x1_ref, gtok_ref, mod_ref, nf_ref, ylo_ref, yhi_ref, o_ref):
    g = gtok_ref[...]
    moe = g[:, 0:1] * _unpack_pairs(ylo_ref[...]) + g[:, 1:2] * _unpack_pairs(yhi_ref[...])
    out = x1_ref[...] + mod_ref[...][5:6] * moe
    if final:
        out = _rms(out, nf_ref[...])
    o_ref[...] = out


def _combine(x1, gtok, y_tok, mod, layer, n, norm_f, final, lat, tc=512):
    t, d = x1.shape
    cond_row_of_tile = _cond_row(lat, tc, n, mod.shape[1] - 1)
    row = lambda w: pl.BlockSpec((tc, w), lambda i: (i, 0))
    return pl.pallas_call(
        functools.partial(_combine_kernel, final),
        grid=(t // tc,),
        in_specs=[
            row(d),
            row(LANES),
            pl.BlockSpec((None, None, 6, d), lambda i: (layer, cond_row_of_tile(i), 0, 0)),
            pl.BlockSpec((1, d), lambda i: (0, 0)),
            pl.BlockSpec((None, tc, d // 2), lambda i: (0, i, 0)),
            pl.BlockSpec((None, tc, d // 2), lambda i: (1, i, 0)),
        ],
        out_specs=row(d),
        out_shape=jax.ShapeDtypeStruct((t, d), F32),
        compiler_params=_params("parallel"),
        name="combine",
    )(x1, gtok, mod, norm_f, y_tok, y_tok)


def _moe_layer(x, mod, layer, n, parts, wts, final, lat, tm):
    t = x.shape[0]
    x1, h2, route, gtok, counts = _mixout(x, mod, layer, n, parts, wts, lat, tm)
    pos, blk_expert, src = _slot_positions(route, counts, MOE_ROW_BLOCK)
    xs = _row_gather(h2, src)
    y = _expert_ffn(xs, blk_expert, wts["w13"], wts["w2"], layer, MOE_ROW_BLOCK)
    y_tok = _row_gather(y, pos.reshape(-1)).reshape(2, t, y.shape[1])
    return _combine(x1, gtok, y_tok, mod, layer, n, wts["norm_f"], final, lat)


def _swap_halves(w):
    nf = MLA_ROPE_DIM // 4
    idx = np.arange(MLA_ROPE_DIM).reshape(2, 2, nf)[:, ::-1, :].reshape(-1)
    return w[..., idx]


def _pack_weights(w_in, mla_wq_up, mla_wkv_up, w1, w3, w2, w_router, b_router):
    depth, d, _ = w_in.shape
    zeros = lambda w: jnp.zeros((depth, d, w), w_in.dtype)
    w_kr = w_in[..., 1920:1952]
    pad_rope = lambda w: jnp.concatenate([zeros(MLA_NOPE_DIM), w, zeros(MLA_QK_PAD - MLA_NOPE_DIM - MLA_ROPE_DIM)], -1)
    w_main = jnp.concatenate([w_in[..., :1920], pad_rope(w_kr), pad_rope(_swap_halves(w_kr)), w_in[..., 1952:]], -1)

    wq = mla_wq_up.reshape(depth, MLA_Q_LORA, MLA_HEADS, MLA_NOPE_DIM + MLA_ROPE_DIM)
    q_nope, q_rope = wq[..., :MLA_NOPE_DIM], wq[..., MLA_NOPE_DIM:]
    tail = jnp.zeros(q_rope.shape[:-1] + (MLA_QK_PAD - MLA_NOPE_DIM - MLA_ROPE_DIM,), wq.dtype)
    wq_a = jnp.concatenate([q_nope, q_rope, tail], -1).reshape(depth, MLA_Q_LORA, -1)
    wq_b = jnp.concatenate([jnp.zeros_like(q_nope), _swap_halves(q_rope), tail], -1).reshape(depth, MLA_Q_LORA, -1)

    wkv = mla_wkv_up.reshape(depth, MLA_KV_LORA, MLA_HEADS, MLA_NOPE_DIM + MLA_V_DIM)
    k_nope, v_up = wkv[..., :MLA_NOPE_DIM], wkv[..., MLA_NOPE_DIM:]
    k_tail = jnp.zeros(k_nope.shape[:-1] + (MLA_QK_PAD - MLA_NOPE_DIM,), wkv.dtype)
    wk_a = jnp.concatenate([k_nope, k_tail], -1).reshape(depth, MLA_KV_LORA, -1)
    wv = v_up.reshape(depth, MLA_KV_LORA, -1)
    v_tail = jnp.zeros(v_up.shape[:-1] + (MLA_V_PAD - MLA_V_DIM,), wkv.dtype)
    wv_ext = jnp.concatenate([v_up, v_tail], -1).reshape(depth, MLA_KV_LORA, -1)
    vone = np.zeros((1, MLA_HEADS * MLA_V_PAD), np.float32)
    vone[0, MLA_V_DIM::MLA_V_PAD] = 1.0

    wr = jnp.pad(w_router, ((0, 0), (0, LANES - N_EXPERTS)))
    wr_hi = wr.astype(BF16)
    wr_lo = (wr - wr_hi.astype(F32)).astype(BF16)
    return {
        "w_in": w_main.astype(BF16), "wq_a": wq_a.astype(BF16), "wq_b": wq_b.astype(BF16),
        "wk_a": wk_a.astype(BF16), "wv": wv.astype(BF16), "wv_ext": wv_ext.astype(BF16),
        "vone_ext": jnp.asarray(vone),
        "w13": jnp.concatenate([w1, w3], -1).astype(BF16), "w2": w2.astype(BF16),
        "wr_hi": wr_hi, "wr_lo": wr_lo,
        "b_router": jnp.pad(b_router, (0, LANES - N_EXPERTS)).reshape(1, LANES).astype(F32),
    }


def _channel_dft():
    c = np.arange(FN_GROUP_DIM)
    ang = 2.0 * np.pi * ((c[:, None] * c[None, :]) % FN_GROUP_DIM) / FN_GROUP_DIM
    out = np.zeros((FN_WIDTH, 2 * FN_WIDTH), np.float32)
    for g in range(FN_GROUPS):
        sl = slice(g * FN_GROUP_DIM, (g + 1) * FN_GROUP_DIM)
        out[sl, sl] = np.cos(ang) * FN_GROUP_DIM ** -0.5
        out[sl, FN_WIDTH + g * FN_GROUP_DIM:FN_WIDTH + (g + 1) * FN_GROUP_DIM] = np.sin(ang) * FN_GROUP_DIM ** -0.5
    return jnp.asarray(out, BF16)


def _rope_tables(n):
    tok = jnp.arange(n)
    pos = jnp.stack([tok // GRID_W, tok % GRID_W], axis=-1).astype(F32)
    nf = MLA_ROPE_DIM // 4
    freqs = ROPE_THETA ** (-jnp.arange(nf, dtype=F32) / nf)
    ang = pos[:, :, None] * freqs
    cos = jnp.broadcast_to(jnp.cos(ang)[:, :, None, :], (n, 2, 2, nf)).reshape(n, MLA_ROPE_DIM)
    sin = jnp.sin(ang)
    sin = jnp.stack([-sin, sin], axis=2).reshape(n, MLA_ROPE_DIM)
    pad = jnp.zeros((n, MLA_QK_PAD - MLA_NOPE_DIM - MLA_ROPE_DIM), F32)
    cos_t = jnp.concatenate([jnp.ones((n, MLA_NOPE_DIM), F32), cos, pad], -1)
    sin_t = jnp.concatenate([jnp.zeros((n, MLA_NOPE_DIM), F32), sin, pad], -1)
    return cos_t, sin_t


def kernel(x_prompt, x_sample, cache_na_k, cache_na_v, cache_mla_ckv, cache_mla_krope, c, c_ctx, w_ada, b_ada,
           norm1, norm2, w_in, conv_w, na_rpb, mla_gq, mla_wq_up, mla_gkv, mla_wkv_up, w_fn, w_out, w_router,
           b_router, w1, w3, w2, norm_f):
    bp, seq, d = x_prompt.shape
    bd, dec_seq, _ = x_sample.shape
    depth = w_in.shape[0]

    wts = _pack_weights(w_in, mla_wq_up, mla_wkv_up, w1, w3, w2, w_router, b_router)
    wts.update({
        "norm1": norm1.reshape(depth, 1, d), "norm2": norm2.reshape(depth, 1, d),
        "mla_gq": mla_gq.reshape(depth, 1, -1), "mla_gkv": mla_gkv.reshape(depth, 1, -1),
        "conv_w": conv_w, "w_fn": w_fn.astype(BF16), "w_out": w_out.astype(BF16),
        "norm_f": norm_f.reshape(1, d), "cs_bd": _channel_dft(),
    })

    cond = jnp.concatenate([c, jnp.zeros((-(bd + 1) % 8, d), c.dtype), c_ctx[None, :]], axis=0)
    mod = _ada_modulation(cond, w_ada, b_ada)

    xp = x_prompt.reshape(bp * seq, d)
    tables = _dft_tables(seq)
    caches = []
    for layer in range(depth):
        outs = _premix(xp, mod, layer, seq, wts, None, False, seq)
        ab, z, qn, kn, vn, km, fab, qm, vm = outs[:9]
        caches.append(outs[9:])
        yna, ymla = _ctx_attention(qn, kn, vn, qm, km, vm, seq)
        g = _fourier(fab, tables, seq)
        xp = _moe_layer(xp, mod, layer, seq, (ab, z, yna, ymla, g), wts, layer == depth - 1, False, seq)
    new_na_k, new_na_v, new_ckv, new_krope = (jnp.stack([cl[j] for cl in caches], axis=1) for j in range(4))

    xs = x_sample.reshape(bd * dec_seq, d)
    kx, vxt = _ctx_kv(cache_mla_ckv, cache_mla_krope, wts["wk_a"], wts["wv_ext"], wts["vone_ext"])
    na_bias = _na_bias(na_rpb, dec_seq // GRID_W)
    rope = _rope_tables(dec_seq)
    tables = _dft_tables(dec_seq)
    for layer in range(depth):
        ab, z, qn, kn, vn, km, fab, qt, vt = _premix(xs, mod, layer, dec_seq, wts, rope, True, TM_LAT_PREMIX)
        yna = _na_lat_attention(qn, kn, vn, cache_na_k, cache_na_v, na_bias, layer, dec_seq)
        ymla = _mla_lat_attention(qt, km, vt, kx, vxt, layer, dec_seq)
        g = _fourier(fab, tables, dec_seq)
        xs = _moe_layer(xs, mod, layer, dec_seq, (ab, z, yna, ymla, g), wts, layer == depth - 1, True, TM_LAT_MIXOUT)

    return (xp.reshape(bp, seq, d), xs.reshape(bd, dec_seq, d), new_na_k, new_na_v, new_ckv, new_krope)
```

```python
import functools
import math

import numpy as np
import jax
import jax.numpy as jnp
from jax import lax
from jax.experimental import pallas as pl
from jax.experimental.pallas import tpu as pltpu
from jax.experimental.pallas import tpu_sc as plsc

F32 = jnp.float32
BF16 = jnp.bfloat16

GRID_W = 64
CONV_WIDTH = 256
NA_HEADS = 4
NA_HEAD_DIM = 64
NA_WIDTH = NA_HEADS * NA_HEAD_DIM
NA_KH = 8
NA_KW = 16
MLA_HEADS = 4
MLA_Q_LORA = 256
MLA_KV_LORA = 128
MLA_NOPE_DIM = 64
MLA_ROPE_DIM = 32
MLA_V_DIM = 64
MLA_QK_PAD = 128
MLA_V_PAD = 96
MLA_KEY_SUB = 128
LOG2E = 1.4426950408889634
FN_GROUPS = 4
FN_GROUP_DIM = 64
FN_WIDTH = FN_GROUPS * FN_GROUP_DIM
N_EXPERTS = 16
N_EXPERT_GROUPS = 4
EXPERTS_PER_GROUP = N_EXPERTS // N_EXPERT_GROUPS
EXPERT_FF = 256
ROPE_THETA = 10000.0
EPS = 1e-6
NEG_INF = -1e30
LANES = 128

NA_SCALE = NA_HEAD_DIM ** -0.5
MLA_SCALE = (MLA_NOPE_DIM + MLA_ROPE_DIM) ** -0.5

NA_Q_ROWS = 4
NA_WIN_ROWS = 12

TM_LAT_PREMIX = 512
TM_LAT_MIXOUT = 256
MOE_ROW_BLOCK = 256
SC_WINDOW = 128
MOE_PIECES = 2

VMEM_LIMIT = 56 * 1024 * 1024

_C_AB, _C_AC, _C_AU, _C_Q, _C_K, _C_V, _C_CQ = 0, 256, 512, 768, 1024, 1280, 1536
_C_CKV, _C_KR, _C_KRS, _C_FU, _C_END = 1792, 1920, 2048, 2176, 2432


def _nt_dot(a, b):
    return lax.dot_general(a, b, (((1,), (1,)), ((), ())), preferred_element_type=F32)


def _dot(a, b):
    return jnp.dot(a, b, preferred_element_type=F32)


def _rms(x, g):
    return x * lax.rsqrt(jnp.mean(x * x, axis=-1, keepdims=True) + EPS) * g


def _params(*sem, flags=None):
    return pltpu.CompilerParams(dimension_semantics=sem, vmem_limit_bytes=VMEM_LIMIT, flags=flags)


def _ada_kernel(c_ref, w_ref, b_ref, o_ref):
    cnd = c_ref[...]
    act = cnd * jax.nn.sigmoid(cnd)
    o_ref[...] = _dot(act.astype(BF16), w_ref[...].astype(BF16)) + b_ref[...]


def _ada_modulation(cond, w_ada, b_ada):
    depth, d, six_d = w_ada.shape
    r = cond.shape[0]
    tn = 1024
    out = pl.pallas_call(
        _ada_kernel,
        grid=(depth, six_d // tn),
        in_specs=[
            pl.BlockSpec((r, d), lambda l, j: (0, 0)),
            pl.BlockSpec((None, d, tn), lambda l, j: (l, 0, j)),
            pl.BlockSpec((None, 1, tn), lambda l, j: (l, 0, j)),
        ],
        out_specs=pl.BlockSpec((None, r, tn), lambda l, j: (l, 0, j)),
        out_shape=jax.ShapeDtypeStruct((depth, r, six_d), F32),
        compiler_params=_params("parallel", "parallel"),
        name="ada_modulation",
    )(cond, w_ada, b_ada.reshape(depth, 1, six_d))
    return out.reshape(depth, r, 6, d)


def _premix_kernel(lat, *refs):
    (x_ref, mod_ref, g1_ref, w_ref, gq_ref, wqa_ref, wqb_ref, gkv_ref, wka_ref, wv_ref, vone_ref, cs_ref,
     cos_ref, sin_ref) = refs[:14]
    outs = refs[14:]
    (ab_ref, z_ref, qn_ref, kn_ref, vn_ref, km_ref, fab_ref) = outs[:7]

    x = x_ref[...]
    mod = mod_ref[...]
    h = _rms(x, g1_ref[...]) * (1.0 + mod[1:2]) + mod[0:1]
    p = _dot(h.astype(BF16), w_ref[...])

    ab_ref[...] = p[:, _C_AB:_C_AC].astype(BF16)
    z_ref[...] = (p[:, _C_AC:_C_AU] * p[:, _C_AU:_C_Q]).astype(BF16)
    k_na = p[:, _C_K:_C_V]
    v_na = p[:, _C_V:_C_CQ]
    qn_ref[...] = (p[:, _C_Q:_C_K] * NA_SCALE).astype(BF16)
    kn_ref[...] = k_na.astype(BF16)
    vn_ref[...] = v_na.astype(BF16)

    cqn = _rms(p[:, _C_CQ:_C_CKV], gq_ref[...]).astype(BF16)
    ckvn = _rms(p[:, _C_CKV:_C_KR], gkv_ref[...])
    ckvn_b = ckvn.astype(BF16)
    qa = _dot(cqn, wqa_ref[...])
    kva = _dot(ckvn_b, wka_ref[...])
    v_mla = _dot(ckvn_b, wv_ref[...]) + vone_ref[...]
    kr = p[:, _C_KR:_C_KRS]
    if lat:
        cos = cos_ref[...]
        sin = sin_ref[...]
        qb = _dot(cqn, wqb_ref[...])
        krot = kr * cos + p[:, _C_KRS:_C_FU] * sin
        qt_ref, vt_ref = outs[7:]
    else:
        krot = kr
        qm_ref, vm_ref, ck_ref, cv_ref, cckv_ref, ckr_ref = outs[7:]
    for hd in range(MLA_HEADS):
        sl = slice(hd * MLA_QK_PAD, (hd + 1) * MLA_QK_PAD)
        km_ref[:, sl] = (kva[:, sl] + krot).astype(BF16)
        if lat:
            qh = (qa[:, sl] * cos + qb[:, sl] * sin) * (MLA_SCALE * LOG2E)
            qt_ref[sl, :] = jnp.transpose(qh).astype(BF16)
        else:
            qm_ref[:, sl] = (qa[:, sl] * MLA_SCALE).astype(BF16)

    fab_ref[...] = _dot(p[:, _C_FU:_C_END].astype(BF16), cs_ref[...]).astype(BF16)

    if lat:
        for j in range(v_mla.shape[1] // LANES):
            sl = slice(j * LANES, (j + 1) * LANES)
            vt_ref[sl, :] = jnp.transpose(v_mla[:, sl]).astype(BF16)
    else:
        vm_ref[...] = v_mla.astype(BF16)
        for hd in range(NA_HEADS):
            sl = slice(hd * NA_HEAD_DIM, (hd + 1) * NA_HEAD_DIM)
            ck_ref[hd] = k_na[:, sl]
            cv_ref[hd] = v_na[:, sl]
        cckv_ref[...] = ckvn
        ckr_ref[...] = kr[:, MLA_NOPE_DIM:MLA_NOPE_DIM + MLA_ROPE_DIM]


def _cond_row(lat, tm, n, ctx_row):
    return (lambda i: (i * tm) // n) if lat else (lambda i: ctx_row)


def _premix(x, mod, layer, n, wts, rope, lat, tm):
    t, d = x.shape
    if lat:
        cos_t, sin_t = rope
        wv, vone = wts["wv_ext"], wts["vone_ext"]
    else:
        cos_t = sin_t = jnp.zeros((8, LANES), F32)
        wv, vone = wts["wv"], jnp.zeros((1, MLA_HEADS * MLA_V_DIM), F32)
    vw = wv.shape[-1]
    qw = MLA_HEADS * MLA_QK_PAD
    tiles_per_seq = n // tm
    cond_row = _cond_row(lat, tm, n, mod.shape[1] - 1)
    const = lambda *_: (0, 0)
    lsel = lambda *_: (layer, 0, 0)
    rope_spec = (pl.BlockSpec((tm, LANES), lambda i: (i % tiles_per_seq, 0)) if lat
                 else pl.BlockSpec((8, LANES), const))
    in_specs = [
        pl.BlockSpec((tm, d), lambda i: (i, 0)),
        pl.BlockSpec((None, None, 6, d), lambda i: (layer, cond_row(i), 0, 0)),
        pl.BlockSpec((None, 1, d), lsel),
        pl.BlockSpec((None, d, _C_END), lsel),
        pl.BlockSpec((None, 1, MLA_Q_LORA), lsel),
        pl.BlockSpec((None, MLA_Q_LORA, qw), lsel),
        pl.BlockSpec((None, MLA_Q_LORA, qw), lsel),
        pl.BlockSpec((None, 1, MLA_KV_LORA), lsel),
        pl.BlockSpec((None, MLA_KV_LORA, qw), lsel),
        pl.BlockSpec((None, MLA_KV_LORA, vw), lsel),
        pl.BlockSpec((1, vw), const),
        pl.BlockSpec((FN_WIDTH, 2 * FN_WIDTH), const),
        rope_spec,
        rope_spec,
    ]
    row = lambda w: pl.BlockSpec((tm, w), lambda i: (i, 0))
    widths = [CONV_WIDTH, CONV_WIDTH, NA_WIDTH, NA_WIDTH, NA_WIDTH, qw, 2 * FN_WIDTH]
    out_specs = [row(w) for w in widths]
    out_shape = [jax.ShapeDtypeStruct((t, w), BF16) for w in widths]
    if lat:
        out_specs += [pl.BlockSpec((None, qw, tm), lambda i: (i, 0, 0)),
                      pl.BlockSpec((None, vw, tm), lambda i: (i, 0, 0))]
        out_shape += [jax.ShapeDtypeStruct((t // tm, qw, tm), BF16),
                      jax.ShapeDtypeStruct((t // tm, vw, tm), BF16)]
    else:
        assert tm == n
        b = t // n
        out_specs += [
            row(qw), row(vw),
            pl.BlockSpec((None, NA_HEADS, n, NA_HEAD_DIM), lambda i: (i, 0, 0, 0)),
            pl.BlockSpec((None, NA_HEADS, n, NA_HEAD_DIM), lambda i: (i, 0, 0, 0)),
            pl.BlockSpec((None, n, MLA_KV_LORA), lambda i: (i, 0, 0)),
            pl.BlockSpec((None, n, MLA_ROPE_DIM), lambda i: (i, 0, 0)),
        ]
        out_shape += [
            jax.ShapeDtypeStruct((t, qw), BF16), jax.ShapeDtypeStruct((t, vw), BF16),
            jax.ShapeDtypeStruct((b, NA_HEADS, n, NA_HEAD_DIM), F32),
            jax.ShapeDtypeStruct((b, NA_HEADS, n, NA_HEAD_DIM), F32),
            jax.ShapeDtypeStruct((b, n, MLA_KV_LORA), F32),
            jax.ShapeDtypeStruct((b, n, MLA_ROPE_DIM), F32),
        ]
    return pl.pallas_call(
        functools.partial(_premix_kernel, lat),
        grid=(t // tm,),
        in_specs=in_specs,
        out_specs=out_specs,
        out_shape=out_shape,
        compiler_params=_params("parallel"),
        name="premix_lat" if lat else "premix_ctx",
    )(x, mod, wts["norm1"], wts["w_in"], wts["mla_gq"], wts["wq_a"], wts["wq_b"], wts["mla_gkv"],
      wts["wk_a"], wv, vone, wts["cs_bd"], cos_t, sin_t)


def _softmax_attend(q, k, v):
    s = _nt_dot(q, k)
    m = jnp.max(s, axis=-1, keepdims=True)
    p = jnp.exp(s - m)
    l = jnp.sum(p, axis=-1, keepdims=True)
    return _dot(p.astype(BF16), v) / l


def _ctx_attn_kernel(qn_ref, kn_ref, vn_ref, qm_ref, km_ref, vm_ref, yna_ref, ymla_ref):
    for hd in range(NA_HEADS):
        sl = slice(hd * NA_HEAD_DIM, (hd + 1) * NA_HEAD_DIM)
        yna_ref[:, sl] = _softmax_attend(qn_ref[:, sl], kn_ref[:, sl], vn_ref[:, sl]).astype(BF16)
    for hd in range(MLA_HEADS):
        sq = slice(hd * MLA_QK_PAD, (hd + 1) * MLA_QK_PAD)
        sv = slice(hd * MLA_V_DIM, (hd + 1) * MLA_V_DIM)
        ymla_ref[:, sv] = _softmax_attend(qm_ref[:, sq], km_ref[:, sq], vm_ref[:, sv]).astype(BF16)


def _ctx_attention(qn, kn, vn, qm, km, vm, n):
    t = qn.shape[0]
    spec = lambda w: pl.BlockSpec((n, w), lambda b: (b, 0))
    ins = [qn, kn, vn, qm, km, vm]
    return pl.pallas_call(
        _ctx_attn_kernel,
        grid=(t // n,),
        in_specs=[spec(a.shape[1]) for a in ins],
        out_specs=[spec(NA_WIDTH), spec(MLA_HEADS * MLA_V_DIM)],
        out_shape=[jax.ShapeDtypeStruct((t, NA_WIDTH), BF16),
                   jax.ShapeDtypeStruct((t, MLA_HEADS * MLA_V_DIM), BF16)],
        compiler_params=_params("parallel"),
        name="ctx_attention",
    )(*ins)


def _mla_lat_kernel(qt_ref, k_ref, vt_ref, kx_ref, vxt_ref, o_ref, s_scr, p_scr):
    nchunk, _, kc = vt_ref.shape
    tq = qt_ref.shape[1]
    sub = MLA_KEY_SUB

    ksl = lambda hd: slice(hd * MLA_QK_PAD, (hd + 1) * MLA_QK_PAD)
    vsl = lambda hd: slice(hd * MLA_V_PAD, (hd + 1) * MLA_V_PAD)

    def scores(slot, k_of, nk):
        cmax = []
        for hd in range(MLA_HEADS):
            qt = qt_ref[ksl(hd), :]
            part = None
            for j in range(0, nk, sub):
                st = _dot(k_of(hd, j), qt)
                s_scr[slot, hd, j:j + sub, :] = st
                blk = jnp.max(st.reshape(sub // 8, 8, tq), axis=0)
                part = blk if part is None else jnp.maximum(part, blk)
            cmax.append(jnp.max(part, axis=0, keepdims=True))
        return tuple(cmax)

    def attend(slot, cmax, state, vt_of, nk):
        new = []
        for hd in range(MLA_HEADS):
            m_i, acc = state[hd]
            m_new = jnp.maximum(m_i, cmax[hd])
            for j in range(0, nk, sub):
                p_scr[hd, j:j + sub, :] = jnp.exp2(s_scr[slot, hd, j:j + sub, :] - m_new).astype(BF16)
            acc = jnp.exp2(m_i - m_new) * acc + _dot(vt_of(hd), p_scr[hd, 0:nk, :])
            new.append((m_new, acc))
        return tuple(new)

    lat_keys = lambda c: (lambda hd, j: k_ref[pl.ds(pl.multiple_of(c * kc, kc) + j, sub), ksl(hd)])
    past = kx_ref.shape[0]
    state = tuple((jnp.full((1, tq), NEG_INF, F32), jnp.zeros((MLA_V_PAD, tq), F32)) for _ in range(MLA_HEADS))
    cmax = scores(0, lambda hd, j: kx_ref[j:j + sub, ksl(hd)], past)
    state = attend(0, cmax, state, lambda hd: vxt_ref[vsl(hd), :], past)
    cmax = scores(0, lat_keys(0), kc)

    lat_vals = lambda c: (lambda hd: vt_ref[c, vsl(hd), :])

    def body(i, carry):
        cmax0, state = carry
        c = 2 * i
        cmax1 = scores(1, lat_keys(c + 1), kc)
        state = attend(0, cmax0, state, lat_vals(c), kc)
        cmax0 = scores(0, lat_keys(c + 2), kc)
        state = attend(1, cmax1, state, lat_vals(c + 1), kc)
        return cmax0, state

    cmax, state = lax.fori_loop(0, nchunk // 2 - 1, body, (cmax, state))
    cmax1 = scores(1, lat_keys(nchunk - 1), kc)
    state = attend(0, cmax, state, lat_vals(nchunk - 2), kc)
    state = attend(1, cmax1, state, lat_vals(nchunk - 1), kc)
    o_t = jnp.concatenate([acc[:MLA_V_DIM] / acc[MLA_V_DIM:MLA_V_DIM + 1] for _, acc in state], axis=0)
    o_ref[...] = jnp.transpose(o_t).astype(BF16)


def _mla_lat_attention(qt, km, vt, kx, vxt, layer, n):
    ntile, qw, tq = qt.shape
    t = ntile * tq
    past = kx.shape[2]
    qpb = n // tq
    return pl.pallas_call(
        _mla_lat_kernel,
        grid=(t // n, qpb),
        in_specs=[
            pl.BlockSpec((None, qw, tq), lambda b, i: (b * qpb + i, 0, 0)),
            pl.BlockSpec((n, km.shape[1]), lambda b, i: (b, 0)),
            pl.BlockSpec((qpb, vt.shape[1], tq), lambda b, i: (b, 0, 0)),
            pl.BlockSpec((None, None, past, kx.shape[3]), lambda b, i: (layer, b, 0, 0)),
            pl.BlockSpec((None, None, vxt.shape[2], past), lambda b, i: (layer, b, 0, 0)),
        ],
        out_specs=pl.BlockSpec((tq, MLA_HEADS * MLA_V_DIM), lambda b, i: (b * qpb + i, 0)),
        out_shape=jax.ShapeDtypeStruct((t, MLA_HEADS * MLA_V_DIM), BF16),
        scratch_shapes=[pltpu.VMEM((2, MLA_HEADS, max(tq, past), tq), F32),
                        pltpu.VMEM((MLA_HEADS, max(tq, past), tq), BF16)],
        compiler_params=_params("parallel", "parallel"),
        name="mla_lat_attention",
    )(qt, km, vt, kx, vxt)


def _ctx_kv_kernel(ckv_ref, kr_ref, wka_ref, wv_ref, vone_ref, place_ref, k_ref, vt_ref):
    ckv = ckv_ref[...].astype(BF16)
    k_ref[...] = (_dot(ckv, wka_ref[...]) + _dot(kr_ref[...].astype(BF16), place_ref[...])).astype(BF16)
    v = _dot(ckv, wv_ref[...]) + vone_ref[...]
    for j in range(v.shape[1] // LANES):
        sl = slice(j * LANES, (j + 1) * LANES)
        vt_ref[sl, :] = jnp.transpose(v[:, sl]).astype(BF16)


def _ctx_kv(cache_ckv, cache_krope, wk_a, wv_ext, vone_ext):
    bd, depth, past, _ = cache_ckv.shape
    place = np.zeros((MLA_ROPE_DIM, MLA_HEADS * MLA_QK_PAD), np.float32)
    for hd in range(MLA_HEADS):
        for i in range(MLA_ROPE_DIM):
            place[i, hd * MLA_QK_PAD + MLA_NOPE_DIM + i] = 1.0
    kw, vw = MLA_HEADS * MLA_QK_PAD, MLA_HEADS * MLA_V_PAD
    return pl.pallas_call(
        _ctx_kv_kernel,
        grid=(depth, bd),
        in_specs=[
            pl.BlockSpec((None, None, past, MLA_KV_LORA), lambda l, b: (b, l, 0, 0)),
            pl.BlockSpec((None, None, past, MLA_ROPE_DIM), lambda l, b: (b, l, 0, 0)),
            pl.BlockSpec((None, MLA_KV_LORA, kw), lambda l, b: (l, 0, 0)),
            pl.BlockSpec((None, MLA_KV_LORA, vw), lambda l, b: (l, 0, 0)),
            pl.BlockSpec((1, vw), lambda l, b: (0, 0)),
            pl.BlockSpec((MLA_ROPE_DIM, kw), lambda l, b: (0, 0)),
        ],
        out_specs=[pl.BlockSpec((None, None, past, kw), lambda l, b: (l, b, 0, 0)),
                   pl.BlockSpec((None, None, vw, past), lambda l, b: (l, b, 0, 0))],
        out_shape=[jax.ShapeDtypeStruct((depth, bd, past, kw), BF16),
                   jax.ShapeDtypeStruct((depth, bd, vw, past), BF16)],
        compiler_params=_params("parallel", "parallel"),
        name="ctx_kv",
    )(cache_ckv, cache_krope, wk_a, wv_ext, vone_ext, jnp.asarray(place, BF16))


def _na_tile_geometry(rows):
    last = rows // NA_Q_ROWS - 1
    geo = []
    for j in (0, 1, last):
        r0 = j * NA_Q_ROWS
        geo.append((r0, min(max(r0 - NA_KH // 2, 0), rows - NA_WIN_ROWS)))
    return geo


def _na_bias_kernel(geo, rows, rpb_ref, o_ref):
    l = pl.program_id(0)
    hd = pl.program_id(1)
    base = (l * NA_HEADS + hd) * (2 * NA_KH - 1) * (2 * NA_KW - 1)
    qc = lax.broadcasted_iota(jnp.int32, (GRID_W, GRID_W), 0)
    kcol = lax.broadcasted_iota(jnp.int32, (GRID_W, GRID_W), 1)
    d_col = jnp.clip(kcol - qc + (NA_KW - 1), 0, 2 * NA_KW - 2)
    col_start = jnp.clip(qc - NA_KW // 2, 0, GRID_W - NA_KW)
    in_cols = (kcol >= col_start) & (kcol < col_start + NA_KW)
    neg = jnp.full((GRID_W, GRID_W), NEG_INF, F32)
    tabs = []
    for dr in range(2 * NA_KH - 1):
        acc = jnp.zeros((GRID_W, GRID_W), F32)
        for dc in range(2 * NA_KW - 1):
            acc = jnp.where(d_col == dc, rpb_ref[base + dr * (2 * NA_KW - 1) + dc], acc)
        tabs.append(jnp.where(in_cols, acc, neg))
    for kind, (r0, ws) in enumerate(geo):
        for i in range(NA_Q_ROWS):
            r = r0 + i
            lo = min(max(r - NA_KH // 2, 0), rows - NA_KH)
            for j in range(NA_WIN_ROWS):
                kr = ws + j
                blk = tabs[kr - r + NA_KH - 1] if lo <= kr < lo + NA_KH else neg
                o_ref[kind, i * GRID_W:(i + 1) * GRID_W, j * GRID_W:(j + 1) * GRID_W] = blk


def _na_bias(na_rpb, rows):
    depth = na_rpb.shape[0]
    geo = _na_tile_geometry(rows)
    qn, kn = NA_Q_ROWS * GRID_W, NA_WIN_ROWS * GRID_W
    return pl.pallas_call(
        functools.partial(_na_bias_kernel, geo, rows),
        grid=(depth, NA_HEADS),
        in_specs=[pl.BlockSpec(memory_space=pltpu.SMEM)],
        out_specs=pl.BlockSpec((None, None, 3, qn, kn), lambda l, h: (l, h, 0, 0, 0)),
        out_shape=jax.ShapeDtypeStruct((depth, NA_HEADS, 3, qn, kn), F32),
        compiler_params=_params("parallel", "parallel"),
        name="na_bias",
    )(na_rpb.reshape(-1))


def _na_lat_kernel(rows, q_ref, k_ref, v_ref, kx_ref, vx_ref, bias_ref, o_ref):
    j = pl.program_id(1)
    ws = jnp.clip(j * NA_Q_ROWS - NA_KH // 2, 0, rows - NA_WIN_ROWS)
    start = pl.multiple_of(ws * GRID_W, GRID_W)
    nk = NA_WIN_ROWS * GRID_W
    for hd in range(NA_HEADS):
        sl = slice(hd * NA_HEAD_DIM, (hd + 1) * NA_HEAD_DIM)
        q = q_ref[:, sl]
        kx = kx_ref[hd].astype(BF16)
        vx = vx_ref[hd].astype(BF16)
        s_win = _nt_dot(q, k_ref[pl.ds(start, nk), sl]) + bias_ref[hd]
        s_ctx = _nt_dot(q, kx)
        m = jnp.maximum(jnp.max(s_win, axis=-1, keepdims=True), jnp.max(s_ctx, axis=-1, keepdims=True))
        p_win = jnp.exp(s_win - m)
        p_ctx = jnp.exp(s_ctx - m)
        l = jnp.sum(p_win, axis=-1, keepdims=True) + jnp.sum(p_ctx, axis=-1, keepdims=True)
        o = _dot(p_win.astype(BF16), v_ref[pl.ds(start, nk), sl]) + _dot(p_ctx.astype(BF16), vx)
        o_ref[:, sl] = (o / l).astype(BF16)


def _na_lat_attention(qn, kn, vn, cache_k, cache_v, bias, layer, n):
    t = qn.shape[0]
    rows = n // GRID_W
    assert rows % NA_Q_ROWS == 0 and rows >= NA_WIN_ROWS + NA_Q_ROWS
    tiles = rows // NA_Q_ROWS
    tq = NA_Q_ROWS * GRID_W
    past = cache_k.shape[3]

    def kind(b, j):
        return (layer, 0, jnp.where(j == 0, 0, jnp.where(j == tiles - 1, 2, 1)), 0, 0)

    return pl.pallas_call(
        functools.partial(_na_lat_kernel, rows),
        grid=(t // n, tiles),
        in_specs=[
            pl.BlockSpec((tq, NA_WIDTH), lambda b, j: (b * tiles + j, 0)),
            pl.BlockSpec((n, NA_WIDTH), lambda b, j: (b, 0)),
            pl.BlockSpec((n, NA_WIDTH), lambda b, j: (b, 0)),
            pl.BlockSpec((None, None, NA_HEADS, past, NA_HEAD_DIM), lambda b, j: (b, layer, 0, 0, 0)),
            pl.BlockSpec((None, None, NA_HEADS, past, NA_HEAD_DIM), lambda b, j: (b, layer, 0, 0, 0)),
            pl.BlockSpec((None, NA_HEADS, None, tq, NA_WIN_ROWS * GRID_W), kind),
        ],
        out_specs=pl.BlockSpec((tq, NA_WIDTH), lambda b, j: (b * tiles + j, 0)),
        out_shape=jax.ShapeDtypeStruct((t, NA_WIDTH), BF16),
        compiler_params=_params("parallel", "parallel"),
        name="na_lat_attention",
    )(qn, kn, vn, cache_k, cache_v, bias)


def _dft_tables(n):
    def thin(j, k, period):
        ang = (2.0 * math.pi / period) * ((j[:, None] * k[None, :]) % period).astype(F32)
        return jnp.cos(ang), jnp.sin(ang)

    k = jnp.arange(n, dtype=jnp.int32)
    scale = float(n) ** -0.5
    if n % 64 == 0 and n > 64:
        n1 = n // 64
        c1, s1 = thin(jnp.arange(n1, dtype=jnp.int32), k, n1)
        c2, s2 = thin(jnp.arange(64, dtype=jnp.int32), k, n)
        c1, s1, c2, s2 = c1[:, None, :], s1[:, None, :], c2[None, :, :], s2[None, :, :]
        cm = (c1 * c2 - s1 * s2).reshape(n, n)
        sm = (s1 * c2 + c1 * s2).reshape(n, n)
    else:
        cm, sm = thin(k, k, n)
    return (cm * scale).astype(BF16), (sm * -scale).astype(BF16)


def _fourier_kernel(c_ref, s_ref, ab_ref, o_ref):
    o_ref[...] = (_dot(c_ref[...], ab_ref[:, :FN_WIDTH]) + _dot(s_ref[...], ab_ref[:, FN_WIDTH:])).astype(BF16)


def _fourier(fab, tables, n, tmf=512):
    t = fab.shape[0]
    tmf = min(tmf, n)
    tiles = n // tmf
    cm, sm = tables
    return pl.pallas_call(
        _fourier_kernel,
        grid=(tiles, t // n),
        in_specs=[
            pl.BlockSpec((tmf, n), lambda i, b: (i, 0)),
            pl.BlockSpec((tmf, n), lambda i, b: (i, 0)),
            pl.BlockSpec((n, 2 * FN_WIDTH), lambda i, b: (b, 0)),
        ],
        out_specs=pl.BlockSpec((tmf, FN_WIDTH), lambda i, b: (b * tiles + i, 0)),
        out_shape=jax.ShapeDtypeStruct((t, FN_WIDTH), BF16),
        compiler_params=_params("parallel", "parallel"),
        name="fourier",
    )(cm, sm, fab)


def _route(s_t, sb_t):
    def top2_sum(v):
        hi1, lo1 = jnp.maximum(v[0], v[1]), jnp.minimum(v[0], v[1])
        hi2, lo2 = jnp.maximum(v[2], v[3]), jnp.minimum(v[2], v[3])
        return jnp.maximum(hi1, hi2) + jnp.maximum(jnp.minimum(hi1, hi2), jnp.maximum(lo1, lo2))

    best = top2_sum(sb_t[0:EXPERTS_PER_GROUP])
    gsel = jnp.zeros_like(best, dtype=jnp.int32)
    for g in range(1, N_EXPERT_GROUPS):
        cand = top2_sum(sb_t[g * EXPERTS_PER_GROUP:(g + 1) * EXPERTS_PER_GROUP])
        better = cand > best
        gsel = jnp.where(better, g, gsel)
        best = jnp.where(better, cand, best)
    chosen = []
    for e in range(N_EXPERTS):
        g = e // EXPERTS_PER_GROUP
        beaten = jnp.zeros_like(gsel)
        for o in range(g * EXPERTS_PER_GROUP, (g + 1) * EXPERTS_PER_GROUP):
            if o == e:
                continue
            ahead = (sb_t[o] > sb_t[e]) | ((sb_t[o] == sb_t[e]) & (o < e))
            beaten = beaten + ahead.astype(jnp.int32)
        chosen.append((gsel == g) & (beaten < 2))
    picked = [jnp.where(chosen[e], s_t[e], 0.0) for e in range(N_EXPERTS)]
    denom = picked[0]
    for e in range(1, N_EXPERTS):
        denom = denom + picked[e]
    return chosen, [pk / denom for pk in picked]


def _pack_pairs(x):
    w = x.shape[1] // 2
    hi = pltpu.bitcast(x[:, :w].astype(BF16).astype(F32), jnp.uint32)
    lo = pltpu.bitcast(x[:, w:].astype(BF16).astype(F32), jnp.uint32)
    return hi | (lo >> 16)


def _unpack_pairs(p):
    hi = pltpu.bitcast(p & jnp.uint32(0xFFFF0000), F32)
    lo = pltpu.bitcast(p << 16, F32)
    return jnp.concatenate([hi, lo], axis=-1)


def _mixout_kernel(n, x_ref, mod_ref, ab_ref, z_ref, zp_ref, zn_ref, yna_ref, ymla_ref, g_ref, cw_ref, wfn_ref,
                   wout_ref, g2_ref, wrh_ref, wrl_ref, br_ref, x1_ref, h2_ref, route_ref, gtok_ref, cnt_ref):
    tm = x_ref.shape[0]
    i = pl.program_id(0)
    mod = mod_ref[...]
    gate1, shift2, scale2 = mod[2:3], mod[3:4], mod[4:5]

    z = z_ref[...].astype(F32)
    ridx = lax.broadcasted_iota(jnp.int32, z.shape, 0)
    at_start = (i * tm) % n == 0
    at_end = ((i + 1) * tm) % n == 0
    prev_row = jnp.where(at_start, 0.0, zp_ref[7:8, :].astype(F32))
    next_row = jnp.where(at_end, 0.0, zn_ref[0:1, :].astype(F32))
    z_m1 = jnp.where(ridx == 0, prev_row, pltpu.roll(z, 1, axis=0))
    z_p1 = jnp.where(ridx == tm - 1, next_row, pltpu.roll(z, tm - 1, axis=0))
    cw = cw_ref[...]
    y_conv = ab_ref[...].astype(F32) * (z_m1 * cw[0:1] + z * cw[1:2] + z_p1 * cw[2:3])

    y_fn = _dot(g_ref[...], wfn_ref[...])
    cat = jnp.concatenate([y_conv.astype(BF16), yna_ref[...], ymla_ref[...], y_fn.astype(BF16)], axis=-1)
    x1 = x_ref[...] + gate1 * _dot(cat, wout_ref[...])
    x1_ref[...] = x1

    h2 = _rms(x1, g2_ref[...]) * (1.0 + scale2) + shift2
    packed = _pack_pairs(h2)
    piece = packed.shape[1] // MOE_PIECES
    for p in range(MOE_PIECES):
        h2_ref[p] = packed[:, p * piece:(p + 1) * piece]
    h2_hi = h2.astype(BF16)
    h2_lo = (h2 - h2_hi.astype(F32)).astype(BF16)
    logits = _dot(h2_hi, wrh_ref[...]) + (_dot(h2_hi, wrl_ref[...]) + _dot(h2_lo, wrh_ref[...]))
    s = jax.nn.sigmoid(logits)
    s_t = jnp.transpose(s)
    sb_t = jnp.transpose(s + br_ref[...])
    chosen, gates = _route([s_t[e:e + 1] for e in range(N_EXPERTS)], [sb_t[e:e + 1] for e in range(N_EXPERTS)])

    @pl.when(i == 0)
    def _():
        cnt_ref[...] = jnp.zeros(cnt_ref.shape, F32)

    chosen_f = jnp.concatenate([ch.astype(F32) for ch in chosen], axis=0)
    before = lax.broadcasted_iota(jnp.int32, (tm, tm), 0) < lax.broadcasted_iota(jnp.int32, (tm, tm), 1)
    prefix = _dot(chosen_f.astype(BF16), jnp.where(before, 1.0, 0.0).astype(BF16))
    base = cnt_ref[...]
    rank = jnp.concatenate([base] * (tm // LANES), axis=1) + prefix
    cnt_ref[...] = base + jnp.sum(chosen_f, axis=1, keepdims=True)

    zero = jnp.zeros((1, tm), F32)
    seen = zero
    slots = [[zero, zero, zero], [zero, zero, zero]]
    for e in range(N_EXPERTS):
        for k in range(2):
            hit = chosen[e] & (seen == float(k))
            for j, val in enumerate((float(e), gates[e], rank[e:e + 1])):
                slots[k][j] = jnp.where(hit, val, slots[k][j])
        seen = seen + chosen_f[e:e + 1]
    (e_lo, g_lo, r_lo), (e_hi, g_hi, r_hi) = slots
    route_ref[...] = jnp.concatenate([g_lo, g_hi, e_lo, e_hi, r_lo, r_hi, zero, zero], axis=0)
    gates_t = jnp.concatenate([g_lo, g_hi, jnp.zeros((LANES - 2, tm), F32)], axis=0)
    gtok_ref[...] = jnp.transpose(gates_t)


def _mixout(x, mod, layer, n, parts, wts, lat, tm):
    t, d = x.shape
    ab, z, yna, ymla, g = parts
    nblk8 = t // 8
    per8 = tm // 8
    cond_row_of_tile = _cond_row(lat, tm, n, mod.shape[1] - 1)
    const2 = lambda i: (0, 0)
    lsel = lambda i: (layer, 0, 0)
    row = lambda w: pl.BlockSpec((tm, w), lambda i: (i, 0))
    in_specs = [
        row(d),
        pl.BlockSpec((None, None, 6, d), lambda i: (layer, cond_row_of_tile(i), 0, 0)),
        row(CONV_WIDTH),
        row(CONV_WIDTH),
        pl.BlockSpec((8, CONV_WIDTH), lambda i: (jnp.maximum(i * per8 - 1, 0), 0)),
        pl.BlockSpec((8, CONV_WIDTH), lambda i: (jnp.minimum((i + 1) * per8, nblk8 - 1), 0)),
        row(NA_WIDTH),
        row(MLA_HEADS * MLA_V_DIM),
        row(FN_WIDTH),
        pl.BlockSpec((None, 3, CONV_WIDTH), lsel),
        pl.BlockSpec((None, FN_WIDTH, FN_WIDTH), lsel),
        pl.BlockSpec((None, d, d), lsel),
        pl.BlockSpec((None, 1, d), lsel),
        pl.BlockSpec((d, LANES), const2),
        pl.BlockSpec((d, LANES), const2),
        pl.BlockSpec((1, LANES), const2),
    ]
    out_specs = [
        row(d),
        pl.BlockSpec((MOE_PIECES, tm, d // 2 // MOE_PIECES), lambda i: (0, i, 0)),
        pl.BlockSpec((8, tm), lambda i: (0, i)),
        row(LANES),
        pl.BlockSpec((N_EXPERTS, LANES), const2),
    ]
    out_shape = [
        jax.ShapeDtypeStruct((t, d), F32),
        jax.ShapeDtypeStruct((MOE_PIECES, t, d // 2 // MOE_PIECES), jnp.uint32),
        jax.ShapeDtypeStruct((8, t), F32),
        jax.ShapeDtypeStruct((t, LANES), F32),
        jax.ShapeDtypeStruct((N_EXPERTS, LANES), F32),
    ]
    return pl.pallas_call(
        functools.partial(_mixout_kernel, n),
        grid=(t // tm,),
        in_specs=in_specs,
        out_specs=out_specs,
        out_shape=out_shape,
        compiler_params=_params("arbitrary"),
        name="mixout",
    )(x, mod, ab, z, z, z, yna, ymla, g, wts["conv_w"], wts["w_fn"], wts["w_out"], wts["norm2"],
      wts["wr_hi"], wts["wr_lo"], wts["b_router"])


def _slot_positions(route, counts, rb):
    cnt = counts[:, 0].astype(jnp.int32)
    padded = (cnt + rb - 1) // rb * rb
    ends = jnp.cumsum(padded)
    offs = ends - padded
    experts = route[2:4].astype(jnp.int32)
    ranks = route[4:6].astype(jnp.int32)
    pos = ranks
    for e in range(N_EXPERTS):
        pos = pos + jnp.where(experts == e, offs[e], 0)
    nblk = (2 * route.shape[1]) // rb + N_EXPERTS
    starts = jnp.arange(nblk, dtype=jnp.int32) * rb
    blk_expert = jnp.sum((starts[:, None] >= ends[None, :]).astype(jnp.int32), axis=1)
    used = blk_expert < N_EXPERTS
    blk_expert = jnp.where(used, blk_expert, 0)
    valid_end = jnp.sum(jnp.where(blk_expert[:, None] == jnp.arange(N_EXPERTS)[None, :], (offs + cnt)[None, :], 0), axis=1)
    blk_valid = jnp.where(used, jnp.clip(valid_end - starts, 0, rb), 0)
    return pos, jnp.stack([blk_expert, blk_valid])


def _sc_mesh():
    return plsc.VectorSubcoreMesh(core_axis_name="c", subcore_axis_name="s")


def _sc_pipeline(body, nwin, in_specs, out_specs):
    return pltpu.emit_pipeline(body, grid=(nwin,), in_specs=in_specs, out_specs=out_specs,
                               core_axis_name=("c", "s"), dimension_semantics=(pltpu.PARALLEL,))


def _row_scatter(table, idx_a, idx_b, nrows):
    b, w = table.shape
    win = SC_WINDOW
    idx_spec = pl.BlockSpec((1, win), lambda i: (0, i))

    @functools.partial(pl.kernel, out_type=jax.ShapeDtypeStruct((nrows, w), table.dtype), mesh=_sc_mesh(),
                       scratch_types=[])
    def scatter(table_hbm, ia_hbm, ib_hbm, out_hbm):
        def body(rows_vmem, ia_vmem, ib_vmem):
            pltpu.sync_copy(rows_vmem, out_hbm.at[ia_vmem.at[0]])
            pltpu.sync_copy(rows_vmem, out_hbm.at[ib_vmem.at[0]])

        _sc_pipeline(body, b // win, [pl.BlockSpec((win, w), lambda i: (i, 0)), idx_spec, idx_spec], [])(
            table_hbm, ia_hbm, ib_hbm)

    return scatter(table, idx_a.reshape(1, b), idx_b.reshape(1, b))


def _row_gather(table, idx):
    b = idx.shape[0]
    w = table.shape[1]
    win = SC_WINDOW

    @functools.partial(pl.kernel, out_type=jax.ShapeDtypeStruct((b, w), table.dtype), mesh=_sc_mesh(),
                       scratch_types=[])
    def gather(table_hbm, idx_hbm, out_hbm):
        def body(idx_vmem, out_vmem):
            pltpu.sync_copy(table_hbm.at[idx_vmem.at[0]], out_vmem)

        _sc_pipeline(body, b // win, [pl.BlockSpec((1, win), lambda i: (0, i))],
                     [pl.BlockSpec((win, w), lambda i: (i, 0))])(idx_hbm, out_hbm)

    return gather(table, idx.reshape(1, b))


def _ffn_kernel(blk_ref, xs_ref, w13_ref, w2_ref, y_ref):
    i = pl.program_id(0)
    e = blk_ref[0, i]
    nvalid = blk_ref[1, i]

    @pl.when(nvalid > 0)
    def _():
        packed = jnp.concatenate([xs_ref[0], xs_ref[1]], axis=-1)
        live = lax.broadcasted_iota(jnp.int32, packed.shape, 0) < nvalid
        xb = _unpack_pairs(jnp.where(live, packed, jnp.uint32(0))).astype(BF16)
        up = _dot(xb, w13_ref[e])
        a, b = up[:, :EXPERT_FF], up[:, EXPERT_FF:]
        hid = (a * jax.nn.sigmoid(a)) * b
        y = _pack_pairs(_dot(hid.astype(BF16), w2_ref[e]))
        half = y.shape[1] // 2
        y_ref[0] = y[:, :half]
        y_ref[1] = y[:, half:]

    @pl.when(nvalid == 0)
    def _():
        y_ref[...] = jnp.zeros(y_ref.shape, y_ref.dtype)


def _expert_ffn(xs, blk, w13, w2, layer, rb):
    pieces, nrows, w = xs.shape
    d = 2 * pieces * w
    resident = dict(pipeline_mode=pl.Buffered(1))
    used = lambda i, blk: (0, jnp.where(blk[1, i] > 0, i, 0), 0)
    return pl.pallas_call(
        _ffn_kernel,
        grid_spec=pltpu.PrefetchScalarGridSpec(
            num_scalar_prefetch=1,
            grid=(nrows // rb,),
            in_specs=[
                pl.BlockSpec((pieces, rb, w), used),
                pl.BlockSpec((None, N_EXPERTS, d, 2 * EXPERT_FF), lambda i, blk: (layer, 0, 0, 0), **resident),
                pl.BlockSpec((None, N_EXPERTS, EXPERT_FF, d), lambda i, blk: (layer, 0, 0, 0), **resident),
            ],
            out_specs=pl.BlockSpec((pieces, rb, w), lambda i, blk: (0, i, 0)),
        ),
        out_shape=jax.ShapeDtypeStruct(xs.shape, xs.dtype),
        compiler_params=_params("parallel"),
        name="expert_ffn",
    )(blk, xs, w13, w2)


def _combine_kernel(final, x1_ref, gtok_ref, mod_ref, nf_ref, y_ref, o_ref):
    g = gtok_ref[...]
    y_lo = _unpack_pairs(jnp.concatenate([y_ref[0, 0], y_ref[1, 0]], axis=-1))
    y_hi = _unpack_pairs(jnp.concatenate([y_ref[0, 1], y_ref[1, 1]], axis=-1))
    out = x1_ref[...] + mod_ref[...][5:6] * (g[:, 0:1] * y_lo + g[:, 1:2] * y_hi)
    if final:
        out = _rms(out, nf_ref[...])
    o_ref[...] = out


def _combine(x1, gtok, y_tok, mod, layer, n, norm_f, final, lat, tc=512):
    t, d = x1.shape
    cond_row_of_tile = _cond_row(lat, tc, n, mod.shape[1] - 1)
    row = lambda w: pl.BlockSpec((tc, w), lambda i: (i, 0))
    return pl.pallas_call(
        functools.partial(_combine_kernel, final),
        grid=(t // tc,),
        in_specs=[
            row(d),
            row(LANES),
            pl.BlockSpec((None, None, 6, d), lambda i: (layer, cond_row_of_tile(i), 0, 0)),
            pl.BlockSpec((1, d), lambda i: (0, 0)),
            pl.BlockSpec(y_tok.shape[:2] + (tc, y_tok.shape[3]), lambda i: (0, 0, i, 0)),
        ],
        out_specs=row(d),
        out_shape=jax.ShapeDtypeStruct((t, d), F32),
        compiler_params=_params("parallel"),
        name="combine",
    )(x1, gtok, mod, norm_f, y_tok)


def _moe_layer(x, mod, layer, n, parts, wts, final, lat, tm):
    t = x.shape[0]
    x1, h2, route, gtok, counts = _mixout(x, mod, layer, n, parts, wts, lat, tm)
    pos, blk = _slot_positions(route, counts, MOE_ROW_BLOCK)
    pieces, _, w = h2.shape
    nrows = blk.shape[1] * MOE_ROW_BLOCK
    piece_base = (jnp.arange(pieces, dtype=jnp.int32) * nrows)[:, None]
    idx = [(piece_base + pos[s][None, :]).reshape(-1) for s in range(2)]
    xs = _row_scatter(h2.reshape(pieces * t, w), idx[0], idx[1], pieces * nrows).reshape(pieces, nrows, w)
    y = _expert_ffn(xs, blk, wts["w13"], wts["w2"], layer, MOE_ROW_BLOCK)
    back = (piece_base[:, :, None] + pos[None, :, :]).reshape(-1)
    y_tok = _row_gather(y.reshape(pieces * nrows, w), back).reshape(pieces, 2, t, w)
    return _combine(x1, gtok, y_tok, mod, layer, n, wts["norm_f"], final, lat)


def _swap_halves(w):
    nf = MLA_ROPE_DIM // 4
    idx = np.arange(MLA_ROPE_DIM).reshape(2, 2, nf)[:, ::-1, :].reshape(-1)
    return w[..., idx]


def _pack_weights(w_in, mla_wq_up, mla_wkv_up, w1, w3, w2, w_router, b_router):
    depth, d, _ = w_in.shape
    zeros = lambda w: jnp.zeros((depth, d, w), w_in.dtype)
    w_kr = w_in[..., 1920:1952]
    pad_rope = lambda w: jnp.concatenate([zeros(MLA_NOPE_DIM), w, zeros(MLA_QK_PAD - MLA_NOPE_DIM - MLA_ROPE_DIM)], -1)
    w_main = jnp.concatenate([w_in[..., :1920], pad_rope(w_kr), pad_rope(_swap_halves(w_kr)), w_in[..., 1952:]], -1)

    wq = mla_wq_up.reshape(depth, MLA_Q_LORA, MLA_HEADS, MLA_NOPE_DIM + MLA_ROPE_DIM)
    q_nope, q_rope = wq[..., :MLA_NOPE_DIM], wq[..., MLA_NOPE_DIM:]
    tail = jnp.zeros(q_rope.shape[:-1] + (MLA_QK_PAD - MLA_NOPE_DIM - MLA_ROPE_DIM,), wq.dtype)
    wq_a = jnp.concatenate([q_nope, q_rope, tail], -1).reshape(depth, MLA_Q_LORA, -1)
    wq_b = jnp.concatenate([jnp.zeros_like(q_nope), _swap_halves(q_rope), tail], -1).reshape(depth, MLA_Q_LORA, -1)

    wkv = mla_wkv_up.reshape(depth, MLA_KV_LORA, MLA_HEADS, MLA_NOPE_DIM + MLA_V_DIM)
    k_nope, v_up = wkv[..., :MLA_NOPE_DIM], wkv[..., MLA_NOPE_DIM:]
    k_tail = jnp.zeros(k_nope.shape[:-1] + (MLA_QK_PAD - MLA_NOPE_DIM,), wkv.dtype)
    wk_a = jnp.concatenate([k_nope, k_tail], -1).reshape(depth, MLA_KV_LORA, -1)
    wv = v_up.reshape(depth, MLA_KV_LORA, -1)
    v_tail = jnp.zeros(v_up.shape[:-1] + (MLA_V_PAD - MLA_V_DIM,), wkv.dtype)
    wv_ext = jnp.concatenate([v_up, v_tail], -1).reshape(depth, MLA_KV_LORA, -1)
    vone = np.zeros((1, MLA_HEADS * MLA_V_PAD), np.float32)
    vone[0, MLA_V_DIM::MLA_V_PAD] = 1.0

    wr = jnp.pad(w_router, ((0, 0), (0, LANES - N_EXPERTS)))
    wr_hi = wr.astype(BF16)
    wr_lo = (wr - wr_hi.astype(F32)).astype(BF16)
    return {
        "w_in": w_main.astype(BF16), "wq_a": wq_a.astype(BF16), "wq_b": wq_b.astype(BF16),
        "wk_a": wk_a.astype(BF16), "wv": wv.astype(BF16), "wv_ext": wv_ext.astype(BF16),
        "vone_ext": jnp.asarray(vone),
        "w13": jnp.concatenate([w1, w3], -1).astype(BF16), "w2": w2.astype(BF16),
        "wr_hi": wr_hi, "wr_lo": wr_lo,
        "b_router": jnp.pad(b_router, (0, LANES - N_EXPERTS)).reshape(1, LANES).astype(F32),
    }


def _channel_dft():
    c = np.arange(FN_GROUP_DIM)
    ang = 2.0 * np.pi * ((c[:, None] * c[None, :]) % FN_GROUP_DIM) / FN_GROUP_DIM
    out = np.zeros((FN_WIDTH, 2 * FN_WIDTH), np.float32)
    for g in range(FN_GROUPS):
        sl = slice(g * FN_GROUP_DIM, (g + 1) * FN_GROUP_DIM)
        out[sl, sl] = np.cos(ang) * FN_GROUP_DIM ** -0.5
        out[sl, FN_WIDTH + g * FN_GROUP_DIM:FN_WIDTH + (g + 1) * FN_GROUP_DIM] = np.sin(ang) * FN_GROUP_DIM ** -0.5
    return jnp.asarray(out, BF16)


def _rope_tables(n):
    tok = jnp.arange(n)
    pos = jnp.stack([tok // GRID_W, tok % GRID_W], axis=-1).astype(F32)
    nf = MLA_ROPE_DIM // 4
    freqs = ROPE_THETA ** (-jnp.arange(nf, dtype=F32) / nf)
    ang = pos[:, :, None] * freqs
    cos = jnp.broadcast_to(jnp.cos(ang)[:, :, None, :], (n, 2, 2, nf)).reshape(n, MLA_ROPE_DIM)
    sin = jnp.sin(ang)
    sin = jnp.stack([-sin, sin], axis=2).reshape(n, MLA_ROPE_DIM)
    pad = jnp.zeros((n, MLA_QK_PAD - MLA_NOPE_DIM - MLA_ROPE_DIM), F32)
    cos_t = jnp.concatenate([jnp.ones((n, MLA_NOPE_DIM), F32), cos, pad], -1)
    sin_t = jnp.concatenate([jnp.zeros((n, MLA_NOPE_DIM), F32), sin, pad], -1)
    return cos_t, sin_t


def kernel(x_prompt, x_sample, cache_na_k, cache_na_v, cache_mla_ckv, cache_mla_krope, c, c_ctx, w_ada, b_ada,
           norm1, norm2, w_in, conv_w, na_rpb, mla_gq, mla_wq_up, mla_gkv, mla_wkv_up, w_fn, w_out, w_router,
           b_router, w1, w3, w2, norm_f):
    bp, seq, d = x_prompt.shape
    bd, dec_seq, _ = x_sample.shape
    depth = w_in.shape[0]

    wts = _pack_weights(w_in, mla_wq_up, mla_wkv_up, w1, w3, w2, w_router, b_router)
    wts.update({
        "norm1": norm1.reshape(depth, 1, d), "norm2": norm2.reshape(depth, 1, d),
        "mla_gq": mla_gq.reshape(depth, 1, -1), "mla_gkv": mla_gkv.reshape(depth, 1, -1),
        "conv_w": conv_w, "w_fn": w_fn.astype(BF16), "w_out": w_out.astype(BF16),
        "norm_f": norm_f.reshape(1, d), "cs_bd": _channel_dft(),
    })

    cond = jnp.concatenate([c, jnp.zeros((-(bd + 1) % 8, d), c.dtype), c_ctx[None, :]], axis=0)
    mod = _ada_modulation(cond, w_ada, b_ada)

    xp = x_prompt.reshape(bp * seq, d)
    tables = _dft_tables(seq)
    caches = []
    for layer in range(depth):
        outs = _premix(xp, mod, layer, seq, wts, None, False, seq)
        ab, z, qn, kn, vn, km, fab, qm, vm = outs[:9]
        caches.append(outs[9:])
        yna, ymla = _ctx_attention(qn, kn, vn, qm, km, vm, seq)
        g = _fourier(fab, tables, seq)
        xp = _moe_layer(xp, mod, layer, seq, (ab, z, yna, ymla, g), wts, layer == depth - 1, False, seq)
    new_na_k, new_na_v, new_ckv, new_krope = (jnp.stack([cl[j] for cl in caches], axis=1) for j in range(4))

    xs = x_sample.reshape(bd * dec_seq, d)
    kx, vxt = _ctx_kv(cache_mla_ckv, cache_mla_krope, wts["wk_a"], wts["wv_ext"], wts["vone_ext"])
    na_bias = _na_bias(na_rpb, dec_seq // GRID_W)
    rope = _rope_tables(dec_seq)
    tables = _dft_tables(dec_seq)
    for layer in range(depth):
        ab, z, qn, kn, vn, km, fab, qt, vt = _premix(xs, mod, layer, dec_seq, wts, rope, True, TM_LAT_PREMIX)
        yna = _na_lat_attention(qn, kn, vn, cache_na_k, cache_na_v, na_bias, layer, dec_seq)
        ymla = _mla_lat_attention(qt, km, vt, kx, vxt, layer, dec_seq)
        g = _fourier(fab, tables, dec_seq)
        xs = _moe_layer(xs, mod, layer, dec_seq, (ab, z, yna, ymla, g), wts, layer == depth - 1, True, TM_LAT_MIXOUT)

    return (xp.reshape(bp, seq, d), xs.reshape(bd, dec_seq, d), new_na_k, new_na_v, new_ckv, new_krope)
```

```python
import functools
import math

import numpy as np
import jax
import jax.numpy as jnp
from jax import lax
from jax.experimental import pallas as pl
from jax.experimental.pallas import tpu as pltpu
from jax.experimental.pallas import tpu_sc as plsc

F32 = jnp.float32
BF16 = jnp.bfloat16

GRID_W = 64
CONV_WIDTH = 256
NA_HEADS = 4
NA_HEAD_DIM = 64
NA_WIDTH = NA_HEADS * NA_HEAD_DIM
NA_KH = 8
NA_KW = 16
MLA_HEADS = 4
MLA_Q_LORA = 256
MLA_KV_LORA = 128
MLA_NOPE_DIM = 64
MLA_ROPE_DIM = 32
MLA_V_DIM = 64
MLA_QK_PAD = 128
MLA_V_PAD = 96
MLA_KEY_SUB = 128
LOG2E = 1.4426950408889634
FN_GROUPS = 4
FN_GROUP_DIM = 64
FN_WIDTH = FN_GROUPS * FN_GROUP_DIM
N_EXPERTS = 16
N_EXPERT_GROUPS = 4
EXPERTS_PER_GROUP = N_EXPERTS // N_EXPERT_GROUPS
EXPERT_FF = 256
ROPE_THETA = 10000.0
EPS = 1e-6
NEG_INF = -1e30
LANES = 128

NA_SCALE = NA_HEAD_DIM ** -0.5
MLA_SCALE = (MLA_NOPE_DIM + MLA_ROPE_DIM) ** -0.5

NA_Q_ROWS = 4
NA_WIN_ROWS = 12

TM_LAT_PREMIX = 512
TM_LAT_MIXOUT = 512
MOE_ROW_BLOCK = 512
SC_WINDOW = 128
MOE_PIECES = 2

VMEM_LIMIT = 56 * 1024 * 1024

_C_AB, _C_AC, _C_AU, _C_Q, _C_K, _C_V, _C_CQ = 0, 256, 512, 768, 1024, 1280, 1536
_C_CKV, _C_KR, _C_KRS, _C_FU, _C_END = 1792, 1920, 2048, 2176, 2432


def _nt_dot(a, b):
    return lax.dot_general(a, b, (((1,), (1,)), ((), ())), preferred_element_type=F32)


def _dot(a, b):
    return jnp.dot(a, b, preferred_element_type=F32)


def _rms(x, g):
    return x * lax.rsqrt(jnp.mean(x * x, axis=-1, keepdims=True) + EPS) * g


def _params(*sem, flags=None):
    return pltpu.CompilerParams(dimension_semantics=sem, vmem_limit_bytes=VMEM_LIMIT, flags=flags)


def _ada_kernel(c_ref, w_ref, b_ref, o_ref):
    cnd = c_ref[...]
    act = cnd * jax.nn.sigmoid(cnd)
    o_ref[...] = _dot(act.astype(BF16), w_ref[...].astype(BF16)) + b_ref[...]


def _ada_modulation(cond, w_ada, b_ada):
    depth, d, six_d = w_ada.shape
    r = cond.shape[0]
    tn = 1024
    out = pl.pallas_call(
        _ada_kernel,
        grid=(depth, six_d // tn),
        in_specs=[
            pl.BlockSpec((r, d), lambda l, j: (0, 0)),
            pl.BlockSpec((None, d, tn), lambda l, j: (l, 0, j)),
            pl.BlockSpec((None, 1, tn), lambda l, j: (l, 0, j)),
        ],
        out_specs=pl.BlockSpec((None, r, tn), lambda l, j: (l, 0, j)),
        out_shape=jax.ShapeDtypeStruct((depth, r, six_d), F32),
        compiler_params=_params("parallel", "parallel"),
        name="ada_modulation",
    )(cond, w_ada, b_ada.reshape(depth, 1, six_d))
    return out.reshape(depth, r, 6, d)


def _premix_kernel(lat, n_in, *refs):
    (x_ref, mod_ref, g1_ref, w_ref, gq_ref, wqa_ref, wqb_ref, gkv_ref, wka_ref, wv_ref, vone_ref, cs_ref,
     cos_ref, sin_ref) = refs[:14]
    outs = refs[n_in:]
    (ab_ref, z_ref, qn_ref, kn_ref, vn_ref, km_ref, fab_ref) = outs[:7]

    x = x_ref[...]
    mod = mod_ref[...]
    h = _rms(x, g1_ref[...]) * (1.0 + mod[1:2]) + mod[0:1]
    p = _dot(h.astype(BF16), w_ref[...])

    ab_ref[...] = p[:, _C_AB:_C_AC].astype(BF16)
    z_ref[...] = (p[:, _C_AC:_C_AU] * p[:, _C_AU:_C_Q]).astype(BF16)
    k_na = p[:, _C_K:_C_V]
    v_na = p[:, _C_V:_C_CQ]
    qn_ref[...] = (p[:, _C_Q:_C_K] * NA_SCALE).astype(BF16)
    kn_ref[...] = k_na.astype(BF16)
    vn_ref[...] = v_na.astype(BF16)

    cqn = _rms(p[:, _C_CQ:_C_CKV], gq_ref[...]).astype(BF16)
    ckvn = _rms(p[:, _C_CKV:_C_KR], gkv_ref[...])
    ckvn_b = ckvn.astype(BF16)
    qa = _dot(cqn, wqa_ref[...])
    kva = _dot(ckvn_b, wka_ref[...])
    v_mla = _dot(ckvn_b, wv_ref[...]) + vone_ref[...]
    kr = p[:, _C_KR:_C_KRS]
    if lat:
        cos = cos_ref[...]
        sin = sin_ref[...]
        qb = _dot(cqn, wqb_ref[...])
        krot = kr * cos + p[:, _C_KRS:_C_FU] * sin
        qt_ref, vt_ref = outs[7:]
    else:
        krot = kr
        qm_ref, vm_ref, ck_ref, cv_ref, cckv_ref, ckr_ref = outs[7:]
    for hd in range(MLA_HEADS):
        sl = slice(hd * MLA_QK_PAD, (hd + 1) * MLA_QK_PAD)
        km_ref[:, sl] = (kva[:, sl] + krot).astype(BF16)
        if lat:
            qh = (qa[:, sl] * cos + qb[:, sl] * sin) * (MLA_SCALE * LOG2E)
            qt_ref[sl, :] = jnp.transpose(qh).astype(BF16)
        else:
            qm_ref[:, sl] = (qa[:, sl] * MLA_SCALE).astype(BF16)

    fab_ref[...] = _dot(p[:, _C_FU:_C_END].astype(BF16), cs_ref[...]).astype(BF16)

    if lat:
        for j in range(v_mla.shape[1] // LANES):
            sl = slice(j * LANES, (j + 1) * LANES)
            vt_ref[sl, :] = jnp.transpose(v_mla[:, sl]).astype(BF16)
    else:
        vm_ref[...] = v_mla.astype(BF16)
        for hd in range(NA_HEADS):
            sl = slice(hd * NA_HEAD_DIM, (hd + 1) * NA_HEAD_DIM)
            ck_ref[hd] = k_na[:, sl]
            cv_ref[hd] = v_na[:, sl]
        cckv_ref[...] = ckvn
        ckr_ref[...] = kr[:, MLA_NOPE_DIM:MLA_NOPE_DIM + MLA_ROPE_DIM]


def _cond_row(lat, tm, n, ctx_row):
    return (lambda i: (i * tm) // n) if lat else (lambda i: ctx_row)


def _premix(x, mod, layer, n, wts, rope, lat, tm, caches=None):
    t, d = x.shape
    if lat:
        cos_t, sin_t = rope
        wv, vone = wts["wv_ext"], wts["vone_ext"]
    else:
        cos_t = sin_t = jnp.zeros((8, LANES), F32)
        wv, vone = wts["wv"], jnp.zeros((1, MLA_HEADS * MLA_V_DIM), F32)
    vw = wv.shape[-1]
    qw = MLA_HEADS * MLA_QK_PAD
    tiles_per_seq = n // tm
    cond_row = _cond_row(lat, tm, n, mod.shape[1] - 1)
    const = lambda *_: (0, 0)
    lsel = lambda *_: (layer, 0, 0)
    rope_spec = (pl.BlockSpec((tm, LANES), lambda i: (i % tiles_per_seq, 0)) if lat
                 else pl.BlockSpec((8, LANES), const))
    in_specs = [
        pl.BlockSpec((tm, d), lambda i: (i, 0)),
        pl.BlockSpec((None, None, 6, d), lambda i: (layer, cond_row(i), 0, 0)),
        pl.BlockSpec((None, 1, d), lsel),
        pl.BlockSpec((None, d, _C_END), lsel),
        pl.BlockSpec((None, 1, MLA_Q_LORA), lsel),
        pl.BlockSpec((None, MLA_Q_LORA, qw), lsel),
        pl.BlockSpec((None, MLA_Q_LORA, qw), lsel),
        pl.BlockSpec((None, 1, MLA_KV_LORA), lsel),
        pl.BlockSpec((None, MLA_KV_LORA, qw), lsel),
        pl.BlockSpec((None, MLA_KV_LORA, vw), lsel),
        pl.BlockSpec((1, vw), const),
        pl.BlockSpec((FN_WIDTH, 2 * FN_WIDTH), const),
        rope_spec,
        rope_spec,
    ]
    row = lambda w: pl.BlockSpec((tm, w), lambda i: (i, 0))
    widths = [CONV_WIDTH, CONV_WIDTH, NA_WIDTH, NA_WIDTH, NA_WIDTH, qw, 2 * FN_WIDTH]
    out_specs = [row(w) for w in widths]
    out_shape = [jax.ShapeDtypeStruct((t, w), BF16) for w in widths]
    if lat:
        out_specs += [pl.BlockSpec((None, qw, tm), lambda i: (i, 0, 0)),
                      pl.BlockSpec((None, vw, tm), lambda i: (i, 0, 0))]
        out_shape += [jax.ShapeDtypeStruct((t // tm, qw, tm), BF16),
                      jax.ShapeDtypeStruct((t // tm, vw, tm), BF16)]
    else:
        assert tm == n
        b = t // n
        depth = wts["w_in"].shape[0]
        out_specs += [
            row(qw), row(vw),
            pl.BlockSpec((None, None, NA_HEADS, n, NA_HEAD_DIM), lambda i: (i, layer, 0, 0, 0)),
            pl.BlockSpec((None, None, NA_HEADS, n, NA_HEAD_DIM), lambda i: (i, layer, 0, 0, 0)),
            pl.BlockSpec((None, None, n, MLA_KV_LORA), lambda i: (i, layer, 0, 0)),
            pl.BlockSpec((None, None, n, MLA_ROPE_DIM), lambda i: (i, layer, 0, 0)),
        ]
        out_shape += [
            jax.ShapeDtypeStruct((t, qw), BF16), jax.ShapeDtypeStruct((t, vw), BF16),
            jax.ShapeDtypeStruct((b, depth, NA_HEADS, n, NA_HEAD_DIM), F32),
            jax.ShapeDtypeStruct((b, depth, NA_HEADS, n, NA_HEAD_DIM), F32),
            jax.ShapeDtypeStruct((b, depth, n, MLA_KV_LORA), F32),
            jax.ShapeDtypeStruct((b, depth, n, MLA_ROPE_DIM), F32),
        ]
    args = [x, mod, wts["norm1"], wts["w_in"], wts["mla_gq"], wts["wq_a"], wts["wq_b"], wts["mla_gkv"],
            wts["wk_a"], wv, vone, wts["cs_bd"], cos_t, sin_t]
    aliases = {}
    if caches is not None:
        first_cache_out = len(out_shape) - len(caches)
        aliases = {len(args) + j: first_cache_out + j for j in range(len(caches))}
        in_specs += [pl.BlockSpec(memory_space=pl.ANY)] * len(caches)
        args += list(caches)
    return pl.pallas_call(
        functools.partial(_premix_kernel, lat, len(args)),
        grid=(t // tm,),
        in_specs=in_specs,
        out_specs=out_specs,
        out_shape=out_shape,
        input_output_aliases=aliases,
        compiler_params=_params("parallel"),
        name="premix_lat" if lat else "premix_ctx",
    )(*args)


def _softmax_attend(q, k, v):
    s = _nt_dot(q, k)
    m = jnp.max(s, axis=-1, keepdims=True)
    p = jnp.exp(s - m)
    l = jnp.sum(p, axis=-1, keepdims=True)
    return _dot(p.astype(BF16), v) / l


def _ctx_attn_kernel(qn_ref, kn_ref, vn_ref, qm_ref, km_ref, vm_ref, yna_ref, ymla_ref):
    for hd in range(NA_HEADS):
        sl = slice(hd * NA_HEAD_DIM, (hd + 1) * NA_HEAD_DIM)
        yna_ref[:, sl] = _softmax_attend(qn_ref[:, sl], kn_ref[:, sl], vn_ref[:, sl]).astype(BF16)
    for hd in range(MLA_HEADS):
        sq = slice(hd * MLA_QK_PAD, (hd + 1) * MLA_QK_PAD)
        sv = slice(hd * MLA_V_DIM, (hd + 1) * MLA_V_DIM)
        ymla_ref[:, sv] = _softmax_attend(qm_ref[:, sq], km_ref[:, sq], vm_ref[:, sv]).astype(BF16)


def _ctx_attention(qn, kn, vn, qm, km, vm, n):
    t = qn.shape[0]
    spec = lambda w: pl.BlockSpec((n, w), lambda b: (b, 0))
    ins = [qn, kn, vn, qm, km, vm]
    return pl.pallas_call(
        _ctx_attn_kernel,
        grid=(t // n,),
        in_specs=[spec(a.shape[1]) for a in ins],
        out_specs=[spec(NA_WIDTH), spec(MLA_HEADS * MLA_V_DIM)],
        out_shape=[jax.ShapeDtypeStruct((t, NA_WIDTH), BF16),
                   jax.ShapeDtypeStruct((t, MLA_HEADS * MLA_V_DIM), BF16)],
        compiler_params=_params("parallel"),
        name="ctx_attention",
    )(*ins)


def _mla_lat_kernel(qt_ref, k_ref, vt_ref, kx_ref, vxt_ref, o_ref, s_scr, p_scr):
    nchunk, _, kc = vt_ref.shape
    tq = qt_ref.shape[1]
    sub = MLA_KEY_SUB

    ksl = lambda hd: slice(hd * MLA_QK_PAD, (hd + 1) * MLA_QK_PAD)
    vsl = lambda hd: slice(hd * MLA_V_PAD, (hd + 1) * MLA_V_PAD)

    def scores(slot, k_of, nk):
        cmax = []
        for hd in range(MLA_HEADS):
            qt = qt_ref[ksl(hd), :]
            part = None
            for j in range(0, nk, sub):
                st = _dot(k_of(hd, j), qt)
                s_scr[slot, hd, j:j + sub, :] = st
                blk = jnp.max(st.reshape(sub // 8, 8, tq), axis=0)
                part = blk if part is None else jnp.maximum(part, blk)
            cmax.append(jnp.max(part, axis=0, keepdims=True))
        return tuple(cmax)

    def attend(slot, cmax, state, vt_of, nk):
        new = []
        for hd in range(MLA_HEADS):
            m_i, acc = state[hd]
            m_new = jnp.maximum(m_i, cmax[hd])
            for j in range(0, nk, sub):
                p_scr[hd, j:j + sub, :] = jnp.exp2(s_scr[slot, hd, j:j + sub, :] - m_new).astype(BF16)
            acc = jnp.exp2(m_i - m_new) * acc + _dot(vt_of(hd), p_scr[hd, 0:nk, :])
            new.append((m_new, acc))
        return tuple(new)

    lat_keys = lambda c: (lambda hd, j: k_ref[pl.ds(pl.multiple_of(c * kc, kc) + j, sub), ksl(hd)])
    past = kx_ref.shape[0]
    state = tuple((jnp.full((1, tq), NEG_INF, F32), jnp.zeros((MLA_V_PAD, tq), F32)) for _ in range(MLA_HEADS))
    cmax = scores(0, lambda hd, j: kx_ref[j:j + sub, ksl(hd)], past)
    state = attend(0, cmax, state, lambda hd: vxt_ref[vsl(hd), :], past)
    cmax = scores(0, lat_keys(0), kc)

    lat_vals = lambda c: (lambda hd: vt_ref[c, vsl(hd), :])

    def body(i, carry):
        cmax0, state = carry
        c = 2 * i
        cmax1 = scores(1, lat_keys(c + 1), kc)
        state = attend(0, cmax0, state, lat_vals(c), kc)
        cmax0 = scores(0, lat_keys(c + 2), kc)
        state = attend(1, cmax1, state, lat_vals(c + 1), kc)
        return cmax0, state

    cmax, state = lax.fori_loop(0, nchunk // 2 - 1, body, (cmax, state))
    cmax1 = scores(1, lat_keys(nchunk - 1), kc)
    state = attend(0, cmax, state, lat_vals(nchunk - 2), kc)
    state = attend(1, cmax1, state, lat_vals(nchunk - 1), kc)
    o_t = jnp.concatenate([acc[:MLA_V_DIM] / acc[MLA_V_DIM:MLA_V_DIM + 1] for _, acc in state], axis=0)
    o_ref[...] = jnp.transpose(o_t).astype(BF16)


def _mla_lat_attention(qt, km, vt, kx, vxt, layer, n):
    ntile, qw, tq = qt.shape
    t = ntile * tq
    past = kx.shape[2]
    qpb = n // tq
    return pl.pallas_call(
        _mla_lat_kernel,
        grid=(t // n, qpb),
        in_specs=[
            pl.BlockSpec((None, qw, tq), lambda b, i: (b * qpb + i, 0, 0)),
            pl.BlockSpec((n, km.shape[1]), lambda b, i: (b, 0)),
            pl.BlockSpec((qpb, vt.shape[1], tq), lambda b, i: (b, 0, 0)),
            pl.BlockSpec((None, None, past, kx.shape[3]), lambda b, i: (layer, b, 0, 0)),
            pl.BlockSpec((None, None, vxt.shape[2], past), lambda b, i: (layer, b, 0, 0)),
        ],
        out_specs=pl.BlockSpec((tq, MLA_HEADS * MLA_V_DIM), lambda b, i: (b * qpb + i, 0)),
        out_shape=jax.ShapeDtypeStruct((t, MLA_HEADS * MLA_V_DIM), BF16),
        scratch_shapes=[pltpu.VMEM((2, MLA_HEADS, max(tq, past), tq), F32),
                        pltpu.VMEM((MLA_HEADS, max(tq, past), tq), BF16)],
        compiler_params=_params("parallel", "parallel"),
        name="mla_lat_attention",
    )(qt, km, vt, kx, vxt)


def _ctx_kv_kernel(ckv_ref, kr_ref, wka_ref, wv_ref, vone_ref, place_ref, k_ref, vt_ref):
    ckv = ckv_ref[...].astype(BF16)
    k_ref[...] = (_dot(ckv, wka_ref[...]) + _dot(kr_ref[...].astype(BF16), place_ref[...])).astype(BF16)
    v = _dot(ckv, wv_ref[...]) + vone_ref[...]
    for j in range(v.shape[1] // LANES):
        sl = slice(j * LANES, (j + 1) * LANES)
        vt_ref[sl, :] = jnp.transpose(v[:, sl]).astype(BF16)


def _ctx_kv(cache_ckv, cache_krope, wk_a, wv_ext, vone_ext):
    bd, depth, past, _ = cache_ckv.shape
    place = np.zeros((MLA_ROPE_DIM, MLA_HEADS * MLA_QK_PAD), np.float32)
    for hd in range(MLA_HEADS):
        for i in range(MLA_ROPE_DIM):
            place[i, hd * MLA_QK_PAD + MLA_NOPE_DIM + i] = 1.0
    kw, vw = MLA_HEADS * MLA_QK_PAD, MLA_HEADS * MLA_V_PAD
    return pl.pallas_call(
        _ctx_kv_kernel,
        grid=(depth, bd),
        in_specs=[
            pl.BlockSpec((None, None, past, MLA_KV_LORA), lambda l, b: (b, l, 0, 0)),
            pl.BlockSpec((None, None, past, MLA_ROPE_DIM), lambda l, b: (b, l, 0, 0)),
            pl.BlockSpec((None, MLA_KV_LORA, kw), lambda l, b: (l, 0, 0)),
            pl.BlockSpec((None, MLA_KV_LORA, vw), lambda l, b: (l, 0, 0)),
            pl.BlockSpec((1, vw), lambda l, b: (0, 0)),
            pl.BlockSpec((MLA_ROPE_DIM, kw), lambda l, b: (0, 0)),
        ],
        out_specs=[pl.BlockSpec((None, None, past, kw), lambda l, b: (l, b, 0, 0)),
                   pl.BlockSpec((None, None, vw, past), lambda l, b: (l, b, 0, 0))],
        out_shape=[jax.ShapeDtypeStruct((depth, bd, past, kw), BF16),
                   jax.ShapeDtypeStruct((depth, bd, vw, past), BF16)],
        compiler_params=_params("parallel", "parallel"),
        name="ctx_kv",
    )(cache_ckv, cache_krope, wk_a, wv_ext, vone_ext, jnp.asarray(place, BF16))


def _na_tile_geometry(rows):
    last = rows // NA_Q_ROWS - 1
    geo = []
    for j in (0, 1, last):
        r0 = j * NA_Q_ROWS
        geo.append((r0, min(max(r0 - NA_KH // 2, 0), rows - NA_WIN_ROWS)))
    return geo


def _na_bias_kernel(geo, rows, rpb_ref, o_ref):
    l = pl.program_id(0)
    hd = pl.program_id(1)
    base = (l * NA_HEADS + hd) * (2 * NA_KH - 1) * (2 * NA_KW - 1)
    qc = lax.broadcasted_iota(jnp.int32, (GRID_W, GRID_W), 0)
    kcol = lax.broadcasted_iota(jnp.int32, (GRID_W, GRID_W), 1)
    d_col = jnp.clip(kcol - qc + (NA_KW - 1), 0, 2 * NA_KW - 2)
    col_start = jnp.clip(qc - NA_KW // 2, 0, GRID_W - NA_KW)
    in_cols = (kcol >= col_start) & (kcol < col_start + NA_KW)
    neg = jnp.full((GRID_W, GRID_W), NEG_INF, F32)
    tabs = []
    for dr in range(2 * NA_KH - 1):
        acc = jnp.zeros((GRID_W, GRID_W), F32)
        for dc in range(2 * NA_KW - 1):
            acc = jnp.where(d_col == dc, rpb_ref[base + dr * (2 * NA_KW - 1) + dc], acc)
        tabs.append(jnp.where(in_cols, acc, neg))
    for kind, (r0, ws) in enumerate(geo):
        for i in range(NA_Q_ROWS):
            r = r0 + i
            lo = min(max(r - NA_KH // 2, 0), rows - NA_KH)
            for j in range(NA_WIN_ROWS):
                kr = ws + j
                blk = tabs[kr - r + NA_KH - 1] if lo <= kr < lo + NA_KH else neg
                o_ref[kind, i * GRID_W:(i + 1) * GRID_W, j * GRID_W:(j + 1) * GRID_W] = blk


def _na_bias(na_rpb, rows):
    depth = na_rpb.shape[0]
    geo = _na_tile_geometry(rows)
    qn, kn = NA_Q_ROWS * GRID_W, NA_WIN_ROWS * GRID_W
    return pl.pallas_call(
        functools.partial(_na_bias_kernel, geo, rows),
        grid=(depth, NA_HEADS),
        in_specs=[pl.BlockSpec(memory_space=pltpu.SMEM)],
        out_specs=pl.BlockSpec((None, None, 3, qn, kn), lambda l, h: (l, h, 0, 0, 0)),
        out_shape=jax.ShapeDtypeStruct((depth, NA_HEADS, 3, qn, kn), F32),
        compiler_params=_params("parallel", "parallel"),
        name="na_bias",
    )(na_rpb.reshape(-1))


def _na_lat_kernel(rows, q_ref, k_ref, v_ref, kx_ref, vx_ref, bias_ref, o_ref):
    j = pl.program_id(1)
    ws = jnp.clip(j * NA_Q_ROWS - NA_KH // 2, 0, rows - NA_WIN_ROWS)
    start = pl.multiple_of(ws * GRID_W, GRID_W)
    nk = NA_WIN_ROWS * GRID_W
    for hd in range(NA_HEADS):
        sl = slice(hd * NA_HEAD_DIM, (hd + 1) * NA_HEAD_DIM)
        q = q_ref[:, sl]
        kx = kx_ref[hd].astype(BF16)
        vx = vx_ref[hd].astype(BF16)
        s_win = _nt_dot(q, k_ref[pl.ds(start, nk), sl]) + bias_ref[hd]
        s_ctx = _nt_dot(q, kx)
        m = jnp.maximum(jnp.max(s_win, axis=-1, keepdims=True), jnp.max(s_ctx, axis=-1, keepdims=True))
        p_win = jnp.exp(s_win - m)
        p_ctx = jnp.exp(s_ctx - m)
        l = jnp.sum(p_win, axis=-1, keepdims=True) + jnp.sum(p_ctx, axis=-1, keepdims=True)
        o = _dot(p_win.astype(BF16), v_ref[pl.ds(start, nk), sl]) + _dot(p_ctx.astype(BF16), vx)
        o_ref[:, sl] = (o / l).astype(BF16)


def _na_lat_attention(qn, kn, vn, cache_k, cache_v, bias, layer, n):
    t = qn.shape[0]
    rows = n // GRID_W
    assert rows % NA_Q_ROWS == 0 and rows >= NA_WIN_ROWS + NA_Q_ROWS
    tiles = rows // NA_Q_ROWS
    tq = NA_Q_ROWS * GRID_W
    past = cache_k.shape[3]

    def kind(b, j):
        return (layer, 0, jnp.where(j == 0, 0, jnp.where(j == tiles - 1, 2, 1)), 0, 0)

    return pl.pallas_call(
        functools.partial(_na_lat_kernel, rows),
        grid=(t // n, tiles),
        in_specs=[
            pl.BlockSpec((tq, NA_WIDTH), lambda b, j: (b * tiles + j, 0)),
            pl.BlockSpec((n, NA_WIDTH), lambda b, j: (b, 0)),
            pl.BlockSpec((n, NA_WIDTH), lambda b, j: (b, 0)),
            pl.BlockSpec((None, None, NA_HEADS, past, NA_HEAD_DIM), lambda b, j: (b, layer, 0, 0, 0)),
            pl.BlockSpec((None, None, NA_HEADS, past, NA_HEAD_DIM), lambda b, j: (b, layer, 0, 0, 0)),
            pl.BlockSpec((None, NA_HEADS, None, tq, NA_WIN_ROWS * GRID_W), kind),
        ],
        out_specs=pl.BlockSpec((tq, NA_WIDTH), lambda b, j: (b * tiles + j, 0)),
        out_shape=jax.ShapeDtypeStruct((t, NA_WIDTH), BF16),
        compiler_params=_params("parallel", "parallel"),
        name="na_lat_attention",
    )(qn, kn, vn, cache_k, cache_v, bias)


def _dft_tables(n):
    def thin(j, k, period):
        ang = (2.0 * math.pi / period) * ((j[:, None] * k[None, :]) % period).astype(F32)
        return jnp.cos(ang), jnp.sin(ang)

    k = jnp.arange(n, dtype=jnp.int32)
    scale = float(n) ** -0.5
    if n % 64 == 0 and n > 64:
        n1 = n // 64
        c1, s1 = thin(jnp.arange(n1, dtype=jnp.int32), k, n1)
        c2, s2 = thin(jnp.arange(64, dtype=jnp.int32), k, n)
        c1, s1, c2, s2 = c1[:, None, :], s1[:, None, :], c2[None, :, :], s2[None, :, :]
        cm = (c1 * c2 - s1 * s2).reshape(n, n)
        sm = (s1 * c2 + c1 * s2).reshape(n, n)
    else:
        cm, sm = thin(k, k, n)
    return (cm * scale).astype(BF16), (sm * -scale).astype(BF16)


def _fourier_kernel(c_ref, s_ref, ab_ref, o_ref):
    o_ref[...] = (_dot(c_ref[...], ab_ref[:, :FN_WIDTH]) + _dot(s_ref[...], ab_ref[:, FN_WIDTH:])).astype(BF16)


def _fourier(fab, tables, n, tmf=512):
    t = fab.shape[0]
    tmf = min(tmf, n)
    tiles = n // tmf
    cm, sm = tables
    return pl.pallas_call(
        _fourier_kernel,
        grid=(tiles, t // n),
        in_specs=[
            pl.BlockSpec((tmf, n), lambda i, b: (i, 0)),
            pl.BlockSpec((tmf, n), lambda i, b: (i, 0)),
            pl.BlockSpec((n, 2 * FN_WIDTH), lambda i, b: (b, 0)),
        ],
        out_specs=pl.BlockSpec((tmf, FN_WIDTH), lambda i, b: (b * tiles + i, 0)),
        out_shape=jax.ShapeDtypeStruct((t, FN_WIDTH), BF16),
        compiler_params=_params("parallel", "parallel"),
        name="fourier",
    )(cm, sm, fab)


def _route(s_t, sb_t):
    def top2_sum(v):
        hi1, lo1 = jnp.maximum(v[0], v[1]), jnp.minimum(v[0], v[1])
        hi2, lo2 = jnp.maximum(v[2], v[3]), jnp.minimum(v[2], v[3])
        return jnp.maximum(hi1, hi2) + jnp.maximum(jnp.minimum(hi1, hi2), jnp.maximum(lo1, lo2))

    best = top2_sum(sb_t[0:EXPERTS_PER_GROUP])
    gsel = jnp.zeros_like(best, dtype=jnp.int32)
    for g in range(1, N_EXPERT_GROUPS):
        cand = top2_sum(sb_t[g * EXPERTS_PER_GROUP:(g + 1) * EXPERTS_PER_GROUP])
        better = cand > best
        gsel = jnp.where(better, g, gsel)
        best = jnp.where(better, cand, best)
    chosen = []
    for e in range(N_EXPERTS):
        g = e // EXPERTS_PER_GROUP
        beaten = jnp.zeros_like(gsel)
        for o in range(g * EXPERTS_PER_GROUP, (g + 1) * EXPERTS_PER_GROUP):
            if o == e:
                continue
            ahead = (sb_t[o] > sb_t[e]) | ((sb_t[o] == sb_t[e]) & (o < e))
            beaten = beaten + ahead.astype(jnp.int32)
        chosen.append((gsel == g) & (beaten < 2))
    picked = [jnp.where(chosen[e], s_t[e], 0.0) for e in range(N_EXPERTS)]
    denom = picked[0]
    for e in range(1, N_EXPERTS):
        denom = denom + picked[e]
    return chosen, [pk / denom for pk in picked]


def _pack_pairs(x):
    w = x.shape[1] // 2
    hi = pltpu.bitcast(x[:, :w].astype(BF16).astype(F32), jnp.uint32)
    lo = pltpu.bitcast(x[:, w:].astype(BF16).astype(F32), jnp.uint32)
    return hi | (lo >> 16)


def _unpack_pairs(p):
    hi = pltpu.bitcast(p & jnp.uint32(0xFFFF0000), F32)
    lo = pltpu.bitcast(p << 16, F32)
    return jnp.concatenate([hi, lo], axis=-1)


def _mixout_kernel(n, x_ref, mod_ref, ab_ref, z_ref, zp_ref, zn_ref, yna_ref, ymla_ref, g_ref, cw_ref, wfn_ref,
                   wout_ref, g2_ref, wrc_ref, br_ref, x1_ref, h2_ref, route_ref, gtok_ref, cnt_ref):
    tm = x_ref.shape[0]
    i = pl.program_id(0)
    mod = mod_ref[...]
    gate1, shift2, scale2 = mod[2:3], mod[3:4], mod[4:5]

    z = z_ref[...].astype(F32)
    ridx = lax.broadcasted_iota(jnp.int32, z.shape, 0)
    at_start = (i * tm) % n == 0
    at_end = ((i + 1) * tm) % n == 0
    prev_row = jnp.where(at_start, 0.0, zp_ref[7:8, :].astype(F32))
    next_row = jnp.where(at_end, 0.0, zn_ref[0:1, :].astype(F32))
    z_m1 = jnp.where(ridx == 0, prev_row, pltpu.roll(z, 1, axis=0))
    z_p1 = jnp.where(ridx == tm - 1, next_row, pltpu.roll(z, tm - 1, axis=0))
    cw = cw_ref[...]
    y_conv = ab_ref[...].astype(F32) * (z_m1 * cw[0:1] + z * cw[1:2] + z_p1 * cw[2:3])

    y_fn = _dot(g_ref[...], wfn_ref[...])
    cat = jnp.concatenate([y_conv.astype(BF16), yna_ref[...], ymla_ref[...], y_fn.astype(BF16)], axis=-1)
    x1 = x_ref[...] + gate1 * _dot(cat, wout_ref[...])
    x1_ref[...] = x1

    h2 = _rms(x1, g2_ref[...]) * (1.0 + scale2) + shift2
    packed = _pack_pairs(h2)
    piece = packed.shape[1] // MOE_PIECES
    for p in range(MOE_PIECES):
        h2_ref[p] = packed[:, p * piece:(p + 1) * piece]
    h2_hi = h2.astype(BF16)
    h2_lo = (h2 - h2_hi.astype(F32)).astype(BF16)
    both = _dot(h2_hi, wrc_ref[...])
    logits = both[:, :LANES] + (both[:, LANES:] + _dot(h2_lo, wrc_ref[:, :LANES]))
    s = jax.nn.sigmoid(logits)
    s_t = jnp.transpose(s)
    sb_t = jnp.transpose(s + br_ref[...])
    chosen, gates = _route([s_t[e:e + 1] for e in range(N_EXPERTS)], [sb_t[e:e + 1] for e in range(N_EXPERTS)])

    @pl.when(i == 0)
    def _():
        cnt_ref[...] = jnp.zeros(cnt_ref.shape, F32)

    chosen_f = jnp.concatenate([ch.astype(F32) for ch in chosen], axis=0)
    before = lax.broadcasted_iota(jnp.int32, (tm, tm), 0) < lax.broadcasted_iota(jnp.int32, (tm, tm), 1)
    prefix = _dot(chosen_f.astype(BF16), jnp.where(before, 1.0, 0.0).astype(BF16))
    base = cnt_ref[...]
    rank = jnp.concatenate([base] * (tm // LANES), axis=1) + prefix
    cnt_ref[...] = base + jnp.sum(chosen_f, axis=1, keepdims=True)

    zero = jnp.zeros((1, tm), F32)
    seen = zero
    slots = [[zero, zero, zero], [zero, zero, zero]]
    for e in range(N_EXPERTS):
        for k in range(2):
            hit = chosen[e] & (seen == float(k))
            for j, val in enumerate((float(e), gates[e], rank[e:e + 1])):
                slots[k][j] = jnp.where(hit, val, slots[k][j])
        seen = seen + chosen_f[e:e + 1]
    (e_lo, g_lo, r_lo), (e_hi, g_hi, r_hi) = slots
    route_ref[...] = jnp.concatenate([g_lo, g_hi, e_lo, e_hi, r_lo, r_hi, zero, zero], axis=0)
    gates_t = jnp.concatenate([g_lo, g_hi, jnp.zeros((LANES - 2, tm), F32)], axis=0)
    gtok_ref[...] = jnp.transpose(gates_t)


def _mixout(x, mod, layer, n, parts, wts, lat, tm):
    t, d = x.shape
    ab, z, yna, ymla, g = parts
    nblk8 = t // 8
    per8 = tm // 8
    cond_row_of_tile = _cond_row(lat, tm, n, mod.shape[1] - 1)
    const2 = lambda i: (0, 0)
    lsel = lambda i: (layer, 0, 0)
    row = lambda w: pl.BlockSpec((tm, w), lambda i: (i, 0))
    in_specs = [
        row(d),
        pl.BlockSpec((None, None, 6, d), lambda i: (layer, cond_row_of_tile(i), 0, 0)),
        row(CONV_WIDTH),
        row(CONV_WIDTH),
        pl.BlockSpec((8, CONV_WIDTH), lambda i: (jnp.maximum(i * per8 - 1, 0), 0)),
        pl.BlockSpec((8, CONV_WIDTH), lambda i: (jnp.minimum((i + 1) * per8, nblk8 - 1), 0)),
        row(NA_WIDTH),
        row(MLA_HEADS * MLA_V_DIM),
        row(FN_WIDTH),
        pl.BlockSpec((None, 3, CONV_WIDTH), lsel),
        pl.BlockSpec((None, FN_WIDTH, FN_WIDTH), lsel),
        pl.BlockSpec((None, d, d), lsel),
        pl.BlockSpec((None, 1, d), lsel),
        pl.BlockSpec((d, 2 * LANES), const2),
        pl.BlockSpec((1, LANES), const2),
    ]
    out_specs = [
        row(d),
        pl.BlockSpec((MOE_PIECES, tm, d // 2 // MOE_PIECES), lambda i: (0, i, 0)),
        pl.BlockSpec((8, tm), lambda i: (0, i)),
        row(LANES),
        pl.BlockSpec((N_EXPERTS, LANES), const2),
    ]
    out_shape = [
        jax.ShapeDtypeStruct((t, d), F32),
        jax.ShapeDtypeStruct((MOE_PIECES, t, d // 2 // MOE_PIECES), jnp.uint32),
        jax.ShapeDtypeStruct((8, t), F32),
        jax.ShapeDtypeStruct((t, LANES), F32),
        jax.ShapeDtypeStruct((N_EXPERTS, LANES), F32),
    ]
    return pl.pallas_call(
        functools.partial(_mixout_kernel, n),
        grid=(t // tm,),
        in_specs=in_specs,
        out_specs=out_specs,
        out_shape=out_shape,
        compiler_params=_params("arbitrary"),
        name="mixout",
    )(x, mod, ab, z, z, z, yna, ymla, g, wts["conv_w"], wts["w_fn"], wts["w_out"], wts["norm2"],
      wts["wr_cat"], wts["b_router"])


def _slot_positions(route, counts, rb):
    cnt = counts[:, 0].astype(jnp.int32)
    padded = (cnt + rb - 1) // rb * rb
    ends = jnp.cumsum(padded)
    offs = ends - padded
    experts = route[2:4].astype(jnp.int32)
    ranks = route[4:6].astype(jnp.int32)
    pos = ranks
    for e in range(N_EXPERTS):
        pos = pos + jnp.where(experts == e, offs[e], 0)
    nblk = (2 * route.shape[1]) // rb + N_EXPERTS
    starts = jnp.arange(nblk, dtype=jnp.int32) * rb
    blk_expert = jnp.sum((starts[:, None] >= ends[None, :]).astype(jnp.int32), axis=1)
    used = blk_expert < N_EXPERTS
    blk_expert = jnp.where(used, blk_expert, 0)
    valid_end = jnp.sum(jnp.where(blk_expert[:, None] == jnp.arange(N_EXPERTS)[None, :], (offs + cnt)[None, :], 0), axis=1)
    blk_valid = jnp.where(used, jnp.clip(valid_end - starts, 0, rb), 0)
    return pos, jnp.stack([blk_expert, blk_valid])


def _sc_mesh():
    return plsc.VectorSubcoreMesh(core_axis_name="c", subcore_axis_name="s")


def _sc_pipeline(body, nwin, in_specs, out_specs):
    return pltpu.emit_pipeline(body, grid=(nwin,), in_specs=in_specs, out_specs=out_specs,
                               core_axis_name=("c", "s"), dimension_semantics=(pltpu.PARALLEL,))


def _row_scatter(table, idx_a, idx_b, nrows):
    b, w = table.shape
    win = SC_WINDOW
    idx_spec = pl.BlockSpec((1, win), lambda i: (0, i))

    @functools.partial(pl.kernel, out_type=jax.ShapeDtypeStruct((nrows, w), table.dtype), mesh=_sc_mesh(),
                       scratch_types=[])
    def scatter(table_hbm, ia_hbm, ib_hbm, out_hbm):
        def body(rows_vmem, ia_vmem, ib_vmem):
            pltpu.sync_copy(rows_vmem, out_hbm.at[ia_vmem.at[0]])
            pltpu.sync_copy(rows_vmem, out_hbm.at[ib_vmem.at[0]])

        _sc_pipeline(body, b // win, [pl.BlockSpec((win, w), lambda i: (i, 0)), idx_spec, idx_spec], [])(
            table_hbm, ia_hbm, ib_hbm)

    return scatter(table, idx_a.reshape(1, b), idx_b.reshape(1, b))


def _row_gather(table, idx):
    b = idx.shape[0]
    w = table.shape[1]
    win = SC_WINDOW

    @functools.partial(pl.kernel, out_type=jax.ShapeDtypeStruct((b, w), table.dtype), mesh=_sc_mesh(),
                       scratch_types=[])
    def gather(table_hbm, idx_hbm, out_hbm):
        def body(idx_vmem, out_vmem):
            pltpu.sync_copy(table_hbm.at[idx_vmem.at[0]], out_vmem)

        _sc_pipeline(body, b // win, [pl.BlockSpec((1, win), lambda i: (0, i))],
                     [pl.BlockSpec((win, w), lambda i: (i, 0))])(idx_hbm, out_hbm)

    return gather(table, idx.reshape(1, b))


def _ffn_kernel(blk_ref, xs_ref, w13_ref, w2_ref, y_ref):
    i = pl.program_id(0)
    e = blk_ref[0, i]
    nvalid = blk_ref[1, i]

    @pl.when(nvalid > 0)
    def _():
        packed = jnp.concatenate([xs_ref[0], xs_ref[1]], axis=-1)
        live = lax.broadcasted_iota(jnp.int32, packed.shape, 0) < nvalid
        xb = _unpack_pairs(jnp.where(live, packed, jnp.uint32(0))).astype(BF16)
        up = _dot(xb, w13_ref[e])
        a, b = up[:, :EXPERT_FF], up[:, EXPERT_FF:]
        hid = (a * jax.nn.sigmoid(a)) * b
        y = _pack_pairs(_dot(hid.astype(BF16), w2_ref[e]))
        half = y.shape[1] // 2
        y_ref[0] = y[:, :half]
        y_ref[1] = y[:, half:]

    @pl.when(nvalid == 0)
    def _():
        y_ref[...] = jnp.zeros(y_ref.shape, y_ref.dtype)


def _expert_ffn(xs, blk, w13, w2, layer, rb):
    pieces, nrows, w = xs.shape
    d = 2 * pieces * w
    resident = dict(pipeline_mode=pl.Buffered(1))
    used = lambda i, blk: (0, jnp.where(blk[1, i] > 0, i, 0), 0)
    return pl.pallas_call(
        _ffn_kernel,
        grid_spec=pltpu.PrefetchScalarGridSpec(
            num_scalar_prefetch=1,
            grid=(nrows // rb,),
            in_specs=[
                pl.BlockSpec((pieces, rb, w), used),
                pl.BlockSpec((None, N_EXPERTS, d, 2 * EXPERT_FF), lambda i, blk: (layer, 0, 0, 0), **resident),
                pl.BlockSpec((None, N_EXPERTS, EXPERT_FF, d), lambda i, blk: (layer, 0, 0, 0), **resident),
            ],
            out_specs=pl.BlockSpec((pieces, rb, w), lambda i, blk: (0, i, 0)),
        ),
        out_shape=jax.ShapeDtypeStruct(xs.shape, xs.dtype),
        compiler_params=_params("parallel"),
        name="expert_ffn",
    )(blk, xs, w13, w2)


def _combine_kernel(final, x1_ref, gtok_ref, mod_ref, nf_ref, y_ref, o_ref):
    g = gtok_ref[...]
    y_lo = _unpack_pairs(jnp.concatenate([y_ref[0, 0], y_ref[1, 0]], axis=-1))
    y_hi = _unpack_pairs(jnp.concatenate([y_ref[0, 1], y_ref[1, 1]], axis=-1))
    out = x1_ref[...] + mod_ref[...][5:6] * (g[:, 0:1] * y_lo + g[:, 1:2] * y_hi)
    if final:
        out = _rms(out, nf_ref[...])
    o_ref[...] = out


def _combine(x1, gtok, y_tok, mod, layer, n, norm_f, final, lat, tc=512):
    t, d = x1.shape
    cond_row_of_tile = _cond_row(lat, tc, n, mod.shape[1] - 1)
    row = lambda w: pl.BlockSpec((tc, w), lambda i: (i, 0))
    return pl.pallas_call(
        functools.partial(_combine_kernel, final),
        grid=(t // tc,),
        in_specs=[
            row(d),
            row(LANES),
            pl.BlockSpec((None, None, 6, d), lambda i: (layer, cond_row_of_tile(i), 0, 0)),
            pl.BlockSpec((1, d), lambda i: (0, 0)),
            pl.BlockSpec(y_tok.shape[:2] + (tc, y_tok.shape[3]), lambda i: (0, 0, i, 0)),
        ],
        out_specs=row(d),
        out_shape=jax.ShapeDtypeStruct((t, d), F32),
        compiler_params=_params("parallel"),
        name="combine",
    )(x1, gtok, mod, norm_f, y_tok)


def _moe_layer(x, mod, layer, n, parts, wts, final, lat, tm):
    t = x.shape[0]
    x1, h2, route, gtok, counts = _mixout(x, mod, layer, n, parts, wts, lat, tm)
    pos, blk = _slot_positions(route, counts, MOE_ROW_BLOCK)
    pieces, _, w = h2.shape
    nrows = blk.shape[1] * MOE_ROW_BLOCK
    piece_base = (jnp.arange(pieces, dtype=jnp.int32) * nrows)[:, None]
    idx = [(piece_base + pos[s][None, :]).reshape(-1) for s in range(2)]
    xs = _row_scatter(h2.reshape(pieces * t, w), idx[0], idx[1], pieces * nrows).reshape(pieces, nrows, w)
    y = _expert_ffn(xs, blk, wts["w13"], wts["w2"], layer, MOE_ROW_BLOCK)
    back = (piece_base[:, :, None] + pos[None, :, :]).reshape(-1)
    y_tok = _row_gather(y.reshape(pieces * nrows, w), back).reshape(pieces, 2, t, w)
    return _combine(x1, gtok, y_tok, mod, layer, n, wts["norm_f"], final, lat)


def _swap_halves(w):
    nf = MLA_ROPE_DIM // 4
    idx = np.arange(MLA_ROPE_DIM).reshape(2, 2, nf)[:, ::-1, :].reshape(-1)
    return w[..., idx]


def _pack_weights(w_in, mla_wq_up, mla_wkv_up, w1, w3, w2, w_router, b_router):
    depth, d, _ = w_in.shape
    zeros = lambda w: jnp.zeros((depth, d, w), w_in.dtype)
    w_kr = w_in[..., 1920:1952]
    pad_rope = lambda w: jnp.concatenate([zeros(MLA_NOPE_DIM), w, zeros(MLA_QK_PAD - MLA_NOPE_DIM - MLA_ROPE_DIM)], -1)
    w_main = jnp.concatenate([w_in[..., :1920], pad_rope(w_kr), pad_rope(_swap_halves(w_kr)), w_in[..., 1952:]], -1)

    wq = mla_wq_up.reshape(depth, MLA_Q_LORA, MLA_HEADS, MLA_NOPE_DIM + MLA_ROPE_DIM)
    q_nope, q_rope = wq[..., :MLA_NOPE_DIM], wq[..., MLA_NOPE_DIM:]
    tail = jnp.zeros(q_rope.shape[:-1] + (MLA_QK_PAD - MLA_NOPE_DIM - MLA_ROPE_DIM,), wq.dtype)
    wq_a = jnp.concatenate([q_nope, q_rope, tail], -1).reshape(depth, MLA_Q_LORA, -1)
    wq_b = jnp.concatenate([jnp.zeros_like(q_nope), _swap_halves(q_rope), tail], -1).reshape(depth, MLA_Q_LORA, -1)

    wkv = mla_wkv_up.reshape(depth, MLA_KV_LORA, MLA_HEADS, MLA_NOPE_DIM + MLA_V_DIM)
    k_nope, v_up = wkv[..., :MLA_NOPE_DIM], wkv[..., MLA_NOPE_DIM:]
    k_tail = jnp.zeros(k_nope.shape[:-1] + (MLA_QK_PAD - MLA_NOPE_DIM,), wkv.dtype)
    wk_a = jnp.concatenate([k_nope, k_tail], -1).reshape(depth, MLA_KV_LORA, -1)
    wv = v_up.reshape(depth, MLA_KV_LORA, -1)
    v_tail = jnp.zeros(v_up.shape[:-1] + (MLA_V_PAD - MLA_V_DIM,), wkv.dtype)
    wv_ext = jnp.concatenate([v_up, v_tail], -1).reshape(depth, MLA_KV_LORA, -1)
    vone = np.zeros((1, MLA_HEADS * MLA_V_PAD), np.float32)
    vone[0, MLA_V_DIM::MLA_V_PAD] = 1.0

    wr = jnp.pad(w_router, ((0, 0), (0, LANES - N_EXPERTS)))
    wr_hi = wr.astype(BF16)
    wr_lo = (wr - wr_hi.astype(F32)).astype(BF16)
    return {
        "w_in": w_main.astype(BF16), "wq_a": wq_a.astype(BF16), "wq_b": wq_b.astype(BF16),
        "wk_a": wk_a.astype(BF16), "wv": wv.astype(BF16), "wv_ext": wv_ext.astype(BF16),
        "vone_ext": jnp.asarray(vone),
        "w13": jnp.concatenate([w1, w3], -1).astype(BF16), "w2": w2.astype(BF16),
        "wr_cat": jnp.concatenate([wr_hi, wr_lo], axis=-1),
        "b_router": jnp.pad(b_router, (0, LANES - N_EXPERTS)).reshape(1, LANES).astype(F32),
    }


def _channel_dft():
    c = np.arange(FN_GROUP_DIM)
    ang = 2.0 * np.pi * ((c[:, None] * c[None, :]) % FN_GROUP_DIM) / FN_GROUP_DIM
    out = np.zeros((FN_WIDTH, 2 * FN_WIDTH), np.float32)
    for g in range(FN_GROUPS):
        sl = slice(g * FN_GROUP_DIM, (g + 1) * FN_GROUP_DIM)
        out[sl, sl] = np.cos(ang) * FN_GROUP_DIM ** -0.5
        out[sl, FN_WIDTH + g * FN_GROUP_DIM:FN_WIDTH + (g + 1) * FN_GROUP_DIM] = np.sin(ang) * FN_GROUP_DIM ** -0.5
    return jnp.asarray(out, BF16)


def _rope_tables(n):
    tok = jnp.arange(n)
    pos = jnp.stack([tok // GRID_W, tok % GRID_W], axis=-1).astype(F32)
    nf = MLA_ROPE_DIM // 4
    freqs = ROPE_THETA ** (-jnp.arange(nf, dtype=F32) / nf)
    ang = pos[:, :, None] * freqs
    cos = jnp.broadcast_to(jnp.cos(ang)[:, :, None, :], (n, 2, 2, nf)).reshape(n, MLA_ROPE_DIM)
    sin = jnp.sin(ang)
    sin = jnp.stack([-sin, sin], axis=2).reshape(n, MLA_ROPE_DIM)
    pad = jnp.zeros((n, MLA_QK_PAD - MLA_NOPE_DIM - MLA_ROPE_DIM), F32)
    cos_t = jnp.concatenate([jnp.ones((n, MLA_NOPE_DIM), F32), cos, pad], -1)
    sin_t = jnp.concatenate([jnp.zeros((n, MLA_NOPE_DIM), F32), sin, pad], -1)
    return cos_t, sin_t


def kernel(x_prompt, x_sample, cache_na_k, cache_na_v, cache_mla_ckv, cache_mla_krope, c, c_ctx, w_ada, b_ada,
           norm1, norm2, w_in, conv_w, na_rpb, mla_gq, mla_wq_up, mla_gkv, mla_wkv_up, w_fn, w_out, w_router,
           b_router, w1, w3, w2, norm_f):
    bp, seq, d = x_prompt.shape
    bd, dec_seq, _ = x_sample.shape
    depth = w_in.shape[0]

    wts = _pack_weights(w_in, mla_wq_up, mla_wkv_up, w1, w3, w2, w_router, b_router)
    wts.update({
        "norm1": norm1.reshape(depth, 1, d), "norm2": norm2.reshape(depth, 1, d),
        "mla_gq": mla_gq.reshape(depth, 1, -1), "mla_gkv": mla_gkv.reshape(depth, 1, -1),
        "conv_w": conv_w, "w_fn": w_fn.astype(BF16), "w_out": w_out.astype(BF16),
        "norm_f": norm_f.reshape(1, d), "cs_bd": _channel_dft(),
    })

    cond = jnp.concatenate([c, jnp.zeros((-(bd + 1) % 8, d), c.dtype), c_ctx[None, :]], axis=0)
    mod = _ada_modulation(cond, w_ada, b_ada)

    xp = x_prompt.reshape(bp * seq, d)
    tables = _dft_tables(seq)
    caches = None
    for layer in range(depth):
        outs = _premix(xp, mod, layer, seq, wts, None, False, seq, caches)
        ab, z, qn, kn, vn, km, fab, qm, vm = outs[:9]
        caches = outs[9:]
        yna, ymla = _ctx_attention(qn, kn, vn, qm, km, vm, seq)
        g = _fourier(fab, tables, seq)
        xp = _moe_layer(xp, mod, layer, seq, (ab, z, yna, ymla, g), wts, layer == depth - 1, False, seq)
    new_na_k, new_na_v, new_ckv, new_krope = caches

    xs = x_sample.reshape(bd * dec_seq, d)
    kx, vxt = _ctx_kv(cache_mla_ckv, cache_mla_krope, wts["wk_a"], wts["wv_ext"], wts["vone_ext"])
    na_bias = _na_bias(na_rpb, dec_seq // GRID_W)
    rope = _rope_tables(dec_seq)
    tables = _dft_tables(dec_seq)
    for layer in range(depth):
        ab, z, qn, kn, vn, km, fab, qt, vt = _premix(xs, mod, layer, dec_seq, wts, rope, True, TM_LAT_PREMIX)
        yna = _na_lat_attention(qn, kn, vn, cache_na_k, cache_na_v, na_bias, layer, dec_seq)
        ymla = _mla_lat_attention(qt, km, vt, kx, vxt, layer, dec_seq)
        g = _fourier(fab, tables, dec_seq)
        xs = _moe_layer(xs, mod, layer, dec_seq, (ab, z, yna, ymla, g), wts, layer == depth - 1, True, TM_LAT_MIXOUT)

    return (xp.reshape(bp, seq, d), xs.reshape(bd, dec_seq, d), new_na_k, new_na_v, new_ckv, new_krope)
```

```python
import functools
import math

import numpy as np
import jax
import jax.numpy as jnp
from jax import lax
from jax.experimental import pallas as pl
from jax.experimental.pallas import tpu as pltpu
from jax.experimental.pallas import tpu_sc as plsc

F32 = jnp.float32
BF16 = jnp.bfloat16

GRID_W = 64
CONV_WIDTH = 256
NA_HEADS = 4
NA_HEAD_DIM = 64
NA_WIDTH = NA_HEADS * NA_HEAD_DIM
NA_KH = 8
NA_KW = 16
MLA_HEADS = 4
MLA_Q_LORA = 256
MLA_KV_LORA = 128
MLA_NOPE_DIM = 64
MLA_ROPE_DIM = 32
MLA_V_DIM = 64
MLA_QK_PAD = 128
MLA_V_PAD = 96
MLA_KEY_SUB = 128
LOG2E = 1.4426950408889634
FN_GROUPS = 4
FN_GROUP_DIM = 64
FN_WIDTH = FN_GROUPS * FN_GROUP_DIM
N_EXPERTS = 16
N_EXPERT_GROUPS = 4
EXPERTS_PER_GROUP = N_EXPERTS // N_EXPERT_GROUPS
EXPERT_FF = 256
ROPE_THETA = 10000.0
EPS = 1e-6
NEG_INF = -1e30
LANES = 128

NA_SCALE = NA_HEAD_DIM ** -0.5
MLA_SCALE = (MLA_NOPE_DIM + MLA_ROPE_DIM) ** -0.5

NA_Q_ROWS = 4
NA_WIN_ROWS = 12

TM_LAT_PREMIX = 512
TM_LAT_MIXOUT = 512
MOE_ROW_BLOCK_CTX = 256
MOE_ROW_BLOCK = 512
SC_WINDOW = 128
MOE_PIECES = 2

VMEM_LIMIT = 56 * 1024 * 1024

_C_AB, _C_AC, _C_AU, _C_Q, _C_K, _C_V, _C_CQ = 0, 256, 512, 768, 1024, 1280, 1536
_C_CKV, _C_KR, _C_KRS, _C_FU, _C_END = 1792, 1920, 2048, 2176, 2432


def _nt_dot(a, b):
    return lax.dot_general(a, b, (((1,), (1,)), ((), ())), preferred_element_type=F32)


def _dot(a, b):
    return jnp.dot(a, b, preferred_element_type=F32)


def _rms(x, g):
    return x * lax.rsqrt(jnp.mean(x * x, axis=-1, keepdims=True) + EPS) * g


def _params(*sem, flags=None):
    return pltpu.CompilerParams(dimension_semantics=sem, vmem_limit_bytes=VMEM_LIMIT, flags=flags)


def _ada_kernel(c_ref, w_ref, b_ref, o_ref):
    cnd = c_ref[...]
    act = cnd * jax.nn.sigmoid(cnd)
    o_ref[...] = _dot(act.astype(BF16), w_ref[...].astype(BF16)) + b_ref[...]


def _ada_modulation(cond, w_ada, b_ada):
    depth, d, six_d = w_ada.shape
    r = cond.shape[0]
    tn = 1024
    out = pl.pallas_call(
        _ada_kernel,
        grid=(depth, six_d // tn),
        in_specs=[
            pl.BlockSpec((r, d), lambda l, j: (0, 0)),
            pl.BlockSpec((None, d, tn), lambda l, j: (l, 0, j)),
            pl.BlockSpec((None, 1, tn), lambda l, j: (l, 0, j)),
        ],
        out_specs=pl.BlockSpec((None, r, tn), lambda l, j: (l, 0, j)),
        out_shape=jax.ShapeDtypeStruct((depth, r, six_d), F32),
        compiler_params=_params("parallel", "parallel"),
        name="ada_modulation",
    )(cond, w_ada, b_ada.reshape(depth, 1, six_d))
    return out.reshape(depth, r, 6, d)


def _unpack_pairs(p):
    hi = pltpu.bitcast(p & jnp.uint32(0xFFFF0000), F32)
    lo = pltpu.bitcast(p << 16, F32)
    return jnp.concatenate([hi, lo], axis=-1)


def _moe_residual(x1, gtok_ref, y_ref, mod_ref):
    g = gtok_ref[...]
    y_lo = _unpack_pairs(jnp.concatenate([y_ref[0, 0], y_ref[1, 0]], axis=-1))
    y_hi = _unpack_pairs(jnp.concatenate([y_ref[0, 1], y_ref[1, 1]], axis=-1))
    return x1 + mod_ref[...][5:6] * (g[:, 0:1] * y_lo + g[:, 1:2] * y_hi)


def _premix_kernel(lat, fused, n_in, *refs):
    (x_ref, mod_ref, g1_ref, w_ref, gq_ref, wqa_ref, wqb_ref, gkv_ref, wka_ref, wv_ref, vone_ref, cs_ref,
     cos_ref, sin_ref) = refs[:14]
    outs = refs[n_in:]
    (ab_ref, z_ref, qn_ref, kn_ref, vn_ref, km_ref, fab_ref) = outs[:7]

    x = x_ref[...]
    if fused:
        gtok_ref, y_ref, modp_ref = refs[n_in - 3:n_in]
        x = _moe_residual(x, gtok_ref, y_ref, modp_ref)
        outs[-1][...] = x
    mod = mod_ref[...]
    h = _rms(x, g1_ref[...]) * (1.0 + mod[1:2]) + mod[0:1]
    p = _dot(h.astype(BF16), w_ref[...])

    ab_ref[...] = p[:, _C_AB:_C_AC].astype(BF16)
    z_ref[...] = (p[:, _C_AC:_C_AU] * p[:, _C_AU:_C_Q]).astype(BF16)
    k_na = p[:, _C_K:_C_V]
    v_na = p[:, _C_V:_C_CQ]
    qn_ref[...] = (p[:, _C_Q:_C_K] * NA_SCALE).astype(BF16)
    kn_ref[...] = k_na.astype(BF16)
    vn_ref[...] = v_na.astype(BF16)

    cqn = _rms(p[:, _C_CQ:_C_CKV], gq_ref[...]).astype(BF16)
    ckvn = _rms(p[:, _C_CKV:_C_KR], gkv_ref[...])
    ckvn_b = ckvn.astype(BF16)
    qa = _dot(cqn, wqa_ref[...])
    kva = _dot(ckvn_b, wka_ref[...])
    v_mla = _dot(ckvn_b, wv_ref[...]) + vone_ref[...]
    kr = p[:, _C_KR:_C_KRS]
    if lat:
        cos = cos_ref[...]
        sin = sin_ref[...]
        qb = _dot(cqn, wqb_ref[...])
        krot = kr * cos + p[:, _C_KRS:_C_FU] * sin
        qt_ref, vt_ref = outs[7:9]
    else:
        krot = kr
        qm_ref, vm_ref, ck_ref, cv_ref, cckv_ref, ckr_ref = outs[7:13]
    for hd in range(MLA_HEADS):
        sl = slice(hd * MLA_QK_PAD, (hd + 1) * MLA_QK_PAD)
        km_ref[:, sl] = (kva[:, sl] + krot).astype(BF16)
        if lat:
            qh = (qa[:, sl] * cos + qb[:, sl] * sin) * (MLA_SCALE * LOG2E)
            qt_ref[sl, :] = jnp.transpose(qh).astype(BF16)
        else:
            qm_ref[:, sl] = (qa[:, sl] * MLA_SCALE).astype(BF16)

    fab_ref[...] = _dot(p[:, _C_FU:_C_END].astype(BF16), cs_ref[...]).astype(BF16)

    if lat:
        for j in range(v_mla.shape[1] // LANES):
            sl = slice(j * LANES, (j + 1) * LANES)
            vt_ref[sl, :] = jnp.transpose(v_mla[:, sl]).astype(BF16)
    else:
        vm_ref[...] = v_mla.astype(BF16)
        for hd in range(NA_HEADS):
            sl = slice(hd * NA_HEAD_DIM, (hd + 1) * NA_HEAD_DIM)
            ck_ref[hd] = k_na[:, sl]
            cv_ref[hd] = v_na[:, sl]
        cckv_ref[...] = ckvn
        ckr_ref[...] = kr[:, MLA_NOPE_DIM:MLA_NOPE_DIM + MLA_ROPE_DIM]


def _cond_row(lat, tm, n, ctx_row):
    return (lambda i: (i * tm) // n) if lat else (lambda i: ctx_row)


def _premix(x, mod, layer, n, wts, rope, lat, tm, caches=None, pending=None):
    t, d = x.shape
    if lat:
        cos_t, sin_t = rope
        wv, vone = wts["wv_ext"], wts["vone_ext"]
    else:
        cos_t = sin_t = jnp.zeros((8, LANES), F32)
        wv, vone = wts["wv"], jnp.zeros((1, MLA_HEADS * MLA_V_DIM), F32)
    vw = wv.shape[-1]
    qw = MLA_HEADS * MLA_QK_PAD
    tiles_per_seq = n // tm
    cond_row = _cond_row(lat, tm, n, mod.shape[1] - 1)
    const = lambda *_: (0, 0)
    lsel = lambda *_: (layer, 0, 0)
    rope_spec = (pl.BlockSpec((tm, LANES), lambda i: (i % tiles_per_seq, 0)) if lat
                 else pl.BlockSpec((8, LANES), const))
    in_specs = [
        pl.BlockSpec((tm, d), lambda i: (i, 0)),
        pl.BlockSpec((None, None, 6, d), lambda i: (layer, cond_row(i), 0, 0)),
        pl.BlockSpec((None, 1, d), lsel),
        pl.BlockSpec((None, d, _C_END), lsel),
        pl.BlockSpec((None, 1, MLA_Q_LORA), lsel),
        pl.BlockSpec((None, MLA_Q_LORA, qw), lsel),
        pl.BlockSpec((None, MLA_Q_LORA, qw), lsel),
        pl.BlockSpec((None, 1, MLA_KV_LORA), lsel),
        pl.BlockSpec((None, MLA_KV_LORA, qw), lsel),
        pl.BlockSpec((None, MLA_KV_LORA, vw), lsel),
        pl.BlockSpec((1, vw), const),
        pl.BlockSpec((FN_WIDTH, 2 * FN_WIDTH), const),
        rope_spec,
        rope_spec,
    ]
    row = lambda w: pl.BlockSpec((tm, w), lambda i: (i, 0))
    widths = [CONV_WIDTH, CONV_WIDTH, NA_WIDTH, NA_WIDTH, NA_WIDTH, qw, 2 * FN_WIDTH]
    out_specs = [row(w) for w in widths]
    out_shape = [jax.ShapeDtypeStruct((t, w), BF16) for w in widths]
    if lat:
        out_specs += [pl.BlockSpec((None, qw, tm), lambda i: (i, 0, 0)),
                      pl.BlockSpec((None, vw, tm), lambda i: (i, 0, 0))]
        out_shape += [jax.ShapeDtypeStruct((t // tm, qw, tm), BF16),
                      jax.ShapeDtypeStruct((t // tm, vw, tm), BF16)]
    else:
        assert tm == n
        b = t // n
        depth = wts["w_in"].shape[0]
        out_specs += [
            row(qw), row(vw),
            pl.BlockSpec((None, None, NA_HEADS, n, NA_HEAD_DIM), lambda i: (i, layer, 0, 0, 0)),
            pl.BlockSpec((None, None, NA_HEADS, n, NA_HEAD_DIM), lambda i: (i, layer, 0, 0, 0)),
            pl.BlockSpec((None, None, n, MLA_KV_LORA), lambda i: (i, layer, 0, 0)),
            pl.BlockSpec((None, None, n, MLA_ROPE_DIM), lambda i: (i, layer, 0, 0)),
        ]
        out_shape += [
            jax.ShapeDtypeStruct((t, qw), BF16), jax.ShapeDtypeStruct((t, vw), BF16),
            jax.ShapeDtypeStruct((b, depth, NA_HEADS, n, NA_HEAD_DIM), F32),
            jax.ShapeDtypeStruct((b, depth, NA_HEADS, n, NA_HEAD_DIM), F32),
            jax.ShapeDtypeStruct((b, depth, n, MLA_KV_LORA), F32),
            jax.ShapeDtypeStruct((b, depth, n, MLA_ROPE_DIM), F32),
        ]
    args = [x, mod, wts["norm1"], wts["w_in"], wts["mla_gq"], wts["wq_a"], wts["wq_b"], wts["mla_gkv"],
            wts["wk_a"], wv, vone, wts["cs_bd"], cos_t, sin_t]
    aliases = {}
    if caches is not None:
        first_cache_out = len(out_shape) - len(caches)
        aliases = {len(args) + j: first_cache_out + j for j in range(len(caches))}
        in_specs += [pl.BlockSpec(memory_space=pl.ANY)] * len(caches)
        args += list(caches)
    if pending is not None:
        gtok, y_tok = pending
        in_specs += [
            row(LANES),
            pl.BlockSpec(y_tok.shape[:2] + (tm, y_tok.shape[3]), lambda i: (0, 0, i, 0)),
            pl.BlockSpec((None, None, 6, d), lambda i: (layer - 1, cond_row(i), 0, 0)),
        ]
        args += [gtok, y_tok, mod]
        out_specs = out_specs + [row(d)]
        out_shape = out_shape + [jax.ShapeDtypeStruct((t, d), F32)]
    return pl.pallas_call(
        functools.partial(_premix_kernel, lat, pending is not None, len(args)),
        grid=(t // tm,),
        in_specs=in_specs,
        out_specs=out_specs,
        out_shape=out_shape,
        input_output_aliases=aliases,
        compiler_params=_params("parallel"),
        name="premix_lat" if lat else "premix_ctx",
    )(*args)


def _softmax_attend(q, k, v):
    s = _nt_dot(q, k)
    m = jnp.max(s, axis=-1, keepdims=True)
    p = jnp.exp(s - m)
    l = jnp.sum(p, axis=-1, keepdims=True)
    return _dot(p.astype(BF16), v) / l


def _ctx_attn_kernel(qn_ref, kn_ref, vn_ref, qm_ref, km_ref, vm_ref, yna_ref, ymla_ref):
    for hd in range(NA_HEADS):
        sl = slice(hd * NA_HEAD_DIM, (hd + 1) * NA_HEAD_DIM)
        yna_ref[:, sl] = _softmax_attend(qn_ref[:, sl], kn_ref[:, sl], vn_ref[:, sl]).astype(BF16)
    for hd in range(MLA_HEADS):
        sq = slice(hd * MLA_QK_PAD, (hd + 1) * MLA_QK_PAD)
        sv = slice(hd * MLA_V_DIM, (hd + 1) * MLA_V_DIM)
        ymla_ref[:, sv] = _softmax_attend(qm_ref[:, sq], km_ref[:, sq], vm_ref[:, sv]).astype(BF16)


def _ctx_attention(qn, kn, vn, qm, km, vm, n):
    t = qn.shape[0]
    spec = lambda w: pl.BlockSpec((n, w), lambda b: (b, 0))
    ins = [qn, kn, vn, qm, km, vm]
    return pl.pallas_call(
        _ctx_attn_kernel,
        grid=(t // n,),
        in_specs=[spec(a.shape[1]) for a in ins],
        out_specs=[spec(NA_WIDTH), spec(MLA_HEADS * MLA_V_DIM)],
        out_shape=[jax.ShapeDtypeStruct((t, NA_WIDTH), BF16),
                   jax.ShapeDtypeStruct((t, MLA_HEADS * MLA_V_DIM), BF16)],
        compiler_params=_params("parallel"),
        name="ctx_attention",
    )(*ins)


def _mla_lat_kernel(qt_ref, k_ref, vt_ref, kx_ref, vxt_ref, o_ref, s_scr, p_scr):
    nchunk, _, kc = vt_ref.shape
    tq = qt_ref.shape[1]
    sub = MLA_KEY_SUB

    ksl = lambda hd: slice(hd * MLA_QK_PAD, (hd + 1) * MLA_QK_PAD)
    vsl = lambda hd: slice(hd * MLA_V_PAD, (hd + 1) * MLA_V_PAD)

    def scores(slot, k_of, nk):
        cmax = []
        for hd in range(MLA_HEADS):
            qt = qt_ref[ksl(hd), :]
            part = None
            for j in range(0, nk, sub):
                st = _dot(k_of(hd, j), qt)
                s_scr[slot, hd, j:j + sub, :] = st
                blk = jnp.max(st.reshape(sub // 8, 8, tq), axis=0)
                part = blk if part is None else jnp.maximum(part, blk)
            cmax.append(jnp.max(part, axis=0, keepdims=True))
        return tuple(cmax)

    def attend(slot, cmax, state, vt_of, nk):
        new = []
        for hd in range(MLA_HEADS):
            m_i, acc = state[hd]
            m_new = jnp.maximum(m_i, cmax[hd])
            for j in range(0, nk, sub):
                p_scr[hd, j:j + sub, :] = jnp.exp2(s_scr[slot, hd, j:j + sub, :] - m_new).astype(BF16)
            acc = jnp.exp2(m_i - m_new) * acc + _dot(vt_of(hd), p_scr[hd, 0:nk, :])
            new.append((m_new, acc))
        return tuple(new)

    lat_keys = lambda c: (lambda hd, j: k_ref[pl.ds(pl.multiple_of(c * kc, kc) + j, sub), ksl(hd)])
    past = kx_ref.shape[0]
    state = tuple((jnp.full((1, tq), NEG_INF, F32), jnp.zeros((MLA_V_PAD, tq), F32)) for _ in range(MLA_HEADS))
    cmax_ctx = scores(1, lambda hd, j: kx_ref[j:j + sub, ksl(hd)], past)
    cmax = scores(0, lat_keys(0), kc)
    state = attend(1, cmax_ctx, state, lambda hd: vxt_ref[vsl(hd), :], past)

    lat_vals = lambda c: (lambda hd: vt_ref[c, vsl(hd), :])

    def body(i, carry):
        cmax0, state = carry
        c = 2 * i
        cmax1 = scores(1, lat_keys(c + 1), kc)
        state = attend(0, cmax0, state, lat_vals(c), kc)
        cmax0 = scores(0, lat_keys(c + 2), kc)
        state = attend(1, cmax1, state, lat_vals(c + 1), kc)
        return cmax0, state

    cmax, state = lax.fori_loop(0, nchunk // 2 - 1, body, (cmax, state))
    cmax1 = scores(1, lat_keys(nchunk - 1), kc)
    state = attend(0, cmax, state, lat_vals(nchunk - 2), kc)
    state = attend(1, cmax1, state, lat_vals(nchunk - 1), kc)
    o_t = jnp.concatenate([acc[:MLA_V_DIM] / acc[MLA_V_DIM:MLA_V_DIM + 1] for _, acc in state], axis=0)
    o_ref[...] = jnp.transpose(o_t).astype(BF16)


def _mla_lat_attention(qt, km, vt, kx, vxt, layer, n):
    ntile, qw, tq = qt.shape
    t = ntile * tq
    past = kx.shape[2]
    qpb = n // tq
    return pl.pallas_call(
        _mla_lat_kernel,
        grid=(t // n, qpb),
        in_specs=[
            pl.BlockSpec((None, qw, tq), lambda b, i: (b * qpb + i, 0, 0)),
            pl.BlockSpec((n, km.shape[1]), lambda b, i: (b, 0)),
            pl.BlockSpec((qpb, vt.shape[1], tq), lambda b, i: (b, 0, 0)),
            pl.BlockSpec((None, None, past, kx.shape[3]), lambda b, i: (layer, b, 0, 0)),
            pl.BlockSpec((None, None, vxt.shape[2], past), lambda b, i: (layer, b, 0, 0)),
        ],
        out_specs=pl.BlockSpec((tq, MLA_HEADS * MLA_V_DIM), lambda b, i: (b * qpb + i, 0)),
        out_shape=jax.ShapeDtypeStruct((t, MLA_HEADS * MLA_V_DIM), BF16),
        scratch_shapes=[pltpu.VMEM((2, MLA_HEADS, max(tq, past), tq), F32),
                        pltpu.VMEM((MLA_HEADS, max(tq, past), tq), BF16)],
        compiler_params=_params("parallel", "parallel"),
        name="mla_lat_attention",
    )(qt, km, vt, kx, vxt)


def _ctx_kv_kernel(ckv_ref, kr_ref, wka_ref, wv_ref, vone_ref, place_ref, k_ref, vt_ref):
    ckv = ckv_ref[...].astype(BF16)
    k_ref[...] = (_dot(ckv, wka_ref[...]) + _dot(kr_ref[...].astype(BF16), place_ref[...])).astype(BF16)
    v = _dot(ckv, wv_ref[...]) + vone_ref[...]
    for j in range(v.shape[1] // LANES):
        sl = slice(j * LANES, (j + 1) * LANES)
        vt_ref[sl, :] = jnp.transpose(v[:, sl]).astype(BF16)


def _ctx_kv(cache_ckv, cache_krope, wk_a, wv_ext, vone_ext):
    bd, depth, past, _ = cache_ckv.shape
    place = np.zeros((MLA_ROPE_DIM, MLA_HEADS * MLA_QK_PAD), np.float32)
    for hd in range(MLA_HEADS):
        for i in range(MLA_ROPE_DIM):
            place[i, hd * MLA_QK_PAD + MLA_NOPE_DIM + i] = 1.0
    kw, vw = MLA_HEADS * MLA_QK_PAD, MLA_HEADS * MLA_V_PAD
    return pl.pallas_call(
        _ctx_kv_kernel,
        grid=(depth, bd),
        in_specs=[
            pl.BlockSpec((None, None, past, MLA_KV_LORA), lambda l, b: (b, l, 0, 0)),
            pl.BlockSpec((None, None, past, MLA_ROPE_DIM), lambda l, b: (b, l, 0, 0)),
            pl.BlockSpec((None, MLA_KV_LORA, kw), lambda l, b: (l, 0, 0)),
            pl.BlockSpec((None, MLA_KV_LORA, vw), lambda l, b: (l, 0, 0)),
            pl.BlockSpec((1, vw), lambda l, b: (0, 0)),
            pl.BlockSpec((MLA_ROPE_DIM, kw), lambda l, b: (0, 0)),
        ],
        out_specs=[pl.BlockSpec((None, None, past, kw), lambda l, b: (l, b, 0, 0)),
                   pl.BlockSpec((None, None, vw, past), lambda l, b: (l, b, 0, 0))],
        out_shape=[jax.ShapeDtypeStruct((depth, bd, past, kw), BF16),
                   jax.ShapeDtypeStruct((depth, bd, vw, past), BF16)],
        compiler_params=_params("parallel", "parallel"),
        name="ctx_kv",
    )(cache_ckv, cache_krope, wk_a, wv_ext, vone_ext, jnp.asarray(place, BF16))


def _na_tile_geometry(rows):
    last = rows // NA_Q_ROWS - 1
    geo = []
    for j in (0, 1, last):
        r0 = j * NA_Q_ROWS
        geo.append((r0, min(max(r0 - NA_KH // 2, 0), rows - NA_WIN_ROWS)))
    return geo


def _na_bias_kernel(geo, rows, rpb_ref, o_ref):
    l = pl.program_id(0)
    hd = pl.program_id(1)
    base = (l * NA_HEADS + hd) * (2 * NA_KH - 1) * (2 * NA_KW - 1)
    qc = lax.broadcasted_iota(jnp.int32, (GRID_W, GRID_W), 0)
    kcol = lax.broadcasted_iota(jnp.int32, (GRID_W, GRID_W), 1)
    d_col = jnp.clip(kcol - qc + (NA_KW - 1), 0, 2 * NA_KW - 2)
    col_start = jnp.clip(qc - NA_KW // 2, 0, GRID_W - NA_KW)
    in_cols = (kcol >= col_start) & (kcol < col_start + NA_KW)
    neg = jnp.full((GRID_W, GRID_W), NEG_INF, F32)
    tabs = []
    for dr in range(2 * NA_KH - 1):
        acc = jnp.zeros((GRID_W, GRID_W), F32)
        for dc in range(2 * NA_KW - 1):
            acc = jnp.where(d_col == dc, rpb_ref[base + dr * (2 * NA_KW - 1) + dc], acc)
        tabs.append(jnp.where(in_cols, acc, neg))
    for kind, (r0, ws) in enumerate(geo):
        for i in range(NA_Q_ROWS):
            r = r0 + i
            lo = min(max(r - NA_KH // 2, 0), rows - NA_KH)
            for j in range(NA_WIN_ROWS):
                kr = ws + j
                blk = tabs[kr - r + NA_KH - 1] if lo <= kr < lo + NA_KH else neg
                o_ref[kind, i * GRID_W:(i + 1) * GRID_W, j * GRID_W:(j + 1) * GRID_W] = blk


def _na_bias(na_rpb, rows):
    depth = na_rpb.shape[0]
    geo = _na_tile_geometry(rows)
    qn, kn = NA_Q_ROWS * GRID_W, NA_WIN_ROWS * GRID_W
    return pl.pallas_call(
        functools.partial(_na_bias_kernel, geo, rows),
        grid=(depth, NA_HEADS),
        in_specs=[pl.BlockSpec(memory_space=pltpu.SMEM)],
        out_specs=pl.BlockSpec((None, None, 3, qn, kn), lambda l, h: (l, h, 0, 0, 0)),
        out_shape=jax.ShapeDtypeStruct((depth, NA_HEADS, 3, qn, kn), F32),
        compiler_params=_params("parallel", "parallel"),
        name="na_bias",
    )(na_rpb.reshape(-1))


def _na_lat_kernel(rows, q_ref, k_ref, v_ref, kx_ref, vx_ref, bias_ref, o_ref):
    j = pl.program_id(1)
    ws = jnp.clip(j * NA_Q_ROWS - NA_KH // 2, 0, rows - NA_WIN_ROWS)
    start = pl.multiple_of(ws * GRID_W, GRID_W)
    nk = NA_WIN_ROWS * GRID_W
    for hd in range(NA_HEADS):
        sl = slice(hd * NA_HEAD_DIM, (hd + 1) * NA_HEAD_DIM)
        q = q_ref[:, sl]
        kx = kx_ref[hd].astype(BF16)
        vx = vx_ref[hd].astype(BF16)
        s_win = _nt_dot(q, k_ref[pl.ds(start, nk), sl]) + bias_ref[hd]
        s_ctx = _nt_dot(q, kx)
        m = jnp.maximum(jnp.max(s_win, axis=-1, keepdims=True), jnp.max(s_ctx, axis=-1, keepdims=True))
        p_win = jnp.exp(s_win - m)
        p_ctx = jnp.exp(s_ctx - m)
        l = jnp.sum(p_win, axis=-1, keepdims=True) + jnp.sum(p_ctx, axis=-1, keepdims=True)
        o = _dot(p_win.astype(BF16), v_ref[pl.ds(start, nk), sl]) + _dot(p_ctx.astype(BF16), vx)
        o_ref[:, sl] = (o / l).astype(BF16)


def _na_lat_attention(qn, kn, vn, cache_k, cache_v, bias, layer, n):
    t = qn.shape[0]
    rows = n // GRID_W
    assert rows % NA_Q_ROWS == 0 and rows >= NA_WIN_ROWS + NA_Q_ROWS
    tiles = rows // NA_Q_ROWS
    tq = NA_Q_ROWS * GRID_W
    past = cache_k.shape[3]

    def kind(b, j):
        return (layer, 0, jnp.where(j == 0, 0, jnp.where(j == tiles - 1, 2, 1)), 0, 0)

    return pl.pallas_call(
        functools.partial(_na_lat_kernel, rows),
        grid=(t // n, tiles),
        in_specs=[
            pl.BlockSpec((tq, NA_WIDTH), lambda b, j: (b * tiles + j, 0)),
            pl.BlockSpec((n, NA_WIDTH), lambda b, j: (b, 0)),
            pl.BlockSpec((n, NA_WIDTH), lambda b, j: (b, 0)),
            pl.BlockSpec((None, None, NA_HEADS, past, NA_HEAD_DIM), lambda b, j: (b, layer, 0, 0, 0)),
            pl.BlockSpec((None, None, NA_HEADS, past, NA_HEAD_DIM), lambda b, j: (b, layer, 0, 0, 0)),
            pl.BlockSpec((None, NA_HEADS, None, tq, NA_WIN_ROWS * GRID_W), kind),
        ],
        out_specs=pl.BlockSpec((tq, NA_WIDTH), lambda b, j: (b * tiles + j, 0)),
        out_shape=jax.ShapeDtypeStruct((t, NA_WIDTH), BF16),
        compiler_params=_params("parallel", "parallel"),
        name="na_lat_attention",
    )(qn, kn, vn, cache_k, cache_v, bias)


def _dft_tables(n):
    def thin(j, k, period):
        ang = (2.0 * math.pi / period) * ((j[:, None] * k[None, :]) % period).astype(F32)
        return jnp.cos(ang), jnp.sin(ang)

    k = jnp.arange(n, dtype=jnp.int32)
    scale = float(n) ** -0.5
    if n % 64 == 0 and n > 64:
        n1 = n // 64
        c1, s1 = thin(jnp.arange(n1, dtype=jnp.int32), k, n1)
        c2, s2 = thin(jnp.arange(64, dtype=jnp.int32), k, n)
        c1, s1, c2, s2 = c1[:, None, :], s1[:, None, :], c2[None, :, :], s2[None, :, :]
        cm = (c1 * c2 - s1 * s2).reshape(n, n)
        sm = (s1 * c2 + c1 * s2).reshape(n, n)
    else:
        cm, sm = thin(k, k, n)
    return (cm * scale).astype(BF16), (sm * -scale).astype(BF16)


def _fourier_kernel(c_ref, s_ref, ab_ref, o_ref):
    o_ref[...] = (_dot(c_ref[...], ab_ref[:, :FN_WIDTH]) + _dot(s_ref[...], ab_ref[:, FN_WIDTH:])).astype(BF16)


def _fourier(fab, tables, n, tmf=512):
    t = fab.shape[0]
    tmf = min(tmf, n)
    tiles = n // tmf
    cm, sm = tables
    return pl.pallas_call(
        _fourier_kernel,
        grid=(tiles, t // n),
        in_specs=[
            pl.BlockSpec((tmf, n), lambda i, b: (i, 0)),
            pl.BlockSpec((tmf, n), lambda i, b: (i, 0)),
            pl.BlockSpec((n, 2 * FN_WIDTH), lambda i, b: (b, 0)),
        ],
        out_specs=pl.BlockSpec((tmf, FN_WIDTH), lambda i, b: (b * tiles + i, 0)),
        out_shape=jax.ShapeDtypeStruct((t, FN_WIDTH), BF16),
        compiler_params=_params("parallel", "parallel"),
        name="fourier",
    )(cm, sm, fab)


def _route(s_t, sb_t):
    def top2_sum(v):
        hi1, lo1 = jnp.maximum(v[0], v[1]), jnp.minimum(v[0], v[1])
        hi2, lo2 = jnp.maximum(v[2], v[3]), jnp.minimum(v[2], v[3])
        return jnp.maximum(hi1, hi2) + jnp.maximum(jnp.minimum(hi1, hi2), jnp.maximum(lo1, lo2))

    best = top2_sum(sb_t[0:EXPERTS_PER_GROUP])
    gsel = jnp.zeros_like(best, dtype=jnp.int32)
    for g in range(1, N_EXPERT_GROUPS):
        cand = top2_sum(sb_t[g * EXPERTS_PER_GROUP:(g + 1) * EXPERTS_PER_GROUP])
        better = cand > best
        gsel = jnp.where(better, g, gsel)
        best = jnp.where(better, cand, best)
    chosen = []
    for e in range(N_EXPERTS):
        g = e // EXPERTS_PER_GROUP
        beaten = jnp.zeros_like(gsel)
        for o in range(g * EXPERTS_PER_GROUP, (g + 1) * EXPERTS_PER_GROUP):
            if o == e:
                continue
            ahead = (sb_t[o] > sb_t[e]) | ((sb_t[o] == sb_t[e]) & (o < e))
            beaten = beaten + ahead.astype(jnp.int32)
        chosen.append((gsel == g) & (beaten < 2))
    picked = [jnp.where(chosen[e], s_t[e], 0.0) for e in range(N_EXPERTS)]
    denom = picked[0]
    for e in range(1, N_EXPERTS):
        denom = denom + picked[e]
    return chosen, [pk / denom for pk in picked]


def _pack_pairs(x):
    w = x.shape[1] // 2
    hi = pltpu.bitcast(x[:, :w].astype(BF16).astype(F32), jnp.uint32)
    lo = pltpu.bitcast(x[:, w:].astype(BF16).astype(F32), jnp.uint32)
    return hi | (lo >> 16)


def _mixout_kernel(n, x_ref, mod_ref, ab_ref, z_ref, zp_ref, zn_ref, yna_ref, ymla_ref, g_ref, cw_ref, wfn_ref,
                   wout_ref, g2_ref, wrc_ref, br_ref, x1_ref, h2_ref, route_ref, gtok_ref, cnt_ref):
    tm = x_ref.shape[0]
    i = pl.program_id(0)
    mod = mod_ref[...]
    gate1, shift2, scale2 = mod[2:3], mod[3:4], mod[4:5]

    z = z_ref[...].astype(F32)
    ridx = lax.broadcasted_iota(jnp.int32, z.shape, 0)
    at_start = (i * tm) % n == 0
    at_end = ((i + 1) * tm) % n == 0
    prev_row = jnp.where(at_start, 0.0, zp_ref[7:8, :].astype(F32))
    next_row = jnp.where(at_end, 0.0, zn_ref[0:1, :].astype(F32))
    z_m1 = jnp.where(ridx == 0, prev_row, pltpu.roll(z, 1, axis=0))
    z_p1 = jnp.where(ridx == tm - 1, next_row, pltpu.roll(z, tm - 1, axis=0))
    cw = cw_ref[...]
    y_conv = ab_ref[...].astype(F32) * (z_m1 * cw[0:1] + z * cw[1:2] + z_p1 * cw[2:3])

    y_fn = _dot(g_ref[...], wfn_ref[...])
    cat = jnp.concatenate([y_conv.astype(BF16), yna_ref[...], ymla_ref[...], y_fn.astype(BF16)], axis=-1)
    x1 = x_ref[...] + gate1 * _dot(cat, wout_ref[...])
    x1_ref[...] = x1

    h2 = _rms(x1, g2_ref[...]) * (1.0 + scale2) + shift2
    packed = _pack_pairs(h2)
    piece = packed.shape[1] // MOE_PIECES
    for p in range(MOE_PIECES):
        h2_ref[p] = packed[:, p * piece:(p + 1) * piece]
    h2_hi = h2.astype(BF16)
    h2_lo = (h2 - h2_hi.astype(F32)).astype(BF16)
    both = _dot(h2_hi, wrc_ref[...])
    logits = both[:, :LANES] + (both[:, LANES:] + _dot(h2_lo, wrc_ref[:, :LANES]))
    s = jax.nn.sigmoid(logits)
    s_t = jnp.transpose(s)
    sb_t = jnp.transpose(s + br_ref[...])
    chosen, gates = _route([s_t[e:e + 1] for e in range(N_EXPERTS)], [sb_t[e:e + 1] for e in range(N_EXPERTS)])

    @pl.when(i == 0)
    def _():
        cnt_ref[...] = jnp.zeros(cnt_ref.shape, F32)

    chosen_f = jnp.concatenate([ch.astype(F32) for ch in chosen], axis=0)
    before = lax.broadcasted_iota(jnp.int32, (tm, tm), 0) < lax.broadcasted_iota(jnp.int32, (tm, tm), 1)
    prefix = _dot(chosen_f.astype(BF16), jnp.where(before, 1.0, 0.0).astype(BF16))
    base = cnt_ref[...]
    rank = jnp.concatenate([base] * (tm // LANES), axis=1) + prefix
    cnt_ref[...] = base + jnp.sum(chosen_f, axis=1, keepdims=True)

    zero = jnp.zeros((1, tm), F32)
    seen = zero
    slots = [[zero, zero, zero], [zero, zero, zero]]
    for e in range(N_EXPERTS):
        for k in range(2):
            hit = chosen[e] & (seen == float(k))
            for j, val in enumerate((float(e), gates[e], rank[e:e + 1])):
                slots[k][j] = jnp.where(hit, val, slots[k][j])
        seen = seen + chosen_f[e:e + 1]
    (e_lo, g_lo, r_lo), (e_hi, g_hi, r_hi) = slots
    route_ref[...] = jnp.concatenate([g_lo, g_hi, e_lo, e_hi, r_lo, r_hi, zero, zero], axis=0)
    gates_t = jnp.concatenate([g_lo, g_hi, jnp.zeros((LANES - 2, tm), F32)], axis=0)
    gtok_ref[...] = jnp.transpose(gates_t)


def _mixout(x, mod, layer, n, parts, wts, lat, tm):
    t, d = x.shape
    ab, z, yna, ymla, g = parts
    nblk8 = t // 8
    per8 = tm // 8
    cond_row_of_tile = _cond_row(lat, tm, n, mod.shape[1] - 1)
    const2 = lambda i: (0, 0)
    lsel = lambda i: (layer, 0, 0)
    row = lambda w: pl.BlockSpec((tm, w), lambda i: (i, 0))
    in_specs = [
        row(d),
        pl.BlockSpec((None, None, 6, d), lambda i: (layer, cond_row_of_tile(i), 0, 0)),
        row(CONV_WIDTH),
        row(CONV_WIDTH),
        pl.BlockSpec((8, CONV_WIDTH), lambda i: (jnp.maximum(i * per8 - 1, 0), 0)),
        pl.BlockSpec((8, CONV_WIDTH), lambda i: (jnp.minimum((i + 1) * per8, nblk8 - 1), 0)),
        row(NA_WIDTH),
        row(MLA_HEADS * MLA_V_DIM),
        row(FN_WIDTH),
        pl.BlockSpec((None, 3, CONV_WIDTH), lsel),
        pl.BlockSpec((None, FN_WIDTH, FN_WIDTH), lsel),
        pl.BlockSpec((None, d, d), lsel),
        pl.BlockSpec((None, 1, d), lsel),
        pl.BlockSpec((d, 2 * LANES), const2),
        pl.BlockSpec((1, LANES), const2),
    ]
    out_specs = [
        row(d),
        pl.BlockSpec((MOE_PIECES, tm, d // 2 // MOE_PIECES), lambda i: (0, i, 0)),
        pl.BlockSpec((8, tm), lambda i: (0, i)),
        row(LANES),
        pl.BlockSpec((N_EXPERTS, LANES), const2),
    ]
    out_shape = [
        jax.ShapeDtypeStruct((t, d), F32),
        jax.ShapeDtypeStruct((MOE_PIECES, t, d // 2 // MOE_PIECES), jnp.uint32),
        jax.ShapeDtypeStruct((8, t), F32),
        jax.ShapeDtypeStruct((t, LANES), F32),
        jax.ShapeDtypeStruct((N_EXPERTS, LANES), F32),
    ]
    return pl.pallas_call(
        functools.partial(_mixout_kernel, n),
        grid=(t // tm,),
        in_specs=in_specs,
        out_specs=out_specs,
        out_shape=out_shape,
        compiler_params=_params("arbitrary"),
        name="mixout",
    )(x, mod, ab, z, z, z, yna, ymla, g, wts["conv_w"], wts["w_fn"], wts["w_out"], wts["norm2"],
      wts["wr_cat"], wts["b_router"])


def _slot_positions(route, counts, rb):
    cnt = counts[:, 0].astype(jnp.int32)
    padded = (cnt + rb - 1) // rb * rb
    ends = jnp.cumsum(padded)
    offs = ends - padded
    experts = route[2:4].astype(jnp.int32)
    ranks = route[4:6].astype(jnp.int32)
    pos = ranks
    for e in range(N_EXPERTS):
        pos = pos + jnp.where(experts == e, offs[e], 0)
    nblk = (2 * route.shape[1]) // rb + N_EXPERTS
    starts = jnp.arange(nblk, dtype=jnp.int32) * rb
    blk_expert = jnp.sum((starts[:, None] >= ends[None, :]).astype(jnp.int32), axis=1)
    used = blk_expert < N_EXPERTS
    blk_expert = jnp.where(used, blk_expert, 0)
    valid_end = jnp.sum(jnp.where(blk_expert[:, None] == jnp.arange(N_EXPERTS)[None, :], (offs + cnt)[None, :], 0), axis=1)
    blk_valid = jnp.where(used, jnp.clip(valid_end - starts, 0, rb), 0)
    return pos, jnp.stack([blk_expert, blk_valid])


def _sc_mesh():
    return plsc.VectorSubcoreMesh(core_axis_name="c", subcore_axis_name="s")


def _sc_pipeline(body, nwin, in_specs, out_specs):
    return pltpu.emit_pipeline(body, grid=(nwin,), in_specs=in_specs, out_specs=out_specs,
                               core_axis_name=("c", "s"), dimension_semantics=(pltpu.PARALLEL,))


def _row_scatter(table, idx_a, idx_b, nrows):
    b, w = table.shape
    win = SC_WINDOW
    idx_spec = pl.BlockSpec((1, win), lambda i: (0, i))

    @functools.partial(pl.kernel, out_type=jax.ShapeDtypeStruct((nrows, w), table.dtype), mesh=_sc_mesh(),
                       scratch_types=[])
    def scatter(table_hbm, ia_hbm, ib_hbm, out_hbm):
        def body(rows_vmem, ia_vmem, ib_vmem):
            pltpu.sync_copy(rows_vmem, out_hbm.at[ia_vmem.at[0]])
            pltpu.sync_copy(rows_vmem, out_hbm.at[ib_vmem.at[0]])

        _sc_pipeline(body, b // win, [pl.BlockSpec((win, w), lambda i: (i, 0)), idx_spec, idx_spec], [])(
            table_hbm, ia_hbm, ib_hbm)

    return scatter(table, idx_a.reshape(1, b), idx_b.reshape(1, b))


def _row_gather(table, idx):
    b = idx.shape[0]
    w = table.shape[1]
    win = SC_WINDOW

    @functools.partial(pl.kernel, out_type=jax.ShapeDtypeStruct((b, w), table.dtype), mesh=_sc_mesh(),
                       scratch_types=[])
    def gather(table_hbm, idx_hbm, out_hbm):
        def body(idx_vmem, out_vmem):
            pltpu.sync_copy(table_hbm.at[idx_vmem.at[0]], out_vmem)

        _sc_pipeline(body, b // win, [pl.BlockSpec((1, win), lambda i: (0, i))],
                     [pl.BlockSpec((win, w), lambda i: (i, 0))])(idx_hbm, out_hbm)

    return gather(table, idx.reshape(1, b))


def _ffn_kernel(blk_ref, xs_ref, w13_ref, w2_ref, y_ref):
    i = pl.program_id(0)
    e = blk_ref[0, i]
    nvalid = blk_ref[1, i]

    @pl.when(nvalid > 0)
    def _():
        packed = jnp.concatenate([xs_ref[0], xs_ref[1]], axis=-1)
        live = lax.broadcasted_iota(jnp.int32, packed.shape, 0) < nvalid
        xb = _unpack_pairs(jnp.where(live, packed, jnp.uint32(0))).astype(BF16)
        up = _dot(xb, w13_ref[e])
        a, b = up[:, :EXPERT_FF], up[:, EXPERT_FF:]
        hid = (a * jax.nn.sigmoid(a)) * b
        y = _pack_pairs(_dot(hid.astype(BF16), w2_ref[e]))
        half = y.shape[1] // 2
        y_ref[0] = y[:, :half]
        y_ref[1] = y[:, half:]

    @pl.when(nvalid == 0)
    def _():
        y_ref[...] = jnp.zeros(y_ref.shape, y_ref.dtype)


def _expert_ffn(xs, blk, w13, w2, layer, rb):
    pieces, nrows, w = xs.shape
    d = 2 * pieces * w
    resident = dict(pipeline_mode=pl.Buffered(1))
    used = lambda i, blk: (0, jnp.where(blk[1, i] > 0, i, 0), 0)
    return pl.pallas_call(
        _ffn_kernel,
        grid_spec=pltpu.PrefetchScalarGridSpec(
            num_scalar_prefetch=1,
            grid=(nrows // rb,),
            in_specs=[
                pl.BlockSpec((pieces, rb, w), used),
                pl.BlockSpec((None, N_EXPERTS, d, 2 * EXPERT_FF), lambda i, blk: (layer, 0, 0, 0), **resident),
                pl.BlockSpec((None, N_EXPERTS, EXPERT_FF, d), lambda i, blk: (layer, 0, 0, 0), **resident),
            ],
            out_specs=pl.BlockSpec((pieces, rb, w), lambda i, blk: (0, i, 0)),
        ),
        out_shape=jax.ShapeDtypeStruct(xs.shape, xs.dtype),
        compiler_params=_params("parallel"),
        name="expert_ffn",
    )(blk, xs, w13, w2)


def _combine_kernel(final, x1_ref, gtok_ref, mod_ref, nf_ref, y_ref, o_ref):
    out = _moe_residual(x1_ref[...], gtok_ref, y_ref, mod_ref)
    if final:
        out = _rms(out, nf_ref[...])
    o_ref[...] = out


def _combine(x1, gtok, y_tok, mod, layer, n, norm_f, final, lat, tc=512):
    t, d = x1.shape
    cond_row_of_tile = _cond_row(lat, tc, n, mod.shape[1] - 1)
    row = lambda w: pl.BlockSpec((tc, w), lambda i: (i, 0))
    return pl.pallas_call(
        functools.partial(_combine_kernel, final),
        grid=(t // tc,),
        in_specs=[
            row(d),
            row(LANES),
            pl.BlockSpec((None, None, 6, d), lambda i: (layer, cond_row_of_tile(i), 0, 0)),
            pl.BlockSpec((1, d), lambda i: (0, 0)),
            pl.BlockSpec(y_tok.shape[:2] + (tc, y_tok.shape[3]), lambda i: (0, 0, i, 0)),
        ],
        out_specs=row(d),
        out_shape=jax.ShapeDtypeStruct((t, d), F32),
        compiler_params=_params("parallel"),
        name="combine",
    )(x1, gtok, mod, norm_f, y_tok)


def _moe_layer(x, mod, layer, n, parts, wts, lat, tm, rb):
    t = x.shape[0]
    x1, h2, route, gtok, counts = _mixout(x, mod, layer, n, parts, wts, lat, tm)
    pos, blk = _slot_positions(route, counts, rb)
    pieces, _, w = h2.shape
    nrows = blk.shape[1] * rb
    piece_base = (jnp.arange(pieces, dtype=jnp.int32) * nrows)[:, None]
    idx = [(piece_base + pos[s][None, :]).reshape(-1) for s in range(2)]
    xs = _row_scatter(h2.reshape(pieces * t, w), idx[0], idx[1], pieces * nrows).reshape(pieces, nrows, w)
    y = _expert_ffn(xs, blk, wts["w13"], wts["w2"], layer, rb)
    back = (piece_base[:, :, None] + pos[None, :, :]).reshape(-1)
    y_tok = _row_gather(y.reshape(pieces * nrows, w), back).reshape(pieces, 2, t, w)
    return x1, (gtok, y_tok)


def _swap_halves(w):
    nf = MLA_ROPE_DIM // 4
    idx = np.arange(MLA_ROPE_DIM).reshape(2, 2, nf)[:, ::-1, :].reshape(-1)
    return w[..., idx]


def _pack_weights(w_in, mla_wq_up, mla_wkv_up, w1, w3, w2, w_router, b_router):
    depth, d, _ = w_in.shape
    zeros = lambda w: jnp.zeros((depth, d, w), w_in.dtype)
    w_kr = w_in[..., 1920:1952]
    pad_rope = lambda w: jnp.concatenate([zeros(MLA_NOPE_DIM), w, zeros(MLA_QK_PAD - MLA_NOPE_DIM - MLA_ROPE_DIM)], -1)
    w_main = jnp.concatenate([w_in[..., :1920], pad_rope(w_kr), pad_rope(_swap_halves(w_kr)), w_in[..., 1952:]], -1)

    wq = mla_wq_up.reshape(depth, MLA_Q_LORA, MLA_HEADS, MLA_NOPE_DIM + MLA_ROPE_DIM)
    q_nope, q_rope = wq[..., :MLA_NOPE_DIM], wq[..., MLA_NOPE_DIM:]
    tail = jnp.zeros(q_rope.shape[:-1] + (MLA_QK_PAD - MLA_NOPE_DIM - MLA_ROPE_DIM,), wq.dtype)
    wq_a = jnp.concatenate([q_nope, q_rope, tail], -1).reshape(depth, MLA_Q_LORA, -1)
    wq_b = jnp.concatenate([jnp.zeros_like(q_nope), _swap_halves(q_rope), tail], -1).reshape(depth, MLA_Q_LORA, -1)

    wkv = mla_wkv_up.reshape(depth, MLA_KV_LORA, MLA_HEADS, MLA_NOPE_DIM + MLA_V_DIM)
    k_nope, v_up = wkv[..., :MLA_NOPE_DIM], wkv[..., MLA_NOPE_DIM:]
    k_tail = jnp.zeros(k_nope.shape[:-1] + (MLA_QK_PAD - MLA_NOPE_DIM,), wkv.dtype)
    wk_a = jnp.concatenate([k_nope, k_tail], -1).reshape(depth, MLA_KV_LORA, -1)
    wv = v_up.reshape(depth, MLA_KV_LORA, -1)
    v_tail = jnp.zeros(v_up.shape[:-1] + (MLA_V_PAD - MLA_V_DIM,), wkv.dtype)
    wv_ext = jnp.concatenate([v_up, v_tail], -1).reshape(depth, MLA_KV_LORA, -1)
    vone = np.zeros((1, MLA_HEADS * MLA_V_PAD), np.float32)
    vone[0, MLA_V_DIM::MLA_V_PAD] = 1.0

    wr = jnp.pad(w_router, ((0, 0), (0, LANES - N_EXPERTS)))
    wr_hi = wr.astype(BF16)
    wr_lo = (wr - wr_hi.astype(F32)).astype(BF16)
    return {
        "w_in": w_main.astype(BF16), "wq_a": wq_a.astype(BF16), "wq_b": wq_b.astype(BF16),
        "wk_a": wk_a.astype(BF16), "wv": wv.astype(BF16), "wv_ext": wv_ext.astype(BF16),
        "vone_ext": jnp.asarray(vone),
        "w13": jnp.concatenate([w1, w3], -1).astype(BF16), "w2": w2.astype(BF16),
        "wr_cat": jnp.concatenate([wr_hi, wr_lo], axis=-1),
        "b_router": jnp.pad(b_router, (0, LANES - N_EXPERTS)).reshape(1, LANES).astype(F32),
    }


def _channel_dft():
    c = np.arange(FN_GROUP_DIM)
    ang = 2.0 * np.pi * ((c[:, None] * c[None, :]) % FN_GROUP_DIM) / FN_GROUP_DIM
    out = np.zeros((FN_WIDTH, 2 * FN_WIDTH), np.float32)
    for g in range(FN_GROUPS):
        sl = slice(g * FN_GROUP_DIM, (g + 1) * FN_GROUP_DIM)
        out[sl, sl] = np.cos(ang) * FN_GROUP_DIM ** -0.5
        out[sl, FN_WIDTH + g * FN_GROUP_DIM:FN_WIDTH + (g + 1) * FN_GROUP_DIM] = np.sin(ang) * FN_GROUP_DIM ** -0.5
    return jnp.asarray(out, BF16)


def _rope_tables(n):
    tok = jnp.arange(n)
    pos = jnp.stack([tok // GRID_W, tok % GRID_W], axis=-1).astype(F32)
    nf = MLA_ROPE_DIM // 4
    freqs = ROPE_THETA ** (-jnp.arange(nf, dtype=F32) / nf)
    ang = pos[:, :, None] * freqs
    cos = jnp.broadcast_to(jnp.cos(ang)[:, :, None, :], (n, 2, 2, nf)).reshape(n, MLA_ROPE_DIM)
    sin = jnp.sin(ang)
    sin = jnp.stack([-sin, sin], axis=2).reshape(n, MLA_ROPE_DIM)
    pad = jnp.zeros((n, MLA_QK_PAD - MLA_NOPE_DIM - MLA_ROPE_DIM), F32)
    cos_t = jnp.concatenate([jnp.ones((n, MLA_NOPE_DIM), F32), cos, pad], -1)
    sin_t = jnp.concatenate([jnp.zeros((n, MLA_NOPE_DIM), F32), sin, pad], -1)
    return cos_t, sin_t


def kernel(x_prompt, x_sample, cache_na_k, cache_na_v, cache_mla_ckv, cache_mla_krope, c, c_ctx, w_ada, b_ada,
           norm1, norm2, w_in, conv_w, na_rpb, mla_gq, mla_wq_up, mla_gkv, mla_wkv_up, w_fn, w_out, w_router,
           b_router, w1, w3, w2, norm_f):
    bp, seq, d = x_prompt.shape
    bd, dec_seq, _ = x_sample.shape
    depth = w_in.shape[0]

    wts = _pack_weights(w_in, mla_wq_up, mla_wkv_up, w1, w3, w2, w_router, b_router)
    wts.update({
        "norm1": norm1.reshape(depth, 1, d), "norm2": norm2.reshape(depth, 1, d),
        "mla_gq": mla_gq.reshape(depth, 1, -1), "mla_gkv": mla_gkv.reshape(depth, 1, -1),
        "conv_w": conv_w, "w_fn": w_fn.astype(BF16), "w_out": w_out.astype(BF16),
        "norm_f": norm_f.reshape(1, d), "cs_bd": _channel_dft(),
    })

    cond = jnp.concatenate([c, jnp.zeros((-(bd + 1) % 8, d), c.dtype), c_ctx[None, :]], axis=0)
    mod = _ada_modulation(cond, w_ada, b_ada)

    xp = x_prompt.reshape(bp * seq, d)
    tables = _dft_tables(seq)
    caches = [
        jnp.zeros((bp, depth, NA_HEADS, seq, NA_HEAD_DIM), F32), jnp.zeros((bp, depth, NA_HEADS, seq, NA_HEAD_DIM), F32),
        jnp.zeros((bp, depth, seq, MLA_KV_LORA), F32), jnp.zeros((bp, depth, seq, MLA_ROPE_DIM), F32)]
    pending = None
    for layer in range(depth):
        outs = _premix(xp, mod, layer, seq, wts, None, False, seq, caches, pending)
        ab, z, qn, kn, vn, km, fab, qm, vm = outs[:9]
        caches = outs[9:13]
        if pending is not None:
            xp = outs[13]
        yna, ymla = _ctx_attention(qn, kn, vn, qm, km, vm, seq)
        g = _fourier(fab, tables, seq)
        xp, pending = _moe_layer(xp, mod, layer, seq, (ab, z, yna, ymla, g), wts, False, seq, MOE_ROW_BLOCK_CTX)
    xp = _combine(xp, pending[0], pending[1], mod, depth - 1, seq, wts["norm_f"], True, False, seq)
    new_na_k, new_na_v, new_ckv, new_krope = caches

    xs = x_sample.reshape(bd * dec_seq, d)
    kx, vxt = _ctx_kv(cache_mla_ckv, cache_mla_krope, wts["wk_a"], wts["wv_ext"], wts["vone_ext"])
    na_bias = _na_bias(na_rpb, dec_seq // GRID_W)
    rope = _rope_tables(dec_seq)
    tables = _dft_tables(dec_seq)
    pending = None
    for layer in range(depth):
        outs = _premix(xs, mod, layer, dec_seq, wts, rope, True, TM_LAT_PREMIX, None, pending)
        ab, z, qn, kn, vn, km, fab, qt, vt = outs[:9]
        if pending is not None:
            xs = outs[9]
        yna = _na_lat_attention(qn, kn, vn, cache_na_k, cache_na_v, na_bias, layer, dec_seq)
        ymla = _mla_lat_attention(qt, km, vt, kx, vxt, layer, dec_seq)
        g = _fourier(fab, tables, dec_seq)
        xs, pending = _moe_layer(xs, mod, layer, dec_seq, (ab, z, yna, ymla, g), wts, True, TM_LAT_MIXOUT,
                                 MOE_ROW_BLOCK)
    xs = _combine(xs, pending[0], pending[1], mod, depth - 1, dec_seq, wts["norm_f"], True, True, TM_LAT_PREMIX)

    return (xp.reshape(bp, seq, d), xs.reshape(bd, dec_seq, d), new_na_k, new_na_v, new_ckv, new_krope)
```

```python
import functools
import math

import numpy as np
import jax
import jax.numpy as jnp
from jax import lax
from jax.experimental import pallas as pl
from jax.experimental.pallas import tpu as pltpu
from jax.experimental.pallas import tpu_sc as plsc

F32 = jnp.float32
BF16 = jnp.bfloat16

GRID_W = 64
CONV_WIDTH = 256
NA_HEADS = 4
NA_HEAD_DIM = 64
NA_WIDTH = NA_HEADS * NA_HEAD_DIM
NA_KH = 8
NA_KW = 16
MLA_HEADS = 4
MLA_Q_LORA = 256
MLA_KV_LORA = 128
MLA_NOPE_DIM = 64
MLA_ROPE_DIM = 32
MLA_V_DIM = 64
MLA_QK_PAD = 128
MLA_V_PAD = 96
MLA_KEY_SUB = 256
LOG2E = 1.4426950408889634
FN_GROUPS = 4
FN_GROUP_DIM = 64
FN_WIDTH = FN_GROUPS * FN_GROUP_DIM
N_EXPERTS = 16
N_EXPERT_GROUPS = 4
EXPERTS_PER_GROUP = N_EXPERTS // N_EXPERT_GROUPS
EXPERT_FF = 256
ROPE_THETA = 10000.0
EPS = 1e-6
NEG_INF = -1e30
LANES = 128

NA_SCALE = NA_HEAD_DIM ** -0.5
MLA_SCALE = (MLA_NOPE_DIM + MLA_ROPE_DIM) ** -0.5

NA_Q_ROWS = 4
NA_WIN_ROWS = 12

TM_LAT_PREMIX = 512
TM_LAT_MIXOUT = 512
MOE_ROW_BLOCK_CTX = 512
MOE_ROW_BLOCK = 512
SC_WINDOW = 128
MOE_PIECES = 2

VMEM_LIMIT = 56 * 1024 * 1024

_C_AB, _C_AC, _C_AU, _C_Q, _C_K, _C_V, _C_CQ = 0, 256, 512, 768, 1024, 1280, 1536
_C_CKV, _C_KR, _C_KRS, _C_FU, _C_END = 1792, 1920, 2048, 2176, 2432


def _nt_dot(a, b):
    return lax.dot_general(a, b, (((1,), (1,)), ((), ())), preferred_element_type=F32)


def _dot(a, b):
    return jnp.dot(a, b, preferred_element_type=F32)


def _rms(x, g):
    return x * lax.rsqrt(jnp.mean(x * x, axis=-1, keepdims=True) + EPS) * g


def _params(*sem, flags=None):
    return pltpu.CompilerParams(dimension_semantics=sem, vmem_limit_bytes=VMEM_LIMIT, flags=flags)


def _ada_kernel(c_ref, w_ref, b_ref, o_ref):
    cnd = c_ref[...]
    act = cnd * jax.nn.sigmoid(cnd)
    o_ref[...] = _dot(act.astype(BF16), w_ref[...].astype(BF16)) + b_ref[...]


def _ada_modulation(cond, w_ada, b_ada):
    depth, d, six_d = w_ada.shape
    r = cond.shape[0]
    tn = 1024
    out = pl.pallas_call(
        _ada_kernel,
        grid=(depth, six_d // tn),
        in_specs=[
            pl.BlockSpec((r, d), lambda l, j: (0, 0)),
            pl.BlockSpec((None, d, tn), lambda l, j: (l, 0, j)),
            pl.BlockSpec((None, 1, tn), lambda l, j: (l, 0, j)),
        ],
        out_specs=pl.BlockSpec((None, r, tn), lambda l, j: (l, 0, j)),
        out_shape=jax.ShapeDtypeStruct((depth, r, six_d), F32),
        compiler_params=_params("parallel", "parallel"),
        name="ada_modulation",
    )(cond, w_ada, b_ada.reshape(depth, 1, six_d))
    return out.reshape(depth, r, 6, d)


def _unpack_pairs(p):
    hi = pltpu.bitcast(p & jnp.uint32(0xFFFF0000), F32)
    lo = pltpu.bitcast(p << 16, F32)
    return jnp.concatenate([hi, lo], axis=-1)


def _moe_residual(x1, gtok_ref, y_ref, mod_ref):
    g = gtok_ref[...]
    y_lo = _unpack_pairs(jnp.concatenate([y_ref[0, 0], y_ref[1, 0]], axis=-1))
    y_hi = _unpack_pairs(jnp.concatenate([y_ref[0, 1], y_ref[1, 1]], axis=-1))
    return x1 + mod_ref[...][5:6] * (g[:, 0:1] * y_lo + g[:, 1:2] * y_hi)


def _premix_kernel(lat, fused, n_in, *refs):
    (x_ref, mod_ref, g1_ref, w_ref, gq_ref, wqa_ref, wqb_ref, gkv_ref, wka_ref, wv_ref, vone_ref, cs_ref,
     cos_ref, sin_ref) = refs[:14]
    outs = refs[n_in:]
    (ab_ref, z_ref, qn_ref, kn_ref, vn_ref, km_ref, fab_ref) = outs[:7]

    x = x_ref[...]
    if fused:
        gtok_ref, y_ref, modp_ref = refs[n_in - 3:n_in]
        x = _moe_residual(x, gtok_ref, y_ref, modp_ref)
        outs[-1][...] = x
    mod = mod_ref[...]
    h = _rms(x, g1_ref[...]) * (1.0 + mod[1:2]) + mod[0:1]
    p = _dot(h.astype(BF16), w_ref[...])

    ab_ref[...] = p[:, _C_AB:_C_AC].astype(BF16)
    z_ref[...] = (p[:, _C_AC:_C_AU] * p[:, _C_AU:_C_Q]).astype(BF16)
    k_na = p[:, _C_K:_C_V]
    v_na = p[:, _C_V:_C_CQ]
    qn_ref[...] = (p[:, _C_Q:_C_K] * NA_SCALE).astype(BF16)
    kn_ref[...] = k_na.astype(BF16)
    vn_ref[...] = v_na.astype(BF16)

    cqn = _rms(p[:, _C_CQ:_C_CKV], gq_ref[...]).astype(BF16)
    ckvn = _rms(p[:, _C_CKV:_C_KR], gkv_ref[...])
    ckvn_b = ckvn.astype(BF16)
    qa = _dot(cqn, wqa_ref[...])
    kva = _dot(ckvn_b, wka_ref[...])
    v_mla = _dot(ckvn_b, wv_ref[...]) + vone_ref[...]
    kr = p[:, _C_KR:_C_KRS]
    if lat:
        cos = cos_ref[...]
        sin = sin_ref[...]
        qb = _dot(cqn, wqb_ref[...])
        krot = kr * cos + p[:, _C_KRS:_C_FU] * sin
        qt_ref, vt_ref = outs[7:9]
    else:
        krot = kr
        qm_ref, vm_ref, ck_ref, cv_ref, cckv_ref, ckr_ref = outs[7:13]
    for hd in range(MLA_HEADS):
        sl = slice(hd * MLA_QK_PAD, (hd + 1) * MLA_QK_PAD)
        km_ref[:, sl] = (kva[:, sl] + krot).astype(BF16)
        if lat:
            qh = (qa[:, sl] * cos + qb[:, sl] * sin) * (MLA_SCALE * LOG2E)
            qt_ref[sl, :] = jnp.transpose(qh).astype(BF16)
        else:
            qm_ref[:, sl] = (qa[:, sl] * MLA_SCALE).astype(BF16)

    fab_ref[...] = _dot(p[:, _C_FU:_C_END].astype(BF16), cs_ref[...]).astype(BF16)

    if lat:
        for j in range(v_mla.shape[1] // LANES):
            sl = slice(j * LANES, (j + 1) * LANES)
            vt_ref[sl, :] = jnp.transpose(v_mla[:, sl]).astype(BF16)
    else:
        vm_ref[...] = v_mla.astype(BF16)
        for hd in range(NA_HEADS):
            sl = slice(hd * NA_HEAD_DIM, (hd + 1) * NA_HEAD_DIM)
            ck_ref[hd] = k_na[:, sl]
            cv_ref[hd] = v_na[:, sl]
        cckv_ref[...] = ckvn
        ckr_ref[...] = kr[:, MLA_NOPE_DIM:MLA_NOPE_DIM + MLA_ROPE_DIM]


def _cond_row(lat, tm, n, ctx_row):
    return (lambda i: (i * tm) // n) if lat else (lambda i: ctx_row)


def _premix(x, mod, layer, n, wts, rope, lat, tm, caches=None, pending=None):
    t, d = x.shape
    if lat:
        cos_t, sin_t = rope
        wv, vone = wts["wv_ext"], wts["vone_ext"]
    else:
        cos_t = sin_t = jnp.zeros((8, LANES), F32)
        wv, vone = wts["wv"], jnp.zeros((1, MLA_HEADS * MLA_V_DIM), F32)
    vw = wv.shape[-1]
    qw = MLA_HEADS * MLA_QK_PAD
    tiles_per_seq = n // tm
    cond_row = _cond_row(lat, tm, n, mod.shape[1] - 1)
    const = lambda *_: (0, 0)
    lsel = lambda *_: (layer, 0, 0)
    rope_spec = (pl.BlockSpec((tm, LANES), lambda i: (i % tiles_per_seq, 0)) if lat
                 else pl.BlockSpec((8, LANES), const))
    in_specs = [
        pl.BlockSpec((tm, d), lambda i: (i, 0)),
        pl.BlockSpec((None, None, 6, d), lambda i: (layer, cond_row(i), 0, 0)),
        pl.BlockSpec((None, 1, d), lsel),
        pl.BlockSpec((None, d, _C_END), lsel),
        pl.BlockSpec((None, 1, MLA_Q_LORA), lsel),
        pl.BlockSpec((None, MLA_Q_LORA, qw), lsel),
        pl.BlockSpec((None, MLA_Q_LORA, qw), lsel),
        pl.BlockSpec((None, 1, MLA_KV_LORA), lsel),
        pl.BlockSpec((None, MLA_KV_LORA, qw), lsel),
        pl.BlockSpec((None, MLA_KV_LORA, vw), lsel),
        pl.BlockSpec((1, vw), const),
        pl.BlockSpec((FN_WIDTH, 2 * FN_WIDTH), const),
        rope_spec,
        rope_spec,
    ]
    row = lambda w: pl.BlockSpec((tm, w), lambda i: (i, 0))
    widths = [CONV_WIDTH, CONV_WIDTH, NA_WIDTH, NA_WIDTH, NA_WIDTH, qw, 2 * FN_WIDTH]
    out_specs = [row(w) for w in widths]
    out_shape = [jax.ShapeDtypeStruct((t, w), BF16) for w in widths]
    if lat:
        out_specs += [pl.BlockSpec((None, qw, tm), lambda i: (i, 0, 0)),
                      pl.BlockSpec((None, vw, tm), lambda i: (i, 0, 0))]
        out_shape += [jax.ShapeDtypeStruct((t // tm, qw, tm), BF16),
                      jax.ShapeDtypeStruct((t // tm, vw, tm), BF16)]
    else:
        assert tm == n
        b = t // n
        depth = wts["w_in"].shape[0]
        out_specs += [
            row(qw), row(vw),
            pl.BlockSpec((None, None, NA_HEADS, n, NA_HEAD_DIM), lambda i: (i, layer, 0, 0, 0)),
            pl.BlockSpec((None, None, NA_HEADS, n, NA_HEAD_DIM), lambda i: (i, layer, 0, 0, 0)),
            pl.BlockSpec((None, None, n, MLA_KV_LORA), lambda i: (i, layer, 0, 0)),
            pl.BlockSpec((None, None, n, MLA_ROPE_DIM), lambda i: (i, layer, 0, 0)),
        ]
        out_shape += [
            jax.ShapeDtypeStruct((t, qw), BF16), jax.ShapeDtypeStruct((t, vw), BF16),
            jax.ShapeDtypeStruct((b, depth, NA_HEADS, n, NA_HEAD_DIM), F32),
            jax.ShapeDtypeStruct((b, depth, NA_HEADS, n, NA_HEAD_DIM), F32),
            jax.ShapeDtypeStruct((b, depth, n, MLA_KV_LORA), F32),
            jax.ShapeDtypeStruct((b, depth, n, MLA_ROPE_DIM), F32),
        ]
    args = [x, mod, wts["norm1"], wts["w_in"], wts["mla_gq"], wts["wq_a"], wts["wq_b"], wts["mla_gkv"],
            wts["wk_a"], wv, vone, wts["cs_bd"], cos_t, sin_t]
    aliases = {}
    if caches is not None:
        first_cache_out = len(out_shape) - len(caches)
        aliases = {len(args) + j: first_cache_out + j for j in range(len(caches))}
        in_specs += [pl.BlockSpec(memory_space=pl.ANY)] * len(caches)
        args += list(caches)
    if pending is not None:
        gtok, y_tok = pending
        in_specs += [
            row(LANES),
            pl.BlockSpec(y_tok.shape[:2] + (tm, y_tok.shape[3]), lambda i: (0, 0, i, 0)),
            pl.BlockSpec((None, None, 6, d), lambda i: (layer - 1, cond_row(i), 0, 0)),
        ]
        args += [gtok, y_tok, mod]
        out_specs = out_specs + [row(d)]
        out_shape = out_shape + [jax.ShapeDtypeStruct((t, d), F32)]
    return pl.pallas_call(
        functools.partial(_premix_kernel, lat, pending is not None, len(args)),
        grid=(t // tm,),
        in_specs=in_specs,
        out_specs=out_specs,
        out_shape=out_shape,
        input_output_aliases=aliases,
        compiler_params=_params("parallel"),
        name="premix_lat" if lat else "premix_ctx",
    )(*args)


def _softmax_attend(q, k, v):
    s = _nt_dot(q, k)
    m = jnp.max(s, axis=-1, keepdims=True)
    p = jnp.exp(s - m)
    l = jnp.sum(p, axis=-1, keepdims=True)
    return _dot(p.astype(BF16), v) / l


def _ctx_attn_kernel(qn_ref, kn_ref, vn_ref, qm_ref, km_ref, vm_ref, yna_ref, ymla_ref):
    for hd in range(NA_HEADS):
        sl = slice(hd * NA_HEAD_DIM, (hd + 1) * NA_HEAD_DIM)
        yna_ref[:, sl] = _softmax_attend(qn_ref[:, sl], kn_ref[:, sl], vn_ref[:, sl]).astype(BF16)
    for hd in range(MLA_HEADS):
        sq = slice(hd * MLA_QK_PAD, (hd + 1) * MLA_QK_PAD)
        sv = slice(hd * MLA_V_DIM, (hd + 1) * MLA_V_DIM)
        ymla_ref[:, sv] = _softmax_attend(qm_ref[:, sq], km_ref[:, sq], vm_ref[:, sv]).astype(BF16)


def _ctx_attention(qn, kn, vn, qm, km, vm, n):
    t = qn.shape[0]
    spec = lambda w: pl.BlockSpec((n, w), lambda b: (b, 0))
    ins = [qn, kn, vn, qm, km, vm]
    return pl.pallas_call(
        _ctx_attn_kernel,
        grid=(t // n,),
        in_specs=[spec(a.shape[1]) for a in ins],
        out_specs=[spec(NA_WIDTH), spec(MLA_HEADS * MLA_V_DIM)],
        out_shape=[jax.ShapeDtypeStruct((t, NA_WIDTH), BF16),
                   jax.ShapeDtypeStruct((t, MLA_HEADS * MLA_V_DIM), BF16)],
        compiler_params=_params("parallel"),
        name="ctx_attention",
    )(*ins)


def _mla_lat_kernel(qt_ref, k_ref, vt_ref, kx_ref, vxt_ref, o_ref, s_scr, p_scr):
    nchunk, _, kc = vt_ref.shape
    tq = qt_ref.shape[1]
    sub = MLA_KEY_SUB

    ksl = lambda hd: slice(hd * MLA_QK_PAD, (hd + 1) * MLA_QK_PAD)
    vsl = lambda hd: slice(hd * MLA_V_PAD, (hd + 1) * MLA_V_PAD)

    def scores(slot, k_of, nk):
        cmax = []
        for hd in range(MLA_HEADS):
            qt = qt_ref[ksl(hd), :]
            part = None
            for j in range(0, nk, sub):
                st = _dot(k_of(hd, j), qt)
                s_scr[slot, hd, j:j + sub, :] = st
                blk = jnp.max(st.reshape(sub // 8, 8, tq), axis=0)
                part = blk if part is None else jnp.maximum(part, blk)
            cmax.append(jnp.max(part, axis=0, keepdims=True))
        return tuple(cmax)

    def attend(slot, cmax, state, vt_of, nk):
        new = []
        for hd in range(MLA_HEADS):
            m_i, acc = state[hd]
            m_new = jnp.maximum(m_i, cmax[hd])
            for j in range(0, nk, sub):
                p_scr[hd, j:j + sub, :] = jnp.exp2(s_scr[slot, hd, j:j + sub, :] - m_new).astype(BF16)
            acc = jnp.exp2(m_i - m_new) * acc + _dot(vt_of(hd), p_scr[hd, 0:nk, :])
            new.append((m_new, acc))
        return tuple(new)

    lat_keys = lambda c: (lambda hd, j: k_ref[pl.ds(pl.multiple_of(c * kc, kc) + j, sub), ksl(hd)])
    past = kx_ref.shape[0]
    state = tuple((jnp.full((1, tq), NEG_INF, F32), jnp.zeros((MLA_V_PAD, tq), F32)) for _ in range(MLA_HEADS))
    cmax_ctx = scores(1, lambda hd, j: kx_ref[j:j + sub, ksl(hd)], past)
    cmax = scores(0, lat_keys(0), kc)
    state = attend(1, cmax_ctx, state, lambda hd: vxt_ref[vsl(hd), :], past)

    lat_vals = lambda c: (lambda hd: vt_ref[c, vsl(hd), :])

    def body(i, carry):
        cmax0, state = carry
        c = 2 * i
        cmax1 = scores(1, lat_keys(c + 1), kc)
        state = attend(0, cmax0, state, lat_vals(c), kc)
        cmax0 = scores(0, lat_keys(c + 2), kc)
        state = attend(1, cmax1, state, lat_vals(c + 1), kc)
        return cmax0, state

    cmax, state = lax.fori_loop(0, nchunk // 2 - 1, body, (cmax, state))
    cmax1 = scores(1, lat_keys(nchunk - 1), kc)
    state = attend(0, cmax, state, lat_vals(nchunk - 2), kc)
    state = attend(1, cmax1, state, lat_vals(nchunk - 1), kc)
    o_t = jnp.concatenate([acc[:MLA_V_DIM] / acc[MLA_V_DIM:MLA_V_DIM + 1] for _, acc in state], axis=0)
    o_ref[...] = jnp.transpose(o_t).astype(BF16)


def _mla_lat_attention(qt, km, vt, kx, vxt, layer, n):
    ntile, qw, tq = qt.shape
    t = ntile * tq
    past = kx.shape[2]
    qpb = n // tq
    return pl.pallas_call(
        _mla_lat_kernel,
        grid=(t // n, qpb),
        in_specs=[
            pl.BlockSpec((None, qw, tq), lambda b, i: (b * qpb + i, 0, 0)),
            pl.BlockSpec((n, km.shape[1]), lambda b, i: (b, 0)),
            pl.BlockSpec((qpb, vt.shape[1], tq), lambda b, i: (b, 0, 0)),
            pl.BlockSpec((None, None, past, kx.shape[3]), lambda b, i: (layer, b, 0, 0)),
            pl.BlockSpec((None, None, vxt.shape[2], past), lambda b, i: (layer, b, 0, 0)),
        ],
        out_specs=pl.BlockSpec((tq, MLA_HEADS * MLA_V_DIM), lambda b, i: (b * qpb + i, 0)),
        out_shape=jax.ShapeDtypeStruct((t, MLA_HEADS * MLA_V_DIM), BF16),
        scratch_shapes=[pltpu.VMEM((2, MLA_HEADS, max(tq, past), tq), F32),
                        pltpu.VMEM((MLA_HEADS, max(tq, past), tq), BF16)],
        compiler_params=_params("parallel", "parallel"),
        name="mla_lat_attention",
    )(qt, km, vt, kx, vxt)


def _ctx_kv_kernel(ckv_ref, kr_ref, wka_ref, wv_ref, vone_ref, place_ref, k_ref, vt_ref):
    ckv = ckv_ref[...].astype(BF16)
    k_ref[...] = (_dot(ckv, wka_ref[...]) + _dot(kr_ref[...].astype(BF16), place_ref[...])).astype(BF16)
    v = _dot(ckv, wv_ref[...]) + vone_ref[...]
    for j in range(v.shape[1] // LANES):
        sl = slice(j * LANES, (j + 1) * LANES)
        vt_ref[sl, :] = jnp.transpose(v[:, sl]).astype(BF16)


def _ctx_kv(cache_ckv, cache_krope, wk_a, wv_ext, vone_ext):
    bd, depth, past, _ = cache_ckv.shape
    place = np.zeros((MLA_ROPE_DIM, MLA_HEADS * MLA_QK_PAD), np.float32)
    for hd in range(MLA_HEADS):
        for i in range(MLA_ROPE_DIM):
            place[i, hd * MLA_QK_PAD + MLA_NOPE_DIM + i] = 1.0
    kw, vw = MLA_HEADS * MLA_QK_PAD, MLA_HEADS * MLA_V_PAD
    return pl.pallas_call(
        _ctx_kv_kernel,
        grid=(depth, bd),
        in_specs=[
            pl.BlockSpec((None, None, past, MLA_KV_LORA), lambda l, b: (b, l, 0, 0)),
            pl.BlockSpec((None, None, past, MLA_ROPE_DIM), lambda l, b: (b, l, 0, 0)),
            pl.BlockSpec((None, MLA_KV_LORA, kw), lambda l, b: (l, 0, 0)),
            pl.BlockSpec((None, MLA_KV_LORA, vw), lambda l, b: (l, 0, 0)),
            pl.BlockSpec((1, vw), lambda l, b: (0, 0)),
            pl.BlockSpec((MLA_ROPE_DIM, kw), lambda l, b: (0, 0)),
        ],
        out_specs=[pl.BlockSpec((None, None, past, kw), lambda l, b: (l, b, 0, 0)),
                   pl.BlockSpec((None, None, vw, past), lambda l, b: (l, b, 0, 0))],
        out_shape=[jax.ShapeDtypeStruct((depth, bd, past, kw), BF16),
                   jax.ShapeDtypeStruct((depth, bd, vw, past), BF16)],
        compiler_params=_params("parallel", "parallel"),
        name="ctx_kv",
    )(cache_ckv, cache_krope, wk_a, wv_ext, vone_ext, jnp.asarray(place, BF16))


def _na_tile_geometry(rows):
    last = rows // NA_Q_ROWS - 1
    geo = []
    for j in (0, 1, last):
        r0 = j * NA_Q_ROWS
        geo.append((r0, min(max(r0 - NA_KH // 2, 0), rows - NA_WIN_ROWS)))
    return geo


def _na_bias_kernel(geo, rows, rpb_ref, o_ref):
    l = pl.program_id(0)
    hd = pl.program_id(1)
    base = (l * NA_HEADS + hd) * (2 * NA_KH - 1) * (2 * NA_KW - 1)
    qc = lax.broadcasted_iota(jnp.int32, (GRID_W, GRID_W), 0)
    kcol = lax.broadcasted_iota(jnp.int32, (GRID_W, GRID_W), 1)
    d_col = jnp.clip(kcol - qc + (NA_KW - 1), 0, 2 * NA_KW - 2)
    col_start = jnp.clip(qc - NA_KW // 2, 0, GRID_W - NA_KW)
    in_cols = (kcol >= col_start) & (kcol < col_start + NA_KW)
    neg = jnp.full((GRID_W, GRID_W), NEG_INF, F32)
    tabs = []
    for dr in range(2 * NA_KH - 1):
        acc = jnp.zeros((GRID_W, GRID_W), F32)
        for dc in range(2 * NA_KW - 1):
            acc = jnp.where(d_col == dc, rpb_ref[base + dr * (2 * NA_KW - 1) + dc], acc)
        tabs.append(jnp.where(in_cols, acc, neg))
    for kind, (r0, ws) in enumerate(geo):
        for i in range(NA_Q_ROWS):
            r = r0 + i
            lo = min(max(r - NA_KH // 2, 0), rows - NA_KH)
            for j in range(NA_WIN_ROWS):
                kr = ws + j
                blk = tabs[kr - r + NA_KH - 1] if lo <= kr < lo + NA_KH else neg
                o_ref[kind, i * GRID_W:(i + 1) * GRID_W, j * GRID_W:(j + 1) * GRID_W] = blk


def _na_bias(na_rpb, rows):
    depth = na_rpb.shape[0]
    geo = _na_tile_geometry(rows)
    qn, kn = NA_Q_ROWS * GRID_W, NA_WIN_ROWS * GRID_W
    return pl.pallas_call(
        functools.partial(_na_bias_kernel, geo, rows),
        grid=(depth, NA_HEADS),
        in_specs=[pl.BlockSpec(memory_space=pltpu.SMEM)],
        out_specs=pl.BlockSpec((None, None, 3, qn, kn), lambda l, h: (l, h, 0, 0, 0)),
        out_shape=jax.ShapeDtypeStruct((depth, NA_HEADS, 3, qn, kn), F32),
        compiler_params=_params("parallel", "parallel"),
        name="na_bias",
    )(na_rpb.reshape(-1))


def _na_lat_kernel(rows, q_ref, k_ref, v_ref, kx_ref, vx_ref, bias_ref, o_ref, s_scr, p_scr):
    j = pl.program_id(1)
    ws = jnp.clip(j * NA_Q_ROWS - NA_KH // 2, 0, rows - NA_WIN_ROWS)
    start = pl.multiple_of(ws * GRID_W, GRID_W)
    nk = NA_WIN_ROWS * GRID_W
    heads = [slice(hd * NA_HEAD_DIM, (hd + 1) * NA_HEAD_DIM) for hd in range(NA_HEADS)]
    m = []
    for hd, sl in enumerate(heads):
        q = q_ref[:, sl]
        s_win = _nt_dot(q, k_ref[pl.ds(start, nk), sl]) + bias_ref[hd]
        s_ctx = _nt_dot(q, kx_ref[hd].astype(BF16))
        s_scr[hd, :, :nk] = s_win
        s_scr[hd, :, nk:] = s_ctx
        m.append(jnp.maximum(jnp.max(s_win, axis=-1, keepdims=True), jnp.max(s_ctx, axis=-1, keepdims=True)))
    l = []
    for hd in range(NA_HEADS):
        p = jnp.exp(s_scr[hd] - m[hd])
        l.append(jnp.sum(p, axis=-1, keepdims=True))
        p_scr[hd] = p.astype(BF16)
    for hd, sl in enumerate(heads):
        o = _dot(p_scr[hd, :, :nk], v_ref[pl.ds(start, nk), sl]) + _dot(p_scr[hd, :, nk:], vx_ref[hd].astype(BF16))
        o_ref[:, sl] = (o / l[hd]).astype(BF16)


def _na_lat_attention(qn, kn, vn, cache_k, cache_v, bias, layer, n):
    t = qn.shape[0]
    rows = n // GRID_W
    assert rows % NA_Q_ROWS == 0 and rows >= NA_WIN_ROWS + NA_Q_ROWS
    tiles = rows // NA_Q_ROWS
    tq = NA_Q_ROWS * GRID_W
    past = cache_k.shape[3]

    def kind(b, j):
        return (layer, 0, jnp.where(j == 0, 0, jnp.where(j == tiles - 1, 2, 1)), 0, 0)

    return pl.pallas_call(
        functools.partial(_na_lat_kernel, rows),
        grid=(t // n, tiles),
        in_specs=[
            pl.BlockSpec((tq, NA_WIDTH), lambda b, j: (b * tiles + j, 0)),
            pl.BlockSpec((n, NA_WIDTH), lambda b, j: (b, 0)),
            pl.BlockSpec((n, NA_WIDTH), lambda b, j: (b, 0)),
            pl.BlockSpec((None, None, NA_HEADS, past, NA_HEAD_DIM), lambda b, j: (b, layer, 0, 0, 0)),
            pl.BlockSpec((None, None, NA_HEADS, past, NA_HEAD_DIM), lambda b, j: (b, layer, 0, 0, 0)),
            pl.BlockSpec((None, NA_HEADS, None, tq, NA_WIN_ROWS * GRID_W), kind),
        ],
        out_specs=pl.BlockSpec((tq, NA_WIDTH), lambda b, j: (b * tiles + j, 0)),
        out_shape=jax.ShapeDtypeStruct((t, NA_WIDTH), BF16),
        scratch_shapes=[pltpu.VMEM((NA_HEADS, tq, NA_WIN_ROWS * GRID_W + past), F32),
                        pltpu.VMEM((NA_HEADS, tq, NA_WIN_ROWS * GRID_W + past), BF16)],
        compiler_params=_params("parallel", "parallel"),
        name="na_lat_attention",
    )(qn, kn, vn, cache_k, cache_v, bias)


def _dft_tables(n):
    def thin(j, k, period):
        ang = (2.0 * math.pi / period) * ((j[:, None] * k[None, :]) % period).astype(F32)
        return jnp.cos(ang), jnp.sin(ang)

    k = jnp.arange(n, dtype=jnp.int32)
    scale = float(n) ** -0.5
    if n % 64 == 0 and n > 64:
        n1 = n // 64
        c1, s1 = thin(jnp.arange(n1, dtype=jnp.int32), k, n1)
        c2, s2 = thin(jnp.arange(64, dtype=jnp.int32), k, n)
        c1, s1, c2, s2 = c1[:, None, :], s1[:, None, :], c2[None, :, :], s2[None, :, :]
        cm = (c1 * c2 - s1 * s2).reshape(n, n)
        sm = (s1 * c2 + c1 * s2).reshape(n, n)
    else:
        cm, sm = thin(k, k, n)
    return (cm * scale).astype(BF16), (sm * -scale).astype(BF16)


def _fourier_kernel(c_ref, s_ref, ab_ref, o_ref):
    o_ref[...] = (_dot(c_ref[...], ab_ref[:, :FN_WIDTH]) + _dot(s_ref[...], ab_ref[:, FN_WIDTH:])).astype(BF16)


def _fourier(fab, tables, n, tmf=512):
    t = fab.shape[0]
    tmf = min(tmf, n)
    tiles = n // tmf
    cm, sm = tables
    return pl.pallas_call(
        _fourier_kernel,
        grid=(tiles, t // n),
        in_specs=[
            pl.BlockSpec((tmf, n), lambda i, b: (i, 0)),
            pl.BlockSpec((tmf, n), lambda i, b: (i, 0)),
            pl.BlockSpec((n, 2 * FN_WIDTH), lambda i, b: (b, 0)),
        ],
        out_specs=pl.BlockSpec((tmf, FN_WIDTH), lambda i, b: (b * tiles + i, 0)),
        out_shape=jax.ShapeDtypeStruct((t, FN_WIDTH), BF16),
        compiler_params=_params("parallel", "parallel"),
        name="fourier",
    )(cm, sm, fab)


def _route(s_t, sb_t):
    def top2_sum(v):
        hi1, lo1 = jnp.maximum(v[0], v[1]), jnp.minimum(v[0], v[1])
        hi2, lo2 = jnp.maximum(v[2], v[3]), jnp.minimum(v[2], v[3])
        return jnp.maximum(hi1, hi2) + jnp.maximum(jnp.minimum(hi1, hi2), jnp.maximum(lo1, lo2))

    best = top2_sum(sb_t[0:EXPERTS_PER_GROUP])
    gsel = jnp.zeros_like(best, dtype=jnp.int32)
    for g in range(1, N_EXPERT_GROUPS):
        cand = top2_sum(sb_t[g * EXPERTS_PER_GROUP:(g + 1) * EXPERTS_PER_GROUP])
        better = cand > best
        gsel = jnp.where(better, g, gsel)
        best = jnp.where(better, cand, best)
    chosen = []
    for e in range(N_EXPERTS):
        g = e // EXPERTS_PER_GROUP
        beaten = jnp.zeros_like(gsel)
        for o in range(g * EXPERTS_PER_GROUP, (g + 1) * EXPERTS_PER_GROUP):
            if o == e:
                continue
            ahead = (sb_t[o] > sb_t[e]) | ((sb_t[o] == sb_t[e]) & (o < e))
            beaten = beaten + ahead.astype(jnp.int32)
        chosen.append((gsel == g) & (beaten < 2))
    picked = [jnp.where(chosen[e], s_t[e], 0.0) for e in range(N_EXPERTS)]
    denom = picked[0]
    for e in range(1, N_EXPERTS):
        denom = denom + picked[e]
    return chosen, [pk / denom for pk in picked]


def _pack_pairs(x):
    w = x.shape[1] // 2
    hi = pltpu.bitcast(x[:, :w].astype(BF16).astype(F32), jnp.uint32)
    lo = pltpu.bitcast(x[:, w:].astype(BF16).astype(F32), jnp.uint32)
    return hi | (lo >> 16)


def _mixout_kernel(n, x_ref, mod_ref, ab_ref, z_ref, zp_ref, zn_ref, yna_ref, ymla_ref, g_ref, cw_ref, wfn_ref,
                   wout_ref, g2_ref, wrc_ref, br_ref, x1_ref, h2_ref, route_ref, gtok_ref, cnt_ref):
    tm = x_ref.shape[0]
    i = pl.program_id(0)
    mod = mod_ref[...]
    gate1, shift2, scale2 = mod[2:3], mod[3:4], mod[4:5]

    z = z_ref[...].astype(F32)
    ridx = lax.broadcasted_iota(jnp.int32, z.shape, 0)
    at_start = (i * tm) % n == 0
    at_end = ((i + 1) * tm) % n == 0
    prev_row = jnp.where(at_start, 0.0, zp_ref[7:8, :].astype(F32))
    next_row = jnp.where(at_end, 0.0, zn_ref[0:1, :].astype(F32))
    z_m1 = jnp.where(ridx == 0, prev_row, pltpu.roll(z, 1, axis=0))
    z_p1 = jnp.where(ridx == tm - 1, next_row, pltpu.roll(z, tm - 1, axis=0))
    cw = cw_ref[...]
    y_conv = ab_ref[...].astype(F32) * (z_m1 * cw[0:1] + z * cw[1:2] + z_p1 * cw[2:3])

    y_fn = _dot(g_ref[...], wfn_ref[...])
    cat = jnp.concatenate([y_conv.astype(BF16), yna_ref[...], ymla_ref[...], y_fn.astype(BF16)], axis=-1)
    x1 = x_ref[...] + gate1 * _dot(cat, wout_ref[...])
    x1_ref[...] = x1

    h2 = _rms(x1, g2_ref[...]) * (1.0 + scale2) + shift2
    packed = _pack_pairs(h2)
    piece = packed.shape[1] // MOE_PIECES
    for p in range(MOE_PIECES):
        h2_ref[p] = packed[:, p * piece:(p + 1) * piece]
    h2_hi = h2.astype(BF16)
    h2_lo = (h2 - h2_hi.astype(F32)).astype(BF16)
    both = _dot(h2_hi, wrc_ref[...])
    logits = both[:, :LANES] + (both[:, LANES:] + _dot(h2_lo, wrc_ref[:, :LANES]))
    s = jax.nn.sigmoid(logits)
    s_t = jnp.transpose(s)
    sb_t = jnp.transpose(s + br_ref[...])
    chosen, gates = _route([s_t[e:e + 1] for e in range(N_EXPERTS)], [sb_t[e:e + 1] for e in range(N_EXPERTS)])

    @pl.when(i == 0)
    def _():
        cnt_ref[...] = jnp.zeros(cnt_ref.shape, F32)

    chosen_f = jnp.concatenate([ch.astype(F32) for ch in chosen], axis=0)
    before = lax.broadcasted_iota(jnp.int32, (tm, tm), 0) < lax.broadcasted_iota(jnp.int32, (tm, tm), 1)
    prefix = _dot(chosen_f.astype(BF16), jnp.where(before, 1.0, 0.0).astype(BF16))
    base = cnt_ref[...]
    rank = jnp.concatenate([base] * (tm // LANES), axis=1) + prefix
    cnt_ref[...] = base + jnp.sum(chosen_f, axis=1, keepdims=True)

    zero = jnp.zeros((1, tm), F32)
    seen = zero
    slots = [[zero, zero, zero], [zero, zero, zero]]
    for e in range(N_EXPERTS):
        for k in range(2):
            hit = chosen[e] & (seen == float(k))
            for j, val in enumerate((float(e), gates[e], rank[e:e + 1])):
                slots[k][j] = jnp.where(hit, val, slots[k][j])
        seen = seen + chosen_f[e:e + 1]
    (e_lo, g_lo, r_lo), (e_hi, g_hi, r_hi) = slots
    route_ref[...] = jnp.concatenate([g_lo, g_hi, e_lo, e_hi, r_lo, r_hi, zero, zero], axis=0)
    gates_t = jnp.concatenate([g_lo, g_hi, jnp.zeros((LANES - 2, tm), F32)], axis=0)
    gtok_ref[...] = jnp.transpose(gates_t)


def _mixout(x, mod, layer, n, parts, wts, lat, tm):
    t, d = x.shape
    ab, z, yna, ymla, g = parts
    nblk8 = t // 8
    per8 = tm // 8
    cond_row_of_tile = _cond_row(lat, tm, n, mod.shape[1] - 1)
    const2 = lambda i: (0, 0)
    lsel = lambda i: (layer, 0, 0)
    row = lambda w: pl.BlockSpec((tm, w), lambda i: (i, 0))
    in_specs = [
        row(d),
        pl.BlockSpec((None, None, 6, d), lambda i: (layer, cond_row_of_tile(i), 0, 0)),
        row(CONV_WIDTH),
        row(CONV_WIDTH),
        pl.BlockSpec((8, CONV_WIDTH), lambda i: (jnp.maximum(i * per8 - 1, 0), 0)),
        pl.BlockSpec((8, CONV_WIDTH), lambda i: (jnp.minimum((i + 1) * per8, nblk8 - 1), 0)),
        row(NA_WIDTH),
        row(MLA_HEADS * MLA_V_DIM),
        row(FN_WIDTH),
        pl.BlockSpec((None, 3, CONV_WIDTH), lsel),
        pl.BlockSpec((None, FN_WIDTH, FN_WIDTH), lsel),
        pl.BlockSpec((None, d, d), lsel),
        pl.BlockSpec((None, 1, d), lsel),
        pl.BlockSpec((d, 2 * LANES), const2),
        pl.BlockSpec((1, LANES), const2),
    ]
    out_specs = [
        row(d),
        pl.BlockSpec((MOE_PIECES, tm, d // 2 // MOE_PIECES), lambda i: (0, i, 0)),
        pl.BlockSpec((8, tm), lambda i: (0, i)),
        row(LANES),
        pl.BlockSpec((N_EXPERTS, LANES), const2),
    ]
    out_shape = [
        jax.ShapeDtypeStruct((t, d), F32),
        jax.ShapeDtypeStruct((MOE_PIECES, t, d // 2 // MOE_PIECES), jnp.uint32),
        jax.ShapeDtypeStruct((8, t), F32),
        jax.ShapeDtypeStruct((t, LANES), F32),
        jax.ShapeDtypeStruct((N_EXPERTS, LANES), F32),
    ]
    return pl.pallas_call(
        functools.partial(_mixout_kernel, n),
        grid=(t // tm,),
        in_specs=in_specs,
        out_specs=out_specs,
        out_shape=out_shape,
        compiler_params=_params("arbitrary"),
        name="mixout",
    )(x, mod, ab, z, z, z, yna, ymla, g, wts["conv_w"], wts["w_fn"], wts["w_out"], wts["norm2"],
      wts["wr_cat"], wts["b_router"])


def _slot_positions(route, counts, rb):
    cnt = counts[:, 0].astype(jnp.int32)
    padded = (cnt + rb - 1) // rb * rb
    ends = jnp.cumsum(padded)
    offs = ends - padded
    experts = route[2:4].astype(jnp.int32)
    ranks = route[4:6].astype(jnp.int32)
    pos = ranks
    for e in range(N_EXPERTS):
        pos = pos + jnp.where(experts == e, offs[e], 0)
    nblk = (2 * route.shape[1]) // rb + N_EXPERTS
    starts = jnp.arange(nblk, dtype=jnp.int32) * rb
    blk_expert = jnp.sum((starts[:, None] >= ends[None, :]).astype(jnp.int32), axis=1)
    used = blk_expert < N_EXPERTS
    blk_expert = jnp.where(used, blk_expert, 0)
    valid_end = jnp.sum(jnp.where(blk_expert[:, None] == jnp.arange(N_EXPERTS)[None, :], (offs + cnt)[None, :], 0), axis=1)
    blk_valid = jnp.where(used, jnp.clip(valid_end - starts, 0, rb), 0)
    return pos, jnp.stack([blk_expert, blk_valid])


def _sc_mesh():
    return plsc.VectorSubcoreMesh(core_axis_name="c", subcore_axis_name="s")


def _sc_pipeline(body, nwin, in_specs, out_specs):
    return pltpu.emit_pipeline(body, grid=(nwin,), in_specs=in_specs, out_specs=out_specs,
                               core_axis_name=("c", "s"), dimension_semantics=(pltpu.PARALLEL,))


def _row_scatter(table, idx_a, idx_b, nrows):
    b, w = table.shape
    win = SC_WINDOW
    idx_spec = pl.BlockSpec((1, win), lambda i: (0, i))

    @functools.partial(pl.kernel, out_type=jax.ShapeDtypeStruct((nrows, w), table.dtype), mesh=_sc_mesh(),
                       scratch_types=[])
    def scatter(table_hbm, ia_hbm, ib_hbm, out_hbm):
        def body(rows_vmem, ia_vmem, ib_vmem):
            pltpu.sync_copy(rows_vmem, out_hbm.at[ia_vmem.at[0]])
            pltpu.sync_copy(rows_vmem, out_hbm.at[ib_vmem.at[0]])

        _sc_pipeline(body, b // win, [pl.BlockSpec((win, w), lambda i: (i, 0)), idx_spec, idx_spec], [])(
            table_hbm, ia_hbm, ib_hbm)

    return scatter(table, idx_a.reshape(1, b), idx_b.reshape(1, b))


def _row_gather(table, idx):
    b = idx.shape[0]
    w = table.shape[1]
    win = SC_WINDOW

    @functools.partial(pl.kernel, out_type=jax.ShapeDtypeStruct((b, w), table.dtype), mesh=_sc_mesh(),
                       scratch_types=[])
    def gather(table_hbm, idx_hbm, out_hbm):
        def body(idx_vmem, out_vmem):
            pltpu.sync_copy(table_hbm.at[idx_vmem.at[0]], out_vmem)

        _sc_pipeline(body, b // win, [pl.BlockSpec((1, win), lambda i: (0, i))],
                     [pl.BlockSpec((win, w), lambda i: (i, 0))])(idx_hbm, out_hbm)

    return gather(table, idx.reshape(1, b))


def _ffn_kernel(blk_ref, xs_ref, w13_ref, w2_ref, y_ref):
    i = pl.program_id(0)
    e = blk_ref[0, i]
    nvalid = blk_ref[1, i]

    @pl.when(nvalid > 0)
    def _():
        packed = jnp.concatenate([xs_ref[0], xs_ref[1]], axis=-1)
        live = lax.broadcasted_iota(jnp.int32, packed.shape, 0) < nvalid
        xb = _unpack_pairs(jnp.where(live, packed, jnp.uint32(0))).astype(BF16)
        up = _dot(xb, w13_ref[e])
        a, b = up[:, :EXPERT_FF], up[:, EXPERT_FF:]
        hid = (a * jax.nn.sigmoid(a)) * b
        y = _pack_pairs(_dot(hid.astype(BF16), w2_ref[e]))
        half = y.shape[1] // 2
        y_ref[0] = y[:, :half]
        y_ref[1] = y[:, half:]

    @pl.when(nvalid == 0)
    def _():
        y_ref[...] = jnp.zeros(y_ref.shape, y_ref.dtype)


def _expert_ffn(xs, blk, w13, w2, layer, rb):
    pieces, nrows, w = xs.shape
    d = 2 * pieces * w
    resident = dict(pipeline_mode=pl.Buffered(1))
    used = lambda i, blk: (0, jnp.where(blk[1, i] > 0, i, 0), 0)
    return pl.pallas_call(
        _ffn_kernel,
        grid_spec=pltpu.PrefetchScalarGridSpec(
            num_scalar_prefetch=1,
            grid=(nrows // rb,),
            in_specs=[
                pl.BlockSpec((pieces, rb, w), used),
                pl.BlockSpec((None, N_EXPERTS, d, 2 * EXPERT_FF), lambda i, blk: (layer, 0, 0, 0), **resident),
                pl.BlockSpec((None, N_EXPERTS, EXPERT_FF, d), lambda i, blk: (layer, 0, 0, 0), **resident),
            ],
            out_specs=pl.BlockSpec((pieces, rb, w), lambda i, blk: (0, i, 0)),
        ),
        out_shape=jax.ShapeDtypeStruct(xs.shape, xs.dtype),
        compiler_params=_params("parallel"),
        name="expert_ffn",
    )(blk, xs, w13, w2)


def _combine_kernel(final, x1_ref, gtok_ref, mod_ref, nf_ref, y_ref, o_ref):
    out = _moe_residual(x1_ref[...], gtok_ref, y_ref, mod_ref)
    if final:
        out = _rms(out, nf_ref[...])
    o_ref[...] = out


def _combine(x1, gtok, y_tok, mod, layer, n, norm_f, final, lat, tc=512):
    t, d = x1.shape
    cond_row_of_tile = _cond_row(lat, tc, n, mod.shape[1] - 1)
    row = lambda w: pl.BlockSpec((tc, w), lambda i: (i, 0))
    return pl.pallas_call(
        functools.partial(_combine_kernel, final),
        grid=(t // tc,),
        in_specs=[
            row(d),
            row(LANES),
            pl.BlockSpec((None, None, 6, d), lambda i: (layer, cond_row_of_tile(i), 0, 0)),
            pl.BlockSpec((1, d), lambda i: (0, 0)),
            pl.BlockSpec(y_tok.shape[:2] + (tc, y_tok.shape[3]), lambda i: (0, 0, i, 0)),
        ],
        out_specs=row(d),
        out_shape=jax.ShapeDtypeStruct((t, d), F32),
        compiler_params=_params("parallel"),
        name="combine",
    )(x1, gtok, mod, norm_f, y_tok)


def _moe_layer(x, mod, layer, n, parts, wts, lat, tm, rb):
    t = x.shape[0]
    x1, h2, route, gtok, counts = _mixout(x, mod, layer, n, parts, wts, lat, tm)
    pos, blk = _slot_positions(route, counts, rb)
    pieces, _, w = h2.shape
    nrows = blk.shape[1] * rb
    piece_base = (jnp.arange(pieces, dtype=jnp.int32) * nrows)[:, None]
    idx = [(piece_base + pos[s][None, :]).reshape(-1) for s in range(2)]
    xs = _row_scatter(h2.reshape(pieces * t, w), idx[0], idx[1], pieces * nrows).reshape(pieces, nrows, w)
    y = _expert_ffn(xs, blk, wts["w13"], wts["w2"], layer, rb)
    back = (piece_base[:, :, None] + pos[None, :, :]).reshape(-1)
    y_tok = _row_gather(y.reshape(pieces * nrows, w), back).reshape(pieces, 2, t, w)
    return x1, (gtok, y_tok)


def _swap_halves(w):
    nf = MLA_ROPE_DIM // 4
    idx = np.arange(MLA_ROPE_DIM).reshape(2, 2, nf)[:, ::-1, :].reshape(-1)
    return w[..., idx]


def _pack_weights(w_in, mla_wq_up, mla_wkv_up, w1, w3, w2, w_router, b_router):
    depth, d, _ = w_in.shape
    zeros = lambda w: jnp.zeros((depth, d, w), w_in.dtype)
    w_kr = w_in[..., 1920:1952]
    pad_rope = lambda w: jnp.concatenate([zeros(MLA_NOPE_DIM), w, zeros(MLA_QK_PAD - MLA_NOPE_DIM - MLA_ROPE_DIM)], -1)
    w_main = jnp.concatenate([w_in[..., :1920], pad_rope(w_kr), pad_rope(_swap_halves(w_kr)), w_in[..., 1952:]], -1)

    wq = mla_wq_up.reshape(depth, MLA_Q_LORA, MLA_HEADS, MLA_NOPE_DIM + MLA_ROPE_DIM)
    q_nope, q_rope = wq[..., :MLA_NOPE_DIM], wq[..., MLA_NOPE_DIM:]
    tail = jnp.zeros(q_rope.shape[:-1] + (MLA_QK_PAD - MLA_NOPE_DIM - MLA_ROPE_DIM,), wq.dtype)
    wq_a = jnp.concatenate([q_nope, q_rope, tail], -1).reshape(depth, MLA_Q_LORA, -1)
    wq_b = jnp.concatenate([jnp.zeros_like(q_nope), _swap_halves(q_rope), tail], -1).reshape(depth, MLA_Q_LORA, -1)

    wkv = mla_wkv_up.reshape(depth, MLA_KV_LORA, MLA_HEADS, MLA_NOPE_DIM + MLA_V_DIM)
    k_nope, v_up = wkv[..., :MLA_NOPE_DIM], wkv[..., MLA_NOPE_DIM:]
    k_tail = jnp.zeros(k_nope.shape[:-1] + (MLA_QK_PAD - MLA_NOPE_DIM,), wkv.dtype)
    wk_a = jnp.concatenate([k_nope, k_tail], -1).reshape(depth, MLA_KV_LORA, -1)
    wv = v_up.reshape(depth, MLA_KV_LORA, -1)
    v_tail = jnp.zeros(v_up.shape[:-1] + (MLA_V_PAD - MLA_V_DIM,), wkv.dtype)
    wv_ext = jnp.concatenate([v_up, v_tail], -1).reshape(depth, MLA_KV_LORA, -1)
    vone = np.zeros((1, MLA_HEADS * MLA_V_PAD), np.float32)
    vone[0, MLA_V_DIM::MLA_V_PAD] = 1.0

    wr = jnp.pad(w_router, ((0, 0), (0, LANES - N_EXPERTS)))
    wr_hi = wr.astype(BF16)
    wr_lo = (wr - wr_hi.astype(F32)).astype(BF16)
    return {
        "w_in": w_main.astype(BF16), "wq_a": wq_a.astype(BF16), "wq_b": wq_b.astype(BF16),
        "wk_a": wk_a.astype(BF16), "wv": wv.astype(BF16), "wv_ext": wv_ext.astype(BF16),
        "vone_ext": jnp.asarray(vone),
        "w13": jnp.concatenate([w1, w3], -1).astype(BF16), "w2": w2.astype(BF16),
        "wr_cat": jnp.concatenate([wr_hi, wr_lo], axis=-1),
        "b_router": jnp.pad(b_router, (0, LANES - N_EXPERTS)).reshape(1, LANES).astype(F32),
    }


def _channel_dft():
    c = np.arange(FN_GROUP_DIM)
    ang = 2.0 * np.pi * ((c[:, None] * c[None, :]) % FN_GROUP_DIM) / FN_GROUP_DIM
    out = np.zeros((FN_WIDTH, 2 * FN_WIDTH), np.float32)
    for g in range(FN_GROUPS):
        sl = slice(g * FN_GROUP_DIM, (g + 1) * FN_GROUP_DIM)
        out[sl, sl] = np.cos(ang) * FN_GROUP_DIM ** -0.5
        out[sl, FN_WIDTH + g * FN_GROUP_DIM:FN_WIDTH + (g + 1) * FN_GROUP_DIM] = np.sin(ang) * FN_GROUP_DIM ** -0.5
    return jnp.asarray(out, BF16)


def _rope_tables(n):
    tok = jnp.arange(n)
    pos = jnp.stack([tok // GRID_W, tok % GRID_W], axis=-1).astype(F32)
    nf = MLA_ROPE_DIM // 4
    freqs = ROPE_THETA ** (-jnp.arange(nf, dtype=F32) / nf)
    ang = pos[:, :, None] * freqs
    cos = jnp.broadcast_to(jnp.cos(ang)[:, :, None, :], (n, 2, 2, nf)).reshape(n, MLA_ROPE_DIM)
    sin = jnp.sin(ang)
    sin = jnp.stack([-sin, sin], axis=2).reshape(n, MLA_ROPE_DIM)
    pad = jnp.zeros((n, MLA_QK_PAD - MLA_NOPE_DIM - MLA_ROPE_DIM), F32)
    cos_t = jnp.concatenate([jnp.ones((n, MLA_NOPE_DIM), F32), cos, pad], -1)
    sin_t = jnp.concatenate([jnp.zeros((n, MLA_NOPE_DIM), F32), sin, pad], -1)
    return cos_t, sin_t


def kernel(x_prompt, x_sample, cache_na_k, cache_na_v, cache_mla_ckv, cache_mla_krope, c, c_ctx, w_ada, b_ada,
           norm1, norm2, w_in, conv_w, na_rpb, mla_gq, mla_wq_up, mla_gkv, mla_wkv_up, w_fn, w_out, w_router,
           b_router, w1, w3, w2, norm_f):
    bp, seq, d = x_prompt.shape
    bd, dec_seq, _ = x_sample.shape
    depth = w_in.shape[0]

    wts = _pack_weights(w_in, mla_wq_up, mla_wkv_up, w1, w3, w2, w_router, b_router)
    wts.update({
        "norm1": norm1.reshape(depth, 1, d), "norm2": norm2.reshape(depth, 1, d),
        "mla_gq": mla_gq.reshape(depth, 1, -1), "mla_gkv": mla_gkv.reshape(depth, 1, -1),
        "conv_w": conv_w, "w_fn": w_fn.astype(BF16), "w_out": w_out.astype(BF16),
        "norm_f": norm_f.reshape(1, d), "cs_bd": _channel_dft(),
    })

    cond = jnp.concatenate([c, jnp.zeros((-(bd + 1) % 8, d), c.dtype), c_ctx[None, :]], axis=0)
    mod = _ada_modulation(cond, w_ada, b_ada)

    xp = x_prompt.reshape(bp * seq, d)
    tables = _dft_tables(seq)
    caches = [
        jnp.zeros((bp, depth, NA_HEADS, seq, NA_HEAD_DIM), F32), jnp.zeros((bp, depth, NA_HEADS, seq, NA_HEAD_DIM), F32),
        jnp.zeros((bp, depth, seq, MLA_KV_LORA), F32), jnp.zeros((bp, depth, seq, MLA_ROPE_DIM), F32)]
    pending = None
    for layer in range(depth):
        outs = _premix(xp, mod, layer, seq, wts, None, False, seq, caches, pending)
        ab, z, qn, kn, vn, km, fab, qm, vm = outs[:9]
        caches = outs[9:13]
        if pending is not None:
            xp = outs[13]
        yna, ymla = _ctx_attention(qn, kn, vn, qm, km, vm, seq)
        g = _fourier(fab, tables, seq)
        xp, pending = _moe_layer(xp, mod, layer, seq, (ab, z, yna, ymla, g), wts, False, seq, MOE_ROW_BLOCK_CTX)
    xp = _combine(xp, pending[0], pending[1], mod, depth - 1, seq, wts["norm_f"], True, False, seq)
    new_na_k, new_na_v, new_ckv, new_krope = caches

    xs = x_sample.reshape(bd * dec_seq, d)
    kx, vxt = _ctx_kv(cache_mla_ckv, cache_mla_krope, wts["wk_a"], wts["wv_ext"], wts["vone_ext"])
    na_bias = _na_bias(na_rpb, dec_seq // GRID_W)
    rope = _rope_tables(dec_seq)
    tables = _dft_tables(dec_seq)
    pending = None
    for layer in range(depth):
        outs = _premix(xs, mod, layer, dec_seq, wts, rope, True, TM_LAT_PREMIX, None, pending)
        ab, z, qn, kn, vn, km, fab, qt, vt = outs[:9]
        if pending is not None:
            xs = outs[9]
        yna = _na_lat_attention(qn, kn, vn, cache_na_k, cache_na_v, na_bias, layer, dec_seq)
        ymla = _mla_lat_attention(qt, km, vt, kx, vxt, layer, dec_seq)
        g = _fourier(fab, tables, dec_seq)
        xs, pending = _moe_layer(xs, mod, layer, dec_seq, (ab, z, yna, ymla, g), wts, True, TM_LAT_MIXOUT,
                                 MOE_ROW_BLOCK)
    xs = _combine(xs, pending[0], pending[1], mod, depth - 1, dec_seq, wts["norm_f"], True, True, TM_LAT_PREMIX)

    return (xp.reshape(bp, seq, d), xs.reshape(bd, dec_seq, d), new_na_k, new_na_v, new_ckv, new_krope)
```

```python
import functools
import math

import numpy as np
import jax
import jax.numpy as jnp
from jax import lax
from jax.experimental import pallas as pl
from jax.experimental.pallas import tpu as pltpu
from jax.experimental.pallas import tpu_sc as plsc

F32 = jnp.float32
BF16 = jnp.bfloat16

GRID_W = 64
CONV_WIDTH = 256
NA_HEADS = 4
NA_HEAD_DIM = 64
NA_WIDTH = NA_HEADS * NA_HEAD_DIM
NA_KH = 8
NA_KW = 16
MLA_HEADS = 4
MLA_Q_LORA = 256
MLA_KV_LORA = 128
MLA_NOPE_DIM = 64
MLA_ROPE_DIM = 32
MLA_V_DIM = 64
MLA_QK_PAD = 128
MLA_V_PAD = 96
MLA_KEY_SUB = 256
LOG2E = 1.4426950408889634
FN_GROUPS = 4
FN_GROUP_DIM = 64
FN_WIDTH = FN_GROUPS * FN_GROUP_DIM
N_EXPERTS = 16
N_EXPERT_GROUPS = 4
EXPERTS_PER_GROUP = N_EXPERTS // N_EXPERT_GROUPS
EXPERT_FF = 256
ROPE_THETA = 10000.0
EPS = 1e-6
NEG_INF = -1e30
LANES = 128

NA_SCALE = NA_HEAD_DIM ** -0.5
MLA_SCALE = (MLA_NOPE_DIM + MLA_ROPE_DIM) ** -0.5

NA_Q_ROWS = 4
NA_WIN_ROWS = 12

TM_LAT_PREMIX = 512
TM_LAT_MIXOUT = 512
MOE_ROW_BLOCK_CTX = 512
MOE_ROW_BLOCK = 512
SC_WINDOW = 128
MOE_PIECES = 2

VMEM_LIMIT = 56 * 1024 * 1024

_C_AB, _C_AC, _C_AU, _C_Q, _C_K, _C_V, _C_CQ = 0, 256, 512, 768, 1024, 1280, 1536
_C_CKV, _C_KR, _C_KRS, _C_FU, _C_END = 1792, 1920, 2048, 2176, 2432


def _nt_dot(a, b):
    return lax.dot_general(a, b, (((1,), (1,)), ((), ())), preferred_element_type=F32)


def _dot(a, b):
    return jnp.dot(a, b, preferred_element_type=F32)


def _rms(x, g):
    return x * lax.rsqrt(jnp.mean(x * x, axis=-1, keepdims=True) + EPS) * g


def _params(*sem, flags=None):
    return pltpu.CompilerParams(dimension_semantics=sem, vmem_limit_bytes=VMEM_LIMIT, flags=flags)


def _ada_kernel(c_ref, w_ref, b_ref, o_ref):
    cnd = c_ref[...]
    act = cnd * jax.nn.sigmoid(cnd)
    o_ref[...] = _dot(act.astype(BF16), w_ref[...].astype(BF16)) + b_ref[...]


def _ada_modulation(cond, w_ada, b_ada):
    depth, d, six_d = w_ada.shape
    r = cond.shape[0]
    tn = 1024
    out = pl.pallas_call(
        _ada_kernel,
        grid=(depth, six_d // tn),
        in_specs=[
            pl.BlockSpec((r, d), lambda l, j: (0, 0)),
            pl.BlockSpec((None, d, tn), lambda l, j: (l, 0, j)),
            pl.BlockSpec((None, 1, tn), lambda l, j: (l, 0, j)),
        ],
        out_specs=pl.BlockSpec((None, r, tn), lambda l, j: (l, 0, j)),
        out_shape=jax.ShapeDtypeStruct((depth, r, six_d), F32),
        compiler_params=_params("parallel", "parallel"),
        name="ada_modulation",
    )(cond, w_ada, b_ada.reshape(depth, 1, six_d))
    return out.reshape(depth, r, 6, d)


def _unpack_pairs(p):
    hi = pltpu.bitcast(p & jnp.uint32(0xFFFF0000), F32)
    lo = pltpu.bitcast(p << 16, F32)
    return jnp.concatenate([hi, lo], axis=-1)


def _moe_residual(x1, gtok_ref, y_ref, mod_ref):
    g = gtok_ref[...]
    y_lo = _unpack_pairs(jnp.concatenate([y_ref[0, 0], y_ref[1, 0]], axis=-1))
    y_hi = _unpack_pairs(jnp.concatenate([y_ref[0, 1], y_ref[1, 1]], axis=-1))
    return x1 + mod_ref[...][5:6] * (g[:, 0:1] * y_lo + g[:, 1:2] * y_hi)


def _premix_kernel(lat, fused, n_in, *refs):
    (x_ref, mod_ref, g1_ref, w_ref, gq_ref, wqa_ref, wqb_ref, gkv_ref, wka_ref, wv_ref, vone_ref, cs_ref,
     cos_ref, sin_ref) = refs[:14]
    outs = refs[n_in:]
    (ab_ref, z_ref, qn_ref, kn_ref, vn_ref, km_ref, fab_ref) = outs[:7]

    x = x_ref[...]
    if fused:
        gtok_ref, y_ref, modp_ref = refs[n_in - 3:n_in]
        x = _moe_residual(x, gtok_ref, y_ref, modp_ref)
        outs[-1][...] = x
    mod = mod_ref[...]
    h = _rms(x, g1_ref[...]) * (1.0 + mod[1:2]) + mod[0:1]
    p = _dot(h.astype(BF16), w_ref[...])

    ab_ref[...] = p[:, _C_AB:_C_AC].astype(BF16)
    z_ref[...] = (p[:, _C_AC:_C_AU] * p[:, _C_AU:_C_Q]).astype(BF16)
    k_na = p[:, _C_K:_C_V]
    v_na = p[:, _C_V:_C_CQ]
    qn_ref[...] = (p[:, _C_Q:_C_K] * NA_SCALE).astype(BF16)
    kn_ref[...] = k_na.astype(BF16)
    vn_ref[...] = v_na.astype(BF16)

    cqn = _rms(p[:, _C_CQ:_C_CKV], gq_ref[...]).astype(BF16)
    ckvn = _rms(p[:, _C_CKV:_C_KR], gkv_ref[...])
    ckvn_b = ckvn.astype(BF16)
    qa = _dot(cqn, wqa_ref[...])
    kva = _dot(ckvn_b, wka_ref[...])
    v_mla = _dot(ckvn_b, wv_ref[...]) + vone_ref[...]
    kr = p[:, _C_KR:_C_KRS]
    if lat:
        cos = cos_ref[...]
        sin = sin_ref[...]
        qb = _dot(cqn, wqb_ref[...])
        krot = kr * cos + p[:, _C_KRS:_C_FU] * sin
        qt_ref, vt_ref = outs[7:9]
    else:
        krot = kr
        qm_ref, vm_ref, ck_ref, cv_ref, cckv_ref, ckr_ref = outs[7:13]
    for hd in range(MLA_HEADS):
        sl = slice(hd * MLA_QK_PAD, (hd + 1) * MLA_QK_PAD)
        km_ref[:, sl] = (kva[:, sl] + krot).astype(BF16)
        if lat:
            qh = (qa[:, sl] * cos + qb[:, sl] * sin) * (MLA_SCALE * LOG2E)
            qt_ref[sl, :] = jnp.transpose(qh).astype(BF16)
        else:
            qm_ref[:, sl] = (qa[:, sl] * MLA_SCALE).astype(BF16)

    fab_ref[...] = _dot(p[:, _C_FU:_C_END].astype(BF16), cs_ref[...]).astype(BF16)

    if lat:
        for j in range(v_mla.shape[1] // LANES):
            sl = slice(j * LANES, (j + 1) * LANES)
            vt_ref[sl, :] = jnp.transpose(v_mla[:, sl]).astype(BF16)
    else:
        vm_ref[...] = v_mla.astype(BF16)
        for hd in range(NA_HEADS):
            sl = slice(hd * NA_HEAD_DIM, (hd + 1) * NA_HEAD_DIM)
            ck_ref[hd] = k_na[:, sl]
            cv_ref[hd] = v_na[:, sl]
        cckv_ref[...] = ckvn
        ckr_ref[...] = kr[:, MLA_NOPE_DIM:MLA_NOPE_DIM + MLA_ROPE_DIM]


def _cond_row(lat, tm, n, ctx_row):
    return (lambda i: (i * tm) // n) if lat else (lambda i: ctx_row)


def _premix(x, mod, layer, n, wts, rope, lat, tm, caches=None, pending=None):
    t, d = x.shape
    if lat:
        cos_t, sin_t = rope
        wv, vone = wts["wv_ext"], wts["vone_ext"]
    else:
        cos_t = sin_t = jnp.zeros((8, LANES), F32)
        wv, vone = wts["wv"], jnp.zeros((1, MLA_HEADS * MLA_V_DIM), F32)
    vw = wv.shape[-1]
    qw = MLA_HEADS * MLA_QK_PAD
    tiles_per_seq = n // tm
    cond_row = _cond_row(lat, tm, n, mod.shape[1] - 1)
    const = lambda *_: (0, 0)
    lsel = lambda *_: (layer, 0, 0)
    rope_spec = (pl.BlockSpec((tm, LANES), lambda i: (i % tiles_per_seq, 0)) if lat
                 else pl.BlockSpec((8, LANES), const))
    in_specs = [
        pl.BlockSpec((tm, d), lambda i: (i, 0)),
        pl.BlockSpec((None, None, 6, d), lambda i: (layer, cond_row(i), 0, 0)),
        pl.BlockSpec((None, 1, d), lsel),
        pl.BlockSpec((None, d, _C_END), lsel),
        pl.BlockSpec((None, 1, MLA_Q_LORA), lsel),
        pl.BlockSpec((None, MLA_Q_LORA, qw), lsel),
        pl.BlockSpec((None, MLA_Q_LORA, qw), lsel),
        pl.BlockSpec((None, 1, MLA_KV_LORA), lsel),
        pl.BlockSpec((None, MLA_KV_LORA, qw), lsel),
        pl.BlockSpec((None, MLA_KV_LORA, vw), lsel),
        pl.BlockSpec((1, vw), const),
        pl.BlockSpec((FN_WIDTH, 2 * FN_WIDTH), const),
        rope_spec,
        rope_spec,
    ]
    row = lambda w: pl.BlockSpec((tm, w), lambda i: (i, 0))
    widths = [CONV_WIDTH, CONV_WIDTH, NA_WIDTH, NA_WIDTH, NA_WIDTH, qw, 2 * FN_WIDTH]
    out_specs = [row(w) for w in widths]
    out_shape = [jax.ShapeDtypeStruct((t, w), BF16) for w in widths]
    if lat:
        out_specs += [pl.BlockSpec((None, qw, tm), lambda i: (i, 0, 0)),
                      pl.BlockSpec((None, vw, tm), lambda i: (i, 0, 0))]
        out_shape += [jax.ShapeDtypeStruct((t // tm, qw, tm), BF16),
                      jax.ShapeDtypeStruct((t // tm, vw, tm), BF16)]
    else:
        assert tm == n
        b = t // n
        depth = wts["w_in"].shape[0]
        out_specs += [
            row(qw), row(vw),
            pl.BlockSpec((None, None, NA_HEADS, n, NA_HEAD_DIM), lambda i: (i, layer, 0, 0, 0)),
            pl.BlockSpec((None, None, NA_HEADS, n, NA_HEAD_DIM), lambda i: (i, layer, 0, 0, 0)),
            pl.BlockSpec((None, None, n, MLA_KV_LORA), lambda i: (i, layer, 0, 0)),
            pl.BlockSpec((None, None, n, MLA_ROPE_DIM), lambda i: (i, layer, 0, 0)),
        ]
        out_shape += [
            jax.ShapeDtypeStruct((t, qw), BF16), jax.ShapeDtypeStruct((t, vw), BF16),
            jax.ShapeDtypeStruct((b, depth, NA_HEADS, n, NA_HEAD_DIM), F32),
            jax.ShapeDtypeStruct((b, depth, NA_HEADS, n, NA_HEAD_DIM), F32),
            jax.ShapeDtypeStruct((b, depth, n, MLA_KV_LORA), F32),
            jax.ShapeDtypeStruct((b, depth, n, MLA_ROPE_DIM), F32),
        ]
    args = [x, mod, wts["norm1"], wts["w_in"], wts["mla_gq"], wts["wq_a"], wts["wq_b"], wts["mla_gkv"],
            wts["wk_a"], wv, vone, wts["cs_bd"], cos_t, sin_t]
    aliases = {}
    if caches is not None:
        first_cache_out = len(out_shape) - len(caches)
        aliases = {len(args) + j: first_cache_out + j for j in range(len(caches))}
        in_specs += [pl.BlockSpec(memory_space=pl.ANY)] * len(caches)
        args += list(caches)
    if pending is not None:
        gtok, y_tok = pending
        in_specs += [
            row(LANES),
            pl.BlockSpec(y_tok.shape[:2] + (tm, y_tok.shape[3]), lambda i: (0, 0, i, 0)),
            pl.BlockSpec((None, None, 6, d), lambda i: (layer - 1, cond_row(i), 0, 0)),
        ]
        args += [gtok, y_tok, mod]
        out_specs = out_specs + [row(d)]
        out_shape = out_shape + [jax.ShapeDtypeStruct((t, d), F32)]
    return pl.pallas_call(
        functools.partial(_premix_kernel, lat, pending is not None, len(args)),
        grid=(t // tm,),
        in_specs=in_specs,
        out_specs=out_specs,
        out_shape=out_shape,
        input_output_aliases=aliases,
        compiler_params=_params("parallel"),
        name="premix_lat" if lat else "premix_ctx",
    )(*args)


def _softmax_attend(q, k, v):
    s = _nt_dot(q, k)
    m = jnp.max(s, axis=-1, keepdims=True)
    p = jnp.exp(s - m)
    l = jnp.sum(p, axis=-1, keepdims=True)
    return _dot(p.astype(BF16), v) / l


def _ctx_attn_kernel(qn_ref, kn_ref, vn_ref, qm_ref, km_ref, vm_ref, yna_ref, ymla_ref):
    for hd in range(NA_HEADS):
        sl = slice(hd * NA_HEAD_DIM, (hd + 1) * NA_HEAD_DIM)
        yna_ref[:, sl] = _softmax_attend(qn_ref[:, sl], kn_ref[:, sl], vn_ref[:, sl]).astype(BF16)
    for hd in range(MLA_HEADS):
        sq = slice(hd * MLA_QK_PAD, (hd + 1) * MLA_QK_PAD)
        sv = slice(hd * MLA_V_DIM, (hd + 1) * MLA_V_DIM)
        ymla_ref[:, sv] = _softmax_attend(qm_ref[:, sq], km_ref[:, sq], vm_ref[:, sv]).astype(BF16)


def _ctx_attention(qn, kn, vn, qm, km, vm, n):
    t = qn.shape[0]
    spec = lambda w: pl.BlockSpec((n, w), lambda b: (b, 0))
    ins = [qn, kn, vn, qm, km, vm]
    return pl.pallas_call(
        _ctx_attn_kernel,
        grid=(t // n,),
        in_specs=[spec(a.shape[1]) for a in ins],
        out_specs=[spec(NA_WIDTH), spec(MLA_HEADS * MLA_V_DIM)],
        out_shape=[jax.ShapeDtypeStruct((t, NA_WIDTH), BF16),
                   jax.ShapeDtypeStruct((t, MLA_HEADS * MLA_V_DIM), BF16)],
        compiler_params=_params("parallel"),
        name="ctx_attention",
    )(*ins)


def _mla_tiles_per_chunk(ntile):
    return 2 if ntile % 4 == 0 else 1


def _mla_lat_kernel(qt_ref, k_ref, vt_ref, kx_ref, vxt_ref, o_ref, s_scr, p_scr):
    ntile, _, tile = vt_ref.shape
    group = _mla_tiles_per_chunk(ntile)
    nchunk, kc = ntile // group, group * tile
    tq = qt_ref.shape[1]
    sub = MLA_KEY_SUB

    ksl = lambda hd: slice(hd * MLA_QK_PAD, (hd + 1) * MLA_QK_PAD)
    vsl = lambda hd: slice(hd * MLA_V_PAD, (hd + 1) * MLA_V_PAD)

    def scores(slot, k_of, nk):
        cmax = []
        for hd in range(MLA_HEADS):
            qt = qt_ref[ksl(hd), :]
            part = None
            for j in range(0, nk, sub):
                st = _dot(k_of(hd, j), qt)
                s_scr[slot, hd, j:j + sub, :] = st
                blk = jnp.max(st.reshape(sub // 8, 8, tq), axis=0)
                part = blk if part is None else jnp.maximum(part, blk)
            cmax.append(jnp.max(part, axis=0, keepdims=True))
        return tuple(cmax)

    def attend(slot, cmax, state, vt_of, nk):
        new = []
        for hd in range(MLA_HEADS):
            m_i, acc = state[hd]
            m_new = jnp.maximum(m_i, cmax[hd])
            for j in range(0, nk, sub):
                p_scr[hd, j:j + sub, :] = jnp.exp2(s_scr[slot, hd, j:j + sub, :] - m_new).astype(BF16)
            acc = jnp.exp2(m_i - m_new) * acc + _dot(vt_of(hd), p_scr[hd, 0:nk, :])
            new.append((m_new, acc))
        return tuple(new)

    lat_keys = lambda c: (lambda hd, j: k_ref[pl.ds(pl.multiple_of(c * kc, kc) + j, sub), ksl(hd)])
    past = kx_ref.shape[0]
    state = tuple((jnp.full((1, tq), NEG_INF, F32), jnp.zeros((MLA_V_PAD, tq), F32)) for _ in range(MLA_HEADS))
    cmax_ctx = scores(1, lambda hd, j: kx_ref[j:j + sub, ksl(hd)], past)
    cmax = scores(0, lat_keys(0), kc)
    state = attend(1, cmax_ctx, state, lambda hd: vxt_ref[vsl(hd), :], past)

    lat_vals = lambda c: (lambda hd: jnp.concatenate(
        [vt_ref[c * group + u, vsl(hd), :] for u in range(group)], axis=-1))

    def body(i, carry):
        cmax0, state = carry
        c = 2 * i
        cmax1 = scores(1, lat_keys(c + 1), kc)
        state = attend(0, cmax0, state, lat_vals(c), kc)
        cmax0 = scores(0, lat_keys(c + 2), kc)
        state = attend(1, cmax1, state, lat_vals(c + 1), kc)
        return cmax0, state

    cmax, state = lax.fori_loop(0, nchunk // 2 - 1, body, (cmax, state))
    cmax1 = scores(1, lat_keys(nchunk - 1), kc)
    state = attend(0, cmax, state, lat_vals(nchunk - 2), kc)
    state = attend(1, cmax1, state, lat_vals(nchunk - 1), kc)
    o_t = jnp.concatenate([acc[:MLA_V_DIM] / acc[MLA_V_DIM:MLA_V_DIM + 1] for _, acc in state], axis=0)
    o_ref[...] = jnp.transpose(o_t).astype(BF16)


def _mla_lat_attention(qt, km, vt, kx, vxt, layer, n):
    ntile, qw, tq = qt.shape
    t = ntile * tq
    past = kx.shape[2]
    qpb = n // tq
    return pl.pallas_call(
        _mla_lat_kernel,
        grid=(t // n, qpb),
        in_specs=[
            pl.BlockSpec((None, qw, tq), lambda b, i: (b * qpb + i, 0, 0)),
            pl.BlockSpec((n, km.shape[1]), lambda b, i: (b, 0)),
            pl.BlockSpec((qpb, vt.shape[1], tq), lambda b, i: (b, 0, 0)),
            pl.BlockSpec((None, None, past, kx.shape[3]), lambda b, i: (layer, b, 0, 0)),
            pl.BlockSpec((None, None, vxt.shape[2], past), lambda b, i: (layer, b, 0, 0)),
        ],
        out_specs=pl.BlockSpec((tq, MLA_HEADS * MLA_V_DIM), lambda b, i: (b * qpb + i, 0)),
        out_shape=jax.ShapeDtypeStruct((t, MLA_HEADS * MLA_V_DIM), BF16),
        scratch_shapes=[pltpu.VMEM((2, MLA_HEADS, max(tq * _mla_tiles_per_chunk(qpb), past), tq), F32),
                        pltpu.VMEM((MLA_HEADS, max(tq * _mla_tiles_per_chunk(qpb), past), tq), BF16)],
        compiler_params=_params("parallel", "parallel"),
        name="mla_lat_attention",
    )(qt, km, vt, kx, vxt)


def _ctx_kv_kernel(ckv_ref, kr_ref, wka_ref, wv_ref, vone_ref, place_ref, k_ref, vt_ref):
    ckv = ckv_ref[...].astype(BF16)
    k_ref[...] = (_dot(ckv, wka_ref[...]) + _dot(kr_ref[...].astype(BF16), place_ref[...])).astype(BF16)
    v = _dot(ckv, wv_ref[...]) + vone_ref[...]
    for j in range(v.shape[1] // LANES):
        sl = slice(j * LANES, (j + 1) * LANES)
        vt_ref[sl, :] = jnp.transpose(v[:, sl]).astype(BF16)


def _ctx_kv(cache_ckv, cache_krope, wk_a, wv_ext, vone_ext):
    bd, depth, past, _ = cache_ckv.shape
    place = np.zeros((MLA_ROPE_DIM, MLA_HEADS * MLA_QK_PAD), np.float32)
    for hd in range(MLA_HEADS):
        for i in range(MLA_ROPE_DIM):
            place[i, hd * MLA_QK_PAD + MLA_NOPE_DIM + i] = 1.0
    kw, vw = MLA_HEADS * MLA_QK_PAD, MLA_HEADS * MLA_V_PAD
    return pl.pallas_call(
        _ctx_kv_kernel,
        grid=(depth, bd),
        in_specs=[
            pl.BlockSpec((None, None, past, MLA_KV_LORA), lambda l, b: (b, l, 0, 0)),
            pl.BlockSpec((None, None, past, MLA_ROPE_DIM), lambda l, b: (b, l, 0, 0)),
            pl.BlockSpec((None, MLA_KV_LORA, kw), lambda l, b: (l, 0, 0)),
            pl.BlockSpec((None, MLA_KV_LORA, vw), lambda l, b: (l, 0, 0)),
            pl.BlockSpec((1, vw), lambda l, b: (0, 0)),
            pl.BlockSpec((MLA_ROPE_DIM, kw), lambda l, b: (0, 0)),
        ],
        out_specs=[pl.BlockSpec((None, None, past, kw), lambda l, b: (l, b, 0, 0)),
                   pl.BlockSpec((None, None, vw, past), lambda l, b: (l, b, 0, 0))],
        out_shape=[jax.ShapeDtypeStruct((depth, bd, past, kw), BF16),
                   jax.ShapeDtypeStruct((depth, bd, vw, past), BF16)],
        compiler_params=_params("parallel", "parallel"),
        name="ctx_kv",
    )(cache_ckv, cache_krope, wk_a, wv_ext, vone_ext, jnp.asarray(place, BF16))


def _na_tile_geometry(rows):
    last = rows // NA_Q_ROWS - 1
    geo = []
    for j in (0, 1, last):
        r0 = j * NA_Q_ROWS
        geo.append((r0, min(max(r0 - NA_KH // 2, 0), rows - NA_WIN_ROWS)))
    return geo


def _na_bias_kernel(geo, rows, rpb_ref, o_ref):
    l = pl.program_id(0)
    hd = pl.program_id(1)
    base = (l * NA_HEADS + hd) * (2 * NA_KH - 1) * (2 * NA_KW - 1)
    qc = lax.broadcasted_iota(jnp.int32, (GRID_W, GRID_W), 0)
    kcol = lax.broadcasted_iota(jnp.int32, (GRID_W, GRID_W), 1)
    d_col = jnp.clip(kcol - qc + (NA_KW - 1), 0, 2 * NA_KW - 2)
    col_start = jnp.clip(qc - NA_KW // 2, 0, GRID_W - NA_KW)
    in_cols = (kcol >= col_start) & (kcol < col_start + NA_KW)
    neg = jnp.full((GRID_W, GRID_W), NEG_INF, F32)
    tabs = []
    for dr in range(2 * NA_KH - 1):
        acc = jnp.zeros((GRID_W, GRID_W), F32)
        for dc in range(2 * NA_KW - 1):
            acc = jnp.where(d_col == dc, rpb_ref[base + dr * (2 * NA_KW - 1) + dc], acc)
        tabs.append(jnp.where(in_cols, acc, neg))
    for kind, (r0, ws) in enumerate(geo):
        for i in range(NA_Q_ROWS):
            r = r0 + i
            lo = min(max(r - NA_KH // 2, 0), rows - NA_KH)
            for j in range(NA_WIN_ROWS):
                kr = ws + j
                blk = tabs[kr - r + NA_KH - 1] if lo <= kr < lo + NA_KH else neg
                o_ref[kind, i * GRID_W:(i + 1) * GRID_W, j * GRID_W:(j + 1) * GRID_W] = blk


def _na_bias(na_rpb, rows):
    depth = na_rpb.shape[0]
    geo = _na_tile_geometry(rows)
    qn, kn = NA_Q_ROWS * GRID_W, NA_WIN_ROWS * GRID_W
    return pl.pallas_call(
        functools.partial(_na_bias_kernel, geo, rows),
        grid=(depth, NA_HEADS),
        in_specs=[pl.BlockSpec(memory_space=pltpu.SMEM)],
        out_specs=pl.BlockSpec((None, None, 3, qn, kn), lambda l, h: (l, h, 0, 0, 0)),
        out_shape=jax.ShapeDtypeStruct((depth, NA_HEADS, 3, qn, kn), F32),
        compiler_params=_params("parallel", "parallel"),
        name="na_bias",
    )(na_rpb.reshape(-1))


def _na_lat_kernel(rows, q_ref, k_ref, v_ref, kx_ref, vx_ref, bias_ref, o_ref, s_scr, p_scr):
    j = pl.program_id(1)
    ws = jnp.clip(j * NA_Q_ROWS - NA_KH // 2, 0, rows - NA_WIN_ROWS)
    start = pl.multiple_of(ws * GRID_W, GRID_W)
    nk = NA_WIN_ROWS * GRID_W
    heads = [slice(hd * NA_HEAD_DIM, (hd + 1) * NA_HEAD_DIM) for hd in range(NA_HEADS)]
    m = []
    for hd, sl in enumerate(heads):
        q = q_ref[:, sl]
        s_win = _nt_dot(q, k_ref[pl.ds(start, nk), sl]) + bias_ref[hd]
        s_ctx = _nt_dot(q, kx_ref[hd].astype(BF16))
        s_scr[hd, :, :nk] = s_win
        s_scr[hd, :, nk:] = s_ctx
        m.append(jnp.maximum(jnp.max(s_win, axis=-1, keepdims=True), jnp.max(s_ctx, axis=-1, keepdims=True)))
    l = []
    for hd in range(NA_HEADS):
        p = jnp.exp(s_scr[hd] - m[hd])
        l.append(jnp.sum(p, axis=-1, keepdims=True))
        p_scr[hd] = p.astype(BF16)
    for hd, sl in enumerate(heads):
        o = _dot(p_scr[hd, :, :nk], v_ref[pl.ds(start, nk), sl]) + _dot(p_scr[hd, :, nk:], vx_ref[hd].astype(BF16))
        o_ref[:, sl] = (o / l[hd]).astype(BF16)


def _na_lat_attention(qn, kn, vn, cache_k, cache_v, bias, layer, n):
    t = qn.shape[0]
    rows = n // GRID_W
    assert rows % NA_Q_ROWS == 0 and rows >= NA_WIN_ROWS + NA_Q_ROWS
    tiles = rows // NA_Q_ROWS
    tq = NA_Q_ROWS * GRID_W
    past = cache_k.shape[3]

    def kind(b, j):
        return (layer, 0, jnp.where(j == 0, 0, jnp.where(j == tiles - 1, 2, 1)), 0, 0)

    return pl.pallas_call(
        functools.partial(_na_lat_kernel, rows),
        grid=(t // n, tiles),
        in_specs=[
            pl.BlockSpec((tq, NA_WIDTH), lambda b, j: (b * tiles + j, 0)),
            pl.BlockSpec((n, NA_WIDTH), lambda b, j: (b, 0)),
            pl.BlockSpec((n, NA_WIDTH), lambda b, j: (b, 0)),
            pl.BlockSpec((None, None, NA_HEADS, past, NA_HEAD_DIM), lambda b, j: (b, layer, 0, 0, 0)),
            pl.BlockSpec((None, None, NA_HEADS, past, NA_HEAD_DIM), lambda b, j: (b, layer, 0, 0, 0)),
            pl.BlockSpec((None, NA_HEADS, None, tq, NA_WIN_ROWS * GRID_W), kind),
        ],
        out_specs=pl.BlockSpec((tq, NA_WIDTH), lambda b, j: (b * tiles + j, 0)),
        out_shape=jax.ShapeDtypeStruct((t, NA_WIDTH), BF16),
        scratch_shapes=[pltpu.VMEM((NA_HEADS, tq, NA_WIN_ROWS * GRID_W + past), F32),
                        pltpu.VMEM((NA_HEADS, tq, NA_WIN_ROWS * GRID_W + past), BF16)],
        compiler_params=_params("parallel", "parallel"),
        name="na_lat_attention",
    )(qn, kn, vn, cache_k, cache_v, bias)


def _dft_tables(n):
    def thin(j, k, period):
        ang = (2.0 * math.pi / period) * ((j[:, None] * k[None, :]) % period).astype(F32)
        return jnp.cos(ang), jnp.sin(ang)

    k = jnp.arange(n, dtype=jnp.int32)
    scale = float(n) ** -0.5
    if n % 64 == 0 and n > 64:
        n1 = n // 64
        c1, s1 = thin(jnp.arange(n1, dtype=jnp.int32), k, n1)
        c2, s2 = thin(jnp.arange(64, dtype=jnp.int32), k, n)
        c1, s1, c2, s2 = c1[:, None, :], s1[:, None, :], c2[None, :, :], s2[None, :, :]
        cm = (c1 * c2 - s1 * s2).reshape(n, n)
        sm = (s1 * c2 + c1 * s2).reshape(n, n)
    else:
        cm, sm = thin(k, k, n)
    return (cm * scale).astype(BF16), (sm * -scale).astype(BF16)


def _fourier_kernel(c_ref, s_ref, ab_ref, o_ref):
    o_ref[...] = (_dot(c_ref[...], ab_ref[:, :FN_WIDTH]) + _dot(s_ref[...], ab_ref[:, FN_WIDTH:])).astype(BF16)


def _fourier(fab, tables, n, tmf=512):
    t = fab.shape[0]
    tmf = min(tmf, n)
    tiles = n // tmf
    cm, sm = tables
    return pl.pallas_call(
        _fourier_kernel,
        grid=(tiles, t // n),
        in_specs=[
            pl.BlockSpec((tmf, n), lambda i, b: (i, 0)),
            pl.BlockSpec((tmf, n), lambda i, b: (i, 0)),
            pl.BlockSpec((n, 2 * FN_WIDTH), lambda i, b: (b, 0)),
        ],
        out_specs=pl.BlockSpec((tmf, FN_WIDTH), lambda i, b: (b * tiles + i, 0)),
        out_shape=jax.ShapeDtypeStruct((t, FN_WIDTH), BF16),
        compiler_params=_params("parallel", "parallel"),
        name="fourier",
    )(cm, sm, fab)


def _route(s_t, sb_t):
    def top2_sum(v):
        hi1, lo1 = jnp.maximum(v[0], v[1]), jnp.minimum(v[0], v[1])
        hi2, lo2 = jnp.maximum(v[2], v[3]), jnp.minimum(v[2], v[3])
        return jnp.maximum(hi1, hi2) + jnp.maximum(jnp.minimum(hi1, hi2), jnp.maximum(lo1, lo2))

    best = top2_sum(sb_t[0:EXPERTS_PER_GROUP])
    gsel = jnp.zeros_like(best, dtype=jnp.int32)
    for g in range(1, N_EXPERT_GROUPS):
        cand = top2_sum(sb_t[g * EXPERTS_PER_GROUP:(g + 1) * EXPERTS_PER_GROUP])
        better = cand > best
        gsel = jnp.where(better, g, gsel)
        best = jnp.where(better, cand, best)
    chosen = []
    for e in range(N_EXPERTS):
        g = e // EXPERTS_PER_GROUP
        beaten = jnp.zeros_like(gsel)
        for o in range(g * EXPERTS_PER_GROUP, (g + 1) * EXPERTS_PER_GROUP):
            if o == e:
                continue
            ahead = (sb_t[o] > sb_t[e]) | ((sb_t[o] == sb_t[e]) & (o < e))
            beaten = beaten + ahead.astype(jnp.int32)
        chosen.append((gsel == g) & (beaten < 2))
    picked = [jnp.where(chosen[e], s_t[e], 0.0) for e in range(N_EXPERTS)]
    denom = picked[0]
    for e in range(1, N_EXPERTS):
        denom = denom + picked[e]
    return chosen, [pk / denom for pk in picked]


def _pack_pairs(x):
    w = x.shape[1] // 2
    hi = pltpu.bitcast(x[:, :w].astype(BF16).astype(F32), jnp.uint32)
    lo = pltpu.bitcast(x[:, w:].astype(BF16).astype(F32), jnp.uint32)
    return hi | (lo >> 16)


def _mixout_kernel(n, x_ref, mod_ref, ab_ref, z_ref, zp_ref, zn_ref, yna_ref, ymla_ref, g_ref, cw_ref, wfn_ref,
                   wout_ref, g2_ref, wrc_ref, br_ref, x1_ref, h2_ref, route_ref, gtok_ref, cnt_ref):
    tm = x_ref.shape[0]
    i = pl.program_id(0)
    mod = mod_ref[...]
    gate1, shift2, scale2 = mod[2:3], mod[3:4], mod[4:5]

    z = z_ref[...].astype(F32)
    ridx = lax.broadcasted_iota(jnp.int32, z.shape, 0)
    at_start = (i * tm) % n == 0
    at_end = ((i + 1) * tm) % n == 0
    prev_row = jnp.where(at_start, 0.0, zp_ref[7:8, :].astype(F32))
    next_row = jnp.where(at_end, 0.0, zn_ref[0:1, :].astype(F32))
    z_m1 = jnp.where(ridx == 0, prev_row, pltpu.roll(z, 1, axis=0))
    z_p1 = jnp.where(ridx == tm - 1, next_row, pltpu.roll(z, tm - 1, axis=0))
    cw = cw_ref[...]
    y_conv = ab_ref[...].astype(F32) * (z_m1 * cw[0:1] + z * cw[1:2] + z_p1 * cw[2:3])

    y_fn = _dot(g_ref[...], wfn_ref[...])
    cat = jnp.concatenate([y_conv.astype(BF16), yna_ref[...], ymla_ref[...], y_fn.astype(BF16)], axis=-1)
    x1 = x_ref[...] + gate1 * _dot(cat, wout_ref[...])
    x1_ref[...] = x1

    h2 = _rms(x1, g2_ref[...]) * (1.0 + scale2) + shift2
    packed = _pack_pairs(h2)
    piece = packed.shape[1] // MOE_PIECES
    for p in range(MOE_PIECES):
        h2_ref[p] = packed[:, p * piece:(p + 1) * piece]
    h2_hi = h2.astype(BF16)
    h2_lo = (h2 - h2_hi.astype(F32)).astype(BF16)
    both = _dot(h2_hi, wrc_ref[...])
    logits = both[:, :LANES] + (both[:, LANES:] + _dot(h2_lo, wrc_ref[:, :LANES]))
    s = jax.nn.sigmoid(logits)
    s_t = jnp.transpose(s)
    sb_t = jnp.transpose(s + br_ref[...])
    chosen, gates = _route([s_t[e:e + 1] for e in range(N_EXPERTS)], [sb_t[e:e + 1] for e in range(N_EXPERTS)])

    @pl.when(i == 0)
    def _():
        cnt_ref[...] = jnp.zeros(cnt_ref.shape, F32)

    chosen_f = jnp.concatenate([ch.astype(F32) for ch in chosen], axis=0)
    before = lax.broadcasted_iota(jnp.int32, (tm, tm), 0) < lax.broadcasted_iota(jnp.int32, (tm, tm), 1)
    prefix = _dot(chosen_f.astype(BF16), jnp.where(before, 1.0, 0.0).astype(BF16))
    base = cnt_ref[...]
    rank = jnp.concatenate([base] * (tm // LANES), axis=1) + prefix
    cnt_ref[...] = base + jnp.sum(chosen_f, axis=1, keepdims=True)

    zero = jnp.zeros((1, tm), F32)
    seen = zero
    slots = [[zero, zero, zero], [zero, zero, zero]]
    for e in range(N_EXPERTS):
        for k in range(2):
            hit = chosen[e] & (seen == float(k))
            for j, val in enumerate((float(e), gates[e], rank[e:e + 1])):
                slots[k][j] = jnp.where(hit, val, slots[k][j])
        seen = seen + chosen_f[e:e + 1]
    (e_lo, g_lo, r_lo), (e_hi, g_hi, r_hi) = slots
    route_ref[...] = jnp.concatenate([g_lo, g_hi, e_lo, e_hi, r_lo, r_hi, zero, zero], axis=0)
    gates_t = jnp.concatenate([g_lo, g_hi, jnp.zeros((LANES - 2, tm), F32)], axis=0)
    gtok_ref[...] = jnp.transpose(gates_t)


def _mixout(x, mod, layer, n, parts, wts, lat, tm):
    t, d = x.shape
    ab, z, yna, ymla, g = parts
    nblk8 = t // 8
    per8 = tm // 8
    cond_row_of_tile = _cond_row(lat, tm, n, mod.shape[1] - 1)
    const2 = lambda i: (0, 0)
    lsel = lambda i: (layer, 0, 0)
    row = lambda w: pl.BlockSpec((tm, w), lambda i: (i, 0))
    in_specs = [
        row(d),
        pl.BlockSpec((None, None, 6, d), lambda i: (layer, cond_row_of_tile(i), 0, 0)),
        row(CONV_WIDTH),
        row(CONV_WIDTH),
        pl.BlockSpec((8, CONV_WIDTH), lambda i: (jnp.maximum(i * per8 - 1, 0), 0)),
        pl.BlockSpec((8, CONV_WIDTH), lambda i: (jnp.minimum((i + 1) * per8, nblk8 - 1), 0)),
        row(NA_WIDTH),
        row(MLA_HEADS * MLA_V_DIM),
        row(FN_WIDTH),
        pl.BlockSpec((None, 3, CONV_WIDTH), lsel),
        pl.BlockSpec((None, FN_WIDTH, FN_WIDTH), lsel),
        pl.BlockSpec((None, d, d), lsel),
        pl.BlockSpec((None, 1, d), lsel),
        pl.BlockSpec((d, 2 * LANES), const2),
        pl.BlockSpec((1, LANES), const2),
    ]
    out_specs = [
        row(d),
        pl.BlockSpec((MOE_PIECES, tm, d // 2 // MOE_PIECES), lambda i: (0, i, 0)),
        pl.BlockSpec((8, tm), lambda i: (0, i)),
        row(LANES),
        pl.BlockSpec((N_EXPERTS, LANES), const2),
    ]
    out_shape = [
        jax.ShapeDtypeStruct((t, d), F32),
        jax.ShapeDtypeStruct((MOE_PIECES, t, d // 2 // MOE_PIECES), jnp.uint32),
        jax.ShapeDtypeStruct((8, t), F32),
        jax.ShapeDtypeStruct((t, LANES), F32),
        jax.ShapeDtypeStruct((N_EXPERTS, LANES), F32),
    ]
    return pl.pallas_call(
        functools.partial(_mixout_kernel, n),
        grid=(t // tm,),
        in_specs=in_specs,
        out_specs=out_specs,
        out_shape=out_shape,
        compiler_params=_params("arbitrary"),
        name="mixout",
    )(x, mod, ab, z, z, z, yna, ymla, g, wts["conv_w"], wts["w_fn"], wts["w_out"], wts["norm2"],
      wts["wr_cat"], wts["b_router"])


def _slot_positions(route, counts, rb):
    cnt = counts[:, 0].astype(jnp.int32)
    padded = (cnt + rb - 1) // rb * rb
    ends = jnp.cumsum(padded)
    offs = ends - padded
    experts = route[2:4].astype(jnp.int32)
    ranks = route[4:6].astype(jnp.int32)
    pos = ranks
    for e in range(N_EXPERTS):
        pos = pos + jnp.where(experts == e, offs[e], 0)
    nblk = (2 * route.shape[1]) // rb + N_EXPERTS
    starts = jnp.arange(nblk, dtype=jnp.int32) * rb
    blk_expert = jnp.sum((starts[:, None] >= ends[None, :]).astype(jnp.int32), axis=1)
    used = blk_expert < N_EXPERTS
    blk_expert = jnp.where(used, blk_expert, 0)
    valid_end = jnp.sum(jnp.where(blk_expert[:, None] == jnp.arange(N_EXPERTS)[None, :], (offs + cnt)[None, :], 0), axis=1)
    blk_valid = jnp.where(used, jnp.clip(valid_end - starts, 0, rb), 0)
    return pos, jnp.stack([blk_expert, blk_valid])


def _sc_mesh():
    return plsc.VectorSubcoreMesh(core_axis_name="c", subcore_axis_name="s")


def _sc_pipeline(body, nwin, in_specs, out_specs):
    return pltpu.emit_pipeline(body, grid=(nwin,), in_specs=in_specs, out_specs=out_specs,
                               core_axis_name=("c", "s"), dimension_semantics=(pltpu.PARALLEL,))


def _row_scatter(table, idx_a, idx_b, nrows):
    b, w = table.shape
    win = SC_WINDOW
    idx_spec = pl.BlockSpec((1, win), lambda i: (0, i))

    @functools.partial(pl.kernel, out_type=jax.ShapeDtypeStruct((nrows, w), table.dtype), mesh=_sc_mesh(),
                       scratch_types=[])
    def scatter(table_hbm, ia_hbm, ib_hbm, out_hbm):
        def body(rows_vmem, ia_vmem, ib_vmem):
            pltpu.sync_copy(rows_vmem, out_hbm.at[ia_vmem.at[0]])
            pltpu.sync_copy(rows_vmem, out_hbm.at[ib_vmem.at[0]])

        _sc_pipeline(body, b // win, [pl.BlockSpec((win, w), lambda i: (i, 0)), idx_spec, idx_spec], [])(
            table_hbm, ia_hbm, ib_hbm)

    return scatter(table, idx_a.reshape(1, b), idx_b.reshape(1, b))


def _row_gather(table, idx):
    b = idx.shape[0]
    w = table.shape[1]
    win = SC_WINDOW

    @functools.partial(pl.kernel, out_type=jax.ShapeDtypeStruct((b, w), table.dtype), mesh=_sc_mesh(),
                       scratch_types=[])
    def gather(table_hbm, idx_hbm, out_hbm):
        def body(idx_vmem, out_vmem):
            pltpu.sync_copy(table_hbm.at[idx_vmem.at[0]], out_vmem)

        _sc_pipeline(body, b // win, [pl.BlockSpec((1, win), lambda i: (0, i))],
                     [pl.BlockSpec((win, w), lambda i: (i, 0))])(idx_hbm, out_hbm)

    return gather(table, idx.reshape(1, b))


def _ffn_kernel(blk_ref, xs_ref, w13_ref, w2_ref, y_ref):
    i = pl.program_id(0)
    e = blk_ref[0, i]
    nvalid = blk_ref[1, i]

    @pl.when(nvalid > 0)
    def _():
        packed = jnp.concatenate([xs_ref[0], xs_ref[1]], axis=-1)
        live = lax.broadcasted_iota(jnp.int32, packed.shape, 0) < nvalid
        xb = _unpack_pairs(jnp.where(live, packed, jnp.uint32(0))).astype(BF16)
        up = _dot(xb, w13_ref[e])
        a, b = up[:, :EXPERT_FF], up[:, EXPERT_FF:]
        hid = (a * jax.nn.sigmoid(a)) * b
        y = _pack_pairs(_dot(hid.astype(BF16), w2_ref[e]))
        half = y.shape[1] // 2
        y_ref[0] = y[:, :half]
        y_ref[1] = y[:, half:]

    @pl.when(nvalid == 0)
    def _():
        y_ref[...] = jnp.zeros(y_ref.shape, y_ref.dtype)


def _expert_ffn(xs, blk, w13, w2, layer, rb):
    pieces, nrows, w = xs.shape
    d = 2 * pieces * w
    resident = dict(pipeline_mode=pl.Buffered(1))
    used = lambda i, blk: (0, jnp.where(blk[1, i] > 0, i, 0), 0)
    return pl.pallas_call(
        _ffn_kernel,
        grid_spec=pltpu.PrefetchScalarGridSpec(
            num_scalar_prefetch=1,
            grid=(nrows // rb,),
            in_specs=[
                pl.BlockSpec((pieces, rb, w), used),
                pl.BlockSpec((None, N_EXPERTS, d, 2 * EXPERT_FF), lambda i, blk: (layer, 0, 0, 0), **resident),
                pl.BlockSpec((None, N_EXPERTS, EXPERT_FF, d), lambda i, blk: (layer, 0, 0, 0), **resident),
            ],
            out_specs=pl.BlockSpec((pieces, rb, w), lambda i, blk: (0, i, 0)),
        ),
        out_shape=jax.ShapeDtypeStruct(xs.shape, xs.dtype),
        compiler_params=_params("parallel"),
        name="expert_ffn",
    )(blk, xs, w13, w2)


def _combine_kernel(final, x1_ref, gtok_ref, mod_ref, nf_ref, y_ref, o_ref):
    out = _moe_residual(x1_ref[...], gtok_ref, y_ref, mod_ref)
    if final:
        out = _rms(out, nf_ref[...])
    o_ref[...] = out


def _combine(x1, gtok, y_tok, mod, layer, n, norm_f, final, lat, tc=512):
    t, d = x1.shape
    cond_row_of_tile = _cond_row(lat, tc, n, mod.shape[1] - 1)
    row = lambda w: pl.BlockSpec((tc, w), lambda i: (i, 0))
    return pl.pallas_call(
        functools.partial(_combine_kernel, final),
        grid=(t // tc,),
        in_specs=[
            row(d),
            row(LANES),
            pl.BlockSpec((None, None, 6, d), lambda i: (layer, cond_row_of_tile(i), 0, 0)),
            pl.BlockSpec((1, d), lambda i: (0, 0)),
            pl.BlockSpec(y_tok.shape[:2] + (tc, y_tok.shape[3]), lambda i: (0, 0, i, 0)),
        ],
        out_specs=row(d),
        out_shape=jax.ShapeDtypeStruct((t, d), F32),
        compiler_params=_params("parallel"),
        name="combine",
    )(x1, gtok, mod, norm_f, y_tok)


def _moe_layer(x, mod, layer, n, parts, wts, lat, tm, rb):
    t = x.shape[0]
    x1, h2, route, gtok, counts = _mixout(x, mod, layer, n, parts, wts, lat, tm)
    pos, blk = _slot_positions(route, counts, rb)
    pieces, _, w = h2.shape
    nrows = blk.shape[1] * rb
    piece_base = (jnp.arange(pieces, dtype=jnp.int32) * nrows)[:, None]
    idx = [(piece_base + pos[s][None, :]).reshape(-1) for s in range(2)]
    xs = _row_scatter(h2.reshape(pieces * t, w), idx[0], idx[1], pieces * nrows).reshape(pieces, nrows, w)
    y = _expert_ffn(xs, blk, wts["w13"], wts["w2"], layer, rb)
    back = (piece_base[:, :, None] + pos[None, :, :]).reshape(-1)
    y_tok = _row_gather(y.reshape(pieces * nrows, w), back).reshape(pieces, 2, t, w)
    return x1, (gtok, y_tok)


def _swap_halves(w):
    nf = MLA_ROPE_DIM // 4
    idx = np.arange(MLA_ROPE_DIM).reshape(2, 2, nf)[:, ::-1, :].reshape(-1)
    return w[..., idx]


def _pack_weights(w_in, mla_wq_up, mla_wkv_up, w1, w3, w2, w_router, b_router):
    depth, d, _ = w_in.shape
    zeros = lambda w: jnp.zeros((depth, d, w), w_in.dtype)
    w_kr = w_in[..., 1920:1952]
    pad_rope = lambda w: jnp.concatenate([zeros(MLA_NOPE_DIM), w, zeros(MLA_QK_PAD - MLA_NOPE_DIM - MLA_ROPE_DIM)], -1)
    w_main = jnp.concatenate([w_in[..., :1920], pad_rope(w_kr), pad_rope(_swap_halves(w_kr)), w_in[..., 1952:]], -1)

    wq = mla_wq_up.reshape(depth, MLA_Q_LORA, MLA_HEADS, MLA_NOPE_DIM + MLA_ROPE_DIM)
    q_nope, q_rope = wq[..., :MLA_NOPE_DIM], wq[..., MLA_NOPE_DIM:]
    tail = jnp.zeros(q_rope.shape[:-1] + (MLA_QK_PAD - MLA_NOPE_DIM - MLA_ROPE_DIM,), wq.dtype)
    wq_a = jnp.concatenate([q_nope, q_rope, tail], -1).reshape(depth, MLA_Q_LORA, -1)
    wq_b = jnp.concatenate([jnp.zeros_like(q_nope), _swap_halves(q_rope), tail], -1).reshape(depth, MLA_Q_LORA, -1)

    wkv = mla_wkv_up.reshape(depth, MLA_KV_LORA, MLA_HEADS, MLA_NOPE_DIM + MLA_V_DIM)
    k_nope, v_up = wkv[..., :MLA_NOPE_DIM], wkv[..., MLA_NOPE_DIM:]
    k_tail = jnp.zeros(k_nope.shape[:-1] + (MLA_QK_PAD - MLA_NOPE_DIM,), wkv.dtype)
    wk_a = jnp.concatenate([k_nope, k_tail], -1).reshape(depth, MLA_KV_LORA, -1)
    wv = v_up.reshape(depth, MLA_KV_LORA, -1)
    v_tail = jnp.zeros(v_up.shape[:-1] + (MLA_V_PAD - MLA_V_DIM,), wkv.dtype)
    wv_ext = jnp.concatenate([v_up, v_tail], -1).reshape(depth, MLA_KV_LORA, -1)
    vone = np.zeros((1, MLA_HEADS * MLA_V_PAD), np.float32)
    vone[0, MLA_V_DIM::MLA_V_PAD] = 1.0

    wr = jnp.pad(w_router, ((0, 0), (0, LANES - N_EXPERTS)))
    wr_hi = wr.astype(BF16)
    wr_lo = (wr - wr_hi.astype(F32)).astype(BF16)
    return {
        "w_in": w_main.astype(BF16), "wq_a": wq_a.astype(BF16), "wq_b": wq_b.astype(BF16),
        "wk_a": wk_a.astype(BF16), "wv": wv.astype(BF16), "wv_ext": wv_ext.astype(BF16),
        "vone_ext": jnp.asarray(vone),
        "w13": jnp.concatenate([w1, w3], -1).astype(BF16), "w2": w2.astype(BF16),
        "wr_cat": jnp.concatenate([wr_hi, wr_lo], axis=-1),
        "b_router": jnp.pad(b_router, (0, LANES - N_EXPERTS)).reshape(1, LANES).astype(F32),
    }


def _channel_dft():
    c = np.arange(FN_GROUP_DIM)
    ang = 2.0 * np.pi * ((c[:, None] * c[None, :]) % FN_GROUP_DIM) / FN_GROUP_DIM
    out = np.zeros((FN_WIDTH, 2 * FN_WIDTH), np.float32)
    for g in range(FN_GROUPS):
        sl = slice(g * FN_GROUP_DIM, (g + 1) * FN_GROUP_DIM)
        out[sl, sl] = np.cos(ang) * FN_GROUP_DIM ** -0.5
        out[sl, FN_WIDTH + g * FN_GROUP_DIM:FN_WIDTH + (g + 1) * FN_GROUP_DIM] = np.sin(ang) * FN_GROUP_DIM ** -0.5
    return jnp.asarray(out, BF16)


def _rope_tables(n):
    tok = jnp.arange(n)
    pos = jnp.stack([tok // GRID_W, tok % GRID_W], axis=-1).astype(F32)
    nf = MLA_ROPE_DIM // 4
    freqs = ROPE_THETA ** (-jnp.arange(nf, dtype=F32) / nf)
    ang = pos[:, :, None] * freqs
    cos = jnp.broadcast_to(jnp.cos(ang)[:, :, None, :], (n, 2, 2, nf)).reshape(n, MLA_ROPE_DIM)
    sin = jnp.sin(ang)
    sin = jnp.stack([-sin, sin], axis=2).reshape(n, MLA_ROPE_DIM)
    pad = jnp.zeros((n, MLA_QK_PAD - MLA_NOPE_DIM - MLA_ROPE_DIM), F32)
    cos_t = jnp.concatenate([jnp.ones((n, MLA_NOPE_DIM), F32), cos, pad], -1)
    sin_t = jnp.concatenate([jnp.zeros((n, MLA_NOPE_DIM), F32), sin, pad], -1)
    return cos_t, sin_t


def kernel(x_prompt, x_sample, cache_na_k, cache_na_v, cache_mla_ckv, cache_mla_krope, c, c_ctx, w_ada, b_ada,
           norm1, norm2, w_in, conv_w, na_rpb, mla_gq, mla_wq_up, mla_gkv, mla_wkv_up, w_fn, w_out, w_router,
           b_router, w1, w3, w2, norm_f):
    bp, seq, d = x_prompt.shape
    bd, dec_seq, _ = x_sample.shape
    depth = w_in.shape[0]

    wts = _pack_weights(w_in, mla_wq_up, mla_wkv_up, w1, w3, w2, w_router, b_router)
    wts.update({
        "norm1": norm1.reshape(depth, 1, d), "norm2": norm2.reshape(depth, 1, d),
        "mla_gq": mla_gq.reshape(depth, 1, -1), "mla_gkv": mla_gkv.reshape(depth, 1, -1),
        "conv_w": conv_w, "w_fn": w_fn.astype(BF16), "w_out": w_out.astype(BF16),
        "norm_f": norm_f.reshape(1, d), "cs_bd": _channel_dft(),
    })

    cond = jnp.concatenate([c, jnp.zeros((-(bd + 1) % 8, d), c.dtype), c_ctx[None, :]], axis=0)
    mod = _ada_modulation(cond, w_ada, b_ada)

    xp = x_prompt.reshape(bp * seq, d)
    tables = _dft_tables(seq)
    caches = [
        jnp.zeros((bp, depth, NA_HEADS, seq, NA_HEAD_DIM), F32), jnp.zeros((bp, depth, NA_HEADS, seq, NA_HEAD_DIM), F32),
        jnp.zeros((bp, depth, seq, MLA_KV_LORA), F32), jnp.zeros((bp, depth, seq, MLA_ROPE_DIM), F32)]
    pending = None
    for layer in range(depth):
        outs = _premix(xp, mod, layer, seq, wts, None, False, seq, caches, pending)
        ab, z, qn, kn, vn, km, fab, qm, vm = outs[:9]
        caches = outs[9:13]
        if pending is not None:
            xp = outs[13]
        yna, ymla = _ctx_attention(qn, kn, vn, qm, km, vm, seq)
        g = _fourier(fab, tables, seq)
        xp, pending = _moe_layer(xp, mod, layer, seq, (ab, z, yna, ymla, g), wts, False, seq, MOE_ROW_BLOCK_CTX)
    xp = _combine(xp, pending[0], pending[1], mod, depth - 1, seq, wts["norm_f"], True, False, seq)
    new_na_k, new_na_v, new_ckv, new_krope = caches

    xs = x_sample.reshape(bd * dec_seq, d)
    kx, vxt = _ctx_kv(cache_mla_ckv, cache_mla_krope, wts["wk_a"], wts["wv_ext"], wts["vone_ext"])
    na_bias = _na_bias(na_rpb, dec_seq // GRID_W)
    rope = _rope_tables(dec_seq)
    tables = _dft_tables(dec_seq)
    pending = None
    for layer in range(depth):
        outs = _premix(xs, mod, layer, dec_seq, wts, rope, True, TM_LAT_PREMIX, None, pending)
        ab, z, qn, kn, vn, km, fab, qt, vt = outs[:9]
        if pending is not None:
            xs = outs[9]
        yna = _na_lat_attention(qn, kn, vn, cache_na_k, cache_na_v, na_bias, layer, dec_seq)
        ymla = _mla_lat_attention(qt, km, vt, kx, vxt, layer, dec_seq)
        g = _fourier(fab, tables, dec_seq)
        xs, pending = _moe_layer(xs, mod, layer, dec_seq, (ab, z, yna, ymla, g), wts, True, TM_LAT_MIXOUT,
                                 MOE_ROW_BLOCK)
    xs = _combine(xs, pending[0], pending[1], mod, depth - 1, dec_seq, wts["norm_f"], True, True, TM_LAT_PREMIX)

    return (xp.reshape(bp, seq, d), xs.reshape(bd, dec_seq, d), new_na_k, new_na_v, new_ckv, new_krope)
```

```python
import functools
import math

import numpy as np
import jax
import jax.numpy as jnp
from jax import lax
from jax.experimental import pallas as pl
from jax.experimental.pallas import tpu as pltpu
from jax.experimental.pallas import tpu_sc as plsc

F32 = jnp.float32
BF16 = jnp.bfloat16

GRID_W = 64
CONV_WIDTH = 256
NA_HEADS = 4
NA_HEAD_DIM = 64
NA_WIDTH = NA_HEADS * NA_HEAD_DIM
NA_KH = 8
NA_KW = 16
MLA_HEADS = 4
MLA_Q_LORA = 256
MLA_KV_LORA = 128
MLA_NOPE_DIM = 64
MLA_ROPE_DIM = 32
MLA_V_DIM = 64
MLA_QK_PAD = 128
MLA_V_PAD = 96
MLA_KEY_SUB = 256
LOG2E = 1.4426950408889634
FN_GROUPS = 4
FN_GROUP_DIM = 64
FN_WIDTH = FN_GROUPS * FN_GROUP_DIM
N_EXPERTS = 16
N_EXPERT_GROUPS = 4
EXPERTS_PER_GROUP = N_EXPERTS // N_EXPERT_GROUPS
EXPERT_FF = 256
ROPE_THETA = 10000.0
EPS = 1e-6
NEG_INF = -1e30
LANES = 128

NA_SCALE = NA_HEAD_DIM ** -0.5
MLA_SCALE = (MLA_NOPE_DIM + MLA_ROPE_DIM) ** -0.5

NA_Q_ROWS = 4
NA_WIN_ROWS = 12

TM_LAT_PREMIX = 512
TM_LAT_MIXOUT = 512
MOE_ROW_BLOCK_CTX = 512
MOE_ROW_BLOCK = 512
SC_WINDOW = 128
MOE_PIECES = 2

VMEM_LIMIT = 56 * 1024 * 1024

_C_AB, _C_AC, _C_AU, _C_Q, _C_K, _C_V, _C_CQ = 0, 256, 512, 768, 1024, 1280, 1536
_C_CKV, _C_KR, _C_KRS, _C_FU, _C_END = 1792, 1920, 2048, 2176, 2432


def _nt_dot(a, b):
    return lax.dot_general(a, b, (((1,), (1,)), ((), ())), preferred_element_type=F32)


def _dot(a, b):
    return jnp.dot(a, b, preferred_element_type=F32)


def _rms(x, g):
    return x * lax.rsqrt(jnp.mean(x * x, axis=-1, keepdims=True) + EPS) * g


def _params(*sem, flags=None):
    return pltpu.CompilerParams(dimension_semantics=sem, vmem_limit_bytes=VMEM_LIMIT, flags=flags)


def _ada_kernel(c_ref, w_ref, b_ref, o_ref):
    cnd = c_ref[...]
    act = cnd * jax.nn.sigmoid(cnd)
    o_ref[...] = _dot(act.astype(BF16), w_ref[...].astype(BF16)) + b_ref[...]


def _ada_modulation(cond, w_ada, b_ada):
    depth, d, six_d = w_ada.shape
    r = cond.shape[0]
    tn = 1024
    out = pl.pallas_call(
        _ada_kernel,
        grid=(depth, six_d // tn),
        in_specs=[
            pl.BlockSpec((r, d), lambda l, j: (0, 0)),
            pl.BlockSpec((None, d, tn), lambda l, j: (l, 0, j)),
            pl.BlockSpec((None, 1, tn), lambda l, j: (l, 0, j)),
        ],
        out_specs=pl.BlockSpec((None, r, tn), lambda l, j: (l, 0, j)),
        out_shape=jax.ShapeDtypeStruct((depth, r, six_d), F32),
        compiler_params=_params("parallel", "parallel"),
        name="ada_modulation",
    )(cond, w_ada, b_ada.reshape(depth, 1, six_d))
    return out.reshape(depth, r, 6, d)


def _unpack_pairs(p):
    hi = pltpu.bitcast(p & jnp.uint32(0xFFFF0000), F32)
    lo = pltpu.bitcast(p << 16, F32)
    return jnp.concatenate([hi, lo], axis=-1)


def _moe_residual(x1, gtok_ref, y_ref, mod_ref):
    g = gtok_ref[...]
    y_lo = _unpack_pairs(jnp.concatenate([y_ref[0, 0], y_ref[1, 0]], axis=-1))
    y_hi = _unpack_pairs(jnp.concatenate([y_ref[0, 1], y_ref[1, 1]], axis=-1))
    return x1 + mod_ref[...][5:6] * (g[:, 0:1] * y_lo + g[:, 1:2] * y_hi)


def _premix_kernel(lat, fused, n_in, *refs):
    (x_ref, mod_ref, g1_ref, w_ref, gq_ref, wqa_ref, wqb_ref, gkv_ref, wka_ref, wv_ref, vone_ref, cs_ref,
     cos_ref, sin_ref) = refs[:14]
    outs = refs[n_in:]
    (ab_ref, z_ref, qn_ref, kn_ref, vn_ref, km_ref, fab_ref) = outs[:7]

    x = x_ref[...]
    if fused:
        gtok_ref, y_ref, modp_ref = refs[n_in - 3:n_in]
        x = _moe_residual(x, gtok_ref, y_ref, modp_ref)
        outs[-1][...] = x
    mod = mod_ref[...]
    h = _rms(x, g1_ref[...]) * (1.0 + mod[1:2]) + mod[0:1]
    p = _dot(h.astype(BF16), w_ref[...])

    ab_ref[...] = p[:, _C_AB:_C_AC].astype(BF16)
    z_ref[...] = (p[:, _C_AC:_C_AU] * p[:, _C_AU:_C_Q]).astype(BF16)
    k_na = p[:, _C_K:_C_V]
    v_na = p[:, _C_V:_C_CQ]
    qn_ref[...] = (p[:, _C_Q:_C_K] * NA_SCALE).astype(BF16)
    kn_ref[...] = k_na.astype(BF16)
    vn_ref[...] = v_na.astype(BF16)

    cqn = _rms(p[:, _C_CQ:_C_CKV], gq_ref[...]).astype(BF16)
    ckvn = _rms(p[:, _C_CKV:_C_KR], gkv_ref[...])
    ckvn_b = ckvn.astype(BF16)
    qa = _dot(cqn, wqa_ref[...])
    kva = _dot(ckvn_b, wka_ref[...])
    v_mla = _dot(ckvn_b, wv_ref[...]) + vone_ref[...]
    kr = p[:, _C_KR:_C_KRS]
    if lat:
        cos = cos_ref[...]
        sin = sin_ref[...]
        qb = _dot(cqn, wqb_ref[...])
        krot = kr * cos + p[:, _C_KRS:_C_FU] * sin
        qt_ref, vt_ref = outs[7:9]
    else:
        krot = kr
        qm_ref, vm_ref, ck_ref, cv_ref, cckv_ref, ckr_ref = outs[7:13]
    for hd in range(MLA_HEADS):
        sl = slice(hd * MLA_QK_PAD, (hd + 1) * MLA_QK_PAD)
        km_ref[:, sl] = (kva[:, sl] + krot).astype(BF16)
        if lat:
            qh = (qa[:, sl] * cos + qb[:, sl] * sin) * (MLA_SCALE * LOG2E)
            qt_ref[sl, :] = jnp.transpose(qh).astype(BF16)
        else:
            qm_ref[:, sl] = (qa[:, sl] * MLA_SCALE).astype(BF16)

    fab_ref[...] = _dot(p[:, _C_FU:_C_END].astype(BF16), cs_ref[...]).astype(BF16)

    if lat:
        for j in range(v_mla.shape[1] // LANES):
            sl = slice(j * LANES, (j + 1) * LANES)
            vt_ref[sl, :] = jnp.transpose(v_mla[:, sl]).astype(BF16)
    else:
        vm_ref[...] = v_mla.astype(BF16)
        for hd in range(NA_HEADS):
            sl = slice(hd * NA_HEAD_DIM, (hd + 1) * NA_HEAD_DIM)
            ck_ref[hd] = k_na[:, sl]
            cv_ref[hd] = v_na[:, sl]
        cckv_ref[...] = ckvn
        ckr_ref[...] = kr[:, MLA_NOPE_DIM:MLA_NOPE_DIM + MLA_ROPE_DIM]


def _cond_row(lat, tm, n, ctx_row):
    return (lambda i: (i * tm) // n) if lat else (lambda i: ctx_row)


def _premix(x, mod, layer, n, wts, rope, lat, tm, caches=None, pending=None):
    t, d = x.shape
    if lat:
        cos_t, sin_t = rope
        wv, vone = wts["wv_ext"], wts["vone_ext"]
    else:
        cos_t = sin_t = jnp.zeros((8, LANES), F32)
        wv, vone = wts["wv"], jnp.zeros((1, MLA_HEADS * MLA_V_DIM), F32)
    vw = wv.shape[-1]
    qw = MLA_HEADS * MLA_QK_PAD
    tiles_per_seq = n // tm
    cond_row = _cond_row(lat, tm, n, mod.shape[1] - 1)
    const = lambda *_: (0, 0)
    lsel = lambda *_: (layer, 0, 0)
    rope_spec = (pl.BlockSpec((tm, LANES), lambda i: (i % tiles_per_seq, 0)) if lat
                 else pl.BlockSpec((8, LANES), const))
    in_specs = [
        pl.BlockSpec((tm, d), lambda i: (i, 0)),
        pl.BlockSpec((None, None, 6, d), lambda i: (layer, cond_row(i), 0, 0)),
        pl.BlockSpec((None, 1, d), lsel),
        pl.BlockSpec((None, d, _C_END), lsel),
        pl.BlockSpec((None, 1, MLA_Q_LORA), lsel),
        pl.BlockSpec((None, MLA_Q_LORA, qw), lsel),
        pl.BlockSpec((None, MLA_Q_LORA, qw), lsel),
        pl.BlockSpec((None, 1, MLA_KV_LORA), lsel),
        pl.BlockSpec((None, MLA_KV_LORA, qw), lsel),
        pl.BlockSpec((None, MLA_KV_LORA, vw), lsel),
        pl.BlockSpec((1, vw), const),
        pl.BlockSpec((FN_WIDTH, 2 * FN_WIDTH), const),
        rope_spec,
        rope_spec,
    ]
    row = lambda w: pl.BlockSpec((tm, w), lambda i: (i, 0))
    widths = [CONV_WIDTH, CONV_WIDTH, NA_WIDTH, NA_WIDTH, NA_WIDTH, qw, 2 * FN_WIDTH]
    out_specs = [row(w) for w in widths]
    out_shape = [jax.ShapeDtypeStruct((t, w), BF16) for w in widths]
    if lat:
        out_specs += [pl.BlockSpec((None, qw, tm), lambda i: (i, 0, 0)),
                      pl.BlockSpec((None, vw, tm), lambda i: (i, 0, 0))]
        out_shape += [jax.ShapeDtypeStruct((t // tm, qw, tm), BF16),
                      jax.ShapeDtypeStruct((t // tm, vw, tm), BF16)]
    else:
        assert tm == n
        b = t // n
        depth = wts["w_in"].shape[0]
        out_specs += [
            row(qw), row(vw),
            pl.BlockSpec((None, None, NA_HEADS, n, NA_HEAD_DIM), lambda i: (i, layer, 0, 0, 0)),
            pl.BlockSpec((None, None, NA_HEADS, n, NA_HEAD_DIM), lambda i: (i, layer, 0, 0, 0)),
            pl.BlockSpec((None, None, n, MLA_KV_LORA), lambda i: (i, layer, 0, 0)),
            pl.BlockSpec((None, None, n, MLA_ROPE_DIM), lambda i: (i, layer, 0, 0)),
        ]
        out_shape += [
            jax.ShapeDtypeStruct((t, qw), BF16), jax.ShapeDtypeStruct((t, vw), BF16),
            jax.ShapeDtypeStruct((b, depth, NA_HEADS, n, NA_HEAD_DIM), F32),
            jax.ShapeDtypeStruct((b, depth, NA_HEADS, n, NA_HEAD_DIM), F32),
            jax.ShapeDtypeStruct((b, depth, n, MLA_KV_LORA), F32),
            jax.ShapeDtypeStruct((b, depth, n, MLA_ROPE_DIM), F32),
        ]
    args = [x, mod, wts["norm1"], wts["w_in"], wts["mla_gq"], wts["wq_a"], wts["wq_b"], wts["mla_gkv"],
            wts["wk_a"], wv, vone, wts["cs_bd"], cos_t, sin_t]
    aliases = {}
    if caches is not None:
        first_cache_out = len(out_shape) - len(caches)
        aliases = {len(args) + j: first_cache_out + j for j in range(len(caches))}
        in_specs += [pl.BlockSpec(memory_space=pl.ANY)] * len(caches)
        args += list(caches)
    if pending is not None:
        gtok, y_tok = pending
        in_specs += [
            row(LANES),
            pl.BlockSpec(y_tok.shape[:2] + (tm, y_tok.shape[3]), lambda i: (0, 0, i, 0)),
            pl.BlockSpec((None, None, 6, d), lambda i: (layer - 1, cond_row(i), 0, 0)),
        ]
        args += [gtok, y_tok, mod]
        out_specs = out_specs + [row(d)]
        out_shape = out_shape + [jax.ShapeDtypeStruct((t, d), F32)]
    return pl.pallas_call(
        functools.partial(_premix_kernel, lat, pending is not None, len(args)),
        grid=(t // tm,),
        in_specs=in_specs,
        out_specs=out_specs,
        out_shape=out_shape,
        input_output_aliases=aliases,
        compiler_params=_params("parallel"),
        name="premix_lat" if lat else "premix_ctx",
    )(*args)


def _softmax_attend(q, k, v):
    s = _nt_dot(q, k)
    m = jnp.max(s, axis=-1, keepdims=True)
    p = jnp.exp(s - m)
    l = jnp.sum(p, axis=-1, keepdims=True)
    return _dot(p.astype(BF16), v) / l


def _ctx_attn_kernel(qn_ref, kn_ref, vn_ref, qm_ref, km_ref, vm_ref, yna_ref, ymla_ref):
    for hd in range(NA_HEADS):
        sl = slice(hd * NA_HEAD_DIM, (hd + 1) * NA_HEAD_DIM)
        yna_ref[:, sl] = _softmax_attend(qn_ref[:, sl], kn_ref[:, sl], vn_ref[:, sl]).astype(BF16)
    for hd in range(MLA_HEADS):
        sq = slice(hd * MLA_QK_PAD, (hd + 1) * MLA_QK_PAD)
        sv = slice(hd * MLA_V_DIM, (hd + 1) * MLA_V_DIM)
        ymla_ref[:, sv] = _softmax_attend(qm_ref[:, sq], km_ref[:, sq], vm_ref[:, sv]).astype(BF16)


def _ctx_attention(qn, kn, vn, qm, km, vm, n):
    t = qn.shape[0]
    spec = lambda w: pl.BlockSpec((n, w), lambda b: (b, 0))
    ins = [qn, kn, vn, qm, km, vm]
    return pl.pallas_call(
        _ctx_attn_kernel,
        grid=(t // n,),
        in_specs=[spec(a.shape[1]) for a in ins],
        out_specs=[spec(NA_WIDTH), spec(MLA_HEADS * MLA_V_DIM)],
        out_shape=[jax.ShapeDtypeStruct((t, NA_WIDTH), BF16),
                   jax.ShapeDtypeStruct((t, MLA_HEADS * MLA_V_DIM), BF16)],
        compiler_params=_params("parallel"),
        name="ctx_attention",
    )(*ins)


def _mla_lat_kernel(qt_ref, k_ref, vt_ref, kx_ref, vxt_ref, o_ref, s_scr, p_scr):
    nchunk, _, kc = vt_ref.shape
    tq = qt_ref.shape[1]
    sub = MLA_KEY_SUB

    ksl = lambda hd: slice(hd * MLA_QK_PAD, (hd + 1) * MLA_QK_PAD)
    vsl = lambda hd: slice(hd * MLA_V_PAD, (hd + 1) * MLA_V_PAD)

    def scores(slot, k_of, nk):
        cmax = []
        for hd in range(MLA_HEADS):
            qt = qt_ref[ksl(hd), :]
            part = None
            for j in range(0, nk, sub):
                st = _dot(k_of(hd, j), qt)
                s_scr[slot, hd, j:j + sub, :] = st
                blk = jnp.max(st.reshape(sub // 8, 8, tq), axis=0)
                part = blk if part is None else jnp.maximum(part, blk)
            cmax.append(jnp.max(part, axis=0, keepdims=True))
        return tuple(cmax)

    def attend(slot, cmax, state, vt_of, nk):
        new = []
        for hd in range(MLA_HEADS):
            m_i, acc = state[hd]
            m_new = jnp.maximum(m_i, cmax[hd])
            for j in range(0, nk, sub):
                p_scr[hd, j:j + sub, :] = jnp.exp2(s_scr[slot, hd, j:j + sub, :] - m_new).astype(BF16)
            acc = jnp.exp2(m_i - m_new) * acc + _dot(vt_of(hd), p_scr[hd, 0:nk, :])
            new.append((m_new, acc))
        return tuple(new)

    lat_keys = lambda c: (lambda hd, j: k_ref[pl.ds(pl.multiple_of(c * kc, kc) + j, sub), ksl(hd)])
    past = kx_ref.shape[0]
    state = tuple((jnp.full((1, tq), NEG_INF, F32), jnp.zeros((MLA_V_PAD, tq), F32)) for _ in range(MLA_HEADS))
    cmax_ctx = scores(1, lambda hd, j: kx_ref[j:j + sub, ksl(hd)], past)
    cmax = scores(0, lat_keys(0), kc)
    state = attend(1, cmax_ctx, state, lambda hd: vxt_ref[vsl(hd), :], past)

    lat_vals = lambda c: (lambda hd: vt_ref[c, vsl(hd), :])

    def body(i, carry):
        cmax0, state = carry
        c = 2 * i
        cmax1 = scores(1, lat_keys(c + 1), kc)
        state = attend(0, cmax0, state, lat_vals(c), kc)
        cmax0 = scores(0, lat_keys(c + 2), kc)
        state = attend(1, cmax1, state, lat_vals(c + 1), kc)
        return cmax0, state

    cmax, state = lax.fori_loop(0, nchunk // 2 - 1, body, (cmax, state))
    cmax1 = scores(1, lat_keys(nchunk - 1), kc)
    state = attend(0, cmax, state, lat_vals(nchunk - 2), kc)
    state = attend(1, cmax1, state, lat_vals(nchunk - 1), kc)
    o_t = jnp.concatenate([acc[:MLA_V_DIM] / acc[MLA_V_DIM:MLA_V_DIM + 1] for _, acc in state], axis=0)
    o_ref[...] = jnp.transpose(o_t).astype(BF16)


def _mla_lat_attention(qt, km, vt, kx, vxt, layer, n):
    ntile, qw, tq = qt.shape
    t = ntile * tq
    past = kx.shape[2]
    qpb = n // tq
    return pl.pallas_call(
        _mla_lat_kernel,
        grid=(t // n, qpb),
        in_specs=[
            pl.BlockSpec((None, qw, tq), lambda b, i: (b * qpb + i, 0, 0)),
            pl.BlockSpec((n, km.shape[1]), lambda b, i: (b, 0)),
            pl.BlockSpec((qpb, vt.shape[1], tq), lambda b, i: (b, 0, 0)),
            pl.BlockSpec((None, None, past, kx.shape[3]), lambda b, i: (layer, b, 0, 0)),
            pl.BlockSpec((None, None, vxt.shape[2], past), lambda b, i: (layer, b, 0, 0)),
        ],
        out_specs=pl.BlockSpec((tq, MLA_HEADS * MLA_V_DIM), lambda b, i: (b * qpb + i, 0)),
        out_shape=jax.ShapeDtypeStruct((t, MLA_HEADS * MLA_V_DIM), BF16),
        scratch_shapes=[pltpu.VMEM((2, MLA_HEADS, max(tq, past), tq), F32),
                        pltpu.VMEM((MLA_HEADS, max(tq, past), tq), BF16)],
        compiler_params=_params("parallel", "parallel"),
        name="mla_lat_attention",
    )(qt, km, vt, kx, vxt)


def _ctx_kv_kernel(ckv_ref, kr_ref, wka_ref, wv_ref, vone_ref, place_ref, k_ref, vt_ref):
    ckv = ckv_ref[...].astype(BF16)
    k_ref[...] = (_dot(ckv, wka_ref[...]) + _dot(kr_ref[...].astype(BF16), place_ref[...])).astype(BF16)
    v = _dot(ckv, wv_ref[...]) + vone_ref[...]
    for j in range(v.shape[1] // LANES):
        sl = slice(j * LANES, (j + 1) * LANES)
        vt_ref[sl, :] = jnp.transpose(v[:, sl]).astype(BF16)


def _ctx_kv(cache_ckv, cache_krope, wk_a, wv_ext, vone_ext):
    bd, depth, past, _ = cache_ckv.shape
    place = np.zeros((MLA_ROPE_DIM, MLA_HEADS * MLA_QK_PAD), np.float32)
    for hd in range(MLA_HEADS):
        for i in range(MLA_ROPE_DIM):
            place[i, hd * MLA_QK_PAD + MLA_NOPE_DIM + i] = 1.0
    kw, vw = MLA_HEADS * MLA_QK_PAD, MLA_HEADS * MLA_V_PAD
    return pl.pallas_call(
        _ctx_kv_kernel,
        grid=(depth, bd),
        in_specs=[
            pl.BlockSpec((None, None, past, MLA_KV_LORA), lambda l, b: (b, l, 0, 0)),
            pl.BlockSpec((None, None, past, MLA_ROPE_DIM), lambda l, b: (b, l, 0, 0)),
            pl.BlockSpec((None, MLA_KV_LORA, kw), lambda l, b: (l, 0, 0)),
            pl.BlockSpec((None, MLA_KV_LORA, vw), lambda l, b: (l, 0, 0)),
            pl.BlockSpec((1, vw), lambda l, b: (0, 0)),
            pl.BlockSpec((MLA_ROPE_DIM, kw), lambda l, b: (0, 0)),
        ],
        out_specs=[pl.BlockSpec((None, None, past, kw), lambda l, b: (l, b, 0, 0)),
                   pl.BlockSpec((None, None, vw, past), lambda l, b: (l, b, 0, 0))],
        out_shape=[jax.ShapeDtypeStruct((depth, bd, past, kw), BF16),
                   jax.ShapeDtypeStruct((depth, bd, vw, past), BF16)],
        compiler_params=_params("parallel", "parallel"),
        name="ctx_kv",
    )(cache_ckv, cache_krope, wk_a, wv_ext, vone_ext, jnp.asarray(place, BF16))


def _na_tile_geometry(rows):
    last = rows // NA_Q_ROWS - 1
    geo = []
    for j in (0, 1, last):
        r0 = j * NA_Q_ROWS
        geo.append((r0, min(max(r0 - NA_KH // 2, 0), rows - NA_WIN_ROWS)))
    return geo


def _na_bias_kernel(geo, rows, rpb_ref, o_ref):
    l = pl.program_id(0)
    hd = pl.program_id(1)
    base = (l * NA_HEADS + hd) * (2 * NA_KH - 1) * (2 * NA_KW - 1)
    qc = lax.broadcasted_iota(jnp.int32, (GRID_W, GRID_W), 0)
    kcol = lax.broadcasted_iota(jnp.int32, (GRID_W, GRID_W), 1)
    d_col = jnp.clip(kcol - qc + (NA_KW - 1), 0, 2 * NA_KW - 2)
    col_start = jnp.clip(qc - NA_KW // 2, 0, GRID_W - NA_KW)
    in_cols = (kcol >= col_start) & (kcol < col_start + NA_KW)
    neg = jnp.full((GRID_W, GRID_W), NEG_INF, F32)
    tabs = []
    for dr in range(2 * NA_KH - 1):
        acc = jnp.zeros((GRID_W, GRID_W), F32)
        for dc in range(2 * NA_KW - 1):
            acc = jnp.where(d_col == dc, rpb_ref[base + dr * (2 * NA_KW - 1) + dc], acc)
        tabs.append(jnp.where(in_cols, acc, neg))
    for kind, (r0, ws) in enumerate(geo):
        for i in range(NA_Q_ROWS):
            r = r0 + i
            lo = min(max(r - NA_KH // 2, 0), rows - NA_KH)
            for j in range(NA_WIN_ROWS):
                kr = ws + j
                blk = tabs[kr - r + NA_KH - 1] if lo <= kr < lo + NA_KH else neg
                o_ref[kind, i * GRID_W:(i + 1) * GRID_W, j * GRID_W:(j + 1) * GRID_W] = blk


def _na_bias(na_rpb, rows):
    depth = na_rpb.shape[0]
    geo = _na_tile_geometry(rows)
    qn, kn = NA_Q_ROWS * GRID_W, NA_WIN_ROWS * GRID_W
    return pl.pallas_call(
        functools.partial(_na_bias_kernel, geo, rows),
        grid=(depth, NA_HEADS),
        in_specs=[pl.BlockSpec(memory_space=pltpu.SMEM)],
        out_specs=pl.BlockSpec((None, None, 3, qn, kn), lambda l, h: (l, h, 0, 0, 0)),
        out_shape=jax.ShapeDtypeStruct((depth, NA_HEADS, 3, qn, kn), F32),
        compiler_params=_params("parallel", "parallel"),
        name="na_bias",
    )(na_rpb.reshape(-1))


def _na_lat_kernel(rows, q_ref, k_ref, v_ref, kx_ref, vx_ref, bias_ref, o_ref, s_scr, p_scr):
    j = pl.program_id(1)
    ws = jnp.clip(j * NA_Q_ROWS - NA_KH // 2, 0, rows - NA_WIN_ROWS)
    start = pl.multiple_of(ws * GRID_W, GRID_W)
    nk = NA_WIN_ROWS * GRID_W
    heads = [slice(hd * NA_HEAD_DIM, (hd + 1) * NA_HEAD_DIM) for hd in range(NA_HEADS)]
    m = []
    for hd, sl in enumerate(heads):
        q = q_ref[:, sl]
        s_win = _nt_dot(q, k_ref[pl.ds(start, nk), sl]) + bias_ref[hd]
        s_ctx = _nt_dot(q, kx_ref[hd].astype(BF16))
        s_scr[hd, :, :nk] = s_win
        s_scr[hd, :, nk:] = s_ctx
        m.append(jnp.maximum(jnp.max(s_win, axis=-1, keepdims=True), jnp.max(s_ctx, axis=-1, keepdims=True)))
    l = []
    for hd in range(NA_HEADS):
        p = jnp.exp(s_scr[hd] - m[hd])
        l.append(jnp.sum(p, axis=-1, keepdims=True))
        p_scr[hd] = p.astype(BF16)
    for hd, sl in enumerate(heads):
        o = _dot(p_scr[hd, :, :nk], v_ref[pl.ds(start, nk), sl]) + _dot(p_scr[hd, :, nk:], vx_ref[hd].astype(BF16))
        o_ref[:, sl] = (o / l[hd]).astype(BF16)


def _na_lat_attention(qn, kn, vn, cache_k, cache_v, bias, layer, n):
    t = qn.shape[0]
    rows = n // GRID_W
    assert rows % NA_Q_ROWS == 0 and rows >= NA_WIN_ROWS + NA_Q_ROWS
    tiles = rows // NA_Q_ROWS
    tq = NA_Q_ROWS * GRID_W
    past = cache_k.shape[3]

    def kind(b, j):
        return (layer, 0, jnp.where(j == 0, 0, jnp.where(j == tiles - 1, 2, 1)), 0, 0)

    return pl.pallas_call(
        functools.partial(_na_lat_kernel, rows),
        grid=(t // n, tiles),
        in_specs=[
            pl.BlockSpec((tq, NA_WIDTH), lambda b, j: (b * tiles + j, 0)),
            pl.BlockSpec((n, NA_WIDTH), lambda b, j: (b, 0)),
            pl.BlockSpec((n, NA_WIDTH), lambda b, j: (b, 0)),
            pl.BlockSpec((None, None, NA_HEADS, past, NA_HEAD_DIM), lambda b, j: (b, layer, 0, 0, 0)),
            pl.BlockSpec((None, None, NA_HEADS, past, NA_HEAD_DIM), lambda b, j: (b, layer, 0, 0, 0)),
            pl.BlockSpec((None, NA_HEADS, None, tq, NA_WIN_ROWS * GRID_W), kind),
        ],
        out_specs=pl.BlockSpec((tq, NA_WIDTH), lambda b, j: (b * tiles + j, 0)),
        out_shape=jax.ShapeDtypeStruct((t, NA_WIDTH), BF16),
        scratch_shapes=[pltpu.VMEM((NA_HEADS, tq, NA_WIN_ROWS * GRID_W + past), F32),
                        pltpu.VMEM((NA_HEADS, tq, NA_WIN_ROWS * GRID_W + past), BF16)],
        compiler_params=_params("parallel", "parallel"),
        name="na_lat_attention",
    )(qn, kn, vn, cache_k, cache_v, bias)


def _dft_tables(n):
    def thin(j, k, period):
        ang = (2.0 * math.pi / period) * ((j[:, None] * k[None, :]) % period).astype(F32)
        return jnp.cos(ang), jnp.sin(ang)

    k = jnp.arange(n, dtype=jnp.int32)
    scale = float(n) ** -0.5
    if n % 64 == 0 and n > 64:
        n1 = n // 64
        c1, s1 = thin(jnp.arange(n1, dtype=jnp.int32), k, n1)
        c2, s2 = thin(jnp.arange(64, dtype=jnp.int32), k, n)
        c1, s1, c2, s2 = c1[:, None, :], s1[:, None, :], c2[None, :, :], s2[None, :, :]
        cm = (c1 * c2 - s1 * s2).reshape(n, n)
        sm = (s1 * c2 + c1 * s2).reshape(n, n)
    else:
        cm, sm = thin(k, k, n)
    return (cm * scale).astype(BF16), (sm * -scale).astype(BF16)


def _fourier_kernel(c_ref, s_ref, ab_ref, o_ref):
    o_ref[...] = (_dot(c_ref[...], ab_ref[:, :FN_WIDTH]) + _dot(s_ref[...], ab_ref[:, FN_WIDTH:])).astype(BF16)


def _fourier_half_kernel(c_ref, s_ref, cmid_ref, ab_ref, plus_ref, minus_ref, mid_ref):
    a = ab_ref[:, :FN_WIDTH]
    p = _dot(c_ref[...], a)
    q = _dot(s_ref[...], ab_ref[:, FN_WIDTH:])
    plus_ref[...] = (p + q).astype(BF16)
    minus_ref[...] = (p - q).astype(BF16)
    mid_ref[...] = _dot(cmid_ref[...], a).astype(BF16)


def _fourier_half(fab, tables, n, tmf=512):
    t = fab.shape[0]
    b = t // n
    half = n // 2
    tiles = half // tmf
    cm, sm = tables
    cmid = jnp.broadcast_to(cm[half:half + 1], (8, n))
    row_out = lambda: pl.BlockSpec((tmf, FN_WIDTH), lambda i, bb: (bb * tiles + i, 0))
    plus, minus, mid = pl.pallas_call(
        _fourier_half_kernel,
        grid=(tiles, b),
        in_specs=[
            pl.BlockSpec((tmf, n), lambda i, bb: (i, 0)),
            pl.BlockSpec((tmf, n), lambda i, bb: (i, 0)),
            pl.BlockSpec((8, n), lambda i, bb: (0, 0)),
            pl.BlockSpec((n, 2 * FN_WIDTH), lambda i, bb: (bb, 0)),
        ],
        out_specs=[row_out(), row_out(), pl.BlockSpec((None, 8, FN_WIDTH), lambda i, bb: (bb, 0, 0))],
        out_shape=[jax.ShapeDtypeStruct((b * half, FN_WIDTH), BF16), jax.ShapeDtypeStruct((b * half, FN_WIDTH), BF16),
                   jax.ShapeDtypeStruct((b, 8, FN_WIDTH), BF16)],
        compiler_params=_params("parallel", "parallel"),
        name="fourier_half",
    )(cm, sm, cmid, fab)
    plus = plus.reshape(b, half, FN_WIDTH)
    minus = minus.reshape(b, half, FN_WIDTH)
    full = jnp.concatenate([plus, mid[:, :1], jnp.flip(minus[:, 1:], axis=1)], axis=1)
    return full.reshape(t, FN_WIDTH)


def _fourier(fab, tables, n, tmf=512):
    if n >= 4 * tmf:
        return _fourier_half(fab, tables, n, tmf)
    t = fab.shape[0]
    tmf = min(tmf, n)
    tiles = n // tmf
    cm, sm = tables
    return pl.pallas_call(
        _fourier_kernel,
        grid=(tiles, t // n),
        in_specs=[
            pl.BlockSpec((tmf, n), lambda i, b: (i, 0)),
            pl.BlockSpec((tmf, n), lambda i, b: (i, 0)),
            pl.BlockSpec((n, 2 * FN_WIDTH), lambda i, b: (b, 0)),
        ],
        out_specs=pl.BlockSpec((tmf, FN_WIDTH), lambda i, b: (b * tiles + i, 0)),
        out_shape=jax.ShapeDtypeStruct((t, FN_WIDTH), BF16),
        compiler_params=_params("parallel", "parallel"),
        name="fourier",
    )(cm, sm, fab)


def _route(s_t, sb_t):
    def top2_sum(v):
        hi1, lo1 = jnp.maximum(v[0], v[1]), jnp.minimum(v[0], v[1])
        hi2, lo2 = jnp.maximum(v[2], v[3]), jnp.minimum(v[2], v[3])
        return jnp.maximum(hi1, hi2) + jnp.maximum(jnp.minimum(hi1, hi2), jnp.maximum(lo1, lo2))

    best = top2_sum(sb_t[0:EXPERTS_PER_GROUP])
    gsel = jnp.zeros_like(best, dtype=jnp.int32)
    for g in range(1, N_EXPERT_GROUPS):
        cand = top2_sum(sb_t[g * EXPERTS_PER_GROUP:(g + 1) * EXPERTS_PER_GROUP])
        better = cand > best
        gsel = jnp.where(better, g, gsel)
        best = jnp.where(better, cand, best)
    chosen = []
    for e in range(N_EXPERTS):
        g = e // EXPERTS_PER_GROUP
        beaten = jnp.zeros_like(gsel)
        for o in range(g * EXPERTS_PER_GROUP, (g + 1) * EXPERTS_PER_GROUP):
            if o == e:
                continue
            ahead = (sb_t[o] > sb_t[e]) | ((sb_t[o] == sb_t[e]) & (o < e))
            beaten = beaten + ahead.astype(jnp.int32)
        chosen.append((gsel == g) & (beaten < 2))
    picked = [jnp.where(chosen[e], s_t[e], 0.0) for e in range(N_EXPERTS)]
    denom = picked[0]
    for e in range(1, N_EXPERTS):
        denom = denom + picked[e]
    return chosen, [pk / denom for pk in picked]


def _pack_pairs(x):
    w = x.shape[1] // 2
    hi = pltpu.bitcast(x[:, :w].astype(BF16).astype(F32), jnp.uint32)
    lo = pltpu.bitcast(x[:, w:].astype(BF16).astype(F32), jnp.uint32)
    return hi | (lo >> 16)


def _mixout_kernel(n, x_ref, mod_ref, ab_ref, z_ref, zp_ref, zn_ref, yna_ref, ymla_ref, g_ref, cw_ref, wfn_ref,
                   wout_ref, g2_ref, wrc_ref, br_ref, x1_ref, h2_ref, route_ref, gtok_ref, cnt_ref):
    tm = x_ref.shape[0]
    i = pl.program_id(0)
    mod = mod_ref[...]
    gate1, shift2, scale2 = mod[2:3], mod[3:4], mod[4:5]

    z = z_ref[...].astype(F32)
    ridx = lax.broadcasted_iota(jnp.int32, z.shape, 0)
    at_start = (i * tm) % n == 0
    at_end = ((i + 1) * tm) % n == 0
    prev_row = jnp.where(at_start, 0.0, zp_ref[7:8, :].astype(F32))
    next_row = jnp.where(at_end, 0.0, zn_ref[0:1, :].astype(F32))
    z_m1 = jnp.where(ridx == 0, prev_row, pltpu.roll(z, 1, axis=0))
    z_p1 = jnp.where(ridx == tm - 1, next_row, pltpu.roll(z, tm - 1, axis=0))
    cw = cw_ref[...]
    y_conv = ab_ref[...].astype(F32) * (z_m1 * cw[0:1] + z * cw[1:2] + z_p1 * cw[2:3])

    y_fn = _dot(g_ref[...], wfn_ref[...])
    cat = jnp.concatenate([y_conv.astype(BF16), yna_ref[...], ymla_ref[...], y_fn.astype(BF16)], axis=-1)
    x1 = x_ref[...] + gate1 * _dot(cat, wout_ref[...])
    x1_ref[...] = x1

    h2 = _rms(x1, g2_ref[...]) * (1.0 + scale2) + shift2
    packed = _pack_pairs(h2)
    piece = packed.shape[1] // MOE_PIECES
    for p in range(MOE_PIECES):
        h2_ref[p] = packed[:, p * piece:(p + 1) * piece]
    h2_hi = h2.astype(BF16)
    h2_lo = (h2 - h2_hi.astype(F32)).astype(BF16)
    both = _dot(h2_hi, wrc_ref[...])
    logits = both[:, :LANES] + (both[:, LANES:] + _dot(h2_lo, wrc_ref[:, :LANES]))
    s = jax.nn.sigmoid(logits)
    s_t = jnp.transpose(s)
    sb_t = jnp.transpose(s + br_ref[...])
    chosen, gates = _route([s_t[e:e + 1] for e in range(N_EXPERTS)], [sb_t[e:e + 1] for e in range(N_EXPERTS)])

    @pl.when(i == 0)
    def _():
        cnt_ref[...] = jnp.zeros(cnt_ref.shape, F32)

    chosen_f = jnp.concatenate([ch.astype(F32) for ch in chosen], axis=0)
    before = lax.broadcasted_iota(jnp.int32, (tm, tm), 0) < lax.broadcasted_iota(jnp.int32, (tm, tm), 1)
    prefix = _dot(chosen_f.astype(BF16), jnp.where(before, 1.0, 0.0).astype(BF16))
    base = cnt_ref[...]
    rank = jnp.concatenate([base] * (tm // LANES), axis=1) + prefix
    cnt_ref[...] = base + jnp.sum(chosen_f, axis=1, keepdims=True)

    zero = jnp.zeros((1, tm), F32)
    seen = zero
    slots = [[zero, zero, zero], [zero, zero, zero]]
    for e in range(N_EXPERTS):
        for k in range(2):
            hit = chosen[e] & (seen == float(k))
            for j, val in enumerate((float(e), gates[e], rank[e:e + 1])):
                slots[k][j] = jnp.where(hit, val, slots[k][j])
        seen = seen + chosen_f[e:e + 1]
    (e_lo, g_lo, r_lo), (e_hi, g_hi, r_hi) = slots
    route_ref[...] = jnp.concatenate([g_lo, g_hi, e_lo, e_hi, r_lo, r_hi, zero, zero], axis=0)
    gates_t = jnp.concatenate([g_lo, g_hi, jnp.zeros((LANES - 2, tm), F32)], axis=0)
    gtok_ref[...] = jnp.transpose(gates_t)


def _mixout(x, mod, layer, n, parts, wts, lat, tm):
    t, d = x.shape
    ab, z, yna, ymla, g = parts
    nblk8 = t // 8
    per8 = tm // 8
    cond_row_of_tile = _cond_row(lat, tm, n, mod.shape[1] - 1)
    const2 = lambda i: (0, 0)
    lsel = lambda i: (layer, 0, 0)
    row = lambda w: pl.BlockSpec((tm, w), lambda i: (i, 0))
    in_specs = [
        row(d),
        pl.BlockSpec((None, None, 6, d), lambda i: (layer, cond_row_of_tile(i), 0, 0)),
        row(CONV_WIDTH),
        row(CONV_WIDTH),
        pl.BlockSpec((8, CONV_WIDTH), lambda i: (jnp.maximum(i * per8 - 1, 0), 0)),
        pl.BlockSpec((8, CONV_WIDTH), lambda i: (jnp.minimum((i + 1) * per8, nblk8 - 1), 0)),
        row(NA_WIDTH),
        row(MLA_HEADS * MLA_V_DIM),
        row(FN_WIDTH),
        pl.BlockSpec((None, 3, CONV_WIDTH), lsel),
        pl.BlockSpec((None, FN_WIDTH, FN_WIDTH), lsel),
        pl.BlockSpec((None, d, d), lsel),
        pl.BlockSpec((None, 1, d), lsel),
        pl.BlockSpec((d, 2 * LANES), const2),
        pl.BlockSpec((1, LANES), const2),
    ]
    out_specs = [
        row(d),
        pl.BlockSpec((MOE_PIECES, tm, d // 2 // MOE_PIECES), lambda i: (0, i, 0)),
        pl.BlockSpec((8, tm), lambda i: (0, i)),
        row(LANES),
        pl.BlockSpec((N_EXPERTS, LANES), const2),
    ]
    out_shape = [
        jax.ShapeDtypeStruct((t, d), F32),
        jax.ShapeDtypeStruct((MOE_PIECES, t, d // 2 // MOE_PIECES), jnp.uint32),
        jax.ShapeDtypeStruct((8, t), F32),
        jax.ShapeDtypeStruct((t, LANES), F32),
        jax.ShapeDtypeStruct((N_EXPERTS, LANES), F32),
    ]
    return pl.pallas_call(
        functools.partial(_mixout_kernel, n),
        grid=(t // tm,),
        in_specs=in_specs,
        out_specs=out_specs,
        out_shape=out_shape,
        compiler_params=_params("arbitrary"),
        name="mixout",
    )(x, mod, ab, z, z, z, yna, ymla, g, wts["conv_w"], wts["w_fn"], wts["w_out"], wts["norm2"],
      wts["wr_cat"], wts["b_router"])


def _slot_positions(route, counts, rb):
    cnt = counts[:, 0].astype(jnp.int32)
    padded = (cnt + rb - 1) // rb * rb
    ends = jnp.cumsum(padded)
    offs = ends - padded
    experts = route[2:4].astype(jnp.int32)
    ranks = route[4:6].astype(jnp.int32)
    pos = ranks
    for e in range(N_EXPERTS):
        pos = pos + jnp.where(experts == e, offs[e], 0)
    nblk = (2 * route.shape[1]) // rb + N_EXPERTS
    starts = jnp.arange(nblk, dtype=jnp.int32) * rb
    blk_expert = jnp.sum((starts[:, None] >= ends[None, :]).astype(jnp.int32), axis=1)
    used = blk_expert < N_EXPERTS
    blk_expert = jnp.where(used, blk_expert, 0)
    valid_end = jnp.sum(jnp.where(blk_expert[:, None] == jnp.arange(N_EXPERTS)[None, :], (offs + cnt)[None, :], 0), axis=1)
    blk_valid = jnp.where(used, jnp.clip(valid_end - starts, 0, rb), 0)
    return pos, jnp.stack([blk_expert, blk_valid])


def _sc_mesh():
    return plsc.VectorSubcoreMesh(core_axis_name="c", subcore_axis_name="s")


def _sc_pipeline(body, nwin, in_specs, out_specs):
    return pltpu.emit_pipeline(body, grid=(nwin,), in_specs=in_specs, out_specs=out_specs,
                               core_axis_name=("c", "s"), dimension_semantics=(pltpu.PARALLEL,))


def _row_scatter(table, idx_a, idx_b, nrows):
    b, w = table.shape
    win = SC_WINDOW
    idx_spec = pl.BlockSpec((1, win), lambda i: (0, i))

    @functools.partial(pl.kernel, out_type=jax.ShapeDtypeStruct((nrows, w), table.dtype), mesh=_sc_mesh(),
                       scratch_types=[])
    def scatter(table_hbm, ia_hbm, ib_hbm, out_hbm):
        def body(rows_vmem, ia_vmem, ib_vmem):
            pltpu.sync_copy(rows_vmem, out_hbm.at[ia_vmem.at[0]])
            pltpu.sync_copy(rows_vmem, out_hbm.at[ib_vmem.at[0]])

        _sc_pipeline(body, b // win, [pl.BlockSpec((win, w), lambda i: (i, 0)), idx_spec, idx_spec], [])(
            table_hbm, ia_hbm, ib_hbm)

    return scatter(table, idx_a.reshape(1, b), idx_b.reshape(1, b))


def _row_gather(table, idx):
    b = idx.shape[0]
    w = table.shape[1]
    win = SC_WINDOW

    @functools.partial(pl.kernel, out_type=jax.ShapeDtypeStruct((b, w), table.dtype), mesh=_sc_mesh(),
                       scratch_types=[])
    def gather(table_hbm, idx_hbm, out_hbm):
        def body(idx_vmem, out_vmem):
            pltpu.sync_copy(table_hbm.at[idx_vmem.at[0]], out_vmem)

        _sc_pipeline(body, b // win, [pl.BlockSpec((1, win), lambda i: (0, i))],
                     [pl.BlockSpec((win, w), lambda i: (i, 0))])(idx_hbm, out_hbm)

    return gather(table, idx.reshape(1, b))


def _ffn_kernel(blk_ref, xs_ref, w13_ref, w2_ref, y_ref):
    i = pl.program_id(0)
    e = blk_ref[0, i]
    nvalid = blk_ref[1, i]

    @pl.when(nvalid > 0)
    def _():
        packed = jnp.concatenate([xs_ref[0], xs_ref[1]], axis=-1)
        live = lax.broadcasted_iota(jnp.int32, packed.shape, 0) < nvalid
        xb = _unpack_pairs(jnp.where(live, packed, jnp.uint32(0))).astype(BF16)
        up = _dot(xb, w13_ref[e])
        a, b = up[:, :EXPERT_FF], up[:, EXPERT_FF:]
        hid = (a * jax.nn.sigmoid(a)) * b
        y = _pack_pairs(_dot(hid.astype(BF16), w2_ref[e]))
        half = y.shape[1] // 2
        y_ref[0] = y[:, :half]
        y_ref[1] = y[:, half:]

    @pl.when(nvalid == 0)
    def _():
        y_ref[...] = jnp.zeros(y_ref.shape, y_ref.dtype)


def _expert_ffn(xs, blk, w13, w2, layer, rb):
    pieces, nrows, w = xs.shape
    d = 2 * pieces * w
    resident = dict(pipeline_mode=pl.Buffered(1))
    used = lambda i, blk: (0, jnp.where(blk[1, i] > 0, i, 0), 0)
    return pl.pallas_call(
        _ffn_kernel,
        grid_spec=pltpu.PrefetchScalarGridSpec(
            num_scalar_prefetch=1,
            grid=(nrows // rb,),
            in_specs=[
                pl.BlockSpec((pieces, rb, w), used),
                pl.BlockSpec((None, N_EXPERTS, d, 2 * EXPERT_FF), lambda i, blk: (layer, 0, 0, 0), **resident),
                pl.BlockSpec((None, N_EXPERTS, EXPERT_FF, d), lambda i, blk: (layer, 0, 0, 0), **resident),
            ],
            out_specs=pl.BlockSpec((pieces, rb, w), lambda i, blk: (0, i, 0)),
        ),
        out_shape=jax.ShapeDtypeStruct(xs.shape, xs.dtype),
        compiler_params=_params("parallel"),
        name="expert_ffn",
    )(blk, xs, w13, w2)


def _combine_kernel(final, x1_ref, gtok_ref, mod_ref, nf_ref, y_ref, o_ref):
    out = _moe_residual(x1_ref[...], gtok_ref, y_ref, mod_ref)
    if final:
        out = _rms(out, nf_ref[...])
    o_ref[...] = out


def _combine(x1, gtok, y_tok, mod, layer, n, norm_f, final, lat, tc=512):
    t, d = x1.shape
    cond_row_of_tile = _cond_row(lat, tc, n, mod.shape[1] - 1)
    row = lambda w: pl.BlockSpec((tc, w), lambda i: (i, 0))
    return pl.pallas_call(
        functools.partial(_combine_kernel, final),
        grid=(t // tc,),
        in_specs=[
            row(d),
            row(LANES),
            pl.BlockSpec((None, None, 6, d), lambda i: (layer, cond_row_of_tile(i), 0, 0)),
            pl.BlockSpec((1, d), lambda i: (0, 0)),
            pl.BlockSpec(y_tok.shape[:2] + (tc, y_tok.shape[3]), lambda i: (0, 0, i, 0)),
        ],
        out_specs=row(d),
        out_shape=jax.ShapeDtypeStruct((t, d), F32),
        compiler_params=_params("parallel"),
        name="combine",
    )(x1, gtok, mod, norm_f, y_tok)


def _moe_layer(x, mod, layer, n, parts, wts, lat, tm, rb):
    t = x.shape[0]
    x1, h2, route, gtok, counts = _mixout(x, mod, layer, n, parts, wts, lat, tm)
    pos, blk = _slot_positions(route, counts, rb)
    pieces, _, w = h2.shape
    nrows = blk.shape[1] * rb
    piece_base = (jnp.arange(pieces, dtype=jnp.int32) * nrows)[:, None]
    idx = [(piece_base + pos[s][None, :]).reshape(-1) for s in range(2)]
    xs = _row_scatter(h2.reshape(pieces * t, w), idx[0], idx[1], pieces * nrows).reshape(pieces, nrows, w)
    y = _expert_ffn(xs, blk, wts["w13"], wts["w2"], layer, rb)
    back = (piece_base[:, :, None] + pos[None, :, :]).reshape(-1)
    y_tok = _row_gather(y.reshape(pieces * nrows, w), back).reshape(pieces, 2, t, w)
    return x1, (gtok, y_tok)


def _swap_halves(w):
    nf = MLA_ROPE_DIM // 4
    idx = np.arange(MLA_ROPE_DIM).reshape(2, 2, nf)[:, ::-1, :].reshape(-1)
    return w[..., idx]


def _pack_weights(w_in, mla_wq_up, mla_wkv_up, w1, w3, w2, w_router, b_router):
    depth, d, _ = w_in.shape
    zeros = lambda w: jnp.zeros((depth, d, w), w_in.dtype)
    w_kr = w_in[..., 1920:1952]
    pad_rope = lambda w: jnp.concatenate([zeros(MLA_NOPE_DIM), w, zeros(MLA_QK_PAD - MLA_NOPE_DIM - MLA_ROPE_DIM)], -1)
    w_main = jnp.concatenate([w_in[..., :1920], pad_rope(w_kr), pad_rope(_swap_halves(w_kr)), w_in[..., 1952:]], -1)

    wq = mla_wq_up.reshape(depth, MLA_Q_LORA, MLA_HEADS, MLA_NOPE_DIM + MLA_ROPE_DIM)
    q_nope, q_rope = wq[..., :MLA_NOPE_DIM], wq[..., MLA_NOPE_DIM:]
    tail = jnp.zeros(q_rope.shape[:-1] + (MLA_QK_PAD - MLA_NOPE_DIM - MLA_ROPE_DIM,), wq.dtype)
    wq_a = jnp.concatenate([q_nope, q_rope, tail], -1).reshape(depth, MLA_Q_LORA, -1)
    wq_b = jnp.concatenate([jnp.zeros_like(q_nope), _swap_halves(q_rope), tail], -1).reshape(depth, MLA_Q_LORA, -1)

    wkv = mla_wkv_up.reshape(depth, MLA_KV_LORA, MLA_HEADS, MLA_NOPE_DIM + MLA_V_DIM)
    k_nope, v_up = wkv[..., :MLA_NOPE_DIM], wkv[..., MLA_NOPE_DIM:]
    k_tail = jnp.zeros(k_nope.shape[:-1] + (MLA_QK_PAD - MLA_NOPE_DIM,), wkv.dtype)
    wk_a = jnp.concatenate([k_nope, k_tail], -1).reshape(depth, MLA_KV_LORA, -1)
    wv = v_up.reshape(depth, MLA_KV_LORA, -1)
    v_tail = jnp.zeros(v_up.shape[:-1] + (MLA_V_PAD - MLA_V_DIM,), wkv.dtype)
    wv_ext = jnp.concatenate([v_up, v_tail], -1).reshape(depth, MLA_KV_LORA, -1)
    vone = np.zeros((1, MLA_HEADS * MLA_V_PAD), np.float32)
    vone[0, MLA_V_DIM::MLA_V_PAD] = 1.0

    wr = jnp.pad(w_router, ((0, 0), (0, LANES - N_EXPERTS)))
    wr_hi = wr.astype(BF16)
    wr_lo = (wr - wr_hi.astype(F32)).astype(BF16)
    return {
        "w_in": w_main.astype(BF16), "wq_a": wq_a.astype(BF16), "wq_b": wq_b.astype(BF16),
        "wk_a": wk_a.astype(BF16), "wv": wv.astype(BF16), "wv_ext": wv_ext.astype(BF16),
        "vone_ext": jnp.asarray(vone),
        "w13": jnp.concatenate([w1, w3], -1).astype(BF16), "w2": w2.astype(BF16),
        "wr_cat": jnp.concatenate([wr_hi, wr_lo], axis=-1),
        "b_router": jnp.pad(b_router, (0, LANES - N_EXPERTS)).reshape(1, LANES).astype(F32),
    }


def _channel_dft():
    c = np.arange(FN_GROUP_DIM)
    ang = 2.0 * np.pi * ((c[:, None] * c[None, :]) % FN_GROUP_DIM) / FN_GROUP_DIM
    out = np.zeros((FN_WIDTH, 2 * FN_WIDTH), np.float32)
    for g in range(FN_GROUPS):
        sl = slice(g * FN_GROUP_DIM, (g + 1) * FN_GROUP_DIM)
        out[sl, sl] = np.cos(ang) * FN_GROUP_DIM ** -0.5
        out[sl, FN_WIDTH + g * FN_GROUP_DIM:FN_WIDTH + (g + 1) * FN_GROUP_DIM] = np.sin(ang) * FN_GROUP_DIM ** -0.5
    return jnp.asarray(out, BF16)


def _rope_tables(n):
    tok = jnp.arange(n)
    pos = jnp.stack([tok // GRID_W, tok % GRID_W], axis=-1).astype(F32)
    nf = MLA_ROPE_DIM // 4
    freqs = ROPE_THETA ** (-jnp.arange(nf, dtype=F32) / nf)
    ang = pos[:, :, None] * freqs
    cos = jnp.broadcast_to(jnp.cos(ang)[:, :, None, :], (n, 2, 2, nf)).reshape(n, MLA_ROPE_DIM)
    sin = jnp.sin(ang)
    sin = jnp.stack([-sin, sin], axis=2).reshape(n, MLA_ROPE_DIM)
    pad = jnp.zeros((n, MLA_QK_PAD - MLA_NOPE_DIM - MLA_ROPE_DIM), F32)
    cos_t = jnp.concatenate([jnp.ones((n, MLA_NOPE_DIM), F32), cos, pad], -1)
    sin_t = jnp.concatenate([jnp.zeros((n, MLA_NOPE_DIM), F32), sin, pad], -1)
    return cos_t, sin_t


def kernel(x_prompt, x_sample, cache_na_k, cache_na_v, cache_mla_ckv, cache_mla_krope, c, c_ctx, w_ada, b_ada,
           norm1, norm2, w_in, conv_w, na_rpb, mla_gq, mla_wq_up, mla_gkv, mla_wkv_up, w_fn, w_out, w_router,
           b_router, w1, w3, w2, norm_f):
    bp, seq, d = x_prompt.shape
    bd, dec_seq, _ = x_sample.shape
    depth = w_in.shape[0]

    wts = _pack_weights(w_in, mla_wq_up, mla_wkv_up, w1, w3, w2, w_router, b_router)
    wts.update({
        "norm1": norm1.reshape(depth, 1, d), "norm2": norm2.reshape(depth, 1, d),
        "mla_gq": mla_gq.reshape(depth, 1, -1), "mla_gkv": mla_gkv.reshape(depth, 1, -1),
        "conv_w": conv_w, "w_fn": w_fn.astype(BF16), "w_out": w_out.astype(BF16),
        "norm_f": norm_f.reshape(1, d), "cs_bd": _channel_dft(),
    })

    cond = jnp.concatenate([c, jnp.zeros((-(bd + 1) % 8, d), c.dtype), c_ctx[None, :]], axis=0)
    mod = _ada_modulation(cond, w_ada, b_ada)

    xp = x_prompt.reshape(bp * seq, d)
    tables = _dft_tables(seq)
    caches = [
        jnp.zeros((bp, depth, NA_HEADS, seq, NA_HEAD_DIM), F32), jnp.zeros((bp, depth, NA_HEADS, seq, NA_HEAD_DIM), F32),
        jnp.zeros((bp, depth, seq, MLA_KV_LORA), F32), jnp.zeros((bp, depth, seq, MLA_ROPE_DIM), F32)]
    pending = None
    for layer in range(depth):
        outs = _premix(xp, mod, layer, seq, wts, None, False, seq, caches, pending)
        ab, z, qn, kn, vn, km, fab, qm, vm = outs[:9]
        caches = outs[9:13]
        if pending is not None:
            xp = outs[13]
        yna, ymla = _ctx_attention(qn, kn, vn, qm, km, vm, seq)
        g = _fourier(fab, tables, seq)
        xp, pending = _moe_layer(xp, mod, layer, seq, (ab, z, yna, ymla, g), wts, False, seq, MOE_ROW_BLOCK_CTX)
    xp = _combine(xp, pending[0], pending[1], mod, depth - 1, seq, wts["norm_f"], True, False, seq)
    new_na_k, new_na_v, new_ckv, new_krope = caches

    xs = x_sample.reshape(bd * dec_seq, d)
    kx, vxt = _ctx_kv(cache_mla_ckv, cache_mla_krope, wts["wk_a"], wts["wv_ext"], wts["vone_ext"])
    na_bias = _na_bias(na_rpb, dec_seq // GRID_W)
    rope = _rope_tables(dec_seq)
    tables = _dft_tables(dec_seq)
    pending = None
    for layer in range(depth):
        outs = _premix(xs, mod, layer, dec_seq, wts, rope, True, TM_LAT_PREMIX, None, pending)
        ab, z, qn, kn, vn, km, fab, qt, vt = outs[:9]
        if pending is not None:
            xs = outs[9]
        yna = _na_lat_attention(qn, kn, vn, cache_na_k, cache_na_v, na_bias, layer, dec_seq)
        ymla = _mla_lat_attention(qt, km, vt, kx, vxt, layer, dec_seq)
        g = _fourier(fab, tables, dec_seq)
        xs, pending = _moe_layer(xs, mod, layer, dec_seq, (ab, z, yna, ymla, g), wts, True, TM_LAT_MIXOUT,
                                 MOE_ROW_BLOCK)
    xs = _combine(xs, pending[0], pending[1], mod, depth - 1, dec_seq, wts["norm_f"], True, True, TM_LAT_PREMIX)

    return (xp.reshape(bp, seq, d), xs.reshape(bd, dec_seq, d), new_na_k, new_na_v, new_ckv, new_krope)
```

```python
import functools
import math

import numpy as np
import jax
import jax.numpy as jnp
from jax import lax
from jax.experimental import pallas as pl
from jax.experimental.pallas import tpu as pltpu
from jax.experimental.pallas import tpu_sc as plsc

F32 = jnp.float32
BF16 = jnp.bfloat16

GRID_W = 64
CONV_WIDTH = 256
NA_HEADS = 4
NA_HEAD_DIM = 64
NA_WIDTH = NA_HEADS * NA_HEAD_DIM
NA_KH = 8
NA_KW = 16
MLA_HEADS = 4
MLA_Q_LORA = 256
MLA_KV_LORA = 128
MLA_NOPE_DIM = 64
MLA_ROPE_DIM = 32
MLA_V_DIM = 64
MLA_QK_PAD = 128
MLA_V_PAD = 96
MLA_KEY_SUB = 256
LOG2E = 1.4426950408889634
FN_GROUPS = 4
FN_GROUP_DIM = 64
FN_WIDTH = FN_GROUPS * FN_GROUP_DIM
N_EXPERTS = 16
N_EXPERT_GROUPS = 4
EXPERTS_PER_GROUP = N_EXPERTS // N_EXPERT_GROUPS
EXPERT_FF = 256
ROPE_THETA = 10000.0
EPS = 1e-6
NEG_INF = -1e30
LANES = 128

NA_SCALE = NA_HEAD_DIM ** -0.5
MLA_SCALE = (MLA_NOPE_DIM + MLA_ROPE_DIM) ** -0.5

NA_Q_ROWS = 4
NA_WIN_ROWS = 12

TM_LAT_PREMIX = 512
TM_LAT_MIXOUT = 512
MOE_ROW_BLOCK_CTX = 512
MOE_ROW_BLOCK = 512
SC_WINDOW = 128
MOE_PIECES = 2

VMEM_LIMIT = 56 * 1024 * 1024

_C_AB, _C_AC, _C_AU, _C_Q, _C_K, _C_V, _C_CQ = 0, 256, 512, 768, 1024, 1280, 1536
_C_CKV, _C_KR, _C_KRS, _C_FU, _C_END = 1792, 1920, 2048, 2176, 2432


def _nt_dot(a, b):
    return lax.dot_general(a, b, (((1,), (1,)), ((), ())), preferred_element_type=F32)


def _dot(a, b):
    return jnp.dot(a, b, preferred_element_type=F32)


def _rms(x, g):
    return x * lax.rsqrt(jnp.mean(x * x, axis=-1, keepdims=True) + EPS) * g


def _params(*sem, flags=None):
    return pltpu.CompilerParams(dimension_semantics=sem, vmem_limit_bytes=VMEM_LIMIT, flags=flags)


def _ada_kernel(c_ref, w_ref, b_ref, o_ref):
    cnd = c_ref[...]
    act = cnd * jax.nn.sigmoid(cnd)
    o_ref[...] = _dot(act.astype(BF16), w_ref[...].astype(BF16)) + b_ref[...]


def _ada_modulation(cond, w_ada, b_ada):
    depth, d, six_d = w_ada.shape
    r = cond.shape[0]
    tn = 1024
    out = pl.pallas_call(
        _ada_kernel,
        grid=(depth, six_d // tn),
        in_specs=[
            pl.BlockSpec((r, d), lambda l, j: (0, 0)),
            pl.BlockSpec((None, d, tn), lambda l, j: (l, 0, j)),
            pl.BlockSpec((None, 1, tn), lambda l, j: (l, 0, j)),
        ],
        out_specs=pl.BlockSpec((None, r, tn), lambda l, j: (l, 0, j)),
        out_shape=jax.ShapeDtypeStruct((depth, r, six_d), F32),
        compiler_params=_params("parallel", "parallel"),
        name="ada_modulation",
    )(cond, w_ada, b_ada.reshape(depth, 1, six_d))
    return out.reshape(depth, r, 6, d)


def _unpack_pairs(p):
    hi = pltpu.bitcast(p & jnp.uint32(0xFFFF0000), F32)
    lo = pltpu.bitcast(p << 16, F32)
    return jnp.concatenate([hi, lo], axis=-1)


def _moe_residual(x1, gtok_ref, y_ref, mod_ref):
    g = gtok_ref[...]
    y_lo = _unpack_pairs(jnp.concatenate([y_ref[0, 0], y_ref[1, 0]], axis=-1))
    y_hi = _unpack_pairs(jnp.concatenate([y_ref[0, 1], y_ref[1, 1]], axis=-1))
    return x1 + mod_ref[...][5:6] * (g[:, 0:1] * y_lo + g[:, 1:2] * y_hi)


def _premix_kernel(lat, fused, n_in, *refs):
    (x_ref, mod_ref, g1_ref, w_ref, gq_ref, wqa_ref, wqb_ref, gkv_ref, wka_ref, wv_ref, vone_ref, cs_ref,
     cos_ref, sin_ref) = refs[:14]
    outs = refs[n_in:]
    (ab_ref, z_ref, qn_ref, kn_ref, vn_ref, km_ref, fab_ref) = outs[:7]

    x = x_ref[...]
    if fused:
        gtok_ref, y_ref, modp_ref = refs[n_in - 3:n_in]
        x = _moe_residual(x, gtok_ref, y_ref, modp_ref)
        outs[-1][...] = x
    mod = mod_ref[...]
    h = _rms(x, g1_ref[...]) * (1.0 + mod[1:2]) + mod[0:1]
    p = _dot(h.astype(BF16), w_ref[...])

    ab_ref[...] = p[:, _C_AB:_C_AC].astype(BF16)
    z_ref[...] = (p[:, _C_AC:_C_AU] * p[:, _C_AU:_C_Q]).astype(BF16)
    k_na = p[:, _C_K:_C_V]
    v_na = p[:, _C_V:_C_CQ]
    qn_ref[...] = (p[:, _C_Q:_C_K] * NA_SCALE).astype(BF16)
    kn_ref[...] = k_na.astype(BF16)
    vn_ref[...] = v_na.astype(BF16)

    cqn = _rms(p[:, _C_CQ:_C_CKV], gq_ref[...]).astype(BF16)
    ckvn = _rms(p[:, _C_CKV:_C_KR], gkv_ref[...])
    ckvn_b = ckvn.astype(BF16)
    qa = _dot(cqn, wqa_ref[...])
    kva = _dot(ckvn_b, wka_ref[...])
    v_mla = _dot(ckvn_b, wv_ref[...]) + vone_ref[...]
    kr = p[:, _C_KR:_C_KRS]
    if lat:
        cos = cos_ref[...]
        sin = sin_ref[...]
        qb = _dot(cqn, wqb_ref[...])
        krot = kr * cos + p[:, _C_KRS:_C_FU] * sin
        qt_ref, vt_ref = outs[7:9]
    else:
        krot = kr
        qm_ref, vm_ref, ck_ref, cv_ref, cckv_ref, ckr_ref = outs[7:13]
    for hd in range(MLA_HEADS):
        sl = slice(hd * MLA_QK_PAD, (hd + 1) * MLA_QK_PAD)
        km_ref[:, sl] = (kva[:, sl] + krot).astype(BF16)
        if lat:
            qh = (qa[:, sl] * cos + qb[:, sl] * sin) * (MLA_SCALE * LOG2E)
            qt_ref[sl, :] = jnp.transpose(qh).astype(BF16)
        else:
            qm_ref[:, sl] = (qa[:, sl] * MLA_SCALE).astype(BF16)

    fab_ref[...] = _dot(p[:, _C_FU:_C_END].astype(BF16), cs_ref[...]).astype(BF16)

    if lat:
        for j in range(v_mla.shape[1] // LANES):
            sl = slice(j * LANES, (j + 1) * LANES)
            vt_ref[sl, :] = jnp.transpose(v_mla[:, sl]).astype(BF16)
    else:
        vm_ref[...] = v_mla.astype(BF16)
        for hd in range(NA_HEADS):
            sl = slice(hd * NA_HEAD_DIM, (hd + 1) * NA_HEAD_DIM)
            ck_ref[hd] = k_na[:, sl]
            cv_ref[hd] = v_na[:, sl]
        cckv_ref[...] = ckvn
        ckr_ref[...] = kr[:, MLA_NOPE_DIM:MLA_NOPE_DIM + MLA_ROPE_DIM]


def _cond_row(lat, tm, n, ctx_row):
    return (lambda i: (i * tm) // n) if lat else (lambda i: ctx_row)


def _premix(x, mod, layer, n, wts, rope, lat, tm, caches=None, pending=None):
    t, d = x.shape
    if lat:
        cos_t, sin_t = rope
        wv, vone = wts["wv_ext"], wts["vone_ext"]
    else:
        cos_t = sin_t = jnp.zeros((8, LANES), F32)
        wv, vone = wts["wv"], jnp.zeros((1, MLA_HEADS * MLA_V_DIM), F32)
    vw = wv.shape[-1]
    qw = MLA_HEADS * MLA_QK_PAD
    tiles_per_seq = n // tm
    cond_row = _cond_row(lat, tm, n, mod.shape[1] - 1)
    const = lambda *_: (0, 0)
    lsel = lambda *_: (layer, 0, 0)
    rope_spec = (pl.BlockSpec((tm, LANES), lambda i: (i % tiles_per_seq, 0)) if lat
                 else pl.BlockSpec((8, LANES), const))
    in_specs = [
        pl.BlockSpec((tm, d), lambda i: (i, 0)),
        pl.BlockSpec((None, None, 6, d), lambda i: (layer, cond_row(i), 0, 0)),
        pl.BlockSpec((None, 1, d), lsel),
        pl.BlockSpec((None, d, _C_END), lsel),
        pl.BlockSpec((None, 1, MLA_Q_LORA), lsel),
        pl.BlockSpec((None, MLA_Q_LORA, qw), lsel),
        pl.BlockSpec((None, MLA_Q_LORA, qw), lsel),
        pl.BlockSpec((None, 1, MLA_KV_LORA), lsel),
        pl.BlockSpec((None, MLA_KV_LORA, qw), lsel),
        pl.BlockSpec((None, MLA_KV_LORA, vw), lsel),
        pl.BlockSpec((1, vw), const),
        pl.BlockSpec((FN_WIDTH, 2 * FN_WIDTH), const),
        rope_spec,
        rope_spec,
    ]
    row = lambda w: pl.BlockSpec((tm, w), lambda i: (i, 0))
    widths = [CONV_WIDTH, CONV_WIDTH, NA_WIDTH, NA_WIDTH, NA_WIDTH, qw, 2 * FN_WIDTH]
    out_specs = [row(w) for w in widths]
    out_shape = [jax.ShapeDtypeStruct((t, w), BF16) for w in widths]
    if lat:
        out_specs += [pl.BlockSpec((None, qw, tm), lambda i: (i, 0, 0)),
                      pl.BlockSpec((None, vw, tm), lambda i: (i, 0, 0))]
        out_shape += [jax.ShapeDtypeStruct((t // tm, qw, tm), BF16),
                      jax.ShapeDtypeStruct((t // tm, vw, tm), BF16)]
    else:
        assert tm == n
        b = t // n
        depth = wts["w_in"].shape[0]
        out_specs += [
            row(qw), row(vw),
            pl.BlockSpec((None, None, NA_HEADS, n, NA_HEAD_DIM), lambda i: (i, layer, 0, 0, 0)),
            pl.BlockSpec((None, None, NA_HEADS, n, NA_HEAD_DIM), lambda i: (i, layer, 0, 0, 0)),
            pl.BlockSpec((None, None, n, MLA_KV_LORA), lambda i: (i, layer, 0, 0)),
            pl.BlockSpec((None, None, n, MLA_ROPE_DIM), lambda i: (i, layer, 0, 0)),
        ]
        out_shape += [
            jax.ShapeDtypeStruct((t, qw), BF16), jax.ShapeDtypeStruct((t, vw), BF16),
            jax.ShapeDtypeStruct((b, depth, NA_HEADS, n, NA_HEAD_DIM), F32),
            jax.ShapeDtypeStruct((b, depth, NA_HEADS, n, NA_HEAD_DIM), F32),
            jax.ShapeDtypeStruct((b, depth, n, MLA_KV_LORA), F32),
            jax.ShapeDtypeStruct((b, depth, n, MLA_ROPE_DIM), F32),
        ]
    args = [x, mod, wts["norm1"], wts["w_in"], wts["mla_gq"], wts["wq_a"], wts["wq_b"], wts["mla_gkv"],
            wts["wk_a"], wv, vone, wts["cs_bd"], cos_t, sin_t]
    aliases = {}
    if caches is not None:
        first_cache_out = len(out_shape) - len(caches)
        aliases = {len(args) + j: first_cache_out + j for j in range(len(caches))}
        in_specs += [pl.BlockSpec(memory_space=pl.ANY)] * len(caches)
        args += list(caches)
    if pending is not None:
        gtok, y_tok = pending
        in_specs += [
            row(LANES),
            pl.BlockSpec(y_tok.shape[:2] + (tm, y_tok.shape[3]), lambda i: (0, 0, i, 0)),
            pl.BlockSpec((None, None, 6, d), lambda i: (layer - 1, cond_row(i), 0, 0)),
        ]
        args += [gtok, y_tok, mod]
        out_specs = out_specs + [row(d)]
        out_shape = out_shape + [jax.ShapeDtypeStruct((t, d), F32)]
    return pl.pallas_call(
        functools.partial(_premix_kernel, lat, pending is not None, len(args)),
        grid=(t // tm,),
        in_specs=in_specs,
        out_specs=out_specs,
        out_shape=out_shape,
        input_output_aliases=aliases,
        compiler_params=_params("parallel"),
        name="premix_lat" if lat else "premix_ctx",
    )(*args)


def _softmax_attend(q, k, v):
    s = _nt_dot(q, k)
    m = jnp.max(s, axis=-1, keepdims=True)
    p = jnp.exp(s - m)
    l = jnp.sum(p, axis=-1, keepdims=True)
    return _dot(p.astype(BF16), v) / l


def _ctx_attn_kernel(qn_ref, kn_ref, vn_ref, qm_ref, km_ref, vm_ref, yna_ref, ymla_ref):
    for hd in range(NA_HEADS):
        sl = slice(hd * NA_HEAD_DIM, (hd + 1) * NA_HEAD_DIM)
        yna_ref[:, sl] = _softmax_attend(qn_ref[:, sl], kn_ref[:, sl], vn_ref[:, sl]).astype(BF16)
    for hd in range(MLA_HEADS):
        sq = slice(hd * MLA_QK_PAD, (hd + 1) * MLA_QK_PAD)
        sv = slice(hd * MLA_V_DIM, (hd + 1) * MLA_V_DIM)
        ymla_ref[:, sv] = _softmax_attend(qm_ref[:, sq], km_ref[:, sq], vm_ref[:, sv]).astype(BF16)


def _ctx_attention(qn, kn, vn, qm, km, vm, n):
    t = qn.shape[0]
    spec = lambda w: pl.BlockSpec((n, w), lambda b: (b, 0))
    ins = [qn, kn, vn, qm, km, vm]
    return pl.pallas_call(
        _ctx_attn_kernel,
        grid=(t // n,),
        in_specs=[spec(a.shape[1]) for a in ins],
        out_specs=[spec(NA_WIDTH), spec(MLA_HEADS * MLA_V_DIM)],
        out_shape=[jax.ShapeDtypeStruct((t, NA_WIDTH), BF16),
                   jax.ShapeDtypeStruct((t, MLA_HEADS * MLA_V_DIM), BF16)],
        compiler_params=_params("parallel"),
        name="ctx_attention",
    )(*ins)


def _mla_lat_kernel(qt_ref, k_ref, vt_ref, kx_ref, vxt_ref, o_ref, s_scr, p_scr):
    nchunk, _, kc = vt_ref.shape
    tq = qt_ref.shape[1]
    sub = MLA_KEY_SUB

    ksl = lambda hd: slice(hd * MLA_QK_PAD, (hd + 1) * MLA_QK_PAD)
    vsl = lambda hd: slice(hd * MLA_V_PAD, (hd + 1) * MLA_V_PAD)

    def scores(slot, k_of, nk):
        cmax = []
        for hd in range(MLA_HEADS):
            qt = qt_ref[ksl(hd), :]
            part = None
            for j in range(0, nk, sub):
                st = _dot(k_of(hd, j), qt)
                s_scr[slot, hd, j:j + sub, :] = st
                blk = jnp.max(st.reshape(sub // 8, 8, tq), axis=0)
                part = blk if part is None else jnp.maximum(part, blk)
            cmax.append(jnp.max(part, axis=0, keepdims=True))
        return tuple(cmax)

    def attend(slot, cmax, state, vt_of, nk):
        new = []
        for hd in range(MLA_HEADS):
            m_i, acc = state[hd]
            m_new = jnp.maximum(m_i, cmax[hd])
            for j in range(0, nk, sub):
                p_scr[hd, j:j + sub, :] = jnp.exp2(s_scr[slot, hd, j:j + sub, :] - m_new).astype(BF16)
            acc = jnp.exp2(m_i - m_new) * acc + _dot(vt_of(hd), p_scr[hd, 0:nk, :])
            new.append((m_new, acc))
        return tuple(new)

    lat_keys = lambda c: (lambda hd, j: k_ref[pl.ds(pl.multiple_of(c * kc, kc) + j, sub), ksl(hd)])
    past = kx_ref.shape[0]
    state = tuple((jnp.full((1, tq), NEG_INF, F32), jnp.zeros((MLA_V_PAD, tq), F32)) for _ in range(MLA_HEADS))
    cmax_ctx = scores(1, lambda hd, j: kx_ref[j:j + sub, ksl(hd)], past)
    cmax = scores(0, lat_keys(0), kc)
    state = attend(1, cmax_ctx, state, lambda hd: vxt_ref[vsl(hd), :], past)

    lat_vals = lambda c: (lambda hd: vt_ref[c, vsl(hd), :])

    def body(i, carry):
        cmax0, state = carry
        c = 2 * i
        cmax1 = scores(1, lat_keys(c + 1), kc)
        state = attend(0, cmax0, state, lat_vals(c), kc)
        cmax0 = scores(0, lat_keys(c + 2), kc)
        state = attend(1, cmax1, state, lat_vals(c + 1), kc)
        return cmax0, state

    cmax, state = lax.fori_loop(0, nchunk // 2 - 1, body, (cmax, state))
    cmax1 = scores(1, lat_keys(nchunk - 1), kc)
    state = attend(0, cmax, state, lat_vals(nchunk - 2), kc)
    state = attend(1, cmax1, state, lat_vals(nchunk - 1), kc)
    o_t = jnp.concatenate([acc[:MLA_V_DIM] / acc[MLA_V_DIM:MLA_V_DIM + 1] for _, acc in state], axis=0)
    o_ref[...] = jnp.transpose(o_t).astype(BF16)


def _mla_lat_attention(qt, km, vt, kx, vxt, layer, n):
    ntile, qw, tq = qt.shape
    t = ntile * tq
    past = kx.shape[2]
    qpb = n // tq
    return pl.pallas_call(
        _mla_lat_kernel,
        grid=(t // n, qpb),
        in_specs=[
            pl.BlockSpec((None, qw, tq), lambda b, i: (b * qpb + i, 0, 0)),
            pl.BlockSpec((n, km.shape[1]), lambda b, i: (b, 0)),
            pl.BlockSpec((qpb, vt.shape[1], tq), lambda b, i: (b, 0, 0)),
            pl.BlockSpec((None, None, past, kx.shape[3]), lambda b, i: (layer, b, 0, 0)),
            pl.BlockSpec((None, None, vxt.shape[2], past), lambda b, i: (layer, b, 0, 0)),
        ],
        out_specs=pl.BlockSpec((tq, MLA_HEADS * MLA_V_DIM), lambda b, i: (b * qpb + i, 0)),
        out_shape=jax.ShapeDtypeStruct((t, MLA_HEADS * MLA_V_DIM), BF16),
        scratch_shapes=[pltpu.VMEM((2, MLA_HEADS, max(tq, past), tq), F32),
                        pltpu.VMEM((MLA_HEADS, max(tq, past), tq), BF16)],
        compiler_params=_params("parallel", "parallel"),
        name="mla_lat_attention",
    )(qt, km, vt, kx, vxt)


def _ctx_kv_kernel(ckv_ref, kr_ref, wka_ref, wv_ref, vone_ref, place_ref, k_ref, vt_ref):
    ckv = ckv_ref[...].astype(BF16)
    k_ref[...] = (_dot(ckv, wka_ref[...]) + _dot(kr_ref[...].astype(BF16), place_ref[...])).astype(BF16)
    v = _dot(ckv, wv_ref[...]) + vone_ref[...]
    for j in range(v.shape[1] // LANES):
        sl = slice(j * LANES, (j + 1) * LANES)
        vt_ref[sl, :] = jnp.transpose(v[:, sl]).astype(BF16)


def _ctx_kv(cache_ckv, cache_krope, wk_a, wv_ext, vone_ext):
    bd, depth, past, _ = cache_ckv.shape
    place = np.zeros((MLA_ROPE_DIM, MLA_HEADS * MLA_QK_PAD), np.float32)
    for hd in range(MLA_HEADS):
        for i in range(MLA_ROPE_DIM):
            place[i, hd * MLA_QK_PAD + MLA_NOPE_DIM + i] = 1.0
    kw, vw = MLA_HEADS * MLA_QK_PAD, MLA_HEADS * MLA_V_PAD
    return pl.pallas_call(
        _ctx_kv_kernel,
        grid=(depth, bd),
        in_specs=[
            pl.BlockSpec((None, None, past, MLA_KV_LORA), lambda l, b: (b, l, 0, 0)),
            pl.BlockSpec((None, None, past, MLA_ROPE_DIM), lambda l, b: (b, l, 0, 0)),
            pl.BlockSpec((None, MLA_KV_LORA, kw), lambda l, b: (l, 0, 0)),
            pl.BlockSpec((None, MLA_KV_LORA, vw), lambda l, b: (l, 0, 0)),
            pl.BlockSpec((1, vw), lambda l, b: (0, 0)),
            pl.BlockSpec((MLA_ROPE_DIM, kw), lambda l, b: (0, 0)),
        ],
        out_specs=[pl.BlockSpec((None, None, past, kw), lambda l, b: (l, b, 0, 0)),
                   pl.BlockSpec((None, None, vw, past), lambda l, b: (l, b, 0, 0))],
        out_shape=[jax.ShapeDtypeStruct((depth, bd, past, kw), BF16),
                   jax.ShapeDtypeStruct((depth, bd, vw, past), BF16)],
        compiler_params=_params("parallel", "parallel"),
        name="ctx_kv",
    )(cache_ckv, cache_krope, wk_a, wv_ext, vone_ext, jnp.asarray(place, BF16))


def _na_tile_geometry(rows):
    last = rows // NA_Q_ROWS - 1
    geo = []
    for j in (0, 1, last):
        r0 = j * NA_Q_ROWS
        geo.append((r0, min(max(r0 - NA_KH // 2, 0), rows - NA_WIN_ROWS)))
    return geo


def _na_bias_kernel(geo, rows, rpb_ref, o_ref):
    l = pl.program_id(0)
    hd = pl.program_id(1)
    base = (l * NA_HEADS + hd) * (2 * NA_KH - 1) * (2 * NA_KW - 1)
    qc = lax.broadcasted_iota(jnp.int32, (GRID_W, GRID_W), 0)
    kcol = lax.broadcasted_iota(jnp.int32, (GRID_W, GRID_W), 1)
    d_col = jnp.clip(kcol - qc + (NA_KW - 1), 0, 2 * NA_KW - 2)
    col_start = jnp.clip(qc - NA_KW // 2, 0, GRID_W - NA_KW)
    in_cols = (kcol >= col_start) & (kcol < col_start + NA_KW)
    neg = jnp.full((GRID_W, GRID_W), NEG_INF, F32)
    tabs = []
    for dr in range(2 * NA_KH - 1):
        acc = jnp.zeros((GRID_W, GRID_W), F32)
        for dc in range(2 * NA_KW - 1):
            acc = jnp.where(d_col == dc, rpb_ref[base + dr * (2 * NA_KW - 1) + dc], acc)
        tabs.append(jnp.where(in_cols, acc, neg))
    for kind, (r0, ws) in enumerate(geo):
        for i in range(NA_Q_ROWS):
            r = r0 + i
            lo = min(max(r - NA_KH // 2, 0), rows - NA_KH)
            for j in range(NA_WIN_ROWS):
                kr = ws + j
                blk = tabs[kr - r + NA_KH - 1] if lo <= kr < lo + NA_KH else neg
                o_ref[kind, i * GRID_W:(i + 1) * GRID_W, j * GRID_W:(j + 1) * GRID_W] = blk


def _na_bias(na_rpb, rows):
    depth = na_rpb.shape[0]
    geo = _na_tile_geometry(rows)
    qn, kn = NA_Q_ROWS * GRID_W, NA_WIN_ROWS * GRID_W
    return pl.pallas_call(
        functools.partial(_na_bias_kernel, geo, rows),
        grid=(depth, NA_HEADS),
        in_specs=[pl.BlockSpec(memory_space=pltpu.SMEM)],
        out_specs=pl.BlockSpec((None, None, 3, qn, kn), lambda l, h: (l, h, 0, 0, 0)),
        out_shape=jax.ShapeDtypeStruct((depth, NA_HEADS, 3, qn, kn), F32),
        compiler_params=_params("parallel", "parallel"),
        name="na_bias",
    )(na_rpb.reshape(-1))


def _na_lat_kernel(rows, q_ref, k_ref, v_ref, kx_ref, vx_ref, bias_ref, o_ref, s_scr, p_scr):
    j = pl.program_id(1)
    ws = jnp.clip(j * NA_Q_ROWS - NA_KH // 2, 0, rows - NA_WIN_ROWS)
    start = pl.multiple_of(ws * GRID_W, GRID_W)
    nk = NA_WIN_ROWS * GRID_W
    heads = [slice(hd * NA_HEAD_DIM, (hd + 1) * NA_HEAD_DIM) for hd in range(NA_HEADS)]
    m = []
    for hd, sl in enumerate(heads):
        q = q_ref[:, sl]
        s_win = _nt_dot(q, k_ref[pl.ds(start, nk), sl]) + bias_ref[hd]
        s_ctx = _nt_dot(q, kx_ref[hd].astype(BF16))
        s_scr[hd, :, :nk] = s_win
        s_scr[hd, :, nk:] = s_ctx
        m.append(jnp.maximum(jnp.max(s_win, axis=-1, keepdims=True), jnp.max(s_ctx, axis=-1, keepdims=True)))
    l = []
    for hd in range(NA_HEADS):
        p = jnp.exp(s_scr[hd] - m[hd])
        l.append(jnp.sum(p, axis=-1, keepdims=True))
        p_scr[hd] = p.astype(BF16)
    for hd, sl in enumerate(heads):
        o = _dot(p_scr[hd, :, :nk], v_ref[pl.ds(start, nk), sl]) + _dot(p_scr[hd, :, nk:], vx_ref[hd].astype(BF16))
        o_ref[:, sl] = (o / l[hd]).astype(BF16)


def _na_lat_attention(qn, kn, vn, cache_k, cache_v, bias, layer, n):
    t = qn.shape[0]
    rows = n // GRID_W
    assert rows % NA_Q_ROWS == 0 and rows >= NA_WIN_ROWS + NA_Q_ROWS
    tiles = rows // NA_Q_ROWS
    tq = NA_Q_ROWS * GRID_W
    past = cache_k.shape[3]

    def kind(b, j):
        return (layer, 0, jnp.where(j == 0, 0, jnp.where(j == tiles - 1, 2, 1)), 0, 0)

    return pl.pallas_call(
        functools.partial(_na_lat_kernel, rows),
        grid=(t // n, tiles),
        in_specs=[
            pl.BlockSpec((tq, NA_WIDTH), lambda b, j: (b * tiles + j, 0)),
            pl.BlockSpec((n, NA_WIDTH), lambda b, j: (b, 0)),
            pl.BlockSpec((n, NA_WIDTH), lambda b, j: (b, 0)),
            pl.BlockSpec((None, None, NA_HEADS, past, NA_HEAD_DIM), lambda b, j: (b, layer, 0, 0, 0)),
            pl.BlockSpec((None, None, NA_HEADS, past, NA_HEAD_DIM), lambda b, j: (b, layer, 0, 0, 0)),
            pl.BlockSpec((None, NA_HEADS, None, tq, NA_WIN_ROWS * GRID_W), kind),
        ],
        out_specs=pl.BlockSpec((tq, NA_WIDTH), lambda b, j: (b * tiles + j, 0)),
        out_shape=jax.ShapeDtypeStruct((t, NA_WIDTH), BF16),
        scratch_shapes=[pltpu.VMEM((NA_HEADS, tq, NA_WIN_ROWS * GRID_W + past), F32),
                        pltpu.VMEM((NA_HEADS, tq, NA_WIN_ROWS * GRID_W + past), BF16)],
        compiler_params=_params("parallel", "parallel"),
        name="na_lat_attention",
    )(qn, kn, vn, cache_k, cache_v, bias)


def _dft_tables(n):
    def thin(j, k, period):
        ang = (2.0 * math.pi / period) * ((j[:, None] * k[None, :]) % period).astype(F32)
        return jnp.cos(ang), jnp.sin(ang)

    k = jnp.arange(n, dtype=jnp.int32)
    scale = float(n) ** -0.5
    if n % 64 == 0 and n > 64:
        n1 = n // 64
        c1, s1 = thin(jnp.arange(n1, dtype=jnp.int32), k, n1)
        c2, s2 = thin(jnp.arange(64, dtype=jnp.int32), k, n)
        c1, s1, c2, s2 = c1[:, None, :], s1[:, None, :], c2[None, :, :], s2[None, :, :]
        cm = (c1 * c2 - s1 * s2).reshape(n, n)
        sm = (s1 * c2 + c1 * s2).reshape(n, n)
    else:
        cm, sm = thin(k, k, n)
    return (cm * scale).astype(BF16), (sm * -scale).astype(BF16)


def _fourier_kernel(c_ref, s_ref, ab_ref, o_ref):
    o_ref[...] = (_dot(c_ref[...], ab_ref[:, :FN_WIDTH]) + _dot(s_ref[...], ab_ref[:, FN_WIDTH:])).astype(BF16)


def _fourier_half_kernel(c_ref, s_ref, cmid_ref, ab_ref, plus_ref, minus_ref, mid_ref):
    a = ab_ref[:, :FN_WIDTH]
    p = _dot(c_ref[...], a)
    q = _dot(s_ref[...], ab_ref[:, FN_WIDTH:])
    plus_ref[...] = (p + q).astype(BF16)
    minus_ref[...] = (p - q).astype(BF16)
    mid_ref[...] = _dot(cmid_ref[...], a).astype(BF16)


def _fourier_half(fab, tables, n, tmf=512):
    t = fab.shape[0]
    b = t // n
    half = n // 2
    tiles = half // tmf
    cm, sm = tables
    cmid = jnp.broadcast_to(cm[half:half + 1], (8, n))
    row_out = lambda: pl.BlockSpec((tmf, FN_WIDTH), lambda i, bb: (bb * tiles + i, 0))
    plus, minus, mid = pl.pallas_call(
        _fourier_half_kernel,
        grid=(tiles, b),
        in_specs=[
            pl.BlockSpec((tmf, n), lambda i, bb: (i, 0)),
            pl.BlockSpec((tmf, n), lambda i, bb: (i, 0)),
            pl.BlockSpec((8, n), lambda i, bb: (0, 0)),
            pl.BlockSpec((n, 2 * FN_WIDTH), lambda i, bb: (bb, 0)),
        ],
        out_specs=[row_out(), row_out(), pl.BlockSpec((None, None, 8, FN_WIDTH), lambda i, bb: (i, bb, 0, 0))],
        out_shape=[jax.ShapeDtypeStruct((b * half, FN_WIDTH), BF16), jax.ShapeDtypeStruct((b * half, FN_WIDTH), BF16),
                   jax.ShapeDtypeStruct((tiles, b, 8, FN_WIDTH), BF16)],
        compiler_params=_params("parallel", "parallel"),
        name="fourier_half",
    )(cm, sm, cmid, fab)
    rev = np.zeros((tmf, tmf), np.float32)
    rev[np.arange(1, tmf), tmf - np.arange(1, tmf)] = 1.0
    blocks8 = tmf // 8

    def assemble(plus_ref, minus_ref, edge_ref, mid_ref, rev_ref, o_ref):
        j = pl.program_id(1) - tiles

        @pl.when(j < 0)
        def _():
            o_ref[...] = plus_ref[...]

        @pl.when(j >= 0)
        def _():
            body = _dot(rev_ref[...], minus_ref[...]).astype(BF16)
            first = jnp.where(j == 0, mid_ref[0:1, :], edge_ref[0:1, :])
            rows = lax.broadcasted_iota(jnp.int32, body.shape, 0)
            o_ref[...] = jnp.where(rows == 0, first, body)

    src_tile = lambda bb, i: bb * tiles + jnp.clip(2 * tiles - 1 - i, 0, tiles - 1)
    return pl.pallas_call(
        assemble,
        grid=(b, 2 * tiles),
        in_specs=[
            pl.BlockSpec((tmf, FN_WIDTH), lambda bb, i: (bb * tiles + jnp.minimum(i, tiles - 1), 0)),
            pl.BlockSpec((tmf, FN_WIDTH), lambda bb, i: (src_tile(bb, i), 0)),
            pl.BlockSpec((8, FN_WIDTH), lambda bb, i: (jnp.minimum(src_tile(bb, i) + 1, b * tiles - 1) * blocks8, 0)),
            pl.BlockSpec((None, None, 8, FN_WIDTH), lambda bb, i: (0, bb, 0, 0)),
            pl.BlockSpec((tmf, tmf), lambda bb, i: (0, 0)),
        ],
        out_specs=pl.BlockSpec((tmf, FN_WIDTH), lambda bb, i: (bb * 2 * tiles + i, 0)),
        out_shape=jax.ShapeDtypeStruct((t, FN_WIDTH), BF16),
        compiler_params=_params("parallel", "parallel"),
        name="fourier_assemble",
    )(plus, minus, minus, mid, jnp.asarray(rev, BF16))


def _fourier(fab, tables, n, tmf=512):
    if n >= 4 * tmf:
        return _fourier_half(fab, tables, n, tmf)
    t = fab.shape[0]
    tmf = min(tmf, n)
    tiles = n // tmf
    cm, sm = tables
    return pl.pallas_call(
        _fourier_kernel,
        grid=(tiles, t // n),
        in_specs=[
            pl.BlockSpec((tmf, n), lambda i, b: (i, 0)),
            pl.BlockSpec((tmf, n), lambda i, b: (i, 0)),
            pl.BlockSpec((n, 2 * FN_WIDTH), lambda i, b: (b, 0)),
        ],
        out_specs=pl.BlockSpec((tmf, FN_WIDTH), lambda i, b: (b * tiles + i, 0)),
        out_shape=jax.ShapeDtypeStruct((t, FN_WIDTH), BF16),
        compiler_params=_params("parallel", "parallel"),
        name="fourier",
    )(cm, sm, fab)


def _route(s_t, sb_t):
    def top2_sum(v):
        hi1, lo1 = jnp.maximum(v[0], v[1]), jnp.minimum(v[0], v[1])
        hi2, lo2 = jnp.maximum(v[2], v[3]), jnp.minimum(v[2], v[3])
        return jnp.maximum(hi1, hi2) + jnp.maximum(jnp.minimum(hi1, hi2), jnp.maximum(lo1, lo2))

    best = top2_sum(sb_t[0:EXPERTS_PER_GROUP])
    gsel = jnp.zeros_like(best, dtype=jnp.int32)
    for g in range(1, N_EXPERT_GROUPS):
        cand = top2_sum(sb_t[g * EXPERTS_PER_GROUP:(g + 1) * EXPERTS_PER_GROUP])
        better = cand > best
        gsel = jnp.where(better, g, gsel)
        best = jnp.where(better, cand, best)
    chosen = []
    for e in range(N_EXPERTS):
        g = e // EXPERTS_PER_GROUP
        beaten = jnp.zeros_like(gsel)
        for o in range(g * EXPERTS_PER_GROUP, (g + 1) * EXPERTS_PER_GROUP):
            if o == e:
                continue
            ahead = (sb_t[o] > sb_t[e]) | ((sb_t[o] == sb_t[e]) & (o < e))
            beaten = beaten + ahead.astype(jnp.int32)
        chosen.append((gsel == g) & (beaten < 2))
    picked = [jnp.where(chosen[e], s_t[e], 0.0) for e in range(N_EXPERTS)]
    denom = picked[0]
    for e in range(1, N_EXPERTS):
        denom = denom + picked[e]
    return chosen, [pk / denom for pk in picked]


def _pack_pairs(x):
    w = x.shape[1] // 2
    hi = pltpu.bitcast(x[:, :w].astype(BF16).astype(F32), jnp.uint32)
    lo = pltpu.bitcast(x[:, w:].astype(BF16).astype(F32), jnp.uint32)
    return hi | (lo >> 16)


def _mixout_kernel(n, x_ref, mod_ref, ab_ref, z_ref, zp_ref, zn_ref, yna_ref, ymla_ref, g_ref, cw_ref, wfn_ref,
                   wout_ref, g2_ref, wrc_ref, br_ref, x1_ref, h2_ref, route_ref, gtok_ref, cnt_ref):
    tm = x_ref.shape[0]
    i = pl.program_id(0)
    mod = mod_ref[...]
    gate1, shift2, scale2 = mod[2:3], mod[3:4], mod[4:5]

    z = z_ref[...].astype(F32)
    ridx = lax.broadcasted_iota(jnp.int32, z.shape, 0)
    at_start = (i * tm) % n == 0
    at_end = ((i + 1) * tm) % n == 0
    prev_row = jnp.where(at_start, 0.0, zp_ref[7:8, :].astype(F32))
    next_row = jnp.where(at_end, 0.0, zn_ref[0:1, :].astype(F32))
    z_m1 = jnp.where(ridx == 0, prev_row, pltpu.roll(z, 1, axis=0))
    z_p1 = jnp.where(ridx == tm - 1, next_row, pltpu.roll(z, tm - 1, axis=0))
    cw = cw_ref[...]
    y_conv = ab_ref[...].astype(F32) * (z_m1 * cw[0:1] + z * cw[1:2] + z_p1 * cw[2:3])

    y_fn = _dot(g_ref[...], wfn_ref[...])
    cat = jnp.concatenate([y_conv.astype(BF16), yna_ref[...], ymla_ref[...], y_fn.astype(BF16)], axis=-1)
    x1 = x_ref[...] + gate1 * _dot(cat, wout_ref[...])
    x1_ref[...] = x1

    h2 = _rms(x1, g2_ref[...]) * (1.0 + scale2) + shift2
    packed = _pack_pairs(h2)
    piece = packed.shape[1] // MOE_PIECES
    for p in range(MOE_PIECES):
        h2_ref[p] = packed[:, p * piece:(p + 1) * piece]
    h2_hi = h2.astype(BF16)
    h2_lo = (h2 - h2_hi.astype(F32)).astype(BF16)
    both = _dot(h2_hi, wrc_ref[...])
    logits = both[:, :LANES] + (both[:, LANES:] + _dot(h2_lo, wrc_ref[:, :LANES]))
    s = jax.nn.sigmoid(logits)
    s_t = jnp.transpose(s)
    sb_t = jnp.transpose(s + br_ref[...])
    chosen, gates = _route([s_t[e:e + 1] for e in range(N_EXPERTS)], [sb_t[e:e + 1] for e in range(N_EXPERTS)])

    @pl.when(i == 0)
    def _():
        cnt_ref[...] = jnp.zeros(cnt_ref.shape, F32)

    chosen_f = jnp.concatenate([ch.astype(F32) for ch in chosen], axis=0)
    before = lax.broadcasted_iota(jnp.int32, (tm, tm), 0) < lax.broadcasted_iota(jnp.int32, (tm, tm), 1)
    prefix = _dot(chosen_f.astype(BF16), jnp.where(before, 1.0, 0.0).astype(BF16))
    base = cnt_ref[...]
    rank = jnp.concatenate([base] * (tm // LANES), axis=1) + prefix
    cnt_ref[...] = base + jnp.sum(chosen_f, axis=1, keepdims=True)

    zero = jnp.zeros((1, tm), F32)
    seen = zero
    slots = [[zero, zero, zero], [zero, zero, zero]]
    for e in range(N_EXPERTS):
        for k in range(2):
            hit = chosen[e] & (seen == float(k))
            for j, val in enumerate((float(e), gates[e], rank[e:e + 1])):
                slots[k][j] = jnp.where(hit, val, slots[k][j])
        seen = seen + chosen_f[e:e + 1]
    (e_lo, g_lo, r_lo), (e_hi, g_hi, r_hi) = slots
    route_ref[...] = jnp.concatenate([g_lo, g_hi, e_lo, e_hi, r_lo, r_hi, zero, zero], axis=0)
    gates_t = jnp.concatenate([g_lo, g_hi, jnp.zeros((LANES - 2, tm), F32)], axis=0)
    gtok_ref[...] = jnp.transpose(gates_t)


def _mixout(x, mod, layer, n, parts, wts, lat, tm):
    t, d = x.shape
    ab, z, yna, ymla, g = parts
    nblk8 = t // 8
    per8 = tm // 8
    cond_row_of_tile = _cond_row(lat, tm, n, mod.shape[1] - 1)
    const2 = lambda i: (0, 0)
    lsel = lambda i: (layer, 0, 0)
    row = lambda w: pl.BlockSpec((tm, w), lambda i: (i, 0))
    in_specs = [
        row(d),
        pl.BlockSpec((None, None, 6, d), lambda i: (layer, cond_row_of_tile(i), 0, 0)),
        row(CONV_WIDTH),
        row(CONV_WIDTH),
        pl.BlockSpec((8, CONV_WIDTH), lambda i: (jnp.maximum(i * per8 - 1, 0), 0)),
        pl.BlockSpec((8, CONV_WIDTH), lambda i: (jnp.minimum((i + 1) * per8, nblk8 - 1), 0)),
        row(NA_WIDTH),
        row(MLA_HEADS * MLA_V_DIM),
        row(FN_WIDTH),
        pl.BlockSpec((None, 3, CONV_WIDTH), lsel),
        pl.BlockSpec((None, FN_WIDTH, FN_WIDTH), lsel),
        pl.BlockSpec((None, d, d), lsel),
        pl.BlockSpec((None, 1, d), lsel),
        pl.BlockSpec((d, 2 * LANES), const2),
        pl.BlockSpec((1, LANES), const2),
    ]
    out_specs = [
        row(d),
        pl.BlockSpec((MOE_PIECES, tm, d // 2 // MOE_PIECES), lambda i: (0, i, 0)),
        pl.BlockSpec((8, tm), lambda i: (0, i)),
        row(LANES),
        pl.BlockSpec((N_EXPERTS, LANES), const2),
    ]
    out_shape = [
        jax.ShapeDtypeStruct((t, d), F32),
        jax.ShapeDtypeStruct((MOE_PIECES, t, d // 2 // MOE_PIECES), jnp.uint32),
        jax.ShapeDtypeStruct((8, t), F32),
        jax.ShapeDtypeStruct((t, LANES), F32),
        jax.ShapeDtypeStruct((N_EXPERTS, LANES), F32),
    ]
    return pl.pallas_call(
        functools.partial(_mixout_kernel, n),
        grid=(t // tm,),
        in_specs=in_specs,
        out_specs=out_specs,
        out_shape=out_shape,
        compiler_params=_params("arbitrary"),
        name="mixout",
    )(x, mod, ab, z, z, z, yna, ymla, g, wts["conv_w"], wts["w_fn"], wts["w_out"], wts["norm2"],
      wts["wr_cat"], wts["b_router"])


def _slot_positions(route, counts, rb):
    cnt = counts[:, 0].astype(jnp.int32)
    padded = (cnt + rb - 1) // rb * rb
    ends = jnp.cumsum(padded)
    offs = ends - padded
    experts = route[2:4].astype(jnp.int32)
    ranks = route[4:6].astype(jnp.int32)
    pos = ranks
    for e in range(N_EXPERTS):
        pos = pos + jnp.where(experts == e, offs[e], 0)
    nblk = (2 * route.shape[1]) // rb + N_EXPERTS
    starts = jnp.arange(nblk, dtype=jnp.int32) * rb
    blk_expert = jnp.sum((starts[:, None] >= ends[None, :]).astype(jnp.int32), axis=1)
    used = blk_expert < N_EXPERTS
    blk_expert = jnp.where(used, blk_expert, 0)
    valid_end = jnp.sum(jnp.where(blk_expert[:, None] == jnp.arange(N_EXPERTS)[None, :], (offs + cnt)[None, :], 0), axis=1)
    blk_valid = jnp.where(used, jnp.clip(valid_end - starts, 0, rb), 0)
    return pos, jnp.stack([blk_expert, blk_valid])


def _sc_mesh():
    return plsc.VectorSubcoreMesh(core_axis_name="c", subcore_axis_name="s")


def _sc_pipeline(body, nwin, in_specs, out_specs):
    return pltpu.emit_pipeline(body, grid=(nwin,), in_specs=in_specs, out_specs=out_specs,
                               core_axis_name=("c", "s"), dimension_semantics=(pltpu.PARALLEL,))


def _row_scatter(table, idx_a, idx_b, nrows):
    b, w = table.shape
    win = SC_WINDOW
    idx_spec = pl.BlockSpec((1, win), lambda i: (0, i))

    @functools.partial(pl.kernel, out_type=jax.ShapeDtypeStruct((nrows, w), table.dtype), mesh=_sc_mesh(),
                       scratch_types=[])
    def scatter(table_hbm, ia_hbm, ib_hbm, out_hbm):
        def body(rows_vmem, ia_vmem, ib_vmem):
            pltpu.sync_copy(rows_vmem, out_hbm.at[ia_vmem.at[0]])
            pltpu.sync_copy(rows_vmem, out_hbm.at[ib_vmem.at[0]])

        _sc_pipeline(body, b // win, [pl.BlockSpec((win, w), lambda i: (i, 0)), idx_spec, idx_spec], [])(
            table_hbm, ia_hbm, ib_hbm)

    return scatter(table, idx_a.reshape(1, b), idx_b.reshape(1, b))


def _row_gather(table, idx):
    b = idx.shape[0]
    w = table.shape[1]
    win = SC_WINDOW

    @functools.partial(pl.kernel, out_type=jax.ShapeDtypeStruct((b, w), table.dtype), mesh=_sc_mesh(),
                       scratch_types=[])
    def gather(table_hbm, idx_hbm, out_hbm):
        def body(idx_vmem, out_vmem):
            pltpu.sync_copy(table_hbm.at[idx_vmem.at[0]], out_vmem)

        _sc_pipeline(body, b // win, [pl.BlockSpec((1, win), lambda i: (0, i))],
                     [pl.BlockSpec((win, w), lambda i: (i, 0))])(idx_hbm, out_hbm)

    return gather(table, idx.reshape(1, b))


def _ffn_kernel(blk_ref, xs_ref, w13_ref, w2_ref, y_ref):
    i = pl.program_id(0)
    e = blk_ref[0, i]
    nvalid = blk_ref[1, i]

    @pl.when(nvalid > 0)
    def _():
        packed = jnp.concatenate([xs_ref[0], xs_ref[1]], axis=-1)
        live = lax.broadcasted_iota(jnp.int32, packed.shape, 0) < nvalid
        xb = _unpack_pairs(jnp.where(live, packed, jnp.uint32(0))).astype(BF16)
        up = _dot(xb, w13_ref[e])
        a, b = up[:, :EXPERT_FF], up[:, EXPERT_FF:]
        hid = (a * jax.nn.sigmoid(a)) * b
        y = _pack_pairs(_dot(hid.astype(BF16), w2_ref[e]))
        half = y.shape[1] // 2
        y_ref[0] = y[:, :half]
        y_ref[1] = y[:, half:]

    @pl.when(nvalid == 0)
    def _():
        y_ref[...] = jnp.zeros(y_ref.shape, y_ref.dtype)


def _expert_ffn(xs, blk, w13, w2, layer, rb):
    pieces, nrows, w = xs.shape
    d = 2 * pieces * w
    resident = dict(pipeline_mode=pl.Buffered(1))
    used = lambda i, blk: (0, jnp.where(blk[1, i] > 0, i, 0), 0)
    return pl.pallas_call(
        _ffn_kernel,
        grid_spec=pltpu.PrefetchScalarGridSpec(
            num_scalar_prefetch=1,
            grid=(nrows // rb,),
            in_specs=[
                pl.BlockSpec((pieces, rb, w), used),
                pl.BlockSpec((None, N_EXPERTS, d, 2 * EXPERT_FF), lambda i, blk: (layer, 0, 0, 0), **resident),
                pl.BlockSpec((None, N_EXPERTS, EXPERT_FF, d), lambda i, blk: (layer, 0, 0, 0), **resident),
            ],
            out_specs=pl.BlockSpec((pieces, rb, w), lambda i, blk: (0, i, 0)),
        ),
        out_shape=jax.ShapeDtypeStruct(xs.shape, xs.dtype),
        compiler_params=_params("parallel"),
        name="expert_ffn",
    )(blk, xs, w13, w2)


def _combine_kernel(final, x1_ref, gtok_ref, mod_ref, nf_ref, y_ref, o_ref):
    out = _moe_residual(x1_ref[...], gtok_ref, y_ref, mod_ref)
    if final:
        out = _rms(out, nf_ref[...])
    o_ref[...] = out


def _combine(x1, gtok, y_tok, mod, layer, n, norm_f, final, lat, tc=512):
    t, d = x1.shape
    cond_row_of_tile = _cond_row(lat, tc, n, mod.shape[1] - 1)
    row = lambda w: pl.BlockSpec((tc, w), lambda i: (i, 0))
    return pl.pallas_call(
        functools.partial(_combine_kernel, final),
        grid=(t // tc,),
        in_specs=[
            row(d),
            row(LANES),
            pl.BlockSpec((None, None, 6, d), lambda i: (layer, cond_row_of_tile(i), 0, 0)),
            pl.BlockSpec((1, d), lambda i: (0, 0)),
            pl.BlockSpec(y_tok.shape[:2] + (tc, y_tok.shape[3]), lambda i: (0, 0, i, 0)),
        ],
        out_specs=row(d),
        out_shape=jax.ShapeDtypeStruct((t, d), F32),
        compiler_params=_params("parallel"),
        name="combine",
    )(x1, gtok, mod, norm_f, y_tok)


def _moe_layer(x, mod, layer, n, parts, wts, lat, tm, rb):
    t = x.shape[0]
    x1, h2, route, gtok, counts = _mixout(x, mod, layer, n, parts, wts, lat, tm)
    pos, blk = _slot_positions(route, counts, rb)
    pieces, _, w = h2.shape
    nrows = blk.shape[1] * rb
    piece_base = (jnp.arange(pieces, dtype=jnp.int32) * nrows)[:, None]
    idx = [(piece_base + pos[s][None, :]).reshape(-1) for s in range(2)]
    xs = _row_scatter(h2.reshape(pieces * t, w), idx[0], idx[1], pieces * nrows).reshape(pieces, nrows, w)
    y = _expert_ffn(xs, blk, wts["w13"], wts["w2"], layer, rb)
    back = (piece_base[:, :, None] + pos[None, :, :]).reshape(-1)
    y_tok = _row_gather(y.reshape(pieces * nrows, w), back).reshape(pieces, 2, t, w)
    return x1, (gtok, y_tok)


def _swap_halves(w):
    nf = MLA_ROPE_DIM // 4
    idx = np.arange(MLA_ROPE_DIM).reshape(2, 2, nf)[:, ::-1, :].reshape(-1)
    return w[..., idx]


def _pack_weights(w_in, mla_wq_up, mla_wkv_up, w1, w3, w2, w_router, b_router):
    depth, d, _ = w_in.shape
    zeros = lambda w: jnp.zeros((depth, d, w), w_in.dtype)
    w_kr = w_in[..., 1920:1952]
    pad_rope = lambda w: jnp.concatenate([zeros(MLA_NOPE_DIM), w, zeros(MLA_QK_PAD - MLA_NOPE_DIM - MLA_ROPE_DIM)], -1)
    w_main = jnp.concatenate([w_in[..., :1920], pad_rope(w_kr), pad_rope(_swap_halves(w_kr)), w_in[..., 1952:]], -1)

    wq = mla_wq_up.reshape(depth, MLA_Q_LORA, MLA_HEADS, MLA_NOPE_DIM + MLA_ROPE_DIM)
    q_nope, q_rope = wq[..., :MLA_NOPE_DIM], wq[..., MLA_NOPE_DIM:]
    tail = jnp.zeros(q_rope.shape[:-1] + (MLA_QK_PAD - MLA_NOPE_DIM - MLA_ROPE_DIM,), wq.dtype)
    wq_a = jnp.concatenate([q_nope, q_rope, tail], -1).reshape(depth, MLA_Q_LORA, -1)
    wq_b = jnp.concatenate([jnp.zeros_like(q_nope), _swap_halves(q_rope), tail], -1).reshape(depth, MLA_Q_LORA, -1)

    wkv = mla_wkv_up.reshape(depth, MLA_KV_LORA, MLA_HEADS, MLA_NOPE_DIM + MLA_V_DIM)
    k_nope, v_up = wkv[..., :MLA_NOPE_DIM], wkv[..., MLA_NOPE_DIM:]
    k_tail = jnp.zeros(k_nope.shape[:-1] + (MLA_QK_PAD - MLA_NOPE_DIM,), wkv.dtype)
    wk_a = jnp.concatenate([k_nope, k_tail], -1).reshape(depth, MLA_KV_LORA, -1)
    wv = v_up.reshape(depth, MLA_KV_LORA, -1)
    v_tail = jnp.zeros(v_up.shape[:-1] + (MLA_V_PAD - MLA_V_DIM,), wkv.dtype)
    wv_ext = jnp.concatenate([v_up, v_tail], -1).reshape(depth, MLA_KV_LORA, -1)
    vone = np.zeros((1, MLA_HEADS * MLA_V_PAD), np.float32)
    vone[0, MLA_V_DIM::MLA_V_PAD] = 1.0

    wr = jnp.pad(w_router, ((0, 0), (0, LANES - N_EXPERTS)))
    wr_hi = wr.astype(BF16)
    wr_lo = (wr - wr_hi.astype(F32)).astype(BF16)
    return {
        "w_in": w_main.astype(BF16), "wq_a": wq_a.astype(BF16), "wq_b": wq_b.astype(BF16),
        "wk_a": wk_a.astype(BF16), "wv": wv.astype(BF16), "wv_ext": wv_ext.astype(BF16),
        "vone_ext": jnp.asarray(vone),
        "w13": jnp.concatenate([w1, w3], -1).astype(BF16), "w2": w2.astype(BF16),
        "wr_cat": jnp.concatenate([wr_hi, wr_lo], axis=-1),
        "b_router": jnp.pad(b_router, (0, LANES - N_EXPERTS)).reshape(1, LANES).astype(F32),
    }


def _channel_dft():
    c = np.arange(FN_GROUP_DIM)
    ang = 2.0 * np.pi * ((c[:, None] * c[None, :]) % FN_GROUP_DIM) / FN_GROUP_DIM
    out = np.zeros((FN_WIDTH, 2 * FN_WIDTH), np.float32)
    for g in range(FN_GROUPS):
        sl = slice(g * FN_GROUP_DIM, (g + 1) * FN_GROUP_DIM)
        out[sl, sl] = np.cos(ang) * FN_GROUP_DIM ** -0.5
        out[sl, FN_WIDTH + g * FN_GROUP_DIM:FN_WIDTH + (g + 1) * FN_GROUP_DIM] = np.sin(ang) * FN_GROUP_DIM ** -0.5
    return jnp.asarray(out, BF16)


def _rope_tables(n):
    tok = jnp.arange(n)
    pos = jnp.stack([tok // GRID_W, tok % GRID_W], axis=-1).astype(F32)
    nf = MLA_ROPE_DIM // 4
    freqs = ROPE_THETA ** (-jnp.arange(nf, dtype=F32) / nf)
    ang = pos[:, :, None] * freqs
    cos = jnp.broadcast_to(jnp.cos(ang)[:, :, None, :], (n, 2, 2, nf)).reshape(n, MLA_ROPE_DIM)
    sin = jnp.sin(ang)
    sin = jnp.stack([-sin, sin], axis=2).reshape(n, MLA_ROPE_DIM)
    pad = jnp.zeros((n, MLA_QK_PAD - MLA_NOPE_DIM - MLA_ROPE_DIM), F32)
    cos_t = jnp.concatenate([jnp.ones((n, MLA_NOPE_DIM), F32), cos, pad], -1)
    sin_t = jnp.concatenate([jnp.zeros((n, MLA_NOPE_DIM), F32), sin, pad], -1)
    return cos_t, sin_t


def kernel(x_prompt, x_sample, cache_na_k, cache_na_v, cache_mla_ckv, cache_mla_krope, c, c_ctx, w_ada, b_ada,
           norm1, norm2, w_in, conv_w, na_rpb, mla_gq, mla_wq_up, mla_gkv, mla_wkv_up, w_fn, w_out, w_router,
           b_router, w1, w3, w2, norm_f):
    bp, seq, d = x_prompt.shape
    bd, dec_seq, _ = x_sample.shape
    depth = w_in.shape[0]

    wts = _pack_weights(w_in, mla_wq_up, mla_wkv_up, w1, w3, w2, w_router, b_router)
    wts.update({
        "norm1": norm1.reshape(depth, 1, d), "norm2": norm2.reshape(depth, 1, d),
        "mla_gq": mla_gq.reshape(depth, 1, -1), "mla_gkv": mla_gkv.reshape(depth, 1, -1),
        "conv_w": conv_w, "w_fn": w_fn.astype(BF16), "w_out": w_out.astype(BF16),
        "norm_f": norm_f.reshape(1, d), "cs_bd": _channel_dft(),
    })

    cond = jnp.concatenate([c, jnp.zeros((-(bd + 1) % 8, d), c.dtype), c_ctx[None, :]], axis=0)
    mod = _ada_modulation(cond, w_ada, b_ada)

    xp = x_prompt.reshape(bp * seq, d)
    tables = _dft_tables(seq)
    caches = [
        jnp.zeros((bp, depth, NA_HEADS, seq, NA_HEAD_DIM), F32), jnp.zeros((bp, depth, NA_HEADS, seq, NA_HEAD_DIM), F32),
        jnp.zeros((bp, depth, seq, MLA_KV_LORA), F32), jnp.zeros((bp, depth, seq, MLA_ROPE_DIM), F32)]
    pending = None
    for layer in range(depth):
        outs = _premix(xp, mod, layer, seq, wts, None, False, seq, caches, pending)
        ab, z, qn, kn, vn, km, fab, qm, vm = outs[:9]
        caches = outs[9:13]
        if pending is not None:
            xp = outs[13]
        yna, ymla = _ctx_attention(qn, kn, vn, qm, km, vm, seq)
        g = _fourier(fab, tables, seq)
        xp, pending = _moe_layer(xp, mod, layer, seq, (ab, z, yna, ymla, g), wts, False, seq, MOE_ROW_BLOCK_CTX)
    xp = _combine(xp, pending[0], pending[1], mod, depth - 1, seq, wts["norm_f"], True, False, seq)
    new_na_k, new_na_v, new_ckv, new_krope = caches

    xs = x_sample.reshape(bd * dec_seq, d)
    kx, vxt = _ctx_kv(cache_mla_ckv, cache_mla_krope, wts["wk_a"], wts["wv_ext"], wts["vone_ext"])
    na_bias = _na_bias(na_rpb, dec_seq // GRID_W)
    rope = _rope_tables(dec_seq)
    tables = _dft_tables(dec_seq)
    pending = None
    for layer in range(depth):
        outs = _premix(xs, mod, layer, dec_seq, wts, rope, True, TM_LAT_PREMIX, None, pending)
        ab, z, qn, kn, vn, km, fab, qt, vt = outs[:9]
        if pending is not None:
            xs = outs[9]
        yna = _na_lat_attention(qn, kn, vn, cache_na_k, cache_na_v, na_bias, layer, dec_seq)
        ymla = _mla_lat_attention(qt, km, vt, kx, vxt, layer, dec_seq)
        g = _fourier(fab, tables, dec_seq)
        xs, pending = _moe_layer(xs, mod, layer, dec_seq, (ab, z, yna, ymla, g), wts, True, TM_LAT_MIXOUT,
                                 MOE_ROW_BLOCK)
    xs = _combine(xs, pending[0], pending[1], mod, depth - 1, dec_seq, wts["norm_f"], True, True, TM_LAT_PREMIX)

    return (xp.reshape(bp, seq, d), xs.reshape(bd, dec_seq, d), new_na_k, new_na_v, new_ckv, new_krope)
```

```python
import functools
import math

import numpy as np
import jax
import jax.numpy as jnp
from jax import lax
from jax.experimental import pallas as pl
from jax.experimental.pallas import tpu as pltpu
from jax.experimental.pallas import tpu_sc as plsc

F32 = jnp.float32
BF16 = jnp.bfloat16

GRID_W = 64
CONV_WIDTH = 256
NA_HEADS = 4
NA_HEAD_DIM = 64
NA_WIDTH = NA_HEADS * NA_HEAD_DIM
NA_KH = 8
NA_KW = 16
MLA_HEADS = 4
MLA_Q_LORA = 256
MLA_KV_LORA = 128
MLA_NOPE_DIM = 64
MLA_ROPE_DIM = 32
MLA_V_DIM = 64
MLA_QK_PAD = 128
MLA_V_PAD = 96
MLA_KEY_SUB = 256
LOG2E = 1.4426950408889634
FN_GROUPS = 4
FN_GROUP_DIM = 64
FN_WIDTH = FN_GROUPS * FN_GROUP_DIM
N_EXPERTS = 16
N_EXPERT_GROUPS = 4
EXPERTS_PER_GROUP = N_EXPERTS // N_EXPERT_GROUPS
EXPERT_FF = 256
ROPE_THETA = 10000.0
EPS = 1e-6
NEG_INF = -1e30
LANES = 128

NA_SCALE = NA_HEAD_DIM ** -0.5
MLA_SCALE = (MLA_NOPE_DIM + MLA_ROPE_DIM) ** -0.5

NA_Q_ROWS = 4
NA_WIN_ROWS = 12

TM_LAT_PREMIX = 512
TM_LAT_MIXOUT = 512
MOE_ROW_BLOCK_CTX = 512
MOE_ROW_BLOCK = 512
SC_WINDOW = 128
MOE_PIECES = 2

VMEM_LIMIT = 56 * 1024 * 1024

_C_AB, _C_AC, _C_AU, _C_Q, _C_K, _C_V, _C_CQ = 0, 256, 512, 768, 1024, 1280, 1536
_C_CKV, _C_KR, _C_KRS, _C_FU, _C_END = 1792, 1920, 2048, 2176, 2432


def _nt_dot(a, b):
    return lax.dot_general(a, b, (((1,), (1,)), ((), ())), preferred_element_type=F32)


def _dot(a, b):
    return jnp.dot(a, b, preferred_element_type=F32)


def _rms(x, g):
    return x * lax.rsqrt(jnp.mean(x * x, axis=-1, keepdims=True) + EPS) * g


def _params(*sem, flags=None):
    return pltpu.CompilerParams(dimension_semantics=sem, vmem_limit_bytes=VMEM_LIMIT, flags=flags)


def _ada_kernel(c_ref, w_ref, b_ref, o_ref):
    cnd = c_ref[...]
    act = cnd * jax.nn.sigmoid(cnd)
    o_ref[...] = _dot(act.astype(BF16), w_ref[...].astype(BF16)) + b_ref[...]


def _ada_modulation(cond, w_ada, b_ada):
    depth, d, six_d = w_ada.shape
    r = cond.shape[0]
    tn = 1024
    out = pl.pallas_call(
        _ada_kernel,
        grid=(depth, six_d // tn),
        in_specs=[
            pl.BlockSpec((r, d), lambda l, j: (0, 0)),
            pl.BlockSpec((None, d, tn), lambda l, j: (l, 0, j)),
            pl.BlockSpec((None, 1, tn), lambda l, j: (l, 0, j)),
        ],
        out_specs=pl.BlockSpec((None, r, tn), lambda l, j: (l, 0, j)),
        out_shape=jax.ShapeDtypeStruct((depth, r, six_d), F32),
        compiler_params=_params("parallel", "parallel"),
        name="ada_modulation",
    )(cond, w_ada, b_ada.reshape(depth, 1, six_d))
    return out.reshape(depth, r, 6, d)


def _unpack_pairs(p):
    hi = pltpu.bitcast(p & jnp.uint32(0xFFFF0000), F32)
    lo = pltpu.bitcast(p << 16, F32)
    return jnp.concatenate([hi, lo], axis=-1)


def _moe_residual(x1, gtok_ref, y_ref, mod_ref):
    g = gtok_ref[...]
    y_lo = _unpack_pairs(jnp.concatenate([y_ref[0, 0], y_ref[1, 0]], axis=-1))
    y_hi = _unpack_pairs(jnp.concatenate([y_ref[0, 1], y_ref[1, 1]], axis=-1))
    return x1 + mod_ref[...][5:6] * (g[:, 0:1] * y_lo + g[:, 1:2] * y_hi)


def _premix_kernel(lat, fused, n_in, *refs):
    (x_ref, mod_ref, g1_ref, w_ref, gq_ref, wqa_ref, wqb_ref, gkv_ref, wka_ref, wv_ref, vone_ref, cs_ref,
     cos_ref, sin_ref) = refs[:14]
    outs = refs[n_in:]
    (ab_ref, z_ref, qn_ref, kn_ref, vn_ref, km_ref, fab_ref) = outs[:7]

    x = x_ref[...]
    if fused:
        gtok_ref, y_ref, modp_ref = refs[n_in - 3:n_in]
        x = _moe_residual(x, gtok_ref, y_ref, modp_ref)
        outs[-1][...] = x
    mod = mod_ref[...]
    h = _rms(x, g1_ref[...]) * (1.0 + mod[1:2]) + mod[0:1]
    p = _dot(h.astype(BF16), w_ref[...])

    ab_ref[...] = p[:, _C_AB:_C_AC].astype(BF16)
    z_ref[...] = (p[:, _C_AC:_C_AU] * p[:, _C_AU:_C_Q]).astype(BF16)
    k_na = p[:, _C_K:_C_V]
    v_na = p[:, _C_V:_C_CQ]
    qn_ref[...] = (p[:, _C_Q:_C_K] * NA_SCALE).astype(BF16)
    kn_ref[...] = k_na.astype(BF16)
    vn_ref[...] = v_na.astype(BF16)

    cqn = _rms(p[:, _C_CQ:_C_CKV], gq_ref[...]).astype(BF16)
    ckvn = _rms(p[:, _C_CKV:_C_KR], gkv_ref[...])
    ckvn_b = ckvn.astype(BF16)
    qa = _dot(cqn, wqa_ref[...])
    kva = _dot(ckvn_b, wka_ref[...])
    v_mla = _dot(ckvn_b, wv_ref[...]) + vone_ref[...]
    kr = p[:, _C_KR:_C_KRS]
    if lat:
        cos = cos_ref[...]
        sin = sin_ref[...]
        qb = _dot(cqn, wqb_ref[...])
        krot = kr * cos + p[:, _C_KRS:_C_FU] * sin
        qt_ref, vt_ref = outs[7:9]
    else:
        krot = kr
        qm_ref, vm_ref, ck_ref, cv_ref, cckv_ref, ckr_ref = outs[7:13]
    for hd in range(MLA_HEADS):
        sl = slice(hd * MLA_QK_PAD, (hd + 1) * MLA_QK_PAD)
        km_ref[:, sl] = (kva[:, sl] + krot).astype(BF16)
        if lat:
            qh = (qa[:, sl] * cos + qb[:, sl] * sin) * (MLA_SCALE * LOG2E)
            qt_ref[sl, :] = jnp.transpose(qh).astype(BF16)
        else:
            qm_ref[:, sl] = (qa[:, sl] * MLA_SCALE).astype(BF16)

    fab_ref[...] = _dot(p[:, _C_FU:_C_END].astype(BF16), cs_ref[...]).astype(BF16)

    if lat:
        for j in range(v_mla.shape[1] // LANES):
            sl = slice(j * LANES, (j + 1) * LANES)
            vt_ref[sl, :] = jnp.transpose(v_mla[:, sl]).astype(BF16)
    else:
        vm_ref[...] = v_mla.astype(BF16)
        for hd in range(NA_HEADS):
            sl = slice(hd * NA_HEAD_DIM, (hd + 1) * NA_HEAD_DIM)
            ck_ref[hd] = k_na[:, sl]
            cv_ref[hd] = v_na[:, sl]
        cckv_ref[...] = ckvn
        ckr_ref[...] = kr[:, MLA_NOPE_DIM:MLA_NOPE_DIM + MLA_ROPE_DIM]


def _cond_row(lat, tm, n, ctx_row):
    return (lambda i: (i * tm) // n) if lat else (lambda i: ctx_row)


def _premix(x, mod, layer, n, wts, rope, lat, tm, caches=None, pending=None):
    t, d = x.shape
    if lat:
        cos_t, sin_t = rope
        wv, vone = wts["wv_ext"], wts["vone_ext"]
    else:
        cos_t = sin_t = jnp.zeros((8, LANES), F32)
        wv, vone = wts["wv"], jnp.zeros((1, MLA_HEADS * MLA_V_DIM), F32)
    vw = wv.shape[-1]
    qw = MLA_HEADS * MLA_QK_PAD
    tiles_per_seq = n // tm
    cond_row = _cond_row(lat, tm, n, mod.shape[1] - 1)
    const = lambda *_: (0, 0)
    lsel = lambda *_: (layer, 0, 0)
    rope_spec = (pl.BlockSpec((tm, LANES), lambda i: (i % tiles_per_seq, 0)) if lat
                 else pl.BlockSpec((8, LANES), const))
    in_specs = [
        pl.BlockSpec((tm, d), lambda i: (i, 0)),
        pl.BlockSpec((None, None, 6, d), lambda i: (layer, cond_row(i), 0, 0)),
        pl.BlockSpec((None, 1, d), lsel),
        pl.BlockSpec((None, d, _C_END), lsel),
        pl.BlockSpec((None, 1, MLA_Q_LORA), lsel),
        pl.BlockSpec((None, MLA_Q_LORA, qw), lsel),
        pl.BlockSpec((None, MLA_Q_LORA, qw), lsel),
        pl.BlockSpec((None, 1, MLA_KV_LORA), lsel),
        pl.BlockSpec((None, MLA_KV_LORA, qw), lsel),
        pl.BlockSpec((None, MLA_KV_LORA, vw), lsel),
        pl.BlockSpec((1, vw), const),
        pl.BlockSpec((FN_WIDTH, 2 * FN_WIDTH), const),
        rope_spec,
        rope_spec,
    ]
    row = lambda w: pl.BlockSpec((tm, w), lambda i: (i, 0))
    widths = [CONV_WIDTH, CONV_WIDTH, NA_WIDTH, NA_WIDTH, NA_WIDTH, qw, 2 * FN_WIDTH]
    out_specs = [row(w) for w in widths]
    out_shape = [jax.ShapeDtypeStruct((t, w), BF16) for w in widths]
    if lat:
        out_specs += [pl.BlockSpec((None, qw, tm), lambda i: (i, 0, 0)),
                      pl.BlockSpec((None, vw, tm), lambda i: (i, 0, 0))]
        out_shape += [jax.ShapeDtypeStruct((t // tm, qw, tm), BF16),
                      jax.ShapeDtypeStruct((t // tm, vw, tm), BF16)]
    else:
        assert tm == n
        b = t // n
        depth = wts["w_in"].shape[0]
        out_specs += [
            row(qw), row(vw),
            pl.BlockSpec((None, None, NA_HEADS, n, NA_HEAD_DIM), lambda i: (i, layer, 0, 0, 0)),
            pl.BlockSpec((None, None, NA_HEADS, n, NA_HEAD_DIM), lambda i: (i, layer, 0, 0, 0)),
            pl.BlockSpec((None, None, n, MLA_KV_LORA), lambda i: (i, layer, 0, 0)),
            pl.BlockSpec((None, None, n, MLA_ROPE_DIM), lambda i: (i, layer, 0, 0)),
        ]
        out_shape += [
            jax.ShapeDtypeStruct((t, qw), BF16), jax.ShapeDtypeStruct((t, vw), BF16),
            jax.ShapeDtypeStruct((b, depth, NA_HEADS, n, NA_HEAD_DIM), F32),
            jax.ShapeDtypeStruct((b, depth, NA_HEADS, n, NA_HEAD_DIM), F32),
            jax.ShapeDtypeStruct((b, depth, n, MLA_KV_LORA), F32),
            jax.ShapeDtypeStruct((b, depth, n, MLA_ROPE_DIM), F32),
        ]
    args = [x, mod, wts["norm1"], wts["w_in"], wts["mla_gq"], wts["wq_a"], wts["wq_b"], wts["mla_gkv"],
            wts["wk_a"], wv, vone, wts["cs_bd"], cos_t, sin_t]
    aliases = {}
    if caches is not None:
        first_cache_out = len(out_shape) - len(caches)
        aliases = {len(args) + j: first_cache_out + j for j in range(len(caches))}
        in_specs += [pl.BlockSpec(memory_space=pl.ANY)] * len(caches)
        args += list(caches)
    if pending is not None:
        gtok, y_tok = pending
        in_specs += [
            row(LANES),
            pl.BlockSpec(y_tok.shape[:2] + (tm, y_tok.shape[3]), lambda i: (0, 0, i, 0)),
            pl.BlockSpec((None, None, 6, d), lambda i: (layer - 1, cond_row(i), 0, 0)),
        ]
        args += [gtok, y_tok, mod]
        out_specs = out_specs + [row(d)]
        out_shape = out_shape + [jax.ShapeDtypeStruct((t, d), F32)]
    return pl.pallas_call(
        functools.partial(_premix_kernel, lat, pending is not None, len(args)),
        grid=(t // tm,),
        in_specs=in_specs,
        out_specs=out_specs,
        out_shape=out_shape,
        input_output_aliases=aliases,
        compiler_params=_params("parallel"),
        name="premix_lat" if lat else "premix_ctx",
    )(*args)


def _softmax_attend(q, k, v):
    s = _nt_dot(q, k)
    m = jnp.max(s, axis=-1, keepdims=True)
    p = jnp.exp(s - m)
    l = jnp.sum(p, axis=-1, keepdims=True)
    return _dot(p.astype(BF16), v) / l


def _ctx_attn_kernel(qn_ref, kn_ref, vn_ref, qm_ref, km_ref, vm_ref, yna_ref, ymla_ref):
    for hd in range(NA_HEADS):
        sl = slice(hd * NA_HEAD_DIM, (hd + 1) * NA_HEAD_DIM)
        yna_ref[:, sl] = _softmax_attend(qn_ref[:, sl], kn_ref[:, sl], vn_ref[:, sl]).astype(BF16)
    for hd in range(MLA_HEADS):
        sq = slice(hd * MLA_QK_PAD, (hd + 1) * MLA_QK_PAD)
        sv = slice(hd * MLA_V_DIM, (hd + 1) * MLA_V_DIM)
        ymla_ref[:, sv] = _softmax_attend(qm_ref[:, sq], km_ref[:, sq], vm_ref[:, sv]).astype(BF16)


def _ctx_attention(qn, kn, vn, qm, km, vm, n):
    t = qn.shape[0]
    spec = lambda w: pl.BlockSpec((n, w), lambda b: (b, 0))
    ins = [qn, kn, vn, qm, km, vm]
    return pl.pallas_call(
        _ctx_attn_kernel,
        grid=(t // n,),
        in_specs=[spec(a.shape[1]) for a in ins],
        out_specs=[spec(NA_WIDTH), spec(MLA_HEADS * MLA_V_DIM)],
        out_shape=[jax.ShapeDtypeStruct((t, NA_WIDTH), BF16),
                   jax.ShapeDtypeStruct((t, MLA_HEADS * MLA_V_DIM), BF16)],
        compiler_params=_params("parallel"),
        name="ctx_attention",
    )(*ins)


def _mla_lat_kernel(qt_ref, k_ref, vt_ref, kx_ref, vxt_ref, o_ref, s_scr, p_scr):
    nchunk, _, kc = vt_ref.shape
    tq = qt_ref.shape[1]
    sub = MLA_KEY_SUB

    ksl = lambda hd: slice(hd * MLA_QK_PAD, (hd + 1) * MLA_QK_PAD)
    vsl = lambda hd: slice(hd * MLA_V_PAD, (hd + 1) * MLA_V_PAD)

    def scores(slot, k_of, nk):
        cmax = []
        for hd in range(MLA_HEADS):
            qt = qt_ref[ksl(hd), :]
            part = None
            for j in range(0, nk, sub):
                st = _dot(k_of(hd, j), qt)
                s_scr[slot, hd, j:j + sub, :] = st
                blk = jnp.max(st.reshape(sub // 8, 8, tq), axis=0)
                part = blk if part is None else jnp.maximum(part, blk)
            cmax.append(jnp.max(part, axis=0, keepdims=True))
        return tuple(cmax)

    def attend(slot, cmax, state, vt_of, nk):
        new = []
        for hd in range(MLA_HEADS):
            m_i, acc = state[hd]
            m_new = jnp.maximum(m_i, cmax[hd])
            for j in range(0, nk, sub):
                p_scr[hd, j:j + sub, :] = jnp.exp2(s_scr[slot, hd, j:j + sub, :] - m_new).astype(BF16)
            acc = jnp.exp2(m_i - m_new) * acc + _dot(vt_of(hd), p_scr[hd, 0:nk, :])
            new.append((m_new, acc))
        return tuple(new)

    lat_keys = lambda c: (lambda hd, j: k_ref[pl.ds(pl.multiple_of(c * kc, kc) + j, sub), ksl(hd)])
    past = kx_ref.shape[0]
    state = tuple((jnp.full((1, tq), NEG_INF, F32), jnp.zeros((MLA_V_PAD, tq), F32)) for _ in range(MLA_HEADS))
    cmax_ctx = scores(1, lambda hd, j: kx_ref[j:j + sub, ksl(hd)], past)
    cmax = scores(0, lat_keys(0), kc)
    state = attend(1, cmax_ctx, state, lambda hd: vxt_ref[vsl(hd), :], past)

    lat_vals = lambda c: (lambda hd: vt_ref[c, vsl(hd), :])

    def body(i, carry):
        cmax0, state = carry
        c = 2 * i
        cmax1 = scores(1, lat_keys(c + 1), kc)
        state = attend(0, cmax0, state, lat_vals(c), kc)
        cmax0 = scores(0, lat_keys(c + 2), kc)
        state = attend(1, cmax1, state, lat_vals(c + 1), kc)
        return cmax0, state

    cmax, state = lax.fori_loop(0, nchunk // 2 - 1, body, (cmax, state))
    cmax1 = scores(1, lat_keys(nchunk - 1), kc)
    state = attend(0, cmax, state, lat_vals(nchunk - 2), kc)
    state = attend(1, cmax1, state, lat_vals(nchunk - 1), kc)
    o_t = jnp.concatenate([acc[:MLA_V_DIM] / acc[MLA_V_DIM:MLA_V_DIM + 1] for _, acc in state], axis=0)
    o_ref[...] = jnp.transpose(o_t).astype(BF16)


def _mla_lat_attention(qt, km, vt, kx, vxt, layer, n):
    ntile, qw, tq = qt.shape
    t = ntile * tq
    past = kx.shape[2]
    qpb = n // tq
    return pl.pallas_call(
        _mla_lat_kernel,
        grid=(t // n, qpb),
        in_specs=[
            pl.BlockSpec((None, qw, tq), lambda b, i: (b * qpb + i, 0, 0)),
            pl.BlockSpec((n, km.shape[1]), lambda b, i: (b, 0)),
            pl.BlockSpec((qpb, vt.shape[1], tq), lambda b, i: (b, 0, 0)),
            pl.BlockSpec((None, None, past, kx.shape[3]), lambda b, i: (layer, b, 0, 0)),
            pl.BlockSpec((None, None, vxt.shape[2], past), lambda b, i: (layer, b, 0, 0)),
        ],
        out_specs=pl.BlockSpec((tq, MLA_HEADS * MLA_V_DIM), lambda b, i: (b * qpb + i, 0)),
        out_shape=jax.ShapeDtypeStruct((t, MLA_HEADS * MLA_V_DIM), BF16),
        scratch_shapes=[pltpu.VMEM((2, MLA_HEADS, max(tq, past), tq), F32),
                        pltpu.VMEM((MLA_HEADS, max(tq, past), tq), BF16)],
        compiler_params=_params("parallel", "parallel"),
        name="mla_lat_attention",
    )(qt, km, vt, kx, vxt)


def _ctx_kv_kernel(ckv_ref, kr_ref, wka_ref, wv_ref, vone_ref, place_ref, k_ref, vt_ref):
    ckv = ckv_ref[...].astype(BF16)
    k_ref[...] = (_dot(ckv, wka_ref[...]) + _dot(kr_ref[...].astype(BF16), place_ref[...])).astype(BF16)
    v = _dot(ckv, wv_ref[...]) + vone_ref[...]
    for j in range(v.shape[1] // LANES):
        sl = slice(j * LANES, (j + 1) * LANES)
        vt_ref[sl, :] = jnp.transpose(v[:, sl]).astype(BF16)


def _ctx_kv(cache_ckv, cache_krope, wk_a, wv_ext, vone_ext):
    bd, depth, past, _ = cache_ckv.shape
    place = np.zeros((MLA_ROPE_DIM, MLA_HEADS * MLA_QK_PAD), np.float32)
    for hd in range(MLA_HEADS):
        for i in range(MLA_ROPE_DIM):
            place[i, hd * MLA_QK_PAD + MLA_NOPE_DIM + i] = 1.0
    kw, vw = MLA_HEADS * MLA_QK_PAD, MLA_HEADS * MLA_V_PAD
    return pl.pallas_call(
        _ctx_kv_kernel,
        grid=(depth, bd),
        in_specs=[
            pl.BlockSpec((None, None, past, MLA_KV_LORA), lambda l, b: (b, l, 0, 0)),
            pl.BlockSpec((None, None, past, MLA_ROPE_DIM), lambda l, b: (b, l, 0, 0)),
            pl.BlockSpec((None, MLA_KV_LORA, kw), lambda l, b: (l, 0, 0)),
            pl.BlockSpec((None, MLA_KV_LORA, vw), lambda l, b: (l, 0, 0)),
            pl.BlockSpec((1, vw), lambda l, b: (0, 0)),
            pl.BlockSpec((MLA_ROPE_DIM, kw), lambda l, b: (0, 0)),
        ],
        out_specs=[pl.BlockSpec((None, None, past, kw), lambda l, b: (l, b, 0, 0)),
                   pl.BlockSpec((None, None, vw, past), lambda l, b: (l, b, 0, 0))],
        out_shape=[jax.ShapeDtypeStruct((depth, bd, past, kw), BF16),
                   jax.ShapeDtypeStruct((depth, bd, vw, past), BF16)],
        compiler_params=_params("parallel", "parallel"),
        name="ctx_kv",
    )(cache_ckv, cache_krope, wk_a, wv_ext, vone_ext, jnp.asarray(place, BF16))


def _na_tile_geometry(rows):
    last = rows // NA_Q_ROWS - 1
    geo = []
    for j in (0, 1, last):
        r0 = j * NA_Q_ROWS
        geo.append((r0, min(max(r0 - NA_KH // 2, 0), rows - NA_WIN_ROWS)))
    return geo


def _na_bias_kernel(geo, rows, rpb_ref, o_ref):
    l = pl.program_id(0)
    hd = pl.program_id(1)
    base = (l * NA_HEADS + hd) * (2 * NA_KH - 1) * (2 * NA_KW - 1)
    qc = lax.broadcasted_iota(jnp.int32, (GRID_W, GRID_W), 0)
    kcol = lax.broadcasted_iota(jnp.int32, (GRID_W, GRID_W), 1)
    d_col = jnp.clip(kcol - qc + (NA_KW - 1), 0, 2 * NA_KW - 2)
    col_start = jnp.clip(qc - NA_KW // 2, 0, GRID_W - NA_KW)
    in_cols = (kcol >= col_start) & (kcol < col_start + NA_KW)
    neg = jnp.full((GRID_W, GRID_W), NEG_INF, F32)
    tabs = []
    for dr in range(2 * NA_KH - 1):
        acc = jnp.zeros((GRID_W, GRID_W), F32)
        for dc in range(2 * NA_KW - 1):
            acc = jnp.where(d_col == dc, rpb_ref[base + dr * (2 * NA_KW - 1) + dc], acc)
        tabs.append(jnp.where(in_cols, acc, neg))
    for kind, (r0, ws) in enumerate(geo):
        for i in range(NA_Q_ROWS):
            r = r0 + i
            lo = min(max(r - NA_KH // 2, 0), rows - NA_KH)
            for j in range(NA_WIN_ROWS):
                kr = ws + j
                blk = tabs[kr - r + NA_KH - 1] if lo <= kr < lo + NA_KH else neg
                o_ref[kind, i * GRID_W:(i + 1) * GRID_W, j * GRID_W:(j + 1) * GRID_W] = blk


def _na_bias(na_rpb, rows):
    depth = na_rpb.shape[0]
    geo = _na_tile_geometry(rows)
    qn, kn = NA_Q_ROWS * GRID_W, NA_WIN_ROWS * GRID_W
    return pl.pallas_call(
        functools.partial(_na_bias_kernel, geo, rows),
        grid=(depth, NA_HEADS),
        in_specs=[pl.BlockSpec(memory_space=pltpu.SMEM)],
        out_specs=pl.BlockSpec((None, None, 3, qn, kn), lambda l, h: (l, h, 0, 0, 0)),
        out_shape=jax.ShapeDtypeStruct((depth, NA_HEADS, 3, qn, kn), F32),
        compiler_params=_params("parallel", "parallel"),
        name="na_bias",
    )(na_rpb.reshape(-1))


def _na_lat_kernel(rows, q_ref, k_ref, v_ref, kx_ref, vx_ref, bias_ref, o_ref, s_scr, p_scr):
    j = pl.program_id(1)
    ws = jnp.clip(j * NA_Q_ROWS - NA_KH // 2, 0, rows - NA_WIN_ROWS)
    start = pl.multiple_of(ws * GRID_W, GRID_W)
    nk = NA_WIN_ROWS * GRID_W
    heads = [slice(hd * NA_HEAD_DIM, (hd + 1) * NA_HEAD_DIM) for hd in range(NA_HEADS)]
    m = []
    for hd, sl in enumerate(heads):
        q = q_ref[:, sl]
        s_win = _nt_dot(q, k_ref[pl.ds(start, nk), sl]) + bias_ref[hd]
        s_ctx = _nt_dot(q, kx_ref[hd].astype(BF16))
        s_scr[hd, :, :nk] = s_win
        s_scr[hd, :, nk:] = s_ctx
        m.append(jnp.maximum(jnp.max(s_win, axis=-1, keepdims=True), jnp.max(s_ctx, axis=-1, keepdims=True)))
    l = []
    for hd in range(NA_HEADS):
        p = jnp.exp(s_scr[hd] - m[hd])
        l.append(jnp.sum(p, axis=-1, keepdims=True))
        p_scr[hd] = p.astype(BF16)
    for hd, sl in enumerate(heads):
        o = _dot(p_scr[hd, :, :nk], v_ref[pl.ds(start, nk), sl]) + _dot(p_scr[hd, :, nk:], vx_ref[hd].astype(BF16))
        o_ref[:, sl] = (o / l[hd]).astype(BF16)


def _na_lat_attention(qn, kn, vn, cache_k, cache_v, bias, layer, n):
    t = qn.shape[0]
    rows = n // GRID_W
    assert rows % NA_Q_ROWS == 0 and rows >= NA_WIN_ROWS + NA_Q_ROWS
    tiles = rows // NA_Q_ROWS
    tq = NA_Q_ROWS * GRID_W
    past = cache_k.shape[3]

    def kind(b, j):
        return (layer, 0, jnp.where(j == 0, 0, jnp.where(j == tiles - 1, 2, 1)), 0, 0)

    return pl.pallas_call(
        functools.partial(_na_lat_kernel, rows),
        grid=(t // n, tiles),
        in_specs=[
            pl.BlockSpec((tq, NA_WIDTH), lambda b, j: (b * tiles + j, 0)),
            pl.BlockSpec((n, NA_WIDTH), lambda b, j: (b, 0)),
            pl.BlockSpec((n, NA_WIDTH), lambda b, j: (b, 0)),
            pl.BlockSpec((None, None, NA_HEADS, past, NA_HEAD_DIM), lambda b, j: (b, layer, 0, 0, 0)),
            pl.BlockSpec((None, None, NA_HEADS, past, NA_HEAD_DIM), lambda b, j: (b, layer, 0, 0, 0)),
            pl.BlockSpec((None, NA_HEADS, None, tq, NA_WIN_ROWS * GRID_W), kind),
        ],
        out_specs=pl.BlockSpec((tq, NA_WIDTH), lambda b, j: (b * tiles + j, 0)),
        out_shape=jax.ShapeDtypeStruct((t, NA_WIDTH), BF16),
        scratch_shapes=[pltpu.VMEM((NA_HEADS, tq, NA_WIN_ROWS * GRID_W + past), F32),
                        pltpu.VMEM((NA_HEADS, tq, NA_WIN_ROWS * GRID_W + past), BF16)],
        compiler_params=_params("parallel", "parallel"),
        name="na_lat_attention",
    )(qn, kn, vn, cache_k, cache_v, bias)


def _dft_tables(n):
    def thin(j, k, period):
        ang = (2.0 * math.pi / period) * ((j[:, None] * k[None, :]) % period).astype(F32)
        return jnp.cos(ang), jnp.sin(ang)

    k = jnp.arange(n, dtype=jnp.int32)
    scale = float(n) ** -0.5
    if n % 64 == 0 and n > 64:
        n1 = n // 64
        c1, s1 = thin(jnp.arange(n1, dtype=jnp.int32), k, n1)
        c2, s2 = thin(jnp.arange(64, dtype=jnp.int32), k, n)
        c1, s1, c2, s2 = c1[:, None, :], s1[:, None, :], c2[None, :, :], s2[None, :, :]
        cm = (c1 * c2 - s1 * s2).reshape(n, n)
        sm = (s1 * c2 + c1 * s2).reshape(n, n)
    else:
        cm, sm = thin(k, k, n)
    return (cm * scale).astype(BF16), (sm * -scale).astype(BF16)


def _fourier_kernel(c_ref, s_ref, ab_ref, o_ref):
    o_ref[...] = (_dot(c_ref[...], ab_ref[:, :FN_WIDTH]) + _dot(s_ref[...], ab_ref[:, FN_WIDTH:])).astype(BF16)


def _fourier_half_kernel(c_ref, s_ref, cmid_ref, ab_ref, plus_ref, minus_ref, mid_ref):
    a = ab_ref[:, :FN_WIDTH]
    p = _dot(c_ref[...], a)
    q = _dot(s_ref[...], ab_ref[:, FN_WIDTH:])
    plus_ref[...] = (p + q).astype(BF16)
    minus_ref[...] = (p - q).astype(BF16)
    mid_ref[...] = _dot(cmid_ref[...], a).astype(BF16)


def _fourier_half(fab, tables, n, tmf=512):
    t = fab.shape[0]
    b = t // n
    half = n // 2
    tiles = half // tmf
    cm, sm = tables
    cmid = jnp.broadcast_to(cm[half:half + 1], (8, n))
    row_out = lambda: pl.BlockSpec((tmf, FN_WIDTH), lambda i, bb: (bb * tiles + i, 0))
    plus, minus, mid = pl.pallas_call(
        _fourier_half_kernel,
        grid=(tiles, b),
        in_specs=[
            pl.BlockSpec((tmf, n), lambda i, bb: (i, 0)),
            pl.BlockSpec((tmf, n), lambda i, bb: (i, 0)),
            pl.BlockSpec((8, n), lambda i, bb: (0, 0)),
            pl.BlockSpec((n, 2 * FN_WIDTH), lambda i, bb: (bb, 0)),
        ],
        out_specs=[row_out(), row_out(), pl.BlockSpec((None, None, 8, FN_WIDTH), lambda i, bb: (i, bb, 0, 0))],
        out_shape=[jax.ShapeDtypeStruct((b * half, FN_WIDTH), BF16), jax.ShapeDtypeStruct((b * half, FN_WIDTH), BF16),
                   jax.ShapeDtypeStruct((tiles, b, 8, FN_WIDTH), BF16)],
        compiler_params=_params("parallel", "parallel"),
        name="fourier_half",
    )(cm, sm, cmid, fab)
    return plus, minus, mid


def _fourier(fab, tables, n, tmf=512):
    if n >= 4 * tmf and tmf == TM_LAT_MIXOUT:
        return _fourier_half(fab, tables, n, tmf)
    t = fab.shape[0]
    tmf = min(tmf, n)
    tiles = n // tmf
    cm, sm = tables
    return pl.pallas_call(
        _fourier_kernel,
        grid=(tiles, t // n),
        in_specs=[
            pl.BlockSpec((tmf, n), lambda i, b: (i, 0)),
            pl.BlockSpec((tmf, n), lambda i, b: (i, 0)),
            pl.BlockSpec((n, 2 * FN_WIDTH), lambda i, b: (b, 0)),
        ],
        out_specs=pl.BlockSpec((tmf, FN_WIDTH), lambda i, b: (b * tiles + i, 0)),
        out_shape=jax.ShapeDtypeStruct((t, FN_WIDTH), BF16),
        compiler_params=_params("parallel", "parallel"),
        name="fourier",
    )(cm, sm, fab)


def _route(s_t, sb_t):
    def top2_sum(v):
        hi1, lo1 = jnp.maximum(v[0], v[1]), jnp.minimum(v[0], v[1])
        hi2, lo2 = jnp.maximum(v[2], v[3]), jnp.minimum(v[2], v[3])
        return jnp.maximum(hi1, hi2) + jnp.maximum(jnp.minimum(hi1, hi2), jnp.maximum(lo1, lo2))

    best = top2_sum(sb_t[0:EXPERTS_PER_GROUP])
    gsel = jnp.zeros_like(best, dtype=jnp.int32)
    for g in range(1, N_EXPERT_GROUPS):
        cand = top2_sum(sb_t[g * EXPERTS_PER_GROUP:(g + 1) * EXPERTS_PER_GROUP])
        better = cand > best
        gsel = jnp.where(better, g, gsel)
        best = jnp.where(better, cand, best)
    chosen = []
    for e in range(N_EXPERTS):
        g = e // EXPERTS_PER_GROUP
        beaten = jnp.zeros_like(gsel)
        for o in range(g * EXPERTS_PER_GROUP, (g + 1) * EXPERTS_PER_GROUP):
            if o == e:
                continue
            ahead = (sb_t[o] > sb_t[e]) | ((sb_t[o] == sb_t[e]) & (o < e))
            beaten = beaten + ahead.astype(jnp.int32)
        chosen.append((gsel == g) & (beaten < 2))
    picked = [jnp.where(chosen[e], s_t[e], 0.0) for e in range(N_EXPERTS)]
    denom = picked[0]
    for e in range(1, N_EXPERTS):
        denom = denom + picked[e]
    return chosen, [pk / denom for pk in picked]


def _pack_pairs(x):
    w = x.shape[1] // 2
    hi = pltpu.bitcast(x[:, :w].astype(BF16).astype(F32), jnp.uint32)
    lo = pltpu.bitcast(x[:, w:].astype(BF16).astype(F32), jnp.uint32)
    return hi | (lo >> 16)


def _half_spectrum_tile(i, tiles_per_seq, plus_ref, minus_ref, edge_ref, mid_ref, rev_ref):
    j = i % tiles_per_seq - tiles_per_seq // 2
    body = _dot(rev_ref[...], minus_ref[...]).astype(BF16)
    first = jnp.where(j == 0, mid_ref[0:1, :], edge_ref[0:1, :])
    rows = lax.broadcasted_iota(jnp.int32, body.shape, 0)
    upper = jnp.where(rows == 0, first, body)
    return jnp.where(j < 0, plus_ref[...], upper)


def _mixout_kernel(n, half, *refs):
    x_ref, mod_ref, ab_ref, z_ref, zp_ref, zn_ref, yna_ref, ymla_ref = refs[:8]
    n_g = 5 if half else 1
    g_refs = refs[8:8 + n_g]
    (cw_ref, wfn_ref, wout_ref, g2_ref, wrc_ref, br_ref, x1_ref, h2_ref, route_ref, gtok_ref,
     cnt_ref) = refs[8 + n_g:]
    tm = x_ref.shape[0]
    i = pl.program_id(0)
    g_tile = _half_spectrum_tile(i, n // tm, *g_refs) if half else g_refs[0][...]
    mod = mod_ref[...]
    gate1, shift2, scale2 = mod[2:3], mod[3:4], mod[4:5]

    z = z_ref[...].astype(F32)
    ridx = lax.broadcasted_iota(jnp.int32, z.shape, 0)
    at_start = (i * tm) % n == 0
    at_end = ((i + 1) * tm) % n == 0
    prev_row = jnp.where(at_start, 0.0, zp_ref[7:8, :].astype(F32))
    next_row = jnp.where(at_end, 0.0, zn_ref[0:1, :].astype(F32))
    z_m1 = jnp.where(ridx == 0, prev_row, pltpu.roll(z, 1, axis=0))
    z_p1 = jnp.where(ridx == tm - 1, next_row, pltpu.roll(z, tm - 1, axis=0))
    cw = cw_ref[...]
    y_conv = ab_ref[...].astype(F32) * (z_m1 * cw[0:1] + z * cw[1:2] + z_p1 * cw[2:3])

    y_fn = _dot(g_tile, wfn_ref[...])
    cat = jnp.concatenate([y_conv.astype(BF16), yna_ref[...], ymla_ref[...], y_fn.astype(BF16)], axis=-1)
    x1 = x_ref[...] + gate1 * _dot(cat, wout_ref[...])
    x1_ref[...] = x1

    h2 = _rms(x1, g2_ref[...]) * (1.0 + scale2) + shift2
    packed = _pack_pairs(h2)
    piece = packed.shape[1] // MOE_PIECES
    for p in range(MOE_PIECES):
        h2_ref[p] = packed[:, p * piece:(p + 1) * piece]
    h2_hi = h2.astype(BF16)
    h2_lo = (h2 - h2_hi.astype(F32)).astype(BF16)
    both = _dot(h2_hi, wrc_ref[...])
    logits = both[:, :LANES] + (both[:, LANES:] + _dot(h2_lo, wrc_ref[:, :LANES]))
    s = jax.nn.sigmoid(logits)
    s_t = jnp.transpose(s)
    sb_t = jnp.transpose(s + br_ref[...])
    chosen, gates = _route([s_t[e:e + 1] for e in range(N_EXPERTS)], [sb_t[e:e + 1] for e in range(N_EXPERTS)])

    @pl.when(i == 0)
    def _():
        cnt_ref[...] = jnp.zeros(cnt_ref.shape, F32)

    chosen_f = jnp.concatenate([ch.astype(F32) for ch in chosen], axis=0)
    before = lax.broadcasted_iota(jnp.int32, (tm, tm), 0) < lax.broadcasted_iota(jnp.int32, (tm, tm), 1)
    prefix = _dot(chosen_f.astype(BF16), jnp.where(before, 1.0, 0.0).astype(BF16))
    base = cnt_ref[...]
    rank = jnp.concatenate([base] * (tm // LANES), axis=1) + prefix
    cnt_ref[...] = base + jnp.sum(chosen_f, axis=1, keepdims=True)

    zero = jnp.zeros((1, tm), F32)
    seen = zero
    slots = [[zero, zero, zero], [zero, zero, zero]]
    for e in range(N_EXPERTS):
        for k in range(2):
            hit = chosen[e] & (seen == float(k))
            for j, val in enumerate((float(e), gates[e], rank[e:e + 1])):
                slots[k][j] = jnp.where(hit, val, slots[k][j])
        seen = seen + chosen_f[e:e + 1]
    (e_lo, g_lo, r_lo), (e_hi, g_hi, r_hi) = slots
    route_ref[...] = jnp.concatenate([g_lo, g_hi, e_lo, e_hi, r_lo, r_hi, zero, zero], axis=0)
    gates_t = jnp.concatenate([g_lo, g_hi, jnp.zeros((LANES - 2, tm), F32)], axis=0)
    gtok_ref[...] = jnp.transpose(gates_t)


def _mixout(x, mod, layer, n, parts, wts, lat, tm):
    t, d = x.shape
    ab, z, yna, ymla, g = parts
    nblk8 = t // 8
    per8 = tm // 8
    cond_row_of_tile = _cond_row(lat, tm, n, mod.shape[1] - 1)
    const2 = lambda i: (0, 0)
    lsel = lambda i: (layer, 0, 0)
    row = lambda w: pl.BlockSpec((tm, w), lambda i: (i, 0))
    in_specs = [
        row(d),
        pl.BlockSpec((None, None, 6, d), lambda i: (layer, cond_row_of_tile(i), 0, 0)),
        row(CONV_WIDTH),
        row(CONV_WIDTH),
        pl.BlockSpec((8, CONV_WIDTH), lambda i: (jnp.maximum(i * per8 - 1, 0), 0)),
        pl.BlockSpec((8, CONV_WIDTH), lambda i: (jnp.minimum((i + 1) * per8, nblk8 - 1), 0)),
        row(NA_WIDTH),
        row(MLA_HEADS * MLA_V_DIM),
    ]
    half = isinstance(g, tuple)
    if half:
        plus, minus, mid = g
        tps = n // tm
        hps = tps // 2
        assert plus.shape[0] * 2 == t and tps % 2 == 0
        rev = np.zeros((tm, tm), np.float32)
        rev[np.arange(1, tm), tm - np.arange(1, tm)] = 1.0
        src = lambda i: (i // tps) * hps + jnp.clip(tps - 1 - i % tps, 0, hps - 1)
        in_specs += [
            pl.BlockSpec((tm, FN_WIDTH), lambda i: ((i // tps) * hps + jnp.minimum(i % tps, hps - 1), 0)),
            pl.BlockSpec((tm, FN_WIDTH), lambda i: (src(i), 0)),
            pl.BlockSpec((8, FN_WIDTH), lambda i: (jnp.minimum(src(i) + 1, plus.shape[0] // tm - 1) * per8, 0)),
            pl.BlockSpec((None, None, 8, FN_WIDTH), lambda i: (0, i // tps, 0, 0)),
            pl.BlockSpec((tm, tm), const2),
        ]
        g_args = [plus, minus, minus, mid, jnp.asarray(rev, BF16)]
    else:
        in_specs += [row(FN_WIDTH)]
        g_args = [g]
    in_specs += [
        pl.BlockSpec((None, 3, CONV_WIDTH), lsel),
        pl.BlockSpec((None, FN_WIDTH, FN_WIDTH), lsel),
        pl.BlockSpec((None, d, d), lsel),
        pl.BlockSpec((None, 1, d), lsel),
        pl.BlockSpec((d, 2 * LANES), const2),
        pl.BlockSpec((1, LANES), const2),
    ]
    out_specs = [
        row(d),
        pl.BlockSpec((MOE_PIECES, tm, d // 2 // MOE_PIECES), lambda i: (0, i, 0)),
        pl.BlockSpec((8, tm), lambda i: (0, i)),
        row(LANES),
        pl.BlockSpec((N_EXPERTS, LANES), const2),
    ]
    out_shape = [
        jax.ShapeDtypeStruct((t, d), F32),
        jax.ShapeDtypeStruct((MOE_PIECES, t, d // 2 // MOE_PIECES), jnp.uint32),
        jax.ShapeDtypeStruct((8, t), F32),
        jax.ShapeDtypeStruct((t, LANES), F32),
        jax.ShapeDtypeStruct((N_EXPERTS, LANES), F32),
    ]
    return pl.pallas_call(
        functools.partial(_mixout_kernel, n, half),
        grid=(t // tm,),
        in_specs=in_specs,
        out_specs=out_specs,
        out_shape=out_shape,
        compiler_params=_params("arbitrary"),
        name="mixout",
    )(x, mod, ab, z, z, z, yna, ymla, *g_args, wts["conv_w"], wts["w_fn"], wts["w_out"], wts["norm2"],
      wts["wr_cat"], wts["b_router"])


def _slot_positions(route, counts, rb):
    cnt = counts[:, 0].astype(jnp.int32)
    padded = (cnt + rb - 1) // rb * rb
    ends = jnp.cumsum(padded)
    offs = ends - padded
    experts = route[2:4].astype(jnp.int32)
    ranks = route[4:6].astype(jnp.int32)
    pos = ranks
    for e in range(N_EXPERTS):
        pos = pos + jnp.where(experts == e, offs[e], 0)
    nblk = (2 * route.shape[1]) // rb + N_EXPERTS
    starts = jnp.arange(nblk, dtype=jnp.int32) * rb
    blk_expert = jnp.sum((starts[:, None] >= ends[None, :]).astype(jnp.int32), axis=1)
    used = blk_expert < N_EXPERTS
    blk_expert = jnp.where(used, blk_expert, 0)
    valid_end = jnp.sum(jnp.where(blk_expert[:, None] == jnp.arange(N_EXPERTS)[None, :], (offs + cnt)[None, :], 0), axis=1)
    blk_valid = jnp.where(used, jnp.clip(valid_end - starts, 0, rb), 0)
    return pos, jnp.stack([blk_expert, blk_valid])


def _sc_mesh():
    return plsc.VectorSubcoreMesh(core_axis_name="c", subcore_axis_name="s")


def _sc_pipeline(body, nwin, in_specs, out_specs):
    return pltpu.emit_pipeline(body, grid=(nwin,), in_specs=in_specs, out_specs=out_specs,
                               core_axis_name=("c", "s"), dimension_semantics=(pltpu.PARALLEL,))


def _row_scatter(table, idx_a, idx_b, nrows):
    b, w = table.shape
    win = SC_WINDOW
    idx_spec = pl.BlockSpec((1, win), lambda i: (0, i))

    @functools.partial(pl.kernel, out_type=jax.ShapeDtypeStruct((nrows, w), table.dtype), mesh=_sc_mesh(),
                       scratch_types=[])
    def scatter(table_hbm, ia_hbm, ib_hbm, out_hbm):
        def body(rows_vmem, ia_vmem, ib_vmem):
            pltpu.sync_copy(rows_vmem, out_hbm.at[ia_vmem.at[0]])
            pltpu.sync_copy(rows_vmem, out_hbm.at[ib_vmem.at[0]])

        _sc_pipeline(body, b // win, [pl.BlockSpec((win, w), lambda i: (i, 0)), idx_spec, idx_spec], [])(
            table_hbm, ia_hbm, ib_hbm)

    return scatter(table, idx_a.reshape(1, b), idx_b.reshape(1, b))


def _row_gather(table, idx):
    b = idx.shape[0]
    w = table.shape[1]
    win = SC_WINDOW

    @functools.partial(pl.kernel, out_type=jax.ShapeDtypeStruct((b, w), table.dtype), mesh=_sc_mesh(),
                       scratch_types=[])
    def gather(table_hbm, idx_hbm, out_hbm):
        def body(idx_vmem, out_vmem):
            pltpu.sync_copy(table_hbm.at[idx_vmem.at[0]], out_vmem)

        _sc_pipeline(body, b // win, [pl.BlockSpec((1, win), lambda i: (0, i))],
                     [pl.BlockSpec((win, w), lambda i: (i, 0))])(idx_hbm, out_hbm)

    return gather(table, idx.reshape(1, b))


def _ffn_kernel(blk_ref, xs_ref, w13_ref, w2_ref, y_ref):
    i = pl.program_id(0)
    e = blk_ref[0, i]
    nvalid = blk_ref[1, i]

    @pl.when(nvalid > 0)
    def _():
        packed = jnp.concatenate([xs_ref[0], xs_ref[1]], axis=-1)
        live = lax.broadcasted_iota(jnp.int32, packed.shape, 0) < nvalid
        xb = _unpack_pairs(jnp.where(live, packed, jnp.uint32(0))).astype(BF16)
        up = _dot(xb, w13_ref[e])
        a, b = up[:, :EXPERT_FF], up[:, EXPERT_FF:]
        hid = (a * jax.nn.sigmoid(a)) * b
        y = _pack_pairs(_dot(hid.astype(BF16), w2_ref[e]))
        half = y.shape[1] // 2
        y_ref[0] = y[:, :half]
        y_ref[1] = y[:, half:]

    @pl.when(nvalid == 0)
    def _():
        y_ref[...] = jnp.zeros(y_ref.shape, y_ref.dtype)


def _expert_ffn(xs, blk, w13, w2, layer, rb):
    pieces, nrows, w = xs.shape
    d = 2 * pieces * w
    resident = dict(pipeline_mode=pl.Buffered(1))
    used = lambda i, blk: (0, jnp.where(blk[1, i] > 0, i, 0), 0)
    return pl.pallas_call(
        _ffn_kernel,
        grid_spec=pltpu.PrefetchScalarGridSpec(
            num_scalar_prefetch=1,
            grid=(nrows // rb,),
            in_specs=[
                pl.BlockSpec((pieces, rb, w), used),
                pl.BlockSpec((None, N_EXPERTS, d, 2 * EXPERT_FF), lambda i, blk: (layer, 0, 0, 0), **resident),
                pl.BlockSpec((None, N_EXPERTS, EXPERT_FF, d), lambda i, blk: (layer, 0, 0, 0), **resident),
            ],
            out_specs=pl.BlockSpec((pieces, rb, w), lambda i, blk: (0, i, 0)),
        ),
        out_shape=jax.ShapeDtypeStruct(xs.shape, xs.dtype),
        compiler_params=_params("parallel"),
        name="expert_ffn",
    )(blk, xs, w13, w2)


def _combine_kernel(final, x1_ref, gtok_ref, mod_ref, nf_ref, y_ref, o_ref):
    out = _moe_residual(x1_ref[...], gtok_ref, y_ref, mod_ref)
    if final:
        out = _rms(out, nf_ref[...])
    o_ref[...] = out


def _combine(x1, gtok, y_tok, mod, layer, n, norm_f, final, lat, tc=512):
    t, d = x1.shape
    cond_row_of_tile = _cond_row(lat, tc, n, mod.shape[1] - 1)
    row = lambda w: pl.BlockSpec((tc, w), lambda i: (i, 0))
    return pl.pallas_call(
        functools.partial(_combine_kernel, final),
        grid=(t // tc,),
        in_specs=[
            row(d),
            row(LANES),
            pl.BlockSpec((None, None, 6, d), lambda i: (layer, cond_row_of_tile(i), 0, 0)),
            pl.BlockSpec((1, d), lambda i: (0, 0)),
            pl.BlockSpec(y_tok.shape[:2] + (tc, y_tok.shape[3]), lambda i: (0, 0, i, 0)),
        ],
        out_specs=row(d),
        out_shape=jax.ShapeDtypeStruct((t, d), F32),
        compiler_params=_params("parallel"),
        name="combine",
    )(x1, gtok, mod, norm_f, y_tok)


def _moe_layer(x, mod, layer, n, parts, wts, lat, tm, rb):
    t = x.shape[0]
    x1, h2, route, gtok, counts = _mixout(x, mod, layer, n, parts, wts, lat, tm)
    pos, blk = _slot_positions(route, counts, rb)
    pieces, _, w = h2.shape
    nrows = blk.shape[1] * rb
    piece_base = (jnp.arange(pieces, dtype=jnp.int32) * nrows)[:, None]
    idx = [(piece_base + pos[s][None, :]).reshape(-1) for s in range(2)]
    xs = _row_scatter(h2.reshape(pieces * t, w), idx[0], idx[1], pieces * nrows).reshape(pieces, nrows, w)
    y = _expert_ffn(xs, blk, wts["w13"], wts["w2"], layer, rb)
    back = (piece_base[:, :, None] + pos[None, :, :]).reshape(-1)
    y_tok = _row_gather(y.reshape(pieces * nrows, w), back).reshape(pieces, 2, t, w)
    return x1, (gtok, y_tok)


def _swap_halves(w):
    nf = MLA_ROPE_DIM // 4
    idx = np.arange(MLA_ROPE_DIM).reshape(2, 2, nf)[:, ::-1, :].reshape(-1)
    return w[..., idx]


def _pack_weights(w_in, mla_wq_up, mla_wkv_up, w1, w3, w2, w_router, b_router):
    depth, d, _ = w_in.shape
    zeros = lambda w: jnp.zeros((depth, d, w), w_in.dtype)
    w_kr = w_in[..., 1920:1952]
    pad_rope = lambda w: jnp.concatenate([zeros(MLA_NOPE_DIM), w, zeros(MLA_QK_PAD - MLA_NOPE_DIM - MLA_ROPE_DIM)], -1)
    w_main = jnp.concatenate([w_in[..., :1920], pad_rope(w_kr), pad_rope(_swap_halves(w_kr)), w_in[..., 1952:]], -1)

    wq = mla_wq_up.reshape(depth, MLA_Q_LORA, MLA_HEADS, MLA_NOPE_DIM + MLA_ROPE_DIM)
    q_nope, q_rope = wq[..., :MLA_NOPE_DIM], wq[..., MLA_NOPE_DIM:]
    tail = jnp.zeros(q_rope.shape[:-1] + (MLA_QK_PAD - MLA_NOPE_DIM - MLA_ROPE_DIM,), wq.dtype)
    wq_a = jnp.concatenate([q_nope, q_rope, tail], -1).reshape(depth, MLA_Q_LORA, -1)
    wq_b = jnp.concatenate([jnp.zeros_like(q_nope), _swap_halves(q_rope), tail], -1).reshape(depth, MLA_Q_LORA, -1)

    wkv = mla_wkv_up.reshape(depth, MLA_KV_LORA, MLA_HEADS, MLA_NOPE_DIM + MLA_V_DIM)
    k_nope, v_up = wkv[..., :MLA_NOPE_DIM], wkv[..., MLA_NOPE_DIM:]
    k_tail = jnp.zeros(k_nope.shape[:-1] + (MLA_QK_PAD - MLA_NOPE_DIM,), wkv.dtype)
    wk_a = jnp.concatenate([k_nope, k_tail], -1).reshape(depth, MLA_KV_LORA, -1)
    wv = v_up.reshape(depth, MLA_KV_LORA, -1)
    v_tail = jnp.zeros(v_up.shape[:-1] + (MLA_V_PAD - MLA_V_DIM,), wkv.dtype)
    wv_ext = jnp.concatenate([v_up, v_tail], -1).reshape(depth, MLA_KV_LORA, -1)
    vone = np.zeros((1, MLA_HEADS * MLA_V_PAD), np.float32)
    vone[0, MLA_V_DIM::MLA_V_PAD] = 1.0

    wr = jnp.pad(w_router, ((0, 0), (0, LANES - N_EXPERTS)))
    wr_hi = wr.astype(BF16)
    wr_lo = (wr - wr_hi.astype(F32)).astype(BF16)
    return {
        "w_in": w_main.astype(BF16), "wq_a": wq_a.astype(BF16), "wq_b": wq_b.astype(BF16),
        "wk_a": wk_a.astype(BF16), "wv": wv.astype(BF16), "wv_ext": wv_ext.astype(BF16),
        "vone_ext": jnp.asarray(vone),
        "w13": jnp.concatenate([w1, w3], -1).astype(BF16), "w2": w2.astype(BF16),
        "wr_cat": jnp.concatenate([wr_hi, wr_lo], axis=-1),
        "b_router": jnp.pad(b_router, (0, LANES - N_EXPERTS)).reshape(1, LANES).astype(F32),
    }


def _channel_dft():
    c = np.arange(FN_GROUP_DIM)
    ang = 2.0 * np.pi * ((c[:, None] * c[None, :]) % FN_GROUP_DIM) / FN_GROUP_DIM
    out = np.zeros((FN_WIDTH, 2 * FN_WIDTH), np.float32)
    for g in range(FN_GROUPS):
        sl = slice(g * FN_GROUP_DIM, (g + 1) * FN_GROUP_DIM)
        out[sl, sl] = np.cos(ang) * FN_GROUP_DIM ** -0.5
        out[sl, FN_WIDTH + g * FN_GROUP_DIM:FN_WIDTH + (g + 1) * FN_GROUP_DIM] = np.sin(ang) * FN_GROUP_DIM ** -0.5
    return jnp.asarray(out, BF16)


def _rope_tables(n):
    tok = jnp.arange(n)
    pos = jnp.stack([tok // GRID_W, tok % GRID_W], axis=-1).astype(F32)
    nf = MLA_ROPE_DIM // 4
    freqs = ROPE_THETA ** (-jnp.arange(nf, dtype=F32) / nf)
    ang = pos[:, :, None] * freqs
    cos = jnp.broadcast_to(jnp.cos(ang)[:, :, None, :], (n, 2, 2, nf)).reshape(n, MLA_ROPE_DIM)
    sin = jnp.sin(ang)
    sin = jnp.stack([-sin, sin], axis=2).reshape(n, MLA_ROPE_DIM)
    pad = jnp.zeros((n, MLA_QK_PAD - MLA_NOPE_DIM - MLA_ROPE_DIM), F32)
    cos_t = jnp.concatenate([jnp.ones((n, MLA_NOPE_DIM), F32), cos, pad], -1)
    sin_t = jnp.concatenate([jnp.zeros((n, MLA_NOPE_DIM), F32), sin, pad], -1)
    return cos_t, sin_t


def kernel(x_prompt, x_sample, cache_na_k, cache_na_v, cache_mla_ckv, cache_mla_krope, c, c_ctx, w_ada, b_ada,
           norm1, norm2, w_in, conv_w, na_rpb, mla_gq, mla_wq_up, mla_gkv, mla_wkv_up, w_fn, w_out, w_router,
           b_router, w1, w3, w2, norm_f):
    bp, seq, d = x_prompt.shape
    bd, dec_seq, _ = x_sample.shape
    depth = w_in.shape[0]

    wts = _pack_weights(w_in, mla_wq_up, mla_wkv_up, w1, w3, w2, w_router, b_router)
    wts.update({
        "norm1": norm1.reshape(depth, 1, d), "norm2": norm2.reshape(depth, 1, d),
        "mla_gq": mla_gq.reshape(depth, 1, -1), "mla_gkv": mla_gkv.reshape(depth, 1, -1),
        "conv_w": conv_w, "w_fn": w_fn.astype(BF16), "w_out": w_out.astype(BF16),
        "norm_f": norm_f.reshape(1, d), "cs_bd": _channel_dft(),
    })

    cond = jnp.concatenate([c, jnp.zeros((-(bd + 1) % 8, d), c.dtype), c_ctx[None, :]], axis=0)
    mod = _ada_modulation(cond, w_ada, b_ada)

    xp = x_prompt.reshape(bp * seq, d)
    tables = _dft_tables(seq)
    caches = [
        jnp.zeros((bp, depth, NA_HEADS, seq, NA_HEAD_DIM), F32), jnp.zeros((bp, depth, NA_HEADS, seq, NA_HEAD_DIM), F32),
        jnp.zeros((bp, depth, seq, MLA_KV_LORA), F32), jnp.zeros((bp, depth, seq, MLA_ROPE_DIM), F32)]
    pending = None
    for layer in range(depth):
        outs = _premix(xp, mod, layer, seq, wts, None, False, seq, caches, pending)
        ab, z, qn, kn, vn, km, fab, qm, vm = outs[:9]
        caches = outs[9:13]
        if pending is not None:
            xp = outs[13]
        yna, ymla = _ctx_attention(qn, kn, vn, qm, km, vm, seq)
        g = _fourier(fab, tables, seq)
        xp, pending = _moe_layer(xp, mod, layer, seq, (ab, z, yna, ymla, g), wts, False, seq, MOE_ROW_BLOCK_CTX)
    xp = _combine(xp, pending[0], pending[1], mod, depth - 1, seq, wts["norm_f"], True, False, seq)
    new_na_k, new_na_v, new_ckv, new_krope = caches

    xs = x_sample.reshape(bd * dec_seq, d)
    kx, vxt = _ctx_kv(cache_mla_ckv, cache_mla_krope, wts["wk_a"], wts["wv_ext"], wts["vone_ext"])
    na_bias = _na_bias(na_rpb, dec_seq // GRID_W)
    rope = _rope_tables(dec_seq)
    tables = _dft_tables(dec_seq)
    pending = None
    for layer in range(depth):
        outs = _premix(xs, mod, layer, dec_seq, wts, rope, True, TM_LAT_PREMIX, None, pending)
        ab, z, qn, kn, vn, km, fab, qt, vt = outs[:9]
        if pending is not None:
            xs = outs[9]
        yna = _na_lat_attention(qn, kn, vn, cache_na_k, cache_na_v, na_bias, layer, dec_seq)
        ymla = _mla_lat_attention(qt, km, vt, kx, vxt, layer, dec_seq)
        g = _fourier(fab, tables, dec_seq)
        xs, pending = _moe_layer(xs, mod, layer, dec_seq, (ab, z, yna, ymla, g), wts, True, TM_LAT_MIXOUT,
                                 MOE_ROW_BLOCK)
    xs = _combine(xs, pending[0], pending[1], mod, depth - 1, dec_seq, wts["norm_f"], True, True, TM_LAT_PREMIX)

    return (xp.reshape(bp, seq, d), xs.reshape(bd, dec_seq, d), new_na_k, new_na_v, new_ckv, new_krope)
```

```python
import functools
import math

import numpy as np
import jax
import jax.numpy as jnp
from jax import lax
from jax.experimental import pallas as pl
from jax.experimental.pallas import tpu as pltpu
from jax.experimental.pallas import tpu_sc as plsc

F32 = jnp.float32
BF16 = jnp.bfloat16

GRID_W = 64
CONV_WIDTH = 256
NA_HEADS = 4
NA_HEAD_DIM = 64
NA_WIDTH = NA_HEADS * NA_HEAD_DIM
NA_KH = 8
NA_KW = 16
MLA_HEADS = 4
MLA_Q_LORA = 256
MLA_KV_LORA = 128
MLA_NOPE_DIM = 64
MLA_ROPE_DIM = 32
MLA_V_DIM = 64
MLA_QK_PAD = 128
MLA_V_PAD = 96
MLA_KEY_SUB = 256
LOG2E = 1.4426950408889634
FN_GROUPS = 4
FN_GROUP_DIM = 64
FN_WIDTH = FN_GROUPS * FN_GROUP_DIM
N_EXPERTS = 16
N_EXPERT_GROUPS = 4
EXPERTS_PER_GROUP = N_EXPERTS // N_EXPERT_GROUPS
EXPERT_FF = 256
ROPE_THETA = 10000.0
EPS = 1e-6
NEG_INF = -1e30
LANES = 128

NA_SCALE = NA_HEAD_DIM ** -0.5
MLA_SCALE = (MLA_NOPE_DIM + MLA_ROPE_DIM) ** -0.5

NA_Q_ROWS = 4
NA_WIN_ROWS = 12

TM_LAT_PREMIX = 512
TM_LAT_MIXOUT = 512
MOE_ROW_BLOCK = 512
SC_WINDOW = 128
MOE_PIECES = 2

V7X_VMEM_BYTES = 64 * 1024 * 1024
VMEM_LIMIT = V7X_VMEM_BYTES - 8 * 1024 * 1024

_C_AB, _C_AC, _C_AU, _C_Q, _C_K, _C_V, _C_CQ = 0, 256, 512, 768, 1024, 1280, 1536
_C_CKV, _C_KR, _C_KRS, _C_FU, _C_END = 1792, 1920, 2048, 2176, 2432


def _nt_dot(a, b):
    return lax.dot_general(a, b, (((1,), (1,)), ((), ())), preferred_element_type=F32)


def _dot(a, b):
    return jnp.dot(a, b, preferred_element_type=F32)


def _rms(x, g):
    return x * lax.rsqrt(jnp.mean(x * x, axis=-1, keepdims=True) + EPS) * g


def _params(*sem, flags=None):
    return pltpu.CompilerParams(dimension_semantics=sem, vmem_limit_bytes=VMEM_LIMIT, flags=flags)


def _ada_kernel(c_ref, w_ref, b_ref, o_ref):
    cnd = c_ref[...]
    act = cnd * jax.nn.sigmoid(cnd)
    o_ref[...] = _dot(act.astype(BF16), w_ref[...].astype(BF16)) + b_ref[...]


def _ada_modulation(cond, w_ada, b_ada):
    depth, d, six_d = w_ada.shape
    r = cond.shape[0]
    tn = 1024
    out = pl.pallas_call(
        _ada_kernel,
        grid=(depth, six_d // tn),
        in_specs=[
            pl.BlockSpec((r, d), lambda l, j: (0, 0)),
            pl.BlockSpec((None, d, tn), lambda l, j: (l, 0, j)),
            pl.BlockSpec((None, 1, tn), lambda l, j: (l, 0, j)),
        ],
        out_specs=pl.BlockSpec((None, r, tn), lambda l, j: (l, 0, j)),
        out_shape=jax.ShapeDtypeStruct((depth, r, six_d), F32),
        compiler_params=_params("parallel", "parallel"),
        name="ada_modulation",
    )(cond, w_ada, b_ada.reshape(depth, 1, six_d))
    return out.reshape(depth, r, 6, d)


def _unpack_pairs(p):
    hi = pltpu.bitcast(p & jnp.uint32(0xFFFF0000), F32)
    lo = pltpu.bitcast(p << 16, F32)
    return jnp.concatenate([hi, lo], axis=-1)


def _moe_residual(x1, gtok_ref, y_ref, mod_ref):
    g = gtok_ref[...]
    y_lo = _unpack_pairs(jnp.concatenate([y_ref[0, 0], y_ref[1, 0]], axis=-1))
    y_hi = _unpack_pairs(jnp.concatenate([y_ref[0, 1], y_ref[1, 1]], axis=-1))
    return x1 + mod_ref[...][5:6] * (g[:, 0:1] * y_lo + g[:, 1:2] * y_hi)


def _premix_kernel(lat, fused, n_in, *refs):
    (x_ref, mod_ref, g1_ref, w_ref, gq_ref, wqa_ref, wqb_ref, gkv_ref, wka_ref, wv_ref, vone_ref, cs_ref,
     cos_ref, sin_ref) = refs[:14]
    outs = refs[n_in:]
    (ab_ref, z_ref, qn_ref, kn_ref, vn_ref, km_ref, fab_ref) = outs[:7]

    x = x_ref[...]
    if fused:
        gtok_ref, y_ref, modp_ref = refs[n_in - 3:n_in]
        x = _moe_residual(x, gtok_ref, y_ref, modp_ref)
        outs[-1][...] = x
    mod = mod_ref[...]
    h = _rms(x, g1_ref[...]) * (1.0 + mod[1:2]) + mod[0:1]
    p = _dot(h.astype(BF16), w_ref[...])

    ab_ref[...] = p[:, _C_AB:_C_AC].astype(BF16)
    z_ref[...] = (p[:, _C_AC:_C_AU] * p[:, _C_AU:_C_Q]).astype(BF16)
    k_na = p[:, _C_K:_C_V]
    v_na = p[:, _C_V:_C_CQ]
    qn_ref[...] = (p[:, _C_Q:_C_K] * NA_SCALE).astype(BF16)
    kn_ref[...] = k_na.astype(BF16)
    vn_ref[...] = v_na.astype(BF16)

    cqn = _rms(p[:, _C_CQ:_C_CKV], gq_ref[...]).astype(BF16)
    ckvn = _rms(p[:, _C_CKV:_C_KR], gkv_ref[...])
    ckvn_b = ckvn.astype(BF16)
    qa = _dot(cqn, wqa_ref[...])
    kva = _dot(ckvn_b, wka_ref[...])
    v_mla = _dot(ckvn_b, wv_ref[...]) + vone_ref[...]
    kr = p[:, _C_KR:_C_KRS]
    if lat:
        cos = cos_ref[...]
        sin = sin_ref[...]
        qb = _dot(cqn, wqb_ref[...])
        krot = kr * cos + p[:, _C_KRS:_C_FU] * sin
        qt_ref, vt_ref = outs[7:9]
    else:
        krot = kr
        qm_ref, vm_ref, ck_ref, cv_ref, cckv_ref, ckr_ref = outs[7:13]
    for hd in range(MLA_HEADS):
        sl = slice(hd * MLA_QK_PAD, (hd + 1) * MLA_QK_PAD)
        km_ref[:, sl] = (kva[:, sl] + krot).astype(BF16)
        if lat:
            qh = (qa[:, sl] * cos + qb[:, sl] * sin) * (MLA_SCALE * LOG2E)
            qt_ref[sl, :] = jnp.transpose(qh).astype(BF16)
        else:
            qm_ref[:, sl] = (qa[:, sl] * MLA_SCALE).astype(BF16)

    fab_ref[...] = _dot(p[:, _C_FU:_C_END].astype(BF16), cs_ref[...]).astype(BF16)

    if lat:
        for j in range(v_mla.shape[1] // LANES):
            sl = slice(j * LANES, (j + 1) * LANES)
            vt_ref[sl, :] = jnp.transpose(v_mla[:, sl]).astype(BF16)
    else:
        vm_ref[...] = v_mla.astype(BF16)
        for hd in range(NA_HEADS):
            sl = slice(hd * NA_HEAD_DIM, (hd + 1) * NA_HEAD_DIM)
            ck_ref[hd] = k_na[:, sl]
            cv_ref[hd] = v_na[:, sl]
        cckv_ref[...] = ckvn
        ckr_ref[...] = kr[:, MLA_NOPE_DIM:MLA_NOPE_DIM + MLA_ROPE_DIM]


def _cond_row(lat, tm, n, ctx_row):
    return (lambda i: (i * tm) // n) if lat else (lambda i: ctx_row)


def _premix(x, mod, layer, n, wts, rope, lat, tm, caches=None, pending=None):
    t, d = x.shape
    if lat:
        cos_t, sin_t = rope
        wv, vone = wts["wv_ext"], wts["vone_ext"]
    else:
        cos_t = sin_t = jnp.zeros((8, LANES), F32)
        wv, vone = wts["wv"], jnp.zeros((1, MLA_HEADS * MLA_V_DIM), F32)
    vw = wv.shape[-1]
    qw = MLA_HEADS * MLA_QK_PAD
    tiles_per_seq = n // tm
    cond_row = _cond_row(lat, tm, n, mod.shape[1] - 1)
    const = lambda *_: (0, 0)
    lsel = lambda *_: (layer, 0, 0)
    rope_spec = (pl.BlockSpec((tm, LANES), lambda i: (i % tiles_per_seq, 0)) if lat
                 else pl.BlockSpec((8, LANES), const))
    in_specs = [
        pl.BlockSpec((tm, d), lambda i: (i, 0)),
        pl.BlockSpec((None, None, 6, d), lambda i: (layer, cond_row(i), 0, 0)),
        pl.BlockSpec((None, 1, d), lsel),
        pl.BlockSpec((None, d, _C_END), lsel),
        pl.BlockSpec((None, 1, MLA_Q_LORA), lsel),
        pl.BlockSpec((None, MLA_Q_LORA, qw), lsel),
        pl.BlockSpec((None, MLA_Q_LORA, qw), lsel),
        pl.BlockSpec((None, 1, MLA_KV_LORA), lsel),
        pl.BlockSpec((None, MLA_KV_LORA, qw), lsel),
        pl.BlockSpec((None, MLA_KV_LORA, vw), lsel),
        pl.BlockSpec((1, vw), const),
        pl.BlockSpec((FN_WIDTH, 2 * FN_WIDTH), const),
        rope_spec,
        rope_spec,
    ]
    row = lambda w: pl.BlockSpec((tm, w), lambda i: (i, 0))
    widths = [CONV_WIDTH, CONV_WIDTH, NA_WIDTH, NA_WIDTH, NA_WIDTH, qw, 2 * FN_WIDTH]
    out_specs = [row(w) for w in widths]
    out_shape = [jax.ShapeDtypeStruct((t, w), BF16) for w in widths]
    if lat:
        out_specs += [pl.BlockSpec((None, qw, tm), lambda i: (i, 0, 0)),
                      pl.BlockSpec((None, vw, tm), lambda i: (i, 0, 0))]
        out_shape += [jax.ShapeDtypeStruct((t // tm, qw, tm), BF16),
                      jax.ShapeDtypeStruct((t // tm, vw, tm), BF16)]
    else:
        assert tm == n
        b = t // n
        depth = wts["w_in"].shape[0]
        out_specs += [
            row(qw), row(vw),
            pl.BlockSpec((None, None, NA_HEADS, n, NA_HEAD_DIM), lambda i: (i, layer, 0, 0, 0)),
            pl.BlockSpec((None, None, NA_HEADS, n, NA_HEAD_DIM), lambda i: (i, layer, 0, 0, 0)),
            pl.BlockSpec((None, None, n, MLA_KV_LORA), lambda i: (i, layer, 0, 0)),
            pl.BlockSpec((None, None, n, MLA_ROPE_DIM), lambda i: (i, layer, 0, 0)),
        ]
        out_shape += [
            jax.ShapeDtypeStruct((t, qw), BF16), jax.ShapeDtypeStruct((t, vw), BF16),
            jax.ShapeDtypeStruct((b, depth, NA_HEADS, n, NA_HEAD_DIM), F32),
            jax.ShapeDtypeStruct((b, depth, NA_HEADS, n, NA_HEAD_DIM), F32),
            jax.ShapeDtypeStruct((b, depth, n, MLA_KV_LORA), F32),
            jax.ShapeDtypeStruct((b, depth, n, MLA_ROPE_DIM), F32),
        ]
    args = [x, mod, wts["norm1"], wts["w_in"], wts["mla_gq"], wts["wq_a"], wts["wq_b"], wts["mla_gkv"],
            wts["wk_a"], wv, vone, wts["cs_bd"], cos_t, sin_t]
    aliases = {}
    if caches is not None:
        first_cache_out = len(out_shape) - len(caches)
        aliases = {len(args) + j: first_cache_out + j for j in range(len(caches))}
        in_specs += [pl.BlockSpec(memory_space=pl.ANY)] * len(caches)
        args += list(caches)
    if pending is not None:
        gtok, y_tok = pending
        in_specs += [
            row(LANES),
            pl.BlockSpec(y_tok.shape[:2] + (tm, y_tok.shape[3]), lambda i: (0, 0, i, 0)),
            pl.BlockSpec((None, None, 6, d), lambda i: (layer - 1, cond_row(i), 0, 0)),
        ]
        args += [gtok, y_tok, mod]
        out_specs = out_specs + [row(d)]
        out_shape = out_shape + [jax.ShapeDtypeStruct((t, d), F32)]
    return pl.pallas_call(
        functools.partial(_premix_kernel, lat, pending is not None, len(args)),
        grid=(t // tm,),
        in_specs=in_specs,
        out_specs=out_specs,
        out_shape=out_shape,
        input_output_aliases=aliases,
        compiler_params=_params("parallel"),
        name="premix_lat" if lat else "premix_ctx",
    )(*args)


def _softmax_attend(q, k, v):
    s = _nt_dot(q, k)
    m = jnp.max(s, axis=-1, keepdims=True)
    p = jnp.exp(s - m)
    l = jnp.sum(p, axis=-1, keepdims=True)
    return _dot(p.astype(BF16), v) / l


def _ctx_attn_kernel(qn_ref, kn_ref, vn_ref, qm_ref, km_ref, vm_ref, yna_ref, ymla_ref):
    for hd in range(NA_HEADS):
        sl = slice(hd * NA_HEAD_DIM, (hd + 1) * NA_HEAD_DIM)
        yna_ref[:, sl] = _softmax_attend(qn_ref[:, sl], kn_ref[:, sl], vn_ref[:, sl]).astype(BF16)
    for hd in range(MLA_HEADS):
        sq = slice(hd * MLA_QK_PAD, (hd + 1) * MLA_QK_PAD)
        sv = slice(hd * MLA_V_DIM, (hd + 1) * MLA_V_DIM)
        ymla_ref[:, sv] = _softmax_attend(qm_ref[:, sq], km_ref[:, sq], vm_ref[:, sv]).astype(BF16)


def _ctx_attention(qn, kn, vn, qm, km, vm, n):
    t = qn.shape[0]
    spec = lambda w: pl.BlockSpec((n, w), lambda b: (b, 0))
    ins = [qn, kn, vn, qm, km, vm]
    return pl.pallas_call(
        _ctx_attn_kernel,
        grid=(t // n,),
        in_specs=[spec(a.shape[1]) for a in ins],
        out_specs=[spec(NA_WIDTH), spec(MLA_HEADS * MLA_V_DIM)],
        out_shape=[jax.ShapeDtypeStruct((t, NA_WIDTH), BF16),
                   jax.ShapeDtypeStruct((t, MLA_HEADS * MLA_V_DIM), BF16)],
        compiler_params=_params("parallel"),
        name="ctx_attention",
    )(*ins)


def _mla_lat_kernel(qt_ref, k_ref, vt_ref, kx_ref, vxt_ref, o_ref, s_scr, p_scr):
    nchunk, _, kc = vt_ref.shape
    tq = qt_ref.shape[1]
    sub = MLA_KEY_SUB

    ksl = lambda hd: slice(hd * MLA_QK_PAD, (hd + 1) * MLA_QK_PAD)
    vsl = lambda hd: slice(hd * MLA_V_PAD, (hd + 1) * MLA_V_PAD)

    def scores(slot, k_of, nk):
        cmax = []
        for hd in range(MLA_HEADS):
            qt = qt_ref[ksl(hd), :]
            part = None
            for j in range(0, nk, sub):
                st = _dot(k_of(hd, j), qt)
                s_scr[slot, hd, j:j + sub, :] = st
                blk = jnp.max(st.reshape(sub // 8, 8, tq), axis=0)
                part = blk if part is None else jnp.maximum(part, blk)
            cmax.append(jnp.max(part, axis=0, keepdims=True))
        return tuple(cmax)

    def attend(slot, cmax, state, vt_of, nk):
        new = []
        for hd in range(MLA_HEADS):
            m_i, acc = state[hd]
            m_new = jnp.maximum(m_i, cmax[hd])
            for j in range(0, nk, sub):
                p_scr[hd, j:j + sub, :] = jnp.exp2(s_scr[slot, hd, j:j + sub, :] - m_new).astype(BF16)
            acc = jnp.exp2(m_i - m_new) * acc + _dot(vt_of(hd), p_scr[hd, 0:nk, :])
            new.append((m_new, acc))
        return tuple(new)

    lat_keys = lambda c: (lambda hd, j: k_ref[pl.ds(pl.multiple_of(c * kc, kc) + j, sub), ksl(hd)])
    past = kx_ref.shape[0]
    state = tuple((jnp.full((1, tq), NEG_INF, F32), jnp.zeros((MLA_V_PAD, tq), F32)) for _ in range(MLA_HEADS))
    cmax_ctx = scores(1, lambda hd, j: kx_ref[j:j + sub, ksl(hd)], past)
    cmax = scores(0, lat_keys(0), kc)
    state = attend(1, cmax_ctx, state, lambda hd: vxt_ref[vsl(hd), :], past)

    lat_vals = lambda c: (lambda hd: vt_ref[c, vsl(hd), :])

    def body(i, carry):
        cmax0, state = carry
        c = 2 * i
        cmax1 = scores(1, lat_keys(c + 1), kc)
        state = attend(0, cmax0, state, lat_vals(c), kc)
        cmax0 = scores(0, lat_keys(c + 2), kc)
        state = attend(1, cmax1, state, lat_vals(c + 1), kc)
        return cmax0, state

    cmax, state = lax.fori_loop(0, nchunk // 2 - 1, body, (cmax, state))
    cmax1 = scores(1, lat_keys(nchunk - 1), kc)
    state = attend(0, cmax, state, lat_vals(nchunk - 2), kc)
    state = attend(1, cmax1, state, lat_vals(nchunk - 1), kc)
    o_t = jnp.concatenate([acc[:MLA_V_DIM] / acc[MLA_V_DIM:MLA_V_DIM + 1] for _, acc in state], axis=0)
    o_ref[...] = jnp.transpose(o_t).astype(BF16)


def _mla_lat_attention(qt, km, vt, kx, vxt, layer, n):
    ntile, qw, tq = qt.shape
    t = ntile * tq
    past = kx.shape[2]
    qpb = n // tq
    return pl.pallas_call(
        _mla_lat_kernel,
        grid=(t // n, qpb),
        in_specs=[
            pl.BlockSpec((None, qw, tq), lambda b, i: (b * qpb + i, 0, 0)),
            pl.BlockSpec((n, km.shape[1]), lambda b, i: (b, 0)),
            pl.BlockSpec((qpb, vt.shape[1], tq), lambda b, i: (b, 0, 0)),
            pl.BlockSpec((None, None, past, kx.shape[3]), lambda b, i: (layer, b, 0, 0)),
            pl.BlockSpec((None, None, vxt.shape[2], past), lambda b, i: (layer, b, 0, 0)),
        ],
        out_specs=pl.BlockSpec((tq, MLA_HEADS * MLA_V_DIM), lambda b, i: (b * qpb + i, 0)),
        out_shape=jax.ShapeDtypeStruct((t, MLA_HEADS * MLA_V_DIM), BF16),
        scratch_shapes=[pltpu.VMEM((2, MLA_HEADS, max(tq, past), tq), F32),
                        pltpu.VMEM((MLA_HEADS, max(tq, past), tq), BF16)],
        compiler_params=_params("parallel", "parallel"),
        name="mla_lat_attention",
    )(qt, km, vt, kx, vxt)


def _ctx_kv_kernel(ckv_ref, kr_ref, wka_ref, wv_ref, vone_ref, place_ref, k_ref, vt_ref):
    ckv = ckv_ref[...].astype(BF16)
    k_ref[...] = (_dot(ckv, wka_ref[...]) + _dot(kr_ref[...].astype(BF16), place_ref[...])).astype(BF16)
    v = _dot(ckv, wv_ref[...]) + vone_ref[...]
    for j in range(v.shape[1] // LANES):
        sl = slice(j * LANES, (j + 1) * LANES)
        vt_ref[sl, :] = jnp.transpose(v[:, sl]).astype(BF16)


def _ctx_kv(cache_ckv, cache_krope, wk_a, wv_ext, vone_ext):
    bd, depth, past, _ = cache_ckv.shape
    place = np.zeros((MLA_ROPE_DIM, MLA_HEADS * MLA_QK_PAD), np.float32)
    for hd in range(MLA_HEADS):
        for i in range(MLA_ROPE_DIM):
            place[i, hd * MLA_QK_PAD + MLA_NOPE_DIM + i] = 1.0
    kw, vw = MLA_HEADS * MLA_QK_PAD, MLA_HEADS * MLA_V_PAD
    return pl.pallas_call(
        _ctx_kv_kernel,
        grid=(depth, bd),
        in_specs=[
            pl.BlockSpec((None, None, past, MLA_KV_LORA), lambda l, b: (b, l, 0, 0)),
            pl.BlockSpec((None, None, past, MLA_ROPE_DIM), lambda l, b: (b, l, 0, 0)),
            pl.BlockSpec((None, MLA_KV_LORA, kw), lambda l, b: (l, 0, 0)),
            pl.BlockSpec((None, MLA_KV_LORA, vw), lambda l, b: (l, 0, 0)),
            pl.BlockSpec((1, vw), lambda l, b: (0, 0)),
            pl.BlockSpec((MLA_ROPE_DIM, kw), lambda l, b: (0, 0)),
        ],
        out_specs=[pl.BlockSpec((None, None, past, kw), lambda l, b: (l, b, 0, 0)),
                   pl.BlockSpec((None, None, vw, past), lambda l, b: (l, b, 0, 0))],
        out_shape=[jax.ShapeDtypeStruct((depth, bd, past, kw), BF16),
                   jax.ShapeDtypeStruct((depth, bd, vw, past), BF16)],
        compiler_params=_params("parallel", "parallel"),
        name="ctx_kv",
    )(cache_ckv, cache_krope, wk_a, wv_ext, vone_ext, jnp.asarray(place, BF16))


def _na_tile_geometry(rows):
    last = rows // NA_Q_ROWS - 1
    geo = []
    for j in (0, 1, last):
        r0 = j * NA_Q_ROWS
        geo.append((r0, min(max(r0 - NA_KH // 2, 0), rows - NA_WIN_ROWS)))
    return geo


def _na_bias_kernel(geo, rows, rpb_ref, o_ref):
    l = pl.program_id(0)
    hd = pl.program_id(1)
    base = (l * NA_HEADS + hd) * (2 * NA_KH - 1) * (2 * NA_KW - 1)
    qc = lax.broadcasted_iota(jnp.int32, (GRID_W, GRID_W), 0)
    kcol = lax.broadcasted_iota(jnp.int32, (GRID_W, GRID_W), 1)
    d_col = jnp.clip(kcol - qc + (NA_KW - 1), 0, 2 * NA_KW - 2)
    col_start = jnp.clip(qc - NA_KW // 2, 0, GRID_W - NA_KW)
    in_cols = (kcol >= col_start) & (kcol < col_start + NA_KW)
    neg = jnp.full((GRID_W, GRID_W), NEG_INF, F32)
    tabs = []
    for dr in range(2 * NA_KH - 1):
        acc = jnp.zeros((GRID_W, GRID_W), F32)
        for dc in range(2 * NA_KW - 1):
            acc = jnp.where(d_col == dc, rpb_ref[base + dr * (2 * NA_KW - 1) + dc], acc)
        tabs.append(jnp.where(in_cols, acc, neg))
    for kind, (r0, ws) in enumerate(geo):
        for i in range(NA_Q_ROWS):
            r = r0 + i
            lo = min(max(r - NA_KH // 2, 0), rows - NA_KH)
            for j in range(NA_WIN_ROWS):
                kr = ws + j
                blk = tabs[kr - r + NA_KH - 1] if lo <= kr < lo + NA_KH else neg
                o_ref[kind, i * GRID_W:(i + 1) * GRID_W, j * GRID_W:(j + 1) * GRID_W] = blk


def _na_bias(na_rpb, rows):
    depth = na_rpb.shape[0]
    geo = _na_tile_geometry(rows)
    qn, kn = NA_Q_ROWS * GRID_W, NA_WIN_ROWS * GRID_W
    return pl.pallas_call(
        functools.partial(_na_bias_kernel, geo, rows),
        grid=(depth, NA_HEADS),
        in_specs=[pl.BlockSpec(memory_space=pltpu.SMEM)],
        out_specs=pl.BlockSpec((None, None, 3, qn, kn), lambda l, h: (l, h, 0, 0, 0)),
        out_shape=jax.ShapeDtypeStruct((depth, NA_HEADS, 3, qn, kn), F32),
        compiler_params=_params("parallel", "parallel"),
        name="na_bias",
    )(na_rpb.reshape(-1))


def _na_lat_kernel(rows, q_ref, k_ref, v_ref, kx_ref, vx_ref, bias_ref, o_ref, s_scr, p_scr):
    j = pl.program_id(1)
    ws = jnp.clip(j * NA_Q_ROWS - NA_KH // 2, 0, rows - NA_WIN_ROWS)
    start = pl.multiple_of(ws * GRID_W, GRID_W)
    nk = NA_WIN_ROWS * GRID_W
    heads = [slice(hd * NA_HEAD_DIM, (hd + 1) * NA_HEAD_DIM) for hd in range(NA_HEADS)]
    m = []
    for hd, sl in enumerate(heads):
        q = q_ref[:, sl]
        s_win = _nt_dot(q, k_ref[pl.ds(start, nk), sl]) + bias_ref[hd]
        s_ctx = _nt_dot(q, kx_ref[hd].astype(BF16))
        s_scr[hd, :, :nk] = s_win
        s_scr[hd, :, nk:] = s_ctx
        m.append(jnp.maximum(jnp.max(s_win, axis=-1, keepdims=True), jnp.max(s_ctx, axis=-1, keepdims=True)))
    l = []
    for hd in range(NA_HEADS):
        p = jnp.exp(s_scr[hd] - m[hd])
        l.append(jnp.sum(p, axis=-1, keepdims=True))
        p_scr[hd] = p.astype(BF16)
    for hd, sl in enumerate(heads):
        o = _dot(p_scr[hd, :, :nk], v_ref[pl.ds(start, nk), sl]) + _dot(p_scr[hd, :, nk:], vx_ref[hd].astype(BF16))
        o_ref[:, sl] = (o / l[hd]).astype(BF16)


def _na_lat_attention(qn, kn, vn, cache_k, cache_v, bias, layer, n):
    t = qn.shape[0]
    rows = n // GRID_W
    assert rows % NA_Q_ROWS == 0 and rows >= NA_WIN_ROWS + NA_Q_ROWS
    tiles = rows // NA_Q_ROWS
    tq = NA_Q_ROWS * GRID_W
    past = cache_k.shape[3]

    def kind(b, j):
        return (layer, 0, jnp.where(j == 0, 0, jnp.where(j == tiles - 1, 2, 1)), 0, 0)

    return pl.pallas_call(
        functools.partial(_na_lat_kernel, rows),
        grid=(t // n, tiles),
        in_specs=[
            pl.BlockSpec((tq, NA_WIDTH), lambda b, j: (b * tiles + j, 0)),
            pl.BlockSpec((n, NA_WIDTH), lambda b, j: (b, 0)),
            pl.BlockSpec((n, NA_WIDTH), lambda b, j: (b, 0)),
            pl.BlockSpec((None, None, NA_HEADS, past, NA_HEAD_DIM), lambda b, j: (b, layer, 0, 0, 0)),
            pl.BlockSpec((None, None, NA_HEADS, past, NA_HEAD_DIM), lambda b, j: (b, layer, 0, 0, 0)),
            pl.BlockSpec((None, NA_HEADS, None, tq, NA_WIN_ROWS * GRID_W), kind),
        ],
        out_specs=pl.BlockSpec((tq, NA_WIDTH), lambda b, j: (b * tiles + j, 0)),
        out_shape=jax.ShapeDtypeStruct((t, NA_WIDTH), BF16),
        scratch_shapes=[pltpu.VMEM((NA_HEADS, tq, NA_WIN_ROWS * GRID_W + past), F32),
                        pltpu.VMEM((NA_HEADS, tq, NA_WIN_ROWS * GRID_W + past), BF16)],
        compiler_params=_params("parallel", "parallel"),
        name="na_lat_attention",
    )(qn, kn, vn, cache_k, cache_v, bias)


def _dft_tables(n):
    def thin(j, k, period):
        ang = (2.0 * math.pi / period) * ((j[:, None] * k[None, :]) % period).astype(F32)
        return jnp.cos(ang), jnp.sin(ang)

    k = jnp.arange(n, dtype=jnp.int32)
    scale = float(n) ** -0.5
    if n % 64 == 0 and n > 64:
        n1 = n // 64
        c1, s1 = thin(jnp.arange(n1, dtype=jnp.int32), k, n1)
        c2, s2 = thin(jnp.arange(64, dtype=jnp.int32), k, n)
        c1, s1, c2, s2 = c1[:, None, :], s1[:, None, :], c2[None, :, :], s2[None, :, :]
        cm = (c1 * c2 - s1 * s2).reshape(n, n)
        sm = (s1 * c2 + c1 * s2).reshape(n, n)
    else:
        cm, sm = thin(k, k, n)
    return (cm * scale).astype(BF16), (sm * -scale).astype(BF16)


def _fourier_kernel(c_ref, s_ref, ab_ref, o_ref):
    o_ref[...] = (_dot(c_ref[...], ab_ref[:, :FN_WIDTH]) + _dot(s_ref[...], ab_ref[:, FN_WIDTH:])).astype(BF16)


def _fourier_half_kernel(c_ref, s_ref, cmid_ref, ab_ref, plus_ref, minus_ref, mid_ref):
    a = ab_ref[:, :FN_WIDTH]
    p = _dot(c_ref[...], a)
    q = _dot(s_ref[...], ab_ref[:, FN_WIDTH:])
    plus_ref[...] = (p + q).astype(BF16)
    minus_ref[...] = (p - q).astype(BF16)
    mid_ref[...] = _dot(cmid_ref[...], a).astype(BF16)


def _fourier_half(fab, tables, n, tmf=512):
    t = fab.shape[0]
    b = t // n
    half = n // 2
    tiles = half // tmf
    cm, sm = tables
    cmid = jnp.broadcast_to(cm[half:half + 1], (8, n))
    row_out = lambda: pl.BlockSpec((tmf, FN_WIDTH), lambda i, bb: (bb * tiles + i, 0))
    plus, minus, mid = pl.pallas_call(
        _fourier_half_kernel,
        grid=(tiles, b),
        in_specs=[
            pl.BlockSpec((tmf, n), lambda i, bb: (i, 0)),
            pl.BlockSpec((tmf, n), lambda i, bb: (i, 0)),
            pl.BlockSpec((8, n), lambda i, bb: (0, 0)),
            pl.BlockSpec((n, 2 * FN_WIDTH), lambda i, bb: (bb, 0)),
        ],
        out_specs=[row_out(), row_out(), pl.BlockSpec((None, None, 8, FN_WIDTH), lambda i, bb: (i, bb, 0, 0))],
        out_shape=[jax.ShapeDtypeStruct((b * half, FN_WIDTH), BF16), jax.ShapeDtypeStruct((b * half, FN_WIDTH), BF16),
                   jax.ShapeDtypeStruct((tiles, b, 8, FN_WIDTH), BF16)],
        compiler_params=_params("parallel", "parallel"),
        name="fourier_half",
    )(cm, sm, cmid, fab)
    return plus, minus, mid


def _fourier(fab, tables, n, tmf=512):
    if n >= 4 * tmf and tmf == TM_LAT_MIXOUT:
        return _fourier_half(fab, tables, n, tmf)
    t = fab.shape[0]
    tmf = min(tmf, n)
    tiles = n // tmf
    cm, sm = tables
    return pl.pallas_call(
        _fourier_kernel,
        grid=(tiles, t // n),
        in_specs=[
            pl.BlockSpec((tmf, n), lambda i, b: (i, 0)),
            pl.BlockSpec((tmf, n), lambda i, b: (i, 0)),
            pl.BlockSpec((n, 2 * FN_WIDTH), lambda i, b: (b, 0)),
        ],
        out_specs=pl.BlockSpec((tmf, FN_WIDTH), lambda i, b: (b * tiles + i, 0)),
        out_shape=jax.ShapeDtypeStruct((t, FN_WIDTH), BF16),
        compiler_params=_params("parallel", "parallel"),
        name="fourier",
    )(cm, sm, fab)


def _route(s_t, sb_t):
    def top2_sum(v):
        hi1, lo1 = jnp.maximum(v[0], v[1]), jnp.minimum(v[0], v[1])
        hi2, lo2 = jnp.maximum(v[2], v[3]), jnp.minimum(v[2], v[3])
        return jnp.maximum(hi1, hi2) + jnp.maximum(jnp.minimum(hi1, hi2), jnp.maximum(lo1, lo2))

    best = top2_sum(sb_t[0:EXPERTS_PER_GROUP])
    gsel = jnp.zeros_like(best, dtype=jnp.int32)
    for g in range(1, N_EXPERT_GROUPS):
        cand = top2_sum(sb_t[g * EXPERTS_PER_GROUP:(g + 1) * EXPERTS_PER_GROUP])
        better = cand > best
        gsel = jnp.where(better, g, gsel)
        best = jnp.where(better, cand, best)
    chosen = []
    for e in range(N_EXPERTS):
        g = e // EXPERTS_PER_GROUP
        beaten = jnp.zeros_like(gsel)
        for o in range(g * EXPERTS_PER_GROUP, (g + 1) * EXPERTS_PER_GROUP):
            if o == e:
                continue
            ahead = (sb_t[o] > sb_t[e]) | ((sb_t[o] == sb_t[e]) & (o < e))
            beaten = beaten + ahead.astype(jnp.int32)
        chosen.append((gsel == g) & (beaten < 2))
    picked = [jnp.where(chosen[e], s_t[e], 0.0) for e in range(N_EXPERTS)]
    denom = picked[0]
    for e in range(1, N_EXPERTS):
        denom = denom + picked[e]
    return chosen, [pk / denom for pk in picked]


def _pack_pairs(x):
    w = x.shape[1] // 2
    hi = pltpu.bitcast(x[:, :w].astype(BF16).astype(F32), jnp.uint32)
    lo = pltpu.bitcast(x[:, w:].astype(BF16).astype(F32), jnp.uint32)
    return hi | (lo >> 16)


def _half_spectrum_tile(i, tiles_per_seq, plus_ref, minus_ref, edge_ref, mid_ref, rev_ref):
    j = i % tiles_per_seq - tiles_per_seq // 2
    body = _dot(rev_ref[...], minus_ref[...]).astype(BF16)
    first = jnp.where(j == 0, mid_ref[0:1, :], edge_ref[0:1, :])
    rows = lax.broadcasted_iota(jnp.int32, body.shape, 0)
    upper = jnp.where(rows == 0, first, body)
    return jnp.where(j < 0, plus_ref[...], upper)


def _mixout_kernel(n, half, *refs):
    x_ref, mod_ref, ab_ref, z_ref, zp_ref, zn_ref, yna_ref, ymla_ref = refs[:8]
    n_g = 5 if half else 1
    g_refs = refs[8:8 + n_g]
    (cw_ref, wfn_ref, wout_ref, g2_ref, wrc_ref, br_ref, x1_ref, h2_ref, route_ref, gtok_ref,
     cnt_ref) = refs[8 + n_g:]
    tm = x_ref.shape[0]
    i = pl.program_id(0)
    g_tile = _half_spectrum_tile(i, n // tm, *g_refs) if half else g_refs[0][...]
    mod = mod_ref[...]
    gate1, shift2, scale2 = mod[2:3], mod[3:4], mod[4:5]

    z = z_ref[...].astype(F32)
    ridx = lax.broadcasted_iota(jnp.int32, z.shape, 0)
    at_start = (i * tm) % n == 0
    at_end = ((i + 1) * tm) % n == 0
    prev_row = jnp.where(at_start, 0.0, zp_ref[7:8, :].astype(F32))
    next_row = jnp.where(at_end, 0.0, zn_ref[0:1, :].astype(F32))
    z_m1 = jnp.where(ridx == 0, prev_row, pltpu.roll(z, 1, axis=0))
    z_p1 = jnp.where(ridx == tm - 1, next_row, pltpu.roll(z, tm - 1, axis=0))
    cw = cw_ref[...]
    y_conv = ab_ref[...].astype(F32) * (z_m1 * cw[0:1] + z * cw[1:2] + z_p1 * cw[2:3])

    y_fn = _dot(g_tile, wfn_ref[...])
    cat = jnp.concatenate([y_conv.astype(BF16), yna_ref[...], ymla_ref[...], y_fn.astype(BF16)], axis=-1)
    x1 = x_ref[...] + gate1 * _dot(cat, wout_ref[...])
    x1_ref[...] = x1

    h2 = _rms(x1, g2_ref[...]) * (1.0 + scale2) + shift2
    packed = _pack_pairs(h2)
    piece = packed.shape[1] // MOE_PIECES
    for p in range(MOE_PIECES):
        h2_ref[p] = packed[:, p * piece:(p + 1) * piece]
    h2_hi = h2.astype(BF16)
    h2_lo = (h2 - h2_hi.astype(F32)).astype(BF16)
    both = _dot(h2_hi, wrc_ref[...])
    logits = both[:, :LANES] + (both[:, LANES:] + _dot(h2_lo, wrc_ref[:, :LANES]))
    s = jax.nn.sigmoid(logits)
    s_t = jnp.transpose(s)
    sb_t = jnp.transpose(s + br_ref[...])
    chosen, gates = _route([s_t[e:e + 1] for e in range(N_EXPERTS)], [sb_t[e:e + 1] for e in range(N_EXPERTS)])

    @pl.when(i == 0)
    def _():
        cnt_ref[...] = jnp.zeros(cnt_ref.shape, F32)

    chosen_f = jnp.concatenate([ch.astype(F32) for ch in chosen], axis=0)
    before = lax.broadcasted_iota(jnp.int32, (tm, tm), 0) < lax.broadcasted_iota(jnp.int32, (tm, tm), 1)
    prefix = _dot(chosen_f.astype(BF16), jnp.where(before, 1.0, 0.0).astype(BF16))
    base = cnt_ref[...]
    rank = jnp.concatenate([base] * (tm // LANES), axis=1) + prefix
    cnt_ref[...] = base + jnp.sum(chosen_f, axis=1, keepdims=True)

    zero = jnp.zeros((1, tm), F32)
    seen = zero
    slots = [[zero, zero, zero], [zero, zero, zero]]
    for e in range(N_EXPERTS):
        for k in range(2):
            hit = chosen[e] & (seen == float(k))
            for j, val in enumerate((float(e), gates[e], rank[e:e + 1])):
                slots[k][j] = jnp.where(hit, val, slots[k][j])
        seen = seen + chosen_f[e:e + 1]
    (e_lo, g_lo, r_lo), (e_hi, g_hi, r_hi) = slots
    route_ref[...] = jnp.concatenate([g_lo, g_hi, e_lo, e_hi, r_lo, r_hi, zero, zero], axis=0)
    gates_t = jnp.concatenate([g_lo, g_hi, jnp.zeros((LANES - 2, tm), F32)], axis=0)
    gtok_ref[...] = jnp.transpose(gates_t)


def _mixout(x, mod, layer, n, parts, wts, lat, tm):
    t, d = x.shape
    ab, z, yna, ymla, g = parts
    nblk8 = t // 8
    per8 = tm // 8
    cond_row_of_tile = _cond_row(lat, tm, n, mod.shape[1] - 1)
    const2 = lambda i: (0, 0)
    lsel = lambda i: (layer, 0, 0)
    row = lambda w: pl.BlockSpec((tm, w), lambda i: (i, 0))
    in_specs = [
        row(d),
        pl.BlockSpec((None, None, 6, d), lambda i: (layer, cond_row_of_tile(i), 0, 0)),
        row(CONV_WIDTH),
        row(CONV_WIDTH),
        pl.BlockSpec((8, CONV_WIDTH), lambda i: (jnp.maximum(i * per8 - 1, 0), 0)),
        pl.BlockSpec((8, CONV_WIDTH), lambda i: (jnp.minimum((i + 1) * per8, nblk8 - 1), 0)),
        row(NA_WIDTH),
        row(MLA_HEADS * MLA_V_DIM),
    ]
    half = isinstance(g, tuple)
    if half:
        plus, minus, mid = g
        tps = n // tm
        hps = tps // 2
        assert plus.shape[0] * 2 == t and tps % 2 == 0
        rev = np.zeros((tm, tm), np.float32)
        rev[np.arange(1, tm), tm - np.arange(1, tm)] = 1.0
        src = lambda i: (i // tps) * hps + jnp.clip(tps - 1 - i % tps, 0, hps - 1)
        in_specs += [
            pl.BlockSpec((tm, FN_WIDTH), lambda i: ((i // tps) * hps + jnp.minimum(i % tps, hps - 1), 0)),
            pl.BlockSpec((tm, FN_WIDTH), lambda i: (src(i), 0)),
            pl.BlockSpec((8, FN_WIDTH), lambda i: (jnp.minimum(src(i) + 1, plus.shape[0] // tm - 1) * per8, 0)),
            pl.BlockSpec((None, None, 8, FN_WIDTH), lambda i: (0, i // tps, 0, 0)),
            pl.BlockSpec((tm, tm), const2),
        ]
        g_args = [plus, minus, minus, mid, jnp.asarray(rev, BF16)]
    else:
        in_specs += [row(FN_WIDTH)]
        g_args = [g]
    in_specs += [
        pl.BlockSpec((None, 3, CONV_WIDTH), lsel),
        pl.BlockSpec((None, FN_WIDTH, FN_WIDTH), lsel),
        pl.BlockSpec((None, d, d), lsel),
        pl.BlockSpec((None, 1, d), lsel),
        pl.BlockSpec((d, 2 * LANES), const2),
        pl.BlockSpec((1, LANES), const2),
    ]
    out_specs = [
        row(d),
        pl.BlockSpec((MOE_PIECES, tm, d // 2 // MOE_PIECES), lambda i: (0, i, 0)),
        pl.BlockSpec((8, tm), lambda i: (0, i)),
        row(LANES),
        pl.BlockSpec((N_EXPERTS, LANES), const2),
    ]
    out_shape = [
        jax.ShapeDtypeStruct((t, d), F32),
        jax.ShapeDtypeStruct((MOE_PIECES, t, d // 2 // MOE_PIECES), jnp.uint32),
        jax.ShapeDtypeStruct((8, t), F32),
        jax.ShapeDtypeStruct((t, LANES), F32),
        jax.ShapeDtypeStruct((N_EXPERTS, LANES), F32),
    ]
    return pl.pallas_call(
        functools.partial(_mixout_kernel, n, half),
        grid=(t // tm,),
        in_specs=in_specs,
        out_specs=out_specs,
        out_shape=out_shape,
        compiler_params=_params("arbitrary"),
        name="mixout",
    )(x, mod, ab, z, z, z, yna, ymla, *g_args, wts["conv_w"], wts["w_fn"], wts["w_out"], wts["norm2"],
      wts["wr_cat"], wts["b_router"])


def _slot_positions(route, counts, rb):
    cnt = counts[:, 0].astype(jnp.int32)
    padded = (cnt + rb - 1) // rb * rb
    ends = jnp.cumsum(padded)
    offs = ends - padded
    experts = route[2:4].astype(jnp.int32)
    ranks = route[4:6].astype(jnp.int32)
    pos = ranks
    for e in range(N_EXPERTS):
        pos = pos + jnp.where(experts == e, offs[e], 0)
    nblk = (2 * route.shape[1]) // rb + N_EXPERTS
    starts = jnp.arange(nblk, dtype=jnp.int32) * rb
    blk_expert = jnp.sum((starts[:, None] >= ends[None, :]).astype(jnp.int32), axis=1)
    used = blk_expert < N_EXPERTS
    blk_expert = jnp.where(used, blk_expert, 0)
    valid_end = jnp.sum(jnp.where(blk_expert[:, None] == jnp.arange(N_EXPERTS)[None, :], (offs + cnt)[None, :], 0), axis=1)
    blk_valid = jnp.where(used, jnp.clip(valid_end - starts, 0, rb), 0)
    return pos, jnp.stack([blk_expert, blk_valid])


def _sc_mesh():
    return plsc.VectorSubcoreMesh(core_axis_name="c", subcore_axis_name="s")


def _sc_pipeline(body, nwin, in_specs, out_specs):
    return pltpu.emit_pipeline(body, grid=(nwin,), in_specs=in_specs, out_specs=out_specs,
                               core_axis_name=("c", "s"), dimension_semantics=(pltpu.PARALLEL,))


def _row_scatter(table, idx_a, idx_b, nrows):
    b, w = table.shape
    win = SC_WINDOW
    idx_spec = pl.BlockSpec((1, win), lambda i: (0, i))

    @functools.partial(pl.kernel, out_type=jax.ShapeDtypeStruct((nrows, w), table.dtype), mesh=_sc_mesh(),
                       scratch_types=[])
    def scatter(table_hbm, ia_hbm, ib_hbm, out_hbm):
        def body(rows_vmem, ia_vmem, ib_vmem):
            pltpu.sync_copy(rows_vmem, out_hbm.at[ia_vmem.at[0]])
            pltpu.sync_copy(rows_vmem, out_hbm.at[ib_vmem.at[0]])

        _sc_pipeline(body, b // win, [pl.BlockSpec((win, w), lambda i: (i, 0)), idx_spec, idx_spec], [])(
            table_hbm, ia_hbm, ib_hbm)

    return scatter(table, idx_a.reshape(1, b), idx_b.reshape(1, b))


def _row_gather(table, idx):
    b = idx.shape[0]
    w = table.shape[1]
    win = SC_WINDOW

    @functools.partial(pl.kernel, out_type=jax.ShapeDtypeStruct((b, w), table.dtype), mesh=_sc_mesh(),
                       scratch_types=[])
    def gather(table_hbm, idx_hbm, out_hbm):
        def body(idx_vmem, out_vmem):
            pltpu.sync_copy(table_hbm.at[idx_vmem.at[0]], out_vmem)

        _sc_pipeline(body, b // win, [pl.BlockSpec((1, win), lambda i: (0, i))],
                     [pl.BlockSpec((win, w), lambda i: (i, 0))])(idx_hbm, out_hbm)

    return gather(table, idx.reshape(1, b))


def _ffn_kernel(blk_ref, xs_ref, w13_ref, w2_ref, y_ref):
    i = pl.program_id(0)
    e = blk_ref[0, i]
    nvalid = blk_ref[1, i]

    @pl.when(nvalid > 0)
    def _():
        packed = jnp.concatenate([xs_ref[0], xs_ref[1]], axis=-1)
        live = lax.broadcasted_iota(jnp.int32, packed.shape, 0) < nvalid
        xb = _unpack_pairs(jnp.where(live, packed, jnp.uint32(0))).astype(BF16)
        up = _dot(xb, w13_ref[e])
        a, b = up[:, :EXPERT_FF], up[:, EXPERT_FF:]
        hid = (a * jax.nn.sigmoid(a)) * b
        y = _pack_pairs(_dot(hid.astype(BF16), w2_ref[e]))
        half = y.shape[1] // 2
        y_ref[0] = y[:, :half]
        y_ref[1] = y[:, half:]

    @pl.when(nvalid == 0)
    def _():
        y_ref[...] = jnp.zeros(y_ref.shape, y_ref.dtype)


def _expert_ffn(xs, blk, w13, w2, layer, rb):
    pieces, nrows, w = xs.shape
    d = 2 * pieces * w
    resident = dict(pipeline_mode=pl.Buffered(1))
    used = lambda i, blk: (0, jnp.where(blk[1, i] > 0, i, 0), 0)
    return pl.pallas_call(
        _ffn_kernel,
        grid_spec=pltpu.PrefetchScalarGridSpec(
            num_scalar_prefetch=1,
            grid=(nrows // rb,),
            in_specs=[
                pl.BlockSpec((pieces, rb, w), used),
                pl.BlockSpec((None, N_EXPERTS, d, 2 * EXPERT_FF), lambda i, blk: (layer, 0, 0, 0), **resident),
                pl.BlockSpec((None, N_EXPERTS, EXPERT_FF, d), lambda i, blk: (layer, 0, 0, 0), **resident),
            ],
            out_specs=pl.BlockSpec((pieces, rb, w), lambda i, blk: (0, i, 0)),
        ),
        out_shape=jax.ShapeDtypeStruct(xs.shape, xs.dtype),
        compiler_params=_params("parallel"),
        name="expert_ffn",
    )(blk, xs, w13, w2)


def _combine_kernel(final, x1_ref, gtok_ref, mod_ref, nf_ref, y_ref, o_ref):
    out = _moe_residual(x1_ref[...], gtok_ref, y_ref, mod_ref)
    if final:
        out = _rms(out, nf_ref[...])
    o_ref[...] = out


def _combine(x1, gtok, y_tok, mod, layer, n, norm_f, final, lat, tc=512):
    t, d = x1.shape
    cond_row_of_tile = _cond_row(lat, tc, n, mod.shape[1] - 1)
    row = lambda w: pl.BlockSpec((tc, w), lambda i: (i, 0))
    return pl.pallas_call(
        functools.partial(_combine_kernel, final),
        grid=(t // tc,),
        in_specs=[
            row(d),
            row(LANES),
            pl.BlockSpec((None, None, 6, d), lambda i: (layer, cond_row_of_tile(i), 0, 0)),
            pl.BlockSpec((1, d), lambda i: (0, 0)),
            pl.BlockSpec(y_tok.shape[:2] + (tc, y_tok.shape[3]), lambda i: (0, 0, i, 0)),
        ],
        out_specs=row(d),
        out_shape=jax.ShapeDtypeStruct((t, d), F32),
        compiler_params=_params("parallel"),
        name="combine",
    )(x1, gtok, mod, norm_f, y_tok)


def _moe_layer(x, mod, layer, n, parts, wts, lat, tm, rb):
    t = x.shape[0]
    x1, h2, route, gtok, counts = _mixout(x, mod, layer, n, parts, wts, lat, tm)
    pos, blk = _slot_positions(route, counts, rb)
    pieces, _, w = h2.shape
    nrows = blk.shape[1] * rb
    piece_base = (jnp.arange(pieces, dtype=jnp.int32) * nrows)[:, None]
    idx = [(piece_base + pos[s][None, :]).reshape(-1) for s in range(2)]
    xs = _row_scatter(h2.reshape(pieces * t, w), idx[0], idx[1], pieces * nrows).reshape(pieces, nrows, w)
    y = _expert_ffn(xs, blk, wts["w13"], wts["w2"], layer, rb)
    back = (piece_base[:, :, None] + pos[None, :, :]).reshape(-1)
    y_tok = _row_gather(y.reshape(pieces * nrows, w), back).reshape(pieces, 2, t, w)
    return x1, (gtok, y_tok)


def _swap_halves(w):
    nf = MLA_ROPE_DIM // 4
    idx = np.arange(MLA_ROPE_DIM).reshape(2, 2, nf)[:, ::-1, :].reshape(-1)
    return w[..., idx]


def _pack_weights(w_in, mla_wq_up, mla_wkv_up, w1, w3, w2, w_router, b_router):
    depth, d, _ = w_in.shape
    zeros = lambda w: jnp.zeros((depth, d, w), w_in.dtype)
    w_kr = w_in[..., 1920:1952]
    pad_rope = lambda w: jnp.concatenate([zeros(MLA_NOPE_DIM), w, zeros(MLA_QK_PAD - MLA_NOPE_DIM - MLA_ROPE_DIM)], -1)
    w_main = jnp.concatenate([w_in[..., :1920], pad_rope(w_kr), pad_rope(_swap_halves(w_kr)), w_in[..., 1952:]], -1)

    wq = mla_wq_up.reshape(depth, MLA_Q_LORA, MLA_HEADS, MLA_NOPE_DIM + MLA_ROPE_DIM)
    q_nope, q_rope = wq[..., :MLA_NOPE_DIM], wq[..., MLA_NOPE_DIM:]
    tail = jnp.zeros(q_rope.shape[:-1] + (MLA_QK_PAD - MLA_NOPE_DIM - MLA_ROPE_DIM,), wq.dtype)
    wq_a = jnp.concatenate([q_nope, q_rope, tail], -1).reshape(depth, MLA_Q_LORA, -1)
    wq_b = jnp.concatenate([jnp.zeros_like(q_nope), _swap_halves(q_rope), tail], -1).reshape(depth, MLA_Q_LORA, -1)

    wkv = mla_wkv_up.reshape(depth, MLA_KV_LORA, MLA_HEADS, MLA_NOPE_DIM + MLA_V_DIM)
    k_nope, v_up = wkv[..., :MLA_NOPE_DIM], wkv[..., MLA_NOPE_DIM:]
    k_tail = jnp.zeros(k_nope.shape[:-1] + (MLA_QK_PAD - MLA_NOPE_DIM,), wkv.dtype)
    wk_a = jnp.concatenate([k_nope, k_tail], -1).reshape(depth, MLA_KV_LORA, -1)
    wv = v_up.reshape(depth, MLA_KV_LORA, -1)
    v_tail = jnp.zeros(v_up.shape[:-1] + (MLA_V_PAD - MLA_V_DIM,), wkv.dtype)
    wv_ext = jnp.concatenate([v_up, v_tail], -1).reshape(depth, MLA_KV_LORA, -1)
    vone = np.zeros((1, MLA_HEADS * MLA_V_PAD), np.float32)
    vone[0, MLA_V_DIM::MLA_V_PAD] = 1.0

    wr = jnp.pad(w_router, ((0, 0), (0, LANES - N_EXPERTS)))
    wr_hi = wr.astype(BF16)
    wr_lo = (wr - wr_hi.astype(F32)).astype(BF16)
    return {
        "w_in": w_main.astype(BF16), "wq_a": wq_a.astype(BF16), "wq_b": wq_b.astype(BF16),
        "wk_a": wk_a.astype(BF16), "wv": wv.astype(BF16), "wv_ext": wv_ext.astype(BF16),
        "vone_ext": jnp.asarray(vone),
        "w13": jnp.concatenate([w1, w3], -1).astype(BF16), "w2": w2.astype(BF16),
        "wr_cat": jnp.concatenate([wr_hi, wr_lo], axis=-1),
        "b_router": jnp.pad(b_router, (0, LANES - N_EXPERTS)).reshape(1, LANES).astype(F32),
    }


def _channel_dft():
    c = np.arange(FN_GROUP_DIM)
    ang = 2.0 * np.pi * ((c[:, None] * c[None, :]) % FN_GROUP_DIM) / FN_GROUP_DIM
    out = np.zeros((FN_WIDTH, 2 * FN_WIDTH), np.float32)
    for g in range(FN_GROUPS):
        sl = slice(g * FN_GROUP_DIM, (g + 1) * FN_GROUP_DIM)
        out[sl, sl] = np.cos(ang) * FN_GROUP_DIM ** -0.5
        out[sl, FN_WIDTH + g * FN_GROUP_DIM:FN_WIDTH + (g + 1) * FN_GROUP_DIM] = np.sin(ang) * FN_GROUP_DIM ** -0.5
    return jnp.asarray(out, BF16)


def _rope_tables(n):
    tok = jnp.arange(n)
    pos = jnp.stack([tok // GRID_W, tok % GRID_W], axis=-1).astype(F32)
    nf = MLA_ROPE_DIM // 4
    freqs = ROPE_THETA ** (-jnp.arange(nf, dtype=F32) / nf)
    ang = pos[:, :, None] * freqs
    cos = jnp.broadcast_to(jnp.cos(ang)[:, :, None, :], (n, 2, 2, nf)).reshape(n, MLA_ROPE_DIM)
    sin = jnp.sin(ang)
    sin = jnp.stack([-sin, sin], axis=2).reshape(n, MLA_ROPE_DIM)
    pad = jnp.zeros((n, MLA_QK_PAD - MLA_NOPE_DIM - MLA_ROPE_DIM), F32)
    cos_t = jnp.concatenate([jnp.ones((n, MLA_NOPE_DIM), F32), cos, pad], -1)
    sin_t = jnp.concatenate([jnp.zeros((n, MLA_NOPE_DIM), F32), sin, pad], -1)
    return cos_t, sin_t


def kernel(x_prompt, x_sample, cache_na_k, cache_na_v, cache_mla_ckv, cache_mla_krope, c, c_ctx, w_ada, b_ada,
           norm1, norm2, w_in, conv_w, na_rpb, mla_gq, mla_wq_up, mla_gkv, mla_wkv_up, w_fn, w_out, w_router,
           b_router, w1, w3, w2, norm_f):
    bp, seq, d = x_prompt.shape
    bd, dec_seq, _ = x_sample.shape
    depth = w_in.shape[0]

    wts = _pack_weights(w_in, mla_wq_up, mla_wkv_up, w1, w3, w2, w_router, b_router)
    wts.update({
        "norm1": norm1.reshape(depth, 1, d), "norm2": norm2.reshape(depth, 1, d),
        "mla_gq": mla_gq.reshape(depth, 1, -1), "mla_gkv": mla_gkv.reshape(depth, 1, -1),
        "conv_w": conv_w, "w_fn": w_fn.astype(BF16), "w_out": w_out.astype(BF16),
        "norm_f": norm_f.reshape(1, d), "cs_bd": _channel_dft(),
    })

    cond = jnp.concatenate([c, jnp.zeros((-(bd + 1) % 8, d), c.dtype), c_ctx[None, :]], axis=0)
    mod = _ada_modulation(cond, w_ada, b_ada)

    xp = x_prompt.reshape(bp * seq, d)
    tables = _dft_tables(seq)
    caches = [
        jnp.zeros((bp, depth, NA_HEADS, seq, NA_HEAD_DIM), F32), jnp.zeros((bp, depth, NA_HEADS, seq, NA_HEAD_DIM), F32),
        jnp.zeros((bp, depth, seq, MLA_KV_LORA), F32), jnp.zeros((bp, depth, seq, MLA_ROPE_DIM), F32)]
    pending = None
    for layer in range(depth):
        outs = _premix(xp, mod, layer, seq, wts, None, False, seq, caches, pending)
        ab, z, qn, kn, vn, km, fab, qm, vm = outs[:9]
        caches = outs[9:13]
        if pending is not None:
            xp = outs[13]
        yna, ymla = _ctx_attention(qn, kn, vn, qm, km, vm, seq)
        g = _fourier(fab, tables, seq)
        xp, pending = _moe_layer(xp, mod, layer, seq, (ab, z, yna, ymla, g), wts, False, seq, MOE_ROW_BLOCK)
    xp = _combine(xp, pending[0], pending[1], mod, depth - 1, seq, wts["norm_f"], True, False, seq)
    new_na_k, new_na_v, new_ckv, new_krope = caches

    xs = x_sample.reshape(bd * dec_seq, d)
    kx, vxt = _ctx_kv(cache_mla_ckv, cache_mla_krope, wts["wk_a"], wts["wv_ext"], wts["vone_ext"])
    na_bias = _na_bias(na_rpb, dec_seq // GRID_W)
    rope = _rope_tables(dec_seq)
    tables = _dft_tables(dec_seq)
    pending = None
    for layer in range(depth):
        outs = _premix(xs, mod, layer, dec_seq, wts, rope, True, TM_LAT_PREMIX, None, pending)
        ab, z, qn, kn, vn, km, fab, qt, vt = outs[:9]
        if pending is not None:
            xs = outs[9]
        yna = _na_lat_attention(qn, kn, vn, cache_na_k, cache_na_v, na_bias, layer, dec_seq)
        ymla = _mla_lat_attention(qt, km, vt, kx, vxt, layer, dec_seq)
        g = _fourier(fab, tables, dec_seq)
        xs, pending = _moe_layer(xs, mod, layer, dec_seq, (ab, z, yna, ymla, g), wts, True, TM_LAT_MIXOUT,
                                 MOE_ROW_BLOCK)
    xs = _combine(xs, pending[0], pending[1], mod, depth - 1, dec_seq, wts["norm_f"], True, True, TM_LAT_PREMIX)

    return (xp.reshape(bp, seq, d), xs.reshape(bd, dec_seq, d), new_na_k, new_na_v, new_ckv, new_krope)
```

```python
import functools
import math

import numpy as np
import jax
import jax.numpy as jnp
from jax import lax
from jax.experimental import pallas as pl
from jax.experimental.pallas import tpu as pltpu
from jax.experimental.pallas import tpu_sc as plsc

F32 = jnp.float32
BF16 = jnp.bfloat16

GRID_W = 64
CONV_WIDTH = 256
NA_HEADS = 4
NA_HEAD_DIM = 64
NA_WIDTH = NA_HEADS * NA_HEAD_DIM
NA_KH = 8
NA_KW = 16
MLA_HEADS = 4
MLA_Q_LORA = 256
MLA_KV_LORA = 128
MLA_NOPE_DIM = 64
MLA_ROPE_DIM = 32
MLA_V_DIM = 64
MLA_QK_PAD = 128
MLA_V_PAD = 96
MLA_KEY_SUB = 256
LOG2E = 1.4426950408889634
FN_GROUPS = 4
FN_GROUP_DIM = 64
FN_WIDTH = FN_GROUPS * FN_GROUP_DIM
N_EXPERTS = 16
N_EXPERT_GROUPS = 4
EXPERTS_PER_GROUP = N_EXPERTS // N_EXPERT_GROUPS
EXPERT_FF = 256
ROPE_THETA = 10000.0
EPS = 1e-6
NEG_INF = -1e30
LANES = 128

NA_SCALE = NA_HEAD_DIM ** -0.5
MLA_SCALE = (MLA_NOPE_DIM + MLA_ROPE_DIM) ** -0.5

NA_Q_ROWS = 4
NA_WIN_ROWS = 12

CTX_SEQS_PER_TILE = 2
TM_LAT_PREMIX = 512
TM_LAT_MIXOUT = 512
MOE_ROW_BLOCK = 512
SC_WINDOW = 128
MOE_PIECES = 2

V7X_VMEM_BYTES = 64 * 1024 * 1024
VMEM_LIMIT = V7X_VMEM_BYTES - 8 * 1024 * 1024

_C_AB, _C_AC, _C_AU, _C_Q, _C_K, _C_V, _C_CQ = 0, 256, 512, 768, 1024, 1280, 1536
_C_CKV, _C_KR, _C_KRS, _C_FU, _C_END = 1792, 1920, 2048, 2176, 2432


def _nt_dot(a, b):
    return lax.dot_general(a, b, (((1,), (1,)), ((), ())), preferred_element_type=F32)


def _dot(a, b):
    return jnp.dot(a, b, preferred_element_type=F32)


def _rms(x, g):
    return x * lax.rsqrt(jnp.mean(x * x, axis=-1, keepdims=True) + EPS) * g


def _params(*sem, flags=None):
    return pltpu.CompilerParams(dimension_semantics=sem, vmem_limit_bytes=VMEM_LIMIT, flags=flags)


def _ada_kernel(c_ref, w_ref, b_ref, o_ref):
    cnd = c_ref[...]
    act = cnd * jax.nn.sigmoid(cnd)
    o_ref[...] = _dot(act.astype(BF16), w_ref[...].astype(BF16)) + b_ref[...]


def _ada_modulation(cond, w_ada, b_ada):
    depth, d, six_d = w_ada.shape
    r = cond.shape[0]
    tn = 1024
    out = pl.pallas_call(
        _ada_kernel,
        grid=(depth, six_d // tn),
        in_specs=[
            pl.BlockSpec((r, d), lambda l, j: (0, 0)),
            pl.BlockSpec((None, d, tn), lambda l, j: (l, 0, j)),
            pl.BlockSpec((None, 1, tn), lambda l, j: (l, 0, j)),
        ],
        out_specs=pl.BlockSpec((None, r, tn), lambda l, j: (l, 0, j)),
        out_shape=jax.ShapeDtypeStruct((depth, r, six_d), F32),
        compiler_params=_params("parallel", "parallel"),
        name="ada_modulation",
    )(cond, w_ada, b_ada.reshape(depth, 1, six_d))
    return out.reshape(depth, r, 6, d)


def _unpack_pairs(p):
    hi = pltpu.bitcast(p & jnp.uint32(0xFFFF0000), F32)
    lo = pltpu.bitcast(p << 16, F32)
    return jnp.concatenate([hi, lo], axis=-1)


def _moe_residual(x1, gtok_ref, y_ref, mod_ref):
    g = gtok_ref[...]
    y_lo = _unpack_pairs(jnp.concatenate([y_ref[0, 0], y_ref[1, 0]], axis=-1))
    y_hi = _unpack_pairs(jnp.concatenate([y_ref[0, 1], y_ref[1, 1]], axis=-1))
    return x1 + mod_ref[...][5:6] * (g[:, 0:1] * y_lo + g[:, 1:2] * y_hi)


def _premix_kernel(lat, fused, n_in, *refs):
    (x_ref, mod_ref, g1_ref, w_ref, gq_ref, wqa_ref, wqb_ref, gkv_ref, wka_ref, wv_ref, vone_ref, cs_ref,
     cos_ref, sin_ref) = refs[:14]
    outs = refs[n_in:]
    (ab_ref, z_ref, qn_ref, kn_ref, vn_ref, km_ref, fab_ref) = outs[:7]

    x = x_ref[...]
    if fused:
        gtok_ref, y_ref, modp_ref = refs[n_in - 3:n_in]
        x = _moe_residual(x, gtok_ref, y_ref, modp_ref)
        outs[-1][...] = x
    mod = mod_ref[...]
    h = _rms(x, g1_ref[...]) * (1.0 + mod[1:2]) + mod[0:1]
    p = _dot(h.astype(BF16), w_ref[...])

    ab_ref[...] = p[:, _C_AB:_C_AC].astype(BF16)
    z_ref[...] = (p[:, _C_AC:_C_AU] * p[:, _C_AU:_C_Q]).astype(BF16)
    k_na = p[:, _C_K:_C_V]
    v_na = p[:, _C_V:_C_CQ]
    qn_ref[...] = (p[:, _C_Q:_C_K] * NA_SCALE).astype(BF16)
    kn_ref[...] = k_na.astype(BF16)
    vn_ref[...] = v_na.astype(BF16)

    cqn = _rms(p[:, _C_CQ:_C_CKV], gq_ref[...]).astype(BF16)
    ckvn = _rms(p[:, _C_CKV:_C_KR], gkv_ref[...])
    ckvn_b = ckvn.astype(BF16)
    qa = _dot(cqn, wqa_ref[...])
    kva = _dot(ckvn_b, wka_ref[...])
    v_mla = _dot(ckvn_b, wv_ref[...]) + vone_ref[...]
    kr = p[:, _C_KR:_C_KRS]
    if lat:
        cos = cos_ref[...]
        sin = sin_ref[...]
        qb = _dot(cqn, wqb_ref[...])
        krot = kr * cos + p[:, _C_KRS:_C_FU] * sin
        qt_ref, vt_ref = outs[7:9]
    else:
        krot = kr
        qm_ref, vm_ref, ck_ref, cv_ref, cckv_ref, ckr_ref = outs[7:13]
    for hd in range(MLA_HEADS):
        sl = slice(hd * MLA_QK_PAD, (hd + 1) * MLA_QK_PAD)
        km_ref[:, sl] = (kva[:, sl] + krot).astype(BF16)
        if lat:
            qh = (qa[:, sl] * cos + qb[:, sl] * sin) * (MLA_SCALE * LOG2E)
            qt_ref[sl, :] = jnp.transpose(qh).astype(BF16)
        else:
            qm_ref[:, sl] = (qa[:, sl] * MLA_SCALE).astype(BF16)

    fab_ref[...] = _dot(p[:, _C_FU:_C_END].astype(BF16), cs_ref[...]).astype(BF16)

    if lat:
        for j in range(v_mla.shape[1] // LANES):
            sl = slice(j * LANES, (j + 1) * LANES)
            vt_ref[sl, :] = jnp.transpose(v_mla[:, sl]).astype(BF16)
    else:
        vm_ref[...] = v_mla.astype(BF16)
        nseq, n = cckv_ref.shape[0], cckv_ref.shape[1]
        for s in range(nseq):
            rows = slice(s * n, (s + 1) * n)
            for hd in range(NA_HEADS):
                sl = slice(hd * NA_HEAD_DIM, (hd + 1) * NA_HEAD_DIM)
                ck_ref[s, hd] = k_na[rows, sl]
                cv_ref[s, hd] = v_na[rows, sl]
            cckv_ref[s] = ckvn[rows]
            ckr_ref[s] = kr[rows, MLA_NOPE_DIM:MLA_NOPE_DIM + MLA_ROPE_DIM]


def _cond_row(lat, tm, n, ctx_row):
    return (lambda i: (i * tm) // n) if lat else (lambda i: ctx_row)


def _premix(x, mod, layer, n, wts, rope, lat, tm, caches=None, pending=None):
    t, d = x.shape
    if lat:
        cos_t, sin_t = rope
        wv, vone = wts["wv_ext"], wts["vone_ext"]
    else:
        cos_t = sin_t = jnp.zeros((8, LANES), F32)
        wv, vone = wts["wv"], jnp.zeros((1, MLA_HEADS * MLA_V_DIM), F32)
    vw = wv.shape[-1]
    qw = MLA_HEADS * MLA_QK_PAD
    tiles_per_seq = n // tm
    cond_row = _cond_row(lat, tm, n, mod.shape[1] - 1)
    const = lambda *_: (0, 0)
    lsel = lambda *_: (layer, 0, 0)
    rope_spec = (pl.BlockSpec((tm, LANES), lambda i: (i % tiles_per_seq, 0)) if lat
                 else pl.BlockSpec((8, LANES), const))
    in_specs = [
        pl.BlockSpec((tm, d), lambda i: (i, 0)),
        pl.BlockSpec((None, None, 6, d), lambda i: (layer, cond_row(i), 0, 0)),
        pl.BlockSpec((None, 1, d), lsel),
        pl.BlockSpec((None, d, _C_END), lsel),
        pl.BlockSpec((None, 1, MLA_Q_LORA), lsel),
        pl.BlockSpec((None, MLA_Q_LORA, qw), lsel),
        pl.BlockSpec((None, MLA_Q_LORA, qw), lsel),
        pl.BlockSpec((None, 1, MLA_KV_LORA), lsel),
        pl.BlockSpec((None, MLA_KV_LORA, qw), lsel),
        pl.BlockSpec((None, MLA_KV_LORA, vw), lsel),
        pl.BlockSpec((1, vw), const),
        pl.BlockSpec((FN_WIDTH, 2 * FN_WIDTH), const),
        rope_spec,
        rope_spec,
    ]
    row = lambda w: pl.BlockSpec((tm, w), lambda i: (i, 0))
    widths = [CONV_WIDTH, CONV_WIDTH, NA_WIDTH, NA_WIDTH, NA_WIDTH, qw, 2 * FN_WIDTH]
    out_specs = [row(w) for w in widths]
    out_shape = [jax.ShapeDtypeStruct((t, w), BF16) for w in widths]
    if lat:
        out_specs += [pl.BlockSpec((None, qw, tm), lambda i: (i, 0, 0)),
                      pl.BlockSpec((None, vw, tm), lambda i: (i, 0, 0))]
        out_shape += [jax.ShapeDtypeStruct((t // tm, qw, tm), BF16),
                      jax.ShapeDtypeStruct((t // tm, vw, tm), BF16)]
    else:
        assert tm % n == 0
        b = t // n
        spt = tm // n
        depth = wts["w_in"].shape[0]
        out_specs += [
            row(qw), row(vw),
            pl.BlockSpec((spt, None, NA_HEADS, n, NA_HEAD_DIM), lambda i: (i, layer, 0, 0, 0)),
            pl.BlockSpec((spt, None, NA_HEADS, n, NA_HEAD_DIM), lambda i: (i, layer, 0, 0, 0)),
            pl.BlockSpec((spt, None, n, MLA_KV_LORA), lambda i: (i, layer, 0, 0)),
            pl.BlockSpec((spt, None, n, MLA_ROPE_DIM), lambda i: (i, layer, 0, 0)),
        ]
        out_shape += [
            jax.ShapeDtypeStruct((t, qw), BF16), jax.ShapeDtypeStruct((t, vw), BF16),
            jax.ShapeDtypeStruct((b, depth, NA_HEADS, n, NA_HEAD_DIM), F32),
            jax.ShapeDtypeStruct((b, depth, NA_HEADS, n, NA_HEAD_DIM), F32),
            jax.ShapeDtypeStruct((b, depth, n, MLA_KV_LORA), F32),
            jax.ShapeDtypeStruct((b, depth, n, MLA_ROPE_DIM), F32),
        ]
    args = [x, mod, wts["norm1"], wts["w_in"], wts["mla_gq"], wts["wq_a"], wts["wq_b"], wts["mla_gkv"],
            wts["wk_a"], wv, vone, wts["cs_bd"], cos_t, sin_t]
    aliases = {}
    if caches is not None:
        first_cache_out = len(out_shape) - len(caches)
        aliases = {len(args) + j: first_cache_out + j for j in range(len(caches))}
        in_specs += [pl.BlockSpec(memory_space=pl.ANY)] * len(caches)
        args += list(caches)
    if pending is not None:
        gtok, y_tok = pending
        in_specs += [
            row(LANES),
            pl.BlockSpec(y_tok.shape[:2] + (tm, y_tok.shape[3]), lambda i: (0, 0, i, 0)),
            pl.BlockSpec((None, None, 6, d), lambda i: (layer - 1, cond_row(i), 0, 0)),
        ]
        args += [gtok, y_tok, mod]
        out_specs = out_specs + [row(d)]
        out_shape = out_shape + [jax.ShapeDtypeStruct((t, d), F32)]
    return pl.pallas_call(
        functools.partial(_premix_kernel, lat, pending is not None, len(args)),
        grid=(t // tm,),
        in_specs=in_specs,
        out_specs=out_specs,
        out_shape=out_shape,
        input_output_aliases=aliases,
        compiler_params=_params("parallel"),
        name="premix_lat" if lat else "premix_ctx",
    )(*args)


def _softmax_attend(q, k, v):
    s = _nt_dot(q, k)
    m = jnp.max(s, axis=-1, keepdims=True)
    p = jnp.exp(s - m)
    l = jnp.sum(p, axis=-1, keepdims=True)
    return _dot(p.astype(BF16), v) / l


def _ctx_attn_kernel(qn_ref, kn_ref, vn_ref, qm_ref, km_ref, vm_ref, yna_ref, ymla_ref):
    for hd in range(NA_HEADS):
        sl = slice(hd * NA_HEAD_DIM, (hd + 1) * NA_HEAD_DIM)
        yna_ref[:, sl] = _softmax_attend(qn_ref[:, sl], kn_ref[:, sl], vn_ref[:, sl]).astype(BF16)
    for hd in range(MLA_HEADS):
        sq = slice(hd * MLA_QK_PAD, (hd + 1) * MLA_QK_PAD)
        sv = slice(hd * MLA_V_DIM, (hd + 1) * MLA_V_DIM)
        ymla_ref[:, sv] = _softmax_attend(qm_ref[:, sq], km_ref[:, sq], vm_ref[:, sv]).astype(BF16)


def _ctx_attention(qn, kn, vn, qm, km, vm, n):
    t = qn.shape[0]
    spec = lambda w: pl.BlockSpec((n, w), lambda b: (b, 0))
    ins = [qn, kn, vn, qm, km, vm]
    return pl.pallas_call(
        _ctx_attn_kernel,
        grid=(t // n,),
        in_specs=[spec(a.shape[1]) for a in ins],
        out_specs=[spec(NA_WIDTH), spec(MLA_HEADS * MLA_V_DIM)],
        out_shape=[jax.ShapeDtypeStruct((t, NA_WIDTH), BF16),
                   jax.ShapeDtypeStruct((t, MLA_HEADS * MLA_V_DIM), BF16)],
        compiler_params=_params("parallel"),
        name="ctx_attention",
    )(*ins)


def _mla_lat_kernel(qt_ref, k_ref, vt_ref, kx_ref, vxt_ref, o_ref, s_scr, p_scr):
    nchunk, _, kc = vt_ref.shape
    tq = qt_ref.shape[1]
    sub = MLA_KEY_SUB

    ksl = lambda hd: slice(hd * MLA_QK_PAD, (hd + 1) * MLA_QK_PAD)
    vsl = lambda hd: slice(hd * MLA_V_PAD, (hd + 1) * MLA_V_PAD)

    def scores(slot, k_of, nk):
        cmax = []
        for hd in range(MLA_HEADS):
            qt = qt_ref[ksl(hd), :]
            part = None
            for j in range(0, nk, sub):
                st = _dot(k_of(hd, j), qt)
                s_scr[slot, hd, j:j + sub, :] = st
                blk = jnp.max(st.reshape(sub // 8, 8, tq), axis=0)
                part = blk if part is None else jnp.maximum(part, blk)
            cmax.append(jnp.max(part, axis=0, keepdims=True))
        return tuple(cmax)

    def attend(slot, cmax, state, vt_of, nk):
        new = []
        for hd in range(MLA_HEADS):
            m_i, acc = state[hd]
            m_new = jnp.maximum(m_i, cmax[hd])
            for j in range(0, nk, sub):
                p_scr[hd, j:j + sub, :] = jnp.exp2(s_scr[slot, hd, j:j + sub, :] - m_new).astype(BF16)
            acc = jnp.exp2(m_i - m_new) * acc + _dot(vt_of(hd), p_scr[hd, 0:nk, :])
            new.append((m_new, acc))
        return tuple(new)

    lat_keys = lambda c: (lambda hd, j: k_ref[pl.ds(pl.multiple_of(c * kc, kc) + j, sub), ksl(hd)])
    past = kx_ref.shape[0]
    state = tuple((jnp.full((1, tq), NEG_INF, F32), jnp.zeros((MLA_V_PAD, tq), F32)) for _ in range(MLA_HEADS))
    cmax_ctx = scores(1, lambda hd, j: kx_ref[j:j + sub, ksl(hd)], past)
    cmax = scores(0, lat_keys(0), kc)
    state = attend(1, cmax_ctx, state, lambda hd: vxt_ref[vsl(hd), :], past)

    lat_vals = lambda c: (lambda hd: vt_ref[c, vsl(hd), :])

    def body(i, carry):
        cmax0, state = carry
        c = 2 * i
        cmax1 = scores(1, lat_keys(c + 1), kc)
        state = attend(0, cmax0, state, lat_vals(c), kc)
        cmax0 = scores(0, lat_keys(c + 2), kc)
        state = attend(1, cmax1, state, lat_vals(c + 1), kc)
        return cmax0, state

    cmax, state = lax.fori_loop(0, nchunk // 2 - 1, body, (cmax, state))
    cmax1 = scores(1, lat_keys(nchunk - 1), kc)
    state = attend(0, cmax, state, lat_vals(nchunk - 2), kc)
    state = attend(1, cmax1, state, lat_vals(nchunk - 1), kc)
    o_t = jnp.concatenate([acc[:MLA_V_DIM] / acc[MLA_V_DIM:MLA_V_DIM + 1] for _, acc in state], axis=0)
    o_ref[...] = jnp.transpose(o_t).astype(BF16)


def _mla_lat_attention(qt, km, vt, kx, vxt, layer, n):
    ntile, qw, tq = qt.shape
    t = ntile * tq
    past = kx.shape[2]
    qpb = n // tq
    return pl.pallas_call(
        _mla_lat_kernel,
        grid=(t // n, qpb),
        in_specs=[
            pl.BlockSpec((None, qw, tq), lambda b, i: (b * qpb + i, 0, 0)),
            pl.BlockSpec((n, km.shape[1]), lambda b, i: (b, 0)),
            pl.BlockSpec((qpb, vt.shape[1], tq), lambda b, i: (b, 0, 0)),
            pl.BlockSpec((None, None, past, kx.shape[3]), lambda b, i: (layer, b, 0, 0)),
            pl.BlockSpec((None, None, vxt.shape[2], past), lambda b, i: (layer, b, 0, 0)),
        ],
        out_specs=pl.BlockSpec((tq, MLA_HEADS * MLA_V_DIM), lambda b, i: (b * qpb + i, 0)),
        out_shape=jax.ShapeDtypeStruct((t, MLA_HEADS * MLA_V_DIM), BF16),
        scratch_shapes=[pltpu.VMEM((2, MLA_HEADS, max(tq, past), tq), F32),
                        pltpu.VMEM((MLA_HEADS, max(tq, past), tq), BF16)],
        compiler_params=_params("parallel", "parallel"),
        name="mla_lat_attention",
    )(qt, km, vt, kx, vxt)


def _ctx_kv_kernel(ckv_ref, kr_ref, wka_ref, wv_ref, vone_ref, place_ref, k_ref, vt_ref):
    ckv = ckv_ref[...].astype(BF16)
    k_ref[...] = (_dot(ckv, wka_ref[...]) + _dot(kr_ref[...].astype(BF16), place_ref[...])).astype(BF16)
    v = _dot(ckv, wv_ref[...]) + vone_ref[...]
    for j in range(v.shape[1] // LANES):
        sl = slice(j * LANES, (j + 1) * LANES)
        vt_ref[sl, :] = jnp.transpose(v[:, sl]).astype(BF16)


def _ctx_kv(cache_ckv, cache_krope, wk_a, wv_ext, vone_ext):
    bd, depth, past, _ = cache_ckv.shape
    place = np.zeros((MLA_ROPE_DIM, MLA_HEADS * MLA_QK_PAD), np.float32)
    for hd in range(MLA_HEADS):
        for i in range(MLA_ROPE_DIM):
            place[i, hd * MLA_QK_PAD + MLA_NOPE_DIM + i] = 1.0
    kw, vw = MLA_HEADS * MLA_QK_PAD, MLA_HEADS * MLA_V_PAD
    return pl.pallas_call(
        _ctx_kv_kernel,
        grid=(depth, bd),
        in_specs=[
            pl.BlockSpec((None, None, past, MLA_KV_LORA), lambda l, b: (b, l, 0, 0)),
            pl.BlockSpec((None, None, past, MLA_ROPE_DIM), lambda l, b: (b, l, 0, 0)),
            pl.BlockSpec((None, MLA_KV_LORA, kw), lambda l, b: (l, 0, 0)),
            pl.BlockSpec((None, MLA_KV_LORA, vw), lambda l, b: (l, 0, 0)),
            pl.BlockSpec((1, vw), lambda l, b: (0, 0)),
            pl.BlockSpec((MLA_ROPE_DIM, kw), lambda l, b: (0, 0)),
        ],
        out_specs=[pl.BlockSpec((None, None, past, kw), lambda l, b: (l, b, 0, 0)),
                   pl.BlockSpec((None, None, vw, past), lambda l, b: (l, b, 0, 0))],
        out_shape=[jax.ShapeDtypeStruct((depth, bd, past, kw), BF16),
                   jax.ShapeDtypeStruct((depth, bd, vw, past), BF16)],
        compiler_params=_params("parallel", "parallel"),
        name="ctx_kv",
    )(cache_ckv, cache_krope, wk_a, wv_ext, vone_ext, jnp.asarray(place, BF16))


def _na_tile_geometry(rows):
    last = rows // NA_Q_ROWS - 1
    geo = []
    for j in (0, 1, last):
        r0 = j * NA_Q_ROWS
        geo.append((r0, min(max(r0 - NA_KH // 2, 0), rows - NA_WIN_ROWS)))
    return geo


def _na_bias_kernel(geo, rows, rpb_ref, o_ref):
    l = pl.program_id(0)
    hd = pl.program_id(1)
    base = (l * NA_HEADS + hd) * (2 * NA_KH - 1) * (2 * NA_KW - 1)
    qc = lax.broadcasted_iota(jnp.int32, (GRID_W, GRID_W), 0)
    kcol = lax.broadcasted_iota(jnp.int32, (GRID_W, GRID_W), 1)
    d_col = jnp.clip(kcol - qc + (NA_KW - 1), 0, 2 * NA_KW - 2)
    col_start = jnp.clip(qc - NA_KW // 2, 0, GRID_W - NA_KW)
    in_cols = (kcol >= col_start) & (kcol < col_start + NA_KW)
    neg = jnp.full((GRID_W, GRID_W), NEG_INF, F32)
    tabs = []
    for dr in range(2 * NA_KH - 1):
        acc = jnp.zeros((GRID_W, GRID_W), F32)
        for dc in range(2 * NA_KW - 1):
            acc = jnp.where(d_col == dc, rpb_ref[base + dr * (2 * NA_KW - 1) + dc], acc)
        tabs.append(jnp.where(in_cols, acc, neg))
    for kind, (r0, ws) in enumerate(geo):
        for i in range(NA_Q_ROWS):
            r = r0 + i
            lo = min(max(r - NA_KH // 2, 0), rows - NA_KH)
            for j in range(NA_WIN_ROWS):
                kr = ws + j
                blk = tabs[kr - r + NA_KH - 1] if lo <= kr < lo + NA_KH else neg
                o_ref[kind, i * GRID_W:(i + 1) * GRID_W, j * GRID_W:(j + 1) * GRID_W] = blk


def _na_bias(na_rpb, rows):
    depth = na_rpb.shape[0]
    geo = _na_tile_geometry(rows)
    qn, kn = NA_Q_ROWS * GRID_W, NA_WIN_ROWS * GRID_W
    return pl.pallas_call(
        functools.partial(_na_bias_kernel, geo, rows),
        grid=(depth, NA_HEADS),
        in_specs=[pl.BlockSpec(memory_space=pltpu.SMEM)],
        out_specs=pl.BlockSpec((None, None, 3, qn, kn), lambda l, h: (l, h, 0, 0, 0)),
        out_shape=jax.ShapeDtypeStruct((depth, NA_HEADS, 3, qn, kn), F32),
        compiler_params=_params("parallel", "parallel"),
        name="na_bias",
    )(na_rpb.reshape(-1))


def _na_lat_kernel(rows, q_ref, k_ref, v_ref, kx_ref, vx_ref, bias_ref, o_ref, s_scr, p_scr):
    j = pl.program_id(1)
    ws = jnp.clip(j * NA_Q_ROWS - NA_KH // 2, 0, rows - NA_WIN_ROWS)
    start = pl.multiple_of(ws * GRID_W, GRID_W)
    nk = NA_WIN_ROWS * GRID_W
    heads = [slice(hd * NA_HEAD_DIM, (hd + 1) * NA_HEAD_DIM) for hd in range(NA_HEADS)]
    m = []
    for hd, sl in enumerate(heads):
        q = q_ref[:, sl]
        s_win = _nt_dot(q, k_ref[pl.ds(start, nk), sl]) + bias_ref[hd]
        s_ctx = _nt_dot(q, kx_ref[hd].astype(BF16))
        s_scr[hd, :, :nk] = s_win
        s_scr[hd, :, nk:] = s_ctx
        m.append(jnp.maximum(jnp.max(s_win, axis=-1, keepdims=True), jnp.max(s_ctx, axis=-1, keepdims=True)))
    l = []
    for hd in range(NA_HEADS):
        p = jnp.exp(s_scr[hd] - m[hd])
        l.append(jnp.sum(p, axis=-1, keepdims=True))
        p_scr[hd] = p.astype(BF16)
    for hd, sl in enumerate(heads):
        o = _dot(p_scr[hd, :, :nk], v_ref[pl.ds(start, nk), sl]) + _dot(p_scr[hd, :, nk:], vx_ref[hd].astype(BF16))
        o_ref[:, sl] = (o / l[hd]).astype(BF16)


def _na_lat_attention(qn, kn, vn, cache_k, cache_v, bias, layer, n):
    t = qn.shape[0]
    rows = n // GRID_W
    assert rows % NA_Q_ROWS == 0 and rows >= NA_WIN_ROWS + NA_Q_ROWS
    tiles = rows // NA_Q_ROWS
    tq = NA_Q_ROWS * GRID_W
    past = cache_k.shape[3]

    def kind(b, j):
        return (layer, 0, jnp.where(j == 0, 0, jnp.where(j == tiles - 1, 2, 1)), 0, 0)

    return pl.pallas_call(
        functools.partial(_na_lat_kernel, rows),
        grid=(t // n, tiles),
        in_specs=[
            pl.BlockSpec((tq, NA_WIDTH), lambda b, j: (b * tiles + j, 0)),
            pl.BlockSpec((n, NA_WIDTH), lambda b, j: (b, 0)),
            pl.BlockSpec((n, NA_WIDTH), lambda b, j: (b, 0)),
            pl.BlockSpec((None, None, NA_HEADS, past, NA_HEAD_DIM), lambda b, j: (b, layer, 0, 0, 0)),
            pl.BlockSpec((None, None, NA_HEADS, past, NA_HEAD_DIM), lambda b, j: (b, layer, 0, 0, 0)),
            pl.BlockSpec((None, NA_HEADS, None, tq, NA_WIN_ROWS * GRID_W), kind),
        ],
        out_specs=pl.BlockSpec((tq, NA_WIDTH), lambda b, j: (b * tiles + j, 0)),
        out_shape=jax.ShapeDtypeStruct((t, NA_WIDTH), BF16),
        scratch_shapes=[pltpu.VMEM((NA_HEADS, tq, NA_WIN_ROWS * GRID_W + past), F32),
                        pltpu.VMEM((NA_HEADS, tq, NA_WIN_ROWS * GRID_W + past), BF16)],
        compiler_params=_params("parallel", "parallel"),
        name="na_lat_attention",
    )(qn, kn, vn, cache_k, cache_v, bias)


def _dft_tables(n):
    def thin(j, k, period):
        ang = (2.0 * math.pi / period) * ((j[:, None] * k[None, :]) % period).astype(F32)
        return jnp.cos(ang), jnp.sin(ang)

    k = jnp.arange(n, dtype=jnp.int32)
    scale = float(n) ** -0.5
    if n % 64 == 0 and n > 64:
        n1 = n // 64
        c1, s1 = thin(jnp.arange(n1, dtype=jnp.int32), k, n1)
        c2, s2 = thin(jnp.arange(64, dtype=jnp.int32), k, n)
        c1, s1, c2, s2 = c1[:, None, :], s1[:, None, :], c2[None, :, :], s2[None, :, :]
        cm = (c1 * c2 - s1 * s2).reshape(n, n)
        sm = (s1 * c2 + c1 * s2).reshape(n, n)
    else:
        cm, sm = thin(k, k, n)
    return (cm * scale).astype(BF16), (sm * -scale).astype(BF16)


def _fourier_kernel(c_ref, s_ref, ab_ref, o_ref):
    o_ref[...] = (_dot(c_ref[...], ab_ref[:, :FN_WIDTH]) + _dot(s_ref[...], ab_ref[:, FN_WIDTH:])).astype(BF16)


def _fourier_half_kernel(c_ref, s_ref, cmid_ref, ab_ref, plus_ref, minus_ref, mid_ref):
    a = ab_ref[:, :FN_WIDTH]
    p = _dot(c_ref[...], a)
    q = _dot(s_ref[...], ab_ref[:, FN_WIDTH:])
    plus_ref[...] = (p + q).astype(BF16)
    minus_ref[...] = (p - q).astype(BF16)
    mid_ref[...] = _dot(cmid_ref[...], a).astype(BF16)


def _fourier_half(fab, tables, n, tmf=512):
    t = fab.shape[0]
    b = t // n
    half = n // 2
    tiles = half // tmf
    cm, sm = tables
    cmid = jnp.broadcast_to(cm[half:half + 1], (8, n))
    row_out = lambda: pl.BlockSpec((tmf, FN_WIDTH), lambda i, bb: (bb * tiles + i, 0))
    plus, minus, mid = pl.pallas_call(
        _fourier_half_kernel,
        grid=(tiles, b),
        in_specs=[
            pl.BlockSpec((tmf, n), lambda i, bb: (i, 0)),
            pl.BlockSpec((tmf, n), lambda i, bb: (i, 0)),
            pl.BlockSpec((8, n), lambda i, bb: (0, 0)),
            pl.BlockSpec((n, 2 * FN_WIDTH), lambda i, bb: (bb, 0)),
        ],
        out_specs=[row_out(), row_out(), pl.BlockSpec((None, None, 8, FN_WIDTH), lambda i, bb: (i, bb, 0, 0))],
        out_shape=[jax.ShapeDtypeStruct((b * half, FN_WIDTH), BF16), jax.ShapeDtypeStruct((b * half, FN_WIDTH), BF16),
                   jax.ShapeDtypeStruct((tiles, b, 8, FN_WIDTH), BF16)],
        compiler_params=_params("parallel", "parallel"),
        name="fourier_half",
    )(cm, sm, cmid, fab)
    return plus, minus, mid


def _fourier(fab, tables, n, tmf=512):
    if n >= 4 * tmf and tmf == TM_LAT_MIXOUT:
        return _fourier_half(fab, tables, n, tmf)
    t = fab.shape[0]
    tmf = min(tmf, n)
    tiles = n // tmf
    cm, sm = tables
    return pl.pallas_call(
        _fourier_kernel,
        grid=(tiles, t // n),
        in_specs=[
            pl.BlockSpec((tmf, n), lambda i, b: (i, 0)),
            pl.BlockSpec((tmf, n), lambda i, b: (i, 0)),
            pl.BlockSpec((n, 2 * FN_WIDTH), lambda i, b: (b, 0)),
        ],
        out_specs=pl.BlockSpec((tmf, FN_WIDTH), lambda i, b: (b * tiles + i, 0)),
        out_shape=jax.ShapeDtypeStruct((t, FN_WIDTH), BF16),
        compiler_params=_params("parallel", "parallel"),
        name="fourier",
    )(cm, sm, fab)


def _route(s_t, sb_t):
    def top2_sum(v):
        hi1, lo1 = jnp.maximum(v[0], v[1]), jnp.minimum(v[0], v[1])
        hi2, lo2 = jnp.maximum(v[2], v[3]), jnp.minimum(v[2], v[3])
        return jnp.maximum(hi1, hi2) + jnp.maximum(jnp.minimum(hi1, hi2), jnp.maximum(lo1, lo2))

    best = top2_sum(sb_t[0:EXPERTS_PER_GROUP])
    gsel = jnp.zeros_like(best, dtype=jnp.int32)
    for g in range(1, N_EXPERT_GROUPS):
        cand = top2_sum(sb_t[g * EXPERTS_PER_GROUP:(g + 1) * EXPERTS_PER_GROUP])
        better = cand > best
        gsel = jnp.where(better, g, gsel)
        best = jnp.where(better, cand, best)
    chosen = []
    for e in range(N_EXPERTS):
        g = e // EXPERTS_PER_GROUP
        beaten = jnp.zeros_like(gsel)
        for o in range(g * EXPERTS_PER_GROUP, (g + 1) * EXPERTS_PER_GROUP):
            if o == e:
                continue
            ahead = (sb_t[o] > sb_t[e]) | ((sb_t[o] == sb_t[e]) & (o < e))
            beaten = beaten + ahead.astype(jnp.int32)
        chosen.append((gsel == g) & (beaten < 2))
    picked = [jnp.where(chosen[e], s_t[e], 0.0) for e in range(N_EXPERTS)]
    denom = picked[0]
    for e in range(1, N_EXPERTS):
        denom = denom + picked[e]
    return chosen, [pk / denom for pk in picked]


def _pack_pairs(x):
    w = x.shape[1] // 2
    hi = pltpu.bitcast(x[:, :w].astype(BF16).astype(F32), jnp.uint32)
    lo = pltpu.bitcast(x[:, w:].astype(BF16).astype(F32), jnp.uint32)
    return hi | (lo >> 16)


def _half_spectrum_tile(i, tiles_per_seq, plus_ref, minus_ref, edge_ref, mid_ref, rev_ref):
    j = i % tiles_per_seq - tiles_per_seq // 2
    body = _dot(rev_ref[...], minus_ref[...]).astype(BF16)
    first = jnp.where(j == 0, mid_ref[0:1, :], edge_ref[0:1, :])
    rows = lax.broadcasted_iota(jnp.int32, body.shape, 0)
    upper = jnp.where(rows == 0, first, body)
    return jnp.where(j < 0, plus_ref[...], upper)


def _mixout_kernel(n, half, *refs):
    x_ref, mod_ref, ab_ref, z_ref, zp_ref, zn_ref, yna_ref, ymla_ref = refs[:8]
    n_g = 5 if half else 1
    g_refs = refs[8:8 + n_g]
    (cw_ref, wfn_ref, wout_ref, g2_ref, wrc_ref, br_ref, x1_ref, h2_ref, route_ref, gtok_ref,
     cnt_ref) = refs[8 + n_g:]
    tm = x_ref.shape[0]
    i = pl.program_id(0)
    g_tile = _half_spectrum_tile(i, n // tm, *g_refs) if half else g_refs[0][...]
    mod = mod_ref[...]
    gate1, shift2, scale2 = mod[2:3], mod[3:4], mod[4:5]

    z = z_ref[...].astype(F32)
    ridx = lax.broadcasted_iota(jnp.int32, z.shape, 0)
    at_start = (i * tm) % n == 0
    at_end = ((i + 1) * tm) % n == 0
    prev_row = jnp.where(at_start, 0.0, zp_ref[7:8, :].astype(F32))
    next_row = jnp.where(at_end, 0.0, zn_ref[0:1, :].astype(F32))
    z_m1 = jnp.where(ridx == 0, prev_row, pltpu.roll(z, 1, axis=0))
    z_p1 = jnp.where(ridx == tm - 1, next_row, pltpu.roll(z, tm - 1, axis=0))
    cw = cw_ref[...]
    y_conv = ab_ref[...].astype(F32) * (z_m1 * cw[0:1] + z * cw[1:2] + z_p1 * cw[2:3])

    y_fn = _dot(g_tile, wfn_ref[...])
    cat = jnp.concatenate([y_conv.astype(BF16), yna_ref[...], ymla_ref[...], y_fn.astype(BF16)], axis=-1)
    x1 = x_ref[...] + gate1 * _dot(cat, wout_ref[...])
    x1_ref[...] = x1

    h2 = _rms(x1, g2_ref[...]) * (1.0 + scale2) + shift2
    packed = _pack_pairs(h2)
    piece = packed.shape[1] // MOE_PIECES
    for p in range(MOE_PIECES):
        h2_ref[p] = packed[:, p * piece:(p + 1) * piece]
    h2_hi = h2.astype(BF16)
    h2_lo = (h2 - h2_hi.astype(F32)).astype(BF16)
    both = _dot(h2_hi, wrc_ref[...])
    logits = both[:, :LANES] + (both[:, LANES:] + _dot(h2_lo, wrc_ref[:, :LANES]))
    s = jax.nn.sigmoid(logits)
    s_t = jnp.transpose(s)
    sb_t = jnp.transpose(s + br_ref[...])
    chosen, gates = _route([s_t[e:e + 1] for e in range(N_EXPERTS)], [sb_t[e:e + 1] for e in range(N_EXPERTS)])

    @pl.when(i == 0)
    def _():
        cnt_ref[...] = jnp.zeros(cnt_ref.shape, F32)

    chosen_f = jnp.concatenate([ch.astype(F32) for ch in chosen], axis=0)
    before = lax.broadcasted_iota(jnp.int32, (tm, tm), 0) < lax.broadcasted_iota(jnp.int32, (tm, tm), 1)
    prefix = _dot(chosen_f.astype(BF16), jnp.where(before, 1.0, 0.0).astype(BF16))
    base = cnt_ref[...]
    rank = jnp.concatenate([base] * (tm // LANES), axis=1) + prefix
    cnt_ref[...] = base + jnp.sum(chosen_f, axis=1, keepdims=True)

    zero = jnp.zeros((1, tm), F32)
    seen = zero
    slots = [[zero, zero, zero], [zero, zero, zero]]
    for e in range(N_EXPERTS):
        for k in range(2):
            hit = chosen[e] & (seen == float(k))
            for j, val in enumerate((float(e), gates[e], rank[e:e + 1])):
                slots[k][j] = jnp.where(hit, val, slots[k][j])
        seen = seen + chosen_f[e:e + 1]
    (e_lo, g_lo, r_lo), (e_hi, g_hi, r_hi) = slots
    route_ref[...] = jnp.concatenate([g_lo, g_hi, e_lo, e_hi, r_lo, r_hi, zero, zero], axis=0)
    gates_t = jnp.concatenate([g_lo, g_hi, jnp.zeros((LANES - 2, tm), F32)], axis=0)
    gtok_ref[...] = jnp.transpose(gates_t)


def _mixout(x, mod, layer, n, parts, wts, lat, tm):
    t, d = x.shape
    ab, z, yna, ymla, g = parts
    nblk8 = t // 8
    per8 = tm // 8
    cond_row_of_tile = _cond_row(lat, tm, n, mod.shape[1] - 1)
    const2 = lambda i: (0, 0)
    lsel = lambda i: (layer, 0, 0)
    row = lambda w: pl.BlockSpec((tm, w), lambda i: (i, 0))
    in_specs = [
        row(d),
        pl.BlockSpec((None, None, 6, d), lambda i: (layer, cond_row_of_tile(i), 0, 0)),
        row(CONV_WIDTH),
        row(CONV_WIDTH),
        pl.BlockSpec((8, CONV_WIDTH), lambda i: (jnp.maximum(i * per8 - 1, 0), 0)),
        pl.BlockSpec((8, CONV_WIDTH), lambda i: (jnp.minimum((i + 1) * per8, nblk8 - 1), 0)),
        row(NA_WIDTH),
        row(MLA_HEADS * MLA_V_DIM),
    ]
    half = isinstance(g, tuple)
    if half:
        plus, minus, mid = g
        tps = n // tm
        hps = tps // 2
        assert plus.shape[0] * 2 == t and tps % 2 == 0
        rev = np.zeros((tm, tm), np.float32)
        rev[np.arange(1, tm), tm - np.arange(1, tm)] = 1.0
        src = lambda i: (i // tps) * hps + jnp.clip(tps - 1 - i % tps, 0, hps - 1)
        in_specs += [
            pl.BlockSpec((tm, FN_WIDTH), lambda i: ((i // tps) * hps + jnp.minimum(i % tps, hps - 1), 0)),
            pl.BlockSpec((tm, FN_WIDTH), lambda i: (src(i), 0)),
            pl.BlockSpec((8, FN_WIDTH), lambda i: (jnp.minimum(src(i) + 1, plus.shape[0] // tm - 1) * per8, 0)),
            pl.BlockSpec((None, None, 8, FN_WIDTH), lambda i: (0, i // tps, 0, 0)),
            pl.BlockSpec((tm, tm), const2),
        ]
        g_args = [plus, minus, minus, mid, jnp.asarray(rev, BF16)]
    else:
        in_specs += [row(FN_WIDTH)]
        g_args = [g]
    in_specs += [
        pl.BlockSpec((None, 3, CONV_WIDTH), lsel),
        pl.BlockSpec((None, FN_WIDTH, FN_WIDTH), lsel),
        pl.BlockSpec((None, d, d), lsel),
        pl.BlockSpec((None, 1, d), lsel),
        pl.BlockSpec((d, 2 * LANES), const2),
        pl.BlockSpec((1, LANES), const2),
    ]
    out_specs = [
        row(d),
        pl.BlockSpec((MOE_PIECES, tm, d // 2 // MOE_PIECES), lambda i: (0, i, 0)),
        pl.BlockSpec((8, tm), lambda i: (0, i)),
        row(LANES),
        pl.BlockSpec((N_EXPERTS, LANES), const2),
    ]
    out_shape = [
        jax.ShapeDtypeStruct((t, d), F32),
        jax.ShapeDtypeStruct((MOE_PIECES, t, d // 2 // MOE_PIECES), jnp.uint32),
        jax.ShapeDtypeStruct((8, t), F32),
        jax.ShapeDtypeStruct((t, LANES), F32),
        jax.ShapeDtypeStruct((N_EXPERTS, LANES), F32),
    ]
    return pl.pallas_call(
        functools.partial(_mixout_kernel, n, half),
        grid=(t // tm,),
        in_specs=in_specs,
        out_specs=out_specs,
        out_shape=out_shape,
        compiler_params=_params("arbitrary"),
        name="mixout",
    )(x, mod, ab, z, z, z, yna, ymla, *g_args, wts["conv_w"], wts["w_fn"], wts["w_out"], wts["norm2"],
      wts["wr_cat"], wts["b_router"])


def _slot_positions(route, counts, rb):
    cnt = counts[:, 0].astype(jnp.int32)
    padded = (cnt + rb - 1) // rb * rb
    ends = jnp.cumsum(padded)
    offs = ends - padded
    experts = route[2:4].astype(jnp.int32)
    ranks = route[4:6].astype(jnp.int32)
    pos = ranks
    for e in range(N_EXPERTS):
        pos = pos + jnp.where(experts == e, offs[e], 0)
    nblk = (2 * route.shape[1]) // rb + N_EXPERTS
    starts = jnp.arange(nblk, dtype=jnp.int32) * rb
    blk_expert = jnp.sum((starts[:, None] >= ends[None, :]).astype(jnp.int32), axis=1)
    used = blk_expert < N_EXPERTS
    blk_expert = jnp.where(used, blk_expert, 0)
    valid_end = jnp.sum(jnp.where(blk_expert[:, None] == jnp.arange(N_EXPERTS)[None, :], (offs + cnt)[None, :], 0), axis=1)
    blk_valid = jnp.where(used, jnp.clip(valid_end - starts, 0, rb), 0)
    return pos, jnp.stack([blk_expert, blk_valid])


def _sc_mesh():
    return plsc.VectorSubcoreMesh(core_axis_name="c", subcore_axis_name="s")


def _sc_pipeline(body, nwin, in_specs, out_specs):
    return pltpu.emit_pipeline(body, grid=(nwin,), in_specs=in_specs, out_specs=out_specs,
                               core_axis_name=("c", "s"), dimension_semantics=(pltpu.PARALLEL,))


def _row_scatter(table, idx_a, idx_b, nrows):
    b, w = table.shape
    win = SC_WINDOW
    idx_spec = pl.BlockSpec((1, win), lambda i: (0, i))

    @functools.partial(pl.kernel, out_type=jax.ShapeDtypeStruct((nrows, w), table.dtype), mesh=_sc_mesh(),
                       scratch_types=[])
    def scatter(table_hbm, ia_hbm, ib_hbm, out_hbm):
        def body(rows_vmem, ia_vmem, ib_vmem):
            pltpu.sync_copy(rows_vmem, out_hbm.at[ia_vmem.at[0]])
            pltpu.sync_copy(rows_vmem, out_hbm.at[ib_vmem.at[0]])

        _sc_pipeline(body, b // win, [pl.BlockSpec((win, w), lambda i: (i, 0)), idx_spec, idx_spec], [])(
            table_hbm, ia_hbm, ib_hbm)

    return scatter(table, idx_a.reshape(1, b), idx_b.reshape(1, b))


def _row_gather(table, idx):
    b = idx.shape[0]
    w = table.shape[1]
    win = SC_WINDOW

    @functools.partial(pl.kernel, out_type=jax.ShapeDtypeStruct((b, w), table.dtype), mesh=_sc_mesh(),
                       scratch_types=[])
    def gather(table_hbm, idx_hbm, out_hbm):
        def body(idx_vmem, out_vmem):
            pltpu.sync_copy(table_hbm.at[idx_vmem.at[0]], out_vmem)

        _sc_pipeline(body, b // win, [pl.BlockSpec((1, win), lambda i: (0, i))],
                     [pl.BlockSpec((win, w), lambda i: (i, 0))])(idx_hbm, out_hbm)

    return gather(table, idx.reshape(1, b))


def _ffn_kernel(blk_ref, xs_ref, w13_ref, w2_ref, y_ref):
    i = pl.program_id(0)
    e = blk_ref[0, i]
    nvalid = blk_ref[1, i]

    @pl.when(nvalid > 0)
    def _():
        packed = jnp.concatenate([xs_ref[0], xs_ref[1]], axis=-1)
        live = lax.broadcasted_iota(jnp.int32, packed.shape, 0) < nvalid
        xb = _unpack_pairs(jnp.where(live, packed, jnp.uint32(0))).astype(BF16)
        up = _dot(xb, w13_ref[e])
        a, b = up[:, :EXPERT_FF], up[:, EXPERT_FF:]
        hid = (a * jax.nn.sigmoid(a)) * b
        y = _pack_pairs(_dot(hid.astype(BF16), w2_ref[e]))
        half = y.shape[1] // 2
        y_ref[0] = y[:, :half]
        y_ref[1] = y[:, half:]

    @pl.when(nvalid == 0)
    def _():
        y_ref[...] = jnp.zeros(y_ref.shape, y_ref.dtype)


def _expert_ffn(xs, blk, w13, w2, layer, rb):
    pieces, nrows, w = xs.shape
    d = 2 * pieces * w
    resident = dict(pipeline_mode=pl.Buffered(1))
    used = lambda i, blk: (0, jnp.where(blk[1, i] > 0, i, 0), 0)
    return pl.pallas_call(
        _ffn_kernel,
        grid_spec=pltpu.PrefetchScalarGridSpec(
            num_scalar_prefetch=1,
            grid=(nrows // rb,),
            in_specs=[
                pl.BlockSpec((pieces, rb, w), used),
                pl.BlockSpec((None, N_EXPERTS, d, 2 * EXPERT_FF), lambda i, blk: (layer, 0, 0, 0), **resident),
                pl.BlockSpec((None, N_EXPERTS, EXPERT_FF, d), lambda i, blk: (layer, 0, 0, 0), **resident),
            ],
            out_specs=pl.BlockSpec((pieces, rb, w), lambda i, blk: (0, i, 0)),
        ),
        out_shape=jax.ShapeDtypeStruct(xs.shape, xs.dtype),
        compiler_params=_params("parallel"),
        name="expert_ffn",
    )(blk, xs, w13, w2)


def _combine_kernel(final, x1_ref, gtok_ref, mod_ref, nf_ref, y_ref, o_ref):
    out = _moe_residual(x1_ref[...], gtok_ref, y_ref, mod_ref)
    if final:
        out = _rms(out, nf_ref[...])
    o_ref[...] = out


def _combine(x1, gtok, y_tok, mod, layer, n, norm_f, final, lat, tc=512):
    t, d = x1.shape
    cond_row_of_tile = _cond_row(lat, tc, n, mod.shape[1] - 1)
    row = lambda w: pl.BlockSpec((tc, w), lambda i: (i, 0))
    return pl.pallas_call(
        functools.partial(_combine_kernel, final),
        grid=(t // tc,),
        in_specs=[
            row(d),
            row(LANES),
            pl.BlockSpec((None, None, 6, d), lambda i: (layer, cond_row_of_tile(i), 0, 0)),
            pl.BlockSpec((1, d), lambda i: (0, 0)),
            pl.BlockSpec(y_tok.shape[:2] + (tc, y_tok.shape[3]), lambda i: (0, 0, i, 0)),
        ],
        out_specs=row(d),
        out_shape=jax.ShapeDtypeStruct((t, d), F32),
        compiler_params=_params("parallel"),
        name="combine",
    )(x1, gtok, mod, norm_f, y_tok)


def _moe_layer(x, mod, layer, n, parts, wts, lat, tm, rb):
    t = x.shape[0]
    x1, h2, route, gtok, counts = _mixout(x, mod, layer, n, parts, wts, lat, tm)
    pos, blk = _slot_positions(route, counts, rb)
    pieces, _, w = h2.shape
    nrows = blk.shape[1] * rb
    piece_base = (jnp.arange(pieces, dtype=jnp.int32) * nrows)[:, None]
    idx = [(piece_base + pos[s][None, :]).reshape(-1) for s in range(2)]
    xs = _row_scatter(h2.reshape(pieces * t, w), idx[0], idx[1], pieces * nrows).reshape(pieces, nrows, w)
    y = _expert_ffn(xs, blk, wts["w13"], wts["w2"], layer, rb)
    back = (piece_base[:, :, None] + pos[None, :, :]).reshape(-1)
    y_tok = _row_gather(y.reshape(pieces * nrows, w), back).reshape(pieces, 2, t, w)
    return x1, (gtok, y_tok)


def _swap_halves(w):
    nf = MLA_ROPE_DIM // 4
    idx = np.arange(MLA_ROPE_DIM).reshape(2, 2, nf)[:, ::-1, :].reshape(-1)
    return w[..., idx]


def _pack_weights(w_in, mla_wq_up, mla_wkv_up, w1, w3, w2, w_router, b_router):
    depth, d, _ = w_in.shape
    zeros = lambda w: jnp.zeros((depth, d, w), w_in.dtype)
    w_kr = w_in[..., 1920:1952]
    pad_rope = lambda w: jnp.concatenate([zeros(MLA_NOPE_DIM), w, zeros(MLA_QK_PAD - MLA_NOPE_DIM - MLA_ROPE_DIM)], -1)
    w_main = jnp.concatenate([w_in[..., :1920], pad_rope(w_kr), pad_rope(_swap_halves(w_kr)), w_in[..., 1952:]], -1)

    wq = mla_wq_up.reshape(depth, MLA_Q_LORA, MLA_HEADS, MLA_NOPE_DIM + MLA_ROPE_DIM)
    q_nope, q_rope = wq[..., :MLA_NOPE_DIM], wq[..., MLA_NOPE_DIM:]
    tail = jnp.zeros(q_rope.shape[:-1] + (MLA_QK_PAD - MLA_NOPE_DIM - MLA_ROPE_DIM,), wq.dtype)
    wq_a = jnp.concatenate([q_nope, q_rope, tail], -1).reshape(depth, MLA_Q_LORA, -1)
    wq_b = jnp.concatenate([jnp.zeros_like(q_nope), _swap_halves(q_rope), tail], -1).reshape(depth, MLA_Q_LORA, -1)

    wkv = mla_wkv_up.reshape(depth, MLA_KV_LORA, MLA_HEADS, MLA_NOPE_DIM + MLA_V_DIM)
    k_nope, v_up = wkv[..., :MLA_NOPE_DIM], wkv[..., MLA_NOPE_DIM:]
    k_tail = jnp.zeros(k_nope.shape[:-1] + (MLA_QK_PAD - MLA_NOPE_DIM,), wkv.dtype)
    wk_a = jnp.concatenate([k_nope, k_tail], -1).reshape(depth, MLA_KV_LORA, -1)
    wv = v_up.reshape(depth, MLA_KV_LORA, -1)
    v_tail = jnp.zeros(v_up.shape[:-1] + (MLA_V_PAD - MLA_V_DIM,), wkv.dtype)
    wv_ext = jnp.concatenate([v_up, v_tail], -1).reshape(depth, MLA_KV_LORA, -1)
    vone = np.zeros((1, MLA_HEADS * MLA_V_PAD), np.float32)
    vone[0, MLA_V_DIM::MLA_V_PAD] = 1.0

    wr = jnp.pad(w_router, ((0, 0), (0, LANES - N_EXPERTS)))
    wr_hi = wr.astype(BF16)
    wr_lo = (wr - wr_hi.astype(F32)).astype(BF16)
    return {
        "w_in": w_main.astype(BF16), "wq_a": wq_a.astype(BF16), "wq_b": wq_b.astype(BF16),
        "wk_a": wk_a.astype(BF16), "wv": wv.astype(BF16), "wv_ext": wv_ext.astype(BF16),
        "vone_ext": jnp.asarray(vone),
        "w13": jnp.concatenate([w1, w3], -1).astype(BF16), "w2": w2.astype(BF16),
        "wr_cat": jnp.concatenate([wr_hi, wr_lo], axis=-1),
        "b_router": jnp.pad(b_router, (0, LANES - N_EXPERTS)).reshape(1, LANES).astype(F32),
    }


def _channel_dft():
    c = np.arange(FN_GROUP_DIM)
    ang = 2.0 * np.pi * ((c[:, None] * c[None, :]) % FN_GROUP_DIM) / FN_GROUP_DIM
    out = np.zeros((FN_WIDTH, 2 * FN_WIDTH), np.float32)
    for g in range(FN_GROUPS):
        sl = slice(g * FN_GROUP_DIM, (g + 1) * FN_GROUP_DIM)
        out[sl, sl] = np.cos(ang) * FN_GROUP_DIM ** -0.5
        out[sl, FN_WIDTH + g * FN_GROUP_DIM:FN_WIDTH + (g + 1) * FN_GROUP_DIM] = np.sin(ang) * FN_GROUP_DIM ** -0.5
    return jnp.asarray(out, BF16)


def _rope_tables(n):
    tok = jnp.arange(n)
    pos = jnp.stack([tok // GRID_W, tok % GRID_W], axis=-1).astype(F32)
    nf = MLA_ROPE_DIM // 4
    freqs = ROPE_THETA ** (-jnp.arange(nf, dtype=F32) / nf)
    ang = pos[:, :, None] * freqs
    cos = jnp.broadcast_to(jnp.cos(ang)[:, :, None, :], (n, 2, 2, nf)).reshape(n, MLA_ROPE_DIM)
    sin = jnp.sin(ang)
    sin = jnp.stack([-sin, sin], axis=2).reshape(n, MLA_ROPE_DIM)
    pad = jnp.zeros((n, MLA_QK_PAD - MLA_NOPE_DIM - MLA_ROPE_DIM), F32)
    cos_t = jnp.concatenate([jnp.ones((n, MLA_NOPE_DIM), F32), cos, pad], -1)
    sin_t = jnp.concatenate([jnp.zeros((n, MLA_NOPE_DIM), F32), sin, pad], -1)
    return cos_t, sin_t


def kernel(x_prompt, x_sample, cache_na_k, cache_na_v, cache_mla_ckv, cache_mla_krope, c, c_ctx, w_ada, b_ada,
           norm1, norm2, w_in, conv_w, na_rpb, mla_gq, mla_wq_up, mla_gkv, mla_wkv_up, w_fn, w_out, w_router,
           b_router, w1, w3, w2, norm_f):
    bp, seq, d = x_prompt.shape
    bd, dec_seq, _ = x_sample.shape
    depth = w_in.shape[0]

    wts = _pack_weights(w_in, mla_wq_up, mla_wkv_up, w1, w3, w2, w_router, b_router)
    wts.update({
        "norm1": norm1.reshape(depth, 1, d), "norm2": norm2.reshape(depth, 1, d),
        "mla_gq": mla_gq.reshape(depth, 1, -1), "mla_gkv": mla_gkv.reshape(depth, 1, -1),
        "conv_w": conv_w, "w_fn": w_fn.astype(BF16), "w_out": w_out.astype(BF16),
        "norm_f": norm_f.reshape(1, d), "cs_bd": _channel_dft(),
    })

    cond = jnp.concatenate([c, jnp.zeros((-(bd + 1) % 8, d), c.dtype), c_ctx[None, :]], axis=0)
    mod = _ada_modulation(cond, w_ada, b_ada)

    xp = x_prompt.reshape(bp * seq, d)
    tables = _dft_tables(seq)
    caches = [
        jnp.zeros((bp, depth, NA_HEADS, seq, NA_HEAD_DIM), F32), jnp.zeros((bp, depth, NA_HEADS, seq, NA_HEAD_DIM), F32),
        jnp.zeros((bp, depth, seq, MLA_KV_LORA), F32), jnp.zeros((bp, depth, seq, MLA_ROPE_DIM), F32)]
    pending = None
    for layer in range(depth):
        outs = _premix(xp, mod, layer, seq, wts, None, False, seq * CTX_SEQS_PER_TILE, caches, pending)
        ab, z, qn, kn, vn, km, fab, qm, vm = outs[:9]
        caches = outs[9:13]
        if pending is not None:
            xp = outs[13]
        yna, ymla = _ctx_attention(qn, kn, vn, qm, km, vm, seq)
        g = _fourier(fab, tables, seq)
        xp, pending = _moe_layer(xp, mod, layer, seq, (ab, z, yna, ymla, g), wts, False, seq, MOE_ROW_BLOCK)
    xp = _combine(xp, pending[0], pending[1], mod, depth - 1, seq, wts["norm_f"], True, False, seq)
    new_na_k, new_na_v, new_ckv, new_krope = caches

    xs = x_sample.reshape(bd * dec_seq, d)
    kx, vxt = _ctx_kv(cache_mla_ckv, cache_mla_krope, wts["wk_a"], wts["wv_ext"], wts["vone_ext"])
    na_bias = _na_bias(na_rpb, dec_seq // GRID_W)
    rope = _rope_tables(dec_seq)
    tables = _dft_tables(dec_seq)
    pending = None
    for layer in range(depth):
        outs = _premix(xs, mod, layer, dec_seq, wts, rope, True, TM_LAT_PREMIX, None, pending)
        ab, z, qn, kn, vn, km, fab, qt, vt = outs[:9]
        if pending is not None:
            xs = outs[9]
        yna = _na_lat_attention(qn, kn, vn, cache_na_k, cache_na_v, na_bias, layer, dec_seq)
        ymla = _mla_lat_attention(qt, km, vt, kx, vxt, layer, dec_seq)
        g = _fourier(fab, tables, dec_seq)
        xs, pending = _moe_layer(xs, mod, layer, dec_seq, (ab, z, yna, ymla, g), wts, True, TM_LAT_MIXOUT,
                                 MOE_ROW_BLOCK)
    xs = _combine(xs, pending[0], pending[1], mod, depth - 1, dec_seq, wts["norm_f"], True, True, TM_LAT_PREMIX)

    return (xp.reshape(bp, seq, d), xs.reshape(bd, dec_seq, d), new_na_k, new_na_v, new_ckv, new_krope)
```

```python
import functools
import math

import numpy as np
import jax
import jax.numpy as jnp
from jax import lax
from jax.experimental import pallas as pl
from jax.experimental.pallas import tpu as pltpu
from jax.experimental.pallas import tpu_sc as plsc

F32 = jnp.float32
BF16 = jnp.bfloat16

GRID_W = 64
CONV_WIDTH = 256
NA_HEADS = 4
NA_HEAD_DIM = 64
NA_WIDTH = NA_HEADS * NA_HEAD_DIM
NA_KH = 8
NA_KW = 16
MLA_HEADS = 4
MLA_Q_LORA = 256
MLA_KV_LORA = 128
MLA_NOPE_DIM = 64
MLA_ROPE_DIM = 32
MLA_V_DIM = 64
MLA_QK_PAD = 128
MLA_V_PAD = 96
MLA_KEY_SUB = 256
LOG2E = 1.4426950408889634
FN_GROUPS = 4
FN_GROUP_DIM = 64
FN_WIDTH = FN_GROUPS * FN_GROUP_DIM
N_EXPERTS = 16
N_EXPERT_GROUPS = 4
EXPERTS_PER_GROUP = N_EXPERTS // N_EXPERT_GROUPS
EXPERT_FF = 256
ROPE_THETA = 10000.0
EPS = 1e-6
NEG_INF = -1e30
LANES = 128

NA_SCALE = NA_HEAD_DIM ** -0.5
MLA_SCALE = (MLA_NOPE_DIM + MLA_ROPE_DIM) ** -0.5

NA_Q_ROWS = 4
NA_WIN_ROWS = 12

TM_LAT_PREMIX = 512
TM_LAT_MIXOUT = 512
MOE_ROW_BLOCK = 512
SC_WINDOW = 128
MOE_PIECES = 2

V7X_VMEM_BYTES = 64 * 1024 * 1024
VMEM_LIMIT = V7X_VMEM_BYTES - 8 * 1024 * 1024

_C_AB, _C_AC, _C_AU, _C_Q, _C_K, _C_V, _C_CQ = 0, 256, 512, 768, 1024, 1280, 1536
_C_CKV, _C_KR, _C_KRS, _C_FU, _C_END = 1792, 1920, 2048, 2176, 2432


def _nt_dot(a, b):
    return lax.dot_general(a, b, (((1,), (1,)), ((), ())), preferred_element_type=F32)


def _dot(a, b):
    return jnp.dot(a, b, preferred_element_type=F32)


def _rms(x, g):
    return x * lax.rsqrt(jnp.mean(x * x, axis=-1, keepdims=True) + EPS) * g


def _params(*sem, flags=None):
    return pltpu.CompilerParams(dimension_semantics=sem, vmem_limit_bytes=VMEM_LIMIT, flags=flags)


def _ada_kernel(c_ref, w_ref, b_ref, o_ref):
    cnd = c_ref[...]
    act = cnd * jax.nn.sigmoid(cnd)
    o_ref[...] = _dot(act.astype(BF16), w_ref[...].astype(BF16)) + b_ref[...]


def _ada_modulation(cond, w_ada, b_ada):
    depth, d, six_d = w_ada.shape
    r = cond.shape[0]
    tn = 1024
    out = pl.pallas_call(
        _ada_kernel,
        grid=(depth, six_d // tn),
        in_specs=[
            pl.BlockSpec((r, d), lambda l, j: (0, 0)),
            pl.BlockSpec((None, d, tn), lambda l, j: (l, 0, j)),
            pl.BlockSpec((None, 1, tn), lambda l, j: (l, 0, j)),
        ],
        out_specs=pl.BlockSpec((None, r, tn), lambda l, j: (l, 0, j)),
        out_shape=jax.ShapeDtypeStruct((depth, r, six_d), F32),
        compiler_params=_params("parallel", "parallel"),
        name="ada_modulation",
    )(cond, w_ada, b_ada.reshape(depth, 1, six_d))
    return out.reshape(depth, r, 6, d)


def _unpack_pairs(p):
    hi = pltpu.bitcast(p & jnp.uint32(0xFFFF0000), F32)
    lo = pltpu.bitcast(p << 16, F32)
    return jnp.concatenate([hi, lo], axis=-1)


def _moe_residual(x1, gtok_ref, y_ref, mod_ref):
    g = gtok_ref[...]
    y_lo = _unpack_pairs(jnp.concatenate([y_ref[0, 0], y_ref[1, 0]], axis=-1))
    y_hi = _unpack_pairs(jnp.concatenate([y_ref[0, 1], y_ref[1, 1]], axis=-1))
    return x1 + mod_ref[...][5:6] * (g[:, 0:1] * y_lo + g[:, 1:2] * y_hi)


def _premix_kernel(lat, fused, cache_layer, n_in, *refs):
    (x_ref, mod_ref, g1_ref, w_ref, gq_ref, wqa_ref, wqb_ref, gkv_ref, wka_ref, wv_ref, vone_ref, cs_ref,
     cos_ref, sin_ref) = refs[:14]
    outs = refs[n_in:]
    (ab_ref, z_ref, qn_ref, kn_ref, vn_ref, km_ref, fab_ref) = outs[:7]

    x = x_ref[...]
    if fused:
        gtok_ref, y_ref, modp_ref = refs[n_in - 3:n_in]
        x = _moe_residual(x, gtok_ref, y_ref, modp_ref)
        outs[-1][...] = x
    mod = mod_ref[...]
    h = _rms(x, g1_ref[...]) * (1.0 + mod[1:2]) + mod[0:1]
    p = _dot(h.astype(BF16), w_ref[...])

    ab_ref[...] = p[:, _C_AB:_C_AC].astype(BF16)
    z_ref[...] = (p[:, _C_AC:_C_AU] * p[:, _C_AU:_C_Q]).astype(BF16)
    k_na = p[:, _C_K:_C_V]
    v_na = p[:, _C_V:_C_CQ]
    qn_ref[...] = (p[:, _C_Q:_C_K] * NA_SCALE).astype(BF16)
    kn_ref[...] = k_na.astype(BF16)
    vn_ref[...] = v_na.astype(BF16)

    cqn = _rms(p[:, _C_CQ:_C_CKV], gq_ref[...]).astype(BF16)
    ckvn = _rms(p[:, _C_CKV:_C_KR], gkv_ref[...])
    ckvn_b = ckvn.astype(BF16)
    qa = _dot(cqn, wqa_ref[...])
    kva = _dot(ckvn_b, wka_ref[...])
    v_mla = _dot(ckvn_b, wv_ref[...]) + vone_ref[...]
    kr = p[:, _C_KR:_C_KRS]
    if lat:
        cos = cos_ref[...]
        sin = sin_ref[...]
        qb = _dot(cqn, wqb_ref[...])
        krot = kr * cos + p[:, _C_KRS:_C_FU] * sin
        qt_ref, vt_ref = outs[7:9]
    else:
        krot = kr
        qm_ref, vm_ref, ck_ref, cv_ref, cckv_ref, ckr_ref = outs[7:13]
    for hd in range(MLA_HEADS):
        sl = slice(hd * MLA_QK_PAD, (hd + 1) * MLA_QK_PAD)
        km_ref[:, sl] = (kva[:, sl] + krot).astype(BF16)
        if lat:
            qh = (qa[:, sl] * cos + qb[:, sl] * sin) * (MLA_SCALE * LOG2E)
            qt_ref[sl, :] = jnp.transpose(qh).astype(BF16)
        else:
            qm_ref[:, sl] = (qa[:, sl] * MLA_SCALE).astype(BF16)

    fab_ref[...] = _dot(p[:, _C_FU:_C_END].astype(BF16), cs_ref[...]).astype(BF16)

    if lat:
        for j in range(v_mla.shape[1] // LANES):
            sl = slice(j * LANES, (j + 1) * LANES)
            vt_ref[sl, :] = jnp.transpose(v_mla[:, sl]).astype(BF16)
    else:
        vm_ref[...] = v_mla.astype(BF16)
        nseq, n = cckv_ref.shape[0], cckv_ref.shape[-2]
        at = () if cache_layer is None else (cache_layer,)
        if cache_layer is not None:
            for ref in (ck_ref, cv_ref, cckv_ref, ckr_ref):
                ref[...] = jnp.zeros(ref.shape, ref.dtype)
        for s in range(nseq):
            rows = slice(s * n, (s + 1) * n)
            for hd in range(NA_HEADS):
                sl = slice(hd * NA_HEAD_DIM, (hd + 1) * NA_HEAD_DIM)
                ck_ref[(s,) + at + (hd,)] = k_na[rows, sl]
                cv_ref[(s,) + at + (hd,)] = v_na[rows, sl]
            cckv_ref[(s,) + at] = ckvn[rows]
            ckr_ref[(s,) + at] = kr[rows, MLA_NOPE_DIM:MLA_NOPE_DIM + MLA_ROPE_DIM]


def _cond_row(lat, tm, n, ctx_row):
    return (lambda i: (i * tm) // n) if lat else (lambda i: ctx_row)


def _premix(x, mod, layer, n, wts, rope, lat, tm, caches=None, pending=None):
    t, d = x.shape
    if lat:
        cos_t, sin_t = rope
        wv, vone = wts["wv_ext"], wts["vone_ext"]
    else:
        cos_t = sin_t = jnp.zeros((8, LANES), F32)
        wv, vone = wts["wv"], jnp.zeros((1, MLA_HEADS * MLA_V_DIM), F32)
    vw = wv.shape[-1]
    qw = MLA_HEADS * MLA_QK_PAD
    cache_layer = None
    tiles_per_seq = n // tm
    cond_row = _cond_row(lat, tm, n, mod.shape[1] - 1)
    const = lambda *_: (0, 0)
    lsel = lambda *_: (layer, 0, 0)
    rope_spec = (pl.BlockSpec((tm, LANES), lambda i: (i % tiles_per_seq, 0)) if lat
                 else pl.BlockSpec((8, LANES), const))
    in_specs = [
        pl.BlockSpec((tm, d), lambda i: (i, 0)),
        pl.BlockSpec((None, None, 6, d), lambda i: (layer, cond_row(i), 0, 0)),
        pl.BlockSpec((None, 1, d), lsel),
        pl.BlockSpec((None, d, _C_END), lsel),
        pl.BlockSpec((None, 1, MLA_Q_LORA), lsel),
        pl.BlockSpec((None, MLA_Q_LORA, qw), lsel),
        pl.BlockSpec((None, MLA_Q_LORA, qw), lsel),
        pl.BlockSpec((None, 1, MLA_KV_LORA), lsel),
        pl.BlockSpec((None, MLA_KV_LORA, qw), lsel),
        pl.BlockSpec((None, MLA_KV_LORA, vw), lsel),
        pl.BlockSpec((1, vw), const),
        pl.BlockSpec((FN_WIDTH, 2 * FN_WIDTH), const),
        rope_spec,
        rope_spec,
    ]
    row = lambda w: pl.BlockSpec((tm, w), lambda i: (i, 0))
    widths = [CONV_WIDTH, CONV_WIDTH, NA_WIDTH, NA_WIDTH, NA_WIDTH, qw, 2 * FN_WIDTH]
    out_specs = [row(w) for w in widths]
    out_shape = [jax.ShapeDtypeStruct((t, w), BF16) for w in widths]
    if lat:
        out_specs += [pl.BlockSpec((None, qw, tm), lambda i: (i, 0, 0)),
                      pl.BlockSpec((None, vw, tm), lambda i: (i, 0, 0))]
        out_shape += [jax.ShapeDtypeStruct((t // tm, qw, tm), BF16),
                      jax.ShapeDtypeStruct((t // tm, vw, tm), BF16)]
    else:
        assert tm % n == 0
        b = t // n
        spt = tm // n
        depth = wts["w_in"].shape[0]
        if caches is None:
            cache_layer = layer
            lay, at = depth, (lambda i, *rest: (i, 0) + rest)
        else:
            lay, at = None, (lambda i, *rest: (i, layer) + rest)
        out_specs += [
            row(qw), row(vw),
            pl.BlockSpec((spt, lay, NA_HEADS, n, NA_HEAD_DIM), lambda i: at(i, 0, 0, 0)),
            pl.BlockSpec((spt, lay, NA_HEADS, n, NA_HEAD_DIM), lambda i: at(i, 0, 0, 0)),
            pl.BlockSpec((spt, lay, n, MLA_KV_LORA), lambda i: at(i, 0, 0)),
            pl.BlockSpec((spt, lay, n, MLA_ROPE_DIM), lambda i: at(i, 0, 0)),
        ]
        out_shape += [
            jax.ShapeDtypeStruct((t, qw), BF16), jax.ShapeDtypeStruct((t, vw), BF16),
            jax.ShapeDtypeStruct((b, depth, NA_HEADS, n, NA_HEAD_DIM), F32),
            jax.ShapeDtypeStruct((b, depth, NA_HEADS, n, NA_HEAD_DIM), F32),
            jax.ShapeDtypeStruct((b, depth, n, MLA_KV_LORA), F32),
            jax.ShapeDtypeStruct((b, depth, n, MLA_ROPE_DIM), F32),
        ]
    args = [x, mod, wts["norm1"], wts["w_in"], wts["mla_gq"], wts["wq_a"], wts["wq_b"], wts["mla_gkv"],
            wts["wk_a"], wv, vone, wts["cs_bd"], cos_t, sin_t]
    aliases = {}
    if caches is not None:
        first_cache_out = len(out_shape) - len(caches)
        aliases = {len(args) + j: first_cache_out + j for j in range(len(caches))}
        in_specs += [pl.BlockSpec(memory_space=pl.ANY)] * len(caches)
        args += list(caches)
    if pending is not None:
        gtok, y_tok = pending
        in_specs += [
            row(LANES),
            pl.BlockSpec(y_tok.shape[:2] + (tm, y_tok.shape[3]), lambda i: (0, 0, i, 0)),
            pl.BlockSpec((None, None, 6, d), lambda i: (layer - 1, cond_row(i), 0, 0)),
        ]
        args += [gtok, y_tok, mod]
        out_specs = out_specs + [row(d)]
        out_shape = out_shape + [jax.ShapeDtypeStruct((t, d), F32)]
    return pl.pallas_call(
        functools.partial(_premix_kernel, lat, pending is not None, cache_layer, len(args)),
        grid=(t // tm,),
        in_specs=in_specs,
        out_specs=out_specs,
        out_shape=out_shape,
        input_output_aliases=aliases,
        compiler_params=_params("parallel"),
        name="premix_lat" if lat else "premix_ctx",
    )(*args)


def _softmax_attend(q, k, v):
    s = _nt_dot(q, k)
    m = jnp.max(s, axis=-1, keepdims=True)
    p = jnp.exp(s - m)
    l = jnp.sum(p, axis=-1, keepdims=True)
    return _dot(p.astype(BF16), v) / l


def _ctx_attn_kernel(qn_ref, kn_ref, vn_ref, qm_ref, km_ref, vm_ref, yna_ref, ymla_ref):
    for hd in range(NA_HEADS):
        sl = slice(hd * NA_HEAD_DIM, (hd + 1) * NA_HEAD_DIM)
        yna_ref[:, sl] = _softmax_attend(qn_ref[:, sl], kn_ref[:, sl], vn_ref[:, sl]).astype(BF16)
    for hd in range(MLA_HEADS):
        sq = slice(hd * MLA_QK_PAD, (hd + 1) * MLA_QK_PAD)
        sv = slice(hd * MLA_V_DIM, (hd + 1) * MLA_V_DIM)
        ymla_ref[:, sv] = _softmax_attend(qm_ref[:, sq], km_ref[:, sq], vm_ref[:, sv]).astype(BF16)


def _ctx_attention(qn, kn, vn, qm, km, vm, n):
    t = qn.shape[0]
    spec = lambda w: pl.BlockSpec((n, w), lambda b: (b, 0))
    ins = [qn, kn, vn, qm, km, vm]
    return pl.pallas_call(
        _ctx_attn_kernel,
        grid=(t // n,),
        in_specs=[spec(a.shape[1]) for a in ins],
        out_specs=[spec(NA_WIDTH), spec(MLA_HEADS * MLA_V_DIM)],
        out_shape=[jax.ShapeDtypeStruct((t, NA_WIDTH), BF16),
                   jax.ShapeDtypeStruct((t, MLA_HEADS * MLA_V_DIM), BF16)],
        compiler_params=_params("parallel"),
        name="ctx_attention",
    )(*ins)


def _mla_lat_kernel(qt_ref, k_ref, vt_ref, kx_ref, vxt_ref, o_ref, s_scr, p_scr):
    nchunk, _, kc = vt_ref.shape
    tq = qt_ref.shape[1]
    sub = MLA_KEY_SUB

    ksl = lambda hd: slice(hd * MLA_QK_PAD, (hd + 1) * MLA_QK_PAD)
    vsl = lambda hd: slice(hd * MLA_V_PAD, (hd + 1) * MLA_V_PAD)

    def scores(slot, k_of, nk):
        cmax = []
        for hd in range(MLA_HEADS):
            qt = qt_ref[ksl(hd), :]
            part = None
            for j in range(0, nk, sub):
                st = _dot(k_of(hd, j), qt)
                s_scr[slot, hd, j:j + sub, :] = st
                blk = jnp.max(st.reshape(sub // 8, 8, tq), axis=0)
                part = blk if part is None else jnp.maximum(part, blk)
            cmax.append(jnp.max(part, axis=0, keepdims=True))
        return tuple(cmax)

    def attend(slot, cmax, state, vt_of, nk):
        new = []
        for hd in range(MLA_HEADS):
            m_i, acc = state[hd]
            m_new = jnp.maximum(m_i, cmax[hd])
            for j in range(0, nk, sub):
                p_scr[hd, j:j + sub, :] = jnp.exp2(s_scr[slot, hd, j:j + sub, :] - m_new).astype(BF16)
            acc = jnp.exp2(m_i - m_new) * acc + _dot(vt_of(hd), p_scr[hd, 0:nk, :])
            new.append((m_new, acc))
        return tuple(new)

    lat_keys = lambda c: (lambda hd, j: k_ref[pl.ds(pl.multiple_of(c * kc, kc) + j, sub), ksl(hd)])
    past = kx_ref.shape[0]
    state = tuple((jnp.full((1, tq), NEG_INF, F32), jnp.zeros((MLA_V_PAD, tq), F32)) for _ in range(MLA_HEADS))
    cmax_ctx = scores(1, lambda hd, j: kx_ref[j:j + sub, ksl(hd)], past)
    cmax = scores(0, lat_keys(0), kc)
    state = attend(1, cmax_ctx, state, lambda hd: vxt_ref[vsl(hd), :], past)

    lat_vals = lambda c: (lambda hd: vt_ref[c, vsl(hd), :])

    def body(i, carry):
        cmax0, state = carry
        c = 2 * i
        cmax1 = scores(1, lat_keys(c + 1), kc)
        state = attend(0, cmax0, state, lat_vals(c), kc)
        cmax0 = scores(0, lat_keys(c + 2), kc)
        state = attend(1, cmax1, state, lat_vals(c + 1), kc)
        return cmax0, state

    cmax, state = lax.fori_loop(0, nchunk // 2 - 1, body, (cmax, state))
    cmax1 = scores(1, lat_keys(nchunk - 1), kc)
    state = attend(0, cmax, state, lat_vals(nchunk - 2), kc)
    state = attend(1, cmax1, state, lat_vals(nchunk - 1), kc)
    o_t = jnp.concatenate([acc[:MLA_V_DIM] / acc[MLA_V_DIM:MLA_V_DIM + 1] for _, acc in state], axis=0)
    o_ref[...] = jnp.transpose(o_t).astype(BF16)


def _mla_lat_attention(qt, km, vt, kx, vxt, layer, n):
    ntile, qw, tq = qt.shape
    t = ntile * tq
    past = kx.shape[2]
    qpb = n // tq
    return pl.pallas_call(
        _mla_lat_kernel,
        grid=(t // n, qpb),
        in_specs=[
            pl.BlockSpec((None, qw, tq), lambda b, i: (b * qpb + i, 0, 0)),
            pl.BlockSpec((n, km.shape[1]), lambda b, i: (b, 0)),
            pl.BlockSpec((qpb, vt.shape[1], tq), lambda b, i: (b, 0, 0)),
            pl.BlockSpec((None, None, past, kx.shape[3]), lambda b, i: (layer, b, 0, 0)),
            pl.BlockSpec((None, None, vxt.shape[2], past), lambda b, i: (layer, b, 0, 0)),
        ],
        out_specs=pl.BlockSpec((tq, MLA_HEADS * MLA_V_DIM), lambda b, i: (b * qpb + i, 0)),
        out_shape=jax.ShapeDtypeStruct((t, MLA_HEADS * MLA_V_DIM), BF16),
        scratch_shapes=[pltpu.VMEM((2, MLA_HEADS, max(tq, past), tq), F32),
                        pltpu.VMEM((MLA_HEADS, max(tq, past), tq), BF16)],
        compiler_params=_params("parallel", "parallel"),
        name="mla_lat_attention",
    )(qt, km, vt, kx, vxt)


def _ctx_kv_kernel(ckv_ref, kr_ref, wka_ref, wv_ref, vone_ref, place_ref, k_ref, vt_ref):
    ckv = ckv_ref[...].astype(BF16)
    k_ref[...] = (_dot(ckv, wka_ref[...]) + _dot(kr_ref[...].astype(BF16), place_ref[...])).astype(BF16)
    v = _dot(ckv, wv_ref[...]) + vone_ref[...]
    for j in range(v.shape[1] // LANES):
        sl = slice(j * LANES, (j + 1) * LANES)
        vt_ref[sl, :] = jnp.transpose(v[:, sl]).astype(BF16)


def _ctx_kv(cache_ckv, cache_krope, wk_a, wv_ext, vone_ext):
    bd, depth, past, _ = cache_ckv.shape
    place = np.zeros((MLA_ROPE_DIM, MLA_HEADS * MLA_QK_PAD), np.float32)
    for hd in range(MLA_HEADS):
        for i in range(MLA_ROPE_DIM):
            place[i, hd * MLA_QK_PAD + MLA_NOPE_DIM + i] = 1.0
    kw, vw = MLA_HEADS * MLA_QK_PAD, MLA_HEADS * MLA_V_PAD
    return pl.pallas_call(
        _ctx_kv_kernel,
        grid=(depth, bd),
        in_specs=[
            pl.BlockSpec((None, None, past, MLA_KV_LORA), lambda l, b: (b, l, 0, 0)),
            pl.BlockSpec((None, None, past, MLA_ROPE_DIM), lambda l, b: (b, l, 0, 0)),
            pl.BlockSpec((None, MLA_KV_LORA, kw), lambda l, b: (l, 0, 0)),
            pl.BlockSpec((None, MLA_KV_LORA, vw), lambda l, b: (l, 0, 0)),
            pl.BlockSpec((1, vw), lambda l, b: (0, 0)),
            pl.BlockSpec((MLA_ROPE_DIM, kw), lambda l, b: (0, 0)),
        ],
        out_specs=[pl.BlockSpec((None, None, past, kw), lambda l, b: (l, b, 0, 0)),
                   pl.BlockSpec((None, None, vw, past), lambda l, b: (l, b, 0, 0))],
        out_shape=[jax.ShapeDtypeStruct((depth, bd, past, kw), BF16),
                   jax.ShapeDtypeStruct((depth, bd, vw, past), BF16)],
        compiler_params=_params("parallel", "parallel"),
        name="ctx_kv",
    )(cache_ckv, cache_krope, wk_a, wv_ext, vone_ext, jnp.asarray(place, BF16))


def _na_tile_geometry(rows):
    last = rows // NA_Q_ROWS - 1
    geo = []
    for j in (0, 1, last):
        r0 = j * NA_Q_ROWS
        geo.append((r0, min(max(r0 - NA_KH // 2, 0), rows - NA_WIN_ROWS)))
    return geo


def _na_bias_kernel(geo, rows, rpb_ref, o_ref):
    l = pl.program_id(0)
    hd = pl.program_id(1)
    base = (l * NA_HEADS + hd) * (2 * NA_KH - 1) * (2 * NA_KW - 1)
    qc = lax.broadcasted_iota(jnp.int32, (GRID_W, GRID_W), 0)
    kcol = lax.broadcasted_iota(jnp.int32, (GRID_W, GRID_W), 1)
    d_col = jnp.clip(kcol - qc + (NA_KW - 1), 0, 2 * NA_KW - 2)
    col_start = jnp.clip(qc - NA_KW // 2, 0, GRID_W - NA_KW)
    in_cols = (kcol >= col_start) & (kcol < col_start + NA_KW)
    neg = jnp.full((GRID_W, GRID_W), NEG_INF, F32)
    tabs = []
    for dr in range(2 * NA_KH - 1):
        acc = jnp.zeros((GRID_W, GRID_W), F32)
        for dc in range(2 * NA_KW - 1):
            acc = jnp.where(d_col == dc, rpb_ref[base + dr * (2 * NA_KW - 1) + dc], acc)
        tabs.append(jnp.where(in_cols, acc, neg))
    for kind, (r0, ws) in enumerate(geo):
        for i in range(NA_Q_ROWS):
            r = r0 + i
            lo = min(max(r - NA_KH // 2, 0), rows - NA_KH)
            for j in range(NA_WIN_ROWS):
                kr = ws + j
                blk = tabs[kr - r + NA_KH - 1] if lo <= kr < lo + NA_KH else neg
                o_ref[kind, i * GRID_W:(i + 1) * GRID_W, j * GRID_W:(j + 1) * GRID_W] = blk


def _na_bias(na_rpb, rows):
    depth = na_rpb.shape[0]
    geo = _na_tile_geometry(rows)
    qn, kn = NA_Q_ROWS * GRID_W, NA_WIN_ROWS * GRID_W
    return pl.pallas_call(
        functools.partial(_na_bias_kernel, geo, rows),
        grid=(depth, NA_HEADS),
        in_specs=[pl.BlockSpec(memory_space=pltpu.SMEM)],
        out_specs=pl.BlockSpec((None, None, 3, qn, kn), lambda l, h: (l, h, 0, 0, 0)),
        out_shape=jax.ShapeDtypeStruct((depth, NA_HEADS, 3, qn, kn), F32),
        compiler_params=_params("parallel", "parallel"),
        name="na_bias",
    )(na_rpb.reshape(-1))


def _na_lat_kernel(rows, q_ref, k_ref, v_ref, kx_ref, vx_ref, bias_ref, o_ref, s_scr, p_scr):
    j = pl.program_id(1)
    ws = jnp.clip(j * NA_Q_ROWS - NA_KH // 2, 0, rows - NA_WIN_ROWS)
    start = pl.multiple_of(ws * GRID_W, GRID_W)
    nk = NA_WIN_ROWS * GRID_W
    heads = [slice(hd * NA_HEAD_DIM, (hd + 1) * NA_HEAD_DIM) for hd in range(NA_HEADS)]
    m = []
    for hd, sl in enumerate(heads):
        q = q_ref[:, sl]
        s_win = _nt_dot(q, k_ref[pl.ds(start, nk), sl]) + bias_ref[hd]
        s_ctx = _nt_dot(q, kx_ref[hd].astype(BF16))
        s_scr[hd, :, :nk] = s_win
        s_scr[hd, :, nk:] = s_ctx
        m.append(jnp.maximum(jnp.max(s_win, axis=-1, keepdims=True), jnp.max(s_ctx, axis=-1, keepdims=True)))
    l = []
    for hd in range(NA_HEADS):
        p = jnp.exp(s_scr[hd] - m[hd])
        l.append(jnp.sum(p, axis=-1, keepdims=True))
        p_scr[hd] = p.astype(BF16)
    for hd, sl in enumerate(heads):
        o = _dot(p_scr[hd, :, :nk], v_ref[pl.ds(start, nk), sl]) + _dot(p_scr[hd, :, nk:], vx_ref[hd].astype(BF16))
        o_ref[:, sl] = (o / l[hd]).astype(BF16)


def _na_lat_attention(qn, kn, vn, cache_k, cache_v, bias, layer, n):
    t = qn.shape[0]
    rows = n // GRID_W
    assert rows % NA_Q_ROWS == 0 and rows >= NA_WIN_ROWS + NA_Q_ROWS
    tiles = rows // NA_Q_ROWS
    tq = NA_Q_ROWS * GRID_W
    past = cache_k.shape[3]

    def kind(b, j):
        return (layer, 0, jnp.where(j == 0, 0, jnp.where(j == tiles - 1, 2, 1)), 0, 0)

    return pl.pallas_call(
        functools.partial(_na_lat_kernel, rows),
        grid=(t // n, tiles),
        in_specs=[
            pl.BlockSpec((tq, NA_WIDTH), lambda b, j: (b * tiles + j, 0)),
            pl.BlockSpec((n, NA_WIDTH), lambda b, j: (b, 0)),
            pl.BlockSpec((n, NA_WIDTH), lambda b, j: (b, 0)),
            pl.BlockSpec((None, None, NA_HEADS, past, NA_HEAD_DIM), lambda b, j: (b, layer, 0, 0, 0)),
            pl.BlockSpec((None, None, NA_HEADS, past, NA_HEAD_DIM), lambda b, j: (b, layer, 0, 0, 0)),
            pl.BlockSpec((None, NA_HEADS, None, tq, NA_WIN_ROWS * GRID_W), kind),
        ],
        out_specs=pl.BlockSpec((tq, NA_WIDTH), lambda b, j: (b * tiles + j, 0)),
        out_shape=jax.ShapeDtypeStruct((t, NA_WIDTH), BF16),
        scratch_shapes=[pltpu.VMEM((NA_HEADS, tq, NA_WIN_ROWS * GRID_W + past), F32),
                        pltpu.VMEM((NA_HEADS, tq, NA_WIN_ROWS * GRID_W + past), BF16)],
        compiler_params=_params("parallel", "parallel"),
        name="na_lat_attention",
    )(qn, kn, vn, cache_k, cache_v, bias)


def _dft_tables(n):
    def thin(j, k, period):
        ang = (2.0 * math.pi / period) * ((j[:, None] * k[None, :]) % period).astype(F32)
        return jnp.cos(ang), jnp.sin(ang)

    k = jnp.arange(n, dtype=jnp.int32)
    scale = float(n) ** -0.5
    if n % 64 == 0 and n > 64:
        n1 = n // 64
        c1, s1 = thin(jnp.arange(n1, dtype=jnp.int32), k, n1)
        c2, s2 = thin(jnp.arange(64, dtype=jnp.int32), k, n)
        c1, s1, c2, s2 = c1[:, None, :], s1[:, None, :], c2[None, :, :], s2[None, :, :]
        cm = (c1 * c2 - s1 * s2).reshape(n, n)
        sm = (s1 * c2 + c1 * s2).reshape(n, n)
    else:
        cm, sm = thin(k, k, n)
    return (cm * scale).astype(BF16), (sm * -scale).astype(BF16)


def _fourier_kernel(c_ref, s_ref, ab_ref, o_ref):
    o_ref[...] = (_dot(c_ref[...], ab_ref[:, :FN_WIDTH]) + _dot(s_ref[...], ab_ref[:, FN_WIDTH:])).astype(BF16)


def _fourier_half_kernel(c_ref, s_ref, cmid_ref, ab_ref, plus_ref, minus_ref, mid_ref):
    a = ab_ref[:, :FN_WIDTH]
    p = _dot(c_ref[...], a)
    q = _dot(s_ref[...], ab_ref[:, FN_WIDTH:])
    plus_ref[...] = (p + q).astype(BF16)
    minus_ref[...] = (p - q).astype(BF16)
    mid_ref[...] = _dot(cmid_ref[...], a).astype(BF16)


def _fourier_half(fab, tables, n, tmf=512):
    t = fab.shape[0]
    b = t // n
    half = n // 2
    tiles = half // tmf
    cm, sm = tables
    cmid = jnp.broadcast_to(cm[half:half + 1], (8, n))
    row_out = lambda: pl.BlockSpec((tmf, FN_WIDTH), lambda i, bb: (bb * tiles + i, 0))
    plus, minus, mid = pl.pallas_call(
        _fourier_half_kernel,
        grid=(tiles, b),
        in_specs=[
            pl.BlockSpec((tmf, n), lambda i, bb: (i, 0)),
            pl.BlockSpec((tmf, n), lambda i, bb: (i, 0)),
            pl.BlockSpec((8, n), lambda i, bb: (0, 0)),
            pl.BlockSpec((n, 2 * FN_WIDTH), lambda i, bb: (bb, 0)),
        ],
        out_specs=[row_out(), row_out(), pl.BlockSpec((None, None, 8, FN_WIDTH), lambda i, bb: (i, bb, 0, 0))],
        out_shape=[jax.ShapeDtypeStruct((b * half, FN_WIDTH), BF16), jax.ShapeDtypeStruct((b * half, FN_WIDTH), BF16),
                   jax.ShapeDtypeStruct((tiles, b, 8, FN_WIDTH), BF16)],
        compiler_params=_params("parallel", "parallel"),
        name="fourier_half",
    )(cm, sm, cmid, fab)
    return plus, minus, mid


def _fourier(fab, tables, n, tmf=512):
    if n >= 4 * tmf and tmf == TM_LAT_MIXOUT:
        return _fourier_half(fab, tables, n, tmf)
    t = fab.shape[0]
    tmf = min(tmf, n)
    tiles = n // tmf
    cm, sm = tables
    return pl.pallas_call(
        _fourier_kernel,
        grid=(tiles, t // n),
        in_specs=[
            pl.BlockSpec((tmf, n), lambda i, b: (i, 0)),
            pl.BlockSpec((tmf, n), lambda i, b: (i, 0)),
            pl.BlockSpec((n, 2 * FN_WIDTH), lambda i, b: (b, 0)),
        ],
        out_specs=pl.BlockSpec((tmf, FN_WIDTH), lambda i, b: (b * tiles + i, 0)),
        out_shape=jax.ShapeDtypeStruct((t, FN_WIDTH), BF16),
        compiler_params=_params("parallel", "parallel"),
        name="fourier",
    )(cm, sm, fab)


def _route(s_t, sb_t):
    def top2_sum(v):
        hi1, lo1 = jnp.maximum(v[0], v[1]), jnp.minimum(v[0], v[1])
        hi2, lo2 = jnp.maximum(v[2], v[3]), jnp.minimum(v[2], v[3])
        return jnp.maximum(hi1, hi2) + jnp.maximum(jnp.minimum(hi1, hi2), jnp.maximum(lo1, lo2))

    best = top2_sum(sb_t[0:EXPERTS_PER_GROUP])
    gsel = jnp.zeros_like(best, dtype=jnp.int32)
    for g in range(1, N_EXPERT_GROUPS):
        cand = top2_sum(sb_t[g * EXPERTS_PER_GROUP:(g + 1) * EXPERTS_PER_GROUP])
        better = cand > best
        gsel = jnp.where(better, g, gsel)
        best = jnp.where(better, cand, best)
    chosen = []
    for e in range(N_EXPERTS):
        g = e // EXPERTS_PER_GROUP
        beaten = jnp.zeros_like(gsel)
        for o in range(g * EXPERTS_PER_GROUP, (g + 1) * EXPERTS_PER_GROUP):
            if o == e:
                continue
            ahead = (sb_t[o] > sb_t[e]) | ((sb_t[o] == sb_t[e]) & (o < e))
            beaten = beaten + ahead.astype(jnp.int32)
        chosen.append((gsel == g) & (beaten < 2))
    picked = [jnp.where(chosen[e], s_t[e], 0.0) for e in range(N_EXPERTS)]
    denom = picked[0]
    for e in range(1, N_EXPERTS):
        denom = denom + picked[e]
    return chosen, [pk / denom for pk in picked]


def _pack_pairs(x):
    w = x.shape[1] // 2
    hi = pltpu.bitcast(x[:, :w].astype(BF16).astype(F32), jnp.uint32)
    lo = pltpu.bitcast(x[:, w:].astype(BF16).astype(F32), jnp.uint32)
    return hi | (lo >> 16)


def _half_spectrum_tile(i, tiles_per_seq, plus_ref, minus_ref, edge_ref, mid_ref, rev_ref):
    j = i % tiles_per_seq - tiles_per_seq // 2
    body = _dot(rev_ref[...], minus_ref[...]).astype(BF16)
    first = jnp.where(j == 0, mid_ref[0:1, :], edge_ref[0:1, :])
    rows = lax.broadcasted_iota(jnp.int32, body.shape, 0)
    upper = jnp.where(rows == 0, first, body)
    return jnp.where(j < 0, plus_ref[...], upper)


def _mixout_kernel(n, half, *refs):
    x_ref, mod_ref, ab_ref, z_ref, zp_ref, zn_ref, yna_ref, ymla_ref = refs[:8]
    n_g = 5 if half else 1
    g_refs = refs[8:8 + n_g]
    (cw_ref, wfn_ref, wout_ref, g2_ref, wrc_ref, br_ref, x1_ref, h2_ref, route_ref, gtok_ref,
     cnt_ref) = refs[8 + n_g:]
    tm = x_ref.shape[0]
    i = pl.program_id(0)
    g_tile = _half_spectrum_tile(i, n // tm, *g_refs) if half else g_refs[0][...]
    mod = mod_ref[...]
    gate1, shift2, scale2 = mod[2:3], mod[3:4], mod[4:5]

    z = z_ref[...].astype(F32)
    ridx = lax.broadcasted_iota(jnp.int32, z.shape, 0)
    at_start = (i * tm) % n == 0
    at_end = ((i + 1) * tm) % n == 0
    prev_row = jnp.where(at_start, 0.0, zp_ref[7:8, :].astype(F32))
    next_row = jnp.where(at_end, 0.0, zn_ref[0:1, :].astype(F32))
    z_m1 = jnp.where(ridx == 0, prev_row, pltpu.roll(z, 1, axis=0))
    z_p1 = jnp.where(ridx == tm - 1, next_row, pltpu.roll(z, tm - 1, axis=0))
    cw = cw_ref[...]
    y_conv = ab_ref[...].astype(F32) * (z_m1 * cw[0:1] + z * cw[1:2] + z_p1 * cw[2:3])

    y_fn = _dot(g_tile, wfn_ref[...])
    cat = jnp.concatenate([y_conv.astype(BF16), yna_ref[...], ymla_ref[...], y_fn.astype(BF16)], axis=-1)
    x1 = x_ref[...] + gate1 * _dot(cat, wout_ref[...])
    x1_ref[...] = x1

    h2 = _rms(x1, g2_ref[...]) * (1.0 + scale2) + shift2
    packed = _pack_pairs(h2)
    piece = packed.shape[1] // MOE_PIECES
    for p in range(MOE_PIECES):
        h2_ref[p] = packed[:, p * piece:(p + 1) * piece]
    h2_hi = h2.astype(BF16)
    h2_lo = (h2 - h2_hi.astype(F32)).astype(BF16)
    both = _dot(h2_hi, wrc_ref[...])
    logits = both[:, :LANES] + (both[:, LANES:] + _dot(h2_lo, wrc_ref[:, :LANES]))
    s = jax.nn.sigmoid(logits)
    s_t = jnp.transpose(s)
    sb_t = jnp.transpose(s + br_ref[...])
    chosen, gates = _route([s_t[e:e + 1] for e in range(N_EXPERTS)], [sb_t[e:e + 1] for e in range(N_EXPERTS)])

    @pl.when(i == 0)
    def _():
        cnt_ref[...] = jnp.zeros(cnt_ref.shape, F32)

    chosen_f = jnp.concatenate([ch.astype(F32) for ch in chosen], axis=0)
    before = lax.broadcasted_iota(jnp.int32, (tm, tm), 0) < lax.broadcasted_iota(jnp.int32, (tm, tm), 1)
    prefix = _dot(chosen_f.astype(BF16), jnp.where(before, 1.0, 0.0).astype(BF16))
    base = cnt_ref[...]
    rank = jnp.concatenate([base] * (tm // LANES), axis=1) + prefix
    cnt_ref[...] = base + jnp.sum(chosen_f, axis=1, keepdims=True)

    zero = jnp.zeros((1, tm), F32)
    seen = zero
    slots = [[zero, zero, zero], [zero, zero, zero]]
    for e in range(N_EXPERTS):
        for k in range(2):
            hit = chosen[e] & (seen == float(k))
            for j, val in enumerate((float(e), gates[e], rank[e:e + 1])):
                slots[k][j] = jnp.where(hit, val, slots[k][j])
        seen = seen + chosen_f[e:e + 1]
    (e_lo, g_lo, r_lo), (e_hi, g_hi, r_hi) = slots
    route_ref[...] = jnp.concatenate([g_lo, g_hi, e_lo, e_hi, r_lo, r_hi, zero, zero], axis=0)
    gates_t = jnp.concatenate([g_lo, g_hi, jnp.zeros((LANES - 2, tm), F32)], axis=0)
    gtok_ref[...] = jnp.transpose(gates_t)


def _mixout(x, mod, layer, n, parts, wts, lat, tm):
    t, d = x.shape
    ab, z, yna, ymla, g = parts
    nblk8 = t // 8
    per8 = tm // 8
    cond_row_of_tile = _cond_row(lat, tm, n, mod.shape[1] - 1)
    const2 = lambda i: (0, 0)
    lsel = lambda i: (layer, 0, 0)
    row = lambda w: pl.BlockSpec((tm, w), lambda i: (i, 0))
    in_specs = [
        row(d),
        pl.BlockSpec((None, None, 6, d), lambda i: (layer, cond_row_of_tile(i), 0, 0)),
        row(CONV_WIDTH),
        row(CONV_WIDTH),
        pl.BlockSpec((8, CONV_WIDTH), lambda i: (jnp.maximum(i * per8 - 1, 0), 0)),
        pl.BlockSpec((8, CONV_WIDTH), lambda i: (jnp.minimum((i + 1) * per8, nblk8 - 1), 0)),
        row(NA_WIDTH),
        row(MLA_HEADS * MLA_V_DIM),
    ]
    half = isinstance(g, tuple)
    if half:
        plus, minus, mid = g
        tps = n // tm
        hps = tps // 2
        assert plus.shape[0] * 2 == t and tps % 2 == 0
        rev = np.zeros((tm, tm), np.float32)
        rev[np.arange(1, tm), tm - np.arange(1, tm)] = 1.0
        src = lambda i: (i // tps) * hps + jnp.clip(tps - 1 - i % tps, 0, hps - 1)
        in_specs += [
            pl.BlockSpec((tm, FN_WIDTH), lambda i: ((i // tps) * hps + jnp.minimum(i % tps, hps - 1), 0)),
            pl.BlockSpec((tm, FN_WIDTH), lambda i: (src(i), 0)),
            pl.BlockSpec((8, FN_WIDTH), lambda i: (jnp.minimum(src(i) + 1, plus.shape[0] // tm - 1) * per8, 0)),
            pl.BlockSpec((None, None, 8, FN_WIDTH), lambda i: (0, i // tps, 0, 0)),
            pl.BlockSpec((tm, tm), const2),
        ]
        g_args = [plus, minus, minus, mid, jnp.asarray(rev, BF16)]
    else:
        in_specs += [row(FN_WIDTH)]
        g_args = [g]
    in_specs += [
        pl.BlockSpec((None, 3, CONV_WIDTH), lsel),
        pl.BlockSpec((None, FN_WIDTH, FN_WIDTH), lsel),
        pl.BlockSpec((None, d, d), lsel),
        pl.BlockSpec((None, 1, d), lsel),
        pl.BlockSpec((d, 2 * LANES), const2),
        pl.BlockSpec((1, LANES), const2),
    ]
    out_specs = [
        row(d),
        pl.BlockSpec((MOE_PIECES, tm, d // 2 // MOE_PIECES), lambda i: (0, i, 0)),
        pl.BlockSpec((8, tm), lambda i: (0, i)),
        row(LANES),
        pl.BlockSpec((N_EXPERTS, LANES), const2),
    ]
    out_shape = [
        jax.ShapeDtypeStruct((t, d), F32),
        jax.ShapeDtypeStruct((MOE_PIECES, t, d // 2 // MOE_PIECES), jnp.uint32),
        jax.ShapeDtypeStruct((8, t), F32),
        jax.ShapeDtypeStruct((t, LANES), F32),
        jax.ShapeDtypeStruct((N_EXPERTS, LANES), F32),
    ]
    return pl.pallas_call(
        functools.partial(_mixout_kernel, n, half),
        grid=(t // tm,),
        in_specs=in_specs,
        out_specs=out_specs,
        out_shape=out_shape,
        compiler_params=_params("arbitrary"),
        name="mixout",
    )(x, mod, ab, z, z, z, yna, ymla, *g_args, wts["conv_w"], wts["w_fn"], wts["w_out"], wts["norm2"],
      wts["wr_cat"], wts["b_router"])


def _slot_positions(route, counts, rb):
    cnt = counts[:, 0].astype(jnp.int32)
    padded = (cnt + rb - 1) // rb * rb
    ends = jnp.cumsum(padded)
    offs = ends - padded
    experts = route[2:4].astype(jnp.int32)
    ranks = route[4:6].astype(jnp.int32)
    pos = ranks
    for e in range(N_EXPERTS):
        pos = pos + jnp.where(experts == e, offs[e], 0)
    nblk = (2 * route.shape[1]) // rb + N_EXPERTS
    starts = jnp.arange(nblk, dtype=jnp.int32) * rb
    blk_expert = jnp.sum((starts[:, None] >= ends[None, :]).astype(jnp.int32), axis=1)
    used = blk_expert < N_EXPERTS
    blk_expert = jnp.where(used, blk_expert, 0)
    valid_end = jnp.sum(jnp.where(blk_expert[:, None] == jnp.arange(N_EXPERTS)[None, :], (offs + cnt)[None, :], 0), axis=1)
    blk_valid = jnp.where(used, jnp.clip(valid_end - starts, 0, rb), 0)
    return pos, jnp.stack([blk_expert, blk_valid])


def _sc_mesh():
    return plsc.VectorSubcoreMesh(core_axis_name="c", subcore_axis_name="s")


def _sc_pipeline(body, nwin, in_specs, out_specs):
    return pltpu.emit_pipeline(body, grid=(nwin,), in_specs=in_specs, out_specs=out_specs,
                               core_axis_name=("c", "s"), dimension_semantics=(pltpu.PARALLEL,))


def _row_scatter(table, idx_a, idx_b, nrows):
    b, w = table.shape
    win = SC_WINDOW
    idx_spec = pl.BlockSpec((1, win), lambda i: (0, i))

    @functools.partial(pl.kernel, out_type=jax.ShapeDtypeStruct((nrows, w), table.dtype), mesh=_sc_mesh(),
                       scratch_types=[])
    def scatter(table_hbm, ia_hbm, ib_hbm, out_hbm):
        def body(rows_vmem, ia_vmem, ib_vmem):
            pltpu.sync_copy(rows_vmem, out_hbm.at[ia_vmem.at[0]])
            pltpu.sync_copy(rows_vmem, out_hbm.at[ib_vmem.at[0]])

        _sc_pipeline(body, b // win, [pl.BlockSpec((win, w), lambda i: (i, 0)), idx_spec, idx_spec], [])(
            table_hbm, ia_hbm, ib_hbm)

    return scatter(table, idx_a.reshape(1, b), idx_b.reshape(1, b))


def _row_gather(table, idx):
    b = idx.shape[0]
    w = table.shape[1]
    win = SC_WINDOW

    @functools.partial(pl.kernel, out_type=jax.ShapeDtypeStruct((b, w), table.dtype), mesh=_sc_mesh(),
                       scratch_types=[])
    def gather(table_hbm, idx_hbm, out_hbm):
        def body(idx_vmem, out_vmem):
            pltpu.sync_copy(table_hbm.at[idx_vmem.at[0]], out_vmem)

        _sc_pipeline(body, b // win, [pl.BlockSpec((1, win), lambda i: (0, i))],
                     [pl.BlockSpec((win, w), lambda i: (i, 0))])(idx_hbm, out_hbm)

    return gather(table, idx.reshape(1, b))


def _ffn_kernel(blk_ref, xs_ref, w13_ref, w2_ref, y_ref):
    i = pl.program_id(0)
    e = blk_ref[0, i]
    nvalid = blk_ref[1, i]

    @pl.when(nvalid > 0)
    def _():
        packed = jnp.concatenate([xs_ref[0], xs_ref[1]], axis=-1)
        live = lax.broadcasted_iota(jnp.int32, packed.shape, 0) < nvalid
        xb = _unpack_pairs(jnp.where(live, packed, jnp.uint32(0))).astype(BF16)
        up = _dot(xb, w13_ref[e])
        a, b = up[:, :EXPERT_FF], up[:, EXPERT_FF:]
        hid = (a * jax.nn.sigmoid(a)) * b
        y = _pack_pairs(_dot(hid.astype(BF16), w2_ref[e]))
        half = y.shape[1] // 2
        y_ref[0] = y[:, :half]
        y_ref[1] = y[:, half:]

    @pl.when(nvalid == 0)
    def _():
        y_ref[...] = jnp.zeros(y_ref.shape, y_ref.dtype)


def _expert_ffn(xs, blk, w13, w2, layer, rb):
    pieces, nrows, w = xs.shape
    d = 2 * pieces * w
    resident = dict(pipeline_mode=pl.Buffered(1))
    used = lambda i, blk: (0, jnp.where(blk[1, i] > 0, i, 0), 0)
    return pl.pallas_call(
        _ffn_kernel,
        grid_spec=pltpu.PrefetchScalarGridSpec(
            num_scalar_prefetch=1,
            grid=(nrows // rb,),
            in_specs=[
                pl.BlockSpec((pieces, rb, w), used),
                pl.BlockSpec((None, N_EXPERTS, d, 2 * EXPERT_FF), lambda i, blk: (layer, 0, 0, 0), **resident),
                pl.BlockSpec((None, N_EXPERTS, EXPERT_FF, d), lambda i, blk: (layer, 0, 0, 0), **resident),
            ],
            out_specs=pl.BlockSpec((pieces, rb, w), lambda i, blk: (0, i, 0)),
        ),
        out_shape=jax.ShapeDtypeStruct(xs.shape, xs.dtype),
        compiler_params=_params("parallel"),
        name="expert_ffn",
    )(blk, xs, w13, w2)


def _combine_kernel(final, x1_ref, gtok_ref, mod_ref, nf_ref, y_ref, o_ref):
    out = _moe_residual(x1_ref[...], gtok_ref, y_ref, mod_ref)
    if final:
        out = _rms(out, nf_ref[...])
    o_ref[...] = out


def _combine(x1, gtok, y_tok, mod, layer, n, norm_f, final, lat, tc=512):
    t, d = x1.shape
    cond_row_of_tile = _cond_row(lat, tc, n, mod.shape[1] - 1)
    row = lambda w: pl.BlockSpec((tc, w), lambda i: (i, 0))
    return pl.pallas_call(
        functools.partial(_combine_kernel, final),
        grid=(t // tc,),
        in_specs=[
            row(d),
            row(LANES),
            pl.BlockSpec((None, None, 6, d), lambda i: (layer, cond_row_of_tile(i), 0, 0)),
            pl.BlockSpec((1, d), lambda i: (0, 0)),
            pl.BlockSpec(y_tok.shape[:2] + (tc, y_tok.shape[3]), lambda i: (0, 0, i, 0)),
        ],
        out_specs=row(d),
        out_shape=jax.ShapeDtypeStruct((t, d), F32),
        compiler_params=_params("parallel"),
        name="combine",
    )(x1, gtok, mod, norm_f, y_tok)


def _moe_layer(x, mod, layer, n, parts, wts, lat, tm, rb):
    t = x.shape[0]
    x1, h2, route, gtok, counts = _mixout(x, mod, layer, n, parts, wts, lat, tm)
    pos, blk = _slot_positions(route, counts, rb)
    pieces, _, w = h2.shape
    nrows = blk.shape[1] * rb
    piece_base = (jnp.arange(pieces, dtype=jnp.int32) * nrows)[:, None]
    idx = [(piece_base + pos[s][None, :]).reshape(-1) for s in range(2)]
    xs = _row_scatter(h2.reshape(pieces * t, w), idx[0], idx[1], pieces * nrows).reshape(pieces, nrows, w)
    y = _expert_ffn(xs, blk, wts["w13"], wts["w2"], layer, rb)
    back = (piece_base[:, :, None] + pos[None, :, :]).reshape(-1)
    y_tok = _row_gather(y.reshape(pieces * nrows, w), back).reshape(pieces, 2, t, w)
    return x1, (gtok, y_tok)


def _swap_halves(w):
    nf = MLA_ROPE_DIM // 4
    idx = np.arange(MLA_ROPE_DIM).reshape(2, 2, nf)[:, ::-1, :].reshape(-1)
    return w[..., idx]


def _pack_weights(w_in, mla_wq_up, mla_wkv_up, w1, w3, w2, w_router, b_router):
    depth, d, _ = w_in.shape
    zeros = lambda w: jnp.zeros((depth, d, w), w_in.dtype)
    w_kr = w_in[..., 1920:1952]
    pad_rope = lambda w: jnp.concatenate([zeros(MLA_NOPE_DIM), w, zeros(MLA_QK_PAD - MLA_NOPE_DIM - MLA_ROPE_DIM)], -1)
    w_main = jnp.concatenate([w_in[..., :1920], pad_rope(w_kr), pad_rope(_swap_halves(w_kr)), w_in[..., 1952:]], -1)

    wq = mla_wq_up.reshape(depth, MLA_Q_LORA, MLA_HEADS, MLA_NOPE_DIM + MLA_ROPE_DIM)
    q_nope, q_rope = wq[..., :MLA_NOPE_DIM], wq[..., MLA_NOPE_DIM:]
    tail = jnp.zeros(q_rope.shape[:-1] + (MLA_QK_PAD - MLA_NOPE_DIM - MLA_ROPE_DIM,), wq.dtype)
    wq_a = jnp.concatenate([q_nope, q_rope, tail], -1).reshape(depth, MLA_Q_LORA, -1)
    wq_b = jnp.concatenate([jnp.zeros_like(q_nope), _swap_halves(q_rope), tail], -1).reshape(depth, MLA_Q_LORA, -1)

    wkv = mla_wkv_up.reshape(depth, MLA_KV_LORA, MLA_HEADS, MLA_NOPE_DIM + MLA_V_DIM)
    k_nope, v_up = wkv[..., :MLA_NOPE_DIM], wkv[..., MLA_NOPE_DIM:]
    k_tail = jnp.zeros(k_nope.shape[:-1] + (MLA_QK_PAD - MLA_NOPE_DIM,), wkv.dtype)
    wk_a = jnp.concatenate([k_nope, k_tail], -1).reshape(depth, MLA_KV_LORA, -1)
    wv = v_up.reshape(depth, MLA_KV_LORA, -1)
    v_tail = jnp.zeros(v_up.shape[:-1] + (MLA_V_PAD - MLA_V_DIM,), wkv.dtype)
    wv_ext = jnp.concatenate([v_up, v_tail], -1).reshape(depth, MLA_KV_LORA, -1)
    vone = np.zeros((1, MLA_HEADS * MLA_V_PAD), np.float32)
    vone[0, MLA_V_DIM::MLA_V_PAD] = 1.0

    wr = jnp.pad(w_router, ((0, 0), (0, LANES - N_EXPERTS)))
    wr_hi = wr.astype(BF16)
    wr_lo = (wr - wr_hi.astype(F32)).astype(BF16)
    return {
        "w_in": w_main.astype(BF16), "wq_a": wq_a.astype(BF16), "wq_b": wq_b.astype(BF16),
        "wk_a": wk_a.astype(BF16), "wv": wv.astype(BF16), "wv_ext": wv_ext.astype(BF16),
        "vone_ext": jnp.asarray(vone),
        "w13": jnp.concatenate([w1, w3], -1).astype(BF16), "w2": w2.astype(BF16),
        "wr_cat": jnp.concatenate([wr_hi, wr_lo], axis=-1),
        "b_router": jnp.pad(b_router, (0, LANES - N_EXPERTS)).reshape(1, LANES).astype(F32),
    }


def _channel_dft():
    c = np.arange(FN_GROUP_DIM)
    ang = 2.0 * np.pi * ((c[:, None] * c[None, :]) % FN_GROUP_DIM) / FN_GROUP_DIM
    out = np.zeros((FN_WIDTH, 2 * FN_WIDTH), np.float32)
    for g in range(FN_GROUPS):
        sl = slice(g * FN_GROUP_DIM, (g + 1) * FN_GROUP_DIM)
        out[sl, sl] = np.cos(ang) * FN_GROUP_DIM ** -0.5
        out[sl, FN_WIDTH + g * FN_GROUP_DIM:FN_WIDTH + (g + 1) * FN_GROUP_DIM] = np.sin(ang) * FN_GROUP_DIM ** -0.5
    return jnp.asarray(out, BF16)


def _rope_tables(n):
    tok = jnp.arange(n)
    pos = jnp.stack([tok // GRID_W, tok % GRID_W], axis=-1).astype(F32)
    nf = MLA_ROPE_DIM // 4
    freqs = ROPE_THETA ** (-jnp.arange(nf, dtype=F32) / nf)
    ang = pos[:, :, None] * freqs
    cos = jnp.broadcast_to(jnp.cos(ang)[:, :, None, :], (n, 2, 2, nf)).reshape(n, MLA_ROPE_DIM)
    sin = jnp.sin(ang)
    sin = jnp.stack([-sin, sin], axis=2).reshape(n, MLA_ROPE_DIM)
    pad = jnp.zeros((n, MLA_QK_PAD - MLA_NOPE_DIM - MLA_ROPE_DIM), F32)
    cos_t = jnp.concatenate([jnp.ones((n, MLA_NOPE_DIM), F32), cos, pad], -1)
    sin_t = jnp.concatenate([jnp.zeros((n, MLA_NOPE_DIM), F32), sin, pad], -1)
    return cos_t, sin_t


def kernel(x_prompt, x_sample, cache_na_k, cache_na_v, cache_mla_ckv, cache_mla_krope, c, c_ctx, w_ada, b_ada,
           norm1, norm2, w_in, conv_w, na_rpb, mla_gq, mla_wq_up, mla_gkv, mla_wkv_up, w_fn, w_out, w_router,
           b_router, w1, w3, w2, norm_f):
    bp, seq, d = x_prompt.shape
    bd, dec_seq, _ = x_sample.shape
    depth = w_in.shape[0]

    wts = _pack_weights(w_in, mla_wq_up, mla_wkv_up, w1, w3, w2, w_router, b_router)
    wts.update({
        "norm1": norm1.reshape(depth, 1, d), "norm2": norm2.reshape(depth, 1, d),
        "mla_gq": mla_gq.reshape(depth, 1, -1), "mla_gkv": mla_gkv.reshape(depth, 1, -1),
        "conv_w": conv_w, "w_fn": w_fn.astype(BF16), "w_out": w_out.astype(BF16),
        "norm_f": norm_f.reshape(1, d), "cs_bd": _channel_dft(),
    })

    cond = jnp.concatenate([c, jnp.zeros((-(bd + 1) % 8, d), c.dtype), c_ctx[None, :]], axis=0)
    mod = _ada_modulation(cond, w_ada, b_ada)

    xp = x_prompt.reshape(bp * seq, d)
    tables = _dft_tables(seq)
    caches = None
    pending = None
    for layer in range(depth):
        outs = _premix(xp, mod, layer, seq, wts, None, False, seq, caches, pending)
        ab, z, qn, kn, vn, km, fab, qm, vm = outs[:9]
        caches = outs[9:13]
        if pending is not None:
            xp = outs[13]
        yna, ymla = _ctx_attention(qn, kn, vn, qm, km, vm, seq)
        g = _fourier(fab, tables, seq)
        xp, pending = _moe_layer(xp, mod, layer, seq, (ab, z, yna, ymla, g), wts, False, seq, MOE_ROW_BLOCK)
    xp = _combine(xp, pending[0], pending[1], mod, depth - 1, seq, wts["norm_f"], True, False, seq)
    new_na_k, new_na_v, new_ckv, new_krope = caches

    xs = x_sample.reshape(bd * dec_seq, d)
    kx, vxt = _ctx_kv(cache_mla_ckv, cache_mla_krope, wts["wk_a"], wts["wv_ext"], wts["vone_ext"])
    na_bias = _na_bias(na_rpb, dec_seq // GRID_W)
    rope = _rope_tables(dec_seq)
    tables = _dft_tables(dec_seq)
    pending = None
    for layer in range(depth):
        outs = _premix(xs, mod, layer, dec_seq, wts, rope, True, TM_LAT_PREMIX, None, pending)
        ab, z, qn, kn, vn, km, fab, qt, vt = outs[:9]
        if pending is not None:
            xs = outs[9]
        yna = _na_lat_attention(qn, kn, vn, cache_na_k, cache_na_v, na_bias, layer, dec_seq)
        ymla = _mla_lat_attention(qt, km, vt, kx, vxt, layer, dec_seq)
        g = _fourier(fab, tables, dec_seq)
        xs, pending = _moe_layer(xs, mod, layer, dec_seq, (ab, z, yna, ymla, g), wts, True, TM_LAT_MIXOUT,
                                 MOE_ROW_BLOCK)
    xs = _combine(xs, pending[0], pending[1], mod, depth - 1, dec_seq, wts["norm_f"], True, True, TM_LAT_PREMIX)

    return (xp.reshape(bp, seq, d), xs.reshape(bd, dec_seq, d), new_na_k, new_na_v, new_ckv, new_krope)
```

```python
import functools
import math

import numpy as np
import jax
import jax.numpy as jnp
from jax import lax
from jax.experimental import pallas as pl
from jax.experimental.pallas import tpu as pltpu
from jax.experimental.pallas import tpu_sc as plsc

F32 = jnp.float32
BF16 = jnp.bfloat16

GRID_W = 64
CONV_WIDTH = 256
NA_HEADS = 4
NA_HEAD_DIM = 64
NA_WIDTH = NA_HEADS * NA_HEAD_DIM
NA_KH = 8
NA_KW = 16
MLA_HEADS = 4
MLA_Q_LORA = 256
MLA_KV_LORA = 128
MLA_NOPE_DIM = 64
MLA_ROPE_DIM = 32
MLA_V_DIM = 64
MLA_QK_PAD = 128
MLA_V_PAD = 96
MLA_KEY_SUB = 256
LOG2E = 1.4426950408889634
FN_GROUPS = 4
FN_GROUP_DIM = 64
FN_WIDTH = FN_GROUPS * FN_GROUP_DIM
N_EXPERTS = 16
N_EXPERT_GROUPS = 4
EXPERTS_PER_GROUP = N_EXPERTS // N_EXPERT_GROUPS
EXPERT_FF = 256
ROPE_THETA = 10000.0
EPS = 1e-6
NEG_INF = -1e30
LANES = 128

NA_SCALE = NA_HEAD_DIM ** -0.5
MLA_SCALE = (MLA_NOPE_DIM + MLA_ROPE_DIM) ** -0.5

NA_Q_ROWS = 4
NA_WIN_ROWS = 12

TM_LAT_PREMIX = 512
TM_LAT_MIXOUT = 512
MOE_ROW_BLOCK = 512
SC_WINDOW = 128
MOE_PIECES = 2

V7X_VMEM_BYTES = 64 * 1024 * 1024
VMEM_LIMIT = V7X_VMEM_BYTES - 8 * 1024 * 1024

_C_AB, _C_AC, _C_AU, _C_Q, _C_K, _C_V, _C_CQ = 0, 256, 512, 768, 1024, 1280, 1536
_C_CKV, _C_KR, _C_KRS, _C_FU, _C_END = 1792, 1920, 2048, 2176, 2432


def _nt_dot(a, b):
    return lax.dot_general(a, b, (((1,), (1,)), ((), ())), preferred_element_type=F32)


def _dot(a, b):
    return jnp.dot(a, b, preferred_element_type=F32)


def _rms(x, g):
    return x * lax.rsqrt(jnp.mean(x * x, axis=-1, keepdims=True) + EPS) * g


def _params(*sem, flags=None):
    return pltpu.CompilerParams(dimension_semantics=sem, vmem_limit_bytes=VMEM_LIMIT, flags=flags)


def _ada_kernel(c_ref, w_ref, b_ref, o_ref):
    cnd = c_ref[...]
    act = cnd * jax.nn.sigmoid(cnd)
    o_ref[...] = _dot(act.astype(BF16), w_ref[...].astype(BF16)) + b_ref[...]


def _ada_modulation(cond, w_ada, b_ada):
    depth, d, six_d = w_ada.shape
    r = cond.shape[0]
    tn = 1024
    out = pl.pallas_call(
        _ada_kernel,
        grid=(depth, six_d // tn),
        in_specs=[
            pl.BlockSpec((r, d), lambda l, j: (0, 0)),
            pl.BlockSpec((None, d, tn), lambda l, j: (l, 0, j)),
            pl.BlockSpec((None, 1, tn), lambda l, j: (l, 0, j)),
        ],
        out_specs=pl.BlockSpec((None, r, tn), lambda l, j: (l, 0, j)),
        out_shape=jax.ShapeDtypeStruct((depth, r, six_d), F32),
        compiler_params=_params("parallel", "parallel"),
        name="ada_modulation",
    )(cond, w_ada, b_ada.reshape(depth, 1, six_d))
    return out.reshape(depth, r, 6, d)


def _unpack_pairs(p):
    hi = pltpu.bitcast(p & jnp.uint32(0xFFFF0000), F32)
    lo = pltpu.bitcast(p << 16, F32)
    return jnp.concatenate([hi, lo], axis=-1)


def _moe_residual(x1, gtok_ref, y_ref, mod_ref):
    g = gtok_ref[...]
    y_lo = _unpack_pairs(jnp.concatenate([y_ref[0, 0], y_ref[1, 0]], axis=-1))
    y_hi = _unpack_pairs(jnp.concatenate([y_ref[0, 1], y_ref[1, 1]], axis=-1))
    return x1 + mod_ref[...][5:6] * (g[:, 0:1] * y_lo + g[:, 1:2] * y_hi)


def _premix_kernel(lat, fused, cache_layer, n_in, *refs):
    (x_ref, mod_ref, g1_ref, w_ref, gq_ref, wqa_ref, wqb_ref, gkv_ref, wka_ref, wv_ref, vone_ref, cs_ref,
     cos_ref, sin_ref) = refs[:14]
    outs = refs[n_in:]
    (ab_ref, z_ref, qn_ref, kn_ref, vn_ref, km_ref, fab_ref) = outs[:7]

    x = x_ref[...]
    if fused:
        gtok_ref, y_ref, modp_ref = refs[n_in - 3:n_in]
        x = _moe_residual(x, gtok_ref, y_ref, modp_ref)
        outs[-1][...] = x
    mod = mod_ref[...]
    h = _rms(x, g1_ref[...]) * (1.0 + mod[1:2]) + mod[0:1]
    p = _dot(h.astype(BF16), w_ref[...])

    ab_ref[...] = p[:, _C_AB:_C_AC].astype(BF16)
    z_ref[...] = (p[:, _C_AC:_C_AU] * p[:, _C_AU:_C_Q]).astype(BF16)
    k_na = p[:, _C_K:_C_V]
    v_na = p[:, _C_V:_C_CQ]
    qn_ref[...] = (p[:, _C_Q:_C_K] * NA_SCALE).astype(BF16)
    kn_ref[...] = k_na.astype(BF16)
    vn_ref[...] = v_na.astype(BF16)

    cqn = _rms(p[:, _C_CQ:_C_CKV], gq_ref[...]).astype(BF16)
    ckvn = _rms(p[:, _C_CKV:_C_KR], gkv_ref[...])
    ckvn_b = ckvn.astype(BF16)
    qa = _dot(cqn, wqa_ref[...])
    kva = _dot(ckvn_b, wka_ref[...])
    v_mla = _dot(ckvn_b, wv_ref[...]) + vone_ref[...]
    kr = p[:, _C_KR:_C_KRS]
    if lat:
        cos = cos_ref[...]
        sin = sin_ref[...]
        qb = _dot(cqn, wqb_ref[...])
        krot = kr * cos + p[:, _C_KRS:_C_FU] * sin
        qt_ref, vt_ref = outs[7:9]
    else:
        krot = kr
        qm_ref, vm_ref, ck_ref, cv_ref, cckv_ref, ckr_ref = outs[7:13]
    for hd in range(MLA_HEADS):
        sl = slice(hd * MLA_QK_PAD, (hd + 1) * MLA_QK_PAD)
        km_ref[:, sl] = (kva[:, sl] + krot).astype(BF16)
        if lat:
            qh = (qa[:, sl] * cos + qb[:, sl] * sin) * (MLA_SCALE * LOG2E)
            qt_ref[sl, :] = jnp.transpose(qh).astype(BF16)
        else:
            qm_ref[:, sl] = (qa[:, sl] * MLA_SCALE).astype(BF16)

    fab_ref[...] = _dot(p[:, _C_FU:_C_END].astype(BF16), cs_ref[...]).astype(BF16)

    if lat:
        for j in range(v_mla.shape[1] // LANES):
            sl = slice(j * LANES, (j + 1) * LANES)
            vt_ref[sl, :] = jnp.transpose(v_mla[:, sl]).astype(BF16)
    else:
        vm_ref[...] = v_mla.astype(BF16)
        nseq, n = cckv_ref.shape[0], cckv_ref.shape[-2]
        at = () if cache_layer is None else (cache_layer,)
        if cache_layer is not None:
            for ref in (ck_ref, cv_ref, cckv_ref, ckr_ref):
                ref[...] = jnp.zeros(ref.shape, ref.dtype)
        for s in range(nseq):
            rows = slice(s * n, (s + 1) * n)
            for hd in range(NA_HEADS):
                sl = slice(hd * NA_HEAD_DIM, (hd + 1) * NA_HEAD_DIM)
                ck_ref[(s,) + at + (hd,)] = k_na[rows, sl]
                cv_ref[(s,) + at + (hd,)] = v_na[rows, sl]
            cckv_ref[(s,) + at] = ckvn[rows]
            ckr_ref[(s,) + at] = kr[rows, MLA_NOPE_DIM:MLA_NOPE_DIM + MLA_ROPE_DIM]


def _cond_row(lat, tm, n, ctx_row):
    return (lambda i: (i * tm) // n) if lat else (lambda i: ctx_row)


def _premix(x, mod, layer, n, wts, rope, lat, tm, caches=None, pending=None):
    t, d = x.shape
    if lat:
        cos_t, sin_t = rope
        wv, vone = wts["wv_ext"], wts["vone_ext"]
    else:
        cos_t = sin_t = jnp.zeros((8, LANES), F32)
        wv, vone = wts["wv"], jnp.zeros((1, MLA_HEADS * MLA_V_DIM), F32)
    vw = wv.shape[-1]
    qw = MLA_HEADS * MLA_QK_PAD
    cache_layer = None
    tiles_per_seq = n // tm
    cond_row = _cond_row(lat, tm, n, mod.shape[1] - 1)
    const = lambda *_: (0, 0)
    lsel = lambda *_: (layer, 0, 0)
    rope_spec = (pl.BlockSpec((tm, LANES), lambda i: (i % tiles_per_seq, 0)) if lat
                 else pl.BlockSpec((8, LANES), const))
    in_specs = [
        pl.BlockSpec((tm, d), lambda i: (i, 0)),
        pl.BlockSpec((None, None, 6, d), lambda i: (layer, cond_row(i), 0, 0)),
        pl.BlockSpec((None, 1, d), lsel),
        pl.BlockSpec((None, d, _C_END), lsel),
        pl.BlockSpec((None, 1, MLA_Q_LORA), lsel),
        pl.BlockSpec((None, MLA_Q_LORA, qw), lsel),
        pl.BlockSpec((None, MLA_Q_LORA, qw), lsel),
        pl.BlockSpec((None, 1, MLA_KV_LORA), lsel),
        pl.BlockSpec((None, MLA_KV_LORA, qw), lsel),
        pl.BlockSpec((None, MLA_KV_LORA, vw), lsel),
        pl.BlockSpec((1, vw), const),
        pl.BlockSpec((FN_WIDTH, 2 * FN_WIDTH), const),
        rope_spec,
        rope_spec,
    ]
    row = lambda w: pl.BlockSpec((tm, w), lambda i: (i, 0))
    widths = [CONV_WIDTH, CONV_WIDTH, NA_WIDTH, NA_WIDTH, NA_WIDTH, qw, 2 * FN_WIDTH]
    out_specs = [row(w) for w in widths]
    out_shape = [jax.ShapeDtypeStruct((t, w), BF16) for w in widths]
    if lat:
        out_specs += [pl.BlockSpec((None, qw, tm), lambda i: (i, 0, 0)),
                      pl.BlockSpec((None, vw, tm), lambda i: (i, 0, 0))]
        out_shape += [jax.ShapeDtypeStruct((t // tm, qw, tm), BF16),
                      jax.ShapeDtypeStruct((t // tm, vw, tm), BF16)]
    else:
        assert tm % n == 0
        b = t // n
        spt = tm // n
        depth = wts["w_in"].shape[0]
        if caches is None:
            cache_layer = layer
            lay, at = depth, (lambda i, *rest: (i, 0) + rest)
        else:
            lay, at = None, (lambda i, *rest: (i, layer) + rest)
        out_specs += [
            row(qw), row(vw),
            pl.BlockSpec((spt, lay, NA_HEADS, n, NA_HEAD_DIM), lambda i: at(i, 0, 0, 0)),
            pl.BlockSpec((spt, lay, NA_HEADS, n, NA_HEAD_DIM), lambda i: at(i, 0, 0, 0)),
            pl.BlockSpec((spt, lay, n, MLA_KV_LORA), lambda i: at(i, 0, 0)),
            pl.BlockSpec((spt, lay, n, MLA_ROPE_DIM), lambda i: at(i, 0, 0)),
        ]
        out_shape += [
            jax.ShapeDtypeStruct((t, qw), BF16), jax.ShapeDtypeStruct((t, vw), BF16),
            jax.ShapeDtypeStruct((b, depth, NA_HEADS, n, NA_HEAD_DIM), F32),
            jax.ShapeDtypeStruct((b, depth, NA_HEADS, n, NA_HEAD_DIM), F32),
            jax.ShapeDtypeStruct((b, depth, n, MLA_KV_LORA), F32),
            jax.ShapeDtypeStruct((b, depth, n, MLA_ROPE_DIM), F32),
        ]
    args = [x, mod, wts["norm1"], wts["w_in"], wts["mla_gq"], wts["wq_a"], wts["wq_b"], wts["mla_gkv"],
            wts["wk_a"], wv, vone, wts["cs_bd"], cos_t, sin_t]
    aliases = {}
    if caches is not None:
        first_cache_out = len(out_shape) - len(caches)
        aliases = {len(args) + j: first_cache_out + j for j in range(len(caches))}
        in_specs += [pl.BlockSpec(memory_space=pl.ANY)] * len(caches)
        args += list(caches)
    if pending is not None:
        gtok, y_tok = pending
        in_specs += [
            row(LANES),
            pl.BlockSpec(y_tok.shape[:2] + (tm, y_tok.shape[3]), lambda i: (0, 0, i, 0)),
            pl.BlockSpec((None, None, 6, d), lambda i: (layer - 1, cond_row(i), 0, 0)),
        ]
        args += [gtok, y_tok, mod]
        out_specs = out_specs + [row(d)]
        out_shape = out_shape + [jax.ShapeDtypeStruct((t, d), F32)]
    return pl.pallas_call(
        functools.partial(_premix_kernel, lat, pending is not None, cache_layer, len(args)),
        grid=(t // tm,),
        in_specs=in_specs,
        out_specs=out_specs,
        out_shape=out_shape,
        input_output_aliases=aliases,
        compiler_params=_params("parallel"),
        name="premix_lat" if lat else "premix_ctx",
    )(*args)


def _ctx_attn_kernel(qn_ref, kn_ref, vn_ref, qm_ref, km_ref, vm_ref, yna_ref, ymla_ref, s_scr, p_scr):
    na = [slice(hd * NA_HEAD_DIM, (hd + 1) * NA_HEAD_DIM) for hd in range(NA_HEADS)]
    mq = [slice(hd * MLA_QK_PAD, (hd + 1) * MLA_QK_PAD) for hd in range(MLA_HEADS)]
    mv = [slice(hd * MLA_V_DIM, (hd + 1) * MLA_V_DIM) for hd in range(MLA_HEADS)]
    qk = [(qn_ref[:, sl], kn_ref[:, sl]) for sl in na] + [(qm_ref[:, sl], km_ref[:, sl]) for sl in mq]
    m, l = [], []
    for j, (q, k) in enumerate(qk):
        s = _nt_dot(q, k)
        s_scr[j] = s
        m.append(jnp.max(s, axis=-1, keepdims=True))
    for j in range(len(qk)):
        p = jnp.exp(s_scr[j] - m[j])
        l.append(jnp.sum(p, axis=-1, keepdims=True))
        p_scr[j] = p.astype(BF16)
    for hd, sl in enumerate(na):
        yna_ref[:, sl] = (_dot(p_scr[hd], vn_ref[:, sl]) / l[hd]).astype(BF16)
    for hd, sl in enumerate(mv):
        j = NA_HEADS + hd
        ymla_ref[:, sl] = (_dot(p_scr[j], vm_ref[:, sl]) / l[j]).astype(BF16)


def _ctx_attention(qn, kn, vn, qm, km, vm, n):
    t = qn.shape[0]
    spec = lambda w: pl.BlockSpec((n, w), lambda b: (b, 0))
    ins = [qn, kn, vn, qm, km, vm]
    return pl.pallas_call(
        _ctx_attn_kernel,
        grid=(t // n,),
        in_specs=[spec(a.shape[1]) for a in ins],
        out_specs=[spec(NA_WIDTH), spec(MLA_HEADS * MLA_V_DIM)],
        out_shape=[jax.ShapeDtypeStruct((t, NA_WIDTH), BF16),
                   jax.ShapeDtypeStruct((t, MLA_HEADS * MLA_V_DIM), BF16)],
        scratch_shapes=[pltpu.VMEM((NA_HEADS + MLA_HEADS, n, n), F32),
                        pltpu.VMEM((NA_HEADS + MLA_HEADS, n, n), BF16)],
        compiler_params=_params("parallel"),
        name="ctx_attention",
    )(*ins)


def _mla_lat_kernel(qt_ref, k_ref, vt_ref, kx_ref, vxt_ref, o_ref, s_scr, p_scr):
    nchunk, _, kc = vt_ref.shape
    tq = qt_ref.shape[1]
    sub = MLA_KEY_SUB

    ksl = lambda hd: slice(hd * MLA_QK_PAD, (hd + 1) * MLA_QK_PAD)
    vsl = lambda hd: slice(hd * MLA_V_PAD, (hd + 1) * MLA_V_PAD)

    def scores(slot, k_of, nk):
        cmax = []
        for hd in range(MLA_HEADS):
            qt = qt_ref[ksl(hd), :]
            part = None
            for j in range(0, nk, sub):
                st = _dot(k_of(hd, j), qt)
                s_scr[slot, hd, j:j + sub, :] = st
                blk = jnp.max(st.reshape(sub // 8, 8, tq), axis=0)
                part = blk if part is None else jnp.maximum(part, blk)
            cmax.append(jnp.max(part, axis=0, keepdims=True))
        return tuple(cmax)

    def attend(slot, cmax, state, vt_of, nk):
        new = []
        for hd in range(MLA_HEADS):
            m_i, acc = state[hd]
            m_new = jnp.maximum(m_i, cmax[hd])
            for j in range(0, nk, sub):
                p_scr[hd, j:j + sub, :] = jnp.exp2(s_scr[slot, hd, j:j + sub, :] - m_new).astype(BF16)
            acc = jnp.exp2(m_i - m_new) * acc + _dot(vt_of(hd), p_scr[hd, 0:nk, :])
            new.append((m_new, acc))
        return tuple(new)

    lat_keys = lambda c: (lambda hd, j: k_ref[pl.ds(pl.multiple_of(c * kc, kc) + j, sub), ksl(hd)])
    past = kx_ref.shape[0]
    state = tuple((jnp.full((1, tq), NEG_INF, F32), jnp.zeros((MLA_V_PAD, tq), F32)) for _ in range(MLA_HEADS))
    cmax_ctx = scores(1, lambda hd, j: kx_ref[j:j + sub, ksl(hd)], past)
    cmax = scores(0, lat_keys(0), kc)
    state = attend(1, cmax_ctx, state, lambda hd: vxt_ref[vsl(hd), :], past)

    lat_vals = lambda c: (lambda hd: vt_ref[c, vsl(hd), :])

    def body(i, carry):
        cmax0, state = carry
        c = 2 * i
        cmax1 = scores(1, lat_keys(c + 1), kc)
        state = attend(0, cmax0, state, lat_vals(c), kc)
        cmax0 = scores(0, lat_keys(c + 2), kc)
        state = attend(1, cmax1, state, lat_vals(c + 1), kc)
        return cmax0, state

    cmax, state = lax.fori_loop(0, nchunk // 2 - 1, body, (cmax, state))
    cmax1 = scores(1, lat_keys(nchunk - 1), kc)
    state = attend(0, cmax, state, lat_vals(nchunk - 2), kc)
    state = attend(1, cmax1, state, lat_vals(nchunk - 1), kc)
    o_t = jnp.concatenate([acc[:MLA_V_DIM] / acc[MLA_V_DIM:MLA_V_DIM + 1] for _, acc in state], axis=0)
    o_ref[...] = jnp.transpose(o_t).astype(BF16)


def _mla_lat_attention(qt, km, vt, kx, vxt, layer, n):
    ntile, qw, tq = qt.shape
    t = ntile * tq
    past = kx.shape[2]
    qpb = n // tq
    return pl.pallas_call(
        _mla_lat_kernel,
        grid=(t // n, qpb),
        in_specs=[
            pl.BlockSpec((None, qw, tq), lambda b, i: (b * qpb + i, 0, 0)),
            pl.BlockSpec((n, km.shape[1]), lambda b, i: (b, 0)),
            pl.BlockSpec((qpb, vt.shape[1], tq), lambda b, i: (b, 0, 0)),
            pl.BlockSpec((None, None, past, kx.shape[3]), lambda b, i: (layer, b, 0, 0)),
            pl.BlockSpec((None, None, vxt.shape[2], past), lambda b, i: (layer, b, 0, 0)),
        ],
        out_specs=pl.BlockSpec((tq, MLA_HEADS * MLA_V_DIM), lambda b, i: (b * qpb + i, 0)),
        out_shape=jax.ShapeDtypeStruct((t, MLA_HEADS * MLA_V_DIM), BF16),
        scratch_shapes=[pltpu.VMEM((2, MLA_HEADS, max(tq, past), tq), F32),
                        pltpu.VMEM((MLA_HEADS, max(tq, past), tq), BF16)],
        compiler_params=_params("parallel", "parallel"),
        name="mla_lat_attention",
    )(qt, km, vt, kx, vxt)


def _ctx_kv_kernel(ckv_ref, kr_ref, wka_ref, wv_ref, vone_ref, place_ref, k_ref, vt_ref):
    ckv = ckv_ref[...].astype(BF16)
    k_ref[...] = (_dot(ckv, wka_ref[...]) + _dot(kr_ref[...].astype(BF16), place_ref[...])).astype(BF16)
    v = _dot(ckv, wv_ref[...]) + vone_ref[...]
    for j in range(v.shape[1] // LANES):
        sl = slice(j * LANES, (j + 1) * LANES)
        vt_ref[sl, :] = jnp.transpose(v[:, sl]).astype(BF16)


def _ctx_kv(cache_ckv, cache_krope, wk_a, wv_ext, vone_ext):
    bd, depth, past, _ = cache_ckv.shape
    place = np.zeros((MLA_ROPE_DIM, MLA_HEADS * MLA_QK_PAD), np.float32)
    for hd in range(MLA_HEADS):
        for i in range(MLA_ROPE_DIM):
            place[i, hd * MLA_QK_PAD + MLA_NOPE_DIM + i] = 1.0
    kw, vw = MLA_HEADS * MLA_QK_PAD, MLA_HEADS * MLA_V_PAD
    return pl.pallas_call(
        _ctx_kv_kernel,
        grid=(depth, bd),
        in_specs=[
            pl.BlockSpec((None, None, past, MLA_KV_LORA), lambda l, b: (b, l, 0, 0)),
            pl.BlockSpec((None, None, past, MLA_ROPE_DIM), lambda l, b: (b, l, 0, 0)),
            pl.BlockSpec((None, MLA_KV_LORA, kw), lambda l, b: (l, 0, 0)),
            pl.BlockSpec((None, MLA_KV_LORA, vw), lambda l, b: (l, 0, 0)),
            pl.BlockSpec((1, vw), lambda l, b: (0, 0)),
            pl.BlockSpec((MLA_ROPE_DIM, kw), lambda l, b: (0, 0)),
        ],
        out_specs=[pl.BlockSpec((None, None, past, kw), lambda l, b: (l, b, 0, 0)),
                   pl.BlockSpec((None, None, vw, past), lambda l, b: (l, b, 0, 0))],
        out_shape=[jax.ShapeDtypeStruct((depth, bd, past, kw), BF16),
                   jax.ShapeDtypeStruct((depth, bd, vw, past), BF16)],
        compiler_params=_params("parallel", "parallel"),
        name="ctx_kv",
    )(cache_ckv, cache_krope, wk_a, wv_ext, vone_ext, jnp.asarray(place, BF16))


def _na_tile_geometry(rows):
    last = rows // NA_Q_ROWS - 1
    geo = []
    for j in (0, 1, last):
        r0 = j * NA_Q_ROWS
        geo.append((r0, min(max(r0 - NA_KH // 2, 0), rows - NA_WIN_ROWS)))
    return geo


def _na_bias_kernel(geo, rows, rpb_ref, o_ref):
    l = pl.program_id(0)
    hd = pl.program_id(1)
    base = (l * NA_HEADS + hd) * (2 * NA_KH - 1) * (2 * NA_KW - 1)
    qc = lax.broadcasted_iota(jnp.int32, (GRID_W, GRID_W), 0)
    kcol = lax.broadcasted_iota(jnp.int32, (GRID_W, GRID_W), 1)
    d_col = jnp.clip(kcol - qc + (NA_KW - 1), 0, 2 * NA_KW - 2)
    col_start = jnp.clip(qc - NA_KW // 2, 0, GRID_W - NA_KW)
    in_cols = (kcol >= col_start) & (kcol < col_start + NA_KW)
    neg = jnp.full((GRID_W, GRID_W), NEG_INF, F32)
    tabs = []
    for dr in range(2 * NA_KH - 1):
        acc = jnp.zeros((GRID_W, GRID_W), F32)
        for dc in range(2 * NA_KW - 1):
            acc = jnp.where(d_col == dc, rpb_ref[base + dr * (2 * NA_KW - 1) + dc], acc)
        tabs.append(jnp.where(in_cols, acc, neg))
    for kind, (r0, ws) in enumerate(geo):
        for i in range(NA_Q_ROWS):
            r = r0 + i
            lo = min(max(r - NA_KH // 2, 0), rows - NA_KH)
            for j in range(NA_WIN_ROWS):
                kr = ws + j
                blk = tabs[kr - r + NA_KH - 1] if lo <= kr < lo + NA_KH else neg
                o_ref[kind, i * GRID_W:(i + 1) * GRID_W, j * GRID_W:(j + 1) * GRID_W] = blk


def _na_bias(na_rpb, rows):
    depth = na_rpb.shape[0]
    geo = _na_tile_geometry(rows)
    qn, kn = NA_Q_ROWS * GRID_W, NA_WIN_ROWS * GRID_W
    return pl.pallas_call(
        functools.partial(_na_bias_kernel, geo, rows),
        grid=(depth, NA_HEADS),
        in_specs=[pl.BlockSpec(memory_space=pltpu.SMEM)],
        out_specs=pl.BlockSpec((None, None, 3, qn, kn), lambda l, h: (l, h, 0, 0, 0)),
        out_shape=jax.ShapeDtypeStruct((depth, NA_HEADS, 3, qn, kn), F32),
        compiler_params=_params("parallel", "parallel"),
        name="na_bias",
    )(na_rpb.reshape(-1))


def _na_lat_kernel(rows, q_ref, k_ref, v_ref, kx_ref, vx_ref, bias_ref, o_ref, s_scr, p_scr):
    j = pl.program_id(1)
    ws = jnp.clip(j * NA_Q_ROWS - NA_KH // 2, 0, rows - NA_WIN_ROWS)
    start = pl.multiple_of(ws * GRID_W, GRID_W)
    nk = NA_WIN_ROWS * GRID_W
    heads = [slice(hd * NA_HEAD_DIM, (hd + 1) * NA_HEAD_DIM) for hd in range(NA_HEADS)]
    m = []
    for hd, sl in enumerate(heads):
        q = q_ref[:, sl]
        s_win = _nt_dot(q, k_ref[pl.ds(start, nk), sl]) + bias_ref[hd]
        s_ctx = _nt_dot(q, kx_ref[hd].astype(BF16))
        s_scr[hd, :, :nk] = s_win
        s_scr[hd, :, nk:] = s_ctx
        m.append(jnp.maximum(jnp.max(s_win, axis=-1, keepdims=True), jnp.max(s_ctx, axis=-1, keepdims=True)))
    l = []
    for hd in range(NA_HEADS):
        p = jnp.exp(s_scr[hd] - m[hd])
        l.append(jnp.sum(p, axis=-1, keepdims=True))
        p_scr[hd] = p.astype(BF16)
    for hd, sl in enumerate(heads):
        o = _dot(p_scr[hd, :, :nk], v_ref[pl.ds(start, nk), sl]) + _dot(p_scr[hd, :, nk:], vx_ref[hd].astype(BF16))
        o_ref[:, sl] = (o / l[hd]).astype(BF16)


def _na_lat_attention(qn, kn, vn, cache_k, cache_v, bias, layer, n):
    t = qn.shape[0]
    rows = n // GRID_W
    assert rows % NA_Q_ROWS == 0 and rows >= NA_WIN_ROWS + NA_Q_ROWS
    tiles = rows // NA_Q_ROWS
    tq = NA_Q_ROWS * GRID_W
    past = cache_k.shape[3]

    def kind(b, j):
        return (layer, 0, jnp.where(j == 0, 0, jnp.where(j == tiles - 1, 2, 1)), 0, 0)

    return pl.pallas_call(
        functools.partial(_na_lat_kernel, rows),
        grid=(t // n, tiles),
        in_specs=[
            pl.BlockSpec((tq, NA_WIDTH), lambda b, j: (b * tiles + j, 0)),
            pl.BlockSpec((n, NA_WIDTH), lambda b, j: (b, 0)),
            pl.BlockSpec((n, NA_WIDTH), lambda b, j: (b, 0)),
            pl.BlockSpec((None, None, NA_HEADS, past, NA_HEAD_DIM), lambda b, j: (b, layer, 0, 0, 0)),
            pl.BlockSpec((None, None, NA_HEADS, past, NA_HEAD_DIM), lambda b, j: (b, layer, 0, 0, 0)),
            pl.BlockSpec((None, NA_HEADS, None, tq, NA_WIN_ROWS * GRID_W), kind),
        ],
        out_specs=pl.BlockSpec((tq, NA_WIDTH), lambda b, j: (b * tiles + j, 0)),
        out_shape=jax.ShapeDtypeStruct((t, NA_WIDTH), BF16),
        scratch_shapes=[pltpu.VMEM((NA_HEADS, tq, NA_WIN_ROWS * GRID_W + past), F32),
                        pltpu.VMEM((NA_HEADS, tq, NA_WIN_ROWS * GRID_W + past), BF16)],
        compiler_params=_params("parallel", "parallel"),
        name="na_lat_attention",
    )(qn, kn, vn, cache_k, cache_v, bias)


def _dft_tables(n):
    def thin(j, k, period):
        ang = (2.0 * math.pi / period) * ((j[:, None] * k[None, :]) % period).astype(F32)
        return jnp.cos(ang), jnp.sin(ang)

    k = jnp.arange(n, dtype=jnp.int32)
    scale = float(n) ** -0.5
    if n % 64 == 0 and n > 64:
        n1 = n // 64
        c1, s1 = thin(jnp.arange(n1, dtype=jnp.int32), k, n1)
        c2, s2 = thin(jnp.arange(64, dtype=jnp.int32), k, n)
        c1, s1, c2, s2 = c1[:, None, :], s1[:, None, :], c2[None, :, :], s2[None, :, :]
        cm = (c1 * c2 - s1 * s2).reshape(n, n)
        sm = (s1 * c2 + c1 * s2).reshape(n, n)
    else:
        cm, sm = thin(k, k, n)
    return (cm * scale).astype(BF16), (sm * -scale).astype(BF16)


def _fourier_kernel(c_ref, s_ref, ab_ref, o_ref):
    o_ref[...] = (_dot(c_ref[...], ab_ref[:, :FN_WIDTH]) + _dot(s_ref[...], ab_ref[:, FN_WIDTH:])).astype(BF16)


def _fourier_half_kernel(c_ref, s_ref, cmid_ref, ab_ref, plus_ref, minus_ref, mid_ref):
    a = ab_ref[:, :FN_WIDTH]
    p = _dot(c_ref[...], a)
    q = _dot(s_ref[...], ab_ref[:, FN_WIDTH:])
    plus_ref[...] = (p + q).astype(BF16)
    minus_ref[...] = (p - q).astype(BF16)
    mid_ref[...] = _dot(cmid_ref[...], a).astype(BF16)


def _fourier_half(fab, tables, n, tmf=512):
    t = fab.shape[0]
    b = t // n
    half = n // 2
    tiles = half // tmf
    cm, sm = tables
    cmid = jnp.broadcast_to(cm[half:half + 1], (8, n))
    row_out = lambda: pl.BlockSpec((tmf, FN_WIDTH), lambda i, bb: (bb * tiles + i, 0))
    plus, minus, mid = pl.pallas_call(
        _fourier_half_kernel,
        grid=(tiles, b),
        in_specs=[
            pl.BlockSpec((tmf, n), lambda i, bb: (i, 0)),
            pl.BlockSpec((tmf, n), lambda i, bb: (i, 0)),
            pl.BlockSpec((8, n), lambda i, bb: (0, 0)),
            pl.BlockSpec((n, 2 * FN_WIDTH), lambda i, bb: (bb, 0)),
        ],
        out_specs=[row_out(), row_out(), pl.BlockSpec((None, None, 8, FN_WIDTH), lambda i, bb: (i, bb, 0, 0))],
        out_shape=[jax.ShapeDtypeStruct((b * half, FN_WIDTH), BF16), jax.ShapeDtypeStruct((b * half, FN_WIDTH), BF16),
                   jax.ShapeDtypeStruct((tiles, b, 8, FN_WIDTH), BF16)],
        compiler_params=_params("parallel", "parallel"),
        name="fourier_half",
    )(cm, sm, cmid, fab)
    return plus, minus, mid


def _fourier(fab, tables, n, tmf=512):
    if n >= 4 * tmf and tmf == TM_LAT_MIXOUT:
        return _fourier_half(fab, tables, n, tmf)
    t = fab.shape[0]
    tmf = min(tmf, n)
    tiles = n // tmf
    cm, sm = tables
    return pl.pallas_call(
        _fourier_kernel,
        grid=(tiles, t // n),
        in_specs=[
            pl.BlockSpec((tmf, n), lambda i, b: (i, 0)),
            pl.BlockSpec((tmf, n), lambda i, b: (i, 0)),
            pl.BlockSpec((n, 2 * FN_WIDTH), lambda i, b: (b, 0)),
        ],
        out_specs=pl.BlockSpec((tmf, FN_WIDTH), lambda i, b: (b * tiles + i, 0)),
        out_shape=jax.ShapeDtypeStruct((t, FN_WIDTH), BF16),
        compiler_params=_params("parallel", "parallel"),
        name="fourier",
    )(cm, sm, fab)


def _route(s_t, sb_t):
    def top2_sum(v):
        hi1, lo1 = jnp.maximum(v[0], v[1]), jnp.minimum(v[0], v[1])
        hi2, lo2 = jnp.maximum(v[2], v[3]), jnp.minimum(v[2], v[3])
        return jnp.maximum(hi1, hi2) + jnp.maximum(jnp.minimum(hi1, hi2), jnp.maximum(lo1, lo2))

    best = top2_sum(sb_t[0:EXPERTS_PER_GROUP])
    gsel = jnp.zeros_like(best, dtype=jnp.int32)
    for g in range(1, N_EXPERT_GROUPS):
        cand = top2_sum(sb_t[g * EXPERTS_PER_GROUP:(g + 1) * EXPERTS_PER_GROUP])
        better = cand > best
        gsel = jnp.where(better, g, gsel)
        best = jnp.where(better, cand, best)
    chosen = []
    for e in range(N_EXPERTS):
        g = e // EXPERTS_PER_GROUP
        beaten = jnp.zeros_like(gsel)
        for o in range(g * EXPERTS_PER_GROUP, (g + 1) * EXPERTS_PER_GROUP):
            if o == e:
                continue
            ahead = (sb_t[o] > sb_t[e]) | ((sb_t[o] == sb_t[e]) & (o < e))
            beaten = beaten + ahead.astype(jnp.int32)
        chosen.append((gsel == g) & (beaten < 2))
    picked = [jnp.where(chosen[e], s_t[e], 0.0) for e in range(N_EXPERTS)]
    denom = picked[0]
    for e in range(1, N_EXPERTS):
        denom = denom + picked[e]
    return chosen, [pk / denom for pk in picked]


def _pack_pairs(x):
    w = x.shape[1] // 2
    hi = pltpu.bitcast(x[:, :w].astype(BF16).astype(F32), jnp.uint32)
    lo = pltpu.bitcast(x[:, w:].astype(BF16).astype(F32), jnp.uint32)
    return hi | (lo >> 16)


def _half_spectrum_tile(i, tiles_per_seq, plus_ref, minus_ref, edge_ref, mid_ref, rev_ref):
    j = i % tiles_per_seq - tiles_per_seq // 2
    body = _dot(rev_ref[...], minus_ref[...]).astype(BF16)
    first = jnp.where(j == 0, mid_ref[0:1, :], edge_ref[0:1, :])
    rows = lax.broadcasted_iota(jnp.int32, body.shape, 0)
    upper = jnp.where(rows == 0, first, body)
    return jnp.where(j < 0, plus_ref[...], upper)


def _mixout_kernel(n, half, *refs):
    x_ref, mod_ref, ab_ref, z_ref, zp_ref, zn_ref, yna_ref, ymla_ref = refs[:8]
    n_g = 5 if half else 1
    g_refs = refs[8:8 + n_g]
    (cw_ref, wfn_ref, wout_ref, g2_ref, wrc_ref, br_ref, x1_ref, h2_ref, route_ref, gtok_ref,
     cnt_ref) = refs[8 + n_g:]
    tm = x_ref.shape[0]
    i = pl.program_id(0)
    g_tile = _half_spectrum_tile(i, n // tm, *g_refs) if half else g_refs[0][...]
    mod = mod_ref[...]
    gate1, shift2, scale2 = mod[2:3], mod[3:4], mod[4:5]

    z = z_ref[...].astype(F32)
    ridx = lax.broadcasted_iota(jnp.int32, z.shape, 0)
    at_start = (i * tm) % n == 0
    at_end = ((i + 1) * tm) % n == 0
    prev_row = jnp.where(at_start, 0.0, zp_ref[7:8, :].astype(F32))
    next_row = jnp.where(at_end, 0.0, zn_ref[0:1, :].astype(F32))
    z_m1 = jnp.where(ridx == 0, prev_row, pltpu.roll(z, 1, axis=0))
    z_p1 = jnp.where(ridx == tm - 1, next_row, pltpu.roll(z, tm - 1, axis=0))
    cw = cw_ref[...]
    y_conv = ab_ref[...].astype(F32) * (z_m1 * cw[0:1] + z * cw[1:2] + z_p1 * cw[2:3])

    y_fn = _dot(g_tile, wfn_ref[...])
    cat = jnp.concatenate([y_conv.astype(BF16), yna_ref[...], ymla_ref[...], y_fn.astype(BF16)], axis=-1)
    x1 = x_ref[...] + gate1 * _dot(cat, wout_ref[...])
    x1_ref[...] = x1

    h2 = _rms(x1, g2_ref[...]) * (1.0 + scale2) + shift2
    packed = _pack_pairs(h2)
    piece = packed.shape[1] // MOE_PIECES
    for p in range(MOE_PIECES):
        h2_ref[p] = packed[:, p * piece:(p + 1) * piece]
    h2_hi = h2.astype(BF16)
    h2_lo = (h2 - h2_hi.astype(F32)).astype(BF16)
    both = _dot(h2_hi, wrc_ref[...])
    logits = both[:, :LANES] + (both[:, LANES:] + _dot(h2_lo, wrc_ref[:, :LANES]))
    s = jax.nn.sigmoid(logits)
    s_t = jnp.transpose(s)
    sb_t = jnp.transpose(s + br_ref[...])
    chosen, gates = _route([s_t[e:e + 1] for e in range(N_EXPERTS)], [sb_t[e:e + 1] for e in range(N_EXPERTS)])

    @pl.when(i == 0)
    def _():
        cnt_ref[...] = jnp.zeros(cnt_ref.shape, F32)

    chosen_f = jnp.concatenate([ch.astype(F32) for ch in chosen], axis=0)
    before = lax.broadcasted_iota(jnp.int32, (tm, tm), 0) < lax.broadcasted_iota(jnp.int32, (tm, tm), 1)
    prefix = _dot(chosen_f.astype(BF16), jnp.where(before, 1.0, 0.0).astype(BF16))
    base = cnt_ref[...]
    rank = jnp.concatenate([base] * (tm // LANES), axis=1) + prefix
    cnt_ref[...] = base + jnp.sum(chosen_f, axis=1, keepdims=True)

    zero = jnp.zeros((1, tm), F32)
    seen = zero
    slots = [[zero, zero, zero], [zero, zero, zero]]
    for e in range(N_EXPERTS):
        for k in range(2):
            hit = chosen[e] & (seen == float(k))
            for j, val in enumerate((float(e), gates[e], rank[e:e + 1])):
                slots[k][j] = jnp.where(hit, val, slots[k][j])
        seen = seen + chosen_f[e:e + 1]
    (e_lo, g_lo, r_lo), (e_hi, g_hi, r_hi) = slots
    route_ref[...] = jnp.concatenate([g_lo, g_hi, e_lo, e_hi, r_lo, r_hi, zero, zero], axis=0)
    gates_t = jnp.concatenate([g_lo, g_hi, jnp.zeros((LANES - 2, tm), F32)], axis=0)
    gtok_ref[...] = jnp.transpose(gates_t)


def _mixout(x, mod, layer, n, parts, wts, lat, tm):
    t, d = x.shape
    ab, z, yna, ymla, g = parts
    nblk8 = t // 8
    per8 = tm // 8
    cond_row_of_tile = _cond_row(lat, tm, n, mod.shape[1] - 1)
    const2 = lambda i: (0, 0)
    lsel = lambda i: (layer, 0, 0)
    row = lambda w: pl.BlockSpec((tm, w), lambda i: (i, 0))
    in_specs = [
        row(d),
        pl.BlockSpec((None, None, 6, d), lambda i: (layer, cond_row_of_tile(i), 0, 0)),
        row(CONV_WIDTH),
        row(CONV_WIDTH),
        pl.BlockSpec((8, CONV_WIDTH), lambda i: (jnp.maximum(i * per8 - 1, 0), 0)),
        pl.BlockSpec((8, CONV_WIDTH), lambda i: (jnp.minimum((i + 1) * per8, nblk8 - 1), 0)),
        row(NA_WIDTH),
        row(MLA_HEADS * MLA_V_DIM),
    ]
    half = isinstance(g, tuple)
    if half:
        plus, minus, mid = g
        tps = n // tm
        hps = tps // 2
        assert plus.shape[0] * 2 == t and tps % 2 == 0
        rev = np.zeros((tm, tm), np.float32)
        rev[np.arange(1, tm), tm - np.arange(1, tm)] = 1.0
        src = lambda i: (i // tps) * hps + jnp.clip(tps - 1 - i % tps, 0, hps - 1)
        in_specs += [
            pl.BlockSpec((tm, FN_WIDTH), lambda i: ((i // tps) * hps + jnp.minimum(i % tps, hps - 1), 0)),
            pl.BlockSpec((tm, FN_WIDTH), lambda i: (src(i), 0)),
            pl.BlockSpec((8, FN_WIDTH), lambda i: (jnp.minimum(src(i) + 1, plus.shape[0] // tm - 1) * per8, 0)),
            pl.BlockSpec((None, None, 8, FN_WIDTH), lambda i: (0, i // tps, 0, 0)),
            pl.BlockSpec((tm, tm), const2),
        ]
        g_args = [plus, minus, minus, mid, jnp.asarray(rev, BF16)]
    else:
        in_specs += [row(FN_WIDTH)]
        g_args = [g]
    in_specs += [
        pl.BlockSpec((None, 3, CONV_WIDTH), lsel),
        pl.BlockSpec((None, FN_WIDTH, FN_WIDTH), lsel),
        pl.BlockSpec((None, d, d), lsel),
        pl.BlockSpec((None, 1, d), lsel),
        pl.BlockSpec((d, 2 * LANES), const2),
        pl.BlockSpec((1, LANES), const2),
    ]
    out_specs = [
        row(d),
        pl.BlockSpec((MOE_PIECES, tm, d // 2 // MOE_PIECES), lambda i: (0, i, 0)),
        pl.BlockSpec((8, tm), lambda i: (0, i)),
        row(LANES),
        pl.BlockSpec((N_EXPERTS, LANES), const2),
    ]
    out_shape = [
        jax.ShapeDtypeStruct((t, d), F32),
        jax.ShapeDtypeStruct((MOE_PIECES, t, d // 2 // MOE_PIECES), jnp.uint32),
        jax.ShapeDtypeStruct((8, t), F32),
        jax.ShapeDtypeStruct((t, LANES), F32),
        jax.ShapeDtypeStruct((N_EXPERTS, LANES), F32),
    ]
    return pl.pallas_call(
        functools.partial(_mixout_kernel, n, half),
        grid=(t // tm,),
        in_specs=in_specs,
        out_specs=out_specs,
        out_shape=out_shape,
        compiler_params=_params("arbitrary"),
        name="mixout",
    )(x, mod, ab, z, z, z, yna, ymla, *g_args, wts["conv_w"], wts["w_fn"], wts["w_out"], wts["norm2"],
      wts["wr_cat"], wts["b_router"])


def _slot_positions(route, counts, rb):
    cnt = counts[:, 0].astype(jnp.int32)
    padded = (cnt + rb - 1) // rb * rb
    ends = jnp.cumsum(padded)
    offs = ends - padded
    experts = route[2:4].astype(jnp.int32)
    ranks = route[4:6].astype(jnp.int32)
    pos = ranks
    for e in range(N_EXPERTS):
        pos = pos + jnp.where(experts == e, offs[e], 0)
    nblk = (2 * route.shape[1]) // rb + N_EXPERTS
    starts = jnp.arange(nblk, dtype=jnp.int32) * rb
    blk_expert = jnp.sum((starts[:, None] >= ends[None, :]).astype(jnp.int32), axis=1)
    used = blk_expert < N_EXPERTS
    blk_expert = jnp.where(used, blk_expert, 0)
    valid_end = jnp.sum(jnp.where(blk_expert[:, None] == jnp.arange(N_EXPERTS)[None, :], (offs + cnt)[None, :], 0), axis=1)
    blk_valid = jnp.where(used, jnp.clip(valid_end - starts, 0, rb), 0)
    return pos, jnp.stack([blk_expert, blk_valid])


def _sc_mesh():
    return plsc.VectorSubcoreMesh(core_axis_name="c", subcore_axis_name="s")


def _sc_pipeline(body, nwin, in_specs, out_specs):
    return pltpu.emit_pipeline(body, grid=(nwin,), in_specs=in_specs, out_specs=out_specs,
                               core_axis_name=("c", "s"), dimension_semantics=(pltpu.PARALLEL,))


def _row_scatter(table, idx_a, idx_b, nrows):
    b, w = table.shape
    win = SC_WINDOW
    idx_spec = pl.BlockSpec((1, win), lambda i: (0, i))

    @functools.partial(pl.kernel, out_type=jax.ShapeDtypeStruct((nrows, w), table.dtype), mesh=_sc_mesh(),
                       scratch_types=[])
    def scatter(table_hbm, ia_hbm, ib_hbm, out_hbm):
        def body(rows_vmem, ia_vmem, ib_vmem):
            pltpu.sync_copy(rows_vmem, out_hbm.at[ia_vmem.at[0]])
            pltpu.sync_copy(rows_vmem, out_hbm.at[ib_vmem.at[0]])

        _sc_pipeline(body, b // win, [pl.BlockSpec((win, w), lambda i: (i, 0)), idx_spec, idx_spec], [])(
            table_hbm, ia_hbm, ib_hbm)

    return scatter(table, idx_a.reshape(1, b), idx_b.reshape(1, b))


def _row_gather(table, idx):
    b = idx.shape[0]
    w = table.shape[1]
    win = SC_WINDOW

    @functools.partial(pl.kernel, out_type=jax.ShapeDtypeStruct((b, w), table.dtype), mesh=_sc_mesh(),
                       scratch_types=[])
    def gather(table_hbm, idx_hbm, out_hbm):
        def body(idx_vmem, out_vmem):
            pltpu.sync_copy(table_hbm.at[idx_vmem.at[0]], out_vmem)

        _sc_pipeline(body, b // win, [pl.BlockSpec((1, win), lambda i: (0, i))],
                     [pl.BlockSpec((win, w), lambda i: (i, 0))])(idx_hbm, out_hbm)

    return gather(table, idx.reshape(1, b))


def _ffn_kernel(blk_ref, xs_ref, w13_ref, w2_ref, y_ref):
    i = pl.program_id(0)
    e = blk_ref[0, i]
    nvalid = blk_ref[1, i]

    @pl.when(nvalid > 0)
    def _():
        packed = jnp.concatenate([xs_ref[0], xs_ref[1]], axis=-1)
        live = lax.broadcasted_iota(jnp.int32, packed.shape, 0) < nvalid
        xb = _unpack_pairs(jnp.where(live, packed, jnp.uint32(0))).astype(BF16)
        up = _dot(xb, w13_ref[e])
        a, b = up[:, :EXPERT_FF], up[:, EXPERT_FF:]
        hid = (a * jax.nn.sigmoid(a)) * b
        y = _pack_pairs(_dot(hid.astype(BF16), w2_ref[e]))
        half = y.shape[1] // 2
        y_ref[0] = y[:, :half]
        y_ref[1] = y[:, half:]

    @pl.when(nvalid == 0)
    def _():
        y_ref[...] = jnp.zeros(y_ref.shape, y_ref.dtype)


def _expert_ffn(xs, blk, w13, w2, layer, rb):
    pieces, nrows, w = xs.shape
    d = 2 * pieces * w
    resident = dict(pipeline_mode=pl.Buffered(1))
    used = lambda i, blk: (0, jnp.where(blk[1, i] > 0, i, 0), 0)
    return pl.pallas_call(
        _ffn_kernel,
        grid_spec=pltpu.PrefetchScalarGridSpec(
            num_scalar_prefetch=1,
            grid=(nrows // rb,),
            in_specs=[
                pl.BlockSpec((pieces, rb, w), used),
                pl.BlockSpec((None, N_EXPERTS, d, 2 * EXPERT_FF), lambda i, blk: (layer, 0, 0, 0), **resident),
                pl.BlockSpec((None, N_EXPERTS, EXPERT_FF, d), lambda i, blk: (layer, 0, 0, 0), **resident),
            ],
            out_specs=pl.BlockSpec((pieces, rb, w), lambda i, blk: (0, i, 0)),
        ),
        out_shape=jax.ShapeDtypeStruct(xs.shape, xs.dtype),
        compiler_params=_params("parallel"),
        name="expert_ffn",
    )(blk, xs, w13, w2)


def _combine_kernel(final, x1_ref, gtok_ref, mod_ref, nf_ref, y_ref, o_ref):
    out = _moe_residual(x1_ref[...], gtok_ref, y_ref, mod_ref)
    if final:
        out = _rms(out, nf_ref[...])
    o_ref[...] = out


def _combine(x1, gtok, y_tok, mod, layer, n, norm_f, final, lat, tc=512):
    t, d = x1.shape
    cond_row_of_tile = _cond_row(lat, tc, n, mod.shape[1] - 1)
    row = lambda w: pl.BlockSpec((tc, w), lambda i: (i, 0))
    return pl.pallas_call(
        functools.partial(_combine_kernel, final),
        grid=(t // tc,),
        in_specs=[
            row(d),
            row(LANES),
            pl.BlockSpec((None, None, 6, d), lambda i: (layer, cond_row_of_tile(i), 0, 0)),
            pl.BlockSpec((1, d), lambda i: (0, 0)),
            pl.BlockSpec(y_tok.shape[:2] + (tc, y_tok.shape[3]), lambda i: (0, 0, i, 0)),
        ],
        out_specs=row(d),
        out_shape=jax.ShapeDtypeStruct((t, d), F32),
        compiler_params=_params("parallel"),
        name="combine",
    )(x1, gtok, mod, norm_f, y_tok)


def _moe_layer(x, mod, layer, n, parts, wts, lat, tm, rb):
    t = x.shape[0]
    x1, h2, route, gtok, counts = _mixout(x, mod, layer, n, parts, wts, lat, tm)
    pos, blk = _slot_positions(route, counts, rb)
    pieces, _, w = h2.shape
    nrows = blk.shape[1] * rb
    piece_base = (jnp.arange(pieces, dtype=jnp.int32) * nrows)[:, None]
    idx = [(piece_base + pos[s][None, :]).reshape(-1) for s in range(2)]
    xs = _row_scatter(h2.reshape(pieces * t, w), idx[0], idx[1], pieces * nrows).reshape(pieces, nrows, w)
    y = _expert_ffn(xs, blk, wts["w13"], wts["w2"], layer, rb)
    back = (piece_base[:, :, None] + pos[None, :, :]).reshape(-1)
    y_tok = _row_gather(y.reshape(pieces * nrows, w), back).reshape(pieces, 2, t, w)
    return x1, (gtok, y_tok)


def _swap_halves(w):
    nf = MLA_ROPE_DIM // 4
    idx = np.arange(MLA_ROPE_DIM).reshape(2, 2, nf)[:, ::-1, :].reshape(-1)
    return w[..., idx]


def _pack_weights(w_in, mla_wq_up, mla_wkv_up, w1, w3, w2, w_router, b_router):
    depth, d, _ = w_in.shape
    zeros = lambda w: jnp.zeros((depth, d, w), w_in.dtype)
    w_kr = w_in[..., 1920:1952]
    pad_rope = lambda w: jnp.concatenate([zeros(MLA_NOPE_DIM), w, zeros(MLA_QK_PAD - MLA_NOPE_DIM - MLA_ROPE_DIM)], -1)
    w_main = jnp.concatenate([w_in[..., :1920], pad_rope(w_kr), pad_rope(_swap_halves(w_kr)), w_in[..., 1952:]], -1)

    wq = mla_wq_up.reshape(depth, MLA_Q_LORA, MLA_HEADS, MLA_NOPE_DIM + MLA_ROPE_DIM)
    q_nope, q_rope = wq[..., :MLA_NOPE_DIM], wq[..., MLA_NOPE_DIM:]
    tail = jnp.zeros(q_rope.shape[:-1] + (MLA_QK_PAD - MLA_NOPE_DIM - MLA_ROPE_DIM,), wq.dtype)
    wq_a = jnp.concatenate([q_nope, q_rope, tail], -1).reshape(depth, MLA_Q_LORA, -1)
    wq_b = jnp.concatenate([jnp.zeros_like(q_nope), _swap_halves(q_rope), tail], -1).reshape(depth, MLA_Q_LORA, -1)

    wkv = mla_wkv_up.reshape(depth, MLA_KV_LORA, MLA_HEADS, MLA_NOPE_DIM + MLA_V_DIM)
    k_nope, v_up = wkv[..., :MLA_NOPE_DIM], wkv[..., MLA_NOPE_DIM:]
    k_tail = jnp.zeros(k_nope.shape[:-1] + (MLA_QK_PAD - MLA_NOPE_DIM,), wkv.dtype)
    wk_a = jnp.concatenate([k_nope, k_tail], -1).reshape(depth, MLA_KV_LORA, -1)
    wv = v_up.reshape(depth, MLA_KV_LORA, -1)
    v_tail = jnp.zeros(v_up.shape[:-1] + (MLA_V_PAD - MLA_V_DIM,), wkv.dtype)
    wv_ext = jnp.concatenate([v_up, v_tail], -1).reshape(depth, MLA_KV_LORA, -1)
    vone = np.zeros((1, MLA_HEADS * MLA_V_PAD), np.float32)
    vone[0, MLA_V_DIM::MLA_V_PAD] = 1.0

    wr = jnp.pad(w_router, ((0, 0), (0, LANES - N_EXPERTS)))
    wr_hi = wr.astype(BF16)
    wr_lo = (wr - wr_hi.astype(F32)).astype(BF16)
    return {
        "w_in": w_main.astype(BF16), "wq_a": wq_a.astype(BF16), "wq_b": wq_b.astype(BF16),
        "wk_a": wk_a.astype(BF16), "wv": wv.astype(BF16), "wv_ext": wv_ext.astype(BF16),
        "vone_ext": jnp.asarray(vone),
        "w13": jnp.concatenate([w1, w3], -1).astype(BF16), "w2": w2.astype(BF16),
        "wr_cat": jnp.concatenate([wr_hi, wr_lo], axis=-1),
        "b_router": jnp.pad(b_router, (0, LANES - N_EXPERTS)).reshape(1, LANES).astype(F32),
    }


def _channel_dft():
    c = np.arange(FN_GROUP_DIM)
    ang = 2.0 * np.pi * ((c[:, None] * c[None, :]) % FN_GROUP_DIM) / FN_GROUP_DIM
    out = np.zeros((FN_WIDTH, 2 * FN_WIDTH), np.float32)
    for g in range(FN_GROUPS):
        sl = slice(g * FN_GROUP_DIM, (g + 1) * FN_GROUP_DIM)
        out[sl, sl] = np.cos(ang) * FN_GROUP_DIM ** -0.5
        out[sl, FN_WIDTH + g * FN_GROUP_DIM:FN_WIDTH + (g + 1) * FN_GROUP_DIM] = np.sin(ang) * FN_GROUP_DIM ** -0.5
    return jnp.asarray(out, BF16)


def _rope_tables(n):
    tok = jnp.arange(n)
    pos = jnp.stack([tok // GRID_W, tok % GRID_W], axis=-1).astype(F32)
    nf = MLA_ROPE_DIM // 4
    freqs = ROPE_THETA ** (-jnp.arange(nf, dtype=F32) / nf)
    ang = pos[:, :, None] * freqs
    cos = jnp.broadcast_to(jnp.cos(ang)[:, :, None, :], (n, 2, 2, nf)).reshape(n, MLA_ROPE_DIM)
    sin = jnp.sin(ang)
    sin = jnp.stack([-sin, sin], axis=2).reshape(n, MLA_ROPE_DIM)
    pad = jnp.zeros((n, MLA_QK_PAD - MLA_NOPE_DIM - MLA_ROPE_DIM), F32)
    cos_t = jnp.concatenate([jnp.ones((n, MLA_NOPE_DIM), F32), cos, pad], -1)
    sin_t = jnp.concatenate([jnp.zeros((n, MLA_NOPE_DIM), F32), sin, pad], -1)
    return cos_t, sin_t


def kernel(x_prompt, x_sample, cache_na_k, cache_na_v, cache_mla_ckv, cache_mla_krope, c, c_ctx, w_ada, b_ada,
           norm1, norm2, w_in, conv_w, na_rpb, mla_gq, mla_wq_up, mla_gkv, mla_wkv_up, w_fn, w_out, w_router,
           b_router, w1, w3, w2, norm_f):
    bp, seq, d = x_prompt.shape
    bd, dec_seq, _ = x_sample.shape
    depth = w_in.shape[0]

    wts = _pack_weights(w_in, mla_wq_up, mla_wkv_up, w1, w3, w2, w_router, b_router)
    wts.update({
        "norm1": norm1.reshape(depth, 1, d), "norm2": norm2.reshape(depth, 1, d),
        "mla_gq": mla_gq.reshape(depth, 1, -1), "mla_gkv": mla_gkv.reshape(depth, 1, -1),
        "conv_w": conv_w, "w_fn": w_fn.astype(BF16), "w_out": w_out.astype(BF16),
        "norm_f": norm_f.reshape(1, d), "cs_bd": _channel_dft(),
    })

    cond = jnp.concatenate([c, jnp.zeros((-(bd + 1) % 8, d), c.dtype), c_ctx[None, :]], axis=0)
    mod = _ada_modulation(cond, w_ada, b_ada)

    xp = x_prompt.reshape(bp * seq, d)
    tables = _dft_tables(seq)
    caches = None
    pending = None
    for layer in range(depth):
        outs = _premix(xp, mod, layer, seq, wts, None, False, seq, caches, pending)
        ab, z, qn, kn, vn, km, fab, qm, vm = outs[:9]
        caches = outs[9:13]
        if pending is not None:
            xp = outs[13]
        yna, ymla = _ctx_attention(qn, kn, vn, qm, km, vm, seq)
        g = _fourier(fab, tables, seq)
        xp, pending = _moe_layer(xp, mod, layer, seq, (ab, z, yna, ymla, g), wts, False, seq, MOE_ROW_BLOCK)
    xp = _combine(xp, pending[0], pending[1], mod, depth - 1, seq, wts["norm_f"], True, False, seq)
    new_na_k, new_na_v, new_ckv, new_krope = caches

    xs = x_sample.reshape(bd * dec_seq, d)
    kx, vxt = _ctx_kv(cache_mla_ckv, cache_mla_krope, wts["wk_a"], wts["wv_ext"], wts["vone_ext"])
    na_bias = _na_bias(na_rpb, dec_seq // GRID_W)
    rope = _rope_tables(dec_seq)
    tables = _dft_tables(dec_seq)
    pending = None
    for layer in range(depth):
        outs = _premix(xs, mod, layer, dec_seq, wts, rope, True, TM_LAT_PREMIX, None, pending)
        ab, z, qn, kn, vn, km, fab, qt, vt = outs[:9]
        if pending is not None:
            xs = outs[9]
        yna = _na_lat_attention(qn, kn, vn, cache_na_k, cache_na_v, na_bias, layer, dec_seq)
        ymla = _mla_lat_attention(qt, km, vt, kx, vxt, layer, dec_seq)
        g = _fourier(fab, tables, dec_seq)
        xs, pending = _moe_layer(xs, mod, layer, dec_seq, (ab, z, yna, ymla, g), wts, True, TM_LAT_MIXOUT,
                                 MOE_ROW_BLOCK)
    xs = _combine(xs, pending[0], pending[1], mod, depth - 1, dec_seq, wts["norm_f"], True, True, TM_LAT_PREMIX)

    return (xp.reshape(bp, seq, d), xs.reshape(bd, dec_seq, d), new_na_k, new_na_v, new_ckv, new_krope)
```

```python
import functools
import math

import numpy as np
import jax
import jax.numpy as jnp
from jax import lax
from jax.experimental import pallas as pl
from jax.experimental.pallas import tpu as pltpu
from jax.experimental.pallas import tpu_sc as plsc

F32 = jnp.float32
BF16 = jnp.bfloat16

GRID_W = 64
CONV_WIDTH = 256
NA_HEADS = 4
NA_HEAD_DIM = 64
NA_WIDTH = NA_HEADS * NA_HEAD_DIM
NA_KH = 8
NA_KW = 16
MLA_HEADS = 4
MLA_Q_LORA = 256
MLA_KV_LORA = 128
MLA_NOPE_DIM = 64
MLA_ROPE_DIM = 32
MLA_V_DIM = 64
MLA_QK_PAD = 128
MLA_V_PAD = 96
MLA_KEY_SUB = 256
LOG2E = 1.4426950408889634
FN_GROUPS = 4
FN_GROUP_DIM = 64
FN_WIDTH = FN_GROUPS * FN_GROUP_DIM
N_EXPERTS = 16
N_EXPERT_GROUPS = 4
EXPERTS_PER_GROUP = N_EXPERTS // N_EXPERT_GROUPS
EXPERT_FF = 256
ROPE_THETA = 10000.0
EPS = 1e-6
NEG_INF = -1e30
LANES = 128

NA_SCALE = NA_HEAD_DIM ** -0.5
MLA_SCALE = (MLA_NOPE_DIM + MLA_ROPE_DIM) ** -0.5

NA_Q_ROWS = 4
NA_WIN_ROWS = 12

TM_LAT_PREMIX = 512
TM_LAT_MIXOUT = 512
TM_LAT_COMBINE = 1024
MOE_ROW_BLOCK = 512
SC_WINDOW = 128
MOE_PIECES = 2

V7X_VMEM_BYTES = 64 * 1024 * 1024
VMEM_LIMIT = V7X_VMEM_BYTES - 8 * 1024 * 1024

_C_AB, _C_AC, _C_AU, _C_Q, _C_K, _C_V, _C_CQ = 0, 256, 512, 768, 1024, 1280, 1536
_C_CKV, _C_KR, _C_KRS, _C_FU, _C_END = 1792, 1920, 2048, 2176, 2432


def _nt_dot(a, b):
    return lax.dot_general(a, b, (((1,), (1,)), ((), ())), preferred_element_type=F32)


def _dot(a, b):
    return jnp.dot(a, b, preferred_element_type=F32)


def _rms(x, g):
    return x * lax.rsqrt(jnp.mean(x * x, axis=-1, keepdims=True) + EPS) * g


def _params(*sem, flags=None):
    return pltpu.CompilerParams(dimension_semantics=sem, vmem_limit_bytes=VMEM_LIMIT, flags=flags)


def _ada_kernel(c_ref, w_ref, b_ref, o_ref):
    cnd = c_ref[...]
    act = cnd * jax.nn.sigmoid(cnd)
    o_ref[...] = _dot(act.astype(BF16), w_ref[...].astype(BF16)) + b_ref[...]


def _ada_modulation(cond, w_ada, b_ada):
    depth, d, six_d = w_ada.shape
    r = cond.shape[0]
    tn = 1024
    out = pl.pallas_call(
        _ada_kernel,
        grid=(depth, six_d // tn),
        in_specs=[
            pl.BlockSpec((r, d), lambda l, j: (0, 0)),
            pl.BlockSpec((None, d, tn), lambda l, j: (l, 0, j)),
            pl.BlockSpec((None, 1, tn), lambda l, j: (l, 0, j)),
        ],
        out_specs=pl.BlockSpec((None, r, tn), lambda l, j: (l, 0, j)),
        out_shape=jax.ShapeDtypeStruct((depth, r, six_d), F32),
        compiler_params=_params("parallel", "parallel"),
        name="ada_modulation",
    )(cond, w_ada, b_ada.reshape(depth, 1, six_d))
    return out.reshape(depth, r, 6, d)


def _unpack_pairs(p):
    hi = pltpu.bitcast(p & jnp.uint32(0xFFFF0000), F32)
    lo = pltpu.bitcast(p << 16, F32)
    return jnp.concatenate([hi, lo], axis=-1)


def _moe_residual(x1, gtok_ref, y_ref, mod_ref):
    g = gtok_ref[...]
    y_lo = _unpack_pairs(jnp.concatenate([y_ref[0, 0], y_ref[1, 0]], axis=-1))
    y_hi = _unpack_pairs(jnp.concatenate([y_ref[0, 1], y_ref[1, 1]], axis=-1))
    return x1 + mod_ref[...][5:6] * (g[:, 0:1] * y_lo + g[:, 1:2] * y_hi)


def _premix_kernel(lat, fused, cache_layer, n_in, *refs):
    (x_ref, mod_ref, g1_ref, w_ref, gq_ref, wqa_ref, wqb_ref, gkv_ref, wka_ref, wv_ref, vone_ref, cs_ref,
     cos_ref, sin_ref) = refs[:14]
    outs = refs[n_in:]
    (ab_ref, z_ref, qn_ref, kn_ref, vn_ref, km_ref, fab_ref) = outs[:7]

    x = x_ref[...]
    if fused:
        gtok_ref, y_ref, modp_ref = refs[n_in - 3:n_in]
        x = _moe_residual(x, gtok_ref, y_ref, modp_ref)
        outs[-1][...] = x
    mod = mod_ref[...]
    h = _rms(x, g1_ref[...]) * (1.0 + mod[1:2]) + mod[0:1]
    p = _dot(h.astype(BF16), w_ref[...])

    ab_ref[...] = p[:, _C_AB:_C_AC].astype(BF16)
    z_ref[...] = (p[:, _C_AC:_C_AU] * p[:, _C_AU:_C_Q]).astype(BF16)
    k_na = p[:, _C_K:_C_V]
    v_na = p[:, _C_V:_C_CQ]
    qn_ref[...] = (p[:, _C_Q:_C_K] * NA_SCALE).astype(BF16)
    kn_ref[...] = k_na.astype(BF16)
    vn_ref[...] = v_na.astype(BF16)

    cqn = _rms(p[:, _C_CQ:_C_CKV], gq_ref[...]).astype(BF16)
    ckvn = _rms(p[:, _C_CKV:_C_KR], gkv_ref[...])
    ckvn_b = ckvn.astype(BF16)
    qa = _dot(cqn, wqa_ref[...])
    kva = _dot(ckvn_b, wka_ref[...])
    v_mla = _dot(ckvn_b, wv_ref[...]) + vone_ref[...]
    kr = p[:, _C_KR:_C_KRS]
    if lat:
        cos = cos_ref[...]
        sin = sin_ref[...]
        qb = _dot(cqn, wqb_ref[...])
        krot = kr * cos + p[:, _C_KRS:_C_FU] * sin
        qt_ref, vt_ref = outs[7:9]
    else:
        krot = kr
        qm_ref, vm_ref, ck_ref, cv_ref, cckv_ref, ckr_ref = outs[7:13]
    for hd in range(MLA_HEADS):
        sl = slice(hd * MLA_QK_PAD, (hd + 1) * MLA_QK_PAD)
        km_ref[:, sl] = (kva[:, sl] + krot).astype(BF16)
        if lat:
            qh = (qa[:, sl] * cos + qb[:, sl] * sin) * (MLA_SCALE * LOG2E)
            qt_ref[sl, :] = jnp.transpose(qh).astype(BF16)
        else:
            qm_ref[:, sl] = (qa[:, sl] * MLA_SCALE).astype(BF16)

    fab_ref[...] = _dot(p[:, _C_FU:_C_END].astype(BF16), cs_ref[...]).astype(BF16)

    if lat:
        for j in range(v_mla.shape[1] // LANES):
            sl = slice(j * LANES, (j + 1) * LANES)
            vt_ref[sl, :] = jnp.transpose(v_mla[:, sl]).astype(BF16)
    else:
        vm_ref[...] = v_mla.astype(BF16)
        nseq, n = cckv_ref.shape[0], cckv_ref.shape[-2]
        at = () if cache_layer is None else (cache_layer,)
        if cache_layer is not None:
            for ref in (ck_ref, cv_ref, cckv_ref, ckr_ref):
                ref[...] = jnp.zeros(ref.shape, ref.dtype)
        for s in range(nseq):
            rows = slice(s * n, (s + 1) * n)
            for hd in range(NA_HEADS):
                sl = slice(hd * NA_HEAD_DIM, (hd + 1) * NA_HEAD_DIM)
                ck_ref[(s,) + at + (hd,)] = k_na[rows, sl]
                cv_ref[(s,) + at + (hd,)] = v_na[rows, sl]
            cckv_ref[(s,) + at] = ckvn[rows]
            ckr_ref[(s,) + at] = kr[rows, MLA_NOPE_DIM:MLA_NOPE_DIM + MLA_ROPE_DIM]


def _cond_row(lat, tm, n, ctx_row):
    return (lambda i: (i * tm) // n) if lat else (lambda i: ctx_row)


def _premix(x, mod, layer, n, wts, rope, lat, tm, caches=None, pending=None):
    t, d = x.shape
    if lat:
        cos_t, sin_t = rope
        wv, vone = wts["wv_ext"], wts["vone_ext"]
    else:
        cos_t = sin_t = jnp.zeros((8, LANES), F32)
        wv, vone = wts["wv"], jnp.zeros((1, MLA_HEADS * MLA_V_DIM), F32)
    vw = wv.shape[-1]
    qw = MLA_HEADS * MLA_QK_PAD
    cache_layer = None
    tiles_per_seq = n // tm
    cond_row = _cond_row(lat, tm, n, mod.shape[1] - 1)
    const = lambda *_: (0, 0)
    lsel = lambda *_: (layer, 0, 0)
    rope_spec = (pl.BlockSpec((tm, LANES), lambda i: (i % tiles_per_seq, 0)) if lat
                 else pl.BlockSpec((8, LANES), const))
    in_specs = [
        pl.BlockSpec((tm, d), lambda i: (i, 0)),
        pl.BlockSpec((None, None, 6, d), lambda i: (layer, cond_row(i), 0, 0)),
        pl.BlockSpec((None, 1, d), lsel),
        pl.BlockSpec((None, d, _C_END), lsel),
        pl.BlockSpec((None, 1, MLA_Q_LORA), lsel),
        pl.BlockSpec((None, MLA_Q_LORA, qw), lsel),
        pl.BlockSpec((None, MLA_Q_LORA, qw), lsel),
        pl.BlockSpec((None, 1, MLA_KV_LORA), lsel),
        pl.BlockSpec((None, MLA_KV_LORA, qw), lsel),
        pl.BlockSpec((None, MLA_KV_LORA, vw), lsel),
        pl.BlockSpec((1, vw), const),
        pl.BlockSpec((FN_WIDTH, 2 * FN_WIDTH), const),
        rope_spec,
        rope_spec,
    ]
    row = lambda w: pl.BlockSpec((tm, w), lambda i: (i, 0))
    widths = [CONV_WIDTH, CONV_WIDTH, NA_WIDTH, NA_WIDTH, NA_WIDTH, qw, 2 * FN_WIDTH]
    out_specs = [row(w) for w in widths]
    out_shape = [jax.ShapeDtypeStruct((t, w), BF16) for w in widths]
    if lat:
        out_specs += [pl.BlockSpec((None, qw, tm), lambda i: (i, 0, 0)),
                      pl.BlockSpec((None, vw, tm), lambda i: (i, 0, 0))]
        out_shape += [jax.ShapeDtypeStruct((t // tm, qw, tm), BF16),
                      jax.ShapeDtypeStruct((t // tm, vw, tm), BF16)]
    else:
        assert tm % n == 0
        b = t // n
        spt = tm // n
        depth = wts["w_in"].shape[0]
        if caches is None:
            cache_layer = layer
            lay, at = depth, (lambda i, *rest: (i, 0) + rest)
        else:
            lay, at = None, (lambda i, *rest: (i, layer) + rest)
        out_specs += [
            row(qw), row(vw),
            pl.BlockSpec((spt, lay, NA_HEADS, n, NA_HEAD_DIM), lambda i: at(i, 0, 0, 0)),
            pl.BlockSpec((spt, lay, NA_HEADS, n, NA_HEAD_DIM), lambda i: at(i, 0, 0, 0)),
            pl.BlockSpec((spt, lay, n, MLA_KV_LORA), lambda i: at(i, 0, 0)),
            pl.BlockSpec((spt, lay, n, MLA_ROPE_DIM), lambda i: at(i, 0, 0)),
        ]
        out_shape += [
            jax.ShapeDtypeStruct((t, qw), BF16), jax.ShapeDtypeStruct((t, vw), BF16),
            jax.ShapeDtypeStruct((b, depth, NA_HEADS, n, NA_HEAD_DIM), F32),
            jax.ShapeDtypeStruct((b, depth, NA_HEADS, n, NA_HEAD_DIM), F32),
            jax.ShapeDtypeStruct((b, depth, n, MLA_KV_LORA), F32),
            jax.ShapeDtypeStruct((b, depth, n, MLA_ROPE_DIM), F32),
        ]
    args = [x, mod, wts["norm1"], wts["w_in"], wts["mla_gq"], wts["wq_a"], wts["wq_b"], wts["mla_gkv"],
            wts["wk_a"], wv, vone, wts["cs_bd"], cos_t, sin_t]
    aliases = {}
    if caches is not None:
        first_cache_out = len(out_shape) - len(caches)
        aliases = {len(args) + j: first_cache_out + j for j in range(len(caches))}
        in_specs += [pl.BlockSpec(memory_space=pl.ANY)] * len(caches)
        args += list(caches)
    if pending is not None:
        gtok, y_tok = pending
        in_specs += [
            row(LANES),
            pl.BlockSpec(y_tok.shape[:2] + (tm, y_tok.shape[3]), lambda i: (0, 0, i, 0)),
            pl.BlockSpec((None, None, 6, d), lambda i: (layer - 1, cond_row(i), 0, 0)),
        ]
        args += [gtok, y_tok, mod]
        out_specs = out_specs + [row(d)]
        out_shape = out_shape + [jax.ShapeDtypeStruct((t, d), F32)]
    return pl.pallas_call(
        functools.partial(_premix_kernel, lat, pending is not None, cache_layer, len(args)),
        grid=(t // tm,),
        in_specs=in_specs,
        out_specs=out_specs,
        out_shape=out_shape,
        input_output_aliases=aliases,
        compiler_params=_params("parallel"),
        name="premix_lat" if lat else "premix_ctx",
    )(*args)


def _ctx_attn_kernel(qn_ref, kn_ref, vn_ref, qm_ref, km_ref, vm_ref, yna_ref, ymla_ref, s_scr, p_scr):
    na = [slice(hd * NA_HEAD_DIM, (hd + 1) * NA_HEAD_DIM) for hd in range(NA_HEADS)]
    mq = [slice(hd * MLA_QK_PAD, (hd + 1) * MLA_QK_PAD) for hd in range(MLA_HEADS)]
    mv = [slice(hd * MLA_V_DIM, (hd + 1) * MLA_V_DIM) for hd in range(MLA_HEADS)]
    qk = [(qn_ref[:, sl], kn_ref[:, sl]) for sl in na] + [(qm_ref[:, sl], km_ref[:, sl]) for sl in mq]
    m, l = [], []
    for j, (q, k) in enumerate(qk):
        s = _nt_dot(q, k)
        s_scr[j] = s
        m.append(jnp.max(s, axis=-1, keepdims=True))
    for j in range(len(qk)):
        p = jnp.exp(s_scr[j] - m[j])
        l.append(jnp.sum(p, axis=-1, keepdims=True))
        p_scr[j] = p.astype(BF16)
    for hd, sl in enumerate(na):
        yna_ref[:, sl] = (_dot(p_scr[hd], vn_ref[:, sl]) / l[hd]).astype(BF16)
    for hd, sl in enumerate(mv):
        j = NA_HEADS + hd
        ymla_ref[:, sl] = (_dot(p_scr[j], vm_ref[:, sl]) / l[j]).astype(BF16)


def _ctx_attention(qn, kn, vn, qm, km, vm, n):
    t = qn.shape[0]
    spec = lambda w: pl.BlockSpec((n, w), lambda b: (b, 0))
    ins = [qn, kn, vn, qm, km, vm]
    return pl.pallas_call(
        _ctx_attn_kernel,
        grid=(t // n,),
        in_specs=[spec(a.shape[1]) for a in ins],
        out_specs=[spec(NA_WIDTH), spec(MLA_HEADS * MLA_V_DIM)],
        out_shape=[jax.ShapeDtypeStruct((t, NA_WIDTH), BF16),
                   jax.ShapeDtypeStruct((t, MLA_HEADS * MLA_V_DIM), BF16)],
        scratch_shapes=[pltpu.VMEM((NA_HEADS + MLA_HEADS, n, n), F32),
                        pltpu.VMEM((NA_HEADS + MLA_HEADS, n, n), BF16)],
        compiler_params=_params("parallel"),
        name="ctx_attention",
    )(*ins)


def _mla_lat_kernel(qt_ref, k_ref, vt_ref, kx_ref, vxt_ref, o_ref, s_scr, p_scr):
    nchunk, _, kc = vt_ref.shape
    tq = qt_ref.shape[1]
    sub = MLA_KEY_SUB

    ksl = lambda hd: slice(hd * MLA_QK_PAD, (hd + 1) * MLA_QK_PAD)
    vsl = lambda hd: slice(hd * MLA_V_PAD, (hd + 1) * MLA_V_PAD)

    def scores(slot, k_of, nk):
        cmax = []
        for hd in range(MLA_HEADS):
            qt = qt_ref[ksl(hd), :]
            part = None
            for j in range(0, nk, sub):
                st = _dot(k_of(hd, j), qt)
                s_scr[slot, hd, j:j + sub, :] = st
                blk = jnp.max(st.reshape(sub // 8, 8, tq), axis=0)
                part = blk if part is None else jnp.maximum(part, blk)
            cmax.append(jnp.max(part, axis=0, keepdims=True))
        return tuple(cmax)

    def attend(slot, cmax, state, vt_of, nk):
        new = []
        for hd in range(MLA_HEADS):
            m_i, acc = state[hd]
            m_new = jnp.maximum(m_i, cmax[hd])
            for j in range(0, nk, sub):
                p_scr[hd, j:j + sub, :] = jnp.exp2(s_scr[slot, hd, j:j + sub, :] - m_new).astype(BF16)
            acc = jnp.exp2(m_i - m_new) * acc + _dot(vt_of(hd), p_scr[hd, 0:nk, :])
            new.append((m_new, acc))
        return tuple(new)

    lat_keys = lambda c: (lambda hd, j: k_ref[pl.ds(pl.multiple_of(c * kc, kc) + j, sub), ksl(hd)])
    past = kx_ref.shape[0]
    state = tuple((jnp.full((1, tq), NEG_INF, F32), jnp.zeros((MLA_V_PAD, tq), F32)) for _ in range(MLA_HEADS))
    cmax_ctx = scores(1, lambda hd, j: kx_ref[j:j + sub, ksl(hd)], past)
    cmax = scores(0, lat_keys(0), kc)
    state = attend(1, cmax_ctx, state, lambda hd: vxt_ref[vsl(hd), :], past)

    lat_vals = lambda c: (lambda hd: vt_ref[c, vsl(hd), :])

    def body(i, carry):
        cmax0, state = carry
        c = 2 * i
        cmax1 = scores(1, lat_keys(c + 1), kc)
        state = attend(0, cmax0, state, lat_vals(c), kc)
        cmax0 = scores(0, lat_keys(c + 2), kc)
        state = attend(1, cmax1, state, lat_vals(c + 1), kc)
        return cmax0, state

    cmax, state = lax.fori_loop(0, nchunk // 2 - 1, body, (cmax, state))
    cmax1 = scores(1, lat_keys(nchunk - 1), kc)
    state = attend(0, cmax, state, lat_vals(nchunk - 2), kc)
    state = attend(1, cmax1, state, lat_vals(nchunk - 1), kc)
    o_t = jnp.concatenate([acc[:MLA_V_DIM] / acc[MLA_V_DIM:MLA_V_DIM + 1] for _, acc in state], axis=0)
    o_ref[...] = jnp.transpose(o_t).astype(BF16)


def _mla_lat_attention(qt, km, vt, kx, vxt, layer, n):
    ntile, qw, tq = qt.shape
    t = ntile * tq
    past = kx.shape[2]
    qpb = n // tq
    return pl.pallas_call(
        _mla_lat_kernel,
        grid=(t // n, qpb),
        in_specs=[
            pl.BlockSpec((None, qw, tq), lambda b, i: (b * qpb + i, 0, 0)),
            pl.BlockSpec((n, km.shape[1]), lambda b, i: (b, 0)),
            pl.BlockSpec((qpb, vt.shape[1], tq), lambda b, i: (b, 0, 0)),
            pl.BlockSpec((None, None, past, kx.shape[3]), lambda b, i: (layer, b, 0, 0)),
            pl.BlockSpec((None, None, vxt.shape[2], past), lambda b, i: (layer, b, 0, 0)),
        ],
        out_specs=pl.BlockSpec((tq, MLA_HEADS * MLA_V_DIM), lambda b, i: (b * qpb + i, 0)),
        out_shape=jax.ShapeDtypeStruct((t, MLA_HEADS * MLA_V_DIM), BF16),
        scratch_shapes=[pltpu.VMEM((2, MLA_HEADS, max(tq, past), tq), F32),
                        pltpu.VMEM((MLA_HEADS, max(tq, past), tq), BF16)],
        compiler_params=_params("parallel", "parallel"),
        name="mla_lat_attention",
    )(qt, km, vt, kx, vxt)


def _ctx_kv_kernel(ckv_ref, kr_ref, wka_ref, wv_ref, vone_ref, place_ref, k_ref, vt_ref):
    ckv = ckv_ref[...].astype(BF16)
    k_ref[...] = (_dot(ckv, wka_ref[...]) + _dot(kr_ref[...].astype(BF16), place_ref[...])).astype(BF16)
    v = _dot(ckv, wv_ref[...]) + vone_ref[...]
    for j in range(v.shape[1] // LANES):
        sl = slice(j * LANES, (j + 1) * LANES)
        vt_ref[sl, :] = jnp.transpose(v[:, sl]).astype(BF16)


def _ctx_kv(cache_ckv, cache_krope, wk_a, wv_ext, vone_ext):
    bd, depth, past, _ = cache_ckv.shape
    place = np.zeros((MLA_ROPE_DIM, MLA_HEADS * MLA_QK_PAD), np.float32)
    for hd in range(MLA_HEADS):
        for i in range(MLA_ROPE_DIM):
            place[i, hd * MLA_QK_PAD + MLA_NOPE_DIM + i] = 1.0
    kw, vw = MLA_HEADS * MLA_QK_PAD, MLA_HEADS * MLA_V_PAD
    return pl.pallas_call(
        _ctx_kv_kernel,
        grid=(depth, bd),
        in_specs=[
            pl.BlockSpec((None, None, past, MLA_KV_LORA), lambda l, b: (b, l, 0, 0)),
            pl.BlockSpec((None, None, past, MLA_ROPE_DIM), lambda l, b: (b, l, 0, 0)),
            pl.BlockSpec((None, MLA_KV_LORA, kw), lambda l, b: (l, 0, 0)),
            pl.BlockSpec((None, MLA_KV_LORA, vw), lambda l, b: (l, 0, 0)),
            pl.BlockSpec((1, vw), lambda l, b: (0, 0)),
            pl.BlockSpec((MLA_ROPE_DIM, kw), lambda l, b: (0, 0)),
        ],
        out_specs=[pl.BlockSpec((None, None, past, kw), lambda l, b: (l, b, 0, 0)),
                   pl.BlockSpec((None, None, vw, past), lambda l, b: (l, b, 0, 0))],
        out_shape=[jax.ShapeDtypeStruct((depth, bd, past, kw), BF16),
                   jax.ShapeDtypeStruct((depth, bd, vw, past), BF16)],
        compiler_params=_params("parallel", "parallel"),
        name="ctx_kv",
    )(cache_ckv, cache_krope, wk_a, wv_ext, vone_ext, jnp.asarray(place, BF16))


def _na_tile_geometry(rows):
    last = rows // NA_Q_ROWS - 1
    geo = []
    for j in (0, 1, last):
        r0 = j * NA_Q_ROWS
        geo.append((r0, min(max(r0 - NA_KH // 2, 0), rows - NA_WIN_ROWS)))
    return geo


def _na_bias_kernel(geo, rows, rpb_ref, o_ref):
    l = pl.program_id(0)
    hd = pl.program_id(1)
    base = (l * NA_HEADS + hd) * (2 * NA_KH - 1) * (2 * NA_KW - 1)
    qc = lax.broadcasted_iota(jnp.int32, (GRID_W, GRID_W), 0)
    kcol = lax.broadcasted_iota(jnp.int32, (GRID_W, GRID_W), 1)
    d_col = jnp.clip(kcol - qc + (NA_KW - 1), 0, 2 * NA_KW - 2)
    col_start = jnp.clip(qc - NA_KW // 2, 0, GRID_W - NA_KW)
    in_cols = (kcol >= col_start) & (kcol < col_start + NA_KW)
    neg = jnp.full((GRID_W, GRID_W), NEG_INF, F32)
    tabs = []
    for dr in range(2 * NA_KH - 1):
        acc = jnp.zeros((GRID_W, GRID_W), F32)
        for dc in range(2 * NA_KW - 1):
            acc = jnp.where(d_col == dc, rpb_ref[base + dr * (2 * NA_KW - 1) + dc], acc)
        tabs.append(jnp.where(in_cols, acc, neg))
    for kind, (r0, ws) in enumerate(geo):
        for i in range(NA_Q_ROWS):
            r = r0 + i
            lo = min(max(r - NA_KH // 2, 0), rows - NA_KH)
            for j in range(NA_WIN_ROWS):
                kr = ws + j
                blk = tabs[kr - r + NA_KH - 1] if lo <= kr < lo + NA_KH else neg
                o_ref[kind, i * GRID_W:(i + 1) * GRID_W, j * GRID_W:(j + 1) * GRID_W] = blk


def _na_bias(na_rpb, rows):
    depth = na_rpb.shape[0]
    geo = _na_tile_geometry(rows)
    qn, kn = NA_Q_ROWS * GRID_W, NA_WIN_ROWS * GRID_W
    return pl.pallas_call(
        functools.partial(_na_bias_kernel, geo, rows),
        grid=(depth, NA_HEADS),
        in_specs=[pl.BlockSpec(memory_space=pltpu.SMEM)],
        out_specs=pl.BlockSpec((None, None, 3, qn, kn), lambda l, h: (l, h, 0, 0, 0)),
        out_shape=jax.ShapeDtypeStruct((depth, NA_HEADS, 3, qn, kn), F32),
        compiler_params=_params("parallel", "parallel"),
        name="na_bias",
    )(na_rpb.reshape(-1))


def _na_lat_kernel(rows, q_ref, k_ref, v_ref, kx_ref, vx_ref, bias_ref, o_ref, s_scr, p_scr):
    j = pl.program_id(1)
    ws = jnp.clip(j * NA_Q_ROWS - NA_KH // 2, 0, rows - NA_WIN_ROWS)
    start = pl.multiple_of(ws * GRID_W, GRID_W)
    nk = NA_WIN_ROWS * GRID_W
    heads = [slice(hd * NA_HEAD_DIM, (hd + 1) * NA_HEAD_DIM) for hd in range(NA_HEADS)]
    m = []
    for hd, sl in enumerate(heads):
        q = q_ref[:, sl]
        s_win = _nt_dot(q, k_ref[pl.ds(start, nk), sl]) + bias_ref[hd]
        s_ctx = _nt_dot(q, kx_ref[hd].astype(BF16))
        s_scr[hd, :, :nk] = s_win
        s_scr[hd, :, nk:] = s_ctx
        m.append(jnp.maximum(jnp.max(s_win, axis=-1, keepdims=True), jnp.max(s_ctx, axis=-1, keepdims=True)))
    l = []
    for hd in range(NA_HEADS):
        p = jnp.exp(s_scr[hd] - m[hd])
        l.append(jnp.sum(p, axis=-1, keepdims=True))
        p_scr[hd] = p.astype(BF16)
    for hd, sl in enumerate(heads):
        o = _dot(p_scr[hd, :, :nk], v_ref[pl.ds(start, nk), sl]) + _dot(p_scr[hd, :, nk:], vx_ref[hd].astype(BF16))
        o_ref[:, sl] = (o / l[hd]).astype(BF16)


def _na_lat_attention(qn, kn, vn, cache_k, cache_v, bias, layer, n):
    t = qn.shape[0]
    rows = n // GRID_W
    assert rows % NA_Q_ROWS == 0 and rows >= NA_WIN_ROWS + NA_Q_ROWS
    tiles = rows // NA_Q_ROWS
    tq = NA_Q_ROWS * GRID_W
    past = cache_k.shape[3]

    def kind(b, j):
        return (layer, 0, jnp.where(j == 0, 0, jnp.where(j == tiles - 1, 2, 1)), 0, 0)

    return pl.pallas_call(
        functools.partial(_na_lat_kernel, rows),
        grid=(t // n, tiles),
        in_specs=[
            pl.BlockSpec((tq, NA_WIDTH), lambda b, j: (b * tiles + j, 0)),
            pl.BlockSpec((n, NA_WIDTH), lambda b, j: (b, 0)),
            pl.BlockSpec((n, NA_WIDTH), lambda b, j: (b, 0)),
            pl.BlockSpec((None, None, NA_HEADS, past, NA_HEAD_DIM), lambda b, j: (b, layer, 0, 0, 0)),
            pl.BlockSpec((None, None, NA_HEADS, past, NA_HEAD_DIM), lambda b, j: (b, layer, 0, 0, 0)),
            pl.BlockSpec((None, NA_HEADS, None, tq, NA_WIN_ROWS * GRID_W), kind),
        ],
        out_specs=pl.BlockSpec((tq, NA_WIDTH), lambda b, j: (b * tiles + j, 0)),
        out_shape=jax.ShapeDtypeStruct((t, NA_WIDTH), BF16),
        scratch_shapes=[pltpu.VMEM((NA_HEADS, tq, NA_WIN_ROWS * GRID_W + past), F32),
                        pltpu.VMEM((NA_HEADS, tq, NA_WIN_ROWS * GRID_W + past), BF16)],
        compiler_params=_params("parallel", "parallel"),
        name="na_lat_attention",
    )(qn, kn, vn, cache_k, cache_v, bias)


def _dft_tables(n):
    def thin(j, k, period):
        ang = (2.0 * math.pi / period) * ((j[:, None] * k[None, :]) % period).astype(F32)
        return jnp.cos(ang), jnp.sin(ang)

    k = jnp.arange(n, dtype=jnp.int32)
    scale = float(n) ** -0.5
    if n % 64 == 0 and n > 64:
        n1 = n // 64
        c1, s1 = thin(jnp.arange(n1, dtype=jnp.int32), k, n1)
        c2, s2 = thin(jnp.arange(64, dtype=jnp.int32), k, n)
        c1, s1, c2, s2 = c1[:, None, :], s1[:, None, :], c2[None, :, :], s2[None, :, :]
        cm = (c1 * c2 - s1 * s2).reshape(n, n)
        sm = (s1 * c2 + c1 * s2).reshape(n, n)
    else:
        cm, sm = thin(k, k, n)
    return (cm * scale).astype(BF16), (sm * -scale).astype(BF16)


def _fourier_kernel(c_ref, s_ref, ab_ref, o_ref):
    o_ref[...] = (_dot(c_ref[...], ab_ref[:, :FN_WIDTH]) + _dot(s_ref[...], ab_ref[:, FN_WIDTH:])).astype(BF16)


def _fourier_half_kernel(c_ref, s_ref, cmid_ref, ab_ref, plus_ref, minus_ref, mid_ref):
    a = ab_ref[:, :FN_WIDTH]
    p = _dot(c_ref[...], a)
    q = _dot(s_ref[...], ab_ref[:, FN_WIDTH:])
    plus_ref[...] = (p + q).astype(BF16)
    minus_ref[...] = (p - q).astype(BF16)
    mid_ref[...] = _dot(cmid_ref[...], a).astype(BF16)


def _fourier_half(fab, tables, n, tmf=512):
    t = fab.shape[0]
    b = t // n
    half = n // 2
    tiles = half // tmf
    cm, sm = tables
    cmid = jnp.broadcast_to(cm[half:half + 1], (8, n))
    row_out = lambda: pl.BlockSpec((tmf, FN_WIDTH), lambda i, bb: (bb * tiles + i, 0))
    plus, minus, mid = pl.pallas_call(
        _fourier_half_kernel,
        grid=(tiles, b),
        in_specs=[
            pl.BlockSpec((tmf, n), lambda i, bb: (i, 0)),
            pl.BlockSpec((tmf, n), lambda i, bb: (i, 0)),
            pl.BlockSpec((8, n), lambda i, bb: (0, 0)),
            pl.BlockSpec((n, 2 * FN_WIDTH), lambda i, bb: (bb, 0)),
        ],
        out_specs=[row_out(), row_out(), pl.BlockSpec((None, None, 8, FN_WIDTH), lambda i, bb: (i, bb, 0, 0))],
        out_shape=[jax.ShapeDtypeStruct((b * half, FN_WIDTH), BF16), jax.ShapeDtypeStruct((b * half, FN_WIDTH), BF16),
                   jax.ShapeDtypeStruct((tiles, b, 8, FN_WIDTH), BF16)],
        compiler_params=_params("parallel", "parallel"),
        name="fourier_half",
    )(cm, sm, cmid, fab)
    return plus, minus, mid


def _fourier(fab, tables, n, tmf=512):
    if n >= 4 * tmf and tmf == TM_LAT_MIXOUT:
        return _fourier_half(fab, tables, n, tmf)
    t = fab.shape[0]
    tmf = min(tmf, n)
    tiles = n // tmf
    cm, sm = tables
    return pl.pallas_call(
        _fourier_kernel,
        grid=(tiles, t // n),
        in_specs=[
            pl.BlockSpec((tmf, n), lambda i, b: (i, 0)),
            pl.BlockSpec((tmf, n), lambda i, b: (i, 0)),
            pl.BlockSpec((n, 2 * FN_WIDTH), lambda i, b: (b, 0)),
        ],
        out_specs=pl.BlockSpec((tmf, FN_WIDTH), lambda i, b: (b * tiles + i, 0)),
        out_shape=jax.ShapeDtypeStruct((t, FN_WIDTH), BF16),
        compiler_params=_params("parallel", "parallel"),
        name="fourier",
    )(cm, sm, fab)


def _route(s_t, sb_t):
    def top2_sum(v):
        hi1, lo1 = jnp.maximum(v[0], v[1]), jnp.minimum(v[0], v[1])
        hi2, lo2 = jnp.maximum(v[2], v[3]), jnp.minimum(v[2], v[3])
        return jnp.maximum(hi1, hi2) + jnp.maximum(jnp.minimum(hi1, hi2), jnp.maximum(lo1, lo2))

    best = top2_sum(sb_t[0:EXPERTS_PER_GROUP])
    gsel = jnp.zeros_like(best, dtype=jnp.int32)
    for g in range(1, N_EXPERT_GROUPS):
        cand = top2_sum(sb_t[g * EXPERTS_PER_GROUP:(g + 1) * EXPERTS_PER_GROUP])
        better = cand > best
        gsel = jnp.where(better, g, gsel)
        best = jnp.where(better, cand, best)
    chosen = []
    for e in range(N_EXPERTS):
        g = e // EXPERTS_PER_GROUP
        beaten = jnp.zeros_like(gsel)
        for o in range(g * EXPERTS_PER_GROUP, (g + 1) * EXPERTS_PER_GROUP):
            if o == e:
                continue
            ahead = (sb_t[o] > sb_t[e]) | ((sb_t[o] == sb_t[e]) & (o < e))
            beaten = beaten + ahead.astype(jnp.int32)
        chosen.append((gsel == g) & (beaten < 2))
    picked = [jnp.where(chosen[e], s_t[e], 0.0) for e in range(N_EXPERTS)]
    denom = picked[0]
    for e in range(1, N_EXPERTS):
        denom = denom + picked[e]
    return chosen, [pk / denom for pk in picked]


def _pack_pairs(x):
    w = x.shape[1] // 2
    hi = pltpu.bitcast(x[:, :w].astype(BF16).astype(F32), jnp.uint32)
    lo = pltpu.bitcast(x[:, w:].astype(BF16).astype(F32), jnp.uint32)
    return hi | (lo >> 16)


def _half_spectrum_tile(i, tiles_per_seq, plus_ref, minus_ref, edge_ref, mid_ref, rev_ref):
    j = i % tiles_per_seq - tiles_per_seq // 2
    body = _dot(rev_ref[...], minus_ref[...]).astype(BF16)
    first = jnp.where(j == 0, mid_ref[0:1, :], edge_ref[0:1, :])
    rows = lax.broadcasted_iota(jnp.int32, body.shape, 0)
    upper = jnp.where(rows == 0, first, body)
    return jnp.where(j < 0, plus_ref[...], upper)


def _mixout_kernel(n, half, *refs):
    x_ref, mod_ref, ab_ref, z_ref, zp_ref, zn_ref, yna_ref, ymla_ref = refs[:8]
    n_g = 5 if half else 1
    g_refs = refs[8:8 + n_g]
    (cw_ref, wfn_ref, wout_ref, g2_ref, wrc_ref, br_ref, x1_ref, h2_ref, route_ref, gtok_ref,
     cnt_ref) = refs[8 + n_g:]
    tm = x_ref.shape[0]
    i = pl.program_id(0)
    g_tile = _half_spectrum_tile(i, n // tm, *g_refs) if half else g_refs[0][...]
    mod = mod_ref[...]
    gate1, shift2, scale2 = mod[2:3], mod[3:4], mod[4:5]

    z = z_ref[...].astype(F32)
    ridx = lax.broadcasted_iota(jnp.int32, z.shape, 0)
    at_start = (i * tm) % n == 0
    at_end = ((i + 1) * tm) % n == 0
    prev_row = jnp.where(at_start, 0.0, zp_ref[7:8, :].astype(F32))
    next_row = jnp.where(at_end, 0.0, zn_ref[0:1, :].astype(F32))
    z_m1 = jnp.where(ridx == 0, prev_row, pltpu.roll(z, 1, axis=0))
    z_p1 = jnp.where(ridx == tm - 1, next_row, pltpu.roll(z, tm - 1, axis=0))
    cw = cw_ref[...]
    y_conv = ab_ref[...].astype(F32) * (z_m1 * cw[0:1] + z * cw[1:2] + z_p1 * cw[2:3])

    y_fn = _dot(g_tile, wfn_ref[...])
    cat = jnp.concatenate([y_conv.astype(BF16), yna_ref[...], ymla_ref[...], y_fn.astype(BF16)], axis=-1)
    x1 = x_ref[...] + gate1 * _dot(cat, wout_ref[...])
    x1_ref[...] = x1

    h2 = _rms(x1, g2_ref[...]) * (1.0 + scale2) + shift2
    packed = _pack_pairs(h2)
    piece = packed.shape[1] // MOE_PIECES
    for p in range(MOE_PIECES):
        h2_ref[p] = packed[:, p * piece:(p + 1) * piece]
    h2_hi = h2.astype(BF16)
    h2_lo = (h2 - h2_hi.astype(F32)).astype(BF16)
    both = _dot(h2_hi, wrc_ref[...])
    logits = both[:, :LANES] + (both[:, LANES:] + _dot(h2_lo, wrc_ref[:, :LANES]))
    s = jax.nn.sigmoid(logits)
    s_t = jnp.transpose(s)
    sb_t = jnp.transpose(s + br_ref[...])
    chosen, gates = _route([s_t[e:e + 1] for e in range(N_EXPERTS)], [sb_t[e:e + 1] for e in range(N_EXPERTS)])

    @pl.when(i == 0)
    def _():
        cnt_ref[...] = jnp.zeros(cnt_ref.shape, F32)

    chosen_f = jnp.concatenate([ch.astype(F32) for ch in chosen], axis=0)
    before = lax.broadcasted_iota(jnp.int32, (tm, tm), 0) < lax.broadcasted_iota(jnp.int32, (tm, tm), 1)
    prefix = _dot(chosen_f.astype(BF16), jnp.where(before, 1.0, 0.0).astype(BF16))
    base = cnt_ref[...]
    rank = jnp.concatenate([base] * (tm // LANES), axis=1) + prefix
    cnt_ref[...] = base + jnp.sum(chosen_f, axis=1, keepdims=True)

    zero = jnp.zeros((1, tm), F32)
    seen = zero
    slots = [[zero, zero, zero], [zero, zero, zero]]
    for e in range(N_EXPERTS):
        for k in range(2):
            hit = chosen[e] & (seen == float(k))
            for j, val in enumerate((float(e), gates[e], rank[e:e + 1])):
                slots[k][j] = jnp.where(hit, val, slots[k][j])
        seen = seen + chosen_f[e:e + 1]
    (e_lo, g_lo, r_lo), (e_hi, g_hi, r_hi) = slots
    route_ref[...] = jnp.concatenate([g_lo, g_hi, e_lo, e_hi, r_lo, r_hi, zero, zero], axis=0)
    gates_t = jnp.concatenate([g_lo, g_hi, jnp.zeros((LANES - 2, tm), F32)], axis=0)
    gtok_ref[...] = jnp.transpose(gates_t)


def _mixout(x, mod, layer, n, parts, wts, lat, tm):
    t, d = x.shape
    ab, z, yna, ymla, g = parts
    nblk8 = t // 8
    per8 = tm // 8
    cond_row_of_tile = _cond_row(lat, tm, n, mod.shape[1] - 1)
    const2 = lambda i: (0, 0)
    lsel = lambda i: (layer, 0, 0)
    row = lambda w: pl.BlockSpec((tm, w), lambda i: (i, 0))
    in_specs = [
        row(d),
        pl.BlockSpec((None, None, 6, d), lambda i: (layer, cond_row_of_tile(i), 0, 0)),
        row(CONV_WIDTH),
        row(CONV_WIDTH),
        pl.BlockSpec((8, CONV_WIDTH), lambda i: (jnp.maximum(i * per8 - 1, 0), 0)),
        pl.BlockSpec((8, CONV_WIDTH), lambda i: (jnp.minimum((i + 1) * per8, nblk8 - 1), 0)),
        row(NA_WIDTH),
        row(MLA_HEADS * MLA_V_DIM),
    ]
    half = isinstance(g, tuple)
    if half:
        plus, minus, mid = g
        tps = n // tm
        hps = tps // 2
        assert plus.shape[0] * 2 == t and tps % 2 == 0
        rev = np.zeros((tm, tm), np.float32)
        rev[np.arange(1, tm), tm - np.arange(1, tm)] = 1.0
        src = lambda i: (i // tps) * hps + jnp.clip(tps - 1 - i % tps, 0, hps - 1)
        in_specs += [
            pl.BlockSpec((tm, FN_WIDTH), lambda i: ((i // tps) * hps + jnp.minimum(i % tps, hps - 1), 0)),
            pl.BlockSpec((tm, FN_WIDTH), lambda i: (src(i), 0)),
            pl.BlockSpec((8, FN_WIDTH), lambda i: (jnp.minimum(src(i) + 1, plus.shape[0] // tm - 1) * per8, 0)),
            pl.BlockSpec((None, None, 8, FN_WIDTH), lambda i: (0, i // tps, 0, 0)),
            pl.BlockSpec((tm, tm), const2),
        ]
        g_args = [plus, minus, minus, mid, jnp.asarray(rev, BF16)]
    else:
        in_specs += [row(FN_WIDTH)]
        g_args = [g]
    in_specs += [
        pl.BlockSpec((None, 3, CONV_WIDTH), lsel),
        pl.BlockSpec((None, FN_WIDTH, FN_WIDTH), lsel),
        pl.BlockSpec((None, d, d), lsel),
        pl.BlockSpec((None, 1, d), lsel),
        pl.BlockSpec((d, 2 * LANES), const2),
        pl.BlockSpec((1, LANES), const2),
    ]
    out_specs = [
        row(d),
        pl.BlockSpec((MOE_PIECES, tm, d // 2 // MOE_PIECES), lambda i: (0, i, 0)),
        pl.BlockSpec((8, tm), lambda i: (0, i)),
        row(LANES),
        pl.BlockSpec((N_EXPERTS, LANES), const2),
    ]
    out_shape = [
        jax.ShapeDtypeStruct((t, d), F32),
        jax.ShapeDtypeStruct((MOE_PIECES, t, d // 2 // MOE_PIECES), jnp.uint32),
        jax.ShapeDtypeStruct((8, t), F32),
        jax.ShapeDtypeStruct((t, LANES), F32),
        jax.ShapeDtypeStruct((N_EXPERTS, LANES), F32),
    ]
    return pl.pallas_call(
        functools.partial(_mixout_kernel, n, half),
        grid=(t // tm,),
        in_specs=in_specs,
        out_specs=out_specs,
        out_shape=out_shape,
        compiler_params=_params("arbitrary"),
        name="mixout",
    )(x, mod, ab, z, z, z, yna, ymla, *g_args, wts["conv_w"], wts["w_fn"], wts["w_out"], wts["norm2"],
      wts["wr_cat"], wts["b_router"])


def _slot_positions(route, counts, rb):
    cnt = counts[:, 0].astype(jnp.int32)
    padded = (cnt + rb - 1) // rb * rb
    ends = jnp.cumsum(padded)
    offs = ends - padded
    experts = route[2:4].astype(jnp.int32)
    ranks = route[4:6].astype(jnp.int32)
    pos = ranks
    for e in range(N_EXPERTS):
        pos = pos + jnp.where(experts == e, offs[e], 0)
    nblk = (2 * route.shape[1]) // rb + N_EXPERTS
    starts = jnp.arange(nblk, dtype=jnp.int32) * rb
    blk_expert = jnp.sum((starts[:, None] >= ends[None, :]).astype(jnp.int32), axis=1)
    used = blk_expert < N_EXPERTS
    blk_expert = jnp.where(used, blk_expert, 0)
    valid_end = jnp.sum(jnp.where(blk_expert[:, None] == jnp.arange(N_EXPERTS)[None, :], (offs + cnt)[None, :], 0), axis=1)
    blk_valid = jnp.where(used, jnp.clip(valid_end - starts, 0, rb), 0)
    return pos, jnp.stack([blk_expert, blk_valid])


def _sc_mesh():
    return plsc.VectorSubcoreMesh(core_axis_name="c", subcore_axis_name="s")


def _sc_pipeline(body, nwin, in_specs, out_specs):
    return pltpu.emit_pipeline(body, grid=(nwin,), in_specs=in_specs, out_specs=out_specs,
                               core_axis_name=("c", "s"), dimension_semantics=(pltpu.PARALLEL,))


def _row_scatter(table, idx_a, idx_b, nrows):
    b, w = table.shape
    win = SC_WINDOW
    idx_spec = pl.BlockSpec((1, win), lambda i: (0, i))

    @functools.partial(pl.kernel, out_type=jax.ShapeDtypeStruct((nrows, w), table.dtype), mesh=_sc_mesh(),
                       scratch_types=[])
    def scatter(table_hbm, ia_hbm, ib_hbm, out_hbm):
        def body(rows_vmem, ia_vmem, ib_vmem):
            pltpu.sync_copy(rows_vmem, out_hbm.at[ia_vmem.at[0]])
            pltpu.sync_copy(rows_vmem, out_hbm.at[ib_vmem.at[0]])

        _sc_pipeline(body, b // win, [pl.BlockSpec((win, w), lambda i: (i, 0)), idx_spec, idx_spec], [])(
            table_hbm, ia_hbm, ib_hbm)

    return scatter(table, idx_a.reshape(1, b), idx_b.reshape(1, b))


def _row_gather(table, idx):
    b = idx.shape[0]
    w = table.shape[1]
    win = SC_WINDOW

    @functools.partial(pl.kernel, out_type=jax.ShapeDtypeStruct((b, w), table.dtype), mesh=_sc_mesh(),
                       scratch_types=[])
    def gather(table_hbm, idx_hbm, out_hbm):
        def body(idx_vmem, out_vmem):
            pltpu.sync_copy(table_hbm.at[idx_vmem.at[0]], out_vmem)

        _sc_pipeline(body, b // win, [pl.BlockSpec((1, win), lambda i: (0, i))],
                     [pl.BlockSpec((win, w), lambda i: (i, 0))])(idx_hbm, out_hbm)

    return gather(table, idx.reshape(1, b))


def _ffn_kernel(blk_ref, xs_ref, w13_ref, w2_ref, y_ref):
    i = pl.program_id(0)
    e = blk_ref[0, i]
    nvalid = blk_ref[1, i]

    @pl.when(nvalid > 0)
    def _():
        packed = jnp.concatenate([xs_ref[0], xs_ref[1]], axis=-1)
        live = lax.broadcasted_iota(jnp.int32, packed.shape, 0) < nvalid
        xb = _unpack_pairs(jnp.where(live, packed, jnp.uint32(0))).astype(BF16)
        up = _dot(xb, w13_ref[e])
        a, b = up[:, :EXPERT_FF], up[:, EXPERT_FF:]
        hid = (a * jax.nn.sigmoid(a)) * b
        y = _pack_pairs(_dot(hid.astype(BF16), w2_ref[e]))
        half = y.shape[1] // 2
        y_ref[0] = y[:, :half]
        y_ref[1] = y[:, half:]

    @pl.when(nvalid == 0)
    def _():
        y_ref[...] = jnp.zeros(y_ref.shape, y_ref.dtype)


def _expert_ffn(xs, blk, w13, w2, layer, rb):
    pieces, nrows, w = xs.shape
    d = 2 * pieces * w
    resident = dict(pipeline_mode=pl.Buffered(1))
    used = lambda i, blk: (0, jnp.where(blk[1, i] > 0, i, 0), 0)
    return pl.pallas_call(
        _ffn_kernel,
        grid_spec=pltpu.PrefetchScalarGridSpec(
            num_scalar_prefetch=1,
            grid=(nrows // rb,),
            in_specs=[
                pl.BlockSpec((pieces, rb, w), used),
                pl.BlockSpec((None, N_EXPERTS, d, 2 * EXPERT_FF), lambda i, blk: (layer, 0, 0, 0), **resident),
                pl.BlockSpec((None, N_EXPERTS, EXPERT_FF, d), lambda i, blk: (layer, 0, 0, 0), **resident),
            ],
            out_specs=pl.BlockSpec((pieces, rb, w), lambda i, blk: (0, i, 0)),
        ),
        out_shape=jax.ShapeDtypeStruct(xs.shape, xs.dtype),
        compiler_params=_params("parallel"),
        name="expert_ffn",
    )(blk, xs, w13, w2)


def _combine_kernel(final, x1_ref, gtok_ref, mod_ref, nf_ref, y_ref, o_ref):
    out = _moe_residual(x1_ref[...], gtok_ref, y_ref, mod_ref)
    if final:
        out = _rms(out, nf_ref[...])
    o_ref[...] = out


def _combine(x1, gtok, y_tok, mod, layer, n, norm_f, final, lat, tc=512):
    t, d = x1.shape
    cond_row_of_tile = _cond_row(lat, tc, n, mod.shape[1] - 1)
    row = lambda w: pl.BlockSpec((tc, w), lambda i: (i, 0))
    return pl.pallas_call(
        functools.partial(_combine_kernel, final),
        grid=(t // tc,),
        in_specs=[
            row(d),
            row(LANES),
            pl.BlockSpec((None, None, 6, d), lambda i: (layer, cond_row_of_tile(i), 0, 0)),
            pl.BlockSpec((1, d), lambda i: (0, 0)),
            pl.BlockSpec(y_tok.shape[:2] + (tc, y_tok.shape[3]), lambda i: (0, 0, i, 0)),
        ],
        out_specs=row(d),
        out_shape=jax.ShapeDtypeStruct((t, d), F32),
        compiler_params=_params("parallel"),
        name="combine",
    )(x1, gtok, mod, norm_f, y_tok)


def _moe_layer(x, mod, layer, n, parts, wts, lat, tm, rb):
    t = x.shape[0]
    x1, h2, route, gtok, counts = _mixout(x, mod, layer, n, parts, wts, lat, tm)
    pos, blk = _slot_positions(route, counts, rb)
    pieces, _, w = h2.shape
    nrows = blk.shape[1] * rb
    piece_base = (jnp.arange(pieces, dtype=jnp.int32) * nrows)[:, None]
    idx = [(piece_base + pos[s][None, :]).reshape(-1) for s in range(2)]
    xs = _row_scatter(h2.reshape(pieces * t, w), idx[0], idx[1], pieces * nrows).reshape(pieces, nrows, w)
    y = _expert_ffn(xs, blk, wts["w13"], wts["w2"], layer, rb)
    back = (piece_base[:, :, None] + pos[None, :, :]).reshape(-1)
    y_tok = _row_gather(y.reshape(pieces * nrows, w), back).reshape(pieces, 2, t, w)
    return x1, (gtok, y_tok)


def _swap_halves(w):
    nf = MLA_ROPE_DIM // 4
    idx = np.arange(MLA_ROPE_DIM).reshape(2, 2, nf)[:, ::-1, :].reshape(-1)
    return w[..., idx]


def _pack_weights(w_in, mla_wq_up, mla_wkv_up, w1, w3, w2, w_router, b_router):
    depth, d, _ = w_in.shape
    zeros = lambda w: jnp.zeros((depth, d, w), w_in.dtype)
    w_kr = w_in[..., 1920:1952]
    pad_rope = lambda w: jnp.concatenate([zeros(MLA_NOPE_DIM), w, zeros(MLA_QK_PAD - MLA_NOPE_DIM - MLA_ROPE_DIM)], -1)
    w_main = jnp.concatenate([w_in[..., :1920], pad_rope(w_kr), pad_rope(_swap_halves(w_kr)), w_in[..., 1952:]], -1)

    wq = mla_wq_up.reshape(depth, MLA_Q_LORA, MLA_HEADS, MLA_NOPE_DIM + MLA_ROPE_DIM)
    q_nope, q_rope = wq[..., :MLA_NOPE_DIM], wq[..., MLA_NOPE_DIM:]
    tail = jnp.zeros(q_rope.shape[:-1] + (MLA_QK_PAD - MLA_NOPE_DIM - MLA_ROPE_DIM,), wq.dtype)
    wq_a = jnp.concatenate([q_nope, q_rope, tail], -1).reshape(depth, MLA_Q_LORA, -1)
    wq_b = jnp.concatenate([jnp.zeros_like(q_nope), _swap_halves(q_rope), tail], -1).reshape(depth, MLA_Q_LORA, -1)

    wkv = mla_wkv_up.reshape(depth, MLA_KV_LORA, MLA_HEADS, MLA_NOPE_DIM + MLA_V_DIM)
    k_nope, v_up = wkv[..., :MLA_NOPE_DIM], wkv[..., MLA_NOPE_DIM:]
    k_tail = jnp.zeros(k_nope.shape[:-1] + (MLA_QK_PAD - MLA_NOPE_DIM,), wkv.dtype)
    wk_a = jnp.concatenate([k_nope, k_tail], -1).reshape(depth, MLA_KV_LORA, -1)
    wv = v_up.reshape(depth, MLA_KV_LORA, -1)
    v_tail = jnp.zeros(v_up.shape[:-1] + (MLA_V_PAD - MLA_V_DIM,), wkv.dtype)
    wv_ext = jnp.concatenate([v_up, v_tail], -1).reshape(depth, MLA_KV_LORA, -1)
    vone = np.zeros((1, MLA_HEADS * MLA_V_PAD), np.float32)
    vone[0, MLA_V_DIM::MLA_V_PAD] = 1.0

    wr = jnp.pad(w_router, ((0, 0), (0, LANES - N_EXPERTS)))
    wr_hi = wr.astype(BF16)
    wr_lo = (wr - wr_hi.astype(F32)).astype(BF16)
    return {
        "w_in": w_main.astype(BF16), "wq_a": wq_a.astype(BF16), "wq_b": wq_b.astype(BF16),
        "wk_a": wk_a.astype(BF16), "wv": wv.astype(BF16), "wv_ext": wv_ext.astype(BF16),
        "vone_ext": jnp.asarray(vone),
        "w13": jnp.concatenate([w1, w3], -1).astype(BF16), "w2": w2.astype(BF16),
        "wr_cat": jnp.concatenate([wr_hi, wr_lo], axis=-1),
        "b_router": jnp.pad(b_router, (0, LANES - N_EXPERTS)).reshape(1, LANES).astype(F32),
    }


def _channel_dft():
    c = np.arange(FN_GROUP_DIM)
    ang = 2.0 * np.pi * ((c[:, None] * c[None, :]) % FN_GROUP_DIM) / FN_GROUP_DIM
    out = np.zeros((FN_WIDTH, 2 * FN_WIDTH), np.float32)
    for g in range(FN_GROUPS):
        sl = slice(g * FN_GROUP_DIM, (g + 1) * FN_GROUP_DIM)
        out[sl, sl] = np.cos(ang) * FN_GROUP_DIM ** -0.5
        out[sl, FN_WIDTH + g * FN_GROUP_DIM:FN_WIDTH + (g + 1) * FN_GROUP_DIM] = np.sin(ang) * FN_GROUP_DIM ** -0.5
    return jnp.asarray(out, BF16)


def _rope_tables(n):
    tok = jnp.arange(n)
    pos = jnp.stack([tok // GRID_W, tok % GRID_W], axis=-1).astype(F32)
    nf = MLA_ROPE_DIM // 4
    freqs = ROPE_THETA ** (-jnp.arange(nf, dtype=F32) / nf)
    ang = pos[:, :, None] * freqs
    cos = jnp.broadcast_to(jnp.cos(ang)[:, :, None, :], (n, 2, 2, nf)).reshape(n, MLA_ROPE_DIM)
    sin = jnp.sin(ang)
    sin = jnp.stack([-sin, sin], axis=2).reshape(n, MLA_ROPE_DIM)
    pad = jnp.zeros((n, MLA_QK_PAD - MLA_NOPE_DIM - MLA_ROPE_DIM), F32)
    cos_t = jnp.concatenate([jnp.ones((n, MLA_NOPE_DIM), F32), cos, pad], -1)
    sin_t = jnp.concatenate([jnp.zeros((n, MLA_NOPE_DIM), F32), sin, pad], -1)
    return cos_t, sin_t


def kernel(x_prompt, x_sample, cache_na_k, cache_na_v, cache_mla_ckv, cache_mla_krope, c, c_ctx, w_ada, b_ada,
           norm1, norm2, w_in, conv_w, na_rpb, mla_gq, mla_wq_up, mla_gkv, mla_wkv_up, w_fn, w_out, w_router,
           b_router, w1, w3, w2, norm_f):
    bp, seq, d = x_prompt.shape
    bd, dec_seq, _ = x_sample.shape
    depth = w_in.shape[0]

    wts = _pack_weights(w_in, mla_wq_up, mla_wkv_up, w1, w3, w2, w_router, b_router)
    wts.update({
        "norm1": norm1.reshape(depth, 1, d), "norm2": norm2.reshape(depth, 1, d),
        "mla_gq": mla_gq.reshape(depth, 1, -1), "mla_gkv": mla_gkv.reshape(depth, 1, -1),
        "conv_w": conv_w, "w_fn": w_fn.astype(BF16), "w_out": w_out.astype(BF16),
        "norm_f": norm_f.reshape(1, d), "cs_bd": _channel_dft(),
    })

    cond = jnp.concatenate([c, jnp.zeros((-(bd + 1) % 8, d), c.dtype), c_ctx[None, :]], axis=0)
    mod = _ada_modulation(cond, w_ada, b_ada)

    xp = x_prompt.reshape(bp * seq, d)
    tables = _dft_tables(seq)
    caches = None
    pending = None
    for layer in range(depth):
        outs = _premix(xp, mod, layer, seq, wts, None, False, seq, caches, pending)
        ab, z, qn, kn, vn, km, fab, qm, vm = outs[:9]
        caches = outs[9:13]
        if pending is not None:
            xp = outs[13]
        yna, ymla = _ctx_attention(qn, kn, vn, qm, km, vm, seq)
        g = _fourier(fab, tables, seq)
        xp, pending = _moe_layer(xp, mod, layer, seq, (ab, z, yna, ymla, g), wts, False, seq, MOE_ROW_BLOCK)
    xp = _combine(xp, pending[0], pending[1], mod, depth - 1, seq, wts["norm_f"], True, False, seq)
    new_na_k, new_na_v, new_ckv, new_krope = caches

    xs = x_sample.reshape(bd * dec_seq, d)
    kx, vxt = _ctx_kv(cache_mla_ckv, cache_mla_krope, wts["wk_a"], wts["wv_ext"], wts["vone_ext"])
    na_bias = _na_bias(na_rpb, dec_seq // GRID_W)
    rope = _rope_tables(dec_seq)
    tables = _dft_tables(dec_seq)
    pending = None
    for layer in range(depth):
        outs = _premix(xs, mod, layer, dec_seq, wts, rope, True, TM_LAT_PREMIX, None, pending)
        ab, z, qn, kn, vn, km, fab, qt, vt = outs[:9]
        if pending is not None:
            xs = outs[9]
        yna = _na_lat_attention(qn, kn, vn, cache_na_k, cache_na_v, na_bias, layer, dec_seq)
        ymla = _mla_lat_attention(qt, km, vt, kx, vxt, layer, dec_seq)
        g = _fourier(fab, tables, dec_seq)
        xs, pending = _moe_layer(xs, mod, layer, dec_seq, (ab, z, yna, ymla, g), wts, True, TM_LAT_MIXOUT,
                                 MOE_ROW_BLOCK)
    xs = _combine(xs, pending[0], pending[1], mod, depth - 1, dec_seq, wts["norm_f"], True, True, TM_LAT_COMBINE)

    return (xp.reshape(bp, seq, d), xs.reshape(bd, dec_seq, d), new_na_k, new_na_v, new_ckv, new_krope)
```
